```python
import jax, jax.numpy as jnp
from jax import lax
import numpy as np

D_MODEL = 1024
BATCH = 8
SEQ = 4096
DEPTH = 4

GRID_W = 64
BLOCK_Q = 128
CHUNK = 128
ROPE_THETA = 10000.0
EPS = 1e-6
A_HEADS = 8
A_KV_HEADS = 2
A_HEAD_DIM = 64
A_WIDTH = A_HEADS * A_HEAD_DIM
A_KV_WIDTH = A_KV_HEADS * A_HEAD_DIM
B_GROUPS = 4
B_GROUP_DIM = 128
B_WIDTH = B_GROUPS * B_GROUP_DIM
M_HEADS = 4
M_HEAD_DIM = 128
M_WIDTH = M_HEADS * M_HEAD_DIM
MEM_LEN = 256
N_BRANCH = 3
BRANCH_WIDTH = 512
IN_SPLITS = (A_WIDTH, A_KV_WIDTH, A_KV_WIDTH, A_WIDTH,
             B_WIDTH, B_WIDTH, B_WIDTH,
             M_WIDTH, M_WIDTH,
             N_BRANCH * D_MODEL)
IN_WIDTH = sum(IN_SPLITS)

kernel_name = "hybrid_gqa_gmlp_memory_encoder"


def _split_points():
    pts, acc = [], 0
    for s in IN_SPLITS[:-1]:
        acc += s
        pts.append(acc)
    return pts


def rms_norm(x, g):
    xf = x.astype(jnp.float32)
    y = xf * lax.rsqrt(jnp.mean(xf * xf, axis=-1, keepdims=True) + EPS)
    return (y * g.astype(jnp.float32)).astype(x.dtype)


def layer_norm(x, g, b):
    xf = x.astype(jnp.float32)
    mu = jnp.mean(xf, axis=-1, keepdims=True)
    xc = xf - mu
    y = xc * lax.rsqrt(jnp.mean(xc * xc, axis=-1, keepdims=True) + EPS)
    return (y * g.astype(jnp.float32) + b.astype(jnp.float32)).astype(x.dtype)


def axial_rope_tables(seq):
    rows = seq // GRID_W
    row = jnp.repeat(jnp.arange(rows, dtype=jnp.float32), GRID_W)
    col = jnp.tile(jnp.arange(GRID_W, dtype=jnp.float32), rows)
    n_freq = A_HEAD_DIM // 4
    inv = ROPE_THETA ** (-jnp.arange(n_freq, dtype=jnp.float32) / n_freq)
    ang = jnp.stack([row[:, None] * inv, col[:, None] * inv], axis=1)
    return jnp.cos(ang), jnp.sin(ang)


def apply_axial_rope(x, cos, sin):
    b, s, h, d = x.shape
    x5 = x.reshape(b, s, h, 2, 2, d // 4)
    x1, x2 = x5[..., 0, :], x5[..., 1, :]
    c = cos[None, :, None].astype(x.dtype)
    sn = sin[None, :, None].astype(x.dtype)
    out = jnp.stack([x1 * c - x2 * sn, x2 * c + x1 * sn], axis=-2)
    return out.reshape(b, s, h, d)


def gqa_axial_attention(q, k, v, q_g, k_g, cos, sin):
    bsz, seq, _ = q.shape
    grp = A_HEADS // A_KV_HEADS
    q = apply_axial_rope(rms_norm(q.reshape(bsz, seq, A_HEADS, A_HEAD_DIM), q_g), cos, sin)
    k = apply_axial_rope(rms_norm(k.reshape(bsz, seq, A_KV_HEADS, A_HEAD_DIM), k_g), cos, sin)
    v = v.reshape(bsz, seq, A_KV_HEADS, A_HEAD_DIM)
    scale = A_HEAD_DIM ** -0.5
    nblk = seq // BLOCK_Q
    qb = q.reshape(bsz, nblk, BLOCK_Q, A_KV_HEADS, grp, A_HEAD_DIM).transpose(1, 0, 3, 4, 2, 5)
    kt = k.transpose(0, 2, 1, 3)
    vt = v.transpose(0, 2, 1, 3)

    def one_block(qblk):
        s = jnp.einsum('bkgqd,bksd->bkgqs', qblk, kt,
                       preferred_element_type=jnp.float32) * scale
        p = jax.nn.softmax(s, axis=-1)
        return jnp.einsum('bkgqs,bksd->bkgqd', p.astype(vt.dtype), vt)

    o = lax.map(one_block, qb)
    return o.transpose(1, 0, 4, 2, 3, 5).reshape(bsz, seq, A_WIDTH)


def chunked_spatial_gating(u, v, ln_g, ln_b, w_s, b_s):
    bsz, seq, _ = v.shape
    v = layer_norm(v, ln_g, ln_b)
    vc = v.reshape(bsz, seq // CHUNK, CHUNK, B_GROUPS, B_GROUP_DIM)
    mixed = jnp.einsum('gpq,bnqgc->bnpgc', w_s, vc) + b_s.T[None, None, :, :, None]
    return u * mixed.reshape(bsz, seq, B_WIDTH)


def memory_cross_attention(q, mem_n, w_kv):
    bsz, seq, _ = q.shape
    kv = mem_n @ w_kv
    k, v = jnp.split(kv, 2, axis=-1)
    k = k.reshape(bsz, -1, M_HEADS, M_HEAD_DIM)
    v = v.reshape(bsz, -1, M_HEADS, M_HEAD_DIM)
    q = q.reshape(bsz, seq, M_HEADS, M_HEAD_DIM)
    s = jnp.einsum('bshd,bmhd->bhsm', q, k,
                   preferred_element_type=jnp.float32) * (M_HEAD_DIM ** -0.5)
    p = jax.nn.softmax(s, axis=-1)
    o = jnp.einsum('bhsm,bmhd->bshd', p.astype(v.dtype), v)
    return o.reshape(bsz, seq, M_WIDTH)


def _fwd_setup_inputs(seed: int = 0) -> dict:
    key = jax.random.key(seed)
    ks = jax.random.split(key, 16)
    nrm = jax.random.normal
    f32 = jnp.float32
    return {
        "x": nrm(ks[0], (BATCH, SEQ, D_MODEL), f32),
        "mem": nrm(ks[1], (BATCH, MEM_LEN, D_MODEL), f32),
        "norm_g": 1.0 + 0.01 * nrm(ks[2], (DEPTH, D_MODEL), f32),
        "w_in": nrm(ks[3], (DEPTH, D_MODEL, IN_WIDTH), f32) * D_MODEL ** -0.5,
        "q_norm_g": 1.0 + 0.01 * nrm(ks[4], (DEPTH, A_HEAD_DIM), f32),
        "k_norm_g": 1.0 + 0.01 * nrm(ks[5], (DEPTH, A_HEAD_DIM), f32),
        "sg_ln_g": 1.0 + 0.01 * nrm(ks[6], (DEPTH, B_WIDTH), f32),
        "sg_ln_b": 0.01 * nrm(ks[7], (DEPTH, B_WIDTH), f32),
        "w_s": nrm(ks[8], (DEPTH, B_GROUPS, CHUNK, CHUNK), f32) * CHUNK ** -0.5,
        "b_s": 1.0 + 0.01 * nrm(ks[9], (DEPTH, B_GROUPS, CHUNK), f32),
        "mem_norm_g": 1.0 + 0.01 * nrm(ks[10], (DEPTH, D_MODEL), f32),
        "w_mem_kv": nrm(ks[11], (DEPTH, D_MODEL, 2 * M_WIDTH), f32) * D_MODEL ** -0.5,
        "w_br": nrm(ks[12], (DEPTH, N_BRANCH, BRANCH_WIDTH, D_MODEL), f32) * BRANCH_WIDTH ** -0.5,
        "w_out": nrm(ks[13], (DEPTH, D_MODEL, D_MODEL), f32) * D_MODEL ** -0.5,
        "final_g": 1.0 + 0.01 * nrm(ks[14], (D_MODEL,), f32),
    }


def _fwd_reference(x, mem, norm_g, w_in, q_norm_g, k_norm_g, sg_ln_g, sg_ln_b, w_s, b_s,
              mem_norm_g, w_mem_kv, w_br, w_out, final_g):
    bsz, seq, d = x.shape
    cos, sin = axial_rope_tables(seq)
    pts = _split_points()
    for l in range(DEPTH):
        h = rms_norm(x, norm_g[l])
        proj = h @ w_in[l]
        qA, kA, vA, zA, uB, vB, zB, qM, zM, g_logits = jnp.split(proj, pts, axis=-1)
        yA = gqa_axial_attention(qA, kA, vA, q_norm_g[l], k_norm_g[l], cos, sin) * jax.nn.silu(zA)
        yB = chunked_spatial_gating(uB, vB, sg_ln_g[l], sg_ln_b[l], w_s[l], b_s[l]) * jax.nn.silu(zB)
        mem_n = rms_norm(mem, mem_norm_g[l])
        yM = memory_cross_attention(qM, mem_n, w_mem_kv[l]) * jax.nn.silu(zM)
        branches = jnp.stack([yA, yB, yM], axis=2)
        up = jnp.einsum('bsnw,nwd->bsnd', branches, w_br[l])
        gates = jax.nn.sigmoid(g_logits.reshape(bsz, seq, N_BRANCH, d))
        merged = jnp.sum(gates * up, axis=2)
        x = x + merged @ w_out[l]
    return rms_norm(x, final_g)


import jax as _jax
import jax.numpy as _jnp

TWIN_FORMAT = 'train_step'
FWD_PARAMS = ['x', 'mem', 'norm_g', 'w_in', 'q_norm_g', 'k_norm_g', 'sg_ln_g', 'sg_ln_b', 'w_s', 'b_s', 'mem_norm_g', 'w_mem_kv', 'w_br', 'w_out', 'final_g']
TWIN_WEIGHTS = ['norm_g', 'w_in', 'q_norm_g', 'k_norm_g', 'sg_ln_g', 'sg_ln_b', 'w_s', 'b_s', 'mem_norm_g', 'w_mem_kv', 'w_br', 'w_out', 'final_g']
TWIN_DIFF_INPUT = 'x'
TWIN_INPUTS = ['x', 'mem', 'norm_g', 'w_in', 'q_norm_g', 'k_norm_g', 'sg_ln_g', 'sg_ln_b', 'w_s', 'b_s', 'mem_norm_g', 'w_mem_kv', 'w_br', 'w_out', 'final_g', 'loss_target', 'm_norm_g', 'm_w_in', 'm_q_norm_g', 'm_k_norm_g', 'm_sg_ln_g', 'm_sg_ln_b', 'm_w_s', 'm_b_s', 'm_mem_norm_g', 'm_w_mem_kv', 'm_w_br', 'm_w_out', 'm_final_g', 'v_norm_g', 'v_w_in', 'v_q_norm_g', 'v_k_norm_g', 'v_sg_ln_g', 'v_sg_ln_b', 'v_w_s', 'v_b_s', 'v_mem_norm_g', 'v_w_mem_kv', 'v_w_br', 'v_w_out', 'v_final_g']
TWIN_OUTPUTS = ['loss', 'grad_x', 'grad_norm_g', 'grad_w_in', 'grad_q_norm_g', 'grad_k_norm_g', 'grad_sg_ln_g', 'grad_sg_ln_b', 'grad_w_s', 'grad_b_s', 'grad_mem_norm_g', 'grad_w_mem_kv', 'grad_w_br', 'grad_w_out', 'grad_final_g', 'delta_norm_g', 'delta_w_in', 'delta_q_norm_g', 'delta_k_norm_g', 'delta_sg_ln_g', 'delta_sg_ln_b', 'delta_w_s', 'delta_b_s', 'delta_mem_norm_g', 'delta_w_mem_kv', 'delta_w_br', 'delta_w_out', 'delta_final_g', 'new_m_norm_g', 'new_m_w_in', 'new_m_q_norm_g', 'new_m_k_norm_g', 'new_m_sg_ln_g', 'new_m_sg_ln_b', 'new_m_w_s', 'new_m_b_s', 'new_m_mem_norm_g', 'new_m_w_mem_kv', 'new_m_w_br', 'new_m_w_out', 'new_m_final_g', 'new_v_norm_g', 'new_v_w_in', 'new_v_q_norm_g', 'new_v_k_norm_g', 'new_v_sg_ln_g', 'new_v_sg_ln_b', 'new_v_w_s', 'new_v_b_s', 'new_v_mem_norm_g', 'new_v_w_mem_kv', 'new_v_w_br', 'new_v_w_out', 'new_v_final_g']
TWIN_LEAF_KINDS = {'loss': 'loss', 'grad_x': 'grad_x', 'grad_norm_g': 'grad_w', 'grad_w_in': 'grad_w', 'grad_q_norm_g': 'grad_w', 'grad_k_norm_g': 'grad_w', 'grad_sg_ln_g': 'grad_w', 'grad_sg_ln_b': 'grad_w', 'grad_w_s': 'grad_w', 'grad_b_s': 'grad_w', 'grad_mem_norm_g': 'grad_w', 'grad_w_mem_kv': 'grad_w', 'grad_w_br': 'grad_w', 'grad_w_out': 'grad_w', 'grad_final_g': 'grad_w', 'delta_norm_g': 'delta_w', 'delta_w_in': 'delta_w', 'delta_q_norm_g': 'delta_w', 'delta_k_norm_g': 'delta_w', 'delta_sg_ln_g': 'delta_w', 'delta_sg_ln_b': 'delta_w', 'delta_w_s': 'delta_w', 'delta_b_s': 'delta_w', 'delta_mem_norm_g': 'delta_w', 'delta_w_mem_kv': 'delta_w', 'delta_w_br': 'delta_w', 'delta_w_out': 'delta_w', 'delta_final_g': 'delta_w', 'new_m_norm_g': 'new_m', 'new_m_w_in': 'new_m', 'new_m_q_norm_g': 'new_m', 'new_m_k_norm_g': 'new_m', 'new_m_sg_ln_g': 'new_m', 'new_m_sg_ln_b': 'new_m', 'new_m_w_s': 'new_m', 'new_m_b_s': 'new_m', 'new_m_mem_norm_g': 'new_m', 'new_m_w_mem_kv': 'new_m', 'new_m_w_br': 'new_m', 'new_m_w_out': 'new_m', 'new_m_final_g': 'new_m', 'new_v_norm_g': 'new_v', 'new_v_w_in': 'new_v', 'new_v_q_norm_g': 'new_v', 'new_v_k_norm_g': 'new_v', 'new_v_sg_ln_g': 'new_v', 'new_v_sg_ln_b': 'new_v', 'new_v_w_s': 'new_v', 'new_v_b_s': 'new_v', 'new_v_mem_norm_g': 'new_v', 'new_v_w_mem_kv': 'new_v', 'new_v_w_br': 'new_v', 'new_v_w_out': 'new_v', 'new_v_final_g': 'new_v'}


def _forward(args):
    return _fwd_reference(*[args[k] for k in FWD_PARAMS])


def _output_shape():
    out = _jax.eval_shape(lambda: _forward(_fwd_setup_inputs(0)))
    return out.shape, out.dtype

N_MICROBATCH = 1
ADAM_LR = 0.001
ADAM_B1 = 0.9
ADAM_B2 = 0.999
ADAM_EPS = 1e-08
ADAM_WD = 0.01
ADAM_STEP = 10
PER_EXAMPLE_BATCH_AXIS = {'x': 0, 'mem': 0, 'loss_target': 0}
SHARED_INPUTS = []
_WEIGHT_DTYPES = {'norm_g': _jnp.float32, 'w_in': _jnp.float32, 'q_norm_g': _jnp.float32, 'k_norm_g': _jnp.float32, 'sg_ln_g': _jnp.float32, 'sg_ln_b': _jnp.float32, 'w_s': _jnp.float32, 'b_s': _jnp.float32, 'mem_norm_g': _jnp.float32, 'w_mem_kv': _jnp.float32, 'w_br': _jnp.float32, 'w_out': _jnp.float32, 'final_g': _jnp.float32}
MOMENT_SCALE = {'norm_g': 1.325107e-01, 'w_in': 5.026566e-02, 'q_norm_g': 2.236514e-02, 'k_norm_g': 2.311822e-02, 'sg_ln_g': 8.103261e-02, 'sg_ln_b': 7.973431e-02, 'w_s': 7.934190e-02, 'b_s': 7.875775e-02, 'mem_norm_g': 8.904072e-03, 'w_mem_kv': 8.378338e-03, 'w_br': 4.550579e-02, 'w_out': 7.875527e-02, 'final_g': 3.198979e+01}


def _to_microbatches(a, axis):
    t = _jnp.moveaxis(a, axis, 0)
    t = t.reshape((N_MICROBATCH, t.shape[0] // N_MICROBATCH) + t.shape[1:])
    return _jnp.moveaxis(t, 1, axis + 1)


def setup_inputs(seed: int = 0) -> dict:
    inp = _fwd_setup_inputs(seed)
    key = _jax.random.fold_in(_jax.random.key(seed), 7919)
    shape, _ = _output_shape()
    out = dict(inp)
    out["loss_target"] = _jax.random.normal(_jax.random.fold_in(key, 0), shape, _jnp.float32)
    for i, name in enumerate(TWIN_WEIGHTS):
        w = inp[name].astype(_jnp.float32)
        if MOMENT_SCALE is None:
            s = _jnp.sqrt(_jnp.mean(_jnp.square(w)) + 1e-30)
        else:
            s = MOMENT_SCALE[name]
        km, kv = _jax.random.split(_jax.random.fold_in(key, i + 1))
        out[name] = w
        out["m_" + name] = s * _jax.random.normal(km, w.shape, _jnp.float32)
        out["v_" + name] = (s * s) * _jax.random.uniform(kv, w.shape, _jnp.float32, 0.5, 1.5)
    if N_MICROBATCH > 1:
        for name, axis in PER_EXAMPLE_BATCH_AXIS.items():
            out[name] = _to_microbatches(out[name], axis)
    return {'x': out['x'], 'mem': out['mem'], 'norm_g': out['norm_g'], 'w_in': out['w_in'], 'q_norm_g': out['q_norm_g'], 'k_norm_g': out['k_norm_g'], 'sg_ln_g': out['sg_ln_g'], 'sg_ln_b': out['sg_ln_b'], 'w_s': out['w_s'], 'b_s': out['b_s'], 'mem_norm_g': out['mem_norm_g'], 'w_mem_kv': out['w_mem_kv'], 'w_br': out['w_br'], 'w_out': out['w_out'], 'final_g': out['final_g'], 'loss_target': out['loss_target'], 'm_norm_g': out['m_norm_g'], 'm_w_in': out['m_w_in'], 'm_q_norm_g': out['m_q_norm_g'], 'm_k_norm_g': out['m_k_norm_g'], 'm_sg_ln_g': out['m_sg_ln_g'], 'm_sg_ln_b': out['m_sg_ln_b'], 'm_w_s': out['m_w_s'], 'm_b_s': out['m_b_s'], 'm_mem_norm_g': out['m_mem_norm_g'], 'm_w_mem_kv': out['m_w_mem_kv'], 'm_w_br': out['m_w_br'], 'm_w_out': out['m_w_out'], 'm_final_g': out['m_final_g'], 'v_norm_g': out['v_norm_g'], 'v_w_in': out['v_w_in'], 'v_q_norm_g': out['v_q_norm_g'], 'v_k_norm_g': out['v_k_norm_g'], 'v_sg_ln_g': out['v_sg_ln_g'], 'v_sg_ln_b': out['v_sg_ln_b'], 'v_w_s': out['v_w_s'], 'v_b_s': out['v_b_s'], 'v_mem_norm_g': out['v_mem_norm_g'], 'v_w_mem_kv': out['v_w_mem_kv'], 'v_w_br': out['v_w_br'], 'v_w_out': out['v_w_out'], 'v_final_g': out['v_final_g']}


def _loss(weights, diff, rest, loss_target):
    with _jax.named_scope("forward"):
        args = {**rest, TWIN_DIFF_INPUT: diff, **{k: w.astype(_WEIGHT_DTYPES[k]) for k, w in weights.items()}}
        y = _forward(args)
    with _jax.named_scope("loss_head"):
        err = _jnp.square(y.astype(_jnp.float32) - loss_target)
        return 0.5 * _jnp.sum(_jnp.mean(err, axis=-1)) if err.ndim else 0.5 * err


def _adamw(w, g, m, v):
    m = ADAM_B1 * m + (1.0 - ADAM_B1) * g
    v = ADAM_B2 * v + (1.0 - ADAM_B2) * _jnp.square(g)
    m_hat = m / (1.0 - ADAM_B1 ** ADAM_STEP)
    v_hat = v / (1.0 - ADAM_B2 ** ADAM_STEP)
    delta = -ADAM_LR * (m_hat / (_jnp.sqrt(v_hat) + ADAM_EPS) + ADAM_WD * w)
    return delta, m, v


def reference(x, mem, norm_g, w_in, q_norm_g, k_norm_g, sg_ln_g, sg_ln_b, w_s, b_s, mem_norm_g, w_mem_kv, w_br, w_out, final_g, loss_target, m_norm_g, m_w_in, m_q_norm_g, m_k_norm_g, m_sg_ln_g, m_sg_ln_b, m_w_s, m_b_s, m_mem_norm_g, m_w_mem_kv, m_w_br, m_w_out, m_final_g, v_norm_g, v_w_in, v_q_norm_g, v_k_norm_g, v_sg_ln_g, v_sg_ln_b, v_w_s, v_b_s, v_mem_norm_g, v_w_mem_kv, v_w_br, v_w_out, v_final_g):
    given = dict(x=x, mem=mem, norm_g=norm_g, w_in=w_in, q_norm_g=q_norm_g, k_norm_g=k_norm_g, sg_ln_g=sg_ln_g, sg_ln_b=sg_ln_b, w_s=w_s, b_s=b_s, mem_norm_g=mem_norm_g, w_mem_kv=w_mem_kv, w_br=w_br, w_out=w_out, final_g=final_g, loss_target=loss_target, m_norm_g=m_norm_g, m_w_in=m_w_in, m_q_norm_g=m_q_norm_g, m_k_norm_g=m_k_norm_g, m_sg_ln_g=m_sg_ln_g, m_sg_ln_b=m_sg_ln_b, m_w_s=m_w_s, m_b_s=m_b_s, m_mem_norm_g=m_mem_norm_g, m_w_mem_kv=m_w_mem_kv, m_w_br=m_w_br, m_w_out=m_w_out, m_final_g=m_final_g, v_norm_g=v_norm_g, v_w_in=v_w_in, v_q_norm_g=v_q_norm_g, v_k_norm_g=v_k_norm_g, v_sg_ln_g=v_sg_ln_g, v_sg_ln_b=v_sg_ln_b, v_w_s=v_w_s, v_b_s=v_b_s, v_mem_norm_g=v_mem_norm_g, v_w_mem_kv=v_w_mem_kv, v_w_br=v_w_br, v_w_out=v_w_out, v_final_g=v_final_g)
    weights = {n: given[n] for n in TWIN_WEIGHTS}
    shared = {n: given[n] for n in SHARED_INPUTS}
    per_example = {n: given[n] for n in ['x', 'mem']}
    grad_fn = _jax.value_and_grad(_loss, argnums=(0, 1))

    def one_microbatch(ex, loss_target):
        ex = dict(ex)
        diff = ex.pop(TWIN_DIFF_INPUT)
        return grad_fn(weights, diff, {**shared, **ex}, loss_target)

    if N_MICROBATCH == 1:
        loss, (grad_w, grad_x) = one_microbatch(per_example, given["loss_target"])
    else:
        def body(carry, xs):
            loss_sum, grad_sum = carry
            l_k, (gw_k, gx_k) = one_microbatch(xs[0], xs[1])
            with _jax.named_scope("update"):
                return (loss_sum + l_k, _jax.tree.map(_jnp.add, grad_sum, gw_k)), gx_k

        init = (_jnp.zeros((), _jnp.float32), _jax.tree.map(_jnp.zeros_like, weights))
        (loss, grad_w), grad_x = _jax.lax.scan(body, init, (per_example, given["loss_target"]))
    with _jax.named_scope("update"):
        delta_w, new_m, new_v = {}, {}, {}
        for n in TWIN_WEIGHTS:
            delta_w[n], new_m[n], new_v[n] = _adamw(weights[n], grad_w[n], given["m_" + n], given["v_" + n])
    return (loss, grad_x, *[grad_w[n] for n in TWIN_WEIGHTS], *[delta_w[n] for n in TWIN_WEIGHTS],
            *[new_m[n] for n in TWIN_WEIGHTS], *[new_v[n] for n in TWIN_WEIGHTS])
```

```python
import functools

import jax
import jax.numpy as jnp
from jax import lax
from jax.experimental import pallas as pl
from jax.experimental.pallas import tpu as pltpu

F32 = jnp.float32
BF16 = jnp.bfloat16
SDS = jax.ShapeDtypeStruct
MESH_ID = pl.DeviceIdType.MESH
AXES = ("x", "y", "c")
N_DEV = 8

D_MODEL = 1024
DEPTH = 4
GRID_W = 64
CHUNK = 128
ROPE_THETA = 10000.0
EPS = 1e-6
HEAD_DIM = 64
A_HEADS = 8
A_KV_HEADS = 2
B_GROUPS = 4
M_HEADS = 4
M_HEAD_DIM = 128
BW = 512
W_A = 1280
W_B = 1536
W_M = 1024
W_G = 3072
IN_WIDTH = W_A + W_B + W_M + W_G
GROUP_OFFS = (0, W_A, W_A + W_B, W_A + W_B + W_M, IN_WIDTH)

ADAM_LR = 0.001
ADAM_B1 = 0.9
ADAM_B2 = 0.999
ADAM_EPS = 1e-08
ADAM_WD = 0.01
ADAM_STEP = 10

LANES = 128
VMEM_LIMIT = 52 * 1024 * 1024


def _tile(n, cap, unit=LANES):
    if n <= cap:
        return n
    t = (cap // unit) * unit
    while n % t:
        t -= unit
    return t


def _params(*sem):
    return pltpu.CompilerParams(dimension_semantics=sem, vmem_limit_bytes=VMEM_LIMIT)


def _sigmoid(z):
    return 1.0 / (1.0 + jnp.exp(-z))


def _silu_and_grad(z):
    s = _sigmoid(z)
    return z * s, s * (1.0 + z * (1.0 - s))


def _dot(a, b, dims):
    return lax.dot_general(a, b, (dims, ((), ())), preferred_element_type=F32)


NN = ((1,), (0,))
NT = ((1,), (1,))
TN = ((0,), (0,))


def _core_major(d):
    return (d % 2) * 4 + d // 2


def _mm(a, b, mode, name, add=None, out_dtype=F32, tm_cap=512, tn_cap=1536, tk_cap=1536, rows_core_major=False):
    if mode == "nn":
        (m, k), (k2, n) = a.shape, b.shape
    elif mode == "nt":
        (m, k), (n, k2) = a.shape, b.shape
    else:
        (k, m), (k2, n) = a.shape, b.shape
    assert k == k2, (a.shape, b.shape, mode)
    tm = _tile(m, tm_cap, 8 if mode != "tn" else LANES)
    tn = _tile(n, tn_cap)
    tk = _tile(k, tk_cap, LANES if mode != "tn" else 16)
    nk = k // tk
    dims = {"nn": NN, "nt": NT, "tn": TN}[mode]
    a_spec = {"nn": pl.BlockSpec((tm, tk), lambda i, j, kk: (i, kk)),
              "nt": pl.BlockSpec((tm, tk), lambda i, j, kk: (i, kk)),
              "tn": pl.BlockSpec((tk, tm), lambda i, j, kk: (kk, i))}[mode]
    b_spec = {"nn": pl.BlockSpec((tk, tn), lambda i, j, kk: (kk, j)),
              "nt": pl.BlockSpec((tn, tk), lambda i, j, kk: (j, kk)),
              "tn": pl.BlockSpec((tk, tn), lambda i, j, kk: (kk, j))}[mode]
    if rows_core_major:
        assert m // tm == N_DEV, (m, tm)
        o_spec = pl.BlockSpec((tm, tn), lambda i, j, kk: (_core_major(i), j))
    else:
        o_spec = pl.BlockSpec((tm, tn), lambda i, j, kk: (i, j))
    has_add = add is not None

    def body(*refs):
        a_ref, b_ref = refs[0], refs[1]
        add_ref = refs[2] if has_add else None
        o_ref = refs[3] if has_add else refs[2]
        part = _dot(a_ref[...].astype(BF16), b_ref[...].astype(BF16), dims)
        if nk == 1:
            if has_add:
                part = part + add_ref[...]
            o_ref[...] = part.astype(out_dtype)
        else:
            acc = refs[-1]
            kk = pl.program_id(2)

            @pl.when(kk == 0)
            def _():
                acc[...] = part + add_ref[...] if has_add else part

            @pl.when(kk > 0)
            def _():
                acc[...] += part

            @pl.when(kk == nk - 1)
            def _():
                o_ref[...] = acc[...].astype(out_dtype)

    ins = [a, b] + ([add] if has_add else [])
    in_specs = [a_spec, b_spec] + ([o_spec] if has_add else [])
    return pl.pallas_call(
        body, name=name, grid=(m // tm, n // tn, nk), in_specs=in_specs, out_specs=o_spec,
        out_shape=SDS((m, n), out_dtype),
        scratch_shapes=[pltpu.VMEM((tm, tn), F32)] if nk > 1 else [],
        compiler_params=_params("parallel", "parallel", "arbitrary"),
    )(*ins)


def _rmsnorm_fwd(x, g, name):
    t, d = x.shape
    tm = _tile(t, 512, 8)

    def body(x_ref, g_ref, h_ref):
        xf = x_ref[...]
        r = lax.rsqrt(jnp.mean(xf * xf, axis=-1, keepdims=True) + EPS)
        h_ref[...] = (xf * r * g_ref[...]).astype(BF16)

    return pl.pallas_call(
        body, name=name, grid=(t // tm,),
        in_specs=[pl.BlockSpec((tm, d), lambda i: (i, 0)), pl.BlockSpec((1, d), lambda i: (0, 0))],
        out_specs=pl.BlockSpec((tm, d), lambda i: (i, 0)),
        out_shape=SDS((t, d), BF16), compiler_params=_params("parallel"),
    )(x, g)


def _rmsnorm_bwd(x, g, dh, dres, name):
    t, d = x.shape
    tm = _tile(t, 512, 8)
    has_res = dres is not None

    def body(*refs):
        x_ref, g_ref, dh_ref = refs[:3]
        dx_ref, dg_ref = refs[-2:]
        xf = x_ref[...]
        r = lax.rsqrt(jnp.mean(xf * xf, axis=-1, keepdims=True) + EPS)
        xh = xf * r
        dhf = dh_ref[...].astype(F32)
        gd = dhf * g_ref[...]
        dx = r * (gd - xh * jnp.mean(gd * xh, axis=-1, keepdims=True))
        if has_res:
            dx = dx + refs[3][...]
        dx_ref[...] = dx

        @pl.when(pl.program_id(0) == 0)
        def _():
            dg_ref[...] = jnp.zeros_like(dg_ref)

        dg_ref[...] += jnp.sum(dhf * xh, axis=0, keepdims=True)

    row = pl.BlockSpec((tm, d), lambda i: (i, 0))
    vec = pl.BlockSpec((1, d), lambda i: (0, 0))
    return pl.pallas_call(
        body, name=name, grid=(t // tm,),
        in_specs=[row, vec, row] + ([row] if has_res else []),
        out_specs=(row, vec), out_shape=(SDS((t, d), F32), SDS((1, d), F32)),
        compiler_params=_params("arbitrary"),
    )(*([x, g, dh] + ([dres] if has_res else [])))


def _rope_tables(t):
    rows = t // GRID_W
    row = jnp.repeat(jnp.arange(rows, dtype=F32), GRID_W)
    col = jnp.tile(jnp.arange(GRID_W, dtype=F32), rows)
    n_freq = HEAD_DIM // 4
    inv = ROPE_THETA ** (-jnp.arange(n_freq, dtype=F32) / n_freq)
    ang = jnp.stack([row[:, None] * inv, col[:, None] * inv], axis=1)
    cos, sin = jnp.cos(ang), jnp.sin(ang)
    c64 = jnp.concatenate([cos[:, 0], cos[:, 0], cos[:, 1], cos[:, 1]], axis=-1)
    s64 = jnp.concatenate([-sin[:, 0], sin[:, 0], -sin[:, 1], sin[:, 1]], axis=-1)
    return jnp.tile(c64, (1, 2)), jnp.tile(s64, (1, 2))


def _head_sums(v, lane):
    lo = jnp.sum(jnp.where(lane < HEAD_DIM, v, 0.0), axis=-1, keepdims=True)
    hi = jnp.sum(jnp.where(lane < HEAD_DIM, 0.0, v), axis=-1, keepdims=True)
    return jnp.where(lane < HEAD_DIM, lo, hi)


def _swap16(v, lane):
    return jnp.where((lane % 32) < 16, pltpu.roll(v, LANES - 16, 1), pltpu.roll(v, 16, 1))


def _attn_prep_fwd(pa, cs, sn, qg2, kg2):
    t = pa.shape[0]
    tq = _tile(t, 512, 8)
    scale = HEAD_DIM ** -0.5

    def body(pa_ref, cs_ref, sn_ref, qg_ref, kg_ref, q_ref, k_ref, v_ref):
        lane = lax.broadcasted_iota(jnp.int32, (tq, LANES), 1)
        c, s = cs_ref[...], sn_ref[...]

        def norm_rope(xg, g2):
            r = lax.rsqrt(_head_sums(xg * xg, lane) * (1.0 / HEAD_DIM) + EPS)
            xn = xg * r * g2
            return xn * c + _swap16(xn, lane) * s

        for gi in range(4):
            sl = slice(gi * LANES, (gi + 1) * LANES)
            q_ref[:, sl] = (norm_rope(pa_ref[:, sl], qg_ref[...]) * scale).astype(BF16)
        kr = norm_rope(pa_ref[:, 512:640], kg_ref[...]).astype(BF16)
        vv = pa_ref[:, 640:768].astype(BF16)
        for kvh in range(A_KV_HEADS):
            hs = slice(kvh * HEAD_DIM, (kvh + 1) * HEAD_DIM)
            k_ref[kvh] = kr[:, hs]
            v_ref[kvh] = vv[:, hs]

    row = lambda w: pl.BlockSpec((tq, w), lambda i: (i, 0))
    vec = pl.BlockSpec((1, LANES), lambda i: (0, 0))
    hm = pl.BlockSpec((A_KV_HEADS, tq, HEAD_DIM), lambda i: (0, i, 0))
    return pl.pallas_call(
        body, name="attn_prep_fwd", grid=(t // tq,),
        in_specs=[row(W_A), row(LANES), row(LANES), vec, vec],
        out_specs=(row(BW), hm, hm),
        out_shape=(SDS((t, BW), BF16), SDS((A_KV_HEADS, t, HEAD_DIM), BF16), SDS((A_KV_HEADS, t, HEAD_DIM), BF16)),
        compiler_params=_params("parallel"),
    )(pa, cs, sn, qg2, kg2)


def _attn_fwd(q, k, v, pa):
    t = q.shape[0]
    tq = _tile(t, 256, 8)
    grp = A_HEADS // A_KV_HEADS

    def body(q_ref, k_ref, v_ref, pa_ref, o_ref, lse_ref, y_ref):
        for h in range(A_HEADS):
            kvh, j = h // grp, h % grp
            hs = slice(h * HEAD_DIM, (h + 1) * HEAD_DIM)
            s = _dot(q_ref[:, hs], k_ref[kvh], NT)
            m = jnp.max(s, axis=-1, keepdims=True)
            p = jnp.exp(s - m)
            l = jnp.sum(p, axis=-1, keepdims=True)
            o = _dot(p.astype(BF16), v_ref[kvh], NN) / l
            o_ref[:, hs] = o
            lse_ref[kvh, :, j:j + 1] = m + jnp.log(l)
            z = pa_ref[:, 768 + h * HEAD_DIM:768 + (h + 1) * HEAD_DIM]
            y_ref[:, hs] = (o * (z * _sigmoid(z))).astype(BF16)

    row = lambda w: pl.BlockSpec((tq, w), lambda i: (i, 0))
    full = pl.BlockSpec((A_KV_HEADS, t, HEAD_DIM), lambda i: (0, 0, 0))
    return pl.pallas_call(
        body, name="attn_fwd", grid=(t // tq,),
        in_specs=[row(BW), full, full, row(W_A)],
        out_specs=(row(BW), pl.BlockSpec((A_KV_HEADS, tq, grp), lambda i: (0, i, 0)), row(BW)),
        out_shape=(SDS((t, BW), F32), SDS((A_KV_HEADS, t, grp), F32), SDS((t, BW), BF16)),
        compiler_params=_params("parallel"),
    )(q, k, v, pa)


def _attn_bwd(q, k, v, pa, o, lse, dy):
    t = q.shape[0]
    tq = _tile(t, 256, 8)
    grp = A_HEADS // A_KV_HEADS
    gw = grp * HEAD_DIM

    def body(q_ref, k_ref, v_ref, z_ref, o_ref, lse_ref, dy_ref, dq_ref, dk_ref, dv_ref):
        @pl.when(pl.program_id(1) == 0)
        def _():
            dk_ref[...] = jnp.zeros_like(dk_ref)
            dv_ref[...] = jnp.zeros_like(dv_ref)

        kk, vv = k_ref[...], v_ref[...]
        for j in range(grp):
            hs = slice(j * HEAD_DIM, (j + 1) * HEAD_DIM)
            qj = q_ref[:, hs]
            z = z_ref[:, hs]
            oj = o_ref[:, hs]
            do = dy_ref[:, hs] * (z * _sigmoid(z))
            delta = jnp.sum(do * oj, axis=-1, keepdims=True)
            do_b = do.astype(BF16)
            s = _dot(qj, kk, NT)
            p = jnp.exp(s - lse_ref[:, j:j + 1])
            dp = _dot(do_b, vv, NT)
            ds_b = (p * (dp - delta)).astype(BF16)
            dq_ref[:, hs] = _dot(ds_b, kk, NN)
            dk_ref[...] += _dot(ds_b, qj, TN)
            dv_ref[...] += _dot(p.astype(BF16), do_b, TN)

    grp_blk = pl.BlockSpec((tq, gw), lambda g, i: (i, g))
    kv_blk = pl.BlockSpec((None, t, HEAD_DIM), lambda g, i: (g, 0, 0))
    return pl.pallas_call(
        body, name="attn_bwd", grid=(A_KV_HEADS, t // tq),
        in_specs=[grp_blk, kv_blk, kv_blk,
                  pl.BlockSpec((tq, gw), lambda g, i: (i, 768 // gw + g)),
                  grp_blk, pl.BlockSpec((None, tq, grp), lambda g, i: (g, i, 0)), grp_blk],
        out_specs=(grp_blk, kv_blk, kv_blk),
        out_shape=(SDS((t, BW), F32), SDS((A_KV_HEADS, t, HEAD_DIM), F32), SDS((A_KV_HEADS, t, HEAD_DIM), F32)),
        compiler_params=_params("arbitrary", "arbitrary"),
    )(q, k, v, pa, o, lse, dy)


def _attn_prep_bwd(pa, cs, sn, qg2, kg2, dq, dk, dv, dy, o):
    t = pa.shape[0]
    tq = _tile(t, 512, 8)
    scale = HEAD_DIM ** -0.5

    def body(pa_ref, cs_ref, sn_ref, qg_ref, kg_ref, dq_ref, dk_ref, dv_ref, dy_ref, o_ref, dpa_ref, dqg_ref, dkg_ref):
        lane = lax.broadcasted_iota(jnp.int32, (tq, LANES), 1)
        c, s = cs_ref[...], sn_ref[...]

        @pl.when(pl.program_id(0) == 0)
        def _():
            dqg_ref[...] = jnp.zeros_like(dqg_ref)
            dkg_ref[...] = jnp.zeros_like(dkg_ref)

        def norm_rope_bwd(xg, g2, dout):
            r = lax.rsqrt(_head_sums(xg * xg, lane) * (1.0 / HEAD_DIM) + EPS)
            xh = xg * r
            dxn = dout * c + _swap16(dout * s, lane)
            gd = dxn * g2
            dx = r * (gd - xh * (_head_sums(gd * xh, lane) * (1.0 / HEAD_DIM)))
            return dx, jnp.sum(dxn * xh, axis=0, keepdims=True)

        for gi in range(4):
            sl = slice(gi * LANES, (gi + 1) * LANES)
            dx, dg = norm_rope_bwd(pa_ref[:, sl], qg_ref[...], dq_ref[:, sl] * scale)
            dpa_ref[:, sl] = dx.astype(BF16)
            dqg_ref[...] += dg
        dx, dg = norm_rope_bwd(pa_ref[:, 512:640], kg_ref[...], dk_ref[...])
        dpa_ref[:, 512:640] = dx.astype(BF16)
        dkg_ref[...] += dg
        dpa_ref[:, 640:768] = dv_ref[...].astype(BF16)
        z = pa_ref[:, 768:1280]
        _, dsilu = _silu_and_grad(z)
        dpa_ref[:, 768:1280] = (dy_ref[...] * o_ref[...] * dsilu).astype(BF16)

    row = lambda w: pl.BlockSpec((tq, w), lambda i: (i, 0))
    vec = pl.BlockSpec((1, LANES), lambda i: (0, 0))
    return pl.pallas_call(
        body, name="attn_prep_bwd", grid=(t // tq,),
        in_specs=[row(W_A), row(LANES), row(LANES), vec, vec, row(BW), row(LANES), row(LANES), row(BW), row(BW)],
        out_specs=(row(W_A), vec, vec),
        out_shape=(SDS((t, W_A), BF16), SDS((1, LANES), F32), SDS((1, LANES), F32)),
        compiler_params=_params("arbitrary"),
    )(pa, cs, sn, qg2, kg2, dq, dk, dv, dy, o)


def _layer_norm(v, g, b):
    mu = jnp.mean(v, axis=-1, keepdims=True)
    xc = v - mu
    rs = lax.rsqrt(jnp.mean(xc * xc, axis=-1, keepdims=True) + EPS)
    xh = xc * rs
    return xh * g + b, xh, rs


def _gmlp_fwd(pb, lng, lnb, ws, bs):
    t = pb.shape[0]
    tb = _tile(t, 512, CHUNK)

    def body(pb_ref, g_ref, b_ref, ws_ref, bs_ref, y_ref):
        vln, _, _ = _layer_norm(pb_ref[:, BW:2 * BW], g_ref[...], b_ref[...])
        vb = vln.astype(BF16)
        for gi in range(B_GROUPS):
            w = ws_ref[gi].astype(BF16)
            cs_ = slice(gi * CHUNK, (gi + 1) * CHUNK)
            for n in range(tb // CHUNK):
                rs_ = slice(n * CHUNK, (n + 1) * CHUNK)
                mixed = _dot(w, vb[rs_, cs_], NN) + bs_ref[gi]
                z = pb_ref[rs_, 2 * BW + gi * CHUNK:2 * BW + (gi + 1) * CHUNK]
                y_ref[rs_, cs_] = (pb_ref[rs_, cs_] * mixed * (z * _sigmoid(z))).astype(BF16)

    return pl.pallas_call(
        body, name="gmlp_fwd", grid=(t // tb,),
        in_specs=[pl.BlockSpec((tb, W_B), lambda i: (i, 0)),
                  pl.BlockSpec((1, BW), lambda i: (0, 0)), pl.BlockSpec((1, BW), lambda i: (0, 0)),
                  pl.BlockSpec((B_GROUPS, CHUNK, CHUNK), lambda i: (0, 0, 0)),
                  pl.BlockSpec((B_GROUPS, CHUNK, 1), lambda i: (0, 0, 0))],
        out_specs=pl.BlockSpec((tb, BW), lambda i: (i, 0)),
        out_shape=SDS((t, BW), BF16), compiler_params=_params("parallel"),
    )(pb, lng, lnb, ws, bs)


def _gmlp_bwd(pb, lng, lnb, ws, bs, dy):
    t = pb.shape[0]
    tb = _tile(t, 256, CHUNK)

    def body(pb_ref, g_ref, b_ref, ws_ref, bs_ref, dy_ref, dpb_ref, dws_ref, dbs_ref, dg_ref, db_ref, dvln_ref):
        @pl.when(pl.program_id(0) == 0)
        def _():
            dws_ref[...] = jnp.zeros_like(dws_ref)
            dbs_ref[...] = jnp.zeros_like(dbs_ref)
            dg_ref[...] = jnp.zeros_like(dg_ref)
            db_ref[...] = jnp.zeros_like(db_ref)

        vln, xh, rs = _layer_norm(pb_ref[:, BW:2 * BW], g_ref[...], b_ref[...])
        vb = vln.astype(BF16)
        for gi in range(B_GROUPS):
            w = ws_ref[gi].astype(BF16)
            cs_ = slice(gi * CHUNK, (gi + 1) * CHUNK)
            for n in range(tb // CHUNK):
                rs_ = slice(n * CHUNK, (n + 1) * CHUNK)
                vbc = vb[rs_, cs_]
                mixed = _dot(w, vbc, NN) + bs_ref[gi]
                zs = slice(2 * BW + gi * CHUNK, 2 * BW + (gi + 1) * CHUNK)
                z = pb_ref[rs_, zs]
                u = pb_ref[rs_, cs_]
                sil, dsil = _silu_and_grad(z)
                dyc = dy_ref[rs_, cs_]
                dmixed = dyc * u * sil
                dpb_ref[rs_, cs_] = (dyc * mixed * sil).astype(BF16)
                dpb_ref[rs_, zs] = (dyc * u * mixed * dsil).astype(BF16)
                dmb = dmixed.astype(BF16)
                dws_ref[gi] += _dot(dmb, vbc, NT)
                dbs_ref[gi] += jnp.sum(dmixed, axis=-1, keepdims=True)
                dvln_ref[rs_, cs_] = _dot(w, dmb, TN)
        dvln = dvln_ref[...]
        dg_ref[...] += jnp.sum(dvln * xh, axis=0, keepdims=True)
        db_ref[...] += jnp.sum(dvln, axis=0, keepdims=True)
        gd = dvln * g_ref[...]
        dv = rs * (gd - jnp.mean(gd, axis=-1, keepdims=True) - xh * jnp.mean(gd * xh, axis=-1, keepdims=True))
        dpb_ref[:, BW:2 * BW] = dv.astype(BF16)

    vec = pl.BlockSpec((1, BW), lambda i: (0, 0))
    wsb = pl.BlockSpec((B_GROUPS, CHUNK, CHUNK), lambda i: (0, 0, 0))
    bsb = pl.BlockSpec((B_GROUPS, CHUNK, 1), lambda i: (0, 0, 0))
    return pl.pallas_call(
        body, name="gmlp_bwd", grid=(t // tb,),
        in_specs=[pl.BlockSpec((tb, W_B), lambda i: (i, 0)), vec, vec, wsb, bsb, pl.BlockSpec((tb, BW), lambda i: (i, 0))],
        out_specs=(pl.BlockSpec((tb, W_B), lambda i: (i, 0)), wsb, bsb, vec, vec),
        out_shape=(SDS((t, W_B), BF16), SDS((B_GROUPS, CHUNK, CHUNK), F32), SDS((B_GROUPS, CHUNK, 1), F32),
                   SDS((1, BW), F32), SDS((1, BW), F32)),
        scratch_shapes=[pltpu.VMEM((tb, BW), F32)],
        compiler_params=_params("arbitrary"),
    )(pb, lng, lnb, ws, bs, dy)


def _mem_softmax(qh, kh):
    s = _dot(qh, kh, NT) * (M_HEAD_DIM ** -0.5)
    e = jnp.exp(s - jnp.max(s, axis=-1, keepdims=True))
    return e / jnp.sum(e, axis=-1, keepdims=True)


def _mem_attn_fwd(pm, kv):
    t = pm.shape[0]
    tq = _tile(t, 512, 8)
    ml = kv.shape[0]

    def body(pm_ref, kv_ref, y_ref):
        for h in range(M_HEADS):
            hs = slice(h * M_HEAD_DIM, (h + 1) * M_HEAD_DIM)
            kh = kv_ref[:, hs].astype(BF16)
            vh = kv_ref[:, BW + h * M_HEAD_DIM:BW + (h + 1) * M_HEAD_DIM].astype(BF16)
            p = _mem_softmax(pm_ref[:, hs].astype(BF16), kh)
            o = _dot(p.astype(BF16), vh, NN)
            z = pm_ref[:, BW + h * M_HEAD_DIM:BW + (h + 1) * M_HEAD_DIM]
            y_ref[:, hs] = (o * (z * _sigmoid(z))).astype(BF16)

    return pl.pallas_call(
        body, name="mem_attn_fwd", grid=(t // tq,),
        in_specs=[pl.BlockSpec((tq, W_M), lambda i: (i, 0)), pl.BlockSpec((ml, 2 * BW), lambda i: (0, 0))],
        out_specs=pl.BlockSpec((tq, BW), lambda i: (i, 0)),
        out_shape=SDS((t, BW), BF16), compiler_params=_params("parallel"),
    )(pm, kv)


def _mem_attn_bwd(pm, kv, dy):
    t = pm.shape[0]
    tq = _tile(t, 512, 8)
    ml = kv.shape[0]
    scale = M_HEAD_DIM ** -0.5

    def body(pm_ref, kv_ref, dy_ref, dpm_ref, dkv_ref):
        @pl.when(pl.program_id(0) == 0)
        def _():
            dkv_ref[...] = jnp.zeros_like(dkv_ref)

        for h in range(M_HEADS):
            hs = slice(h * M_HEAD_DIM, (h + 1) * M_HEAD_DIM)
            zs = slice(BW + h * M_HEAD_DIM, BW + (h + 1) * M_HEAD_DIM)
            kh = kv_ref[:, hs].astype(BF16)
            vh = kv_ref[:, zs].astype(BF16)
            qh = pm_ref[:, hs].astype(BF16)
            p = _mem_softmax(qh, kh)
            pb = p.astype(BF16)
            o = _dot(pb, vh, NN)
            sil, dsil = _silu_and_grad(pm_ref[:, zs])
            dyh = dy_ref[:, hs]
            do = dyh * sil
            dpm_ref[:, zs] = (dyh * o * dsil).astype(BF16)
            delta = jnp.sum(do * o, axis=-1, keepdims=True)
            do_b = do.astype(BF16)
            dp = _dot(do_b, vh, NT)
            dr_b = (p * (dp - delta) * scale).astype(BF16)
            dpm_ref[:, hs] = _dot(dr_b, kh, NN).astype(BF16)
            dkv_ref[:, hs] += _dot(dr_b, qh, TN)
            dkv_ref[:, zs] += _dot(pb, do_b, TN)

    kvb = pl.BlockSpec((ml, 2 * BW), lambda i: (0, 0))
    return pl.pallas_call(
        body, name="mem_attn_bwd", grid=(t // tq,),
        in_specs=[pl.BlockSpec((tq, W_M), lambda i: (i, 0)), kvb, pl.BlockSpec((tq, BW), lambda i: (i, 0))],
        out_specs=(pl.BlockSpec((tq, W_M), lambda i: (i, 0)), kvb),
        out_shape=(SDS((t, W_M), BF16), SDS((ml, 2 * BW), F32)),
        compiler_params=_params("arbitrary"),
    )(pm, kv, dy)


def _merge_fwd(ya, yb, ym, pg, wbr):
    t = ya.shape[0]
    tm = _tile(t, 512, 8)

    def body(ya_ref, yb_ref, ym_ref, pg_ref, w_ref, m_ref):
        acc = None
        for n, y_ref in enumerate((ya_ref, yb_ref, ym_ref)):
            up = _dot(y_ref[...], w_ref[n], NN)
            term = _sigmoid(pg_ref[:, n * D_MODEL:(n + 1) * D_MODEL]) * up
            acc = term if acc is None else acc + term
        m_ref[...] = acc.astype(BF16)

    yb_spec = pl.BlockSpec((tm, BW), lambda i: (i, 0))
    return pl.pallas_call(
        body, name="merge_fwd", grid=(t // tm,),
        in_specs=[yb_spec, yb_spec, yb_spec, pl.BlockSpec((tm, W_G), lambda i: (i, 0)),
                  pl.BlockSpec((3, BW, D_MODEL), lambda i: (0, 0, 0))],
        out_specs=pl.BlockSpec((tm, D_MODEL), lambda i: (i, 0)),
        out_shape=SDS((t, D_MODEL), BF16), compiler_params=_params("parallel"),
    )(ya, yb, ym, pg, wbr)


def _merge_bwd(ya, yb, ym, pg, wbr, dm):
    t = ya.shape[0]
    tm = _tile(t, 256, 8)

    def body(ya_ref, yb_ref, ym_ref, pg_ref, w_ref, dm_ref, dya_ref, dyb_ref, dym_ref, dpg_ref, dw_ref):
        @pl.when(pl.program_id(0) == 0)
        def _():
            dw_ref[...] = jnp.zeros_like(dw_ref)

        dmf = dm_ref[...]
        for n, (y_ref, dy_ref) in enumerate(((ya_ref, dya_ref), (yb_ref, dyb_ref), (ym_ref, dym_ref))):
            cs_ = slice(n * D_MODEL, (n + 1) * D_MODEL)
            y = y_ref[...]
            w = w_ref[n]
            up = _dot(y, w, NN)
            gt = _sigmoid(pg_ref[:, cs_])
            dpg_ref[:, cs_] = (dmf * up * gt * (1.0 - gt)).astype(BF16)
            dup = (dmf * gt).astype(BF16)
            dy_ref[...] = _dot(dup, w, NT)
            dw_ref[n] += _dot(y, dup, TN)

    y_spec = pl.BlockSpec((tm, BW), lambda i: (i, 0))
    w_spec = pl.BlockSpec((3, BW, D_MODEL), lambda i: (0, 0, 0))
    return pl.pallas_call(
        body, name="merge_bwd", grid=(t // tm,),
        in_specs=[y_spec, y_spec, y_spec, pl.BlockSpec((tm, W_G), lambda i: (i, 0)), w_spec,
                  pl.BlockSpec((tm, D_MODEL), lambda i: (i, 0))],
        out_specs=(y_spec, y_spec, y_spec, pl.BlockSpec((tm, W_G), lambda i: (i, 0)), w_spec),
        out_shape=(SDS((t, BW), F32), SDS((t, BW), F32), SDS((t, BW), F32), SDS((t, W_G), BF16),
                   SDS((3, BW, D_MODEL), F32)),
        compiler_params=_params("arbitrary"),
    )(ya, yb, ym, pg, wbr, dm)


def _loss_head(x, g, target):
    t, d = x.shape
    tm = _tile(t, 512, 8)

    def body(x_ref, g_ref, t_ref, loss_ref, dx_ref, dg_ref):
        @pl.when(pl.program_id(0) == 0)
        def _():
            loss_ref[...] = jnp.zeros_like(loss_ref)
            dg_ref[...] = jnp.zeros_like(dg_ref)

        xf = x_ref[...]
        r = lax.rsqrt(jnp.mean(xf * xf, axis=-1, keepdims=True) + EPS)
        xh = xf * r
        err = xh * g_ref[...] - t_ref[...]
        per_tok = jnp.mean(err * err, axis=-1, keepdims=True)
        loss_ref[...] += 0.5 * jnp.sum(per_tok, axis=0, keepdims=True)
        dy = err * (1.0 / d)
        gd = dy * g_ref[...]
        dx_ref[...] = r * (gd - xh * jnp.mean(gd * xh, axis=-1, keepdims=True))
        dg_ref[...] += jnp.sum(dy * xh, axis=0, keepdims=True)

    row = pl.BlockSpec((tm, d), lambda i: (i, 0))
    vec = pl.BlockSpec((1, d), lambda i: (0, 0))
    return pl.pallas_call(
        body, name="loss_head", grid=(t // tm,),
        in_specs=[row, vec, row],
        out_specs=(pl.BlockSpec((1, 1), lambda i: (0, 0)), row, vec),
        out_shape=(SDS((1, 1), F32), SDS((t, d), F32), SDS((1, d), F32)),
        compiler_params=_params("arbitrary"),
    )(x, g, target)


def _layer_fwd(x, mem, w, tabs):
    cs, sn = tabs
    h = _rmsnorm_fwd(x, w["norm_g"], "rmsnorm_fwd")
    pa = _mm(h, w["w_a"], "nn", "proj_a")
    pb = _mm(h, w["w_b"], "nn", "proj_b")
    pm = _mm(h, w["w_m"], "nn", "proj_m")
    pg = _mm(h, w["w_g"], "nn", "proj_g")
    q, k, v = _attn_prep_fwd(pa, cs, sn, w["qg2"], w["kg2"])
    o, lse, ya = _attn_fwd(q, k, v, pa)
    yb = _gmlp_fwd(pb, w["ln_g"], w["ln_b"], w["w_s"], w["b_s"])
    memn = _rmsnorm_fwd(mem, w["mem_g"], "mem_rmsnorm_fwd")
    kv = _mm(memn, w["w_kv"], "nn", "mem_kv")
    ym = _mem_attn_fwd(pm, kv)
    merged = _merge_fwd(ya, yb, ym, pg, w["w_br"])
    x_next = _mm(merged, w["w_out"], "nn", "out_proj", add=x)
    saved = dict(x=x, h=h, pa=pa, pb=pb, pm=pm, pg=pg, q=q, k=k, v=v, o=o, lse=lse, ya=ya, yb=yb, ym=ym,
                 memn=memn, kv=kv, merged=merged)
    return x_next, saved


def _layer_bwd(dx_out, mem, w, s, tabs):
    cs, sn = tabs
    t = dx_out.shape[0]
    dmerged = _mm(dx_out, w["w_out"], "nt", "d_merged")
    shard_rows = D_MODEL // N_DEV
    d_w_out = _mm(s["merged"], dx_out, "tn", "d_w_out", out_dtype=BF16, tm_cap=shard_rows, rows_core_major=True)
    dya, dyb, dym, dpg, d_w_br = _merge_bwd(s["ya"], s["yb"], s["ym"], s["pg"], w["w_br"], dmerged)
    dq, dk, dv = _attn_bwd(s["q"], s["k"], s["v"], s["pa"], s["o"], s["lse"], dya)
    dk_tm = dk.transpose(1, 0, 2).reshape(t, A_KV_HEADS * HEAD_DIM)
    dv_tm = dv.transpose(1, 0, 2).reshape(t, A_KV_HEADS * HEAD_DIM)
    dpa, d_qg2, d_kg2 = _attn_prep_bwd(s["pa"], cs, sn, w["qg2"], w["kg2"], dq, dk_tm, dv_tm, dya, s["o"])
    dpb, d_w_s, d_b_s, d_ln_g, d_ln_b = _gmlp_bwd(s["pb"], w["ln_g"], w["ln_b"], w["w_s"], w["b_s"], dyb)
    dpm, dkv = _mem_attn_bwd(s["pm"], s["kv"], dym)
    d_w_kv = _mm(s["memn"], dkv, "tn", "d_w_kv", out_dtype=BF16, tm_cap=shard_rows, rows_core_major=True)
    dmemn = _mm(dkv, w["w_kv"], "nt", "d_memn")
    _, d_mem_g = _rmsnorm_bwd(mem, w["mem_g"], dmemn, None, "mem_rmsnorm_bwd")
    dh = _mm(dpa, w["w_a"], "nt", "dh_a")
    dh = _mm(dpb, w["w_b"], "nt", "dh_b", add=dh)
    dh = _mm(dpm, w["w_m"], "nt", "dh_m", add=dh)
    dh = _mm(dpg, w["w_g"], "nt", "dh_g", add=dh)
    d_w_in = _split_w_in(*[_mm(s["h"], dp, "tn", "d_w_" + nm, out_dtype=BF16, tm_cap=1024)
                           for nm, dp in (("a", dpa), ("b", dpb), ("m", dpm), ("g", dpg))])
    dx_in, d_norm_g = _rmsnorm_bwd(s["x"], w["norm_g"], dh, dx_out, "rmsnorm_bwd")
    grads = dict(w_in=d_w_in[:, None], w_mem_kv=d_w_kv.reshape(N_DEV, 1, shard_rows, -1), w_br=_split_w_br(d_w_br),
                 w_out=d_w_out.reshape(N_DEV, 1, shard_rows, -1),
                 norm_g=d_norm_g[0], q_norm_g=d_qg2[0, :HEAD_DIM] + d_qg2[0, HEAD_DIM:],
                 k_norm_g=d_kg2[0, :HEAD_DIM] + d_kg2[0, HEAD_DIM:], sg_ln_g=d_ln_g[0], sg_ln_b=d_ln_b[0],
                 w_s=d_w_s, b_s=d_b_s[:, :, 0], mem_norm_g=d_mem_g[0])
    return dx_in, grads


def _local_step(x, mem, target, layers, final_g):
    tabs = _rope_tables(x.shape[0])
    saved = []
    for w in layers:
        x, s = _layer_fwd(x, mem, w, tabs)
        saved.append(s)
    loss, dx, d_final_g = _loss_head(x, final_g.reshape(1, -1), target)
    grads = [None] * len(layers)
    for l in reversed(range(len(layers))):
        dx, grads[l] = _layer_bwd(dx, mem, layers[l], saved[l], tabs)
    return loss, dx, grads, d_final_g[0]


def _layer_weights(l, w_groups, w_kv, w_br, w_out, small):
    tile2 = lambda g: jnp.tile(g.reshape(1, -1), (1, 2))
    w_a, w_b, w_m, w_g = w_groups
    return dict(
        w_a=w_a, w_b=w_b, w_m=w_m, w_g=w_g, w_kv=w_kv, w_br=w_br, w_out=w_out,
        norm_g=small["norm_g"][l].reshape(1, -1), qg2=tile2(small["q_norm_g"][l]), kg2=tile2(small["k_norm_g"][l]),
        ln_g=small["sg_ln_g"][l].reshape(1, -1), ln_b=small["sg_ln_b"][l].reshape(1, -1),
        w_s=small["w_s"][l], b_s=small["b_s"][l][:, :, None], mem_g=small["mem_norm_g"][l].reshape(1, -1))


def _all_gather(xs, name):
    n = len(xs)
    hbm = pl.BlockSpec(memory_space=pl.ANY)

    def body(*refs):
        x_refs, out_refs = refs[:n], refs[n:2 * n]
        send_sems, recv_sems, local_sems = refs[2 * n:]
        x, y, c = lax.axis_index("x"), lax.axis_index("y"), lax.axis_index("c")
        me, sibling = (x, y, c), (x, y, 1 - c)
        chips = [(1 - x, y), (x, 1 - y), (1 - x, 1 - y)]

        def blk(t, px, py, pc):
            return out_refs[t].at[4 * px + 2 * py + pc]

        def copy(t, k, block, to, own=False):
            return pltpu.make_async_remote_copy(
                src_ref=x_refs[t] if own else blk(t, *block), dst_ref=blk(t, *block),
                send_sem=send_sems.at[t, k], recv_sem=recv_sems.at[t, k], device_id=to, device_id_type=MESH_ID)

        mine = [pltpu.make_async_copy(x_refs[t], blk(t, *me), local_sems.at[t]) for t in range(n)]
        for cp in mine:
            cp.start()
        first = []
        for t in range(n):
            first.append(copy(t, 0, me, sibling, own=True))
            first += [copy(t, 1 + j, me, (*chip, c), own=True) for j, chip in enumerate(chips)]
        for cp in first:
            cp.start()
        passed = []
        for j, chip in enumerate(chips):
            for t in range(n):
                copy(t, 1 + j, (*chip, c), me).wait_recv()
                passed.append(copy(t, 4 + j, (*chip, c), sibling))
                passed[-1].start()
        for t in range(n):
            copy(t, 0, sibling, me).wait_recv()
            for j, chip in enumerate(chips):
                copy(t, 4 + j, (*chip, 1 - c), me).wait_recv()
        for cp in first + passed:
            cp.wait_send()
        for cp in mine:
            cp.wait()

    return pl.pallas_call(
        body, name=name, out_shape=tuple(SDS((N_DEV,) + a.shape, a.dtype) for a in xs),
        in_specs=[hbm] * n, out_specs=tuple([hbm] * n),
        scratch_shapes=[pltpu.SemaphoreType.DMA((n, 7)), pltpu.SemaphoreType.DMA((n, 7)), pltpu.SemaphoreType.DMA((n,))],
    )(*xs)


def _shard_segments(d):
    shard = IN_WIDTH // N_DEV
    lo, hi = d * shard, (d + 1) * shard
    out = []
    for g in range(4):
        a, b = max(lo, GROUP_OFFS[g]), min(hi, GROUP_OFFS[g + 1])
        if a < b:
            out.append((g, a - GROUP_OFFS[g], b - GROUP_OFFS[g], a - lo))
    return out


def _assemble_w_in(gathered):
    _, rows, shard = gathered.shape
    tr = _tile(rows, 256, 16)
    widths = [GROUP_OFFS[g + 1] - GROUP_OFFS[g] for g in range(4)]

    def body(g_ref, *outs):
        for d in range(N_DEV):
            for g, lo, hi, off in _shard_segments(d):
                outs[g][:, lo:hi] = g_ref[d, :, off:off + hi - lo]

    return pl.pallas_call(
        body, name="assemble_w_in", grid=(rows // tr,),
        in_specs=[pl.BlockSpec((N_DEV, tr, shard), lambda i: (0, i, 0))],
        out_specs=tuple(pl.BlockSpec((tr, w), lambda i: (i, 0)) for w in widths),
        out_shape=tuple(SDS((rows, w), gathered.dtype) for w in widths), compiler_params=_params("parallel"),
    )(gathered)


def _split_w_in(da, db, dm, dg):
    rows = da.shape[0]
    shard = IN_WIDTH // N_DEV
    tr = _tile(rows, 256, 16)
    widths = [GROUP_OFFS[g + 1] - GROUP_OFFS[g] for g in range(4)]

    def body(a_ref, b_ref, m_ref, g_ref, o_ref):
        ins = (a_ref, b_ref, m_ref, g_ref)
        for d in range(N_DEV):
            for g, lo, hi, off in _shard_segments(d):
                o_ref[_core_major(d), :, off:off + hi - lo] = ins[g][:, lo:hi]

    return pl.pallas_call(
        body, name="split_w_in", grid=(rows // tr,),
        in_specs=[pl.BlockSpec((tr, w), lambda i: (i, 0)) for w in widths],
        out_specs=pl.BlockSpec((N_DEV, tr, shard), lambda i: (0, i, 0)),
        out_shape=SDS((N_DEV, rows, shard), da.dtype), compiler_params=_params("parallel"),
    )(da, db, dm, dg)


def _assemble_w_br(gathered):
    _, nb, rows, shard = gathered.shape

    def body(g_ref, o_ref):
        o_ref[...] = g_ref[...]

    return pl.pallas_call(
        body, name="assemble_w_br", grid=(N_DEV,),
        in_specs=[pl.BlockSpec((None, nb, rows, shard), lambda d: (d, 0, 0, 0))],
        out_specs=pl.BlockSpec((nb, rows, shard), lambda d: (0, 0, d)),
        out_shape=SDS((nb, rows, N_DEV * shard), gathered.dtype), compiler_params=_params("parallel"),
    )(gathered)


def _split_w_br(dw):
    nb, rows, cols = dw.shape
    shard = cols // N_DEV

    def body(d_ref, o_ref):
        o_ref[...] = d_ref[...].astype(BF16)

    return pl.pallas_call(
        body, name="split_w_br", grid=(N_DEV,),
        in_specs=[pl.BlockSpec((nb, rows, shard), lambda d: (0, 0, d))],
        out_specs=pl.BlockSpec((None, nb, rows, shard), lambda d: (_core_major(d), 0, 0, 0)),
        out_shape=SDS((N_DEV, nb, rows, shard), BF16), compiler_params=_params("parallel"),
    )(dw)


def _swap_with_sibling(dests):
    n = len(dests)
    hbm = pl.BlockSpec(memory_space=pl.ANY)

    def body(*refs):
        d_refs, recv_refs = refs[:n], refs[n:2 * n]
        send_sems, recv_sems = refs[2 * n:]
        x, y, c = lax.axis_index("x"), lax.axis_index("y"), lax.axis_index("c")
        copies = [pltpu.make_async_remote_copy(
            src_ref=d_refs[t].at[pl.ds((1 - c) * 4, 4)], dst_ref=recv_refs[t], send_sem=send_sems.at[t],
            recv_sem=recv_sems.at[t], device_id=(x, y, 1 - c), device_id_type=MESH_ID) for t in range(n)]
        for cp in copies:
            cp.start()
        for cp in copies:
            cp.wait()

    return pl.pallas_call(
        body, name="rs_sibling_swap", out_shape=tuple(SDS((4,) + a.shape[1:], a.dtype) for a in dests),
        in_specs=[hbm] * n, out_specs=tuple([hbm] * n),
        scratch_shapes=[pltpu.SemaphoreType.DMA((n,)), pltpu.SemaphoreType.DMA((n,))],
    )(*dests)


def _swap_with_chips(parts):
    n = len(parts)
    hbm = pl.BlockSpec(memory_space=pl.ANY)

    def body(*refs):
        p_refs, recv_refs = refs[:n], refs[n:2 * n]
        send_sems, recv_sems = refs[2 * n:]
        x, y, c = lax.axis_index("x"), lax.axis_index("y"), lax.axis_index("c")
        chips = [(1 - x, y), (x, 1 - y), (1 - x, 1 - y)]
        copies = [pltpu.make_async_remote_copy(
            src_ref=p_refs[t].at[2 * cx + cy], dst_ref=recv_refs[t].at[k], send_sem=send_sems.at[t, k],
            recv_sem=recv_sems.at[t, k], device_id=(cx, cy, c), device_id_type=MESH_ID)
            for t in range(n) for k, (cx, cy) in enumerate(chips)]
        for cp in copies:
            cp.start()
        for cp in copies:
            cp.wait()

    return pl.pallas_call(
        body, name="rs_chip_swap", out_shape=tuple(SDS((3,) + a.shape[1:], a.dtype) for a in parts),
        in_specs=[hbm] * n, out_specs=tuple([hbm] * n),
        scratch_shapes=[pltpu.SemaphoreType.DMA((n, 3)), pltpu.SemaphoreType.DMA((n, 3))],
    )(*parts)


def _pair_sum(dest, recv, core, name):
    _, na, r, cdim = dest.shape
    tr = _tile(r, 256, 16)

    def body(core_ref, a_ref, b_ref, o_ref):
        o_ref[...] = (a_ref[...].astype(F32) + b_ref[...].astype(F32)).astype(o_ref.dtype)

    blk = pl.BlockSpec((None, None, tr, cdim), lambda j, a, i, core_ref: (j, a, i, 0))
    return pl.pallas_call(
        body, name=name, out_shape=SDS(recv.shape, BF16),
        grid_spec=pltpu.PrefetchScalarGridSpec(
            num_scalar_prefetch=1, grid=(4, na, r // tr),
            in_specs=[pl.BlockSpec((None, None, tr, cdim), lambda j, a, i, core_ref: (core_ref[0] * 4 + j, a, i, 0)), blk],
            out_specs=blk),
        compiler_params=_params("parallel", "parallel", "parallel"),
    )(core, dest, recv)


def _adamw_math(w, g, m, v):
    m = ADAM_B1 * m + (1.0 - ADAM_B1) * g
    v = ADAM_B2 * v + (1.0 - ADAM_B2) * (g * g)
    m_hat = m / (1.0 - ADAM_B1 ** ADAM_STEP)
    v_hat = v / (1.0 - ADAM_B2 ** ADAM_STEP)
    delta = -ADAM_LR * (m_hat / (jnp.sqrt(v_hat) + ADAM_EPS) + ADAM_WD * w)
    return delta, m, v


def _sum_and_adamw(parts, w, m, v, name):
    n, r, ln = parts.shape
    tr = _tile(r, 512, 16)

    def body(p_ref, w_ref, m_ref, v_ref, g_out, d_out, m_out, v_out):
        g = p_ref[0].astype(F32)
        for j in range(1, n):
            g = g + p_ref[j].astype(F32)
        delta, nm, nv = _adamw_math(w_ref[...], g, m_ref[...], v_ref[...])
        g_out[...] = g
        d_out[...] = delta
        m_out[...] = nm
        v_out[...] = nv

    blk = pl.BlockSpec((tr, ln), lambda i: (i, 0))
    return pl.pallas_call(
        body, name=name, grid=(r // tr,),
        in_specs=[pl.BlockSpec((n, tr, ln), lambda i: (0, i, 0)), blk, blk, blk],
        out_specs=(blk, blk, blk, blk), out_shape=tuple(SDS((r, ln), F32) for _ in range(4)),
        compiler_params=_params("parallel"),
    )(parts, w, m, v)


BIG = ("w_in", "w_mem_kv", "w_br", "w_out")
SMALL = ("norm_g", "q_norm_g", "k_norm_g", "sg_ln_g", "sg_ln_b", "w_s", "b_s", "mem_norm_g", "final_g")


def _pack(arrs, row_unit=16):
    flat = jnp.concatenate([a.reshape(-1) for a in arrs])
    pad = (-flat.shape[0]) % (row_unit * LANES)
    if pad:
        flat = jnp.concatenate([flat, jnp.zeros((pad,), flat.dtype)])
    return flat.reshape(-1, LANES)


def _unpack(buf, shapes):
    flat = buf.reshape(-1)
    out, off = [], 0
    for shp in shapes:
        n = 1
        for s_ in shp:
            n *= s_
        out.append(flat[off:off + n].reshape(shp))
        off += n
    return out


def _shard_sum_adamw(part, from_chips, chip, w, m, v, layer, name):
    _, na, r, cdim = part.shape
    tr = _tile(r, 256, 8)

    def body(chip_ref, p_ref, f_ref, w_ref, m_ref, v_ref, g_out, d_out, m_out, v_out):
        g = p_ref[...].astype(F32)
        for j in range(3):
            g = g + f_ref[j].astype(F32)
        delta, nm, nv = _adamw_math(w_ref[...], g, m_ref[...], v_ref[...])
        g_out[...] = g
        d_out[...] = delta
        m_out[...] = nm
        v_out[...] = nv

    blk = (None, None, tr, cdim)
    lay = pl.BlockSpec(blk, lambda a, i, chip_ref: (layer, a, i, 0))
    out = pl.BlockSpec((None, tr, cdim), lambda a, i, chip_ref: (a, i, 0))
    return pl.pallas_call(
        body, name=name, out_shape=tuple(SDS((na, r, cdim), F32) for _ in range(4)),
        grid_spec=pltpu.PrefetchScalarGridSpec(
            num_scalar_prefetch=1, grid=(na, r // tr),
            in_specs=[pl.BlockSpec(blk, lambda a, i, chip_ref: (chip_ref[0], a, i, 0)),
                      pl.BlockSpec((3, None, tr, cdim), lambda a, i, chip_ref: (0, a, i, 0)), lay, lay, lay],
            out_specs=(out, out, out, out)),
        compiler_params=_params("parallel", "parallel"),
    )(chip, part, from_chips, w, m, v)


def kernel(x, mem, norm_g, w_in, q_norm_g, k_norm_g, sg_ln_g, sg_ln_b, w_s, b_s, mem_norm_g, w_mem_kv, w_br, w_out, final_g, loss_target, m_norm_g, m_w_in, m_q_norm_g, m_k_norm_g, m_sg_ln_g, m_sg_ln_b, m_w_s, m_b_s, m_mem_norm_g, m_w_mem_kv, m_w_br, m_w_out, m_final_g, v_norm_g, v_w_in, v_q_norm_g, v_k_norm_g, v_sg_ln_g, v_sg_ln_b, v_w_s, v_b_s, v_mem_norm_g, v_w_mem_kv, v_w_br, v_w_out, v_final_g):
    wts = dict(norm_g=norm_g, w_in=w_in, q_norm_g=q_norm_g, k_norm_g=k_norm_g, sg_ln_g=sg_ln_g, sg_ln_b=sg_ln_b,
               w_s=w_s, b_s=b_s, mem_norm_g=mem_norm_g, w_mem_kv=w_mem_kv, w_br=w_br, w_out=w_out, final_g=final_g)
    mom1 = dict(norm_g=m_norm_g, w_in=m_w_in, q_norm_g=m_q_norm_g, k_norm_g=m_k_norm_g, sg_ln_g=m_sg_ln_g,
                sg_ln_b=m_sg_ln_b, w_s=m_w_s, b_s=m_b_s, mem_norm_g=m_mem_norm_g, w_mem_kv=m_w_mem_kv, w_br=m_w_br,
                w_out=m_w_out, final_g=m_final_g)
    mom2 = dict(norm_g=v_norm_g, w_in=v_w_in, q_norm_g=v_q_norm_g, k_norm_g=v_k_norm_g, sg_ln_g=v_sg_ln_g,
                sg_ln_b=v_sg_ln_b, w_s=v_w_s, b_s=v_b_s, mem_norm_g=v_mem_norm_g, w_mem_kv=v_w_mem_kv, w_br=v_w_br,
                w_out=v_w_out, final_g=v_final_g)
    dp = w_in.shape[0]
    core = lax.axis_index("c").astype(jnp.int32).reshape(1)
    chip = (2 * lax.axis_index("x") + lax.axis_index("y")).astype(jnp.int32).reshape(1)

    shard_bf = {n: wts[n].astype(BF16) for n in BIG}
    layers = []
    for l in range(dp):
        g_in, g_kv, g_br, g_out = _all_gather([shard_bf[n][l] for n in BIG], "weights_all_gather")
        layers.append(_layer_weights(l, _assemble_w_in(g_in), g_kv.reshape(D_MODEL, -1), _assemble_w_br(g_br),
                                     g_out.reshape(D_MODEL, -1), wts))

    loss_local, grad_x, grads, d_final_g = _local_step(x[0], mem[0], loss_target[0], layers, final_g)
    loss = lax.psum(loss_local[0, 0], AXES)

    as4 = lambda a: a.reshape(a.shape[:1] + (1,) * (4 - a.ndim) + a.shape[1:])
    per_layer = []
    for l in reversed(range(dp)):
        dests = [grads[l][n] for n in BIG]
        recv = _swap_with_sibling(dests)
        parts = [_pair_sum(d, r, core, "rs_pair_sum_" + n) for n, d, r in zip(BIG, dests, recv)]
        from_chips = _swap_with_chips(parts)
        per_layer.append({n: _shard_sum_adamw(p, f, chip, as4(wts[n]), as4(mom1[n]), as4(mom2[n]), l, "sum_adamw_" + n)
                          for n, p, f in zip(BIG, parts, from_chips)})
    per_layer.reverse()
    big_out = [{n: jnp.stack([per_layer[l][n][k] for l in range(dp)]).reshape(wts[n].shape) for n in BIG} for k in range(4)]

    small_g = {n: jnp.stack([g[n] for g in grads]) for n in SMALL if n != "final_g"}
    small_g["final_g"] = d_final_g
    (all_small,) = _all_gather([_pack([small_g[n] for n in SMALL])], "small_all_gather")
    small_bufs = _sum_and_adamw(
        all_small, _pack([wts[n] for n in SMALL]), _pack([mom1[n] for n in SMALL]), _pack([mom2[n] for n in SMALL]),
        "small_sum_adamw")

    outs = []
    for big_vals, small_buf in zip(big_out, small_bufs):
        vals = dict(big_vals)
        vals.update(zip(SMALL, _unpack(small_buf, [wts[n].shape for n in SMALL])))
        outs.append(vals)
    order = ("norm_g", "w_in", "q_norm_g", "k_norm_g", "sg_ln_g", "sg_ln_b", "w_s", "b_s", "mem_norm_g", "w_mem_kv",
             "w_br", "w_out", "final_g")
    result = [loss, grad_x[None]]
    for vals in outs:
        result += [vals[n] for n in order]
    return tuple(result)
```

```python
import functools

import jax
import jax.numpy as jnp
from jax import lax
from jax.experimental import pallas as pl
from jax.experimental.pallas import tpu as pltpu

F32 = jnp.float32
BF16 = jnp.bfloat16
SDS = jax.ShapeDtypeStruct
MESH_ID = pl.DeviceIdType.MESH
AXES = ("x", "y", "c")
N_DEV = 8

D_MODEL = 1024
DEPTH = 4
GRID_W = 64
CHUNK = 128
ROPE_THETA = 10000.0
EPS = 1e-6
HEAD_DIM = 64
A_HEADS = 8
A_KV_HEADS = 2
B_GROUPS = 4
M_HEADS = 4
M_HEAD_DIM = 128
BW = 512
W_A = 1280
W_B = 1536
W_M = 1024
W_G = 3072
IN_WIDTH = W_A + W_B + W_M + W_G
GROUP_OFFS = (0, W_A, W_A + W_B, W_A + W_B + W_M, IN_WIDTH)

ADAM_LR = 0.001
ADAM_B1 = 0.9
ADAM_B2 = 0.999
ADAM_EPS = 1e-08
ADAM_WD = 0.01
ADAM_STEP = 10

LANES = 128
VMEM_LIMIT = 52 * 1024 * 1024


def _tile(n, cap, unit=LANES):
    if n <= cap:
        return n
    t = (cap // unit) * unit
    while n % t:
        t -= unit
    return t


def _params(*sem):
    return pltpu.CompilerParams(dimension_semantics=sem, vmem_limit_bytes=VMEM_LIMIT)


def _sigmoid(z):
    return 1.0 / (1.0 + jnp.exp(-z))


def _silu_and_grad(z):
    s = _sigmoid(z)
    return z * s, s * (1.0 + z * (1.0 - s))


def _dot(a, b, dims):
    return lax.dot_general(a, b, (dims, ((), ())), preferred_element_type=F32)


NN = ((1,), (0,))
NT = ((1,), (1,))
TN = ((0,), (0,))
HBM = pl.BlockSpec(memory_space=pl.ANY)


class _Exchange:
    def __init__(self, ins, out_shapes, sems, make, aliases=None):
        self.ins, self.out_shapes, self.sems, self.make = list(ins), list(out_shapes), list(sems), make
        self.aliases = dict(aliases or {})

    def start(self, in_refs, out_refs, sems):
        for cp in self.make(in_refs, out_refs, sems):
            cp.start()

    def finish(self, in_refs, out_refs, sems):
        for cp in self.make(in_refs, out_refs, sems):
            cp.wait()


def _host_call(body, ins, in_specs, out_specs, out_shape, *, name, grid, semantics, scratch=(), exchange=None):
    ins, out_shape, scratch = list(ins), list(out_shape), list(scratch)
    if exchange is None:
        res = pl.pallas_call(
            body, name=name, grid=grid, in_specs=list(in_specs), out_specs=tuple(out_specs), out_shape=tuple(out_shape),
            scratch_shapes=scratch, compiler_params=_params(*semantics))(*ins)
        return tuple(res), ()
    n_in, n_out, n_scr = len(ins), len(out_shape), len(scratch)
    x_in, x_out = len(exchange.ins), len(exchange.out_shapes)

    def carrying(*refs):
        o0 = n_in + x_in
        s0 = o0 + n_out + x_out
        c_in, c_out, c_sems = refs[n_in:o0], refs[o0 + n_out:s0], refs[s0 + n_scr:]
        ids = [pl.program_id(a) for a in range(len(grid))]
        first = functools.reduce(jnp.logical_and, [i == 0 for i in ids])
        last = functools.reduce(jnp.logical_and, [i == g - 1 for i, g in zip(ids, grid)])

        @pl.when(first)
        def _():
            exchange.start(c_in, c_out, c_sems)

        body(*refs[:n_in], *refs[o0:o0 + n_out], *refs[s0:s0 + n_scr])

        @pl.when(last)
        def _():
            exchange.finish(c_in, c_out, c_sems)

    res = pl.pallas_call(
        carrying, name=name, grid=grid, in_specs=list(in_specs) + [HBM] * x_in,
        out_specs=tuple(out_specs) + (HBM,) * x_out, out_shape=tuple(out_shape) + tuple(exchange.out_shapes),
        scratch_shapes=scratch + exchange.sems,
        input_output_aliases={n_in + i: n_out + o for i, o in exchange.aliases.items()},
        compiler_params=_params(*(["arbitrary"] * len(grid))))(*ins, *exchange.ins)
    return tuple(res[:n_out]), tuple(res[n_out:])


def _exchange_call(exchange, name):
    x_in = len(exchange.ins)

    def body(*refs):
        x_out = len(exchange.out_shapes)
        c_in, c_out, c_sems = refs[:x_in], refs[x_in:x_in + x_out], refs[x_in + x_out:]
        exchange.start(c_in, c_out, c_sems)
        exchange.finish(c_in, c_out, c_sems)

    return pl.pallas_call(
        body, name=name, in_specs=[HBM] * x_in, out_specs=tuple([HBM] * len(exchange.out_shapes)),
        out_shape=tuple(exchange.out_shapes), scratch_shapes=exchange.sems, input_output_aliases=exchange.aliases,
    )(*exchange.ins)


def _core_major(d):
    return (d % 2) * 4 + d // 2


def _mm(a, b, mode, name, add=None, out_dtype=F32, tm_cap=512, tn_cap=1536, tk_cap=1536):
    if mode == "nn":
        (m, k), (k2, n) = a.shape, b.shape
    elif mode == "nt":
        (m, k), (n, k2) = a.shape, b.shape
    else:
        (k, m), (k2, n) = a.shape, b.shape
    assert k == k2, (a.shape, b.shape, mode)
    tm = _tile(m, tm_cap, 8 if mode != "tn" else LANES)
    tn = _tile(n, tn_cap)
    tk = _tile(k, tk_cap, LANES if mode != "tn" else 16)
    nk = k // tk
    dims = {"nn": NN, "nt": NT, "tn": TN}[mode]
    a_spec = {"nn": pl.BlockSpec((tm, tk), lambda i, j, kk: (i, kk)),
              "nt": pl.BlockSpec((tm, tk), lambda i, j, kk: (i, kk)),
              "tn": pl.BlockSpec((tk, tm), lambda i, j, kk: (kk, i))}[mode]
    b_spec = {"nn": pl.BlockSpec((tk, tn), lambda i, j, kk: (kk, j)),
              "nt": pl.BlockSpec((tn, tk), lambda i, j, kk: (j, kk)),
              "tn": pl.BlockSpec((tk, tn), lambda i, j, kk: (kk, j))}[mode]
    o_spec = pl.BlockSpec((tm, tn), lambda i, j, kk: (i, j))
    has_add = add is not None

    def body(*refs):
        a_ref, b_ref = refs[0], refs[1]
        add_ref = refs[2] if has_add else None
        o_ref = refs[3] if has_add else refs[2]
        part = _dot(a_ref[...].astype(BF16), b_ref[...].astype(BF16), dims)
        if nk == 1:
            if has_add:
                part = part + add_ref[...]
            o_ref[...] = part.astype(out_dtype)
        else:
            acc = refs[-1]
            kk = pl.program_id(2)

            @pl.when(kk == 0)
            def _():
                acc[...] = part + add_ref[...] if has_add else part

            @pl.when(kk > 0)
            def _():
                acc[...] += part

            @pl.when(kk == nk - 1)
            def _():
                o_ref[...] = acc[...].astype(out_dtype)

    ins = [a, b] + ([add] if has_add else [])
    in_specs = [a_spec, b_spec] + ([o_spec] if has_add else [])
    return pl.pallas_call(
        body, name=name, grid=(m // tm, n // tn, nk), in_specs=in_specs, out_specs=o_spec,
        out_shape=SDS((m, n), out_dtype),
        scratch_shapes=[pltpu.VMEM((tm, tn), F32)] if nk > 1 else [],
        compiler_params=_params("parallel", "parallel", "arbitrary"),
    )(*ins)


def _mm_tn_by_owner(a, b, name):
    (k, m), (k2, n) = a.shape, b.shape
    assert k == k2 and m % N_DEV == 0, (a.shape, b.shape)
    tk = _tile(k, 512, 16)
    nk = k // tk
    rows = m // N_DEV

    def body(a_ref, b_ref, o_ref, acc):
        kk = pl.program_id(0)
        part = _dot(a_ref[...].astype(BF16), b_ref[...].astype(BF16), TN)

        @pl.when(kk == 0)
        def _():
            acc[...] = part

        @pl.when(kk > 0)
        def _():
            acc[...] += part

        @pl.when(kk == nk - 1)
        def _():
            for d in range(N_DEV):
                pos = _core_major(d)
                o_ref[pos * rows:(pos + 1) * rows, :] = acc[d * rows:(d + 1) * rows, :].astype(BF16)

    return pl.pallas_call(
        body, name=name, grid=(nk,),
        in_specs=[pl.BlockSpec((tk, m), lambda kk: (kk, 0)), pl.BlockSpec((tk, n), lambda kk: (kk, 0))],
        out_specs=pl.BlockSpec((m, n), lambda kk: (0, 0)), out_shape=SDS((m, n), BF16),
        scratch_shapes=[pltpu.VMEM((m, n), F32)], compiler_params=_params("arbitrary"),
    )(a, b)


def _rmsnorm_fwd(x, g, name):
    t, d = x.shape
    tm = _tile(t, 512, 8)

    def body(x_ref, g_ref, h_ref):
        xf = x_ref[...]
        r = lax.rsqrt(jnp.mean(xf * xf, axis=-1, keepdims=True) + EPS)
        h_ref[...] = (xf * r * g_ref[...]).astype(BF16)

    return pl.pallas_call(
        body, name=name, grid=(t // tm,),
        in_specs=[pl.BlockSpec((tm, d), lambda i: (i, 0)), pl.BlockSpec((1, d), lambda i: (0, 0))],
        out_specs=pl.BlockSpec((tm, d), lambda i: (i, 0)),
        out_shape=SDS((t, d), BF16), compiler_params=_params("parallel"),
    )(x, g)


def _rmsnorm_bwd(x, g, dh, dres, name):
    t, d = x.shape
    tm = _tile(t, 512, 8)
    has_res = dres is not None

    def body(*refs):
        x_ref, g_ref, dh_ref = refs[:3]
        dx_ref, dg_ref = refs[-2:]
        xf = x_ref[...]
        r = lax.rsqrt(jnp.mean(xf * xf, axis=-1, keepdims=True) + EPS)
        xh = xf * r
        dhf = dh_ref[...].astype(F32)
        gd = dhf * g_ref[...]
        dx = r * (gd - xh * jnp.mean(gd * xh, axis=-1, keepdims=True))
        if has_res:
            dx = dx + refs[3][...]
        dx_ref[...] = dx

        @pl.when(pl.program_id(0) == 0)
        def _():
            dg_ref[...] = jnp.zeros_like(dg_ref)

        dg_ref[...] += jnp.sum(dhf * xh, axis=0, keepdims=True)

    row = pl.BlockSpec((tm, d), lambda i: (i, 0))
    vec = pl.BlockSpec((1, d), lambda i: (0, 0))
    return pl.pallas_call(
        body, name=name, grid=(t // tm,),
        in_specs=[row, vec, row] + ([row] if has_res else []),
        out_specs=(row, vec), out_shape=(SDS((t, d), F32), SDS((1, d), F32)),
        compiler_params=_params("arbitrary"),
    )(*([x, g, dh] + ([dres] if has_res else [])))


def _rope_tables(t):
    rows = t // GRID_W
    row = jnp.repeat(jnp.arange(rows, dtype=F32), GRID_W)
    col = jnp.tile(jnp.arange(GRID_W, dtype=F32), rows)
    n_freq = HEAD_DIM // 4
    inv = ROPE_THETA ** (-jnp.arange(n_freq, dtype=F32) / n_freq)
    ang = jnp.stack([row[:, None] * inv, col[:, None] * inv], axis=1)
    cos, sin = jnp.cos(ang), jnp.sin(ang)
    c64 = jnp.concatenate([cos[:, 0], cos[:, 0], cos[:, 1], cos[:, 1]], axis=-1)
    s64 = jnp.concatenate([-sin[:, 0], sin[:, 0], -sin[:, 1], sin[:, 1]], axis=-1)
    return jnp.tile(c64, (1, 2)), jnp.tile(s64, (1, 2))


def _head_sums(v, lane):
    lo = jnp.sum(jnp.where(lane < HEAD_DIM, v, 0.0), axis=-1, keepdims=True)
    hi = jnp.sum(jnp.where(lane < HEAD_DIM, 0.0, v), axis=-1, keepdims=True)
    return jnp.where(lane < HEAD_DIM, lo, hi)


def _swap16(v, lane):
    return jnp.where((lane % 32) < 16, pltpu.roll(v, LANES - 16, 1), pltpu.roll(v, 16, 1))


def _attn_prep_fwd(pa, cs, sn, qg2, kg2):
    t = pa.shape[0]
    tq = _tile(t, 512, 8)
    scale = HEAD_DIM ** -0.5

    def body(pa_ref, cs_ref, sn_ref, qg_ref, kg_ref, q_ref, k_ref, v_ref):
        lane = lax.broadcasted_iota(jnp.int32, (tq, LANES), 1)
        c, s = cs_ref[...], sn_ref[...]

        def norm_rope(xg, g2):
            r = lax.rsqrt(_head_sums(xg * xg, lane) * (1.0 / HEAD_DIM) + EPS)
            xn = xg * r * g2
            return xn * c + _swap16(xn, lane) * s

        for gi in range(4):
            sl = slice(gi * LANES, (gi + 1) * LANES)
            q_ref[:, sl] = (norm_rope(pa_ref[:, sl], qg_ref[...]) * scale).astype(BF16)
        kr = norm_rope(pa_ref[:, 512:640], kg_ref[...]).astype(BF16)
        vv = pa_ref[:, 640:768].astype(BF16)
        for kvh in range(A_KV_HEADS):
            hs = slice(kvh * HEAD_DIM, (kvh + 1) * HEAD_DIM)
            k_ref[kvh] = kr[:, hs]
            v_ref[kvh] = vv[:, hs]

    row = lambda w: pl.BlockSpec((tq, w), lambda i: (i, 0))
    vec = pl.BlockSpec((1, LANES), lambda i: (0, 0))
    hm = pl.BlockSpec((A_KV_HEADS, tq, HEAD_DIM), lambda i: (0, i, 0))
    return pl.pallas_call(
        body, name="attn_prep_fwd", grid=(t // tq,),
        in_specs=[row(W_A), row(LANES), row(LANES), vec, vec],
        out_specs=(row(BW), hm, hm),
        out_shape=(SDS((t, BW), BF16), SDS((A_KV_HEADS, t, HEAD_DIM), BF16), SDS((A_KV_HEADS, t, HEAD_DIM), BF16)),
        compiler_params=_params("parallel"),
    )(pa, cs, sn, qg2, kg2)


def _attn_fwd(q, k, v, pa, exchange=None):
    t = q.shape[0]
    tq = _tile(t, 256, 8)
    grp = A_HEADS // A_KV_HEADS

    def body(q_ref, k_ref, v_ref, pa_ref, o_ref, lse_ref, y_ref):
        for h in range(A_HEADS):
            kvh, j = h // grp, h % grp
            hs = slice(h * HEAD_DIM, (h + 1) * HEAD_DIM)
            s = _dot(q_ref[:, hs], k_ref[kvh], NT)
            m = jnp.max(s, axis=-1, keepdims=True)
            p = jnp.exp(s - m)
            l = jnp.sum(p, axis=-1, keepdims=True)
            o = _dot(p.astype(BF16), v_ref[kvh], NN) / l
            o_ref[:, hs] = o
            lse_ref[kvh, :, j:j + 1] = m + jnp.log(l)
            z = pa_ref[:, 768 + h * HEAD_DIM:768 + (h + 1) * HEAD_DIM]
            y_ref[:, hs] = (o * (z * _sigmoid(z))).astype(BF16)

    row = lambda w: pl.BlockSpec((tq, w), lambda i: (i, 0))
    full = pl.BlockSpec((A_KV_HEADS, t, HEAD_DIM), lambda i: (0, 0, 0))
    return _host_call(
        body, (q, k, v, pa), [row(BW), full, full, row(W_A)],
        (row(BW), pl.BlockSpec((A_KV_HEADS, tq, grp), lambda i: (0, i, 0)), row(BW)),
        (SDS((t, BW), F32), SDS((A_KV_HEADS, t, grp), F32), SDS((t, BW), BF16)),
        name="attn_fwd", grid=(t // tq,), semantics=("parallel",), exchange=exchange)


def _attn_bwd(q, k, v, pa, o, lse, dy, exchange=None):
    t = q.shape[0]
    tq = _tile(t, 256, 8)
    grp = A_HEADS // A_KV_HEADS
    gw = grp * HEAD_DIM

    def body(q_ref, k_ref, v_ref, z_ref, o_ref, lse_ref, dy_ref, dq_ref, dk_ref, dv_ref):
        @pl.when(pl.program_id(1) == 0)
        def _():
            dk_ref[...] = jnp.zeros_like(dk_ref)
            dv_ref[...] = jnp.zeros_like(dv_ref)

        kk, vv = k_ref[...], v_ref[...]
        for j in range(grp):
            hs = slice(j * HEAD_DIM, (j + 1) * HEAD_DIM)
            qj = q_ref[:, hs]
            z = z_ref[:, hs]
            oj = o_ref[:, hs]
            do = dy_ref[:, hs] * (z * _sigmoid(z))
            delta = jnp.sum(do * oj, axis=-1, keepdims=True)
            do_b = do.astype(BF16)
            s = _dot(qj, kk, NT)
            p = jnp.exp(s - lse_ref[:, j:j + 1])
            dp = _dot(do_b, vv, NT)
            ds_b = (p * (dp - delta)).astype(BF16)
            dq_ref[:, hs] = _dot(ds_b, kk, NN)
            dk_ref[...] += _dot(ds_b, qj, TN)
            dv_ref[...] += _dot(p.astype(BF16), do_b, TN)

    grp_blk = pl.BlockSpec((tq, gw), lambda g, i: (i, g))
    kv_blk = pl.BlockSpec((None, t, HEAD_DIM), lambda g, i: (g, 0, 0))
    return _host_call(
        body, (q, k, v, pa, o, lse, dy),
        [grp_blk, kv_blk, kv_blk, pl.BlockSpec((tq, gw), lambda g, i: (i, 768 // gw + g)),
         grp_blk, pl.BlockSpec((None, tq, grp), lambda g, i: (g, i, 0)), grp_blk],
        (grp_blk, kv_blk, kv_blk),
        (SDS((t, BW), F32), SDS((A_KV_HEADS, t, HEAD_DIM), F32), SDS((A_KV_HEADS, t, HEAD_DIM), F32)),
        name="attn_bwd", grid=(A_KV_HEADS, t // tq), semantics=("arbitrary", "arbitrary"), exchange=exchange)


def _attn_prep_bwd(pa, cs, sn, qg2, kg2, dq, dk, dv, dy, o):
    t = pa.shape[0]
    tq = _tile(t, 512, 8)
    scale = HEAD_DIM ** -0.5

    def body(pa_ref, cs_ref, sn_ref, qg_ref, kg_ref, dq_ref, dk_ref, dv_ref, dy_ref, o_ref, dpa_ref, dqg_ref, dkg_ref):
        lane = lax.broadcasted_iota(jnp.int32, (tq, LANES), 1)
        c, s = cs_ref[...], sn_ref[...]

        @pl.when(pl.program_id(0) == 0)
        def _():
            dqg_ref[...] = jnp.zeros_like(dqg_ref)
            dkg_ref[...] = jnp.zeros_like(dkg_ref)

        def norm_rope_bwd(xg, g2, dout):
            r = lax.rsqrt(_head_sums(xg * xg, lane) * (1.0 / HEAD_DIM) + EPS)
            xh = xg * r
            dxn = dout * c + _swap16(dout * s, lane)
            gd = dxn * g2
            dx = r * (gd - xh * (_head_sums(gd * xh, lane) * (1.0 / HEAD_DIM)))
            return dx, jnp.sum(dxn * xh, axis=0, keepdims=True)

        for gi in range(4):
            sl = slice(gi * LANES, (gi + 1) * LANES)
            dx, dg = norm_rope_bwd(pa_ref[:, sl], qg_ref[...], dq_ref[:, sl] * scale)
            dpa_ref[:, sl] = dx.astype(BF16)
            dqg_ref[...] += dg
        dx, dg = norm_rope_bwd(pa_ref[:, 512:640], kg_ref[...], dk_ref[...])
        dpa_ref[:, 512:640] = dx.astype(BF16)
        dkg_ref[...] += dg
        dpa_ref[:, 640:768] = dv_ref[...].astype(BF16)
        z = pa_ref[:, 768:1280]
        _, dsilu = _silu_and_grad(z)
        dpa_ref[:, 768:1280] = (dy_ref[...] * o_ref[...] * dsilu).astype(BF16)

    row = lambda w: pl.BlockSpec((tq, w), lambda i: (i, 0))
    vec = pl.BlockSpec((1, LANES), lambda i: (0, 0))
    return pl.pallas_call(
        body, name="attn_prep_bwd", grid=(t // tq,),
        in_specs=[row(W_A), row(LANES), row(LANES), vec, vec, row(BW), row(LANES), row(LANES), row(BW), row(BW)],
        out_specs=(row(W_A), vec, vec),
        out_shape=(SDS((t, W_A), BF16), SDS((1, LANES), F32), SDS((1, LANES), F32)),
        compiler_params=_params("arbitrary"),
    )(pa, cs, sn, qg2, kg2, dq, dk, dv, dy, o)


def _layer_norm(v, g, b):
    mu = jnp.mean(v, axis=-1, keepdims=True)
    xc = v - mu
    rs = lax.rsqrt(jnp.mean(xc * xc, axis=-1, keepdims=True) + EPS)
    xh = xc * rs
    return xh * g + b, xh, rs


def _gmlp_fwd(pb, lng, lnb, ws, bs):
    t = pb.shape[0]
    tb = _tile(t, 512, CHUNK)

    def body(pb_ref, g_ref, b_ref, ws_ref, bs_ref, y_ref):
        vln, _, _ = _layer_norm(pb_ref[:, BW:2 * BW], g_ref[...], b_ref[...])
        vb = vln.astype(BF16)
        for gi in range(B_GROUPS):
            w = ws_ref[gi].astype(BF16)
            cs_ = slice(gi * CHUNK, (gi + 1) * CHUNK)
            for n in range(tb // CHUNK):
                rs_ = slice(n * CHUNK, (n + 1) * CHUNK)
                mixed = _dot(w, vb[rs_, cs_], NN) + bs_ref[gi]
                z = pb_ref[rs_, 2 * BW + gi * CHUNK:2 * BW + (gi + 1) * CHUNK]
                y_ref[rs_, cs_] = (pb_ref[rs_, cs_] * mixed * (z * _sigmoid(z))).astype(BF16)

    return pl.pallas_call(
        body, name="gmlp_fwd", grid=(t // tb,),
        in_specs=[pl.BlockSpec((tb, W_B), lambda i: (i, 0)),
                  pl.BlockSpec((1, BW), lambda i: (0, 0)), pl.BlockSpec((1, BW), lambda i: (0, 0)),
                  pl.BlockSpec((B_GROUPS, CHUNK, CHUNK), lambda i: (0, 0, 0)),
                  pl.BlockSpec((B_GROUPS, CHUNK, 1), lambda i: (0, 0, 0))],
        out_specs=pl.BlockSpec((tb, BW), lambda i: (i, 0)),
        out_shape=SDS((t, BW), BF16), compiler_params=_params("parallel"),
    )(pb, lng, lnb, ws, bs)


def _gmlp_bwd(pb, lng, lnb, ws, bs, dy):
    t = pb.shape[0]
    tb = _tile(t, 256, CHUNK)

    def body(pb_ref, g_ref, b_ref, ws_ref, bs_ref, dy_ref, dpb_ref, dws_ref, dbs_ref, dg_ref, db_ref, dvln_ref):
        @pl.when(pl.program_id(0) == 0)
        def _():
            dws_ref[...] = jnp.zeros_like(dws_ref)
            dbs_ref[...] = jnp.zeros_like(dbs_ref)
            dg_ref[...] = jnp.zeros_like(dg_ref)
            db_ref[...] = jnp.zeros_like(db_ref)

        vln, xh, rs = _layer_norm(pb_ref[:, BW:2 * BW], g_ref[...], b_ref[...])
        vb = vln.astype(BF16)
        for gi in range(B_GROUPS):
            w = ws_ref[gi].astype(BF16)
            cs_ = slice(gi * CHUNK, (gi + 1) * CHUNK)
            for n in range(tb // CHUNK):
                rs_ = slice(n * CHUNK, (n + 1) * CHUNK)
                vbc = vb[rs_, cs_]
                mixed = _dot(w, vbc, NN) + bs_ref[gi]
                zs = slice(2 * BW + gi * CHUNK, 2 * BW + (gi + 1) * CHUNK)
                z = pb_ref[rs_, zs]
                u = pb_ref[rs_, cs_]
                sil, dsil = _silu_and_grad(z)
                dyc = dy_ref[rs_, cs_]
                dmixed = dyc * u * sil
                dpb_ref[rs_, cs_] = (dyc * mixed * sil).astype(BF16)
                dpb_ref[rs_, zs] = (dyc * u * mixed * dsil).astype(BF16)
                dmb = dmixed.astype(BF16)
                dws_ref[gi] += _dot(dmb, vbc, NT)
                dbs_ref[gi] += jnp.sum(dmixed, axis=-1, keepdims=True)
                dvln_ref[rs_, cs_] = _dot(w, dmb, TN)
        dvln = dvln_ref[...]
        dg_ref[...] += jnp.sum(dvln * xh, axis=0, keepdims=True)
        db_ref[...] += jnp.sum(dvln, axis=0, keepdims=True)
        gd = dvln * g_ref[...]
        dv = rs * (gd - jnp.mean(gd, axis=-1, keepdims=True) - xh * jnp.mean(gd * xh, axis=-1, keepdims=True))
        dpb_ref[:, BW:2 * BW] = dv.astype(BF16)

    vec = pl.BlockSpec((1, BW), lambda i: (0, 0))
    wsb = pl.BlockSpec((B_GROUPS, CHUNK, CHUNK), lambda i: (0, 0, 0))
    bsb = pl.BlockSpec((B_GROUPS, CHUNK, 1), lambda i: (0, 0, 0))
    return pl.pallas_call(
        body, name="gmlp_bwd", grid=(t // tb,),
        in_specs=[pl.BlockSpec((tb, W_B), lambda i: (i, 0)), vec, vec, wsb, bsb, pl.BlockSpec((tb, BW), lambda i: (i, 0))],
        out_specs=(pl.BlockSpec((tb, W_B), lambda i: (i, 0)), wsb, bsb, vec, vec),
        out_shape=(SDS((t, W_B), BF16), SDS((B_GROUPS, CHUNK, CHUNK), F32), SDS((B_GROUPS, CHUNK, 1), F32),
                   SDS((1, BW), F32), SDS((1, BW), F32)),
        scratch_shapes=[pltpu.VMEM((tb, BW), F32)],
        compiler_params=_params("arbitrary"),
    )(pb, lng, lnb, ws, bs, dy)


def _mem_softmax(qh, kh):
    s = _dot(qh, kh, NT) * (M_HEAD_DIM ** -0.5)
    e = jnp.exp(s - jnp.max(s, axis=-1, keepdims=True))
    return e / jnp.sum(e, axis=-1, keepdims=True)


def _mem_attn_fwd(pm, kv):
    t = pm.shape[0]
    tq = _tile(t, 512, 8)
    ml = kv.shape[0]

    def body(pm_ref, kv_ref, y_ref):
        for h in range(M_HEADS):
            hs = slice(h * M_HEAD_DIM, (h + 1) * M_HEAD_DIM)
            kh = kv_ref[:, hs].astype(BF16)
            vh = kv_ref[:, BW + h * M_HEAD_DIM:BW + (h + 1) * M_HEAD_DIM].astype(BF16)
            p = _mem_softmax(pm_ref[:, hs].astype(BF16), kh)
            o = _dot(p.astype(BF16), vh, NN)
            z = pm_ref[:, BW + h * M_HEAD_DIM:BW + (h + 1) * M_HEAD_DIM]
            y_ref[:, hs] = (o * (z * _sigmoid(z))).astype(BF16)

    return pl.pallas_call(
        body, name="mem_attn_fwd", grid=(t // tq,),
        in_specs=[pl.BlockSpec((tq, W_M), lambda i: (i, 0)), pl.BlockSpec((ml, 2 * BW), lambda i: (0, 0))],
        out_specs=pl.BlockSpec((tq, BW), lambda i: (i, 0)),
        out_shape=SDS((t, BW), BF16), compiler_params=_params("parallel"),
    )(pm, kv)


def _mem_attn_bwd(pm, kv, dy):
    t = pm.shape[0]
    tq = _tile(t, 512, 8)
    ml = kv.shape[0]
    scale = M_HEAD_DIM ** -0.5

    def body(pm_ref, kv_ref, dy_ref, dpm_ref, dkv_ref):
        @pl.when(pl.program_id(0) == 0)
        def _():
            dkv_ref[...] = jnp.zeros_like(dkv_ref)

        for h in range(M_HEADS):
            hs = slice(h * M_HEAD_DIM, (h + 1) * M_HEAD_DIM)
            zs = slice(BW + h * M_HEAD_DIM, BW + (h + 1) * M_HEAD_DIM)
            kh = kv_ref[:, hs].astype(BF16)
            vh = kv_ref[:, zs].astype(BF16)
            qh = pm_ref[:, hs].astype(BF16)
            p = _mem_softmax(qh, kh)
            pb = p.astype(BF16)
            o = _dot(pb, vh, NN)
            sil, dsil = _silu_and_grad(pm_ref[:, zs])
            dyh = dy_ref[:, hs]
            do = dyh * sil
            dpm_ref[:, zs] = (dyh * o * dsil).astype(BF16)
            delta = jnp.sum(do * o, axis=-1, keepdims=True)
            do_b = do.astype(BF16)
            dp = _dot(do_b, vh, NT)
            dr_b = (p * (dp - delta) * scale).astype(BF16)
            dpm_ref[:, hs] = _dot(dr_b, kh, NN).astype(BF16)
            dkv_ref[:, hs] += _dot(dr_b, qh, TN)
            dkv_ref[:, zs] += _dot(pb, do_b, TN)

    kvb = pl.BlockSpec((ml, 2 * BW), lambda i: (0, 0))
    return pl.pallas_call(
        body, name="mem_attn_bwd", grid=(t // tq,),
        in_specs=[pl.BlockSpec((tq, W_M), lambda i: (i, 0)), kvb, pl.BlockSpec((tq, BW), lambda i: (i, 0))],
        out_specs=(pl.BlockSpec((tq, W_M), lambda i: (i, 0)), kvb),
        out_shape=(SDS((t, W_M), BF16), SDS((ml, 2 * BW), F32)),
        compiler_params=_params("arbitrary"),
    )(pm, kv, dy)


def _merge_fwd(ya, yb, ym, pg, wbr, exchange=None):
    t = ya.shape[0]
    tm = _tile(t, 512, 8)

    def body(ya_ref, yb_ref, ym_ref, pg_ref, w_ref, m_ref):
        acc = None
        for n, y_ref in enumerate((ya_ref, yb_ref, ym_ref)):
            up = _dot(y_ref[...], w_ref[n], NN)
            term = _sigmoid(pg_ref[:, n * D_MODEL:(n + 1) * D_MODEL]) * up
            acc = term if acc is None else acc + term
        m_ref[...] = acc.astype(BF16)

    yb_spec = pl.BlockSpec((tm, BW), lambda i: (i, 0))
    return _host_call(
        body, (ya, yb, ym, pg, wbr),
        [yb_spec, yb_spec, yb_spec, pl.BlockSpec((tm, W_G), lambda i: (i, 0)),
         pl.BlockSpec((3, BW, D_MODEL), lambda i: (0, 0, 0))],
        (pl.BlockSpec((tm, D_MODEL), lambda i: (i, 0)),), (SDS((t, D_MODEL), BF16),),
        name="merge_fwd", grid=(t // tm,), semantics=("parallel",), exchange=exchange)


def _merge_bwd(ya, yb, ym, pg, wbr, dm, exchange=None):
    t = ya.shape[0]
    tm = _tile(t, 256, 8)

    def body(ya_ref, yb_ref, ym_ref, pg_ref, w_ref, dm_ref, dya_ref, dyb_ref, dym_ref, dpg_ref, dw_ref):
        @pl.when(pl.program_id(0) == 0)
        def _():
            dw_ref[...] = jnp.zeros_like(dw_ref)

        dmf = dm_ref[...]
        for n, (y_ref, dy_ref) in enumerate(((ya_ref, dya_ref), (yb_ref, dyb_ref), (ym_ref, dym_ref))):
            cs_ = slice(n * D_MODEL, (n + 1) * D_MODEL)
            y = y_ref[...]
            w = w_ref[n]
            up = _dot(y, w, NN)
            gt = _sigmoid(pg_ref[:, cs_])
            dpg_ref[:, cs_] = (dmf * up * gt * (1.0 - gt)).astype(BF16)
            dup = (dmf * gt).astype(BF16)
            dy_ref[...] = _dot(dup, w, NT)
            dw_ref[n] += _dot(y, dup, TN)

    y_spec = pl.BlockSpec((tm, BW), lambda i: (i, 0))
    w_spec = pl.BlockSpec((3, BW, D_MODEL), lambda i: (0, 0, 0))
    return _host_call(
        body, (ya, yb, ym, pg, wbr, dm),
        [y_spec, y_spec, y_spec, pl.BlockSpec((tm, W_G), lambda i: (i, 0)), w_spec,
         pl.BlockSpec((tm, D_MODEL), lambda i: (i, 0))],
        (y_spec, y_spec, y_spec, pl.BlockSpec((tm, W_G), lambda i: (i, 0)), w_spec),
        (SDS((t, BW), F32), SDS((t, BW), F32), SDS((t, BW), F32), SDS((t, W_G), BF16), SDS((3, BW, D_MODEL), F32)),
        name="merge_bwd", grid=(t // tm,), semantics=("arbitrary",), exchange=exchange)


def _loss_head(x, g, target):
    t, d = x.shape
    tm = _tile(t, 512, 8)

    def body(x_ref, g_ref, t_ref, loss_ref, dx_ref, dg_ref):
        @pl.when(pl.program_id(0) == 0)
        def _():
            loss_ref[...] = jnp.zeros_like(loss_ref)
            dg_ref[...] = jnp.zeros_like(dg_ref)

        xf = x_ref[...]
        r = lax.rsqrt(jnp.mean(xf * xf, axis=-1, keepdims=True) + EPS)
        xh = xf * r
        err = xh * g_ref[...] - t_ref[...]
        per_tok = jnp.mean(err * err, axis=-1, keepdims=True)
        loss_ref[...] += 0.5 * jnp.sum(per_tok, axis=0, keepdims=True)
        dy = err * (1.0 / d)
        gd = dy * g_ref[...]
        dx_ref[...] = r * (gd - xh * jnp.mean(gd * xh, axis=-1, keepdims=True))
        dg_ref[...] += jnp.sum(dy * xh, axis=0, keepdims=True)

    row = pl.BlockSpec((tm, d), lambda i: (i, 0))
    vec = pl.BlockSpec((1, d), lambda i: (0, 0))
    return pl.pallas_call(
        body, name="loss_head", grid=(t // tm,),
        in_specs=[row, vec, row],
        out_specs=(pl.BlockSpec((1, 1), lambda i: (0, 0)), row, vec),
        out_shape=(SDS((1, 1), F32), SDS((t, d), F32), SDS((1, d), F32)),
        compiler_params=_params("arbitrary"),
    )(x, g, target)


def _layer_fwd(x, mem, w, tabs, next_shards=None):
    cs, sn = tabs
    riding = next_shards is not None
    h = _rmsnorm_fwd(x, w["norm_g"], "rmsnorm_fwd")
    pa = _mm(h, w["w_a"], "nn", "proj_a")
    pb = _mm(h, w["w_b"], "nn", "proj_b")
    pm = _mm(h, w["w_m"], "nn", "proj_m")
    pg = _mm(h, w["w_g"], "nn", "proj_g")
    q, k, v = _attn_prep_fwd(pa, cs, sn, w["qg2"], w["kg2"])
    (o, lse, ya), gathered = _attn_fwd(q, k, v, pa, exchange=_gather_first_hop(next_shards) if riding else None)
    yb = _gmlp_fwd(pb, w["ln_g"], w["ln_b"], w["w_s"], w["b_s"])
    memn = _rmsnorm_fwd(mem, w["mem_g"], "mem_rmsnorm_fwd")
    kv = _mm(memn, w["w_kv"], "nn", "mem_kv")
    ym = _mem_attn_fwd(pm, kv)
    (merged,), gathered = _merge_fwd(ya, yb, ym, pg, w["w_br"], exchange=_gather_forward(gathered) if riding else None)
    x_next = _mm(merged, w["w_out"], "nn", "out_proj", add=x)
    saved = dict(x=x, h=h, pa=pa, pb=pb, pm=pm, pg=pg, q=q, k=k, v=v, o=o, lse=lse, ya=ya, yb=yb, ym=ym,
                 memn=memn, kv=kv, merged=merged)
    return x_next, saved, gathered


def _layer_bwd(dx_out, mem, w, s, tabs, pending=None, core=None):
    cs, sn = tabs
    t = dx_out.shape[0]
    riding = pending is not None
    dmerged = _mm(dx_out, w["w_out"], "nt", "d_merged")
    shard_rows = D_MODEL // N_DEV
    d_w_out = _mm_tn_by_owner(s["merged"], dx_out, "d_w_out")
    (dya, dyb, dym, dpg, d_w_br), recv = _merge_bwd(s["ya"], s["yb"], s["ym"], s["pg"], w["w_br"], dmerged,
                                                    exchange=_scatter_to_sibling(pending) if riding else None)
    parts = _pair_sums(pending, recv, core) if riding else None
    (dq, dk, dv), from_chips = _attn_bwd(s["q"], s["k"], s["v"], s["pa"], s["o"], s["lse"], dya,
                                         exchange=_scatter_to_chips(parts) if riding else None)
    dk_tm = dk.transpose(1, 0, 2).reshape(t, A_KV_HEADS * HEAD_DIM)
    dv_tm = dv.transpose(1, 0, 2).reshape(t, A_KV_HEADS * HEAD_DIM)
    dpa, d_qg2, d_kg2 = _attn_prep_bwd(s["pa"], cs, sn, w["qg2"], w["kg2"], dq, dk_tm, dv_tm, dya, s["o"])
    dpb, d_w_s, d_b_s, d_ln_g, d_ln_b = _gmlp_bwd(s["pb"], w["ln_g"], w["ln_b"], w["w_s"], w["b_s"], dyb)
    dpm, dkv = _mem_attn_bwd(s["pm"], s["kv"], dym)
    d_w_kv = _mm_tn_by_owner(s["memn"], dkv, "d_w_kv")
    dmemn = _mm(dkv, w["w_kv"], "nt", "d_memn")
    _, d_mem_g = _rmsnorm_bwd(mem, w["mem_g"], dmemn, None, "mem_rmsnorm_bwd")
    dh = _mm(dpa, w["w_a"], "nt", "dh_a")
    dh = _mm(dpb, w["w_b"], "nt", "dh_b", add=dh)
    dh = _mm(dpm, w["w_m"], "nt", "dh_m", add=dh)
    dh = _mm(dpg, w["w_g"], "nt", "dh_g", add=dh)
    d_w_in = _split_w_in(*[_mm(s["h"], dp, "tn", "d_w_" + nm, out_dtype=BF16, tm_cap=1024)
                           for nm, dp in (("a", dpa), ("b", dpb), ("m", dpm), ("g", dpg))])
    dx_in, d_norm_g = _rmsnorm_bwd(s["x"], w["norm_g"], dh, dx_out, "rmsnorm_bwd")
    grads = dict(w_in=d_w_in[:, None], w_mem_kv=d_w_kv.reshape(N_DEV, 1, shard_rows, -1), w_br=_split_w_br(d_w_br),
                 w_out=d_w_out.reshape(N_DEV, 1, shard_rows, -1),
                 norm_g=d_norm_g[0], q_norm_g=d_qg2[0, :HEAD_DIM] + d_qg2[0, HEAD_DIM:],
                 k_norm_g=d_kg2[0, :HEAD_DIM] + d_kg2[0, HEAD_DIM:], sg_ln_g=d_ln_g[0], sg_ln_b=d_ln_b[0],
                 w_s=d_w_s, b_s=d_b_s[:, :, 0], mem_norm_g=d_mem_g[0])
    return dx_in, grads, ((parts, from_chips) if riding else None)


def _layer_weights(l, w_groups, w_kv, w_br, w_out, small):
    tile2 = lambda g: jnp.tile(g.reshape(1, -1), (1, 2))
    w_a, w_b, w_m, w_g = w_groups
    return dict(
        w_a=w_a, w_b=w_b, w_m=w_m, w_g=w_g, w_kv=w_kv, w_br=w_br, w_out=w_out,
        norm_g=small["norm_g"][l].reshape(1, -1), qg2=tile2(small["q_norm_g"][l]), kg2=tile2(small["k_norm_g"][l]),
        ln_g=small["sg_ln_g"][l].reshape(1, -1), ln_b=small["sg_ln_b"][l].reshape(1, -1),
        w_s=small["w_s"][l], b_s=small["b_s"][l][:, :, None], mem_g=small["mem_norm_g"][l].reshape(1, -1))


def _position():
    x, y, c = lax.axis_index("x"), lax.axis_index("y"), lax.axis_index("c")
    return x, y, c, [(1 - x, y), (x, 1 - y), (1 - x, 1 - y)]


def _gather_first_hop(shards):
    n = len(shards)

    def make(x_refs, out_refs, sems):
        send_sems, recv_sems, local_sems = sems
        x, y, c, chips = _position()
        me = 4 * x + 2 * y + c
        peers = [(x, y, 1 - c)] + [(cx, cy, c) for cx, cy in chips]
        copies = [pltpu.make_async_copy(x_refs[t], out_refs[t].at[me], local_sems.at[t]) for t in range(n)]
        copies += [pltpu.make_async_remote_copy(
            src_ref=x_refs[t], dst_ref=out_refs[t].at[me], send_sem=send_sems.at[t, k], recv_sem=recv_sems.at[t, k],
            device_id=peer, device_id_type=MESH_ID) for t in range(n) for k, peer in enumerate(peers)]
        return copies

    return _Exchange(shards, [SDS((N_DEV,) + a.shape, a.dtype) for a in shards],
                     [pltpu.SemaphoreType.DMA((n, 4)), pltpu.SemaphoreType.DMA((n, 4)), pltpu.SemaphoreType.DMA((n,))], make)


def _gather_forward(gathered):
    n = len(gathered)

    def make(in_refs, out_refs, sems):
        send_sems, recv_sems = sems
        x, y, c, chips = _position()
        return [pltpu.make_async_remote_copy(
            src_ref=in_refs[t].at[4 * cx + 2 * cy + c], dst_ref=out_refs[t].at[4 * cx + 2 * cy + c],
            send_sem=send_sems.at[t, j], recv_sem=recv_sems.at[t, j], device_id=(x, y, 1 - c), device_id_type=MESH_ID)
            for t in range(n) for j, (cx, cy) in enumerate(chips)]

    return _Exchange(gathered, [SDS(a.shape, a.dtype) for a in gathered],
                     [pltpu.SemaphoreType.DMA((n, 3)), pltpu.SemaphoreType.DMA((n, 3))], make,
                     aliases={t: t for t in range(n)})


def _all_gather(shards, name):
    return _exchange_call(_gather_forward(_exchange_call(_gather_first_hop(shards), name + "_hop1")), name + "_hop2")


def _shard_segments(d):
    shard = IN_WIDTH // N_DEV
    lo, hi = d * shard, (d + 1) * shard
    out = []
    for g in range(4):
        a, b = max(lo, GROUP_OFFS[g]), min(hi, GROUP_OFFS[g + 1])
        if a < b:
            out.append((g, a - GROUP_OFFS[g], b - GROUP_OFFS[g], a - lo))
    return out


def _assemble_w_in(gathered):
    _, rows, shard = gathered.shape
    tr = _tile(rows, 256, 16)
    widths = [GROUP_OFFS[g + 1] - GROUP_OFFS[g] for g in range(4)]

    def body(g_ref, *outs):
        for d in range(N_DEV):
            for g, lo, hi, off in _shard_segments(d):
                outs[g][:, lo:hi] = g_ref[d, :, off:off + hi - lo]

    return pl.pallas_call(
        body, name="assemble_w_in", grid=(rows // tr,),
        in_specs=[pl.BlockSpec((N_DEV, tr, shard), lambda i: (0, i, 0))],
        out_specs=tuple(pl.BlockSpec((tr, w), lambda i: (i, 0)) for w in widths),
        out_shape=tuple(SDS((rows, w), gathered.dtype) for w in widths), compiler_params=_params("parallel"),
    )(gathered)


def _split_w_in(da, db, dm, dg):
    rows = da.shape[0]
    shard = IN_WIDTH // N_DEV
    tr = _tile(rows, 256, 16)
    widths = [GROUP_OFFS[g + 1] - GROUP_OFFS[g] for g in range(4)]

    def body(a_ref, b_ref, m_ref, g_ref, o_ref):
        ins = (a_ref, b_ref, m_ref, g_ref)
        for d in range(N_DEV):
            for g, lo, hi, off in _shard_segments(d):
                o_ref[_core_major(d), :, off:off + hi - lo] = ins[g][:, lo:hi]

    return pl.pallas_call(
        body, name="split_w_in", grid=(rows // tr,),
        in_specs=[pl.BlockSpec((tr, w), lambda i: (i, 0)) for w in widths],
        out_specs=pl.BlockSpec((N_DEV, tr, shard), lambda i: (0, i, 0)),
        out_shape=SDS((N_DEV, rows, shard), da.dtype), compiler_params=_params("parallel"),
    )(da, db, dm, dg)


def _assemble_w_br(gathered):
    _, nb, rows, shard = gathered.shape

    def body(g_ref, o_ref):
        o_ref[...] = g_ref[...]

    return pl.pallas_call(
        body, name="assemble_w_br", grid=(N_DEV,),
        in_specs=[pl.BlockSpec((None, nb, rows, shard), lambda d: (d, 0, 0, 0))],
        out_specs=pl.BlockSpec((nb, rows, shard), lambda d: (0, 0, d)),
        out_shape=SDS((nb, rows, N_DEV * shard), gathered.dtype), compiler_params=_params("parallel"),
    )(gathered)


def _split_w_br(dw):
    nb, rows, cols = dw.shape
    shard = cols // N_DEV

    def body(d_ref, o_ref):
        o_ref[...] = d_ref[...].astype(BF16)

    return pl.pallas_call(
        body, name="split_w_br", grid=(N_DEV,),
        in_specs=[pl.BlockSpec((nb, rows, shard), lambda d: (0, 0, d))],
        out_specs=pl.BlockSpec((None, nb, rows, shard), lambda d: (_core_major(d), 0, 0, 0)),
        out_shape=SDS((N_DEV, nb, rows, shard), BF16), compiler_params=_params("parallel"),
    )(dw)


def _scatter_to_sibling(dests):
    n = len(dests)

    def make(d_refs, recv_refs, sems):
        send_sems, recv_sems = sems
        x, y, c, _ = _position()
        return [pltpu.make_async_remote_copy(
            src_ref=d_refs[t].at[pl.ds((1 - c) * 4, 4)], dst_ref=recv_refs[t], send_sem=send_sems.at[t],
            recv_sem=recv_sems.at[t], device_id=(x, y, 1 - c), device_id_type=MESH_ID) for t in range(n)]

    return _Exchange(dests, [SDS((4,) + a.shape[1:], a.dtype) for a in dests],
                     [pltpu.SemaphoreType.DMA((n,)), pltpu.SemaphoreType.DMA((n,))], make)


def _scatter_to_chips(parts):
    n = len(parts)

    def make(p_refs, recv_refs, sems):
        send_sems, recv_sems = sems
        _, _, c, chips = _position()
        return [pltpu.make_async_remote_copy(
            src_ref=p_refs[t].at[2 * cx + cy], dst_ref=recv_refs[t].at[k], send_sem=send_sems.at[t, k],
            recv_sem=recv_sems.at[t, k], device_id=(cx, cy, c), device_id_type=MESH_ID)
            for t in range(n) for k, (cx, cy) in enumerate(chips)]

    return _Exchange(parts, [SDS((3,) + a.shape[1:], a.dtype) for a in parts],
                     [pltpu.SemaphoreType.DMA((n, 3)), pltpu.SemaphoreType.DMA((n, 3))], make)


def _pair_sums(dests, recv, core):
    return [_pair_sum(d, r, core, "rs_pair_sum_" + n) for n, d, r in zip(BIG, dests, recv)]


def _pair_sum(dest, recv, core, name):
    _, na, r, cdim = dest.shape
    tr = _tile(r, 256, 16)

    def body(core_ref, a_ref, b_ref, o_ref):
        o_ref[...] = (a_ref[...].astype(F32) + b_ref[...].astype(F32)).astype(o_ref.dtype)

    blk = pl.BlockSpec((None, None, tr, cdim), lambda j, a, i, core_ref: (j, a, i, 0))
    return pl.pallas_call(
        body, name=name, out_shape=SDS(recv.shape, BF16),
        grid_spec=pltpu.PrefetchScalarGridSpec(
            num_scalar_prefetch=1, grid=(4, na, r // tr),
            in_specs=[pl.BlockSpec((None, None, tr, cdim), lambda j, a, i, core_ref: (core_ref[0] * 4 + j, a, i, 0)), blk],
            out_specs=blk),
        compiler_params=_params("parallel", "parallel", "parallel"),
    )(core, dest, recv)


def _adamw_math(w, g, m, v):
    m = ADAM_B1 * m + (1.0 - ADAM_B1) * g
    v = ADAM_B2 * v + (1.0 - ADAM_B2) * (g * g)
    m_hat = m / (1.0 - ADAM_B1 ** ADAM_STEP)
    v_hat = v / (1.0 - ADAM_B2 ** ADAM_STEP)
    delta = -ADAM_LR * (m_hat / (jnp.sqrt(v_hat) + ADAM_EPS) + ADAM_WD * w)
    return delta, m, v


def _sum_and_adamw(parts, w, m, v, name):
    n, r, ln = parts.shape
    tr = _tile(r, 512, 16)

    def body(p_ref, w_ref, m_ref, v_ref, g_out, d_out, m_out, v_out):
        g = p_ref[0].astype(F32)
        for j in range(1, n):
            g = g + p_ref[j].astype(F32)
        delta, nm, nv = _adamw_math(w_ref[...], g, m_ref[...], v_ref[...])
        g_out[...] = g
        d_out[...] = delta
        m_out[...] = nm
        v_out[...] = nv

    blk = pl.BlockSpec((tr, ln), lambda i: (i, 0))
    return pl.pallas_call(
        body, name=name, grid=(r // tr,),
        in_specs=[pl.BlockSpec((n, tr, ln), lambda i: (0, i, 0)), blk, blk, blk],
        out_specs=(blk, blk, blk, blk), out_shape=tuple(SDS((r, ln), F32) for _ in range(4)),
        compiler_params=_params("parallel"),
    )(parts, w, m, v)


BIG = ("w_in", "w_mem_kv", "w_br", "w_out")
SMALL = ("norm_g", "q_norm_g", "k_norm_g", "sg_ln_g", "sg_ln_b", "w_s", "b_s", "mem_norm_g", "final_g")


def _pack(arrs, row_unit=16):
    flat = jnp.concatenate([a.reshape(-1) for a in arrs])
    pad = (-flat.shape[0]) % (row_unit * LANES)
    if pad:
        flat = jnp.concatenate([flat, jnp.zeros((pad,), flat.dtype)])
    return flat.reshape(-1, LANES)


def _unpack(buf, shapes):
    flat = buf.reshape(-1)
    out, off = [], 0
    for shp in shapes:
        n = 1
        for s_ in shp:
            n *= s_
        out.append(flat[off:off + n].reshape(shp))
        off += n
    return out


def _shard_sum_adamw(part, from_chips, chip, w, m, v, layer, name):
    _, na, r, cdim = part.shape
    tr = _tile(r, 256, 8)

    def body(chip_ref, p_ref, f_ref, w_ref, m_ref, v_ref, g_out, d_out, m_out, v_out):
        g = p_ref[...].astype(F32)
        for j in range(3):
            g = g + f_ref[j].astype(F32)
        delta, nm, nv = _adamw_math(w_ref[...], g, m_ref[...], v_ref[...])
        g_out[...] = g
        d_out[...] = delta
        m_out[...] = nm
        v_out[...] = nv

    blk = (None, None, tr, cdim)
    lay = pl.BlockSpec(blk, lambda a, i, chip_ref: (layer, a, i, 0))
    out = pl.BlockSpec((None, tr, cdim), lambda a, i, chip_ref: (a, i, 0))
    return pl.pallas_call(
        body, name=name, out_shape=tuple(SDS((na, r, cdim), F32) for _ in range(4)),
        grid_spec=pltpu.PrefetchScalarGridSpec(
            num_scalar_prefetch=1, grid=(na, r // tr),
            in_specs=[pl.BlockSpec(blk, lambda a, i, chip_ref: (chip_ref[0], a, i, 0)),
                      pl.BlockSpec((3, None, tr, cdim), lambda a, i, chip_ref: (0, a, i, 0)), lay, lay, lay],
            out_specs=(out, out, out, out)),
        compiler_params=_params("parallel", "parallel"),
    )(chip, part, from_chips, w, m, v)


def kernel(x, mem, norm_g, w_in, q_norm_g, k_norm_g, sg_ln_g, sg_ln_b, w_s, b_s, mem_norm_g, w_mem_kv, w_br, w_out, final_g, loss_target, m_norm_g, m_w_in, m_q_norm_g, m_k_norm_g, m_sg_ln_g, m_sg_ln_b, m_w_s, m_b_s, m_mem_norm_g, m_w_mem_kv, m_w_br, m_w_out, m_final_g, v_norm_g, v_w_in, v_q_norm_g, v_k_norm_g, v_sg_ln_g, v_sg_ln_b, v_w_s, v_b_s, v_mem_norm_g, v_w_mem_kv, v_w_br, v_w_out, v_final_g):
    wts = dict(norm_g=norm_g, w_in=w_in, q_norm_g=q_norm_g, k_norm_g=k_norm_g, sg_ln_g=sg_ln_g, sg_ln_b=sg_ln_b,
               w_s=w_s, b_s=b_s, mem_norm_g=mem_norm_g, w_mem_kv=w_mem_kv, w_br=w_br, w_out=w_out, final_g=final_g)
    mom1 = dict(norm_g=m_norm_g, w_in=m_w_in, q_norm_g=m_q_norm_g, k_norm_g=m_k_norm_g, sg_ln_g=m_sg_ln_g,
                sg_ln_b=m_sg_ln_b, w_s=m_w_s, b_s=m_b_s, mem_norm_g=m_mem_norm_g, w_mem_kv=m_w_mem_kv, w_br=m_w_br,
                w_out=m_w_out, final_g=m_final_g)
    mom2 = dict(norm_g=v_norm_g, w_in=v_w_in, q_norm_g=v_q_norm_g, k_norm_g=v_k_norm_g, sg_ln_g=v_sg_ln_g,
                sg_ln_b=v_sg_ln_b, w_s=v_w_s, b_s=v_b_s, mem_norm_g=v_mem_norm_g, w_mem_kv=v_w_mem_kv, w_br=v_w_br,
                w_out=v_w_out, final_g=v_final_g)
    dp = w_in.shape[0]
    core = lax.axis_index("c").astype(jnp.int32).reshape(1)
    chip = (2 * lax.axis_index("x") + lax.axis_index("y")).astype(jnp.int32).reshape(1)

    shard_bf = {n: wts[n].astype(BF16) for n in BIG}
    shards = lambda l: [shard_bf[n][l] for n in BIG]
    x_l, mem_l = x[0], mem[0]
    tabs = _rope_tables(x_l.shape[0])

    gathered = _all_gather(shards(0), "weights_all_gather")
    layers, saved = [], []
    for l in range(dp):
        g_in, g_kv, g_br, g_out = gathered
        layers.append(_layer_weights(l, _assemble_w_in(g_in), g_kv.reshape(D_MODEL, -1), _assemble_w_br(g_br),
                                     g_out.reshape(D_MODEL, -1), wts))
        x_l, s, gathered = _layer_fwd(x_l, mem_l, layers[l], tabs, next_shards=shards(l + 1) if l + 1 < dp else None)
        saved.append(s)
    loss_local, dx, d_final_g = _loss_head(x_l, final_g.reshape(1, -1), loss_target[0])
    loss = lax.psum(loss_local[0, 0], AXES)

    as4 = lambda a: a.reshape(a.shape[:1] + (1,) * (4 - a.ndim) + a.shape[1:])

    def finish(l, parts, from_chips):
        return {n: _shard_sum_adamw(p, f, chip, as4(wts[n]), as4(mom1[n]), as4(mom2[n]), l, "sum_adamw_" + n)
                for n, p, f in zip(BIG, parts, from_chips)}

    grads, per_layer, pending = [None] * dp, [None] * dp, None
    for l in reversed(range(dp)):
        dx, grads[l], scattered = _layer_bwd(dx, mem_l, layers[l], saved[l], tabs, pending=pending, core=core)
        if scattered is not None:
            per_layer[l + 1] = finish(l + 1, *scattered)
        pending = [grads[l][n] for n in BIG]
    recv = _exchange_call(_scatter_to_sibling(pending), "rs_sibling_swap")
    parts = _pair_sums(pending, recv, core)
    per_layer[0] = finish(0, parts, _exchange_call(_scatter_to_chips(parts), "rs_chip_swap"))
    grad_x = dx
    big_out = [{n: jnp.stack([per_layer[l][n][k] for l in range(dp)]).reshape(wts[n].shape) for n in BIG} for k in range(4)]

    small_g = {n: jnp.stack([g[n] for g in grads]) for n in SMALL if n != "final_g"}
    small_g["final_g"] = d_final_g
    (all_small,) = _all_gather([_pack([small_g[n] for n in SMALL])], "small_all_gather")
    small_bufs = _sum_and_adamw(
        all_small, _pack([wts[n] for n in SMALL]), _pack([mom1[n] for n in SMALL]), _pack([mom2[n] for n in SMALL]),
        "small_sum_adamw")

    outs = []
    for big_vals, small_buf in zip(big_out, small_bufs):
        vals = dict(big_vals)
        vals.update(zip(SMALL, _unpack(small_buf, [wts[n].shape for n in SMALL])))
        outs.append(vals)
    order = ("norm_g", "w_in", "q_norm_g", "k_norm_g", "sg_ln_g", "sg_ln_b", "w_s", "b_s", "mem_norm_g", "w_mem_kv",
             "w_br", "w_out", "final_g")
    result = [loss, grad_x[None]]
    for vals in outs:
        result += [vals[n] for n in order]
    return tuple(result)
```

```python
import functools

import jax
import jax.numpy as jnp
from jax import lax
from jax.experimental import pallas as pl
from jax.experimental.pallas import tpu as pltpu

F32 = jnp.float32
BF16 = jnp.bfloat16
SDS = jax.ShapeDtypeStruct
MESH_ID = pl.DeviceIdType.MESH
AXES = ("x", "y", "c")
N_DEV = 8

D_MODEL = 1024
DEPTH = 4
GRID_W = 64
CHUNK = 128
ROPE_THETA = 10000.0
EPS = 1e-6
HEAD_DIM = 64
A_HEADS = 8
A_KV_HEADS = 2
B_GROUPS = 4
M_HEADS = 4
M_HEAD_DIM = 128
BW = 512
W_A = 1280
W_B = 1536
W_M = 1024
W_G = 3072
IN_WIDTH = W_A + W_B + W_M + W_G
GROUP_OFFS = (0, W_A, W_A + W_B, W_A + W_B + W_M, IN_WIDTH)

ADAM_LR = 0.001
ADAM_B1 = 0.9
ADAM_B2 = 0.999
ADAM_EPS = 1e-08
ADAM_WD = 0.01
ADAM_STEP = 10

LANES = 128
VMEM_LIMIT = 52 * 1024 * 1024


def _tile(n, cap, unit=LANES):
    if n <= cap:
        return n
    t = (cap // unit) * unit
    while n % t:
        t -= unit
    return t


def _params(*sem):
    return pltpu.CompilerParams(dimension_semantics=sem, vmem_limit_bytes=VMEM_LIMIT)


def _sigmoid(z):
    return 1.0 / (1.0 + jnp.exp(-z))


def _silu_and_grad(z):
    s = _sigmoid(z)
    return z * s, s * (1.0 + z * (1.0 - s))


def _dot(a, b, dims):
    return lax.dot_general(a, b, (dims, ((), ())), preferred_element_type=F32)


NN = ((1,), (0,))
NT = ((1,), (1,))
TN = ((0,), (0,))
HBM = pl.BlockSpec(memory_space=pl.ANY)


class _Exchange:
    def __init__(self, ins, out_shapes, sems, make, aliases=None):
        self.ins, self.out_shapes, self.sems, self.make = list(ins), list(out_shapes), list(sems), make
        self.aliases = dict(aliases or {})

    def start(self, in_refs, out_refs, sems):
        for cp in self.make(in_refs, out_refs, sems):
            cp.start()

    def finish(self, in_refs, out_refs, sems):
        for cp in self.make(in_refs, out_refs, sems):
            cp.wait()


def _host_call(body, ins, in_specs, out_specs, out_shape, *, name, grid, semantics, scratch=(), exchange=None):
    ins, out_shape, scratch = list(ins), list(out_shape), list(scratch)
    if exchange is None:
        res = pl.pallas_call(
            body, name=name, grid=grid, in_specs=list(in_specs), out_specs=tuple(out_specs), out_shape=tuple(out_shape),
            scratch_shapes=scratch, compiler_params=_params(*semantics))(*ins)
        return tuple(res), ()
    n_in, n_out, n_scr = len(ins), len(out_shape), len(scratch)
    x_in, x_out = len(exchange.ins), len(exchange.out_shapes)

    def carrying(*refs):
        o0 = n_in + x_in
        s0 = o0 + n_out + x_out
        c_in, c_out, c_sems = refs[n_in:o0], refs[o0 + n_out:s0], refs[s0 + n_scr:]
        ids = [pl.program_id(a) for a in range(len(grid))]
        first = functools.reduce(jnp.logical_and, [i == 0 for i in ids])
        last = functools.reduce(jnp.logical_and, [i == g - 1 for i, g in zip(ids, grid)])

        @pl.when(first)
        def _():
            exchange.start(c_in, c_out, c_sems)

        body(*refs[:n_in], *refs[o0:o0 + n_out], *refs[s0:s0 + n_scr])

        @pl.when(last)
        def _():
            exchange.finish(c_in, c_out, c_sems)

    res = pl.pallas_call(
        carrying, name=name, grid=grid, in_specs=list(in_specs) + [HBM] * x_in,
        out_specs=tuple(out_specs) + (HBM,) * x_out, out_shape=tuple(out_shape) + tuple(exchange.out_shapes),
        scratch_shapes=scratch + exchange.sems,
        input_output_aliases={n_in + i: n_out + o for i, o in exchange.aliases.items()},
        compiler_params=_params(*(["arbitrary"] * len(grid))))(*ins, *exchange.ins)
    return tuple(res[:n_out]), tuple(res[n_out:])


def _exchange_call(exchange, name):
    x_in = len(exchange.ins)

    def body(*refs):
        x_out = len(exchange.out_shapes)
        c_in, c_out, c_sems = refs[:x_in], refs[x_in:x_in + x_out], refs[x_in + x_out:]
        exchange.start(c_in, c_out, c_sems)
        exchange.finish(c_in, c_out, c_sems)

    return pl.pallas_call(
        body, name=name, in_specs=[HBM] * x_in, out_specs=tuple([HBM] * len(exchange.out_shapes)),
        out_shape=tuple(exchange.out_shapes), scratch_shapes=exchange.sems, input_output_aliases=exchange.aliases,
    )(*exchange.ins)


def _core_major(d):
    return (d % 2) * 4 + d // 2


def _mm(a, b, mode, name, add=None, out_dtype=F32, tm_cap=512, tn_cap=1536, tk_cap=1536):
    if mode == "nn":
        (m, k), (k2, n) = a.shape, b.shape
    elif mode == "nt":
        (m, k), (n, k2) = a.shape, b.shape
    else:
        (k, m), (k2, n) = a.shape, b.shape
    assert k == k2, (a.shape, b.shape, mode)
    tm = _tile(m, tm_cap, 8 if mode != "tn" else LANES)
    tn = _tile(n, tn_cap)
    tk = _tile(k, tk_cap, LANES if mode != "tn" else 16)
    nk = k // tk
    dims = {"nn": NN, "nt": NT, "tn": TN}[mode]
    a_spec = {"nn": pl.BlockSpec((tm, tk), lambda i, j, kk: (i, kk)),
              "nt": pl.BlockSpec((tm, tk), lambda i, j, kk: (i, kk)),
              "tn": pl.BlockSpec((tk, tm), lambda i, j, kk: (kk, i))}[mode]
    b_spec = {"nn": pl.BlockSpec((tk, tn), lambda i, j, kk: (kk, j)),
              "nt": pl.BlockSpec((tn, tk), lambda i, j, kk: (j, kk)),
              "tn": pl.BlockSpec((tk, tn), lambda i, j, kk: (kk, j))}[mode]
    o_spec = pl.BlockSpec((tm, tn), lambda i, j, kk: (i, j))
    has_add = add is not None

    def body(*refs):
        a_ref, b_ref = refs[0], refs[1]
        add_ref = refs[2] if has_add else None
        o_ref = refs[3] if has_add else refs[2]
        part = _dot(a_ref[...].astype(BF16), b_ref[...].astype(BF16), dims)
        if nk == 1:
            if has_add:
                part = part + add_ref[...]
            o_ref[...] = part.astype(out_dtype)
        else:
            acc = refs[-1]
            kk = pl.program_id(2)

            @pl.when(kk == 0)
            def _():
                acc[...] = part + add_ref[...] if has_add else part

            @pl.when(kk > 0)
            def _():
                acc[...] += part

            @pl.when(kk == nk - 1)
            def _():
                o_ref[...] = acc[...].astype(out_dtype)

    ins = [a, b] + ([add] if has_add else [])
    in_specs = [a_spec, b_spec] + ([o_spec] if has_add else [])
    return pl.pallas_call(
        body, name=name, grid=(m // tm, n // tn, nk), in_specs=in_specs, out_specs=o_spec,
        out_shape=SDS((m, n), out_dtype),
        scratch_shapes=[pltpu.VMEM((tm, tn), F32)] if nk > 1 else [],
        compiler_params=_params("parallel", "parallel", "arbitrary"),
    )(*ins)


def _mm_tn_by_owner(a, b, name):
    (k, m), (k2, n) = a.shape, b.shape
    assert k == k2 and m % N_DEV == 0, (a.shape, b.shape)
    tk = _tile(k, 512, 16)
    nk = k // tk
    rows = m // N_DEV

    def body(a_ref, b_ref, o_ref, acc):
        kk = pl.program_id(0)
        part = _dot(a_ref[...].astype(BF16), b_ref[...].astype(BF16), TN)

        @pl.when(kk == 0)
        def _():
            acc[...] = part

        @pl.when(kk > 0)
        def _():
            acc[...] += part

        @pl.when(kk == nk - 1)
        def _():
            for d in range(N_DEV):
                pos = _core_major(d)
                o_ref[pos * rows:(pos + 1) * rows, :] = acc[d * rows:(d + 1) * rows, :].astype(BF16)

    return pl.pallas_call(
        body, name=name, grid=(nk,),
        in_specs=[pl.BlockSpec((tk, m), lambda kk: (kk, 0)), pl.BlockSpec((tk, n), lambda kk: (kk, 0))],
        out_specs=pl.BlockSpec((m, n), lambda kk: (0, 0)), out_shape=SDS((m, n), BF16),
        scratch_shapes=[pltpu.VMEM((m, n), F32)], compiler_params=_params("arbitrary"),
    )(a, b)


def _rmsnorm_fwd(x, g, name):
    t, d = x.shape
    tm = _tile(t, 512, 8)

    def body(x_ref, g_ref, h_ref):
        xf = x_ref[...]
        r = lax.rsqrt(jnp.mean(xf * xf, axis=-1, keepdims=True) + EPS)
        h_ref[...] = (xf * r * g_ref[...]).astype(BF16)

    return pl.pallas_call(
        body, name=name, grid=(t // tm,),
        in_specs=[pl.BlockSpec((tm, d), lambda i: (i, 0)), pl.BlockSpec((1, d), lambda i: (0, 0))],
        out_specs=pl.BlockSpec((tm, d), lambda i: (i, 0)),
        out_shape=SDS((t, d), BF16), compiler_params=_params("parallel"),
    )(x, g)


def _rmsnorm_bwd(x, g, dh, dres, name):
    t, d = x.shape
    tm = _tile(t, 512, 8)
    has_res = dres is not None

    def body(*refs):
        x_ref, g_ref, dh_ref = refs[:3]
        dx_ref, dg_ref = refs[-2:]
        xf = x_ref[...]
        r = lax.rsqrt(jnp.mean(xf * xf, axis=-1, keepdims=True) + EPS)
        xh = xf * r
        dhf = dh_ref[...].astype(F32)
        gd = dhf * g_ref[...]
        dx = r * (gd - xh * jnp.mean(gd * xh, axis=-1, keepdims=True))
        if has_res:
            dx = dx + refs[3][...]
        dx_ref[...] = dx

        @pl.when(pl.program_id(0) == 0)
        def _():
            dg_ref[...] = jnp.zeros_like(dg_ref)

        dg_ref[...] += jnp.sum(dhf * xh, axis=0, keepdims=True)

    row = pl.BlockSpec((tm, d), lambda i: (i, 0))
    vec = pl.BlockSpec((1, d), lambda i: (0, 0))
    return pl.pallas_call(
        body, name=name, grid=(t // tm,),
        in_specs=[row, vec, row] + ([row] if has_res else []),
        out_specs=(row, vec), out_shape=(SDS((t, d), F32), SDS((1, d), F32)),
        compiler_params=_params("arbitrary"),
    )(*([x, g, dh] + ([dres] if has_res else [])))


def _rope_tables(t):
    rows = t // GRID_W
    row = jnp.repeat(jnp.arange(rows, dtype=F32), GRID_W)
    col = jnp.tile(jnp.arange(GRID_W, dtype=F32), rows)
    n_freq = HEAD_DIM // 4
    inv = ROPE_THETA ** (-jnp.arange(n_freq, dtype=F32) / n_freq)
    ang = jnp.stack([row[:, None] * inv, col[:, None] * inv], axis=1)
    cos, sin = jnp.cos(ang), jnp.sin(ang)
    c64 = jnp.concatenate([cos[:, 0], cos[:, 0], cos[:, 1], cos[:, 1]], axis=-1)
    s64 = jnp.concatenate([-sin[:, 0], sin[:, 0], -sin[:, 1], sin[:, 1]], axis=-1)
    return jnp.tile(c64, (1, 2)), jnp.tile(s64, (1, 2))


def _head_sums(v, lane):
    lo = jnp.sum(jnp.where(lane < HEAD_DIM, v, 0.0), axis=-1, keepdims=True)
    hi = jnp.sum(jnp.where(lane < HEAD_DIM, 0.0, v), axis=-1, keepdims=True)
    return jnp.where(lane < HEAD_DIM, lo, hi)


def _swap16(v, lane):
    return jnp.where((lane % 32) < 16, pltpu.roll(v, LANES - 16, 1), pltpu.roll(v, 16, 1))


def _attn_prep_fwd(pa, cs, sn, qg2, kg2):
    t = pa.shape[0]
    tq = _tile(t, 512, 8)
    scale = HEAD_DIM ** -0.5

    def body(pa_ref, cs_ref, sn_ref, qg_ref, kg_ref, q_ref, k_ref, v_ref):
        lane = lax.broadcasted_iota(jnp.int32, (tq, LANES), 1)
        c, s = cs_ref[...], sn_ref[...]

        def norm_rope(xg, g2):
            r = lax.rsqrt(_head_sums(xg * xg, lane) * (1.0 / HEAD_DIM) + EPS)
            xn = xg * r * g2
            return xn * c + _swap16(xn, lane) * s

        for gi in range(4):
            sl = slice(gi * LANES, (gi + 1) * LANES)
            q_ref[:, sl] = (norm_rope(pa_ref[:, sl].astype(F32), qg_ref[...]) * scale).astype(BF16)
        kr = norm_rope(pa_ref[:, 512:640].astype(F32), kg_ref[...]).astype(BF16)
        vv = pa_ref[:, 640:768].astype(BF16)
        for kvh in range(A_KV_HEADS):
            hs = slice(kvh * HEAD_DIM, (kvh + 1) * HEAD_DIM)
            k_ref[kvh] = kr[:, hs]
            v_ref[kvh] = vv[:, hs]

    row = lambda w: pl.BlockSpec((tq, w), lambda i: (i, 0))
    vec = pl.BlockSpec((1, LANES), lambda i: (0, 0))
    hm = pl.BlockSpec((A_KV_HEADS, tq, HEAD_DIM), lambda i: (0, i, 0))
    return pl.pallas_call(
        body, name="attn_prep_fwd", grid=(t // tq,),
        in_specs=[row(W_A), row(LANES), row(LANES), vec, vec],
        out_specs=(row(BW), hm, hm),
        out_shape=(SDS((t, BW), BF16), SDS((A_KV_HEADS, t, HEAD_DIM), BF16), SDS((A_KV_HEADS, t, HEAD_DIM), BF16)),
        compiler_params=_params("parallel"),
    )(pa, cs, sn, qg2, kg2)


def _attn_fwd(q, k, v, pa, exchange=None):
    t = q.shape[0]
    tq = _tile(t, 256, 8)
    grp = A_HEADS // A_KV_HEADS

    def body(q_ref, k_ref, v_ref, pa_ref, o_ref, lse_ref, y_ref):
        for h in range(A_HEADS):
            kvh, j = h // grp, h % grp
            hs = slice(h * HEAD_DIM, (h + 1) * HEAD_DIM)
            s = _dot(q_ref[:, hs], k_ref[kvh], NT)
            m = jnp.max(s, axis=-1, keepdims=True)
            p = jnp.exp(s - m)
            l = jnp.sum(p, axis=-1, keepdims=True)
            o = _dot(p.astype(BF16), v_ref[kvh], NN) / l
            o_ref[:, hs] = o
            lse_ref[kvh, :, j:j + 1] = m + jnp.log(l)
            z = pa_ref[:, 768 + h * HEAD_DIM:768 + (h + 1) * HEAD_DIM].astype(F32)
            y_ref[:, hs] = (o * (z * _sigmoid(z))).astype(BF16)

    row = lambda w: pl.BlockSpec((tq, w), lambda i: (i, 0))
    full = pl.BlockSpec((A_KV_HEADS, t, HEAD_DIM), lambda i: (0, 0, 0))
    return _host_call(
        body, (q, k, v, pa), [row(BW), full, full, row(W_A)],
        (row(BW), pl.BlockSpec((A_KV_HEADS, tq, grp), lambda i: (0, i, 0)), row(BW)),
        (SDS((t, BW), F32), SDS((A_KV_HEADS, t, grp), F32), SDS((t, BW), BF16)),
        name="attn_fwd", grid=(t // tq,), semantics=("parallel",), exchange=exchange)


def _attn_bwd(q, k, v, pa, o, lse, dy, exchange=None):
    t = q.shape[0]
    tq = _tile(t, 256, 8)
    grp = A_HEADS // A_KV_HEADS
    gw = grp * HEAD_DIM

    def body(q_ref, k_ref, v_ref, z_ref, o_ref, lse_ref, dy_ref, dq_ref, dk_ref, dv_ref):
        @pl.when(pl.program_id(1) == 0)
        def _():
            dk_ref[...] = jnp.zeros_like(dk_ref)
            dv_ref[...] = jnp.zeros_like(dv_ref)

        kk, vv = k_ref[...], v_ref[...]
        for j in range(grp):
            hs = slice(j * HEAD_DIM, (j + 1) * HEAD_DIM)
            qj = q_ref[:, hs]
            z = z_ref[:, hs].astype(F32)
            oj = o_ref[:, hs]
            do = dy_ref[:, hs] * (z * _sigmoid(z))
            delta = jnp.sum(do * oj, axis=-1, keepdims=True)
            do_b = do.astype(BF16)
            s = _dot(qj, kk, NT)
            p = jnp.exp(s - lse_ref[:, j:j + 1])
            dp = _dot(do_b, vv, NT)
            ds_b = (p * (dp - delta)).astype(BF16)
            dq_ref[:, hs] = _dot(ds_b, kk, NN)
            dk_ref[...] += _dot(ds_b, qj, TN)
            dv_ref[...] += _dot(p.astype(BF16), do_b, TN)

    grp_blk = pl.BlockSpec((tq, gw), lambda g, i: (i, g))
    kv_blk = pl.BlockSpec((None, t, HEAD_DIM), lambda g, i: (g, 0, 0))
    return _host_call(
        body, (q, k, v, pa, o, lse, dy),
        [grp_blk, kv_blk, kv_blk, pl.BlockSpec((tq, gw), lambda g, i: (i, 768 // gw + g)),
         grp_blk, pl.BlockSpec((None, tq, grp), lambda g, i: (g, i, 0)), grp_blk],
        (grp_blk, kv_blk, kv_blk),
        (SDS((t, BW), F32), SDS((A_KV_HEADS, t, HEAD_DIM), F32), SDS((A_KV_HEADS, t, HEAD_DIM), F32)),
        name="attn_bwd", grid=(A_KV_HEADS, t // tq), semantics=("arbitrary", "arbitrary"), exchange=exchange)


def _attn_prep_bwd(pa, cs, sn, qg2, kg2, dq, dk, dv, dy, o):
    t = pa.shape[0]
    tq = _tile(t, 512, 8)
    scale = HEAD_DIM ** -0.5

    def body(pa_ref, cs_ref, sn_ref, qg_ref, kg_ref, dq_ref, dk_ref, dv_ref, dy_ref, o_ref, dpa_ref, dqg_ref, dkg_ref):
        lane = lax.broadcasted_iota(jnp.int32, (tq, LANES), 1)
        c, s = cs_ref[...], sn_ref[...]

        @pl.when(pl.program_id(0) == 0)
        def _():
            dqg_ref[...] = jnp.zeros_like(dqg_ref)
            dkg_ref[...] = jnp.zeros_like(dkg_ref)

        def norm_rope_bwd(xg, g2, dout):
            r = lax.rsqrt(_head_sums(xg * xg, lane) * (1.0 / HEAD_DIM) + EPS)
            xh = xg * r
            dxn = dout * c + _swap16(dout * s, lane)
            gd = dxn * g2
            dx = r * (gd - xh * (_head_sums(gd * xh, lane) * (1.0 / HEAD_DIM)))
            return dx, jnp.sum(dxn * xh, axis=0, keepdims=True)

        for gi in range(4):
            sl = slice(gi * LANES, (gi + 1) * LANES)
            dx, dg = norm_rope_bwd(pa_ref[:, sl].astype(F32), qg_ref[...], dq_ref[:, sl] * scale)
            dpa_ref[:, sl] = dx.astype(BF16)
            dqg_ref[...] += dg
        dx, dg = norm_rope_bwd(pa_ref[:, 512:640].astype(F32), kg_ref[...], dk_ref[...])
        dpa_ref[:, 512:640] = dx.astype(BF16)
        dkg_ref[...] += dg
        dpa_ref[:, 640:768] = dv_ref[...].astype(BF16)
        z = pa_ref[:, 768:1280].astype(F32)
        _, dsilu = _silu_and_grad(z)
        dpa_ref[:, 768:1280] = (dy_ref[...] * o_ref[...] * dsilu).astype(BF16)

    row = lambda w: pl.BlockSpec((tq, w), lambda i: (i, 0))
    vec = pl.BlockSpec((1, LANES), lambda i: (0, 0))
    return pl.pallas_call(
        body, name="attn_prep_bwd", grid=(t // tq,),
        in_specs=[row(W_A), row(LANES), row(LANES), vec, vec, row(BW), row(LANES), row(LANES), row(BW), row(BW)],
        out_specs=(row(W_A), vec, vec),
        out_shape=(SDS((t, W_A), BF16), SDS((1, LANES), F32), SDS((1, LANES), F32)),
        compiler_params=_params("arbitrary"),
    )(pa, cs, sn, qg2, kg2, dq, dk, dv, dy, o)


def _layer_norm(v, g, b):
    mu = jnp.mean(v, axis=-1, keepdims=True)
    xc = v - mu
    rs = lax.rsqrt(jnp.mean(xc * xc, axis=-1, keepdims=True) + EPS)
    xh = xc * rs
    return xh * g + b, xh, rs


def _gmlp_fwd(pb, lng, lnb, ws, bs):
    t = pb.shape[0]
    tb = _tile(t, 512, CHUNK)

    def body(pb_ref, g_ref, b_ref, ws_ref, bs_ref, y_ref):
        vln, _, _ = _layer_norm(pb_ref[:, BW:2 * BW].astype(F32), g_ref[...], b_ref[...])
        vb = vln.astype(BF16)
        for gi in range(B_GROUPS):
            w = ws_ref[gi].astype(BF16)
            cs_ = slice(gi * CHUNK, (gi + 1) * CHUNK)
            for n in range(tb // CHUNK):
                rs_ = slice(n * CHUNK, (n + 1) * CHUNK)
                mixed = _dot(w, vb[rs_, cs_], NN) + bs_ref[gi]
                z = pb_ref[rs_, 2 * BW + gi * CHUNK:2 * BW + (gi + 1) * CHUNK].astype(F32)
                y_ref[rs_, cs_] = (pb_ref[rs_, cs_].astype(F32) * mixed * (z * _sigmoid(z))).astype(BF16)

    return pl.pallas_call(
        body, name="gmlp_fwd", grid=(t // tb,),
        in_specs=[pl.BlockSpec((tb, W_B), lambda i: (i, 0)),
                  pl.BlockSpec((1, BW), lambda i: (0, 0)), pl.BlockSpec((1, BW), lambda i: (0, 0)),
                  pl.BlockSpec((B_GROUPS, CHUNK, CHUNK), lambda i: (0, 0, 0)),
                  pl.BlockSpec((B_GROUPS, CHUNK, 1), lambda i: (0, 0, 0))],
        out_specs=pl.BlockSpec((tb, BW), lambda i: (i, 0)),
        out_shape=SDS((t, BW), BF16), compiler_params=_params("parallel"),
    )(pb, lng, lnb, ws, bs)


def _gmlp_bwd(pb, lng, lnb, ws, bs, dy):
    t = pb.shape[0]
    tb = _tile(t, 256, CHUNK)

    def body(pb_ref, g_ref, b_ref, ws_ref, bs_ref, dy_ref, dpb_ref, dws_ref, dbs_ref, dg_ref, db_ref, dvln_ref):
        @pl.when(pl.program_id(0) == 0)
        def _():
            dws_ref[...] = jnp.zeros_like(dws_ref)
            dbs_ref[...] = jnp.zeros_like(dbs_ref)
            dg_ref[...] = jnp.zeros_like(dg_ref)
            db_ref[...] = jnp.zeros_like(db_ref)

        vln, xh, rs = _layer_norm(pb_ref[:, BW:2 * BW].astype(F32), g_ref[...], b_ref[...])
        vb = vln.astype(BF16)
        for gi in range(B_GROUPS):
            w = ws_ref[gi].astype(BF16)
            cs_ = slice(gi * CHUNK, (gi + 1) * CHUNK)
            for n in range(tb // CHUNK):
                rs_ = slice(n * CHUNK, (n + 1) * CHUNK)
                vbc = vb[rs_, cs_]
                mixed = _dot(w, vbc, NN) + bs_ref[gi]
                zs = slice(2 * BW + gi * CHUNK, 2 * BW + (gi + 1) * CHUNK)
                z = pb_ref[rs_, zs].astype(F32)
                u = pb_ref[rs_, cs_].astype(F32)
                sil, dsil = _silu_and_grad(z)
                dyc = dy_ref[rs_, cs_]
                dmixed = dyc * u * sil
                dpb_ref[rs_, cs_] = (dyc * mixed * sil).astype(BF16)
                dpb_ref[rs_, zs] = (dyc * u * mixed * dsil).astype(BF16)
                dmb = dmixed.astype(BF16)
                dws_ref[gi] += _dot(dmb, vbc, NT)
                dbs_ref[gi] += jnp.sum(dmixed, axis=-1, keepdims=True)
                dvln_ref[rs_, cs_] = _dot(w, dmb, TN)
        dvln = dvln_ref[...]
        dg_ref[...] += jnp.sum(dvln * xh, axis=0, keepdims=True)
        db_ref[...] += jnp.sum(dvln, axis=0, keepdims=True)
        gd = dvln * g_ref[...]
        dv = rs * (gd - jnp.mean(gd, axis=-1, keepdims=True) - xh * jnp.mean(gd * xh, axis=-1, keepdims=True))
        dpb_ref[:, BW:2 * BW] = dv.astype(BF16)

    vec = pl.BlockSpec((1, BW), lambda i: (0, 0))
    wsb = pl.BlockSpec((B_GROUPS, CHUNK, CHUNK), lambda i: (0, 0, 0))
    bsb = pl.BlockSpec((B_GROUPS, CHUNK, 1), lambda i: (0, 0, 0))
    return pl.pallas_call(
        body, name="gmlp_bwd", grid=(t // tb,),
        in_specs=[pl.BlockSpec((tb, W_B), lambda i: (i, 0)), vec, vec, wsb, bsb, pl.BlockSpec((tb, BW), lambda i: (i, 0))],
        out_specs=(pl.BlockSpec((tb, W_B), lambda i: (i, 0)), wsb, bsb, vec, vec),
        out_shape=(SDS((t, W_B), BF16), SDS((B_GROUPS, CHUNK, CHUNK), F32), SDS((B_GROUPS, CHUNK, 1), F32),
                   SDS((1, BW), F32), SDS((1, BW), F32)),
        scratch_shapes=[pltpu.VMEM((tb, BW), F32)],
        compiler_params=_params("arbitrary"),
    )(pb, lng, lnb, ws, bs, dy)


def _mem_softmax(qh, kh):
    s = _dot(qh, kh, NT) * (M_HEAD_DIM ** -0.5)
    e = jnp.exp(s - jnp.max(s, axis=-1, keepdims=True))
    return e / jnp.sum(e, axis=-1, keepdims=True)


def _mem_attn_fwd(pm, kv):
    t = pm.shape[0]
    tq = _tile(t, 512, 8)
    ml = kv.shape[0]

    def body(pm_ref, kv_ref, y_ref):
        for h in range(M_HEADS):
            hs = slice(h * M_HEAD_DIM, (h + 1) * M_HEAD_DIM)
            kh = kv_ref[:, hs].astype(BF16)
            vh = kv_ref[:, BW + h * M_HEAD_DIM:BW + (h + 1) * M_HEAD_DIM].astype(BF16)
            p = _mem_softmax(pm_ref[:, hs].astype(BF16), kh)
            o = _dot(p.astype(BF16), vh, NN)
            z = pm_ref[:, BW + h * M_HEAD_DIM:BW + (h + 1) * M_HEAD_DIM].astype(F32)
            y_ref[:, hs] = (o * (z * _sigmoid(z))).astype(BF16)

    return pl.pallas_call(
        body, name="mem_attn_fwd", grid=(t // tq,),
        in_specs=[pl.BlockSpec((tq, W_M), lambda i: (i, 0)), pl.BlockSpec((ml, 2 * BW), lambda i: (0, 0))],
        out_specs=pl.BlockSpec((tq, BW), lambda i: (i, 0)),
        out_shape=SDS((t, BW), BF16), compiler_params=_params("parallel"),
    )(pm, kv)


def _mem_attn_bwd(pm, kv, dy):
    t = pm.shape[0]
    tq = _tile(t, 512, 8)
    ml = kv.shape[0]
    scale = M_HEAD_DIM ** -0.5

    def body(pm_ref, kv_ref, dy_ref, dpm_ref, dkv_ref):
        @pl.when(pl.program_id(0) == 0)
        def _():
            dkv_ref[...] = jnp.zeros_like(dkv_ref)

        for h in range(M_HEADS):
            hs = slice(h * M_HEAD_DIM, (h + 1) * M_HEAD_DIM)
            zs = slice(BW + h * M_HEAD_DIM, BW + (h + 1) * M_HEAD_DIM)
            kh = kv_ref[:, hs].astype(BF16)
            vh = kv_ref[:, zs].astype(BF16)
            qh = pm_ref[:, hs].astype(BF16)
            p = _mem_softmax(qh, kh)
            pb = p.astype(BF16)
            o = _dot(pb, vh, NN)
            sil, dsil = _silu_and_grad(pm_ref[:, zs].astype(F32))
            dyh = dy_ref[:, hs]
            do = dyh * sil
            dpm_ref[:, zs] = (dyh * o * dsil).astype(BF16)
            delta = jnp.sum(do * o, axis=-1, keepdims=True)
            do_b = do.astype(BF16)
            dp = _dot(do_b, vh, NT)
            dr_b = (p * (dp - delta) * scale).astype(BF16)
            dpm_ref[:, hs] = _dot(dr_b, kh, NN).astype(BF16)
            dkv_ref[:, hs] += _dot(dr_b, qh, TN)
            dkv_ref[:, zs] += _dot(pb, do_b, TN)

    kvb = pl.BlockSpec((ml, 2 * BW), lambda i: (0, 0))
    return pl.pallas_call(
        body, name="mem_attn_bwd", grid=(t // tq,),
        in_specs=[pl.BlockSpec((tq, W_M), lambda i: (i, 0)), kvb, pl.BlockSpec((tq, BW), lambda i: (i, 0))],
        out_specs=(pl.BlockSpec((tq, W_M), lambda i: (i, 0)), kvb),
        out_shape=(SDS((t, W_M), BF16), SDS((ml, 2 * BW), F32)),
        compiler_params=_params("arbitrary"),
    )(pm, kv, dy)


def _merge_fwd(ya, yb, ym, pg, wbr, exchange=None):
    t = ya.shape[0]
    tm = _tile(t, 512, 8)

    def body(ya_ref, yb_ref, ym_ref, pg_ref, w_ref, m_ref):
        acc = None
        for n, y_ref in enumerate((ya_ref, yb_ref, ym_ref)):
            up = _dot(y_ref[...], w_ref[n], NN)
            term = _sigmoid(pg_ref[:, n * D_MODEL:(n + 1) * D_MODEL].astype(F32)) * up
            acc = term if acc is None else acc + term
        m_ref[...] = acc.astype(BF16)

    yb_spec = pl.BlockSpec((tm, BW), lambda i: (i, 0))
    return _host_call(
        body, (ya, yb, ym, pg, wbr),
        [yb_spec, yb_spec, yb_spec, pl.BlockSpec((tm, W_G), lambda i: (i, 0)),
         pl.BlockSpec((3, BW, D_MODEL), lambda i: (0, 0, 0))],
        (pl.BlockSpec((tm, D_MODEL), lambda i: (i, 0)),), (SDS((t, D_MODEL), BF16),),
        name="merge_fwd", grid=(t // tm,), semantics=("parallel",), exchange=exchange)


def _merge_bwd(ya, yb, ym, pg, wbr, dm, exchange=None):
    t = ya.shape[0]
    tm = _tile(t, 256, 8)

    def body(ya_ref, yb_ref, ym_ref, pg_ref, w_ref, dm_ref, dya_ref, dyb_ref, dym_ref, dpg_ref, dw_ref):
        @pl.when(pl.program_id(0) == 0)
        def _():
            dw_ref[...] = jnp.zeros_like(dw_ref)

        dmf = dm_ref[...]
        for n, (y_ref, dy_ref) in enumerate(((ya_ref, dya_ref), (yb_ref, dyb_ref), (ym_ref, dym_ref))):
            cs_ = slice(n * D_MODEL, (n + 1) * D_MODEL)
            y = y_ref[...]
            w = w_ref[n]
            up = _dot(y, w, NN)
            gt = _sigmoid(pg_ref[:, cs_].astype(F32))
            dpg_ref[:, cs_] = (dmf * up * gt * (1.0 - gt)).astype(BF16)
            dup = (dmf * gt).astype(BF16)
            dy_ref[...] = _dot(dup, w, NT)
            dw_ref[n] += _dot(y, dup, TN)

    y_spec = pl.BlockSpec((tm, BW), lambda i: (i, 0))
    w_spec = pl.BlockSpec((3, BW, D_MODEL), lambda i: (0, 0, 0))
    return _host_call(
        body, (ya, yb, ym, pg, wbr, dm),
        [y_spec, y_spec, y_spec, pl.BlockSpec((tm, W_G), lambda i: (i, 0)), w_spec,
         pl.BlockSpec((tm, D_MODEL), lambda i: (i, 0))],
        (y_spec, y_spec, y_spec, pl.BlockSpec((tm, W_G), lambda i: (i, 0)), w_spec),
        (SDS((t, BW), F32), SDS((t, BW), F32), SDS((t, BW), F32), SDS((t, W_G), BF16), SDS((3, BW, D_MODEL), F32)),
        name="merge_bwd", grid=(t // tm,), semantics=("arbitrary",), exchange=exchange)


def _loss_head(x, g, target):
    t, d = x.shape
    tm = _tile(t, 512, 8)

    def body(x_ref, g_ref, t_ref, loss_ref, dx_ref, dg_ref):
        @pl.when(pl.program_id(0) == 0)
        def _():
            loss_ref[...] = jnp.zeros_like(loss_ref)
            dg_ref[...] = jnp.zeros_like(dg_ref)

        xf = x_ref[...]
        r = lax.rsqrt(jnp.mean(xf * xf, axis=-1, keepdims=True) + EPS)
        xh = xf * r
        err = xh * g_ref[...] - t_ref[...]
        per_tok = jnp.mean(err * err, axis=-1, keepdims=True)
        loss_ref[...] += 0.5 * jnp.sum(per_tok, axis=0, keepdims=True)
        dy = err * (1.0 / d)
        gd = dy * g_ref[...]
        dx_ref[...] = r * (gd - xh * jnp.mean(gd * xh, axis=-1, keepdims=True))
        dg_ref[...] += jnp.sum(dy * xh, axis=0, keepdims=True)

    row = pl.BlockSpec((tm, d), lambda i: (i, 0))
    vec = pl.BlockSpec((1, d), lambda i: (0, 0))
    return pl.pallas_call(
        body, name="loss_head", grid=(t // tm,),
        in_specs=[row, vec, row],
        out_specs=(pl.BlockSpec((1, 1), lambda i: (0, 0)), row, vec),
        out_shape=(SDS((1, 1), F32), SDS((t, d), F32), SDS((1, d), F32)),
        compiler_params=_params("arbitrary"),
    )(x, g, target)


def _layer_fwd(x, mem, w, tabs, next_shards=None):
    cs, sn = tabs
    riding = next_shards is not None
    h = _rmsnorm_fwd(x, w["norm_g"], "rmsnorm_fwd")
    pa, pb, pm, pg = [_mm(h, w["w_" + nm], "nn", "proj_" + nm, out_dtype=BF16, tm_cap=1024) for nm in "abmg"]
    q, k, v = _attn_prep_fwd(pa, cs, sn, w["qg2"], w["kg2"])
    (o, lse, ya), gathered = _attn_fwd(q, k, v, pa, exchange=_gather_first_hop(next_shards) if riding else None)
    yb = _gmlp_fwd(pb, w["ln_g"], w["ln_b"], w["w_s"], w["b_s"])
    memn = _rmsnorm_fwd(mem, w["mem_g"], "mem_rmsnorm_fwd")
    kv = _mm(memn, w["w_kv"], "nn", "mem_kv")
    ym = _mem_attn_fwd(pm, kv)
    (merged,), gathered = _merge_fwd(ya, yb, ym, pg, w["w_br"], exchange=_gather_forward(gathered) if riding else None)
    x_next = _mm(merged, w["w_out"], "nn", "out_proj", add=x)
    saved = dict(x=x, h=h, pa=pa, pb=pb, pm=pm, pg=pg, q=q, k=k, v=v, o=o, lse=lse, ya=ya, yb=yb, ym=ym,
                 memn=memn, kv=kv, merged=merged)
    return x_next, saved, gathered


def _layer_bwd(dx_out, mem, w, s, tabs, pending=None, core=None):
    cs, sn = tabs
    t = dx_out.shape[0]
    riding = pending is not None
    dmerged = _mm(dx_out, w["w_out"], "nt", "d_merged")
    shard_rows = D_MODEL // N_DEV
    d_w_out = _mm_tn_by_owner(s["merged"], dx_out, "d_w_out")
    (dya, dyb, dym, dpg, d_w_br), recv = _merge_bwd(s["ya"], s["yb"], s["ym"], s["pg"], w["w_br"], dmerged,
                                                    exchange=_scatter_to_sibling(pending) if riding else None)
    parts = _pair_sums(pending, recv, core) if riding else None
    (dq, dk, dv), from_chips = _attn_bwd(s["q"], s["k"], s["v"], s["pa"], s["o"], s["lse"], dya,
                                         exchange=_scatter_to_chips(parts) if riding else None)
    dk_tm = dk.transpose(1, 0, 2).reshape(t, A_KV_HEADS * HEAD_DIM)
    dv_tm = dv.transpose(1, 0, 2).reshape(t, A_KV_HEADS * HEAD_DIM)
    dpa, d_qg2, d_kg2 = _attn_prep_bwd(s["pa"], cs, sn, w["qg2"], w["kg2"], dq, dk_tm, dv_tm, dya, s["o"])
    dpb, d_w_s, d_b_s, d_ln_g, d_ln_b = _gmlp_bwd(s["pb"], w["ln_g"], w["ln_b"], w["w_s"], w["b_s"], dyb)
    dpm, dkv = _mem_attn_bwd(s["pm"], s["kv"], dym)
    d_w_kv = _mm_tn_by_owner(s["memn"], dkv, "d_w_kv")
    dmemn = _mm(dkv, w["w_kv"], "nt", "d_memn")
    _, d_mem_g = _rmsnorm_bwd(mem, w["mem_g"], dmemn, None, "mem_rmsnorm_bwd")
    dh = None
    for nm, dp in (("a", dpa), ("b", dpb), ("m", dpm), ("g", dpg)):
        dh = _mm(dp, w["w_" + nm], "nt", "dh_" + nm, add=dh, tk_cap=W_G)
    d_w_in = _split_w_in(*[_mm(s["h"], dp, "tn", "d_w_" + nm, out_dtype=BF16, tm_cap=1024, tk_cap=2048)
                           for nm, dp in (("a", dpa), ("b", dpb), ("m", dpm), ("g", dpg))])
    dx_in, d_norm_g = _rmsnorm_bwd(s["x"], w["norm_g"], dh, dx_out, "rmsnorm_bwd")
    grads = dict(w_in=d_w_in[:, None], w_mem_kv=d_w_kv.reshape(N_DEV, 1, shard_rows, -1), w_br=_split_w_br(d_w_br),
                 w_out=d_w_out.reshape(N_DEV, 1, shard_rows, -1),
                 norm_g=d_norm_g[0], q_norm_g=d_qg2[0, :HEAD_DIM] + d_qg2[0, HEAD_DIM:],
                 k_norm_g=d_kg2[0, :HEAD_DIM] + d_kg2[0, HEAD_DIM:], sg_ln_g=d_ln_g[0], sg_ln_b=d_ln_b[0],
                 w_s=d_w_s, b_s=d_b_s[:, :, 0], mem_norm_g=d_mem_g[0])
    return dx_in, grads, ((parts, from_chips) if riding else None)


def _layer_weights(l, w_groups, w_kv, w_br, w_out, small):
    tile2 = lambda g: jnp.tile(g.reshape(1, -1), (1, 2))
    w_a, w_b, w_m, w_g = w_groups
    return dict(
        w_a=w_a, w_b=w_b, w_m=w_m, w_g=w_g, w_kv=w_kv, w_br=w_br, w_out=w_out,
        norm_g=small["norm_g"][l].reshape(1, -1), qg2=tile2(small["q_norm_g"][l]), kg2=tile2(small["k_norm_g"][l]),
        ln_g=small["sg_ln_g"][l].reshape(1, -1), ln_b=small["sg_ln_b"][l].reshape(1, -1),
        w_s=small["w_s"][l], b_s=small["b_s"][l][:, :, None], mem_g=small["mem_norm_g"][l].reshape(1, -1))


def _position():
    x, y, c = lax.axis_index("x"), lax.axis_index("y"), lax.axis_index("c")
    return x, y, c, [(1 - x, y), (x, 1 - y), (1 - x, 1 - y)]


def _gather_first_hop(shards):
    n = len(shards)

    def make(x_refs, out_refs, sems):
        send_sems, recv_sems, local_sems = sems
        x, y, c, chips = _position()
        me = 4 * x + 2 * y + c
        peers = [(x, y, 1 - c)] + [(cx, cy, c) for cx, cy in chips]
        copies = [pltpu.make_async_copy(x_refs[t], out_refs[t].at[me], local_sems.at[t]) for t in range(n)]
        copies += [pltpu.make_async_remote_copy(
            src_ref=x_refs[t], dst_ref=out_refs[t].at[me], send_sem=send_sems.at[t, k], recv_sem=recv_sems.at[t, k],
            device_id=peer, device_id_type=MESH_ID) for t in range(n) for k, peer in enumerate(peers)]
        return copies

    return _Exchange(shards, [SDS((N_DEV,) + a.shape, a.dtype) for a in shards],
                     [pltpu.SemaphoreType.DMA((n, 4)), pltpu.SemaphoreType.DMA((n, 4)), pltpu.SemaphoreType.DMA((n,))], make)


def _gather_forward(gathered):
    n = len(gathered)

    def make(in_refs, out_refs, sems):
        send_sems, recv_sems = sems
        x, y, c, chips = _position()
        return [pltpu.make_async_remote_copy(
            src_ref=in_refs[t].at[4 * cx + 2 * cy + c], dst_ref=out_refs[t].at[4 * cx + 2 * cy + c],
            send_sem=send_sems.at[t, j], recv_sem=recv_sems.at[t, j], device_id=(x, y, 1 - c), device_id_type=MESH_ID)
            for t in range(n) for j, (cx, cy) in enumerate(chips)]

    return _Exchange(gathered, [SDS(a.shape, a.dtype) for a in gathered],
                     [pltpu.SemaphoreType.DMA((n, 3)), pltpu.SemaphoreType.DMA((n, 3))], make,
                     aliases={t: t for t in range(n)})


def _all_gather(shards, name):
    return _exchange_call(_gather_forward(_exchange_call(_gather_first_hop(shards), name + "_hop1")), name + "_hop2")


def _shard_segments(d):
    shard = IN_WIDTH // N_DEV
    lo, hi = d * shard, (d + 1) * shard
    out = []
    for g in range(4):
        a, b = max(lo, GROUP_OFFS[g]), min(hi, GROUP_OFFS[g + 1])
        if a < b:
            out.append((g, a - GROUP_OFFS[g], b - GROUP_OFFS[g], a - lo))
    return out


def _assemble_w_in(gathered):
    _, rows, shard = gathered.shape
    tr = _tile(rows, 256, 16)
    widths = [GROUP_OFFS[g + 1] - GROUP_OFFS[g] for g in range(4)]

    def body(g_ref, *outs):
        for d in range(N_DEV):
            for g, lo, hi, off in _shard_segments(d):
                outs[g][:, lo:hi] = g_ref[d, :, off:off + hi - lo]

    return pl.pallas_call(
        body, name="assemble_w_in", grid=(rows // tr,),
        in_specs=[pl.BlockSpec((N_DEV, tr, shard), lambda i: (0, i, 0))],
        out_specs=tuple(pl.BlockSpec((tr, w), lambda i: (i, 0)) for w in widths),
        out_shape=tuple(SDS((rows, w), gathered.dtype) for w in widths), compiler_params=_params("parallel"),
    )(gathered)


def _split_w_in(da, db, dm, dg):
    rows = da.shape[0]
    shard = IN_WIDTH // N_DEV
    tr = _tile(rows, 256, 16)
    widths = [GROUP_OFFS[g + 1] - GROUP_OFFS[g] for g in range(4)]

    def body(a_ref, b_ref, m_ref, g_ref, o_ref):
        ins = (a_ref, b_ref, m_ref, g_ref)
        for d in range(N_DEV):
            for g, lo, hi, off in _shard_segments(d):
                o_ref[_core_major(d), :, off:off + hi - lo] = ins[g][:, lo:hi]

    return pl.pallas_call(
        body, name="split_w_in", grid=(rows // tr,),
        in_specs=[pl.BlockSpec((tr, w), lambda i: (i, 0)) for w in widths],
        out_specs=pl.BlockSpec((N_DEV, tr, shard), lambda i: (0, i, 0)),
        out_shape=SDS((N_DEV, rows, shard), da.dtype), compiler_params=_params("parallel"),
    )(da, db, dm, dg)


def _assemble_w_br(gathered):
    _, nb, rows, shard = gathered.shape

    def body(g_ref, o_ref):
        o_ref[...] = g_ref[...]

    return pl.pallas_call(
        body, name="assemble_w_br", grid=(N_DEV,),
        in_specs=[pl.BlockSpec((None, nb, rows, shard), lambda d: (d, 0, 0, 0))],
        out_specs=pl.BlockSpec((nb, rows, shard), lambda d: (0, 0, d)),
        out_shape=SDS((nb, rows, N_DEV * shard), gathered.dtype), compiler_params=_params("parallel"),
    )(gathered)


def _split_w_br(dw):
    nb, rows, cols = dw.shape
    shard = cols // N_DEV

    def body(d_ref, o_ref):
        o_ref[...] = d_ref[...].astype(BF16)

    return pl.pallas_call(
        body, name="split_w_br", grid=(N_DEV,),
        in_specs=[pl.BlockSpec((nb, rows, shard), lambda d: (0, 0, d))],
        out_specs=pl.BlockSpec((None, nb, rows, shard), lambda d: (_core_major(d), 0, 0, 0)),
        out_shape=SDS((N_DEV, nb, rows, shard), BF16), compiler_params=_params("parallel"),
    )(dw)


def _scatter_to_sibling(dests):
    n = len(dests)

    def make(d_refs, recv_refs, sems):
        send_sems, recv_sems = sems
        x, y, c, _ = _position()
        return [pltpu.make_async_remote_copy(
            src_ref=d_refs[t].at[pl.ds((1 - c) * 4, 4)], dst_ref=recv_refs[t], send_sem=send_sems.at[t],
            recv_sem=recv_sems.at[t], device_id=(x, y, 1 - c), device_id_type=MESH_ID) for t in range(n)]

    return _Exchange(dests, [SDS((4,) + a.shape[1:], a.dtype) for a in dests],
                     [pltpu.SemaphoreType.DMA((n,)), pltpu.SemaphoreType.DMA((n,))], make)


def _scatter_to_chips(parts):
    n = len(parts)

    def make(p_refs, recv_refs, sems):
        send_sems, recv_sems = sems
        _, _, c, chips = _position()
        return [pltpu.make_async_remote_copy(
            src_ref=p_refs[t].at[2 * cx + cy], dst_ref=recv_refs[t].at[k], send_sem=send_sems.at[t, k],
            recv_sem=recv_sems.at[t, k], device_id=(cx, cy, c), device_id_type=MESH_ID)
            for t in range(n) for k, (cx, cy) in enumerate(chips)]

    return _Exchange(parts, [SDS((3,) + a.shape[1:], a.dtype) for a in parts],
                     [pltpu.SemaphoreType.DMA((n, 3)), pltpu.SemaphoreType.DMA((n, 3))], make)


def _pair_sums(dests, recv, core):
    return [_pair_sum(d, r, core, "rs_pair_sum_" + n) for n, d, r in zip(BIG, dests, recv)]


def _pair_sum(dest, recv, core, name):
    _, na, r, cdim = dest.shape
    tr = _tile(r, 256, 16)

    def body(core_ref, a_ref, b_ref, o_ref):
        o_ref[...] = (a_ref[...].astype(F32) + b_ref[...].astype(F32)).astype(o_ref.dtype)

    blk = pl.BlockSpec((None, None, tr, cdim), lambda j, a, i, core_ref: (j, a, i, 0))
    return pl.pallas_call(
        body, name=name, out_shape=SDS(recv.shape, BF16),
        grid_spec=pltpu.PrefetchScalarGridSpec(
            num_scalar_prefetch=1, grid=(4, na, r // tr),
            in_specs=[pl.BlockSpec((None, None, tr, cdim), lambda j, a, i, core_ref: (core_ref[0] * 4 + j, a, i, 0)), blk],
            out_specs=blk),
        compiler_params=_params("parallel", "parallel", "parallel"),
    )(core, dest, recv)


def _adamw_math(w, g, m, v):
    m = ADAM_B1 * m + (1.0 - ADAM_B1) * g
    v = ADAM_B2 * v + (1.0 - ADAM_B2) * (g * g)
    m_hat = m / (1.0 - ADAM_B1 ** ADAM_STEP)
    v_hat = v / (1.0 - ADAM_B2 ** ADAM_STEP)
    delta = -ADAM_LR * (m_hat / (jnp.sqrt(v_hat) + ADAM_EPS) + ADAM_WD * w)
    return delta, m, v


def _sum_and_adamw(parts, w, m, v, name):
    n, r, ln = parts.shape
    tr = _tile(r, 512, 16)

    def body(p_ref, w_ref, m_ref, v_ref, g_out, d_out, m_out, v_out):
        g = p_ref[0].astype(F32)
        for j in range(1, n):
            g = g + p_ref[j].astype(F32)
        delta, nm, nv = _adamw_math(w_ref[...], g, m_ref[...], v_ref[...])
        g_out[...] = g
        d_out[...] = delta
        m_out[...] = nm
        v_out[...] = nv

    blk = pl.BlockSpec((tr, ln), lambda i: (i, 0))
    return pl.pallas_call(
        body, name=name, grid=(r // tr,),
        in_specs=[pl.BlockSpec((n, tr, ln), lambda i: (0, i, 0)), blk, blk, blk],
        out_specs=(blk, blk, blk, blk), out_shape=tuple(SDS((r, ln), F32) for _ in range(4)),
        compiler_params=_params("parallel"),
    )(parts, w, m, v)


BIG = ("w_in", "w_mem_kv", "w_br", "w_out")
SMALL = ("norm_g", "q_norm_g", "k_norm_g", "sg_ln_g", "sg_ln_b", "w_s", "b_s", "mem_norm_g", "final_g")


def _pack(arrs, row_unit=16):
    flat = jnp.concatenate([a.reshape(-1) for a in arrs])
    pad = (-flat.shape[0]) % (row_unit * LANES)
    if pad:
        flat = jnp.concatenate([flat, jnp.zeros((pad,), flat.dtype)])
    return flat.reshape(-1, LANES)


def _unpack(buf, shapes):
    flat = buf.reshape(-1)
    out, off = [], 0
    for shp in shapes:
        n = 1
        for s_ in shp:
            n *= s_
        out.append(flat[off:off + n].reshape(shp))
        off += n
    return out


def _shard_sum_adamw(part, from_chips, chip, w, m, v, layer, prev, name):
    _, na, r, cdim = part.shape
    tr = _tile(r, 256, 8)
    n_prev = 0 if prev is None else len(prev)

    def body(chip_ref, p_ref, f_ref, w_ref, m_ref, v_ref, *rest):
        g_out, d_out, m_out, v_out = rest[n_prev:]
        g = p_ref[...].astype(F32)
        for j in range(3):
            g = g + f_ref[j].astype(F32)
        delta, nm, nv = _adamw_math(w_ref[...], g, m_ref[...], v_ref[...])
        g_out[...] = g
        d_out[...] = delta
        m_out[...] = nm
        v_out[...] = nv

    blk = (None, None, tr, cdim)
    lay = pl.BlockSpec(blk, lambda a, i, chip_ref: (layer, a, i, 0))
    return pl.pallas_call(
        body, name=name, out_shape=tuple(SDS(w.shape, F32) for _ in range(4)),
        grid_spec=pltpu.PrefetchScalarGridSpec(
            num_scalar_prefetch=1, grid=(na, r // tr),
            in_specs=[pl.BlockSpec(blk, lambda a, i, chip_ref: (chip_ref[0], a, i, 0)),
                      pl.BlockSpec((3, None, tr, cdim), lambda a, i, chip_ref: (0, a, i, 0)), lay, lay, lay]
            + [HBM] * n_prev,
            out_specs=(lay, lay, lay, lay)),
        input_output_aliases={6 + j: j for j in range(n_prev)},
        compiler_params=_params("parallel", "parallel"),
    )(chip, part, from_chips, w, m, v, *(prev or ()))


def kernel(x, mem, norm_g, w_in, q_norm_g, k_norm_g, sg_ln_g, sg_ln_b, w_s, b_s, mem_norm_g, w_mem_kv, w_br, w_out, final_g, loss_target, m_norm_g, m_w_in, m_q_norm_g, m_k_norm_g, m_sg_ln_g, m_sg_ln_b, m_w_s, m_b_s, m_mem_norm_g, m_w_mem_kv, m_w_br, m_w_out, m_final_g, v_norm_g, v_w_in, v_q_norm_g, v_k_norm_g, v_sg_ln_g, v_sg_ln_b, v_w_s, v_b_s, v_mem_norm_g, v_w_mem_kv, v_w_br, v_w_out, v_final_g):
    wts = dict(norm_g=norm_g, w_in=w_in, q_norm_g=q_norm_g, k_norm_g=k_norm_g, sg_ln_g=sg_ln_g, sg_ln_b=sg_ln_b,
               w_s=w_s, b_s=b_s, mem_norm_g=mem_norm_g, w_mem_kv=w_mem_kv, w_br=w_br, w_out=w_out, final_g=final_g)
    mom1 = dict(norm_g=m_norm_g, w_in=m_w_in, q_norm_g=m_q_norm_g, k_norm_g=m_k_norm_g, sg_ln_g=m_sg_ln_g,
                sg_ln_b=m_sg_ln_b, w_s=m_w_s, b_s=m_b_s, mem_norm_g=m_mem_norm_g, w_mem_kv=m_w_mem_kv, w_br=m_w_br,
                w_out=m_w_out, final_g=m_final_g)
    mom2 = dict(norm_g=v_norm_g, w_in=v_w_in, q_norm_g=v_q_norm_g, k_norm_g=v_k_norm_g, sg_ln_g=v_sg_ln_g,
                sg_ln_b=v_sg_ln_b, w_s=v_w_s, b_s=v_b_s, mem_norm_g=v_mem_norm_g, w_mem_kv=v_w_mem_kv, w_br=v_w_br,
                w_out=v_w_out, final_g=v_final_g)
    dp = w_in.shape[0]
    core = lax.axis_index("c").astype(jnp.int32).reshape(1)
    chip = (2 * lax.axis_index("x") + lax.axis_index("y")).astype(jnp.int32).reshape(1)

    shard_bf = {n: wts[n].astype(BF16) for n in BIG}
    shards = lambda l: [shard_bf[n][l] for n in BIG]
    x_l, mem_l = x[0], mem[0]
    tabs = _rope_tables(x_l.shape[0])

    gathered = _all_gather(shards(0), "weights_all_gather")
    layers, saved = [], []
    for l in range(dp):
        g_in, g_kv, g_br, g_out = gathered
        layers.append(_layer_weights(l, _assemble_w_in(g_in), g_kv.reshape(D_MODEL, -1), _assemble_w_br(g_br),
                                     g_out.reshape(D_MODEL, -1), wts))
        x_l, s, gathered = _layer_fwd(x_l, mem_l, layers[l], tabs, next_shards=shards(l + 1) if l + 1 < dp else None)
        saved.append(s)
    loss_local, dx, d_final_g = _loss_head(x_l, final_g.reshape(1, -1), loss_target[0])
    loss = lax.psum(loss_local[0, 0], AXES)

    as4 = lambda a: a.reshape(a.shape[:1] + (1,) * (4 - a.ndim) + a.shape[1:])

    def finish(l, parts, from_chips, prev):
        return {n: _shard_sum_adamw(p, f, chip, as4(wts[n]), as4(mom1[n]), as4(mom2[n]), l,
                                    None if prev is None else prev[n], "sum_adamw_" + n)
                for n, p, f in zip(BIG, parts, from_chips)}

    grads, updated, pending = [None] * dp, None, None
    for l in reversed(range(dp)):
        dx, grads[l], scattered = _layer_bwd(dx, mem_l, layers[l], saved[l], tabs, pending=pending, core=core)
        if scattered is not None:
            updated = finish(l + 1, *scattered, updated)
        pending = [grads[l][n] for n in BIG]
    recv = _exchange_call(_scatter_to_sibling(pending), "rs_sibling_swap")
    parts = _pair_sums(pending, recv, core)
    updated = finish(0, parts, _exchange_call(_scatter_to_chips(parts), "rs_chip_swap"), updated)
    grad_x = dx
    big_out = [{n: updated[n][k].reshape(wts[n].shape) for n in BIG} for k in range(4)]

    small_g = {n: jnp.stack([g[n] for g in grads]) for n in SMALL if n != "final_g"}
    small_g["final_g"] = d_final_g
    (all_small,) = _all_gather([_pack([small_g[n] for n in SMALL])], "small_all_gather")
    small_bufs = _sum_and_adamw(
        all_small, _pack([wts[n] for n in SMALL]), _pack([mom1[n] for n in SMALL]), _pack([mom2[n] for n in SMALL]),
        "small_sum_adamw")

    outs = []
    for big_vals, small_buf in zip(big_out, small_bufs):
        vals = dict(big_vals)
        vals.update(zip(SMALL, _unpack(small_buf, [wts[n].shape for n in SMALL])))
        outs.append(vals)
    order = ("norm_g", "w_in", "q_norm_g", "k_norm_g", "sg_ln_g", "sg_ln_b", "w_s", "b_s", "mem_norm_g", "w_mem_kv",
             "w_br", "w_out", "final_g")
    result = [loss, grad_x[None]]
    for vals in outs:
        result += [vals[n] for n in order]
    return tuple(result)
```

```python
import functools

import jax
import jax.numpy as jnp
from jax import lax
from jax.experimental import pallas as pl
from jax.experimental.pallas import tpu as pltpu

F32 = jnp.float32
BF16 = jnp.bfloat16
SDS = jax.ShapeDtypeStruct
MESH_ID = pl.DeviceIdType.MESH
AXES = ("x", "y", "c")
N_DEV = 8

D_MODEL = 1024
DEPTH = 4
GRID_W = 64
CHUNK = 128
ROPE_THETA = 10000.0
EPS = 1e-6
HEAD_DIM = 64
A_HEADS = 8
A_KV_HEADS = 2
B_GROUPS = 4
M_HEADS = 4
M_HEAD_DIM = 128
BW = 512
W_A = 1280
W_B = 1536
W_M = 1024
W_G = 3072
IN_WIDTH = W_A + W_B + W_M + W_G
GROUP_OFFS = (0, W_A, W_A + W_B, W_A + W_B + W_M, IN_WIDTH)

ADAM_LR = 0.001
ADAM_B1 = 0.9
ADAM_B2 = 0.999
ADAM_EPS = 1e-08
ADAM_WD = 0.01
ADAM_STEP = 10

LANES = 128
VMEM_LIMIT = 52 * 1024 * 1024


def _tile(n, cap, unit=LANES):
    if n <= cap:
        return n
    t = (cap // unit) * unit
    while n % t:
        t -= unit
    return t


def _params(*sem):
    return pltpu.CompilerParams(dimension_semantics=sem, vmem_limit_bytes=VMEM_LIMIT)


def _sigmoid(z):
    return 1.0 / (1.0 + jnp.exp(-z))


def _silu_and_grad(z):
    s = _sigmoid(z)
    return z * s, s * (1.0 + z * (1.0 - s))


def _dot(a, b, dims):
    return lax.dot_general(a, b, (dims, ((), ())), preferred_element_type=F32)


NN = ((1,), (0,))
NT = ((1,), (1,))
TN = ((0,), (0,))
HBM = pl.BlockSpec(memory_space=pl.ANY)


class _Exchange:
    def __init__(self, ins, out_shapes, sems, make, aliases=None):
        self.ins, self.out_shapes, self.sems, self.make = list(ins), list(out_shapes), list(sems), make
        self.aliases = dict(aliases or {})

    def start(self, in_refs, out_refs, sems):
        for cp in self.make(in_refs, out_refs, sems):
            cp.start()

    def finish(self, in_refs, out_refs, sems):
        for cp in self.make(in_refs, out_refs, sems):
            cp.wait()


def _host_call(body, ins, in_specs, out_specs, out_shape, *, name, grid, semantics, scratch=(), exchange=None):
    ins, out_shape, scratch = list(ins), list(out_shape), list(scratch)
    if exchange is None:
        res = pl.pallas_call(
            body, name=name, grid=grid, in_specs=list(in_specs), out_specs=tuple(out_specs), out_shape=tuple(out_shape),
            scratch_shapes=scratch, compiler_params=_params(*semantics))(*ins)
        return tuple(res), ()
    n_in, n_out, n_scr = len(ins), len(out_shape), len(scratch)
    x_in, x_out = len(exchange.ins), len(exchange.out_shapes)

    def carrying(*refs):
        o0 = n_in + x_in
        s0 = o0 + n_out + x_out
        c_in, c_out, c_sems = refs[n_in:o0], refs[o0 + n_out:s0], refs[s0 + n_scr:]
        ids = [pl.program_id(a) for a in range(len(grid))]
        first = functools.reduce(jnp.logical_and, [i == 0 for i in ids])
        last = functools.reduce(jnp.logical_and, [i == g - 1 for i, g in zip(ids, grid)])

        @pl.when(first)
        def _():
            exchange.start(c_in, c_out, c_sems)

        body(*refs[:n_in], *refs[o0:o0 + n_out], *refs[s0:s0 + n_scr])

        @pl.when(last)
        def _():
            exchange.finish(c_in, c_out, c_sems)

    res = pl.pallas_call(
        carrying, name=name, grid=grid, in_specs=list(in_specs) + [HBM] * x_in,
        out_specs=tuple(out_specs) + (HBM,) * x_out, out_shape=tuple(out_shape) + tuple(exchange.out_shapes),
        scratch_shapes=scratch + exchange.sems,
        input_output_aliases={n_in + i: n_out + o for i, o in exchange.aliases.items()},
        compiler_params=_params(*(["arbitrary"] * len(grid))))(*ins, *exchange.ins)
    return tuple(res[:n_out]), tuple(res[n_out:])


def _exchange_call(exchange, name):
    x_in = len(exchange.ins)

    def body(*refs):
        x_out = len(exchange.out_shapes)
        c_in, c_out, c_sems = refs[:x_in], refs[x_in:x_in + x_out], refs[x_in + x_out:]
        exchange.start(c_in, c_out, c_sems)
        exchange.finish(c_in, c_out, c_sems)

    return pl.pallas_call(
        body, name=name, in_specs=[HBM] * x_in, out_specs=tuple([HBM] * len(exchange.out_shapes)),
        out_shape=tuple(exchange.out_shapes), scratch_shapes=exchange.sems, input_output_aliases=exchange.aliases,
    )(*exchange.ins)


def _core_major(d):
    return (d % 2) * 4 + d // 2


def _mm(a, b, mode, name, add=None, out_dtype=F32, tm_cap=512, tn_cap=1536, tk_cap=1536):
    if mode == "nn":
        (m, k), (k2, n) = a.shape, b.shape
    elif mode == "nt":
        (m, k), (n, k2) = a.shape, b.shape
    else:
        (k, m), (k2, n) = a.shape, b.shape
    assert k == k2, (a.shape, b.shape, mode)
    tm = _tile(m, tm_cap, 8 if mode != "tn" else LANES)
    tn = _tile(n, tn_cap)
    tk = _tile(k, tk_cap, LANES if mode != "tn" else 16)
    nk = k // tk
    dims = {"nn": NN, "nt": NT, "tn": TN}[mode]
    a_spec = {"nn": pl.BlockSpec((tm, tk), lambda i, j, kk: (i, kk)),
              "nt": pl.BlockSpec((tm, tk), lambda i, j, kk: (i, kk)),
              "tn": pl.BlockSpec((tk, tm), lambda i, j, kk: (kk, i))}[mode]
    b_spec = {"nn": pl.BlockSpec((tk, tn), lambda i, j, kk: (kk, j)),
              "nt": pl.BlockSpec((tn, tk), lambda i, j, kk: (j, kk)),
              "tn": pl.BlockSpec((tk, tn), lambda i, j, kk: (kk, j))}[mode]
    o_spec = pl.BlockSpec((tm, tn), lambda i, j, kk: (i, j))
    has_add = add is not None

    def body(*refs):
        a_ref, b_ref = refs[0], refs[1]
        add_ref = refs[2] if has_add else None
        o_ref = refs[3] if has_add else refs[2]
        part = _dot(a_ref[...].astype(BF16), b_ref[...].astype(BF16), dims)
        if nk == 1:
            if has_add:
                part = part + add_ref[...]
            o_ref[...] = part.astype(out_dtype)
        else:
            acc = refs[-1]
            kk = pl.program_id(2)

            @pl.when(kk == 0)
            def _():
                acc[...] = part + add_ref[...] if has_add else part

            @pl.when(kk > 0)
            def _():
                acc[...] += part

            @pl.when(kk == nk - 1)
            def _():
                o_ref[...] = acc[...].astype(out_dtype)

    ins = [a, b] + ([add] if has_add else [])
    in_specs = [a_spec, b_spec] + ([o_spec] if has_add else [])
    return pl.pallas_call(
        body, name=name, grid=(m // tm, n // tn, nk), in_specs=in_specs, out_specs=o_spec,
        out_shape=SDS((m, n), out_dtype),
        scratch_shapes=[pltpu.VMEM((tm, tn), F32)] if nk > 1 else [],
        compiler_params=_params("parallel", "parallel", "arbitrary"),
    )(*ins)


def _mm_tn_by_owner(a, b, name):
    (k, m), (k2, n) = a.shape, b.shape
    assert k == k2 and m % N_DEV == 0, (a.shape, b.shape)
    tk = _tile(k, 512, 16)
    nk = k // tk
    rows = m // N_DEV

    def body(a_ref, b_ref, o_ref, acc):
        kk = pl.program_id(0)
        part = _dot(a_ref[...].astype(BF16), b_ref[...].astype(BF16), TN)

        @pl.when(kk == 0)
        def _():
            acc[...] = part

        @pl.when(kk > 0)
        def _():
            acc[...] += part

        @pl.when(kk == nk - 1)
        def _():
            for d in range(N_DEV):
                pos = _core_major(d)
                o_ref[pos * rows:(pos + 1) * rows, :] = acc[d * rows:(d + 1) * rows, :].astype(BF16)

    return pl.pallas_call(
        body, name=name, grid=(nk,),
        in_specs=[pl.BlockSpec((tk, m), lambda kk: (kk, 0)), pl.BlockSpec((tk, n), lambda kk: (kk, 0))],
        out_specs=pl.BlockSpec((m, n), lambda kk: (0, 0)), out_shape=SDS((m, n), BF16),
        scratch_shapes=[pltpu.VMEM((m, n), F32)], compiler_params=_params("arbitrary"),
    )(a, b)


def _rmsnorm_fwd(x, g, name):
    t, d = x.shape
    tm = _tile(t, 512, 8)

    def body(x_ref, g_ref, h_ref):
        xf = x_ref[...]
        r = lax.rsqrt(jnp.mean(xf * xf, axis=-1, keepdims=True) + EPS)
        h_ref[...] = (xf * r * g_ref[...]).astype(BF16)

    return pl.pallas_call(
        body, name=name, grid=(t // tm,),
        in_specs=[pl.BlockSpec((tm, d), lambda i: (i, 0)), pl.BlockSpec((1, d), lambda i: (0, 0))],
        out_specs=pl.BlockSpec((tm, d), lambda i: (i, 0)),
        out_shape=SDS((t, d), BF16), compiler_params=_params("parallel"),
    )(x, g)


def _rmsnorm_bwd(x, g, dh, dres, name):
    t, d = x.shape
    tm = _tile(t, 512, 8)
    has_res = dres is not None

    def body(*refs):
        x_ref, g_ref, dh_ref = refs[:3]
        dx_ref, dg_ref = refs[-2:]
        xf = x_ref[...]
        r = lax.rsqrt(jnp.mean(xf * xf, axis=-1, keepdims=True) + EPS)
        xh = xf * r
        dhf = dh_ref[...].astype(F32)
        gd = dhf * g_ref[...]
        dx = r * (gd - xh * jnp.mean(gd * xh, axis=-1, keepdims=True))
        if has_res:
            dx = dx + refs[3][...]
        dx_ref[...] = dx

        @pl.when(pl.program_id(0) == 0)
        def _():
            dg_ref[...] = jnp.zeros_like(dg_ref)

        dg_ref[...] += jnp.sum(dhf * xh, axis=0, keepdims=True)

    row = pl.BlockSpec((tm, d), lambda i: (i, 0))
    vec = pl.BlockSpec((1, d), lambda i: (0, 0))
    return pl.pallas_call(
        body, name=name, grid=(t // tm,),
        in_specs=[row, vec, row] + ([row] if has_res else []),
        out_specs=(row, vec), out_shape=(SDS((t, d), F32), SDS((1, d), F32)),
        compiler_params=_params("arbitrary"),
    )(*([x, g, dh] + ([dres] if has_res else [])))


def _rope_tables(t):
    rows = t // GRID_W
    row = jnp.repeat(jnp.arange(rows, dtype=F32), GRID_W)
    col = jnp.tile(jnp.arange(GRID_W, dtype=F32), rows)
    n_freq = HEAD_DIM // 4
    inv = ROPE_THETA ** (-jnp.arange(n_freq, dtype=F32) / n_freq)
    ang = jnp.stack([row[:, None] * inv, col[:, None] * inv], axis=1)
    cos, sin = jnp.cos(ang), jnp.sin(ang)
    c64 = jnp.concatenate([cos[:, 0], cos[:, 0], cos[:, 1], cos[:, 1]], axis=-1)
    s64 = jnp.concatenate([-sin[:, 0], sin[:, 0], -sin[:, 1], sin[:, 1]], axis=-1)
    return jnp.tile(c64, (1, 2)), jnp.tile(s64, (1, 2))


def _head_sums(v, lane):
    lo = jnp.sum(jnp.where(lane < HEAD_DIM, v, 0.0), axis=-1, keepdims=True)
    hi = jnp.sum(jnp.where(lane < HEAD_DIM, 0.0, v), axis=-1, keepdims=True)
    return jnp.where(lane < HEAD_DIM, lo, hi)


def _swap16(v, lane):
    return jnp.where((lane % 32) < 16, pltpu.roll(v, LANES - 16, 1), pltpu.roll(v, 16, 1))


def _attn_prep_fwd(pa, cs, sn, qg2, kg2):
    t = pa.shape[0]
    tq = _tile(t, 512, LANES)
    scale = HEAD_DIM ** -0.5

    def body(pa_ref, cs_ref, sn_ref, qg_ref, kg_ref, q_ref, qT_ref, k_ref, kT_ref, vT_ref):
        lane = lax.broadcasted_iota(jnp.int32, (tq, LANES), 1)
        c, s = cs_ref[...], sn_ref[...]

        def norm_rope(xg, g2):
            r = lax.rsqrt(_head_sums(xg * xg, lane) * (1.0 / HEAD_DIM) + EPS)
            xn = xg * r * g2
            return xn * c + _swap16(xn, lane) * s

        for gi in range(4):
            sl = slice(gi * LANES, (gi + 1) * LANES)
            qr = norm_rope(pa_ref[:, sl].astype(F32), qg_ref[...]) * scale
            q_ref[:, sl] = qr.astype(BF16)
            qT_ref[sl, :] = qr.T.astype(BF16)
        kr = norm_rope(pa_ref[:, 512:640].astype(F32), kg_ref[...])
        kT_ref[...] = kr.T.astype(BF16)
        vT_ref[...] = pa_ref[:, 640:768].astype(F32).T.astype(BF16)
        kr = kr.astype(BF16)
        for kvh in range(A_KV_HEADS):
            k_ref[kvh] = kr[:, kvh * HEAD_DIM:(kvh + 1) * HEAD_DIM]

    row = lambda w: pl.BlockSpec((tq, w), lambda i: (i, 0))
    col = lambda r: pl.BlockSpec((r, tq), lambda i: (0, i))
    vec = pl.BlockSpec((1, LANES), lambda i: (0, 0))
    hm = pl.BlockSpec((A_KV_HEADS, tq, HEAD_DIM), lambda i: (0, i, 0))
    return pl.pallas_call(
        body, name="attn_prep_fwd", grid=(t // tq,),
        in_specs=[row(W_A), row(LANES), row(LANES), vec, vec],
        out_specs=(row(BW), col(BW), hm, col(LANES), col(LANES)),
        out_shape=(SDS((t, BW), BF16), SDS((BW, t), BF16), SDS((A_KV_HEADS, t, HEAD_DIM), BF16),
                   SDS((LANES, t), BF16), SDS((LANES, t), BF16)),
        compiler_params=_params("parallel"),
    )(pa, cs, sn, qg2, kg2)


def _attn_fwd(qT, k, vT, pa, exchange=None):
    t = qT.shape[1]
    tq = _tile(t, 256, LANES)
    grp = A_HEADS // A_KV_HEADS

    def body(qT_ref, k_ref, vT_ref, pa_ref, o_ref, lse_ref, y_ref):
        def scores(h):
            return _dot(k_ref[h // grp], qT_ref[h * HEAD_DIM:(h + 1) * HEAD_DIM, :], NN)

        sT, o_pair = scores(0), []
        for h in range(A_HEADS):
            kvh = h // grp
            sT_next = scores(h + 1) if h + 1 < A_HEADS else None
            m = jnp.max(sT, axis=0, keepdims=True)
            pT = jnp.exp(sT - m)
            l = jnp.sum(pT, axis=0, keepdims=True)
            oT = _dot(vT_ref[kvh * HEAD_DIM:(kvh + 1) * HEAD_DIM, :], pT.astype(BF16), NN)
            o_pair.append(oT / l)
            lse_ref[h:h + 1, :] = m + jnp.log(l)
            sT = sT_next
            if h % 2:
                o2 = jnp.concatenate(o_pair, axis=0).T
                o_pair = []
                cols = slice((h // 2) * LANES, (h // 2 + 1) * LANES)
                o_ref[:, cols] = o2
                z = pa_ref[:, 768 + (h // 2) * LANES:768 + (h // 2 + 1) * LANES].astype(F32)
                y_ref[:, cols] = (o2 * (z * _sigmoid(z))).astype(BF16)

    row = lambda w: pl.BlockSpec((tq, w), lambda i: (i, 0))
    return _host_call(
        body, (qT, k, vT, pa),
        [pl.BlockSpec((BW, tq), lambda i: (0, i)), pl.BlockSpec((A_KV_HEADS, t, HEAD_DIM), lambda i: (0, 0, 0)),
         pl.BlockSpec((A_KV_HEADS * HEAD_DIM, t), lambda i: (0, 0)), row(W_A)],
        (row(BW), pl.BlockSpec((A_HEADS, tq), lambda i: (0, i)), row(BW)),
        (SDS((t, BW), F32), SDS((A_HEADS, t), F32), SDS((t, BW), BF16)),
        name="attn_fwd", grid=(t // tq,), semantics=("parallel",), exchange=exchange)


def _attn_bwd(q, qT, k, kT, vT, pa, o, lse, dy, exchange=None):
    t = q.shape[0]
    tq = _tile(t, 256, LANES)
    grp = A_HEADS // A_KV_HEADS
    gw = grp * HEAD_DIM

    def body(q_ref, qT_ref, k_ref, kT_ref, vT_ref, z_ref, o_ref, lse_ref, dy_ref, dq_ref, dkT_ref, dvT_ref):
        @pl.when(pl.program_id(1) == 0)
        def _():
            dkT_ref[...] = jnp.zeros_like(dkT_ref)
            dvT_ref[...] = jnp.zeros_like(dvT_ref)

        z = z_ref[...].astype(F32)
        do = dy_ref[...] * (z * _sigmoid(z))
        doo = do * o_ref[...]
        doT = do.T
        kk, kT, vT = k_ref[...], kT_ref[...], vT_ref[...]
        for j in range(grp):
            hs = slice(j * HEAD_DIM, (j + 1) * HEAD_DIM)
            delta = jnp.sum(doo[:, hs], axis=-1, keepdims=True)
            s = _dot(q_ref[:, hs], kT, NN)
            p = jnp.exp(s - lse_ref[:, j:j + 1])
            dp = _dot(do[:, hs].astype(BF16), vT, NN)
            ds_b = (p * (dp - delta)).astype(BF16)
            dq_ref[:, hs] = _dot(ds_b, kk, NN)
            dkT_ref[...] += _dot(qT_ref[hs, :], ds_b, NN)
            dvT_ref[...] += _dot(doT[hs, :].astype(BF16), p.astype(BF16), NN)

    grp_blk = pl.BlockSpec((tq, gw), lambda g, i: (i, g))
    kvT_blk = pl.BlockSpec((HEAD_DIM, t), lambda g, i: (g, 0))
    acc_blk = pl.BlockSpec((None, HEAD_DIM, t), lambda g, i: (g, 0, 0))
    return _host_call(
        body, (q, qT, k, kT, vT, pa, o, lse, dy),
        [grp_blk, pl.BlockSpec((gw, tq), lambda g, i: (g, i)), pl.BlockSpec((None, t, HEAD_DIM), lambda g, i: (g, 0, 0)),
         kvT_blk, kvT_blk, pl.BlockSpec((tq, gw), lambda g, i: (i, 768 // gw + g)), grp_blk,
         pl.BlockSpec((None, tq, grp), lambda g, i: (g, i, 0)), grp_blk],
        (grp_blk, acc_blk, acc_blk),
        (SDS((t, BW), F32), SDS((A_KV_HEADS, HEAD_DIM, t), F32), SDS((A_KV_HEADS, HEAD_DIM, t), F32)),
        name="attn_bwd", grid=(A_KV_HEADS, t // tq), semantics=("arbitrary", "arbitrary"), exchange=exchange)


def _attn_prep_bwd(pa, cs, sn, qg2, kg2, dq, dkT, dvT, dy, o):
    t = pa.shape[0]
    tq = _tile(t, 512, LANES)
    scale = HEAD_DIM ** -0.5

    def body(pa_ref, cs_ref, sn_ref, qg_ref, kg_ref, dq_ref, dkT_ref, dvT_ref, dy_ref, o_ref, dpa_ref, dqg_ref, dkg_ref):
        lane = lax.broadcasted_iota(jnp.int32, (tq, LANES), 1)
        c, s = cs_ref[...], sn_ref[...]

        @pl.when(pl.program_id(0) == 0)
        def _():
            dqg_ref[...] = jnp.zeros_like(dqg_ref)
            dkg_ref[...] = jnp.zeros_like(dkg_ref)

        def norm_rope_bwd(xg, g2, dout):
            r = lax.rsqrt(_head_sums(xg * xg, lane) * (1.0 / HEAD_DIM) + EPS)
            xh = xg * r
            dxn = dout * c + _swap16(dout * s, lane)
            gd = dxn * g2
            dx = r * (gd - xh * (_head_sums(gd * xh, lane) * (1.0 / HEAD_DIM)))
            return dx, jnp.sum(dxn * xh, axis=0, keepdims=True)

        for gi in range(4):
            sl = slice(gi * LANES, (gi + 1) * LANES)
            dx, dg = norm_rope_bwd(pa_ref[:, sl].astype(F32), qg_ref[...], dq_ref[:, sl] * scale)
            dpa_ref[:, sl] = dx.astype(BF16)
            dqg_ref[...] += dg
        dx, dg = norm_rope_bwd(pa_ref[:, 512:640].astype(F32), kg_ref[...], dkT_ref[...].T)
        dpa_ref[:, 512:640] = dx.astype(BF16)
        dkg_ref[...] += dg
        dpa_ref[:, 640:768] = dvT_ref[...].T.astype(BF16)
        z = pa_ref[:, 768:1280].astype(F32)
        _, dsilu = _silu_and_grad(z)
        dpa_ref[:, 768:1280] = (dy_ref[...] * o_ref[...] * dsilu).astype(BF16)

    row = lambda w: pl.BlockSpec((tq, w), lambda i: (i, 0))
    col = pl.BlockSpec((LANES, tq), lambda i: (0, i))
    vec = pl.BlockSpec((1, LANES), lambda i: (0, 0))
    return pl.pallas_call(
        body, name="attn_prep_bwd", grid=(t // tq,),
        in_specs=[row(W_A), row(LANES), row(LANES), vec, vec, row(BW), col, col, row(BW), row(BW)],
        out_specs=(row(W_A), vec, vec),
        out_shape=(SDS((t, W_A), BF16), SDS((1, LANES), F32), SDS((1, LANES), F32)),
        compiler_params=_params("arbitrary"),
    )(pa, cs, sn, qg2, kg2, dq, dkT, dvT, dy, o)


def _layer_norm(v, g, b):
    mu = jnp.mean(v, axis=-1, keepdims=True)
    xc = v - mu
    rs = lax.rsqrt(jnp.mean(xc * xc, axis=-1, keepdims=True) + EPS)
    xh = xc * rs
    return xh * g + b, xh, rs


def _gmlp_fwd(pb, lng, lnb, ws, bs):
    t = pb.shape[0]
    tb = _tile(t, 512, CHUNK)

    def body(pb_ref, g_ref, b_ref, ws_ref, bs_ref, y_ref):
        vln, _, _ = _layer_norm(pb_ref[:, BW:2 * BW].astype(F32), g_ref[...], b_ref[...])
        vb = vln.astype(BF16)
        for gi in range(B_GROUPS):
            w = ws_ref[gi].astype(BF16)
            cs_ = slice(gi * CHUNK, (gi + 1) * CHUNK)
            for n in range(tb // CHUNK):
                rs_ = slice(n * CHUNK, (n + 1) * CHUNK)
                mixed = _dot(w, vb[rs_, cs_], NN) + bs_ref[gi]
                z = pb_ref[rs_, 2 * BW + gi * CHUNK:2 * BW + (gi + 1) * CHUNK].astype(F32)
                y_ref[rs_, cs_] = (pb_ref[rs_, cs_].astype(F32) * mixed * (z * _sigmoid(z))).astype(BF16)

    return pl.pallas_call(
        body, name="gmlp_fwd", grid=(t // tb,),
        in_specs=[pl.BlockSpec((tb, W_B), lambda i: (i, 0)),
                  pl.BlockSpec((1, BW), lambda i: (0, 0)), pl.BlockSpec((1, BW), lambda i: (0, 0)),
                  pl.BlockSpec((B_GROUPS, CHUNK, CHUNK), lambda i: (0, 0, 0)),
                  pl.BlockSpec((B_GROUPS, CHUNK, 1), lambda i: (0, 0, 0))],
        out_specs=pl.BlockSpec((tb, BW), lambda i: (i, 0)),
        out_shape=SDS((t, BW), BF16), compiler_params=_params("parallel"),
    )(pb, lng, lnb, ws, bs)


def _gmlp_bwd(pb, lng, lnb, ws, bs, dy):
    t = pb.shape[0]
    tb = _tile(t, 256, CHUNK)

    def body(pb_ref, g_ref, b_ref, ws_ref, bs_ref, dy_ref, dpb_ref, dws_ref, dbs_ref, dg_ref, db_ref, dvln_ref):
        @pl.when(pl.program_id(0) == 0)
        def _():
            dws_ref[...] = jnp.zeros_like(dws_ref)
            dbs_ref[...] = jnp.zeros_like(dbs_ref)
            dg_ref[...] = jnp.zeros_like(dg_ref)
            db_ref[...] = jnp.zeros_like(db_ref)

        vln, xh, rs = _layer_norm(pb_ref[:, BW:2 * BW].astype(F32), g_ref[...], b_ref[...])
        vb = vln.astype(BF16)
        for gi in range(B_GROUPS):
            w = ws_ref[gi].astype(BF16)
            cs_ = slice(gi * CHUNK, (gi + 1) * CHUNK)
            for n in range(tb // CHUNK):
                rs_ = slice(n * CHUNK, (n + 1) * CHUNK)
                vbc = vb[rs_, cs_]
                mixed = _dot(w, vbc, NN) + bs_ref[gi]
                zs = slice(2 * BW + gi * CHUNK, 2 * BW + (gi + 1) * CHUNK)
                z = pb_ref[rs_, zs].astype(F32)
                u = pb_ref[rs_, cs_].astype(F32)
                sil, dsil = _silu_and_grad(z)
                dyc = dy_ref[rs_, cs_]
                dmixed = dyc * u * sil
                dpb_ref[rs_, cs_] = (dyc * mixed * sil).astype(BF16)
                dpb_ref[rs_, zs] = (dyc * u * mixed * dsil).astype(BF16)
                dmb = dmixed.astype(BF16)
                dws_ref[gi] += _dot(dmb, vbc, NT)
                dbs_ref[gi] += jnp.sum(dmixed, axis=-1, keepdims=True)
                dvln_ref[rs_, cs_] = _dot(w, dmb, TN)
        dvln = dvln_ref[...]
        dg_ref[...] += jnp.sum(dvln * xh, axis=0, keepdims=True)
        db_ref[...] += jnp.sum(dvln, axis=0, keepdims=True)
        gd = dvln * g_ref[...]
        dv = rs * (gd - jnp.mean(gd, axis=-1, keepdims=True) - xh * jnp.mean(gd * xh, axis=-1, keepdims=True))
        dpb_ref[:, BW:2 * BW] = dv.astype(BF16)

    vec = pl.BlockSpec((1, BW), lambda i: (0, 0))
    wsb = pl.BlockSpec((B_GROUPS, CHUNK, CHUNK), lambda i: (0, 0, 0))
    bsb = pl.BlockSpec((B_GROUPS, CHUNK, 1), lambda i: (0, 0, 0))
    return pl.pallas_call(
        body, name="gmlp_bwd", grid=(t // tb,),
        in_specs=[pl.BlockSpec((tb, W_B), lambda i: (i, 0)), vec, vec, wsb, bsb, pl.BlockSpec((tb, BW), lambda i: (i, 0))],
        out_specs=(pl.BlockSpec((tb, W_B), lambda i: (i, 0)), wsb, bsb, vec, vec),
        out_shape=(SDS((t, W_B), BF16), SDS((B_GROUPS, CHUNK, CHUNK), F32), SDS((B_GROUPS, CHUNK, 1), F32),
                   SDS((1, BW), F32), SDS((1, BW), F32)),
        scratch_shapes=[pltpu.VMEM((tb, BW), F32)],
        compiler_params=_params("arbitrary"),
    )(pb, lng, lnb, ws, bs, dy)


def _mem_softmax(qh, kh):
    s = _dot(qh, kh, NT) * (M_HEAD_DIM ** -0.5)
    e = jnp.exp(s - jnp.max(s, axis=-1, keepdims=True))
    return e / jnp.sum(e, axis=-1, keepdims=True)


def _mem_attn_fwd(pm, kv):
    t = pm.shape[0]
    tq = _tile(t, 512, 8)
    ml = kv.shape[0]

    def body(pm_ref, kv_ref, y_ref):
        for h in range(M_HEADS):
            hs = slice(h * M_HEAD_DIM, (h + 1) * M_HEAD_DIM)
            kh = kv_ref[:, hs].astype(BF16)
            vh = kv_ref[:, BW + h * M_HEAD_DIM:BW + (h + 1) * M_HEAD_DIM].astype(BF16)
            p = _mem_softmax(pm_ref[:, hs].astype(BF16), kh)
            o = _dot(p.astype(BF16), vh, NN)
            z = pm_ref[:, BW + h * M_HEAD_DIM:BW + (h + 1) * M_HEAD_DIM].astype(F32)
            y_ref[:, hs] = (o * (z * _sigmoid(z))).astype(BF16)

    return pl.pallas_call(
        body, name="mem_attn_fwd", grid=(t // tq,),
        in_specs=[pl.BlockSpec((tq, W_M), lambda i: (i, 0)), pl.BlockSpec((ml, 2 * BW), lambda i: (0, 0))],
        out_specs=pl.BlockSpec((tq, BW), lambda i: (i, 0)),
        out_shape=SDS((t, BW), BF16), compiler_params=_params("parallel"),
    )(pm, kv)


def _mem_attn_bwd(pm, kv, dy):
    t = pm.shape[0]
    tq = _tile(t, 512, 8)
    ml = kv.shape[0]
    scale = M_HEAD_DIM ** -0.5

    def body(pm_ref, kv_ref, dy_ref, dpm_ref, dkv_ref):
        @pl.when(pl.program_id(0) == 0)
        def _():
            dkv_ref[...] = jnp.zeros_like(dkv_ref)

        for h in range(M_HEADS):
            hs = slice(h * M_HEAD_DIM, (h + 1) * M_HEAD_DIM)
            zs = slice(BW + h * M_HEAD_DIM, BW + (h + 1) * M_HEAD_DIM)
            kh = kv_ref[:, hs].astype(BF16)
            vh = kv_ref[:, zs].astype(BF16)
            qh = pm_ref[:, hs].astype(BF16)
            p = _mem_softmax(qh, kh)
            pb = p.astype(BF16)
            o = _dot(pb, vh, NN)
            sil, dsil = _silu_and_grad(pm_ref[:, zs].astype(F32))
            dyh = dy_ref[:, hs]
            do = dyh * sil
            dpm_ref[:, zs] = (dyh * o * dsil).astype(BF16)
            delta = jnp.sum(do * o, axis=-1, keepdims=True)
            do_b = do.astype(BF16)
            dp = _dot(do_b, vh, NT)
            dr_b = (p * (dp - delta) * scale).astype(BF16)
            dpm_ref[:, hs] = _dot(dr_b, kh, NN).astype(BF16)
            dkv_ref[:, hs] += _dot(dr_b, qh, TN)
            dkv_ref[:, zs] += _dot(pb, do_b, TN)

    kvb = pl.BlockSpec((ml, 2 * BW), lambda i: (0, 0))
    return pl.pallas_call(
        body, name="mem_attn_bwd", grid=(t // tq,),
        in_specs=[pl.BlockSpec((tq, W_M), lambda i: (i, 0)), kvb, pl.BlockSpec((tq, BW), lambda i: (i, 0))],
        out_specs=(pl.BlockSpec((tq, W_M), lambda i: (i, 0)), kvb),
        out_shape=(SDS((t, W_M), BF16), SDS((ml, 2 * BW), F32)),
        compiler_params=_params("arbitrary"),
    )(pm, kv, dy)


def _merge_fwd(ya, yb, ym, pg, wbr, exchange=None):
    t = ya.shape[0]
    tm = _tile(t, 512, 8)

    def body(ya_ref, yb_ref, ym_ref, pg_ref, w_ref, m_ref):
        acc = None
        for n, y_ref in enumerate((ya_ref, yb_ref, ym_ref)):
            up = _dot(y_ref[...], w_ref[n], NN)
            term = _sigmoid(pg_ref[:, n * D_MODEL:(n + 1) * D_MODEL].astype(F32)) * up
            acc = term if acc is None else acc + term
        m_ref[...] = acc.astype(BF16)

    yb_spec = pl.BlockSpec((tm, BW), lambda i: (i, 0))
    return _host_call(
        body, (ya, yb, ym, pg, wbr),
        [yb_spec, yb_spec, yb_spec, pl.BlockSpec((tm, W_G), lambda i: (i, 0)),
         pl.BlockSpec((3, BW, D_MODEL), lambda i: (0, 0, 0))],
        (pl.BlockSpec((tm, D_MODEL), lambda i: (i, 0)),), (SDS((t, D_MODEL), BF16),),
        name="merge_fwd", grid=(t // tm,), semantics=("parallel",), exchange=exchange)


def _merge_bwd(ya, yb, ym, pg, wbr, dm, exchange=None):
    t = ya.shape[0]
    tm = _tile(t, 256, 8)

    def body(ya_ref, yb_ref, ym_ref, pg_ref, w_ref, dm_ref, dya_ref, dyb_ref, dym_ref, dpg_ref, dw_ref):
        @pl.when(pl.program_id(0) == 0)
        def _():
            dw_ref[...] = jnp.zeros_like(dw_ref)

        dmf = dm_ref[...]
        for n, (y_ref, dy_ref) in enumerate(((ya_ref, dya_ref), (yb_ref, dyb_ref), (ym_ref, dym_ref))):
            cs_ = slice(n * D_MODEL, (n + 1) * D_MODEL)
            y = y_ref[...]
            w = w_ref[n]
            up = _dot(y, w, NN)
            gt = _sigmoid(pg_ref[:, cs_].astype(F32))
            dpg_ref[:, cs_] = (dmf * up * gt * (1.0 - gt)).astype(BF16)
            dup = (dmf * gt).astype(BF16)
            dy_ref[...] = _dot(dup, w, NT)
            dw_ref[n] += _dot(y, dup, TN)

    y_spec = pl.BlockSpec((tm, BW), lambda i: (i, 0))
    w_spec = pl.BlockSpec((3, BW, D_MODEL), lambda i: (0, 0, 0))
    return _host_call(
        body, (ya, yb, ym, pg, wbr, dm),
        [y_spec, y_spec, y_spec, pl.BlockSpec((tm, W_G), lambda i: (i, 0)), w_spec,
         pl.BlockSpec((tm, D_MODEL), lambda i: (i, 0))],
        (y_spec, y_spec, y_spec, pl.BlockSpec((tm, W_G), lambda i: (i, 0)), w_spec),
        (SDS((t, BW), F32), SDS((t, BW), F32), SDS((t, BW), F32), SDS((t, W_G), BF16), SDS((3, BW, D_MODEL), F32)),
        name="merge_bwd", grid=(t // tm,), semantics=("arbitrary",), exchange=exchange)


def _loss_head(x, g, target):
    t, d = x.shape
    tm = _tile(t, 512, 8)

    def body(x_ref, g_ref, t_ref, loss_ref, dx_ref, dg_ref):
        @pl.when(pl.program_id(0) == 0)
        def _():
            loss_ref[...] = jnp.zeros_like(loss_ref)
            dg_ref[...] = jnp.zeros_like(dg_ref)

        xf = x_ref[...]
        r = lax.rsqrt(jnp.mean(xf * xf, axis=-1, keepdims=True) + EPS)
        xh = xf * r
        err = xh * g_ref[...] - t_ref[...]
        per_tok = jnp.mean(err * err, axis=-1, keepdims=True)
        loss_ref[...] += 0.5 * jnp.sum(per_tok, axis=0, keepdims=True)
        dy = err * (1.0 / d)
        gd = dy * g_ref[...]
        dx_ref[...] = r * (gd - xh * jnp.mean(gd * xh, axis=-1, keepdims=True))
        dg_ref[...] += jnp.sum(dy * xh, axis=0, keepdims=True)

    row = pl.BlockSpec((tm, d), lambda i: (i, 0))
    vec = pl.BlockSpec((1, d), lambda i: (0, 0))
    return pl.pallas_call(
        body, name="loss_head", grid=(t // tm,),
        in_specs=[row, vec, row],
        out_specs=(pl.BlockSpec((1, 1), lambda i: (0, 0)), row, vec),
        out_shape=(SDS((1, 1), F32), SDS((t, d), F32), SDS((1, d), F32)),
        compiler_params=_params("arbitrary"),
    )(x, g, target)


def _layer_fwd(x, mem, w, tabs, next_shards=None):
    cs, sn = tabs
    riding = next_shards is not None
    h = _rmsnorm_fwd(x, w["norm_g"], "rmsnorm_fwd")
    pa, pb, pm, pg = [_mm(h, w["w_" + nm], "nn", "proj_" + nm, out_dtype=BF16, tm_cap=1024) for nm in "abmg"]
    q, qT, k, kT, vT = _attn_prep_fwd(pa, cs, sn, w["qg2"], w["kg2"])
    (o, lse, ya), gathered = _attn_fwd(qT, k, vT, pa, exchange=_gather_first_hop(next_shards) if riding else None)
    yb = _gmlp_fwd(pb, w["ln_g"], w["ln_b"], w["w_s"], w["b_s"])
    memn = _rmsnorm_fwd(mem, w["mem_g"], "mem_rmsnorm_fwd")
    kv = _mm(memn, w["w_kv"], "nn", "mem_kv")
    ym = _mem_attn_fwd(pm, kv)
    (merged,), gathered = _merge_fwd(ya, yb, ym, pg, w["w_br"], exchange=_gather_forward(gathered) if riding else None)
    x_next = _mm(merged, w["w_out"], "nn", "out_proj", add=x)
    saved = dict(x=x, h=h, pa=pa, pb=pb, pm=pm, pg=pg, q=q, qT=qT, k=k, kT=kT, vT=vT, o=o, lse=lse, ya=ya, yb=yb, ym=ym,
                 memn=memn, kv=kv, merged=merged)
    return x_next, saved, gathered


def _layer_bwd(dx_out, mem, w, s, tabs, pending=None, core=None):
    cs, sn = tabs
    t = dx_out.shape[0]
    riding = pending is not None
    dmerged = _mm(dx_out, w["w_out"], "nt", "d_merged")
    shard_rows = D_MODEL // N_DEV
    d_w_out = _mm_tn_by_owner(s["merged"], dx_out, "d_w_out")
    (dya, dyb, dym, dpg, d_w_br), recv = _merge_bwd(s["ya"], s["yb"], s["ym"], s["pg"], w["w_br"], dmerged,
                                                    exchange=_scatter_to_sibling(pending) if riding else None)
    parts = _pair_sums(pending, recv, core) if riding else None
    grp = A_HEADS // A_KV_HEADS
    lse_cols = s["lse"].reshape(A_KV_HEADS, grp, t).transpose(0, 2, 1)
    (dq, dkT, dvT), from_chips = _attn_bwd(s["q"], s["qT"], s["k"], s["kT"], s["vT"], s["pa"], s["o"], lse_cols, dya,
                                           exchange=_scatter_to_chips(parts) if riding else None)
    dpa, d_qg2, d_kg2 = _attn_prep_bwd(s["pa"], cs, sn, w["qg2"], w["kg2"], dq, dkT.reshape(-1, t), dvT.reshape(-1, t),
                                       dya, s["o"])
    dpb, d_w_s, d_b_s, d_ln_g, d_ln_b = _gmlp_bwd(s["pb"], w["ln_g"], w["ln_b"], w["w_s"], w["b_s"], dyb)
    dpm, dkv = _mem_attn_bwd(s["pm"], s["kv"], dym)
    d_w_kv = _mm_tn_by_owner(s["memn"], dkv, "d_w_kv")
    dmemn = _mm(dkv, w["w_kv"], "nt", "d_memn")
    _, d_mem_g = _rmsnorm_bwd(mem, w["mem_g"], dmemn, None, "mem_rmsnorm_bwd")
    dh = None
    for nm, dp in (("a", dpa), ("b", dpb), ("m", dpm), ("g", dpg)):
        dh = _mm(dp, w["w_" + nm], "nt", "dh_" + nm, add=dh, tk_cap=W_G)
    d_w_in = _split_w_in(*[_mm(s["h"], dp, "tn", "d_w_" + nm, out_dtype=BF16, tm_cap=1024, tk_cap=2048)
                           for nm, dp in (("a", dpa), ("b", dpb), ("m", dpm), ("g", dpg))])
    dx_in, d_norm_g = _rmsnorm_bwd(s["x"], w["norm_g"], dh, dx_out, "rmsnorm_bwd")
    grads = dict(w_in=d_w_in[:, None], w_mem_kv=d_w_kv.reshape(N_DEV, 1, shard_rows, -1), w_br=_split_w_br(d_w_br),
                 w_out=d_w_out.reshape(N_DEV, 1, shard_rows, -1),
                 norm_g=d_norm_g[0], q_norm_g=d_qg2[0, :HEAD_DIM] + d_qg2[0, HEAD_DIM:],
                 k_norm_g=d_kg2[0, :HEAD_DIM] + d_kg2[0, HEAD_DIM:], sg_ln_g=d_ln_g[0], sg_ln_b=d_ln_b[0],
                 w_s=d_w_s, b_s=d_b_s[:, :, 0], mem_norm_g=d_mem_g[0])
    return dx_in, grads, ((parts, from_chips) if riding else None)


def _layer_weights(l, w_groups, w_kv, w_br, w_out, small):
    tile2 = lambda g: jnp.tile(g.reshape(1, -1), (1, 2))
    w_a, w_b, w_m, w_g = w_groups
    return dict(
        w_a=w_a, w_b=w_b, w_m=w_m, w_g=w_g, w_kv=w_kv, w_br=w_br, w_out=w_out,
        norm_g=small["norm_g"][l].reshape(1, -1), qg2=tile2(small["q_norm_g"][l]), kg2=tile2(small["k_norm_g"][l]),
        ln_g=small["sg_ln_g"][l].reshape(1, -1), ln_b=small["sg_ln_b"][l].reshape(1, -1),
        w_s=small["w_s"][l], b_s=small["b_s"][l][:, :, None], mem_g=small["mem_norm_g"][l].reshape(1, -1))


def _position():
    x, y, c = lax.axis_index("x"), lax.axis_index("y"), lax.axis_index("c")
    return x, y, c, [(1 - x, y), (x, 1 - y), (1 - x, 1 - y)]


def _gather_first_hop(shards):
    n = len(shards)

    def make(x_refs, out_refs, sems):
        send_sems, recv_sems, local_sems = sems
        x, y, c, chips = _position()
        me = 4 * x + 2 * y + c
        peers = [(x, y, 1 - c)] + [(cx, cy, c) for cx, cy in chips]
        copies = [pltpu.make_async_copy(x_refs[t], out_refs[t].at[me], local_sems.at[t]) for t in range(n)]
        copies += [pltpu.make_async_remote_copy(
            src_ref=x_refs[t], dst_ref=out_refs[t].at[me], send_sem=send_sems.at[t, k], recv_sem=recv_sems.at[t, k],
            device_id=peer, device_id_type=MESH_ID) for t in range(n) for k, peer in enumerate(peers)]
        return copies

    return _Exchange(shards, [SDS((N_DEV,) + a.shape, a.dtype) for a in shards],
                     [pltpu.SemaphoreType.DMA((n, 4)), pltpu.SemaphoreType.DMA((n, 4)), pltpu.SemaphoreType.DMA((n,))], make)


def _gather_forward(gathered):
    n = len(gathered)

    def make(in_refs, out_refs, sems):
        send_sems, recv_sems = sems
        x, y, c, chips = _position()
        return [pltpu.make_async_remote_copy(
            src_ref=in_refs[t].at[4 * cx + 2 * cy + c], dst_ref=out_refs[t].at[4 * cx + 2 * cy + c],
            send_sem=send_sems.at[t, j], recv_sem=recv_sems.at[t, j], device_id=(x, y, 1 - c), device_id_type=MESH_ID)
            for t in range(n) for j, (cx, cy) in enumerate(chips)]

    return _Exchange(gathered, [SDS(a.shape, a.dtype) for a in gathered],
                     [pltpu.SemaphoreType.DMA((n, 3)), pltpu.SemaphoreType.DMA((n, 3))], make,
                     aliases={t: t for t in range(n)})


def _all_gather(shards, name):
    return _exchange_call(_gather_forward(_exchange_call(_gather_first_hop(shards), name + "_hop1")), name + "_hop2")


def _shard_segments(d):
    shard = IN_WIDTH // N_DEV
    lo, hi = d * shard, (d + 1) * shard
    out = []
    for g in range(4):
        a, b = max(lo, GROUP_OFFS[g]), min(hi, GROUP_OFFS[g + 1])
        if a < b:
            out.append((g, a - GROUP_OFFS[g], b - GROUP_OFFS[g], a - lo))
    return out


def _assemble_w_in(gathered):
    _, rows, shard = gathered.shape
    tr = _tile(rows, 256, 16)
    widths = [GROUP_OFFS[g + 1] - GROUP_OFFS[g] for g in range(4)]

    def body(g_ref, *outs):
        for d in range(N_DEV):
            for g, lo, hi, off in _shard_segments(d):
                outs[g][:, lo:hi] = g_ref[d, :, off:off + hi - lo]

    return pl.pallas_call(
        body, name="assemble_w_in", grid=(rows // tr,),
        in_specs=[pl.BlockSpec((N_DEV, tr, shard), lambda i: (0, i, 0))],
        out_specs=tuple(pl.BlockSpec((tr, w), lambda i: (i, 0)) for w in widths),
        out_shape=tuple(SDS((rows, w), gathered.dtype) for w in widths), compiler_params=_params("parallel"),
    )(gathered)


def _split_w_in(da, db, dm, dg):
    rows = da.shape[0]
    shard = IN_WIDTH // N_DEV
    tr = _tile(rows, 256, 16)
    widths = [GROUP_OFFS[g + 1] - GROUP_OFFS[g] for g in range(4)]

    def body(a_ref, b_ref, m_ref, g_ref, o_ref):
        ins = (a_ref, b_ref, m_ref, g_ref)
        for d in range(N_DEV):
            for g, lo, hi, off in _shard_segments(d):
                o_ref[_core_major(d), :, off:off + hi - lo] = ins[g][:, lo:hi]

    return pl.pallas_call(
        body, name="split_w_in", grid=(rows // tr,),
        in_specs=[pl.BlockSpec((tr, w), lambda i: (i, 0)) for w in widths],
        out_specs=pl.BlockSpec((N_DEV, tr, shard), lambda i: (0, i, 0)),
        out_shape=SDS((N_DEV, rows, shard), da.dtype), compiler_params=_params("parallel"),
    )(da, db, dm, dg)


def _assemble_w_br(gathered):
    _, nb, rows, shard = gathered.shape

    def body(g_ref, o_ref):
        o_ref[...] = g_ref[...]

    return pl.pallas_call(
        body, name="assemble_w_br", grid=(N_DEV,),
        in_specs=[pl.BlockSpec((None, nb, rows, shard), lambda d: (d, 0, 0, 0))],
        out_specs=pl.BlockSpec((nb, rows, shard), lambda d: (0, 0, d)),
        out_shape=SDS((nb, rows, N_DEV * shard), gathered.dtype), compiler_params=_params("parallel"),
    )(gathered)


def _split_w_br(dw):
    nb, rows, cols = dw.shape
    shard = cols // N_DEV

    def body(d_ref, o_ref):
        o_ref[...] = d_ref[...].astype(BF16)

    return pl.pallas_call(
        body, name="split_w_br", grid=(N_DEV,),
        in_specs=[pl.BlockSpec((nb, rows, shard), lambda d: (0, 0, d))],
        out_specs=pl.BlockSpec((None, nb, rows, shard), lambda d: (_core_major(d), 0, 0, 0)),
        out_shape=SDS((N_DEV, nb, rows, shard), BF16), compiler_params=_params("parallel"),
    )(dw)


def _scatter_to_sibling(dests):
    n = len(dests)

    def make(d_refs, recv_refs, sems):
        send_sems, recv_sems = sems
        x, y, c, _ = _position()
        return [pltpu.make_async_remote_copy(
            src_ref=d_refs[t].at[pl.ds((1 - c) * 4, 4)], dst_ref=recv_refs[t], send_sem=send_sems.at[t],
            recv_sem=recv_sems.at[t], device_id=(x, y, 1 - c), device_id_type=MESH_ID) for t in range(n)]

    return _Exchange(dests, [SDS((4,) + a.shape[1:], a.dtype) for a in dests],
                     [pltpu.SemaphoreType.DMA((n,)), pltpu.SemaphoreType.DMA((n,))], make)


def _scatter_to_chips(parts):
    n = len(parts)

    def make(p_refs, recv_refs, sems):
        send_sems, recv_sems = sems
        _, _, c, chips = _position()
        return [pltpu.make_async_remote_copy(
            src_ref=p_refs[t].at[2 * cx + cy], dst_ref=recv_refs[t].at[k], send_sem=send_sems.at[t, k],
            recv_sem=recv_sems.at[t, k], device_id=(cx, cy, c), device_id_type=MESH_ID)
            for t in range(n) for k, (cx, cy) in enumerate(chips)]

    return _Exchange(parts, [SDS((3,) + a.shape[1:], a.dtype) for a in parts],
                     [pltpu.SemaphoreType.DMA((n, 3)), pltpu.SemaphoreType.DMA((n, 3))], make)


def _pair_sums(dests, recv, core):
    return [_pair_sum(d, r, core, "rs_pair_sum_" + n) for n, d, r in zip(BIG, dests, recv)]


def _pair_sum(dest, recv, core, name):
    _, na, r, cdim = dest.shape
    tr = _tile(r, 256, 16)

    def body(core_ref, a_ref, b_ref, o_ref):
        o_ref[...] = (a_ref[...].astype(F32) + b_ref[...].astype(F32)).astype(o_ref.dtype)

    blk = pl.BlockSpec((None, None, tr, cdim), lambda j, a, i, core_ref: (j, a, i, 0))
    return pl.pallas_call(
        body, name=name, out_shape=SDS(recv.shape, BF16),
        grid_spec=pltpu.PrefetchScalarGridSpec(
            num_scalar_prefetch=1, grid=(4, na, r // tr),
            in_specs=[pl.BlockSpec((None, None, tr, cdim), lambda j, a, i, core_ref: (core_ref[0] * 4 + j, a, i, 0)), blk],
            out_specs=blk),
        compiler_params=_params("parallel", "parallel", "parallel"),
    )(core, dest, recv)


def _adamw_math(w, g, m, v):
    m = ADAM_B1 * m + (1.0 - ADAM_B1) * g
    v = ADAM_B2 * v + (1.0 - ADAM_B2) * (g * g)
    m_hat = m / (1.0 - ADAM_B1 ** ADAM_STEP)
    v_hat = v / (1.0 - ADAM_B2 ** ADAM_STEP)
    delta = -ADAM_LR * (m_hat / (jnp.sqrt(v_hat) + ADAM_EPS) + ADAM_WD * w)
    return delta, m, v


def _sum_and_adamw(parts, w, m, v, name):
    n, r, ln = parts.shape
    tr = _tile(r, 512, 16)

    def body(p_ref, w_ref, m_ref, v_ref, g_out, d_out, m_out, v_out):
        g = p_ref[0].astype(F32)
        for j in range(1, n):
            g = g + p_ref[j].astype(F32)
        delta, nm, nv = _adamw_math(w_ref[...], g, m_ref[...], v_ref[...])
        g_out[...] = g
        d_out[...] = delta
        m_out[...] = nm
        v_out[...] = nv

    blk = pl.BlockSpec((tr, ln), lambda i: (i, 0))
    return pl.pallas_call(
        body, name=name, grid=(r // tr,),
        in_specs=[pl.BlockSpec((n, tr, ln), lambda i: (0, i, 0)), blk, blk, blk],
        out_specs=(blk, blk, blk, blk), out_shape=tuple(SDS((r, ln), F32) for _ in range(4)),
        compiler_params=_params("parallel"),
    )(parts, w, m, v)


BIG = ("w_in", "w_mem_kv", "w_br", "w_out")
SMALL = ("norm_g", "q_norm_g", "k_norm_g", "sg_ln_g", "sg_ln_b", "w_s", "b_s", "mem_norm_g", "final_g")


def _pack(arrs, row_unit=16):
    flat = jnp.concatenate([a.reshape(-1) for a in arrs])
    pad = (-flat.shape[0]) % (row_unit * LANES)
    if pad:
        flat = jnp.concatenate([flat, jnp.zeros((pad,), flat.dtype)])
    return flat.reshape(-1, LANES)


def _unpack(buf, shapes):
    flat = buf.reshape(-1)
    out, off = [], 0
    for shp in shapes:
        n = 1
        for s_ in shp:
            n *= s_
        out.append(flat[off:off + n].reshape(shp))
        off += n
    return out


def _shard_sum_adamw(part, from_chips, chip, w, m, v, layer, prev, name):
    _, na, r, cdim = part.shape
    tr = _tile(r, 256, 8)
    n_prev = 0 if prev is None else len(prev)

    def body(chip_ref, p_ref, f_ref, w_ref, m_ref, v_ref, *rest):
        g_out, d_out, m_out, v_out = rest[n_prev:]
        g = p_ref[...].astype(F32)
        for j in range(3):
            g = g + f_ref[j].astype(F32)
        delta, nm, nv = _adamw_math(w_ref[...], g, m_ref[...], v_ref[...])
        g_out[...] = g
        d_out[...] = delta
        m_out[...] = nm
        v_out[...] = nv

    blk = (None, None, tr, cdim)
    lay = pl.BlockSpec(blk, lambda a, i, chip_ref: (layer, a, i, 0))
    return pl.pallas_call(
        body, name=name, out_shape=tuple(SDS(w.shape, F32) for _ in range(4)),
        grid_spec=pltpu.PrefetchScalarGridSpec(
            num_scalar_prefetch=1, grid=(na, r // tr),
            in_specs=[pl.BlockSpec(blk, lambda a, i, chip_ref: (chip_ref[0], a, i, 0)),
                      pl.BlockSpec((3, None, tr, cdim), lambda a, i, chip_ref: (0, a, i, 0)), lay, lay, lay]
            + [HBM] * n_prev,
            out_specs=(lay, lay, lay, lay)),
        input_output_aliases={6 + j: j for j in range(n_prev)},
        compiler_params=_params("parallel", "parallel"),
    )(chip, part, from_chips, w, m, v, *(prev or ()))


def kernel(x, mem, norm_g, w_in, q_norm_g, k_norm_g, sg_ln_g, sg_ln_b, w_s, b_s, mem_norm_g, w_mem_kv, w_br, w_out, final_g, loss_target, m_norm_g, m_w_in, m_q_norm_g, m_k_norm_g, m_sg_ln_g, m_sg_ln_b, m_w_s, m_b_s, m_mem_norm_g, m_w_mem_kv, m_w_br, m_w_out, m_final_g, v_norm_g, v_w_in, v_q_norm_g, v_k_norm_g, v_sg_ln_g, v_sg_ln_b, v_w_s, v_b_s, v_mem_norm_g, v_w_mem_kv, v_w_br, v_w_out, v_final_g):
    wts = dict(norm_g=norm_g, w_in=w_in, q_norm_g=q_norm_g, k_norm_g=k_norm_g, sg_ln_g=sg_ln_g, sg_ln_b=sg_ln_b,
               w_s=w_s, b_s=b_s, mem_norm_g=mem_norm_g, w_mem_kv=w_mem_kv, w_br=w_br, w_out=w_out, final_g=final_g)
    mom1 = dict(norm_g=m_norm_g, w_in=m_w_in, q_norm_g=m_q_norm_g, k_norm_g=m_k_norm_g, sg_ln_g=m_sg_ln_g,
                sg_ln_b=m_sg_ln_b, w_s=m_w_s, b_s=m_b_s, mem_norm_g=m_mem_norm_g, w_mem_kv=m_w_mem_kv, w_br=m_w_br,
                w_out=m_w_out, final_g=m_final_g)
    mom2 = dict(norm_g=v_norm_g, w_in=v_w_in, q_norm_g=v_q_norm_g, k_norm_g=v_k_norm_g, sg_ln_g=v_sg_ln_g,
                sg_ln_b=v_sg_ln_b, w_s=v_w_s, b_s=v_b_s, mem_norm_g=v_mem_norm_g, w_mem_kv=v_w_mem_kv, w_br=v_w_br,
                w_out=v_w_out, final_g=v_final_g)
    dp = w_in.shape[0]
    core = lax.axis_index("c").astype(jnp.int32).reshape(1)
    chip = (2 * lax.axis_index("x") + lax.axis_index("y")).astype(jnp.int32).reshape(1)

    shard_bf = {n: wts[n].astype(BF16) for n in BIG}
    shards = lambda l: [shard_bf[n][l] for n in BIG]
    x_l, mem_l = x[0], mem[0]
    tabs = _rope_tables(x_l.shape[0])

    gathered = _all_gather(shards(0), "weights_all_gather")
    layers, saved = [], []
    for l in range(dp):
        g_in, g_kv, g_br, g_out = gathered
        layers.append(_layer_weights(l, _assemble_w_in(g_in), g_kv.reshape(D_MODEL, -1), _assemble_w_br(g_br),
                                     g_out.reshape(D_MODEL, -1), wts))
        x_l, s, gathered = _layer_fwd(x_l, mem_l, layers[l], tabs, next_shards=shards(l + 1) if l + 1 < dp else None)
        saved.append(s)
    loss_local, dx, d_final_g = _loss_head(x_l, final_g.reshape(1, -1), loss_target[0])
    loss = lax.psum(loss_local[0, 0], AXES)

    as4 = lambda a: a.reshape(a.shape[:1] + (1,) * (4 - a.ndim) + a.shape[1:])

    def finish(l, parts, from_chips, prev):
        return {n: _shard_sum_adamw(p, f, chip, as4(wts[n]), as4(mom1[n]), as4(mom2[n]), l,
                                    None if prev is None else prev[n], "sum_adamw_" + n)
                for n, p, f in zip(BIG, parts, from_chips)}

    grads, updated, pending = [None] * dp, None, None
    for l in reversed(range(dp)):
        dx, grads[l], scattered = _layer_bwd(dx, mem_l, layers[l], saved[l], tabs, pending=pending, core=core)
        if scattered is not None:
            updated = finish(l + 1, *scattered, updated)
        pending = [grads[l][n] for n in BIG]
    recv = _exchange_call(_scatter_to_sibling(pending), "rs_sibling_swap")
    parts = _pair_sums(pending, recv, core)
    updated = finish(0, parts, _exchange_call(_scatter_to_chips(parts), "rs_chip_swap"), updated)
    grad_x = dx
    big_out = [{n: updated[n][k].reshape(wts[n].shape) for n in BIG} for k in range(4)]

    small_g = {n: jnp.stack([g[n] for g in grads]) for n in SMALL if n != "final_g"}
    small_g["final_g"] = d_final_g
    (all_small,) = _all_gather([_pack([small_g[n] for n in SMALL])], "small_all_gather")
    small_bufs = _sum_and_adamw(
        all_small, _pack([wts[n] for n in SMALL]), _pack([mom1[n] for n in SMALL]), _pack([mom2[n] for n in SMALL]),
        "small_sum_adamw")

    outs = []
    for big_vals, small_buf in zip(big_out, small_bufs):
        vals = dict(big_vals)
        vals.update(zip(SMALL, _unpack(small_buf, [wts[n].shape for n in SMALL])))
        outs.append(vals)
    order = ("norm_g", "w_in", "q_norm_g", "k_norm_g", "sg_ln_g", "sg_ln_b", "w_s", "b_s", "mem_norm_g", "w_mem_kv",
             "w_br", "w_out", "final_g")
    result = [loss, grad_x[None]]
    for vals in outs:
        result += [vals[n] for n in order]
    return tuple(result)
```

```python
import functools

import jax
import jax.numpy as jnp
from jax import lax
from jax.experimental import pallas as pl
from jax.experimental.pallas import tpu as pltpu

F32 = jnp.float32
BF16 = jnp.bfloat16
SDS = jax.ShapeDtypeStruct
MESH_ID = pl.DeviceIdType.MESH
AXES = ("x", "y", "c")
N_DEV = 8

D_MODEL = 1024
DEPTH = 4
GRID_W = 64
CHUNK = 128
ROPE_THETA = 10000.0
EPS = 1e-6
HEAD_DIM = 64
A_HEADS = 8
A_KV_HEADS = 2
B_GROUPS = 4
M_HEADS = 4
M_HEAD_DIM = 128
BW = 512
W_A = 1280
W_B = 1536
W_M = 1024
W_G = 3072
IN_WIDTH = W_A + W_B + W_M + W_G
GROUP_OFFS = (0, W_A, W_A + W_B, W_A + W_B + W_M, IN_WIDTH)

ADAM_LR = 0.001
ADAM_B1 = 0.9
ADAM_B2 = 0.999
ADAM_EPS = 1e-08
ADAM_WD = 0.01
ADAM_STEP = 10

LANES = 128
VMEM_LIMIT = 52 * 1024 * 1024


def _tile(n, cap, unit=LANES):
    if n <= cap:
        return n
    t = (cap // unit) * unit
    while n % t:
        t -= unit
    return t


def _params(*sem):
    return pltpu.CompilerParams(dimension_semantics=sem, vmem_limit_bytes=VMEM_LIMIT)


def _sigmoid(z):
    return 1.0 / (1.0 + jnp.exp(-z))


def _silu_and_grad(z):
    s = _sigmoid(z)
    return z * s, s * (1.0 + z * (1.0 - s))


def _dot(a, b, dims):
    return lax.dot_general(a, b, (dims, ((), ())), preferred_element_type=F32)


NN = ((1,), (0,))
NT = ((1,), (1,))
TN = ((0,), (0,))
HBM = pl.BlockSpec(memory_space=pl.ANY)


class _Exchange:
    def __init__(self, ins, out_shapes, sems, make, aliases=None):
        self.ins, self.out_shapes, self.sems, self.make = list(ins), list(out_shapes), list(sems), make
        self.aliases = dict(aliases or {})

    def start(self, in_refs, out_refs, sems):
        for cp in self.make(in_refs, out_refs, sems):
            cp.start()

    def finish(self, in_refs, out_refs, sems):
        for cp in self.make(in_refs, out_refs, sems):
            cp.wait()


def _host_call(body, ins, in_specs, out_specs, out_shape, *, name, grid, semantics, scratch=(), exchange=None):
    ins, out_shape, scratch = list(ins), list(out_shape), list(scratch)
    if exchange is None:
        res = pl.pallas_call(
            body, name=name, grid=grid, in_specs=list(in_specs), out_specs=tuple(out_specs), out_shape=tuple(out_shape),
            scratch_shapes=scratch, compiler_params=_params(*semantics))(*ins)
        return tuple(res), ()
    n_in, n_out, n_scr = len(ins), len(out_shape), len(scratch)
    x_in, x_out = len(exchange.ins), len(exchange.out_shapes)

    def carrying(*refs):
        o0 = n_in + x_in
        s0 = o0 + n_out + x_out
        c_in, c_out, c_sems = refs[n_in:o0], refs[o0 + n_out:s0], refs[s0 + n_scr:]
        ids = [pl.program_id(a) for a in range(len(grid))]
        first = functools.reduce(jnp.logical_and, [i == 0 for i in ids])
        last = functools.reduce(jnp.logical_and, [i == g - 1 for i, g in zip(ids, grid)])

        @pl.when(first)
        def _():
            exchange.start(c_in, c_out, c_sems)

        body(*refs[:n_in], *refs[o0:o0 + n_out], *refs[s0:s0 + n_scr])

        @pl.when(last)
        def _():
            exchange.finish(c_in, c_out, c_sems)

    res = pl.pallas_call(
        carrying, name=name, grid=grid, in_specs=list(in_specs) + [HBM] * x_in,
        out_specs=tuple(out_specs) + (HBM,) * x_out, out_shape=tuple(out_shape) + tuple(exchange.out_shapes),
        scratch_shapes=scratch + exchange.sems,
        input_output_aliases={n_in + i: n_out + o for i, o in exchange.aliases.items()},
        compiler_params=_params(*(["arbitrary"] * len(grid))))(*ins, *exchange.ins)
    return tuple(res[:n_out]), tuple(res[n_out:])


def _exchange_call(exchange, name):
    x_in = len(exchange.ins)

    def body(*refs):
        x_out = len(exchange.out_shapes)
        c_in, c_out, c_sems = refs[:x_in], refs[x_in:x_in + x_out], refs[x_in + x_out:]
        exchange.start(c_in, c_out, c_sems)
        exchange.finish(c_in, c_out, c_sems)

    return pl.pallas_call(
        body, name=name, in_specs=[HBM] * x_in, out_specs=tuple([HBM] * len(exchange.out_shapes)),
        out_shape=tuple(exchange.out_shapes), scratch_shapes=exchange.sems, input_output_aliases=exchange.aliases,
    )(*exchange.ins)


def _core_major(d):
    return (d % 2) * 4 + d // 2


def _mm(a, b, mode, name, add=None, out_dtype=F32, tm_cap=512, tn_cap=1536, tk_cap=1536):
    if mode == "nn":
        (m, k), (k2, n) = a.shape, b.shape
    elif mode == "nt":
        (m, k), (n, k2) = a.shape, b.shape
    else:
        (k, m), (k2, n) = a.shape, b.shape
    assert k == k2, (a.shape, b.shape, mode)
    tm = _tile(m, tm_cap, 8 if mode != "tn" else LANES)
    tn = _tile(n, tn_cap)
    tk = _tile(k, tk_cap, LANES if mode != "tn" else 16)
    nk = k // tk
    dims = {"nn": NN, "nt": NT, "tn": TN}[mode]
    a_spec = {"nn": pl.BlockSpec((tm, tk), lambda i, j, kk: (i, kk)),
              "nt": pl.BlockSpec((tm, tk), lambda i, j, kk: (i, kk)),
              "tn": pl.BlockSpec((tk, tm), lambda i, j, kk: (kk, i))}[mode]
    b_spec = {"nn": pl.BlockSpec((tk, tn), lambda i, j, kk: (kk, j)),
              "nt": pl.BlockSpec((tn, tk), lambda i, j, kk: (j, kk)),
              "tn": pl.BlockSpec((tk, tn), lambda i, j, kk: (kk, j))}[mode]
    o_spec = pl.BlockSpec((tm, tn), lambda i, j, kk: (i, j))
    has_add = add is not None

    def body(*refs):
        a_ref, b_ref = refs[0], refs[1]
        add_ref = refs[2] if has_add else None
        o_ref = refs[3] if has_add else refs[2]
        part = _dot(a_ref[...].astype(BF16), b_ref[...].astype(BF16), dims)
        if nk == 1:
            if has_add:
                part = part + add_ref[...]
            o_ref[...] = part.astype(out_dtype)
        else:
            acc = refs[-1]
            kk = pl.program_id(2)

            @pl.when(kk == 0)
            def _():
                acc[...] = part + add_ref[...] if has_add else part

            @pl.when(kk > 0)
            def _():
                acc[...] += part

            @pl.when(kk == nk - 1)
            def _():
                o_ref[...] = acc[...].astype(out_dtype)

    ins = [a, b] + ([add] if has_add else [])
    in_specs = [a_spec, b_spec] + ([o_spec] if has_add else [])
    return pl.pallas_call(
        body, name=name, grid=(m // tm, n // tn, nk), in_specs=in_specs, out_specs=o_spec,
        out_shape=SDS((m, n), out_dtype),
        scratch_shapes=[pltpu.VMEM((tm, tn), F32)] if nk > 1 else [],
        compiler_params=_params("parallel", "parallel", "arbitrary"),
    )(*ins)


def _mm_tn_by_owner(a, b, name):
    (k, m), (k2, n) = a.shape, b.shape
    assert k == k2 and m % N_DEV == 0, (a.shape, b.shape)
    tk = _tile(k, 512, 16)
    nk = k // tk
    rows = m // N_DEV

    def body(a_ref, b_ref, o_ref, acc):
        kk = pl.program_id(0)
        part = _dot(a_ref[...].astype(BF16), b_ref[...].astype(BF16), TN)

        @pl.when(kk == 0)
        def _():
            acc[...] = part

        @pl.when(kk > 0)
        def _():
            acc[...] += part

        @pl.when(kk == nk - 1)
        def _():
            for d in range(N_DEV):
                pos = _core_major(d)
                o_ref[pos * rows:(pos + 1) * rows, :] = acc[d * rows:(d + 1) * rows, :].astype(BF16)

    return pl.pallas_call(
        body, name=name, grid=(nk,),
        in_specs=[pl.BlockSpec((tk, m), lambda kk: (kk, 0)), pl.BlockSpec((tk, n), lambda kk: (kk, 0))],
        out_specs=pl.BlockSpec((m, n), lambda kk: (0, 0)), out_shape=SDS((m, n), BF16),
        scratch_shapes=[pltpu.VMEM((m, n), F32)], compiler_params=_params("arbitrary"),
    )(a, b)


def _rmsnorm_fwd(x, g, name):
    t, d = x.shape
    tm = _tile(t, 512, 8)

    def body(x_ref, g_ref, h_ref):
        xf = x_ref[...]
        r = lax.rsqrt(jnp.mean(xf * xf, axis=-1, keepdims=True) + EPS)
        h_ref[...] = (xf * r * g_ref[...]).astype(BF16)

    return pl.pallas_call(
        body, name=name, grid=(t // tm,),
        in_specs=[pl.BlockSpec((tm, d), lambda i: (i, 0)), pl.BlockSpec((1, d), lambda i: (0, 0))],
        out_specs=pl.BlockSpec((tm, d), lambda i: (i, 0)),
        out_shape=SDS((t, d), BF16), compiler_params=_params("parallel"),
    )(x, g)


def _rmsnorm_bwd_math(xf, g, dh):
    r = lax.rsqrt(jnp.mean(xf * xf, axis=-1, keepdims=True) + EPS)
    xh = xf * r
    gd = dh * g
    dx = r * (gd - xh * jnp.mean(gd * xh, axis=-1, keepdims=True))
    return dx, jnp.sum(dh * xh, axis=0, keepdims=True)


def _rmsnorm_dg(x, g, dh, name):
    t, d = x.shape
    tm = _tile(t, 512, 8)

    def body(x_ref, g_ref, dh_ref, dg_ref):
        @pl.when(pl.program_id(0) == 0)
        def _():
            dg_ref[...] = jnp.zeros_like(dg_ref)

        dg_ref[...] += _rmsnorm_bwd_math(x_ref[...], g_ref[...], dh_ref[...])[1]

    row = pl.BlockSpec((tm, d), lambda i: (i, 0))
    vec = pl.BlockSpec((1, d), lambda i: (0, 0))
    return pl.pallas_call(
        body, name=name, grid=(t // tm,), in_specs=[row, vec, row], out_specs=vec, out_shape=SDS((1, d), F32),
        compiler_params=_params("arbitrary"),
    )(x, g, dh)


def _dh_rmsnorm_bwd(dps, ws, x, g, dres, exchange=None):
    t, d = x.shape
    tm = _tile(t, 256, 16)
    n = len(dps)

    def body(*refs):
        dp_refs, w_refs = refs[:n], refs[n:2 * n]
        x_ref, g_ref, dres_ref, dx_ref, dg_ref = refs[2 * n:]
        dh = None
        for dp_ref, w_ref in zip(dp_refs, w_refs):
            part = _dot(dp_ref[...], w_ref[...], NT)
            dh = part if dh is None else dh + part
        dx, dg = _rmsnorm_bwd_math(x_ref[...], g_ref[...], dh)
        dx_ref[...] = dx + dres_ref[...]

        @pl.when(pl.program_id(0) == 0)
        def _():
            dg_ref[...] = jnp.zeros_like(dg_ref)

        dg_ref[...] += dg

    row = pl.BlockSpec((tm, d), lambda i: (i, 0))
    vec = pl.BlockSpec((1, d), lambda i: (0, 0))
    return _host_call(
        body, list(dps) + list(ws) + [x, g, dres],
        [pl.BlockSpec((tm, a.shape[1]), lambda i: (i, 0)) for a in dps]
        + [pl.BlockSpec(w.shape, lambda i: (0, 0), pipeline_mode=pl.Buffered(1)) for w in ws] + [row, vec, row],
        (row, vec), (SDS((t, d), F32), SDS((1, d), F32)),
        name="dh_rmsnorm_bwd", grid=(t // tm,), semantics=("arbitrary",), exchange=exchange)


def _rope_tables(t):
    rows = t // GRID_W
    row = jnp.repeat(jnp.arange(rows, dtype=F32), GRID_W)
    col = jnp.tile(jnp.arange(GRID_W, dtype=F32), rows)
    n_freq = HEAD_DIM // 4
    inv = ROPE_THETA ** (-jnp.arange(n_freq, dtype=F32) / n_freq)
    ang = jnp.stack([row[:, None] * inv, col[:, None] * inv], axis=1)
    cos, sin = jnp.cos(ang), jnp.sin(ang)
    c64 = jnp.concatenate([cos[:, 0], cos[:, 0], cos[:, 1], cos[:, 1]], axis=-1)
    s64 = jnp.concatenate([-sin[:, 0], sin[:, 0], -sin[:, 1], sin[:, 1]], axis=-1)
    return jnp.tile(c64, (1, 2)), jnp.tile(s64, (1, 2))


def _head_sums(v, lane):
    lo = jnp.sum(jnp.where(lane < HEAD_DIM, v, 0.0), axis=-1, keepdims=True)
    hi = jnp.sum(jnp.where(lane < HEAD_DIM, 0.0, v), axis=-1, keepdims=True)
    return jnp.where(lane < HEAD_DIM, lo, hi)


def _swap16(v, lane):
    return jnp.where((lane % 32) < 16, pltpu.roll(v, LANES - 16, 1), pltpu.roll(v, 16, 1))


def _attn_prep_fwd(pa, cs, sn, qg2, kg2):
    t = pa.shape[0]
    tq = _tile(t, 512, LANES)
    scale = HEAD_DIM ** -0.5

    def body(pa_ref, cs_ref, sn_ref, qg_ref, kg_ref, q_ref, qT_ref, k_ref, kT_ref, vT_ref):
        lane = lax.broadcasted_iota(jnp.int32, (tq, LANES), 1)
        c, s = cs_ref[...], sn_ref[...]

        def norm_rope(xg, g2):
            r = lax.rsqrt(_head_sums(xg * xg, lane) * (1.0 / HEAD_DIM) + EPS)
            xn = xg * r * g2
            return xn * c + _swap16(xn, lane) * s

        for gi in range(4):
            sl = slice(gi * LANES, (gi + 1) * LANES)
            qr = norm_rope(pa_ref[:, sl].astype(F32), qg_ref[...]) * scale
            q_ref[:, sl] = qr.astype(BF16)
            qT_ref[sl, :] = qr.T.astype(BF16)
        kr = norm_rope(pa_ref[:, 512:640].astype(F32), kg_ref[...])
        kT_ref[...] = kr.T.astype(BF16)
        vT_ref[...] = pa_ref[:, 640:768].astype(F32).T.astype(BF16)
        kr = kr.astype(BF16)
        for kvh in range(A_KV_HEADS):
            k_ref[kvh] = kr[:, kvh * HEAD_DIM:(kvh + 1) * HEAD_DIM]

    row = lambda w: pl.BlockSpec((tq, w), lambda i: (i, 0))
    col = lambda r: pl.BlockSpec((r, tq), lambda i: (0, i))
    vec = pl.BlockSpec((1, LANES), lambda i: (0, 0))
    hm = pl.BlockSpec((A_KV_HEADS, tq, HEAD_DIM), lambda i: (0, i, 0))
    return pl.pallas_call(
        body, name="attn_prep_fwd", grid=(t // tq,),
        in_specs=[row(W_A), row(LANES), row(LANES), vec, vec],
        out_specs=(row(BW), col(BW), hm, col(LANES), col(LANES)),
        out_shape=(SDS((t, BW), BF16), SDS((BW, t), BF16), SDS((A_KV_HEADS, t, HEAD_DIM), BF16),
                   SDS((LANES, t), BF16), SDS((LANES, t), BF16)),
        compiler_params=_params("parallel"),
    )(pa, cs, sn, qg2, kg2)


def _attn_fwd(qT, k, vT, pa, exchange=None):
    t = qT.shape[1]
    tq = _tile(t, 256, LANES)
    grp = A_HEADS // A_KV_HEADS

    def body(qT_ref, k_ref, vT_ref, pa_ref, o_ref, lse_ref, y_ref):
        def scores(h):
            return _dot(k_ref[h // grp], qT_ref[h * HEAD_DIM:(h + 1) * HEAD_DIM, :], NN)

        sT, o_pair = scores(0), []
        for h in range(A_HEADS):
            kvh = h // grp
            sT_next = scores(h + 1) if h + 1 < A_HEADS else None
            m = jnp.max(sT, axis=0, keepdims=True)
            pT = jnp.exp(sT - m)
            l = jnp.sum(pT, axis=0, keepdims=True)
            oT = _dot(vT_ref[kvh * HEAD_DIM:(kvh + 1) * HEAD_DIM, :], pT.astype(BF16), NN)
            o_pair.append(oT / l)
            lse_ref[h:h + 1, :] = m + jnp.log(l)
            sT = sT_next
            if h % 2:
                o2 = jnp.concatenate(o_pair, axis=0).T
                o_pair = []
                cols = slice((h // 2) * LANES, (h // 2 + 1) * LANES)
                o_ref[:, cols] = o2
                z = pa_ref[:, 768 + (h // 2) * LANES:768 + (h // 2 + 1) * LANES].astype(F32)
                y_ref[:, cols] = (o2 * (z * _sigmoid(z))).astype(BF16)

    row = lambda w: pl.BlockSpec((tq, w), lambda i: (i, 0))
    return _host_call(
        body, (qT, k, vT, pa),
        [pl.BlockSpec((BW, tq), lambda i: (0, i)), pl.BlockSpec((A_KV_HEADS, t, HEAD_DIM), lambda i: (0, 0, 0)),
         pl.BlockSpec((A_KV_HEADS * HEAD_DIM, t), lambda i: (0, 0)), row(W_A)],
        (row(BW), pl.BlockSpec((A_HEADS, tq), lambda i: (0, i)), row(BW)),
        (SDS((t, BW), F32), SDS((A_HEADS, t), F32), SDS((t, BW), BF16)),
        name="attn_fwd", grid=(t // tq,), semantics=("parallel",), exchange=exchange)


def _attn_bwd(q, qT, k, kT, vT, pa, o, lse, dy, exchange=None):
    t = q.shape[0]
    tq = _tile(t, 256, LANES)
    grp = A_HEADS // A_KV_HEADS
    gw = grp * HEAD_DIM

    def body(q_ref, qT_ref, k_ref, kT_ref, vT_ref, z_ref, o_ref, lse_ref, dy_ref, dq_ref, dkT_ref, dvT_ref):
        @pl.when(pl.program_id(1) == 0)
        def _():
            dkT_ref[...] = jnp.zeros_like(dkT_ref)
            dvT_ref[...] = jnp.zeros_like(dvT_ref)

        z = z_ref[...].astype(F32)
        do = dy_ref[...] * (z * _sigmoid(z))
        doo = do * o_ref[...]
        doT = do.T
        kk, kT, vT = k_ref[...], kT_ref[...], vT_ref[...]
        for j in range(grp):
            hs = slice(j * HEAD_DIM, (j + 1) * HEAD_DIM)
            delta = jnp.sum(doo[:, hs], axis=-1, keepdims=True)
            s = _dot(q_ref[:, hs], kT, NN)
            p = jnp.exp(s - lse_ref[:, j:j + 1])
            dp = _dot(do[:, hs].astype(BF16), vT, NN)
            ds_b = (p * (dp - delta)).astype(BF16)
            dq_ref[:, hs] = _dot(ds_b, kk, NN)
            dkT_ref[...] += _dot(qT_ref[hs, :], ds_b, NN)
            dvT_ref[...] += _dot(doT[hs, :].astype(BF16), p.astype(BF16), NN)

    grp_blk = pl.BlockSpec((tq, gw), lambda g, i: (i, g))
    kvT_blk = pl.BlockSpec((HEAD_DIM, t), lambda g, i: (g, 0))
    acc_blk = pl.BlockSpec((None, HEAD_DIM, t), lambda g, i: (g, 0, 0))
    return _host_call(
        body, (q, qT, k, kT, vT, pa, o, lse, dy),
        [grp_blk, pl.BlockSpec((gw, tq), lambda g, i: (g, i)), pl.BlockSpec((None, t, HEAD_DIM), lambda g, i: (g, 0, 0)),
         kvT_blk, kvT_blk, pl.BlockSpec((tq, gw), lambda g, i: (i, 768 // gw + g)), grp_blk,
         pl.BlockSpec((None, tq, grp), lambda g, i: (g, i, 0)), grp_blk],
        (grp_blk, acc_blk, acc_blk),
        (SDS((t, BW), F32), SDS((A_KV_HEADS, HEAD_DIM, t), F32), SDS((A_KV_HEADS, HEAD_DIM, t), F32)),
        name="attn_bwd", grid=(A_KV_HEADS, t // tq), semantics=("arbitrary", "arbitrary"), exchange=exchange)


def _attn_prep_bwd(pa, cs, sn, qg2, kg2, dq, dkT, dvT, dy, o):
    t = pa.shape[0]
    tq = _tile(t, 512, LANES)
    scale = HEAD_DIM ** -0.5

    def body(pa_ref, cs_ref, sn_ref, qg_ref, kg_ref, dq_ref, dkT_ref, dvT_ref, dy_ref, o_ref, dpa_ref, dqg_ref, dkg_ref):
        lane = lax.broadcasted_iota(jnp.int32, (tq, LANES), 1)
        c, s = cs_ref[...], sn_ref[...]

        @pl.when(pl.program_id(0) == 0)
        def _():
            dqg_ref[...] = jnp.zeros_like(dqg_ref)
            dkg_ref[...] = jnp.zeros_like(dkg_ref)

        def norm_rope_bwd(xg, g2, dout):
            r = lax.rsqrt(_head_sums(xg * xg, lane) * (1.0 / HEAD_DIM) + EPS)
            xh = xg * r
            dxn = dout * c + _swap16(dout * s, lane)
            gd = dxn * g2
            dx = r * (gd - xh * (_head_sums(gd * xh, lane) * (1.0 / HEAD_DIM)))
            return dx, jnp.sum(dxn * xh, axis=0, keepdims=True)

        for gi in range(4):
            sl = slice(gi * LANES, (gi + 1) * LANES)
            dx, dg = norm_rope_bwd(pa_ref[:, sl].astype(F32), qg_ref[...], dq_ref[:, sl] * scale)
            dpa_ref[:, sl] = dx.astype(BF16)
            dqg_ref[...] += dg
        dx, dg = norm_rope_bwd(pa_ref[:, 512:640].astype(F32), kg_ref[...], dkT_ref[...].T)
        dpa_ref[:, 512:640] = dx.astype(BF16)
        dkg_ref[...] += dg
        dpa_ref[:, 640:768] = dvT_ref[...].T.astype(BF16)
        z = pa_ref[:, 768:1280].astype(F32)
        _, dsilu = _silu_and_grad(z)
        dpa_ref[:, 768:1280] = (dy_ref[...] * o_ref[...] * dsilu).astype(BF16)

    row = lambda w: pl.BlockSpec((tq, w), lambda i: (i, 0))
    col = pl.BlockSpec((LANES, tq), lambda i: (0, i))
    vec = pl.BlockSpec((1, LANES), lambda i: (0, 0))
    return pl.pallas_call(
        body, name="attn_prep_bwd", grid=(t // tq,),
        in_specs=[row(W_A), row(LANES), row(LANES), vec, vec, row(BW), col, col, row(BW), row(BW)],
        out_specs=(row(W_A), vec, vec),
        out_shape=(SDS((t, W_A), BF16), SDS((1, LANES), F32), SDS((1, LANES), F32)),
        compiler_params=_params("arbitrary"),
    )(pa, cs, sn, qg2, kg2, dq, dkT, dvT, dy, o)


def _layer_norm(v, g, b):
    mu = jnp.mean(v, axis=-1, keepdims=True)
    xc = v - mu
    rs = lax.rsqrt(jnp.mean(xc * xc, axis=-1, keepdims=True) + EPS)
    xh = xc * rs
    return xh * g + b, xh, rs


def _gmlp_fwd(pb, lng, lnb, ws, bs):
    t = pb.shape[0]
    tb = _tile(t, 512, CHUNK)

    def body(pb_ref, g_ref, b_ref, ws_ref, bs_ref, y_ref):
        vln, _, _ = _layer_norm(pb_ref[:, BW:2 * BW].astype(F32), g_ref[...], b_ref[...])
        vb = vln.astype(BF16)
        for gi in range(B_GROUPS):
            w = ws_ref[gi].astype(BF16)
            cs_ = slice(gi * CHUNK, (gi + 1) * CHUNK)
            for n in range(tb // CHUNK):
                rs_ = slice(n * CHUNK, (n + 1) * CHUNK)
                mixed = _dot(w, vb[rs_, cs_], NN) + bs_ref[gi]
                z = pb_ref[rs_, 2 * BW + gi * CHUNK:2 * BW + (gi + 1) * CHUNK].astype(F32)
                y_ref[rs_, cs_] = (pb_ref[rs_, cs_].astype(F32) * mixed * (z * _sigmoid(z))).astype(BF16)

    return pl.pallas_call(
        body, name="gmlp_fwd", grid=(t // tb,),
        in_specs=[pl.BlockSpec((tb, W_B), lambda i: (i, 0)),
                  pl.BlockSpec((1, BW), lambda i: (0, 0)), pl.BlockSpec((1, BW), lambda i: (0, 0)),
                  pl.BlockSpec((B_GROUPS, CHUNK, CHUNK), lambda i: (0, 0, 0)),
                  pl.BlockSpec((B_GROUPS, CHUNK, 1), lambda i: (0, 0, 0))],
        out_specs=pl.BlockSpec((tb, BW), lambda i: (i, 0)),
        out_shape=SDS((t, BW), BF16), compiler_params=_params("parallel"),
    )(pb, lng, lnb, ws, bs)


def _gmlp_bwd(pb, lng, lnb, ws, bs, dy):
    t = pb.shape[0]
    tb = _tile(t, 256, CHUNK)

    def body(pb_ref, g_ref, b_ref, ws_ref, bs_ref, dy_ref, dpb_ref, dws_ref, dbs_ref, dg_ref, db_ref, dvln_ref):
        @pl.when(pl.program_id(0) == 0)
        def _():
            dws_ref[...] = jnp.zeros_like(dws_ref)
            dbs_ref[...] = jnp.zeros_like(dbs_ref)
            dg_ref[...] = jnp.zeros_like(dg_ref)
            db_ref[...] = jnp.zeros_like(db_ref)

        vln, xh, rs = _layer_norm(pb_ref[:, BW:2 * BW].astype(F32), g_ref[...], b_ref[...])
        vb = vln.astype(BF16)
        for gi in range(B_GROUPS):
            w = ws_ref[gi].astype(BF16)
            cs_ = slice(gi * CHUNK, (gi + 1) * CHUNK)
            for n in range(tb // CHUNK):
                rs_ = slice(n * CHUNK, (n + 1) * CHUNK)
                vbc = vb[rs_, cs_]
                mixed = _dot(w, vbc, NN) + bs_ref[gi]
                zs = slice(2 * BW + gi * CHUNK, 2 * BW + (gi + 1) * CHUNK)
                z = pb_ref[rs_, zs].astype(F32)
                u = pb_ref[rs_, cs_].astype(F32)
                sil, dsil = _silu_and_grad(z)
                dyc = dy_ref[rs_, cs_]
                dmixed = dyc * u * sil
                dpb_ref[rs_, cs_] = (dyc * mixed * sil).astype(BF16)
                dpb_ref[rs_, zs] = (dyc * u * mixed * dsil).astype(BF16)
                dmb = dmixed.astype(BF16)
                dws_ref[gi] += _dot(dmb, vbc, NT)
                dbs_ref[gi] += jnp.sum(dmixed, axis=-1, keepdims=True)
                dvln_ref[rs_, cs_] = _dot(w, dmb, TN)
        dvln = dvln_ref[...]
        dg_ref[...] += jnp.sum(dvln * xh, axis=0, keepdims=True)
        db_ref[...] += jnp.sum(dvln, axis=0, keepdims=True)
        gd = dvln * g_ref[...]
        dv = rs * (gd - jnp.mean(gd, axis=-1, keepdims=True) - xh * jnp.mean(gd * xh, axis=-1, keepdims=True))
        dpb_ref[:, BW:2 * BW] = dv.astype(BF16)

    vec = pl.BlockSpec((1, BW), lambda i: (0, 0))
    wsb = pl.BlockSpec((B_GROUPS, CHUNK, CHUNK), lambda i: (0, 0, 0))
    bsb = pl.BlockSpec((B_GROUPS, CHUNK, 1), lambda i: (0, 0, 0))
    return pl.pallas_call(
        body, name="gmlp_bwd", grid=(t // tb,),
        in_specs=[pl.BlockSpec((tb, W_B), lambda i: (i, 0)), vec, vec, wsb, bsb, pl.BlockSpec((tb, BW), lambda i: (i, 0))],
        out_specs=(pl.BlockSpec((tb, W_B), lambda i: (i, 0)), wsb, bsb, vec, vec),
        out_shape=(SDS((t, W_B), BF16), SDS((B_GROUPS, CHUNK, CHUNK), F32), SDS((B_GROUPS, CHUNK, 1), F32),
                   SDS((1, BW), F32), SDS((1, BW), F32)),
        scratch_shapes=[pltpu.VMEM((tb, BW), F32)],
        compiler_params=_params("arbitrary"),
    )(pb, lng, lnb, ws, bs, dy)


def _mem_softmax(qh, kh):
    s = _dot(qh, kh, NT) * (M_HEAD_DIM ** -0.5)
    e = jnp.exp(s - jnp.max(s, axis=-1, keepdims=True))
    return e / jnp.sum(e, axis=-1, keepdims=True)


def _mem_attn_fwd(pm, kv):
    t = pm.shape[0]
    tq = _tile(t, 512, 8)
    ml = kv.shape[0]

    def body(pm_ref, kv_ref, y_ref):
        for h in range(M_HEADS):
            hs = slice(h * M_HEAD_DIM, (h + 1) * M_HEAD_DIM)
            kh = kv_ref[:, hs].astype(BF16)
            vh = kv_ref[:, BW + h * M_HEAD_DIM:BW + (h + 1) * M_HEAD_DIM].astype(BF16)
            p = _mem_softmax(pm_ref[:, hs].astype(BF16), kh)
            o = _dot(p.astype(BF16), vh, NN)
            z = pm_ref[:, BW + h * M_HEAD_DIM:BW + (h + 1) * M_HEAD_DIM].astype(F32)
            y_ref[:, hs] = (o * (z * _sigmoid(z))).astype(BF16)

    return pl.pallas_call(
        body, name="mem_attn_fwd", grid=(t // tq,),
        in_specs=[pl.BlockSpec((tq, W_M), lambda i: (i, 0)), pl.BlockSpec((ml, 2 * BW), lambda i: (0, 0))],
        out_specs=pl.BlockSpec((tq, BW), lambda i: (i, 0)),
        out_shape=SDS((t, BW), BF16), compiler_params=_params("parallel"),
    )(pm, kv)


def _mem_attn_bwd(pm, kv, dy):
    t = pm.shape[0]
    tq = _tile(t, 512, 8)
    ml = kv.shape[0]
    scale = M_HEAD_DIM ** -0.5

    def body(pm_ref, kv_ref, dy_ref, dpm_ref, dkv_ref):
        @pl.when(pl.program_id(0) == 0)
        def _():
            dkv_ref[...] = jnp.zeros_like(dkv_ref)

        for h in range(M_HEADS):
            hs = slice(h * M_HEAD_DIM, (h + 1) * M_HEAD_DIM)
            zs = slice(BW + h * M_HEAD_DIM, BW + (h + 1) * M_HEAD_DIM)
            kh = kv_ref[:, hs].astype(BF16)
            vh = kv_ref[:, zs].astype(BF16)
            qh = pm_ref[:, hs].astype(BF16)
            p = _mem_softmax(qh, kh)
            pb = p.astype(BF16)
            o = _dot(pb, vh, NN)
            sil, dsil = _silu_and_grad(pm_ref[:, zs].astype(F32))
            dyh = dy_ref[:, hs]
            do = dyh * sil
            dpm_ref[:, zs] = (dyh * o * dsil).astype(BF16)
            delta = jnp.sum(do * o, axis=-1, keepdims=True)
            do_b = do.astype(BF16)
            dp = _dot(do_b, vh, NT)
            dr_b = (p * (dp - delta) * scale).astype(BF16)
            dpm_ref[:, hs] = _dot(dr_b, kh, NN).astype(BF16)
            dkv_ref[:, hs] += _dot(dr_b, qh, TN)
            dkv_ref[:, zs] += _dot(pb, do_b, TN)

    kvb = pl.BlockSpec((ml, 2 * BW), lambda i: (0, 0))
    return pl.pallas_call(
        body, name="mem_attn_bwd", grid=(t // tq,),
        in_specs=[pl.BlockSpec((tq, W_M), lambda i: (i, 0)), kvb, pl.BlockSpec((tq, BW), lambda i: (i, 0))],
        out_specs=(pl.BlockSpec((tq, W_M), lambda i: (i, 0)), kvb),
        out_shape=(SDS((t, W_M), BF16), SDS((ml, 2 * BW), F32)),
        compiler_params=_params("arbitrary"),
    )(pm, kv, dy)


def _merge_fwd(ya, yb, ym, pg, wbr, exchange=None):
    t = ya.shape[0]
    tm = _tile(t, 512, 8)

    def body(ya_ref, yb_ref, ym_ref, pg_ref, w_ref, m_ref):
        acc = None
        for n, y_ref in enumerate((ya_ref, yb_ref, ym_ref)):
            up = _dot(y_ref[...], w_ref[n], NN)
            term = _sigmoid(pg_ref[:, n * D_MODEL:(n + 1) * D_MODEL].astype(F32)) * up
            acc = term if acc is None else acc + term
        m_ref[...] = acc.astype(BF16)

    yb_spec = pl.BlockSpec((tm, BW), lambda i: (i, 0))
    return _host_call(
        body, (ya, yb, ym, pg, wbr),
        [yb_spec, yb_spec, yb_spec, pl.BlockSpec((tm, W_G), lambda i: (i, 0)),
         pl.BlockSpec((3, BW, D_MODEL), lambda i: (0, 0, 0))],
        (pl.BlockSpec((tm, D_MODEL), lambda i: (i, 0)),), (SDS((t, D_MODEL), BF16),),
        name="merge_fwd", grid=(t // tm,), semantics=("parallel",), exchange=exchange)


def _merge_bwd(ya, yb, ym, pg, wbr, dm, exchange=None):
    t = ya.shape[0]
    tm = _tile(t, 256, 8)

    def body(ya_ref, yb_ref, ym_ref, pg_ref, w_ref, dm_ref, dya_ref, dyb_ref, dym_ref, dpg_ref, dw_ref):
        @pl.when(pl.program_id(0) == 0)
        def _():
            dw_ref[...] = jnp.zeros_like(dw_ref)

        dmf = dm_ref[...]
        for n, (y_ref, dy_ref) in enumerate(((ya_ref, dya_ref), (yb_ref, dyb_ref), (ym_ref, dym_ref))):
            cs_ = slice(n * D_MODEL, (n + 1) * D_MODEL)
            y = y_ref[...]
            w = w_ref[n]
            up = _dot(y, w, NN)
            gt = _sigmoid(pg_ref[:, cs_].astype(F32))
            dpg_ref[:, cs_] = (dmf * up * gt * (1.0 - gt)).astype(BF16)
            dup = (dmf * gt).astype(BF16)
            dy_ref[...] = _dot(dup, w, NT)
            dw_ref[n] += _dot(y, dup, TN)

    y_spec = pl.BlockSpec((tm, BW), lambda i: (i, 0))
    w_spec = pl.BlockSpec((3, BW, D_MODEL), lambda i: (0, 0, 0))
    return _host_call(
        body, (ya, yb, ym, pg, wbr, dm),
        [y_spec, y_spec, y_spec, pl.BlockSpec((tm, W_G), lambda i: (i, 0)), w_spec,
         pl.BlockSpec((tm, D_MODEL), lambda i: (i, 0))],
        (y_spec, y_spec, y_spec, pl.BlockSpec((tm, W_G), lambda i: (i, 0)), w_spec),
        (SDS((t, BW), F32), SDS((t, BW), F32), SDS((t, BW), F32), SDS((t, W_G), BF16), SDS((3, BW, D_MODEL), F32)),
        name="merge_bwd", grid=(t // tm,), semantics=("arbitrary",), exchange=exchange)


def _loss_head(x, g, target):
    t, d = x.shape
    tm = _tile(t, 512, 8)

    def body(x_ref, g_ref, t_ref, loss_ref, dx_ref, dg_ref):
        @pl.when(pl.program_id(0) == 0)
        def _():
            loss_ref[...] = jnp.zeros_like(loss_ref)
            dg_ref[...] = jnp.zeros_like(dg_ref)

        xf = x_ref[...]
        r = lax.rsqrt(jnp.mean(xf * xf, axis=-1, keepdims=True) + EPS)
        xh = xf * r
        err = xh * g_ref[...] - t_ref[...]
        per_tok = jnp.mean(err * err, axis=-1, keepdims=True)
        loss_ref[...] += 0.5 * jnp.sum(per_tok, axis=0, keepdims=True)
        dy = err * (1.0 / d)
        gd = dy * g_ref[...]
        dx_ref[...] = r * (gd - xh * jnp.mean(gd * xh, axis=-1, keepdims=True))
        dg_ref[...] += jnp.sum(dy * xh, axis=0, keepdims=True)

    row = pl.BlockSpec((tm, d), lambda i: (i, 0))
    vec = pl.BlockSpec((1, d), lambda i: (0, 0))
    return pl.pallas_call(
        body, name="loss_head", grid=(t // tm,),
        in_specs=[row, vec, row],
        out_specs=(pl.BlockSpec((1, 1), lambda i: (0, 0)), row, vec),
        out_shape=(SDS((1, 1), F32), SDS((t, d), F32), SDS((1, d), F32)),
        compiler_params=_params("arbitrary"),
    )(x, g, target)


def _layer_fwd(x, mem, w, tabs, next_shards=None):
    cs, sn = tabs
    riding = next_shards is not None
    h = _rmsnorm_fwd(x, w["norm_g"], "rmsnorm_fwd")
    pa, pb, pm, pg = [_mm(h, w["w_" + nm], "nn", "proj_" + nm, out_dtype=BF16, tm_cap=1024) for nm in "abmg"]
    q, qT, k, kT, vT = _attn_prep_fwd(pa, cs, sn, w["qg2"], w["kg2"])
    (o, lse, ya), gathered = _attn_fwd(qT, k, vT, pa, exchange=_gather_first_hop(next_shards) if riding else None)
    yb = _gmlp_fwd(pb, w["ln_g"], w["ln_b"], w["w_s"], w["b_s"])
    memn = _rmsnorm_fwd(mem, w["mem_g"], "mem_rmsnorm_fwd")
    kv = _mm(memn, w["w_kv"], "nn", "mem_kv")
    ym = _mem_attn_fwd(pm, kv)
    (merged,), gathered = _merge_fwd(ya, yb, ym, pg, w["w_br"], exchange=_gather_forward(gathered) if riding else None)
    x_next = _mm(merged, w["w_out"], "nn", "out_proj", add=x)
    saved = dict(x=x, h=h, pa=pa, pb=pb, pm=pm, pg=pg, q=q, qT=qT, k=k, kT=kT, vT=vT, o=o, lse=lse, ya=ya, yb=yb, ym=ym,
                 memn=memn, kv=kv, merged=merged)
    return x_next, saved, gathered


def _layer_bwd(dx_out, mem, w, s, tabs, pending=None, core=None, scatter_own=False):
    cs, sn = tabs
    t = dx_out.shape[0]
    riding = pending is not None
    dmerged = _mm(dx_out, w["w_out"], "nt", "d_merged")
    shard_rows = D_MODEL // N_DEV
    d_w_out = _mm_tn_by_owner(s["merged"], dx_out, "d_w_out")
    (dya, dyb, dym, dpg, d_w_br), recv = _merge_bwd(s["ya"], s["yb"], s["ym"], s["pg"], w["w_br"], dmerged,
                                                    exchange=_scatter_to_sibling(pending) if riding else None)
    parts = _pair_sums(pending, recv, core) if riding else None
    grp = A_HEADS // A_KV_HEADS
    lse_cols = s["lse"].reshape(A_KV_HEADS, grp, t).transpose(0, 2, 1)
    (dq, dkT, dvT), from_chips = _attn_bwd(s["q"], s["qT"], s["k"], s["kT"], s["vT"], s["pa"], s["o"], lse_cols, dya,
                                           exchange=_scatter_to_chips(parts) if riding else None)
    dpa, d_qg2, d_kg2 = _attn_prep_bwd(s["pa"], cs, sn, w["qg2"], w["kg2"], dq, dkT.reshape(-1, t), dvT.reshape(-1, t),
                                       dya, s["o"])
    dpb, d_w_s, d_b_s, d_ln_g, d_ln_b = _gmlp_bwd(s["pb"], w["ln_g"], w["ln_b"], w["w_s"], w["b_s"], dyb)
    dpm, dkv = _mem_attn_bwd(s["pm"], s["kv"], dym)
    d_w_kv = _mm_tn_by_owner(s["memn"], dkv, "d_w_kv")
    dmemn = _mm(dkv, w["w_kv"], "nt", "d_memn")
    d_mem_g = _rmsnorm_dg(mem, w["mem_g"], dmemn, "mem_rmsnorm_bwd")
    dps = (dpa, dpb, dpm, dpg)
    d_w_in = _split_w_in(*[_mm(s["h"], dp, "tn", "d_w_" + nm, out_dtype=BF16, tm_cap=1024, tk_cap=2048)
                           for nm, dp in zip("abmg", dps)])
    grads = dict(w_in=d_w_in[:, None], w_mem_kv=d_w_kv.reshape(N_DEV, 1, shard_rows, -1), w_br=_split_w_br(d_w_br),
                 w_out=d_w_out.reshape(N_DEV, 1, shard_rows, -1))
    own = None
    if scatter_own:
        dests = [grads[n] for n in BIG]
        own_parts = _pair_sums(dests, _exchange_call(_scatter_to_sibling(dests), "rs_sibling_swap"), core)
        own = _scatter_to_chips(own_parts)
    (dx_in, d_norm_g), own_from_chips = _dh_rmsnorm_bwd(dps, [w["w_" + nm] for nm in "abmg"], s["x"], w["norm_g"], dx_out,
                                                        exchange=own)
    grads.update(norm_g=d_norm_g[0], q_norm_g=d_qg2[0, :HEAD_DIM] + d_qg2[0, HEAD_DIM:],
                 k_norm_g=d_kg2[0, :HEAD_DIM] + d_kg2[0, HEAD_DIM:], sg_ln_g=d_ln_g[0], sg_ln_b=d_ln_b[0],
                 w_s=d_w_s, b_s=d_b_s[:, :, 0], mem_norm_g=d_mem_g[0])
    return (dx_in, grads, ((parts, from_chips) if riding else None),
            ((own_parts, own_from_chips) if scatter_own else None))


def _layer_weights(l, w_groups, w_kv, w_br, w_out, small):
    tile2 = lambda g: jnp.tile(g.reshape(1, -1), (1, 2))
    w_a, w_b, w_m, w_g = w_groups
    return dict(
        w_a=w_a, w_b=w_b, w_m=w_m, w_g=w_g, w_kv=w_kv, w_br=w_br, w_out=w_out,
        norm_g=small["norm_g"][l].reshape(1, -1), qg2=tile2(small["q_norm_g"][l]), kg2=tile2(small["k_norm_g"][l]),
        ln_g=small["sg_ln_g"][l].reshape(1, -1), ln_b=small["sg_ln_b"][l].reshape(1, -1),
        w_s=small["w_s"][l], b_s=small["b_s"][l][:, :, None], mem_g=small["mem_norm_g"][l].reshape(1, -1))


def _position():
    x, y, c = lax.axis_index("x"), lax.axis_index("y"), lax.axis_index("c")
    return x, y, c, [(1 - x, y), (x, 1 - y), (1 - x, 1 - y)]


def _gather_first_hop(shards):
    n = len(shards)

    def make(x_refs, out_refs, sems):
        send_sems, recv_sems, local_sems = sems
        x, y, c, chips = _position()
        me = 4 * x + 2 * y + c
        peers = [(x, y, 1 - c)] + [(cx, cy, c) for cx, cy in chips]
        copies = [pltpu.make_async_copy(x_refs[t], out_refs[t].at[me], local_sems.at[t]) for t in range(n)]
        copies += [pltpu.make_async_remote_copy(
            src_ref=x_refs[t], dst_ref=out_refs[t].at[me], send_sem=send_sems.at[t, k], recv_sem=recv_sems.at[t, k],
            device_id=peer, device_id_type=MESH_ID) for t in range(n) for k, peer in enumerate(peers)]
        return copies

    return _Exchange(shards, [SDS((N_DEV,) + a.shape, a.dtype) for a in shards],
                     [pltpu.SemaphoreType.DMA((n, 4)), pltpu.SemaphoreType.DMA((n, 4)), pltpu.SemaphoreType.DMA((n,))], make)


def _gather_forward(gathered):
    n = len(gathered)

    def make(in_refs, out_refs, sems):
        send_sems, recv_sems = sems
        x, y, c, chips = _position()
        return [pltpu.make_async_remote_copy(
            src_ref=in_refs[t].at[4 * cx + 2 * cy + c], dst_ref=out_refs[t].at[4 * cx + 2 * cy + c],
            send_sem=send_sems.at[t, j], recv_sem=recv_sems.at[t, j], device_id=(x, y, 1 - c), device_id_type=MESH_ID)
            for t in range(n) for j, (cx, cy) in enumerate(chips)]

    return _Exchange(gathered, [SDS(a.shape, a.dtype) for a in gathered],
                     [pltpu.SemaphoreType.DMA((n, 3)), pltpu.SemaphoreType.DMA((n, 3))], make,
                     aliases={t: t for t in range(n)})


def _all_gather(shards, name):
    return _exchange_call(_gather_forward(_exchange_call(_gather_first_hop(shards), name + "_hop1")), name + "_hop2")


def _shard_segments(d):
    shard = IN_WIDTH // N_DEV
    lo, hi = d * shard, (d + 1) * shard
    out = []
    for g in range(4):
        a, b = max(lo, GROUP_OFFS[g]), min(hi, GROUP_OFFS[g + 1])
        if a < b:
            out.append((g, a - GROUP_OFFS[g], b - GROUP_OFFS[g], a - lo))
    return out


def _assemble_w_in(gathered):
    _, rows, shard = gathered.shape
    tr = _tile(rows, 256, 16)
    widths = [GROUP_OFFS[g + 1] - GROUP_OFFS[g] for g in range(4)]

    def body(g_ref, *outs):
        for d in range(N_DEV):
            for g, lo, hi, off in _shard_segments(d):
                outs[g][:, lo:hi] = g_ref[d, :, off:off + hi - lo]

    return pl.pallas_call(
        body, name="assemble_w_in", grid=(rows // tr,),
        in_specs=[pl.BlockSpec((N_DEV, tr, shard), lambda i: (0, i, 0))],
        out_specs=tuple(pl.BlockSpec((tr, w), lambda i: (i, 0)) for w in widths),
        out_shape=tuple(SDS((rows, w), gathered.dtype) for w in widths), compiler_params=_params("parallel"),
    )(gathered)


def _split_w_in(da, db, dm, dg):
    rows = da.shape[0]
    shard = IN_WIDTH // N_DEV
    tr = _tile(rows, 256, 16)
    widths = [GROUP_OFFS[g + 1] - GROUP_OFFS[g] for g in range(4)]

    def body(a_ref, b_ref, m_ref, g_ref, o_ref):
        ins = (a_ref, b_ref, m_ref, g_ref)
        for d in range(N_DEV):
            for g, lo, hi, off in _shard_segments(d):
                o_ref[_core_major(d), :, off:off + hi - lo] = ins[g][:, lo:hi]

    return pl.pallas_call(
        body, name="split_w_in", grid=(rows // tr,),
        in_specs=[pl.BlockSpec((tr, w), lambda i: (i, 0)) for w in widths],
        out_specs=pl.BlockSpec((N_DEV, tr, shard), lambda i: (0, i, 0)),
        out_shape=SDS((N_DEV, rows, shard), da.dtype), compiler_params=_params("parallel"),
    )(da, db, dm, dg)


def _assemble_w_br(gathered):
    _, nb, rows, shard = gathered.shape

    def body(g_ref, o_ref):
        o_ref[...] = g_ref[...]

    return pl.pallas_call(
        body, name="assemble_w_br", grid=(N_DEV,),
        in_specs=[pl.BlockSpec((None, nb, rows, shard), lambda d: (d, 0, 0, 0))],
        out_specs=pl.BlockSpec((nb, rows, shard), lambda d: (0, 0, d)),
        out_shape=SDS((nb, rows, N_DEV * shard), gathered.dtype), compiler_params=_params("parallel"),
    )(gathered)


def _split_w_br(dw):
    nb, rows, cols = dw.shape
    shard = cols // N_DEV

    def body(d_ref, o_ref):
        o_ref[...] = d_ref[...].astype(BF16)

    return pl.pallas_call(
        body, name="split_w_br", grid=(N_DEV,),
        in_specs=[pl.BlockSpec((nb, rows, shard), lambda d: (0, 0, d))],
        out_specs=pl.BlockSpec((None, nb, rows, shard), lambda d: (_core_major(d), 0, 0, 0)),
        out_shape=SDS((N_DEV, nb, rows, shard), BF16), compiler_params=_params("parallel"),
    )(dw)


def _scatter_to_sibling(dests):
    n = len(dests)

    def make(d_refs, recv_refs, sems):
        send_sems, recv_sems = sems
        x, y, c, _ = _position()
        return [pltpu.make_async_remote_copy(
            src_ref=d_refs[t].at[pl.ds((1 - c) * 4, 4)], dst_ref=recv_refs[t], send_sem=send_sems.at[t],
            recv_sem=recv_sems.at[t], device_id=(x, y, 1 - c), device_id_type=MESH_ID) for t in range(n)]

    return _Exchange(dests, [SDS((4,) + a.shape[1:], a.dtype) for a in dests],
                     [pltpu.SemaphoreType.DMA((n,)), pltpu.SemaphoreType.DMA((n,))], make)


def _scatter_to_chips(parts):
    n = len(parts)

    def make(p_refs, recv_refs, sems):
        send_sems, recv_sems = sems
        _, _, c, chips = _position()
        return [pltpu.make_async_remote_copy(
            src_ref=p_refs[t].at[2 * cx + cy], dst_ref=recv_refs[t].at[k], send_sem=send_sems.at[t, k],
            recv_sem=recv_sems.at[t, k], device_id=(cx, cy, c), device_id_type=MESH_ID)
            for t in range(n) for k, (cx, cy) in enumerate(chips)]

    return _Exchange(parts, [SDS((3,) + a.shape[1:], a.dtype) for a in parts],
                     [pltpu.SemaphoreType.DMA((n, 3)), pltpu.SemaphoreType.DMA((n, 3))], make)


def _pair_sums(dests, recv, core):
    return [_pair_sum(d, r, core, "rs_pair_sum_" + n) for n, d, r in zip(BIG, dests, recv)]


def _pair_sum(dest, recv, core, name):
    _, na, r, cdim = dest.shape

    def body(core_ref, a_ref, b_ref, o_ref):
        o_ref[...] = (a_ref[...].astype(F32) + b_ref[...].astype(F32)).astype(o_ref.dtype)

    blk = pl.BlockSpec((None, na, r, cdim), lambda j, core_ref: (j, 0, 0, 0))
    return pl.pallas_call(
        body, name=name, out_shape=SDS(recv.shape, BF16),
        grid_spec=pltpu.PrefetchScalarGridSpec(
            num_scalar_prefetch=1, grid=(4,),
            in_specs=[pl.BlockSpec((None, na, r, cdim), lambda j, core_ref: (core_ref[0] * 4 + j, 0, 0, 0)), blk],
            out_specs=blk),
        compiler_params=_params("parallel"),
    )(core, dest, recv)


def _adamw_math(w, g, m, v):
    m = ADAM_B1 * m + (1.0 - ADAM_B1) * g
    v = ADAM_B2 * v + (1.0 - ADAM_B2) * (g * g)
    m_hat = m / (1.0 - ADAM_B1 ** ADAM_STEP)
    v_hat = v / (1.0 - ADAM_B2 ** ADAM_STEP)
    delta = -ADAM_LR * (m_hat / (jnp.sqrt(v_hat) + ADAM_EPS) + ADAM_WD * w)
    return delta, m, v


def _sum_and_adamw(parts, w, m, v, name):
    n, r, ln = parts.shape
    tr = _tile(r, 512, 16)

    def body(p_ref, w_ref, m_ref, v_ref, g_out, d_out, m_out, v_out):
        g = p_ref[0].astype(F32)
        for j in range(1, n):
            g = g + p_ref[j].astype(F32)
        delta, nm, nv = _adamw_math(w_ref[...], g, m_ref[...], v_ref[...])
        g_out[...] = g
        d_out[...] = delta
        m_out[...] = nm
        v_out[...] = nv

    blk = pl.BlockSpec((tr, ln), lambda i: (i, 0))
    return pl.pallas_call(
        body, name=name, grid=(r // tr,),
        in_specs=[pl.BlockSpec((n, tr, ln), lambda i: (0, i, 0)), blk, blk, blk],
        out_specs=(blk, blk, blk, blk), out_shape=tuple(SDS((r, ln), F32) for _ in range(4)),
        compiler_params=_params("parallel"),
    )(parts, w, m, v)


BIG = ("w_in", "w_mem_kv", "w_br", "w_out")
SMALL = ("norm_g", "q_norm_g", "k_norm_g", "sg_ln_g", "sg_ln_b", "w_s", "b_s", "mem_norm_g", "final_g")


def _pack(arrs, row_unit=16):
    flat = jnp.concatenate([a.reshape(-1) for a in arrs])
    pad = (-flat.shape[0]) % (row_unit * LANES)
    if pad:
        flat = jnp.concatenate([flat, jnp.zeros((pad,), flat.dtype)])
    return flat.reshape(-1, LANES)


def _unpack(buf, shapes):
    flat = buf.reshape(-1)
    out, off = [], 0
    for shp in shapes:
        n = 1
        for s_ in shp:
            n *= s_
        out.append(flat[off:off + n].reshape(shp))
        off += n
    return out


def _shard_sum_adamw(part, from_chips, chip, w, m, v, layer, prev, name):
    _, na, r, cdim = part.shape
    flat = w.ndim == 3
    assert not flat or na == 1
    tr = _tile(r, max(8, (256 * 1024) // (na * cdim)), 8)
    n_prev = 0 if prev is None else len(prev)

    def body(chip_ref, p_ref, f_ref, w_ref, m_ref, v_ref, *rest):
        g_out, d_out, m_out, v_out = rest[n_prev:]
        g = p_ref[...].astype(F32)
        for j in range(3):
            g = g + f_ref[j].astype(F32)
        delta, nm, nv = _adamw_math(w_ref[...], g, m_ref[...], v_ref[...])
        g_out[...] = g
        d_out[...] = delta
        m_out[...] = nm
        v_out[...] = nv

    a_blk = None if flat else na
    if flat:
        lay = pl.BlockSpec((None, tr, cdim), lambda i, chip_ref: (layer, i, 0))
    else:
        lay = pl.BlockSpec((None, na, tr, cdim), lambda i, chip_ref: (layer, 0, i, 0))
    return pl.pallas_call(
        body, name=name, out_shape=tuple(SDS(w.shape, F32) for _ in range(4)),
        grid_spec=pltpu.PrefetchScalarGridSpec(
            num_scalar_prefetch=1, grid=(r // tr,),
            in_specs=[pl.BlockSpec((None, a_blk, tr, cdim), lambda i, chip_ref: (chip_ref[0], 0, i, 0)),
                      pl.BlockSpec((3, a_blk, tr, cdim), lambda i, chip_ref: (0, 0, i, 0)), lay, lay, lay]
            + [HBM] * n_prev,
            out_specs=(lay, lay, lay, lay)),
        input_output_aliases={6 + j: j for j in range(n_prev)},
        compiler_params=_params("parallel"),
    )(chip, part, from_chips, w, m, v, *(prev or ()))


def kernel(x, mem, norm_g, w_in, q_norm_g, k_norm_g, sg_ln_g, sg_ln_b, w_s, b_s, mem_norm_g, w_mem_kv, w_br, w_out, final_g, loss_target, m_norm_g, m_w_in, m_q_norm_g, m_k_norm_g, m_sg_ln_g, m_sg_ln_b, m_w_s, m_b_s, m_mem_norm_g, m_w_mem_kv, m_w_br, m_w_out, m_final_g, v_norm_g, v_w_in, v_q_norm_g, v_k_norm_g, v_sg_ln_g, v_sg_ln_b, v_w_s, v_b_s, v_mem_norm_g, v_w_mem_kv, v_w_br, v_w_out, v_final_g):
    wts = dict(norm_g=norm_g, w_in=w_in, q_norm_g=q_norm_g, k_norm_g=k_norm_g, sg_ln_g=sg_ln_g, sg_ln_b=sg_ln_b,
               w_s=w_s, b_s=b_s, mem_norm_g=mem_norm_g, w_mem_kv=w_mem_kv, w_br=w_br, w_out=w_out, final_g=final_g)
    mom1 = dict(norm_g=m_norm_g, w_in=m_w_in, q_norm_g=m_q_norm_g, k_norm_g=m_k_norm_g, sg_ln_g=m_sg_ln_g,
                sg_ln_b=m_sg_ln_b, w_s=m_w_s, b_s=m_b_s, mem_norm_g=m_mem_norm_g, w_mem_kv=m_w_mem_kv, w_br=m_w_br,
                w_out=m_w_out, final_g=m_final_g)
    mom2 = dict(norm_g=v_norm_g, w_in=v_w_in, q_norm_g=v_q_norm_g, k_norm_g=v_k_norm_g, sg_ln_g=v_sg_ln_g,
                sg_ln_b=v_sg_ln_b, w_s=v_w_s, b_s=v_b_s, mem_norm_g=v_mem_norm_g, w_mem_kv=v_w_mem_kv, w_br=v_w_br,
                w_out=v_w_out, final_g=v_final_g)
    dp = w_in.shape[0]
    core = lax.axis_index("c").astype(jnp.int32).reshape(1)
    chip = (2 * lax.axis_index("x") + lax.axis_index("y")).astype(jnp.int32).reshape(1)

    shard_bf = {n: wts[n].astype(BF16) for n in BIG}
    shards = lambda l: [shard_bf[n][l] for n in BIG]
    x_l, mem_l = x[0], mem[0]
    tabs = _rope_tables(x_l.shape[0])

    gathered = _all_gather(shards(0), "weights_all_gather")
    layers, saved = [], []
    for l in range(dp):
        g_in, g_kv, g_br, g_out = gathered
        layers.append(_layer_weights(l, _assemble_w_in(g_in), g_kv.reshape(D_MODEL, -1), _assemble_w_br(g_br),
                                     g_out.reshape(D_MODEL, -1), wts))
        x_l, s, gathered = _layer_fwd(x_l, mem_l, layers[l], tabs, next_shards=shards(l + 1) if l + 1 < dp else None)
        saved.append(s)
    loss_local, dx, d_final_g = _loss_head(x_l, final_g.reshape(1, -1), loss_target[0])
    loss = lax.psum(loss_local[0, 0], AXES)

    def finish(l, parts, from_chips, prev):
        return {n: _shard_sum_adamw(p, f, chip, wts[n], mom1[n], mom2[n], l, None if prev is None else prev[n],
                                    "sum_adamw_" + n)
                for n, p, f in zip(BIG, parts, from_chips)}

    grads, updated, pending = [None] * dp, None, None
    for l in reversed(range(dp)):
        dx, grads[l], scattered, own = _layer_bwd(dx, mem_l, layers[l], saved[l], tabs, pending=pending, core=core,
                                                  scatter_own=(l == 0))
        if scattered is not None:
            updated = finish(l + 1, *scattered, updated)
        pending = [grads[l][n] for n in BIG]
    updated = finish(0, *own, updated)
    grad_x = dx
    big_out = [{n: updated[n][k] for n in BIG} for k in range(4)]

    small_g = {n: jnp.stack([g[n] for g in grads]) for n in SMALL if n != "final_g"}
    small_g["final_g"] = d_final_g
    (all_small,) = _all_gather([_pack([small_g[n] for n in SMALL])], "small_all_gather")
    small_bufs = _sum_and_adamw(
        all_small, _pack([wts[n] for n in SMALL]), _pack([mom1[n] for n in SMALL]), _pack([mom2[n] for n in SMALL]),
        "small_sum_adamw")

    outs = []
    for big_vals, small_buf in zip(big_out, small_bufs):
        vals = dict(big_vals)
        vals.update(zip(SMALL, _unpack(small_buf, [wts[n].shape for n in SMALL])))
        outs.append(vals)
    order = ("norm_g", "w_in", "q_norm_g", "k_norm_g", "sg_ln_g", "sg_ln_b", "w_s", "b_s", "mem_norm_g", "w_mem_kv",
             "w_br", "w_out", "final_g")
    result = [loss, grad_x[None]]
    for vals in outs:
        result += [vals[n] for n in order]
    return tuple(result)
```

```python
import functools

import jax
import jax.numpy as jnp
from jax import lax
from jax.experimental import pallas as pl
from jax.experimental.pallas import tpu as pltpu

F32 = jnp.float32
BF16 = jnp.bfloat16
SDS = jax.ShapeDtypeStruct
MESH_ID = pl.DeviceIdType.MESH
AXES = ("x", "y", "c")
N_DEV = 8

D_MODEL = 1024
DEPTH = 4
GRID_W = 64
CHUNK = 128
ROPE_THETA = 10000.0
EPS = 1e-6
HEAD_DIM = 64
A_HEADS = 8
A_KV_HEADS = 2
B_GROUPS = 4
M_HEADS = 4
M_HEAD_DIM = 128
BW = 512
W_A = 1280
W_B = 1536
W_M = 1024
W_G = 3072
IN_WIDTH = W_A + W_B + W_M + W_G
GROUP_OFFS = (0, W_A, W_A + W_B, W_A + W_B + W_M, IN_WIDTH)

ADAM_LR = 0.001
ADAM_B1 = 0.9
ADAM_B2 = 0.999
ADAM_EPS = 1e-08
ADAM_WD = 0.01
ADAM_STEP = 10

LANES = 128
VMEM_LIMIT = 52 * 1024 * 1024


def _tile(n, cap, unit=LANES):
    if n <= cap:
        return n
    t = (cap // unit) * unit
    while n % t:
        t -= unit
    return t


def _params(*sem):
    return pltpu.CompilerParams(dimension_semantics=sem, vmem_limit_bytes=VMEM_LIMIT)


def _sigmoid(z):
    return 1.0 / (1.0 + jnp.exp(-z))


def _silu_and_grad(z):
    s = _sigmoid(z)
    return z * s, s * (1.0 + z * (1.0 - s))


def _dot(a, b, dims):
    return lax.dot_general(a, b, (dims, ((), ())), preferred_element_type=F32)


NN = ((1,), (0,))
NT = ((1,), (1,))
TN = ((0,), (0,))
HBM = pl.BlockSpec(memory_space=pl.ANY)


class _Exchange:
    def __init__(self, ins, out_shapes, sems, make, aliases=None):
        self.ins, self.out_shapes, self.sems, self.make = list(ins), list(out_shapes), list(sems), make
        self.aliases = dict(aliases or {})

    def start(self, in_refs, out_refs, sems):
        for cp in self.make(in_refs, out_refs, sems):
            cp.start()

    def finish(self, in_refs, out_refs, sems):
        for cp in self.make(in_refs, out_refs, sems):
            cp.wait()


def _host_call(body, ins, in_specs, out_specs, out_shape, *, name, grid, semantics, scratch=(), exchange=None):
    ins, out_shape, scratch = list(ins), list(out_shape), list(scratch)
    if exchange is None:
        res = pl.pallas_call(
            body, name=name, grid=grid, in_specs=list(in_specs), out_specs=tuple(out_specs), out_shape=tuple(out_shape),
            scratch_shapes=scratch, compiler_params=_params(*semantics))(*ins)
        return tuple(res), ()
    n_in, n_out, n_scr = len(ins), len(out_shape), len(scratch)
    x_in, x_out = len(exchange.ins), len(exchange.out_shapes)

    def carrying(*refs):
        o0 = n_in + x_in
        s0 = o0 + n_out + x_out
        c_in, c_out, c_sems = refs[n_in:o0], refs[o0 + n_out:s0], refs[s0 + n_scr:]
        ids = [pl.program_id(a) for a in range(len(grid))]
        first = functools.reduce(jnp.logical_and, [i == 0 for i in ids])
        last = functools.reduce(jnp.logical_and, [i == g - 1 for i, g in zip(ids, grid)])

        @pl.when(first)
        def _():
            exchange.start(c_in, c_out, c_sems)

        body(*refs[:n_in], *refs[o0:o0 + n_out], *refs[s0:s0 + n_scr])

        @pl.when(last)
        def _():
            exchange.finish(c_in, c_out, c_sems)

    res = pl.pallas_call(
        carrying, name=name, grid=grid, in_specs=list(in_specs) + [HBM] * x_in,
        out_specs=tuple(out_specs) + (HBM,) * x_out, out_shape=tuple(out_shape) + tuple(exchange.out_shapes),
        scratch_shapes=scratch + exchange.sems,
        input_output_aliases={n_in + i: n_out + o for i, o in exchange.aliases.items()},
        compiler_params=_params(*(["arbitrary"] * len(grid))))(*ins, *exchange.ins)
    return tuple(res[:n_out]), tuple(res[n_out:])


def _exchange_call(exchange, name):
    x_in = len(exchange.ins)

    def body(*refs):
        x_out = len(exchange.out_shapes)
        c_in, c_out, c_sems = refs[:x_in], refs[x_in:x_in + x_out], refs[x_in + x_out:]
        exchange.start(c_in, c_out, c_sems)
        exchange.finish(c_in, c_out, c_sems)

    return pl.pallas_call(
        body, name=name, in_specs=[HBM] * x_in, out_specs=tuple([HBM] * len(exchange.out_shapes)),
        out_shape=tuple(exchange.out_shapes), scratch_shapes=exchange.sems, input_output_aliases=exchange.aliases,
    )(*exchange.ins)


def _core_major(d):
    return (d % 2) * 4 + d // 2


def _mm(a, b, mode, name, add=None, out_dtype=F32, tm_cap=512, tn_cap=1536, tk_cap=1536):
    if mode == "nn":
        (m, k), (k2, n) = a.shape, b.shape
    elif mode == "nt":
        (m, k), (n, k2) = a.shape, b.shape
    else:
        (k, m), (k2, n) = a.shape, b.shape
    assert k == k2, (a.shape, b.shape, mode)
    tm = _tile(m, tm_cap, 8 if mode != "tn" else LANES)
    tn = _tile(n, tn_cap)
    tk = _tile(k, tk_cap, LANES if mode != "tn" else 16)
    nk = k // tk
    dims = {"nn": NN, "nt": NT, "tn": TN}[mode]
    a_spec = {"nn": pl.BlockSpec((tm, tk), lambda i, j, kk: (i, kk)),
              "nt": pl.BlockSpec((tm, tk), lambda i, j, kk: (i, kk)),
              "tn": pl.BlockSpec((tk, tm), lambda i, j, kk: (kk, i))}[mode]
    b_spec = {"nn": pl.BlockSpec((tk, tn), lambda i, j, kk: (kk, j)),
              "nt": pl.BlockSpec((tn, tk), lambda i, j, kk: (j, kk)),
              "tn": pl.BlockSpec((tk, tn), lambda i, j, kk: (kk, j))}[mode]
    o_spec = pl.BlockSpec((tm, tn), lambda i, j, kk: (i, j))
    has_add = add is not None

    def body(*refs):
        a_ref, b_ref = refs[0], refs[1]
        add_ref = refs[2] if has_add else None
        o_ref = refs[3] if has_add else refs[2]
        part = _dot(a_ref[...].astype(BF16), b_ref[...].astype(BF16), dims)
        if nk == 1:
            if has_add:
                part = part + add_ref[...]
            o_ref[...] = part.astype(out_dtype)
        else:
            acc = refs[-1]
            kk = pl.program_id(2)

            @pl.when(kk == 0)
            def _():
                acc[...] = part + add_ref[...] if has_add else part

            @pl.when(kk > 0)
            def _():
                acc[...] += part

            @pl.when(kk == nk - 1)
            def _():
                o_ref[...] = acc[...].astype(out_dtype)

    ins = [a, b] + ([add] if has_add else [])
    in_specs = [a_spec, b_spec] + ([o_spec] if has_add else [])
    return pl.pallas_call(
        body, name=name, grid=(m // tm, n // tn, nk), in_specs=in_specs, out_specs=o_spec,
        out_shape=SDS((m, n), out_dtype),
        scratch_shapes=[pltpu.VMEM((tm, tn), F32)] if nk > 1 else [],
        compiler_params=_params("parallel", "parallel", "arbitrary"),
    )(*ins)


def _mm_tn_by_owner(a, b, name):
    (k, m), (k2, n) = a.shape, b.shape
    assert k == k2 and m % N_DEV == 0, (a.shape, b.shape)
    tk = _tile(k, 512, 16)
    nk = k // tk
    rows = m // N_DEV

    def body(a_ref, b_ref, o_ref, acc):
        kk = pl.program_id(0)
        part = _dot(a_ref[...].astype(BF16), b_ref[...].astype(BF16), TN)

        @pl.when(kk == 0)
        def _():
            acc[...] = part

        @pl.when(kk > 0)
        def _():
            acc[...] += part

        @pl.when(kk == nk - 1)
        def _():
            for d in range(N_DEV):
                pos = _core_major(d)
                o_ref[pos * rows:(pos + 1) * rows, :] = acc[d * rows:(d + 1) * rows, :].astype(BF16)

    return pl.pallas_call(
        body, name=name, grid=(nk,),
        in_specs=[pl.BlockSpec((tk, m), lambda kk: (kk, 0)), pl.BlockSpec((tk, n), lambda kk: (kk, 0))],
        out_specs=pl.BlockSpec((m, n), lambda kk: (0, 0)), out_shape=SDS((m, n), BF16),
        scratch_shapes=[pltpu.VMEM((m, n), F32)], compiler_params=_params("arbitrary"),
    )(a, b)


def _rmsnorm_fwd(x, g, name):
    t, d = x.shape
    tm = _tile(t, 512, 8)

    def body(x_ref, g_ref, h_ref):
        xf = x_ref[...]
        r = lax.rsqrt(jnp.mean(xf * xf, axis=-1, keepdims=True) + EPS)
        h_ref[...] = (xf * r * g_ref[...]).astype(BF16)

    return pl.pallas_call(
        body, name=name, grid=(t // tm,),
        in_specs=[pl.BlockSpec((tm, d), lambda i: (i, 0)), pl.BlockSpec((1, d), lambda i: (0, 0))],
        out_specs=pl.BlockSpec((tm, d), lambda i: (i, 0)),
        out_shape=SDS((t, d), BF16), compiler_params=_params("parallel"),
    )(x, g)


def _rmsnorm_bwd_math(xf, g, dh):
    r = lax.rsqrt(jnp.mean(xf * xf, axis=-1, keepdims=True) + EPS)
    xh = xf * r
    gd = dh * g
    dx = r * (gd - xh * jnp.mean(gd * xh, axis=-1, keepdims=True))
    return dx, jnp.sum(dh * xh, axis=0, keepdims=True)


def _rmsnorm_dg(x, g, dh, name):
    t, d = x.shape
    tm = _tile(t, 512, 8)

    def body(x_ref, g_ref, dh_ref, dg_ref):
        @pl.when(pl.program_id(0) == 0)
        def _():
            dg_ref[...] = jnp.zeros_like(dg_ref)

        dg_ref[...] += _rmsnorm_bwd_math(x_ref[...], g_ref[...], dh_ref[...])[1]

    row = pl.BlockSpec((tm, d), lambda i: (i, 0))
    vec = pl.BlockSpec((1, d), lambda i: (0, 0))
    return pl.pallas_call(
        body, name=name, grid=(t // tm,), in_specs=[row, vec, row], out_specs=vec, out_shape=SDS((1, d), F32),
        compiler_params=_params("arbitrary"),
    )(x, g, dh)


def _group_rows(p):
    return slice(GROUP_OFFS[p], GROUP_OFFS[p + 1])


def _in_proj(h, w_t):
    t, d = h.shape
    tm = _tile(t, 512, 16)
    widths = [GROUP_OFFS[p + 1] - GROUP_OFFS[p] for p in range(4)]

    def body(h_ref, w_ref, *outs):
        hb = h_ref[...]
        for p, o_ref in enumerate(outs):
            o_ref[...] = _dot(hb, w_ref[_group_rows(p), :], NT).astype(BF16)

    return pl.pallas_call(
        body, name="in_proj", grid=(t // tm,),
        in_specs=[pl.BlockSpec((tm, d), lambda i: (i, 0)),
                  pl.BlockSpec(w_t.shape, lambda i: (0, 0), pipeline_mode=pl.Buffered(1))],
        out_specs=tuple(pl.BlockSpec((tm, wd), lambda i: (i, 0)) for wd in widths),
        out_shape=tuple(SDS((t, wd), BF16) for wd in widths), compiler_params=_params("parallel"),
    )(h, w_t)


def _dh_rmsnorm_bwd(dps, w_t, x, g, dres, exchange=None):
    t, d = x.shape
    tm = _tile(t, 256, 16)
    n = len(dps)

    def body(*refs):
        dp_refs, w_ref = refs[:n], refs[n]
        x_ref, g_ref, dres_ref, dx_ref, dg_ref = refs[n + 1:]
        dh = None
        for p, dp_ref in enumerate(dp_refs):
            part = _dot(dp_ref[...], w_ref[_group_rows(p), :], NN)
            dh = part if dh is None else dh + part
        dx, dg = _rmsnorm_bwd_math(x_ref[...], g_ref[...], dh)
        dx_ref[...] = dx + dres_ref[...]

        @pl.when(pl.program_id(0) == 0)
        def _():
            dg_ref[...] = jnp.zeros_like(dg_ref)

        dg_ref[...] += dg

    row = pl.BlockSpec((tm, d), lambda i: (i, 0))
    vec = pl.BlockSpec((1, d), lambda i: (0, 0))
    return _host_call(
        body, list(dps) + [w_t, x, g, dres],
        [pl.BlockSpec((tm, a.shape[1]), lambda i: (i, 0)) for a in dps]
        + [pl.BlockSpec(w_t.shape, lambda i: (0, 0), pipeline_mode=pl.Buffered(1)), row, vec, row],
        (row, vec), (SDS((t, d), F32), SDS((1, d), F32)),
        name="dh_rmsnorm_bwd", grid=(t // tm,), semantics=("arbitrary",), exchange=exchange)


def _rope_tables(t):
    rows = t // GRID_W
    row = jnp.repeat(jnp.arange(rows, dtype=F32), GRID_W)
    col = jnp.tile(jnp.arange(GRID_W, dtype=F32), rows)
    n_freq = HEAD_DIM // 4
    inv = ROPE_THETA ** (-jnp.arange(n_freq, dtype=F32) / n_freq)
    ang = jnp.stack([row[:, None] * inv, col[:, None] * inv], axis=1)
    cos, sin = jnp.cos(ang), jnp.sin(ang)
    c64 = jnp.concatenate([cos[:, 0], cos[:, 0], cos[:, 1], cos[:, 1]], axis=-1)
    s64 = jnp.concatenate([-sin[:, 0], sin[:, 0], -sin[:, 1], sin[:, 1]], axis=-1)
    return jnp.tile(c64, (1, 2)), jnp.tile(s64, (1, 2))


def _head_sums(v, lane):
    lo = jnp.sum(jnp.where(lane < HEAD_DIM, v, 0.0), axis=-1, keepdims=True)
    hi = jnp.sum(jnp.where(lane < HEAD_DIM, 0.0, v), axis=-1, keepdims=True)
    return jnp.where(lane < HEAD_DIM, lo, hi)


def _swap16(v, lane):
    return jnp.where((lane % 32) < 16, pltpu.roll(v, LANES - 16, 1), pltpu.roll(v, 16, 1))


def _attn_prep_fwd(pa, cs, sn, qg2, kg2):
    t = pa.shape[0]
    tq = _tile(t, 512, LANES)
    scale = HEAD_DIM ** -0.5

    def body(pa_ref, cs_ref, sn_ref, qg_ref, kg_ref, q_ref, qT_ref, k_ref, kT_ref, vT_ref):
        lane = lax.broadcasted_iota(jnp.int32, (tq, LANES), 1)
        c, s = cs_ref[...], sn_ref[...]

        def norm_rope(xg, g2):
            r = lax.rsqrt(_head_sums(xg * xg, lane) * (1.0 / HEAD_DIM) + EPS)
            xn = xg * r * g2
            return xn * c + _swap16(xn, lane) * s

        for gi in range(4):
            sl = slice(gi * LANES, (gi + 1) * LANES)
            qr = norm_rope(pa_ref[:, sl].astype(F32), qg_ref[...]) * scale
            q_ref[:, sl] = qr.astype(BF16)
            qT_ref[sl, :] = qr.T.astype(BF16)
        kr = norm_rope(pa_ref[:, 512:640].astype(F32), kg_ref[...])
        kT_ref[...] = kr.T.astype(BF16)
        vT_ref[...] = pa_ref[:, 640:768].astype(F32).T.astype(BF16)
        kr = kr.astype(BF16)
        for kvh in range(A_KV_HEADS):
            k_ref[kvh] = kr[:, kvh * HEAD_DIM:(kvh + 1) * HEAD_DIM]

    row = lambda w: pl.BlockSpec((tq, w), lambda i: (i, 0))
    col = lambda r: pl.BlockSpec((r, tq), lambda i: (0, i))
    vec = pl.BlockSpec((1, LANES), lambda i: (0, 0))
    hm = pl.BlockSpec((A_KV_HEADS, tq, HEAD_DIM), lambda i: (0, i, 0))
    return pl.pallas_call(
        body, name="attn_prep_fwd", grid=(t // tq,),
        in_specs=[row(W_A), row(LANES), row(LANES), vec, vec],
        out_specs=(row(BW), col(BW), hm, col(LANES), col(LANES)),
        out_shape=(SDS((t, BW), BF16), SDS((BW, t), BF16), SDS((A_KV_HEADS, t, HEAD_DIM), BF16),
                   SDS((LANES, t), BF16), SDS((LANES, t), BF16)),
        compiler_params=_params("parallel"),
    )(pa, cs, sn, qg2, kg2)


def _attn_fwd(qT, k, vT, pa, exchange=None):
    t = qT.shape[1]
    tq = _tile(t, 256, LANES)
    grp = A_HEADS // A_KV_HEADS

    def body(qT_ref, k_ref, vT_ref, pa_ref, o_ref, lse_ref, y_ref):
        def scores(h):
            return _dot(k_ref[h // grp], qT_ref[h * HEAD_DIM:(h + 1) * HEAD_DIM, :], NN)

        sT, o_pair = scores(0), []
        for h in range(A_HEADS):
            kvh = h // grp
            sT_next = scores(h + 1) if h + 1 < A_HEADS else None
            m = jnp.max(sT, axis=0, keepdims=True)
            pT = jnp.exp(sT - m)
            l = jnp.sum(pT, axis=0, keepdims=True)
            oT = _dot(vT_ref[kvh * HEAD_DIM:(kvh + 1) * HEAD_DIM, :], pT.astype(BF16), NN)
            o_pair.append(oT / l)
            lse_ref[h:h + 1, :] = m + jnp.log(l)
            sT = sT_next
            if h % 2:
                o2 = jnp.concatenate(o_pair, axis=0).T
                o_pair = []
                cols = slice((h // 2) * LANES, (h // 2 + 1) * LANES)
                o_ref[:, cols] = o2
                z = pa_ref[:, 768 + (h // 2) * LANES:768 + (h // 2 + 1) * LANES].astype(F32)
                y_ref[:, cols] = (o2 * (z * _sigmoid(z))).astype(BF16)

    row = lambda w: pl.BlockSpec((tq, w), lambda i: (i, 0))
    return _host_call(
        body, (qT, k, vT, pa),
        [pl.BlockSpec((BW, tq), lambda i: (0, i)), pl.BlockSpec((A_KV_HEADS, t, HEAD_DIM), lambda i: (0, 0, 0)),
         pl.BlockSpec((A_KV_HEADS * HEAD_DIM, t), lambda i: (0, 0)), row(W_A)],
        (row(BW), pl.BlockSpec((A_HEADS, tq), lambda i: (0, i)), row(BW)),
        (SDS((t, BW), F32), SDS((A_HEADS, t), F32), SDS((t, BW), BF16)),
        name="attn_fwd", grid=(t // tq,), semantics=("parallel",), exchange=exchange)


def _attn_bwd(q, qT, k, kT, vT, pa, o, lse, dy, exchange=None):
    t = q.shape[0]
    tq = _tile(t, 256, LANES)
    grp = A_HEADS // A_KV_HEADS
    gw = grp * HEAD_DIM

    def body(q_ref, qT_ref, k_ref, kT_ref, vT_ref, z_ref, o_ref, lse_ref, dy_ref, dq_ref, dkT_ref, dvT_ref):
        @pl.when(pl.program_id(1) == 0)
        def _():
            dkT_ref[...] = jnp.zeros_like(dkT_ref)
            dvT_ref[...] = jnp.zeros_like(dvT_ref)

        z = z_ref[...].astype(F32)
        do = dy_ref[...] * (z * _sigmoid(z))
        doo = do * o_ref[...]
        doT = do.T
        kk, kT, vT = k_ref[...], kT_ref[...], vT_ref[...]
        for j in range(grp):
            hs = slice(j * HEAD_DIM, (j + 1) * HEAD_DIM)
            delta = jnp.sum(doo[:, hs], axis=-1, keepdims=True)
            s = _dot(q_ref[:, hs], kT, NN)
            p = jnp.exp(s - lse_ref[:, j:j + 1])
            dp = _dot(do[:, hs].astype(BF16), vT, NN)
            ds_b = (p * (dp - delta)).astype(BF16)
            dq_ref[:, hs] = _dot(ds_b, kk, NN)
            dkT_ref[...] += _dot(qT_ref[hs, :], ds_b, NN)
            dvT_ref[...] += _dot(doT[hs, :].astype(BF16), p.astype(BF16), NN)

    grp_blk = pl.BlockSpec((tq, gw), lambda g, i: (i, g))
    kvT_blk = pl.BlockSpec((HEAD_DIM, t), lambda g, i: (g, 0))
    acc_blk = pl.BlockSpec((None, HEAD_DIM, t), lambda g, i: (g, 0, 0))
    return _host_call(
        body, (q, qT, k, kT, vT, pa, o, lse, dy),
        [grp_blk, pl.BlockSpec((gw, tq), lambda g, i: (g, i)), pl.BlockSpec((None, t, HEAD_DIM), lambda g, i: (g, 0, 0)),
         kvT_blk, kvT_blk, pl.BlockSpec((tq, gw), lambda g, i: (i, 768 // gw + g)), grp_blk,
         pl.BlockSpec((None, tq, grp), lambda g, i: (g, i, 0)), grp_blk],
        (grp_blk, acc_blk, acc_blk),
        (SDS((t, BW), F32), SDS((A_KV_HEADS, HEAD_DIM, t), F32), SDS((A_KV_HEADS, HEAD_DIM, t), F32)),
        name="attn_bwd", grid=(A_KV_HEADS, t // tq), semantics=("arbitrary", "arbitrary"), exchange=exchange)


def _attn_prep_bwd(pa, cs, sn, qg2, kg2, dq, dkT, dvT, dy, o):
    t = pa.shape[0]
    tq = _tile(t, 512, LANES)
    scale = HEAD_DIM ** -0.5

    def body(pa_ref, cs_ref, sn_ref, qg_ref, kg_ref, dq_ref, dkT_ref, dvT_ref, dy_ref, o_ref, dpa_ref, dqg_ref, dkg_ref):
        lane = lax.broadcasted_iota(jnp.int32, (tq, LANES), 1)
        c, s = cs_ref[...], sn_ref[...]

        @pl.when(pl.program_id(0) == 0)
        def _():
            dqg_ref[...] = jnp.zeros_like(dqg_ref)
            dkg_ref[...] = jnp.zeros_like(dkg_ref)

        def norm_rope_bwd(xg, g2, dout):
            r = lax.rsqrt(_head_sums(xg * xg, lane) * (1.0 / HEAD_DIM) + EPS)
            xh = xg * r
            dxn = dout * c + _swap16(dout * s, lane)
            gd = dxn * g2
            dx = r * (gd - xh * (_head_sums(gd * xh, lane) * (1.0 / HEAD_DIM)))
            return dx, jnp.sum(dxn * xh, axis=0, keepdims=True)

        for gi in range(4):
            sl = slice(gi * LANES, (gi + 1) * LANES)
            dx, dg = norm_rope_bwd(pa_ref[:, sl].astype(F32), qg_ref[...], dq_ref[:, sl] * scale)
            dpa_ref[:, sl] = dx.astype(BF16)
            dqg_ref[...] += dg
        dx, dg = norm_rope_bwd(pa_ref[:, 512:640].astype(F32), kg_ref[...], dkT_ref[...].T)
        dpa_ref[:, 512:640] = dx.astype(BF16)
        dkg_ref[...] += dg
        dpa_ref[:, 640:768] = dvT_ref[...].T.astype(BF16)
        z = pa_ref[:, 768:1280].astype(F32)
        _, dsilu = _silu_and_grad(z)
        dpa_ref[:, 768:1280] = (dy_ref[...] * o_ref[...] * dsilu).astype(BF16)

    row = lambda w: pl.BlockSpec((tq, w), lambda i: (i, 0))
    col = pl.BlockSpec((LANES, tq), lambda i: (0, i))
    vec = pl.BlockSpec((1, LANES), lambda i: (0, 0))
    return pl.pallas_call(
        body, name="attn_prep_bwd", grid=(t // tq,),
        in_specs=[row(W_A), row(LANES), row(LANES), vec, vec, row(BW), col, col, row(BW), row(BW)],
        out_specs=(row(W_A), vec, vec),
        out_shape=(SDS((t, W_A), BF16), SDS((1, LANES), F32), SDS((1, LANES), F32)),
        compiler_params=_params("arbitrary"),
    )(pa, cs, sn, qg2, kg2, dq, dkT, dvT, dy, o)


def _layer_norm(v, g, b):
    mu = jnp.mean(v, axis=-1, keepdims=True)
    xc = v - mu
    rs = lax.rsqrt(jnp.mean(xc * xc, axis=-1, keepdims=True) + EPS)
    xh = xc * rs
    return xh * g + b, xh, rs


def _gmlp_fwd(pb, lng, lnb, ws, bs):
    t = pb.shape[0]
    tb = _tile(t, 512, CHUNK)

    def body(pb_ref, g_ref, b_ref, ws_ref, bs_ref, y_ref):
        vln, _, _ = _layer_norm(pb_ref[:, BW:2 * BW].astype(F32), g_ref[...], b_ref[...])
        vb = vln.astype(BF16)
        for gi in range(B_GROUPS):
            w = ws_ref[gi].astype(BF16)
            cs_ = slice(gi * CHUNK, (gi + 1) * CHUNK)
            for n in range(tb // CHUNK):
                rs_ = slice(n * CHUNK, (n + 1) * CHUNK)
                mixed = _dot(w, vb[rs_, cs_], NN) + bs_ref[gi]
                z = pb_ref[rs_, 2 * BW + gi * CHUNK:2 * BW + (gi + 1) * CHUNK].astype(F32)
                y_ref[rs_, cs_] = (pb_ref[rs_, cs_].astype(F32) * mixed * (z * _sigmoid(z))).astype(BF16)

    return pl.pallas_call(
        body, name="gmlp_fwd", grid=(t // tb,),
        in_specs=[pl.BlockSpec((tb, W_B), lambda i: (i, 0)),
                  pl.BlockSpec((1, BW), lambda i: (0, 0)), pl.BlockSpec((1, BW), lambda i: (0, 0)),
                  pl.BlockSpec((B_GROUPS, CHUNK, CHUNK), lambda i: (0, 0, 0)),
                  pl.BlockSpec((B_GROUPS, CHUNK, 1), lambda i: (0, 0, 0))],
        out_specs=pl.BlockSpec((tb, BW), lambda i: (i, 0)),
        out_shape=SDS((t, BW), BF16), compiler_params=_params("parallel"),
    )(pb, lng, lnb, ws, bs)


def _gmlp_bwd(pb, lng, lnb, ws, bs, dy):
    t = pb.shape[0]
    tb = _tile(t, 256, CHUNK)

    def body(pb_ref, g_ref, b_ref, ws_ref, bs_ref, dy_ref, dpb_ref, dws_ref, dbs_ref, dg_ref, db_ref, dvln_ref):
        @pl.when(pl.program_id(0) == 0)
        def _():
            dws_ref[...] = jnp.zeros_like(dws_ref)
            dbs_ref[...] = jnp.zeros_like(dbs_ref)
            dg_ref[...] = jnp.zeros_like(dg_ref)
            db_ref[...] = jnp.zeros_like(db_ref)

        vln, xh, rs = _layer_norm(pb_ref[:, BW:2 * BW].astype(F32), g_ref[...], b_ref[...])
        vb = vln.astype(BF16)
        for gi in range(B_GROUPS):
            w = ws_ref[gi].astype(BF16)
            cs_ = slice(gi * CHUNK, (gi + 1) * CHUNK)
            for n in range(tb // CHUNK):
                rs_ = slice(n * CHUNK, (n + 1) * CHUNK)
                vbc = vb[rs_, cs_]
                mixed = _dot(w, vbc, NN) + bs_ref[gi]
                zs = slice(2 * BW + gi * CHUNK, 2 * BW + (gi + 1) * CHUNK)
                z = pb_ref[rs_, zs].astype(F32)
                u = pb_ref[rs_, cs_].astype(F32)
                sil, dsil = _silu_and_grad(z)
                dyc = dy_ref[rs_, cs_]
                dmixed = dyc * u * sil
                dpb_ref[rs_, cs_] = (dyc * mixed * sil).astype(BF16)
                dpb_ref[rs_, zs] = (dyc * u * mixed * dsil).astype(BF16)
                dmb = dmixed.astype(BF16)
                dws_ref[gi] += _dot(dmb, vbc, NT)
                dbs_ref[gi] += jnp.sum(dmixed, axis=-1, keepdims=True)
                dvln_ref[rs_, cs_] = _dot(w, dmb, TN)
        dvln = dvln_ref[...]
        dg_ref[...] += jnp.sum(dvln * xh, axis=0, keepdims=True)
        db_ref[...] += jnp.sum(dvln, axis=0, keepdims=True)
        gd = dvln * g_ref[...]
        dv = rs * (gd - jnp.mean(gd, axis=-1, keepdims=True) - xh * jnp.mean(gd * xh, axis=-1, keepdims=True))
        dpb_ref[:, BW:2 * BW] = dv.astype(BF16)

    vec = pl.BlockSpec((1, BW), lambda i: (0, 0))
    wsb = pl.BlockSpec((B_GROUPS, CHUNK, CHUNK), lambda i: (0, 0, 0))
    bsb = pl.BlockSpec((B_GROUPS, CHUNK, 1), lambda i: (0, 0, 0))
    return pl.pallas_call(
        body, name="gmlp_bwd", grid=(t // tb,),
        in_specs=[pl.BlockSpec((tb, W_B), lambda i: (i, 0)), vec, vec, wsb, bsb, pl.BlockSpec((tb, BW), lambda i: (i, 0))],
        out_specs=(pl.BlockSpec((tb, W_B), lambda i: (i, 0)), wsb, bsb, vec, vec),
        out_shape=(SDS((t, W_B), BF16), SDS((B_GROUPS, CHUNK, CHUNK), F32), SDS((B_GROUPS, CHUNK, 1), F32),
                   SDS((1, BW), F32), SDS((1, BW), F32)),
        scratch_shapes=[pltpu.VMEM((tb, BW), F32)],
        compiler_params=_params("arbitrary"),
    )(pb, lng, lnb, ws, bs, dy)


def _mem_softmax(qh, kh):
    s = _dot(qh, kh, NT) * (M_HEAD_DIM ** -0.5)
    e = jnp.exp(s - jnp.max(s, axis=-1, keepdims=True))
    return e / jnp.sum(e, axis=-1, keepdims=True)


def _mem_attn_fwd(pm, kv):
    t = pm.shape[0]
    tq = _tile(t, 512, 8)
    ml = kv.shape[0]

    def body(pm_ref, kv_ref, y_ref):
        for h in range(M_HEADS):
            hs = slice(h * M_HEAD_DIM, (h + 1) * M_HEAD_DIM)
            kh = kv_ref[:, hs].astype(BF16)
            vh = kv_ref[:, BW + h * M_HEAD_DIM:BW + (h + 1) * M_HEAD_DIM].astype(BF16)
            p = _mem_softmax(pm_ref[:, hs].astype(BF16), kh)
            o = _dot(p.astype(BF16), vh, NN)
            z = pm_ref[:, BW + h * M_HEAD_DIM:BW + (h + 1) * M_HEAD_DIM].astype(F32)
            y_ref[:, hs] = (o * (z * _sigmoid(z))).astype(BF16)

    return pl.pallas_call(
        body, name="mem_attn_fwd", grid=(t // tq,),
        in_specs=[pl.BlockSpec((tq, W_M), lambda i: (i, 0)), pl.BlockSpec((ml, 2 * BW), lambda i: (0, 0))],
        out_specs=pl.BlockSpec((tq, BW), lambda i: (i, 0)),
        out_shape=SDS((t, BW), BF16), compiler_params=_params("parallel"),
    )(pm, kv)


def _mem_attn_bwd(pm, kv, dy):
    t = pm.shape[0]
    tq = _tile(t, 512, 8)
    ml = kv.shape[0]
    scale = M_HEAD_DIM ** -0.5

    def body(pm_ref, kv_ref, dy_ref, dpm_ref, dkv_ref):
        @pl.when(pl.program_id(0) == 0)
        def _():
            dkv_ref[...] = jnp.zeros_like(dkv_ref)

        for h in range(M_HEADS):
            hs = slice(h * M_HEAD_DIM, (h + 1) * M_HEAD_DIM)
            zs = slice(BW + h * M_HEAD_DIM, BW + (h + 1) * M_HEAD_DIM)
            kh = kv_ref[:, hs].astype(BF16)
            vh = kv_ref[:, zs].astype(BF16)
            qh = pm_ref[:, hs].astype(BF16)
            p = _mem_softmax(qh, kh)
            pb = p.astype(BF16)
            o = _dot(pb, vh, NN)
            sil, dsil = _silu_and_grad(pm_ref[:, zs].astype(F32))
            dyh = dy_ref[:, hs]
            do = dyh * sil
            dpm_ref[:, zs] = (dyh * o * dsil).astype(BF16)
            delta = jnp.sum(do * o, axis=-1, keepdims=True)
            do_b = do.astype(BF16)
            dp = _dot(do_b, vh, NT)
            dr_b = (p * (dp - delta) * scale).astype(BF16)
            dpm_ref[:, hs] = _dot(dr_b, kh, NN).astype(BF16)
            dkv_ref[:, hs] += _dot(dr_b, qh, TN)
            dkv_ref[:, zs] += _dot(pb, do_b, TN)

    kvb = pl.BlockSpec((ml, 2 * BW), lambda i: (0, 0))
    return pl.pallas_call(
        body, name="mem_attn_bwd", grid=(t // tq,),
        in_specs=[pl.BlockSpec((tq, W_M), lambda i: (i, 0)), kvb, pl.BlockSpec((tq, BW), lambda i: (i, 0))],
        out_specs=(pl.BlockSpec((tq, W_M), lambda i: (i, 0)), kvb),
        out_shape=(SDS((t, W_M), BF16), SDS((ml, 2 * BW), F32)),
        compiler_params=_params("arbitrary"),
    )(pm, kv, dy)


def _merge_fwd(ya, yb, ym, pg, wbr, exchange=None):
    t = ya.shape[0]
    tm = _tile(t, 512, 8)

    def body(ya_ref, yb_ref, ym_ref, pg_ref, w_ref, m_ref):
        acc = None
        for n, y_ref in enumerate((ya_ref, yb_ref, ym_ref)):
            up = _dot(y_ref[...], w_ref[n], NN)
            term = _sigmoid(pg_ref[:, n * D_MODEL:(n + 1) * D_MODEL].astype(F32)) * up
            acc = term if acc is None else acc + term
        m_ref[...] = acc.astype(BF16)

    yb_spec = pl.BlockSpec((tm, BW), lambda i: (i, 0))
    return _host_call(
        body, (ya, yb, ym, pg, wbr),
        [yb_spec, yb_spec, yb_spec, pl.BlockSpec((tm, W_G), lambda i: (i, 0)),
         pl.BlockSpec((3, BW, D_MODEL), lambda i: (0, 0, 0))],
        (pl.BlockSpec((tm, D_MODEL), lambda i: (i, 0)),), (SDS((t, D_MODEL), BF16),),
        name="merge_fwd", grid=(t // tm,), semantics=("parallel",), exchange=exchange)


def _merge_bwd(ya, yb, ym, pg, wbr, dm, exchange=None):
    t = ya.shape[0]
    tm = _tile(t, 256, 8)

    def body(ya_ref, yb_ref, ym_ref, pg_ref, w_ref, dm_ref, dya_ref, dyb_ref, dym_ref, dpg_ref, dw_ref):
        @pl.when(pl.program_id(0) == 0)
        def _():
            dw_ref[...] = jnp.zeros_like(dw_ref)

        dmf = dm_ref[...]
        for n, (y_ref, dy_ref) in enumerate(((ya_ref, dya_ref), (yb_ref, dyb_ref), (ym_ref, dym_ref))):
            cs_ = slice(n * D_MODEL, (n + 1) * D_MODEL)
            y = y_ref[...]
            w = w_ref[n]
            up = _dot(y, w, NN)
            gt = _sigmoid(pg_ref[:, cs_].astype(F32))
            dpg_ref[:, cs_] = (dmf * up * gt * (1.0 - gt)).astype(BF16)
            dup = (dmf * gt).astype(BF16)
            dy_ref[...] = _dot(dup, w, NT)
            dw_ref[n] += _dot(y, dup, TN)

    y_spec = pl.BlockSpec((tm, BW), lambda i: (i, 0))
    w_spec = pl.BlockSpec((3, BW, D_MODEL), lambda i: (0, 0, 0))
    return _host_call(
        body, (ya, yb, ym, pg, wbr, dm),
        [y_spec, y_spec, y_spec, pl.BlockSpec((tm, W_G), lambda i: (i, 0)), w_spec,
         pl.BlockSpec((tm, D_MODEL), lambda i: (i, 0))],
        (y_spec, y_spec, y_spec, pl.BlockSpec((tm, W_G), lambda i: (i, 0)), w_spec),
        (SDS((t, BW), F32), SDS((t, BW), F32), SDS((t, BW), F32), SDS((t, W_G), BF16), SDS((3, BW, D_MODEL), F32)),
        name="merge_bwd", grid=(t // tm,), semantics=("arbitrary",), exchange=exchange)


def _loss_head(x, g, target):
    t, d = x.shape
    tm = _tile(t, 512, 8)

    def body(x_ref, g_ref, t_ref, loss_ref, dx_ref, dg_ref):
        @pl.when(pl.program_id(0) == 0)
        def _():
            loss_ref[...] = jnp.zeros_like(loss_ref)
            dg_ref[...] = jnp.zeros_like(dg_ref)

        xf = x_ref[...]
        r = lax.rsqrt(jnp.mean(xf * xf, axis=-1, keepdims=True) + EPS)
        xh = xf * r
        err = xh * g_ref[...] - t_ref[...]
        per_tok = jnp.mean(err * err, axis=-1, keepdims=True)
        loss_ref[...] += 0.5 * jnp.sum(per_tok, axis=0, keepdims=True)
        dy = err * (1.0 / d)
        gd = dy * g_ref[...]
        dx_ref[...] = r * (gd - xh * jnp.mean(gd * xh, axis=-1, keepdims=True))
        dg_ref[...] += jnp.sum(dy * xh, axis=0, keepdims=True)

    row = pl.BlockSpec((tm, d), lambda i: (i, 0))
    vec = pl.BlockSpec((1, d), lambda i: (0, 0))
    return pl.pallas_call(
        body, name="loss_head", grid=(t // tm,),
        in_specs=[row, vec, row],
        out_specs=(pl.BlockSpec((1, 1), lambda i: (0, 0)), row, vec),
        out_shape=(SDS((1, 1), F32), SDS((t, d), F32), SDS((1, d), F32)),
        compiler_params=_params("arbitrary"),
    )(x, g, target)


def _layer_fwd(x, mem, w, tabs, next_shards=None):
    cs, sn = tabs
    riding = next_shards is not None
    h = _rmsnorm_fwd(x, w["norm_g"], "rmsnorm_fwd")
    pa, pb, pm, pg = _in_proj(h, w["w_in_t"])
    q, qT, k, kT, vT = _attn_prep_fwd(pa, cs, sn, w["qg2"], w["kg2"])
    (o, lse, ya), gathered = _attn_fwd(qT, k, vT, pa, exchange=_gather_first_hop(next_shards) if riding else None)
    yb = _gmlp_fwd(pb, w["ln_g"], w["ln_b"], w["w_s"], w["b_s"])
    memn = _rmsnorm_fwd(mem, w["mem_g"], "mem_rmsnorm_fwd")
    kv = _mm(memn, w["w_kv"], "nn", "mem_kv")
    ym = _mem_attn_fwd(pm, kv)
    (merged,), gathered = _merge_fwd(ya, yb, ym, pg, w["w_br"], exchange=_gather_forward(gathered) if riding else None)
    x_next = _mm(merged, w["w_out"], "nn", "out_proj", add=x)
    saved = dict(x=x, h=h, pa=pa, pb=pb, pm=pm, pg=pg, q=q, qT=qT, k=k, kT=kT, vT=vT, o=o, lse=lse, ya=ya, yb=yb, ym=ym,
                 memn=memn, kv=kv, merged=merged)
    return x_next, saved, gathered


def _layer_bwd(dx_out, mem, w, s, tabs, pending=None, core=None, scatter_own=False):
    cs, sn = tabs
    t = dx_out.shape[0]
    riding = pending is not None
    dmerged = _mm(dx_out, w["w_out"], "nt", "d_merged")
    shard_rows = D_MODEL // N_DEV
    d_w_out = _mm_tn_by_owner(s["merged"], dx_out, "d_w_out")
    (dya, dyb, dym, dpg, d_w_br), recv = _merge_bwd(s["ya"], s["yb"], s["ym"], s["pg"], w["w_br"], dmerged,
                                                    exchange=_scatter_to_sibling(pending) if riding else None)
    parts = _pair_sums(pending, recv, core) if riding else None
    grp = A_HEADS // A_KV_HEADS
    lse_cols = s["lse"].reshape(A_KV_HEADS, grp, t).transpose(0, 2, 1)
    (dq, dkT, dvT), from_chips = _attn_bwd(s["q"], s["qT"], s["k"], s["kT"], s["vT"], s["pa"], s["o"], lse_cols, dya,
                                           exchange=_scatter_to_chips(parts) if riding else None)
    dpa, d_qg2, d_kg2 = _attn_prep_bwd(s["pa"], cs, sn, w["qg2"], w["kg2"], dq, dkT.reshape(-1, t), dvT.reshape(-1, t),
                                       dya, s["o"])
    dpb, d_w_s, d_b_s, d_ln_g, d_ln_b = _gmlp_bwd(s["pb"], w["ln_g"], w["ln_b"], w["w_s"], w["b_s"], dyb)
    dpm, dkv = _mem_attn_bwd(s["pm"], s["kv"], dym)
    d_w_kv = _mm_tn_by_owner(s["memn"], dkv, "d_w_kv")
    dmemn = _mm(dkv, w["w_kv"], "nt", "d_memn")
    d_mem_g = _rmsnorm_dg(mem, w["mem_g"], dmemn, "mem_rmsnorm_bwd")
    dps = (dpa, dpb, dpm, dpg)
    d_w_in = _split_w_in_t(*[_mm(dp, s["h"], "tn", "d_w_" + nm, out_dtype=BF16, tm_cap=1536, tn_cap=D_MODEL, tk_cap=1024)
                             for nm, dp in zip("abmg", dps)])
    grads = dict(w_in=d_w_in[:, None], w_mem_kv=d_w_kv.reshape(N_DEV, 1, shard_rows, -1), w_br=_split_w_br(d_w_br),
                 w_out=d_w_out.reshape(N_DEV, 1, shard_rows, -1))
    own = None
    if scatter_own:
        dests = [grads[n] for n in BIG]
        own_parts = _pair_sums(dests, _exchange_call(_scatter_to_sibling(dests), "rs_sibling_swap"), core)
        own = _scatter_to_chips(own_parts)
    (dx_in, d_norm_g), own_from_chips = _dh_rmsnorm_bwd(dps, w["w_in_t"], s["x"], w["norm_g"], dx_out, exchange=own)
    grads.update(norm_g=d_norm_g[0], q_norm_g=d_qg2[0, :HEAD_DIM] + d_qg2[0, HEAD_DIM:],
                 k_norm_g=d_kg2[0, :HEAD_DIM] + d_kg2[0, HEAD_DIM:], sg_ln_g=d_ln_g[0], sg_ln_b=d_ln_b[0],
                 w_s=d_w_s, b_s=d_b_s[:, :, 0], mem_norm_g=d_mem_g[0])
    return (dx_in, grads, ((parts, from_chips) if riding else None),
            ((own_parts, own_from_chips) if scatter_own else None))


def _layer_weights(l, w_in_t, w_kv, w_br, w_out, small):
    tile2 = lambda g: jnp.tile(g.reshape(1, -1), (1, 2))
    return dict(
        w_in_t=w_in_t, w_kv=w_kv, w_br=w_br, w_out=w_out,
        norm_g=small["norm_g"][l].reshape(1, -1), qg2=tile2(small["q_norm_g"][l]), kg2=tile2(small["k_norm_g"][l]),
        ln_g=small["sg_ln_g"][l].reshape(1, -1), ln_b=small["sg_ln_b"][l].reshape(1, -1),
        w_s=small["w_s"][l], b_s=small["b_s"][l][:, :, None], mem_g=small["mem_norm_g"][l].reshape(1, -1))


def _position():
    x, y, c = lax.axis_index("x"), lax.axis_index("y"), lax.axis_index("c")
    return x, y, c, [(1 - x, y), (x, 1 - y), (1 - x, 1 - y)]


def _gather_first_hop(shards):
    n = len(shards)

    def make(x_refs, out_refs, sems):
        send_sems, recv_sems, local_sems = sems
        x, y, c, chips = _position()
        me = 4 * x + 2 * y + c
        peers = [(x, y, 1 - c)] + [(cx, cy, c) for cx, cy in chips]
        copies = [pltpu.make_async_copy(x_refs[t], out_refs[t].at[me], local_sems.at[t]) for t in range(n)]
        copies += [pltpu.make_async_remote_copy(
            src_ref=x_refs[t], dst_ref=out_refs[t].at[me], send_sem=send_sems.at[t, k], recv_sem=recv_sems.at[t, k],
            device_id=peer, device_id_type=MESH_ID) for t in range(n) for k, peer in enumerate(peers)]
        return copies

    return _Exchange(shards, [SDS((N_DEV,) + a.shape, a.dtype) for a in shards],
                     [pltpu.SemaphoreType.DMA((n, 4)), pltpu.SemaphoreType.DMA((n, 4)), pltpu.SemaphoreType.DMA((n,))], make)


def _gather_forward(gathered):
    n = len(gathered)

    def make(in_refs, out_refs, sems):
        send_sems, recv_sems = sems
        x, y, c, chips = _position()
        return [pltpu.make_async_remote_copy(
            src_ref=in_refs[t].at[4 * cx + 2 * cy + c], dst_ref=out_refs[t].at[4 * cx + 2 * cy + c],
            send_sem=send_sems.at[t, j], recv_sem=recv_sems.at[t, j], device_id=(x, y, 1 - c), device_id_type=MESH_ID)
            for t in range(n) for j, (cx, cy) in enumerate(chips)]

    return _Exchange(gathered, [SDS(a.shape, a.dtype) for a in gathered],
                     [pltpu.SemaphoreType.DMA((n, 3)), pltpu.SemaphoreType.DMA((n, 3))], make,
                     aliases={t: t for t in range(n)})


def _all_gather(shards, name):
    return _exchange_call(_gather_forward(_exchange_call(_gather_first_hop(shards), name + "_hop1")), name + "_hop2")


def _shard_segments(d):
    shard = IN_WIDTH // N_DEV
    lo, hi = d * shard, (d + 1) * shard
    out = []
    for g in range(4):
        a, b = max(lo, GROUP_OFFS[g]), min(hi, GROUP_OFFS[g + 1])
        if a < b:
            out.append((g, a - GROUP_OFFS[g], b - GROUP_OFFS[g], a - lo))
    return out


def _split_w_in_t(da, db, dm, dg):
    cols = da.shape[1]
    shard = IN_WIDTH // N_DEV
    tc = _tile(cols, 256)

    def body(a_ref, b_ref, m_ref, g_ref, o_ref):
        ins = (a_ref, b_ref, m_ref, g_ref)
        for d in range(N_DEV):
            for g, lo, hi, off in _shard_segments(d):
                o_ref[_core_major(d), off:off + hi - lo, :] = ins[g][lo:hi, :]

    return pl.pallas_call(
        body, name="split_w_in", grid=(cols // tc,),
        in_specs=[pl.BlockSpec((a.shape[0], tc), lambda j: (0, j)) for a in (da, db, dm, dg)],
        out_specs=pl.BlockSpec((N_DEV, shard, tc), lambda j: (0, 0, j)),
        out_shape=SDS((N_DEV, shard, cols), da.dtype), compiler_params=_params("parallel"),
    )(da, db, dm, dg)


def _assemble_w_br(gathered):
    _, nb, rows, shard = gathered.shape

    def body(g_ref, o_ref):
        o_ref[...] = g_ref[...]

    return pl.pallas_call(
        body, name="assemble_w_br", grid=(N_DEV,),
        in_specs=[pl.BlockSpec((None, nb, rows, shard), lambda d: (d, 0, 0, 0))],
        out_specs=pl.BlockSpec((nb, rows, shard), lambda d: (0, 0, d)),
        out_shape=SDS((nb, rows, N_DEV * shard), gathered.dtype), compiler_params=_params("parallel"),
    )(gathered)


def _split_w_br(dw):
    nb, rows, cols = dw.shape
    shard = cols // N_DEV

    def body(d_ref, o_ref):
        o_ref[...] = d_ref[...].astype(BF16)

    return pl.pallas_call(
        body, name="split_w_br", grid=(N_DEV,),
        in_specs=[pl.BlockSpec((nb, rows, shard), lambda d: (0, 0, d))],
        out_specs=pl.BlockSpec((None, nb, rows, shard), lambda d: (_core_major(d), 0, 0, 0)),
        out_shape=SDS((N_DEV, nb, rows, shard), BF16), compiler_params=_params("parallel"),
    )(dw)


def _scatter_to_sibling(dests):
    n = len(dests)

    def make(d_refs, recv_refs, sems):
        send_sems, recv_sems = sems
        x, y, c, _ = _position()
        return [pltpu.make_async_remote_copy(
            src_ref=d_refs[t].at[pl.ds((1 - c) * 4, 4)], dst_ref=recv_refs[t], send_sem=send_sems.at[t],
            recv_sem=recv_sems.at[t], device_id=(x, y, 1 - c), device_id_type=MESH_ID) for t in range(n)]

    return _Exchange(dests, [SDS((4,) + a.shape[1:], a.dtype) for a in dests],
                     [pltpu.SemaphoreType.DMA((n,)), pltpu.SemaphoreType.DMA((n,))], make)


def _scatter_to_chips(parts):
    n = len(parts)

    def make(p_refs, recv_refs, sems):
        send_sems, recv_sems = sems
        _, _, c, chips = _position()
        return [pltpu.make_async_remote_copy(
            src_ref=p_refs[t].at[2 * cx + cy], dst_ref=recv_refs[t].at[k], send_sem=send_sems.at[t, k],
            recv_sem=recv_sems.at[t, k], device_id=(cx, cy, c), device_id_type=MESH_ID)
            for t in range(n) for k, (cx, cy) in enumerate(chips)]

    return _Exchange(parts, [SDS((3,) + a.shape[1:], a.dtype) for a in parts],
                     [pltpu.SemaphoreType.DMA((n, 3)), pltpu.SemaphoreType.DMA((n, 3))], make)


def _pair_sums(dests, recv, core):
    return [_pair_sum(d, r, core, "rs_pair_sum_" + n) for n, d, r in zip(BIG, dests, recv)]


def _pair_sum(dest, recv, core, name):
    _, na, r, cdim = dest.shape

    def body(core_ref, a_ref, b_ref, o_ref):
        o_ref[...] = (a_ref[...].astype(F32) + b_ref[...].astype(F32)).astype(o_ref.dtype)

    blk = pl.BlockSpec((None, na, r, cdim), lambda j, core_ref: (j, 0, 0, 0))
    return pl.pallas_call(
        body, name=name, out_shape=SDS(recv.shape, BF16),
        grid_spec=pltpu.PrefetchScalarGridSpec(
            num_scalar_prefetch=1, grid=(4,),
            in_specs=[pl.BlockSpec((None, na, r, cdim), lambda j, core_ref: (core_ref[0] * 4 + j, 0, 0, 0)), blk],
            out_specs=blk),
        compiler_params=_params("parallel"),
    )(core, dest, recv)


def _adamw_math(w, g, m, v):
    m = ADAM_B1 * m + (1.0 - ADAM_B1) * g
    v = ADAM_B2 * v + (1.0 - ADAM_B2) * (g * g)
    m_hat = m / (1.0 - ADAM_B1 ** ADAM_STEP)
    v_hat = v / (1.0 - ADAM_B2 ** ADAM_STEP)
    delta = -ADAM_LR * (m_hat / (jnp.sqrt(v_hat) + ADAM_EPS) + ADAM_WD * w)
    return delta, m, v


def _sum_and_adamw(parts, w, m, v, name):
    n, r, ln = parts.shape
    tr = _tile(r, 512, 16)

    def body(p_ref, w_ref, m_ref, v_ref, g_out, d_out, m_out, v_out):
        g = p_ref[0].astype(F32)
        for j in range(1, n):
            g = g + p_ref[j].astype(F32)
        delta, nm, nv = _adamw_math(w_ref[...], g, m_ref[...], v_ref[...])
        g_out[...] = g
        d_out[...] = delta
        m_out[...] = nm
        v_out[...] = nv

    blk = pl.BlockSpec((tr, ln), lambda i: (i, 0))
    return pl.pallas_call(
        body, name=name, grid=(r // tr,),
        in_specs=[pl.BlockSpec((n, tr, ln), lambda i: (0, i, 0)), blk, blk, blk],
        out_specs=(blk, blk, blk, blk), out_shape=tuple(SDS((r, ln), F32) for _ in range(4)),
        compiler_params=_params("parallel"),
    )(parts, w, m, v)


BIG = ("w_in", "w_mem_kv", "w_br", "w_out")
SMALL = ("norm_g", "q_norm_g", "k_norm_g", "sg_ln_g", "sg_ln_b", "w_s", "b_s", "mem_norm_g", "final_g")


def _pack(arrs, row_unit=16):
    flat = jnp.concatenate([a.reshape(-1) for a in arrs])
    pad = (-flat.shape[0]) % (row_unit * LANES)
    if pad:
        flat = jnp.concatenate([flat, jnp.zeros((pad,), flat.dtype)])
    return flat.reshape(-1, LANES)


def _unpack(buf, shapes):
    flat = buf.reshape(-1)
    out, off = [], 0
    for shp in shapes:
        n = 1
        for s_ in shp:
            n *= s_
        out.append(flat[off:off + n].reshape(shp))
        off += n
    return out


def _shard_sum_adamw(part, from_chips, chip, w, m, v, layer, prev, name):
    _, na, r, cdim = part.shape
    flat = w.ndim == 3
    assert not flat or na == 1
    tr = _tile(r, max(8, (256 * 1024) // (na * cdim)), 8)
    n_prev = 0 if prev is None else len(prev)

    def body(chip_ref, p_ref, f_ref, w_ref, m_ref, v_ref, *rest):
        g_out, d_out, m_out, v_out = rest[n_prev:]
        g = p_ref[...].astype(F32)
        for j in range(3):
            g = g + f_ref[j].astype(F32)
        delta, nm, nv = _adamw_math(w_ref[...], g, m_ref[...], v_ref[...])
        g_out[...] = g
        d_out[...] = delta
        m_out[...] = nm
        v_out[...] = nv

    a_blk = None if flat else na
    if flat:
        lay = pl.BlockSpec((None, tr, cdim), lambda i, chip_ref: (layer, i, 0))
    else:
        lay = pl.BlockSpec((None, na, tr, cdim), lambda i, chip_ref: (layer, 0, i, 0))
    return pl.pallas_call(
        body, name=name, out_shape=tuple(SDS(w.shape, F32) for _ in range(4)),
        grid_spec=pltpu.PrefetchScalarGridSpec(
            num_scalar_prefetch=1, grid=(r // tr,),
            in_specs=[pl.BlockSpec((None, a_blk, tr, cdim), lambda i, chip_ref: (chip_ref[0], 0, i, 0)),
                      pl.BlockSpec((3, a_blk, tr, cdim), lambda i, chip_ref: (0, 0, i, 0)), lay, lay, lay]
            + [HBM] * n_prev,
            out_specs=(lay, lay, lay, lay)),
        input_output_aliases={6 + j: j for j in range(n_prev)},
        compiler_params=_params("parallel"),
    )(chip, part, from_chips, w, m, v, *(prev or ()))


def kernel(x, mem, norm_g, w_in, q_norm_g, k_norm_g, sg_ln_g, sg_ln_b, w_s, b_s, mem_norm_g, w_mem_kv, w_br, w_out, final_g, loss_target, m_norm_g, m_w_in, m_q_norm_g, m_k_norm_g, m_sg_ln_g, m_sg_ln_b, m_w_s, m_b_s, m_mem_norm_g, m_w_mem_kv, m_w_br, m_w_out, m_final_g, v_norm_g, v_w_in, v_q_norm_g, v_k_norm_g, v_sg_ln_g, v_sg_ln_b, v_w_s, v_b_s, v_mem_norm_g, v_w_mem_kv, v_w_br, v_w_out, v_final_g):
    wts = dict(norm_g=norm_g, w_in=w_in, q_norm_g=q_norm_g, k_norm_g=k_norm_g, sg_ln_g=sg_ln_g, sg_ln_b=sg_ln_b,
               w_s=w_s, b_s=b_s, mem_norm_g=mem_norm_g, w_mem_kv=w_mem_kv, w_br=w_br, w_out=w_out, final_g=final_g)
    mom1 = dict(norm_g=m_norm_g, w_in=m_w_in, q_norm_g=m_q_norm_g, k_norm_g=m_k_norm_g, sg_ln_g=m_sg_ln_g,
                sg_ln_b=m_sg_ln_b, w_s=m_w_s, b_s=m_b_s, mem_norm_g=m_mem_norm_g, w_mem_kv=m_w_mem_kv, w_br=m_w_br,
                w_out=m_w_out, final_g=m_final_g)
    mom2 = dict(norm_g=v_norm_g, w_in=v_w_in, q_norm_g=v_q_norm_g, k_norm_g=v_k_norm_g, sg_ln_g=v_sg_ln_g,
                sg_ln_b=v_sg_ln_b, w_s=v_w_s, b_s=v_b_s, mem_norm_g=v_mem_norm_g, w_mem_kv=v_w_mem_kv, w_br=v_w_br,
                w_out=v_w_out, final_g=v_final_g)
    dp = w_in.shape[0]
    core = lax.axis_index("c").astype(jnp.int32).reshape(1)
    chip = (2 * lax.axis_index("x") + lax.axis_index("y")).astype(jnp.int32).reshape(1)

    t_in = lambda a: jnp.swapaxes(a, 1, 2)
    wts, mom1, mom2 = [dict(d_, w_in=t_in(d_["w_in"])) for d_ in (wts, mom1, mom2)]

    shard_bf = {n: wts[n].astype(BF16) for n in BIG}
    shards = lambda l: [shard_bf[n][l] for n in BIG]
    x_l, mem_l = x[0], mem[0]
    tabs = _rope_tables(x_l.shape[0])

    gathered = _all_gather(shards(0), "weights_all_gather")
    layers, saved = [], []
    for l in range(dp):
        g_in, g_kv, g_br, g_out = gathered
        layers.append(_layer_weights(l, g_in.reshape(IN_WIDTH, -1), g_kv.reshape(D_MODEL, -1), _assemble_w_br(g_br),
                                     g_out.reshape(D_MODEL, -1), wts))
        x_l, s, gathered = _layer_fwd(x_l, mem_l, layers[l], tabs, next_shards=shards(l + 1) if l + 1 < dp else None)
        saved.append(s)
    loss_local, dx, d_final_g = _loss_head(x_l, final_g.reshape(1, -1), loss_target[0])
    loss = lax.psum(loss_local[0, 0], AXES)

    def finish(l, parts, from_chips, prev):
        return {n: _shard_sum_adamw(p, f, chip, wts[n], mom1[n], mom2[n], l, None if prev is None else prev[n],
                                    "sum_adamw_" + n)
                for n, p, f in zip(BIG, parts, from_chips)}

    grads, updated, pending = [None] * dp, None, None
    for l in reversed(range(dp)):
        dx, grads[l], scattered, own = _layer_bwd(dx, mem_l, layers[l], saved[l], tabs, pending=pending, core=core,
                                                  scatter_own=(l == 0))
        if scattered is not None:
            updated = finish(l + 1, *scattered, updated)
        pending = [grads[l][n] for n in BIG]
    updated = finish(0, *own, updated)
    grad_x = dx
    big_out = [{n: (t_in(updated[n][k]) if n == "w_in" else updated[n][k]) for n in BIG} for k in range(4)]

    small_g = {n: jnp.stack([g[n] for g in grads]) for n in SMALL if n != "final_g"}
    small_g["final_g"] = d_final_g
    (all_small,) = _all_gather([_pack([small_g[n] for n in SMALL])], "small_all_gather")
    small_bufs = _sum_and_adamw(
        all_small, _pack([wts[n] for n in SMALL]), _pack([mom1[n] for n in SMALL]), _pack([mom2[n] for n in SMALL]),
        "small_sum_adamw")

    outs = []
    for big_vals, small_buf in zip(big_out, small_bufs):
        vals = dict(big_vals)
        vals.update(zip(SMALL, _unpack(small_buf, [wts[n].shape for n in SMALL])))
        outs.append(vals)
    order = ("norm_g", "w_in", "q_norm_g", "k_norm_g", "sg_ln_g", "sg_ln_b", "w_s", "b_s", "mem_norm_g", "w_mem_kv",
             "w_br", "w_out", "final_g")
    result = [loss, grad_x[None]]
    for vals in outs:
        result += [vals[n] for n in order]
    return tuple(result)
```

```python
import functools

import jax
import jax.numpy as jnp
from jax import lax
from jax.experimental import pallas as pl
from jax.experimental.pallas import tpu as pltpu

F32 = jnp.float32
BF16 = jnp.bfloat16
SDS = jax.ShapeDtypeStruct
MESH_ID = pl.DeviceIdType.MESH
AXES = ("x", "y", "c")
N_DEV = 8

D_MODEL = 1024
DEPTH = 4
GRID_W = 64
CHUNK = 128
ROPE_THETA = 10000.0
EPS = 1e-6
HEAD_DIM = 64
A_HEADS = 8
A_KV_HEADS = 2
B_GROUPS = 4
M_HEADS = 4
M_HEAD_DIM = 128
BW = 512
W_A = 1280
W_B = 1536
W_M = 1024
W_G = 3072
IN_WIDTH = W_A + W_B + W_M + W_G
GROUP_OFFS = (0, W_A, W_A + W_B, W_A + W_B + W_M, IN_WIDTH)

ADAM_LR = 0.001
ADAM_B1 = 0.9
ADAM_B2 = 0.999
ADAM_EPS = 1e-08
ADAM_WD = 0.01
ADAM_STEP = 10

LANES = 128
VMEM_LIMIT = 52 * 1024 * 1024


def _tile(n, cap, unit=LANES):
    if n <= cap:
        return n
    t = (cap // unit) * unit
    while n % t:
        t -= unit
    return t


def _params(*sem):
    return pltpu.CompilerParams(dimension_semantics=sem, vmem_limit_bytes=VMEM_LIMIT)


def _sigmoid(z):
    return 0.5 * jnp.tanh(0.5 * z) + 0.5


def _silu_and_grad(z):
    s = _sigmoid(z)
    return z * s, s * (1.0 + z * (1.0 - s))


def _dot(a, b, dims):
    return lax.dot_general(a, b, (dims, ((), ())), preferred_element_type=F32)


NN = ((1,), (0,))
NT = ((1,), (1,))
TN = ((0,), (0,))
HBM = pl.BlockSpec(memory_space=pl.ANY)


class _Exchange:
    def __init__(self, ins, out_shapes, sems, make, aliases=None):
        self.ins, self.out_shapes, self.sems, self.make = list(ins), list(out_shapes), list(sems), make
        self.aliases = dict(aliases or {})

    def start(self, in_refs, out_refs, sems):
        for cp in self.make(in_refs, out_refs, sems):
            cp.start()

    def finish(self, in_refs, out_refs, sems):
        for cp in self.make(in_refs, out_refs, sems):
            cp.wait()


def _host_call(body, ins, in_specs, out_specs, out_shape, *, name, grid, semantics, scratch=(), exchange=None):
    ins, out_shape, scratch = list(ins), list(out_shape), list(scratch)
    if exchange is None:
        res = pl.pallas_call(
            body, name=name, grid=grid, in_specs=list(in_specs), out_specs=tuple(out_specs), out_shape=tuple(out_shape),
            scratch_shapes=scratch, compiler_params=_params(*semantics))(*ins)
        return tuple(res), ()
    n_in, n_out, n_scr = len(ins), len(out_shape), len(scratch)
    x_in, x_out = len(exchange.ins), len(exchange.out_shapes)

    def carrying(*refs):
        o0 = n_in + x_in
        s0 = o0 + n_out + x_out
        c_in, c_out, c_sems = refs[n_in:o0], refs[o0 + n_out:s0], refs[s0 + n_scr:]
        ids = [pl.program_id(a) for a in range(len(grid))]
        first = functools.reduce(jnp.logical_and, [i == 0 for i in ids])
        last = functools.reduce(jnp.logical_and, [i == g - 1 for i, g in zip(ids, grid)])

        @pl.when(first)
        def _():
            exchange.start(c_in, c_out, c_sems)

        body(*refs[:n_in], *refs[o0:o0 + n_out], *refs[s0:s0 + n_scr])

        @pl.when(last)
        def _():
            exchange.finish(c_in, c_out, c_sems)

    res = pl.pallas_call(
        carrying, name=name, grid=grid, in_specs=list(in_specs) + [HBM] * x_in,
        out_specs=tuple(out_specs) + (HBM,) * x_out, out_shape=tuple(out_shape) + tuple(exchange.out_shapes),
        scratch_shapes=scratch + exchange.sems,
        input_output_aliases={n_in + i: n_out + o for i, o in exchange.aliases.items()},
        compiler_params=_params(*(["arbitrary"] * len(grid))))(*ins, *exchange.ins)
    return tuple(res[:n_out]), tuple(res[n_out:])


def _exchange_call(exchange, name):
    x_in = len(exchange.ins)

    def body(*refs):
        x_out = len(exchange.out_shapes)
        c_in, c_out, c_sems = refs[:x_in], refs[x_in:x_in + x_out], refs[x_in + x_out:]
        exchange.start(c_in, c_out, c_sems)
        exchange.finish(c_in, c_out, c_sems)

    return pl.pallas_call(
        body, name=name, in_specs=[HBM] * x_in, out_specs=tuple([HBM] * len(exchange.out_shapes)),
        out_shape=tuple(exchange.out_shapes), scratch_shapes=exchange.sems, input_output_aliases=exchange.aliases,
    )(*exchange.ins)


def _core_major(d):
    return (d % 2) * 4 + d // 2


def _mm(a, b, mode, name, add=None, out_dtype=F32, tm_cap=512, tn_cap=1536, tk_cap=1536):
    if mode == "nn":
        (m, k), (k2, n) = a.shape, b.shape
    elif mode == "nt":
        (m, k), (n, k2) = a.shape, b.shape
    else:
        (k, m), (k2, n) = a.shape, b.shape
    assert k == k2, (a.shape, b.shape, mode)
    tm = _tile(m, tm_cap, 8 if mode != "tn" else LANES)
    tn = _tile(n, tn_cap)
    tk = _tile(k, tk_cap, LANES if mode != "tn" else 16)
    nk = k // tk
    dims = {"nn": NN, "nt": NT, "tn": TN}[mode]
    a_spec = {"nn": pl.BlockSpec((tm, tk), lambda i, j, kk: (i, kk)),
              "nt": pl.BlockSpec((tm, tk), lambda i, j, kk: (i, kk)),
              "tn": pl.BlockSpec((tk, tm), lambda i, j, kk: (kk, i))}[mode]
    b_spec = {"nn": pl.BlockSpec((tk, tn), lambda i, j, kk: (kk, j)),
              "nt": pl.BlockSpec((tn, tk), lambda i, j, kk: (j, kk)),
              "tn": pl.BlockSpec((tk, tn), lambda i, j, kk: (kk, j))}[mode]
    o_spec = pl.BlockSpec((tm, tn), lambda i, j, kk: (i, j))
    has_add = add is not None

    def body(*refs):
        a_ref, b_ref = refs[0], refs[1]
        add_ref = refs[2] if has_add else None
        o_ref = refs[3] if has_add else refs[2]
        part = _dot(a_ref[...].astype(BF16), b_ref[...].astype(BF16), dims)
        if nk == 1:
            if has_add:
                part = part + add_ref[...]
            o_ref[...] = part.astype(out_dtype)
        else:
            acc = refs[-1]
            kk = pl.program_id(2)

            @pl.when(kk == 0)
            def _():
                acc[...] = part + add_ref[...] if has_add else part

            @pl.when(kk > 0)
            def _():
                acc[...] += part

            @pl.when(kk == nk - 1)
            def _():
                o_ref[...] = acc[...].astype(out_dtype)

    ins = [a, b] + ([add] if has_add else [])
    in_specs = [a_spec, b_spec] + ([o_spec] if has_add else [])
    return pl.pallas_call(
        body, name=name, grid=(m // tm, n // tn, nk), in_specs=in_specs, out_specs=o_spec,
        out_shape=SDS((m, n), out_dtype),
        scratch_shapes=[pltpu.VMEM((tm, tn), F32)] if nk > 1 else [],
        compiler_params=_params("parallel", "parallel", "arbitrary"),
    )(*ins)


def _mm_tn_by_owner(a, b, name):
    (k, m), (k2, n) = a.shape, b.shape
    assert k == k2 and m % N_DEV == 0, (a.shape, b.shape)
    tk = _tile(k, 2048, 16)
    nk = k // tk
    rows = m // N_DEV

    def body(a_ref, b_ref, o_ref, acc):
        kk = pl.program_id(0)
        part = _dot(a_ref[...].astype(BF16), b_ref[...].astype(BF16), TN)

        @pl.when(kk == 0)
        def _():
            acc[...] = part

        @pl.when(kk > 0)
        def _():
            acc[...] += part

        @pl.when(kk == nk - 1)
        def _():
            for d in range(N_DEV):
                pos = _core_major(d)
                o_ref[pos * rows:(pos + 1) * rows, :] = acc[d * rows:(d + 1) * rows, :].astype(BF16)

    return pl.pallas_call(
        body, name=name, grid=(nk,),
        in_specs=[pl.BlockSpec((tk, m), lambda kk: (kk, 0)), pl.BlockSpec((tk, n), lambda kk: (kk, 0))],
        out_specs=pl.BlockSpec((m, n), lambda kk: (0, 0)), out_shape=SDS((m, n), BF16),
        scratch_shapes=[pltpu.VMEM((m, n), F32)], compiler_params=_params("arbitrary"),
    )(a, b)


def _rmsnorm_fwd(x, g, name):
    t, d = x.shape
    tm = _tile(t, 512, 8)

    def body(x_ref, g_ref, h_ref):
        xf = x_ref[...]
        r = lax.rsqrt(jnp.mean(xf * xf, axis=-1, keepdims=True) + EPS)
        h_ref[...] = (xf * r * g_ref[...]).astype(BF16)

    return pl.pallas_call(
        body, name=name, grid=(t // tm,),
        in_specs=[pl.BlockSpec((tm, d), lambda i: (i, 0)), pl.BlockSpec((1, d), lambda i: (0, 0))],
        out_specs=pl.BlockSpec((tm, d), lambda i: (i, 0)),
        out_shape=SDS((t, d), BF16), compiler_params=_params("parallel"),
    )(x, g)


def _rmsnorm_bwd_math(xf, g, dh):
    r = lax.rsqrt(jnp.mean(xf * xf, axis=-1, keepdims=True) + EPS)
    xh = xf * r
    gd = dh * g
    dx = r * (gd - xh * jnp.mean(gd * xh, axis=-1, keepdims=True))
    return dx, jnp.sum(dh * xh, axis=0, keepdims=True)


def _rmsnorm_dg(x, g, dh, name):
    t, d = x.shape
    tm = _tile(t, 512, 8)

    def body(x_ref, g_ref, dh_ref, dg_ref):
        @pl.when(pl.program_id(0) == 0)
        def _():
            dg_ref[...] = jnp.zeros_like(dg_ref)

        dg_ref[...] += _rmsnorm_bwd_math(x_ref[...], g_ref[...], dh_ref[...])[1]

    row = pl.BlockSpec((tm, d), lambda i: (i, 0))
    vec = pl.BlockSpec((1, d), lambda i: (0, 0))
    return pl.pallas_call(
        body, name=name, grid=(t // tm,), in_specs=[row, vec, row], out_specs=vec, out_shape=SDS((1, d), F32),
        compiler_params=_params("arbitrary"),
    )(x, g, dh)


def _group_rows(p):
    return slice(GROUP_OFFS[p], GROUP_OFFS[p + 1])


def _norm_in_proj(x, g, w_t):
    t, d = x.shape
    tm = _tile(t, 512, 16)
    widths = [GROUP_OFFS[p + 1] - GROUP_OFFS[p] for p in range(4)]

    def body(x_ref, g_ref, w_ref, h_ref, *outs):
        xf = x_ref[...]
        r = lax.rsqrt(jnp.mean(xf * xf, axis=-1, keepdims=True) + EPS)
        hb = (xf * r * g_ref[...]).astype(BF16)
        h_ref[...] = hb
        for p, o_ref in enumerate(outs):
            o_ref[...] = _dot(hb, w_ref[_group_rows(p), :], NT).astype(BF16)

    row = lambda wd: pl.BlockSpec((tm, wd), lambda i: (i, 0))
    return pl.pallas_call(
        body, name="norm_in_proj", grid=(t // tm,),
        in_specs=[row(d), pl.BlockSpec((1, d), lambda i: (0, 0)),
                  pl.BlockSpec(w_t.shape, lambda i: (0, 0), pipeline_mode=pl.Buffered(1))],
        out_specs=(row(d),) + tuple(row(wd) for wd in widths),
        out_shape=(SDS((t, d), BF16),) + tuple(SDS((t, wd), BF16) for wd in widths),
        compiler_params=_params("parallel"),
    )(x, g, w_t)


def _dh_rmsnorm_bwd(dps, w_t, x, g, dres, exchange=None):
    t, d = x.shape
    tm = _tile(t, 256, 16)
    n = len(dps)

    def body(*refs):
        dp_refs, w_ref = refs[:n], refs[n]
        x_ref, g_ref, dres_ref, dx_ref, dg_ref = refs[n + 1:]
        dh = None
        for p, dp_ref in enumerate(dp_refs):
            part = _dot(dp_ref[...], w_ref[_group_rows(p), :], NN)
            dh = part if dh is None else dh + part
        dx, dg = _rmsnorm_bwd_math(x_ref[...], g_ref[...], dh)
        dx_ref[...] = dx + dres_ref[...]

        @pl.when(pl.program_id(0) == 0)
        def _():
            dg_ref[...] = jnp.zeros_like(dg_ref)

        dg_ref[...] += dg

    row = pl.BlockSpec((tm, d), lambda i: (i, 0))
    vec = pl.BlockSpec((1, d), lambda i: (0, 0))
    return _host_call(
        body, list(dps) + [w_t, x, g, dres],
        [pl.BlockSpec((tm, a.shape[1]), lambda i: (i, 0)) for a in dps]
        + [pl.BlockSpec(w_t.shape, lambda i: (0, 0), pipeline_mode=pl.Buffered(1)), row, vec, row],
        (row, vec), (SDS((t, d), F32), SDS((1, d), F32)),
        name="dh_rmsnorm_bwd", grid=(t // tm,), semantics=("arbitrary",), exchange=exchange)


def _rope_tables(t):
    rows = t // GRID_W
    row = jnp.repeat(jnp.arange(rows, dtype=F32), GRID_W)
    col = jnp.tile(jnp.arange(GRID_W, dtype=F32), rows)
    n_freq = HEAD_DIM // 4
    inv = ROPE_THETA ** (-jnp.arange(n_freq, dtype=F32) / n_freq)
    ang = jnp.stack([row[:, None] * inv, col[:, None] * inv], axis=1)
    cos, sin = jnp.cos(ang), jnp.sin(ang)
    c64 = jnp.concatenate([cos[:, 0], cos[:, 0], cos[:, 1], cos[:, 1]], axis=-1)
    s64 = jnp.concatenate([-sin[:, 0], sin[:, 0], -sin[:, 1], sin[:, 1]], axis=-1)
    return jnp.tile(c64, (1, 2)), jnp.tile(s64, (1, 2))


def _head_sums(v, lane):
    lo = jnp.sum(jnp.where(lane < HEAD_DIM, v, 0.0), axis=-1, keepdims=True)
    hi = jnp.sum(jnp.where(lane < HEAD_DIM, 0.0, v), axis=-1, keepdims=True)
    return jnp.where(lane < HEAD_DIM, lo, hi)


def _swap16(v, lane):
    return jnp.where((lane % 32) < 16, pltpu.roll(v, LANES - 16, 1), pltpu.roll(v, 16, 1))


def _attn_prep_fwd(pa, cs, sn, qg2, kg2):
    t = pa.shape[0]
    tq = _tile(t, 512, LANES)
    scale = HEAD_DIM ** -0.5

    def body(pa_ref, cs_ref, sn_ref, qg_ref, kg_ref, q_ref, qT_ref, k_ref, kT_ref, vT_ref):
        lane = lax.broadcasted_iota(jnp.int32, (tq, LANES), 1)
        c, s = cs_ref[...], sn_ref[...]

        def norm_rope(xg, g2):
            r = lax.rsqrt(_head_sums(xg * xg, lane) * (1.0 / HEAD_DIM) + EPS)
            xn = xg * r * g2
            return xn * c + _swap16(xn, lane) * s

        for gi in range(4):
            sl = slice(gi * LANES, (gi + 1) * LANES)
            qr = norm_rope(pa_ref[:, sl].astype(F32), qg_ref[...]) * scale
            q_ref[:, sl] = qr.astype(BF16)
            qT_ref[sl, :] = qr.T.astype(BF16)
        kr = norm_rope(pa_ref[:, 512:640].astype(F32), kg_ref[...])
        kT_ref[...] = kr.T.astype(BF16)
        vT_ref[...] = pa_ref[:, 640:768].astype(F32).T.astype(BF16)
        kr = kr.astype(BF16)
        for kvh in range(A_KV_HEADS):
            k_ref[kvh] = kr[:, kvh * HEAD_DIM:(kvh + 1) * HEAD_DIM]

    row = lambda w: pl.BlockSpec((tq, w), lambda i: (i, 0))
    col = lambda r: pl.BlockSpec((r, tq), lambda i: (0, i))
    vec = pl.BlockSpec((1, LANES), lambda i: (0, 0))
    hm = pl.BlockSpec((A_KV_HEADS, tq, HEAD_DIM), lambda i: (0, i, 0))
    return pl.pallas_call(
        body, name="attn_prep_fwd", grid=(t // tq,),
        in_specs=[row(W_A), row(LANES), row(LANES), vec, vec],
        out_specs=(row(BW), col(BW), hm, col(LANES), col(LANES)),
        out_shape=(SDS((t, BW), BF16), SDS((BW, t), BF16), SDS((A_KV_HEADS, t, HEAD_DIM), BF16),
                   SDS((LANES, t), BF16), SDS((LANES, t), BF16)),
        compiler_params=_params("parallel"),
    )(pa, cs, sn, qg2, kg2)


def _attn_fwd(qT, k, vT, pa, exchange=None):
    t = qT.shape[1]
    tq = _tile(t, 256, LANES)
    grp = A_HEADS // A_KV_HEADS

    def body(qT_ref, k_ref, vT_ref, pa_ref, o_ref, lse_ref, y_ref):
        def scores(h):
            return _dot(k_ref[h // grp], qT_ref[h * HEAD_DIM:(h + 1) * HEAD_DIM, :], NN)

        def weighted_values(h, pT, l):
            oT = _dot(vT_ref[(h // grp) * HEAD_DIM:(h // grp + 1) * HEAD_DIM, :], pT, NN)
            return oT / l

        def finish_pair(pair, o_pair):
            o2 = jnp.concatenate(o_pair, axis=0).T
            cols = slice(pair * LANES, (pair + 1) * LANES)
            o_ref[:, cols] = o2
            z = pa_ref[:, 768 + pair * LANES:768 + (pair + 1) * LANES].astype(F32)
            y_ref[:, cols] = (o2 * (z * _sigmoid(z))).astype(BF16)

        sT, prev, o_pair = scores(0), None, []
        for h in range(A_HEADS + 1):
            sT_next = scores(h + 1) if h + 1 < A_HEADS else None
            if prev is not None:
                o_pair.append(weighted_values(h - 1, *prev))
                if len(o_pair) == 2:
                    finish_pair((h - 1) // 2, o_pair)
                    o_pair = []
            if h < A_HEADS:
                m = jnp.max(sT, axis=0, keepdims=True)
                pT = jnp.exp(sT - m)
                l = jnp.sum(pT, axis=0, keepdims=True)
                lse_ref[h:h + 1, :] = m + jnp.log(l)
                prev = (pT.astype(BF16), l)
            sT = sT_next

    row = lambda w: pl.BlockSpec((tq, w), lambda i: (i, 0))
    return _host_call(
        body, (qT, k, vT, pa),
        [pl.BlockSpec((BW, tq), lambda i: (0, i)), pl.BlockSpec((A_KV_HEADS, t, HEAD_DIM), lambda i: (0, 0, 0)),
         pl.BlockSpec((A_KV_HEADS * HEAD_DIM, t), lambda i: (0, 0)), row(W_A)],
        (row(BW), pl.BlockSpec((A_HEADS, tq), lambda i: (0, i)), row(BW)),
        (SDS((t, BW), F32), SDS((A_HEADS, t), F32), SDS((t, BW), BF16)),
        name="attn_fwd", grid=(t // tq,), semantics=("parallel",), exchange=exchange)


def _attn_bwd(q, qT, k, kT, vT, pa, o, lse, dy, exchange=None):
    t = q.shape[0]
    tq = _tile(t, 256, LANES)
    grp = A_HEADS // A_KV_HEADS
    gw = grp * HEAD_DIM

    def body(q_ref, qT_ref, k_ref, kT_ref, vT_ref, z_ref, o_ref, lse_ref, dy_ref, dq_ref, dkT_ref, dvT_ref):
        @pl.when(pl.program_id(1) == 0)
        def _():
            dkT_ref[...] = jnp.zeros_like(dkT_ref)
            dvT_ref[...] = jnp.zeros_like(dvT_ref)

        z = z_ref[...].astype(F32)
        do = dy_ref[...] * (z * _sigmoid(z))
        doo = do * o_ref[...]
        doT = do.T
        kk, kT, vT = k_ref[...], kT_ref[...], vT_ref[...]
        for j in range(grp):
            hs = slice(j * HEAD_DIM, (j + 1) * HEAD_DIM)
            delta = jnp.sum(doo[:, hs], axis=-1, keepdims=True)
            s = _dot(q_ref[:, hs], kT, NN)
            p = jnp.exp(s - lse_ref[:, j:j + 1])
            dp = _dot(do[:, hs].astype(BF16), vT, NN)
            ds_b = (p * (dp - delta)).astype(BF16)
            dq_ref[:, hs] = _dot(ds_b, kk, NN)
            dkT_ref[...] += _dot(qT_ref[hs, :], ds_b, NN)
            dvT_ref[...] += _dot(doT[hs, :].astype(BF16), p.astype(BF16), NN)

    grp_blk = pl.BlockSpec((tq, gw), lambda g, i: (i, g))
    kvT_blk = pl.BlockSpec((HEAD_DIM, t), lambda g, i: (g, 0))
    acc_blk = pl.BlockSpec((None, HEAD_DIM, t), lambda g, i: (g, 0, 0))
    return _host_call(
        body, (q, qT, k, kT, vT, pa, o, lse, dy),
        [grp_blk, pl.BlockSpec((gw, tq), lambda g, i: (g, i)), pl.BlockSpec((None, t, HEAD_DIM), lambda g, i: (g, 0, 0)),
         kvT_blk, kvT_blk, pl.BlockSpec((tq, gw), lambda g, i: (i, 768 // gw + g)), grp_blk,
         pl.BlockSpec((None, tq, grp), lambda g, i: (g, i, 0)), grp_blk],
        (grp_blk, acc_blk, acc_blk),
        (SDS((t, BW), F32), SDS((A_KV_HEADS, HEAD_DIM, t), F32), SDS((A_KV_HEADS, HEAD_DIM, t), F32)),
        name="attn_bwd", grid=(A_KV_HEADS, t // tq), semantics=("arbitrary", "arbitrary"), exchange=exchange)


def _attn_prep_bwd(pa, cs, sn, qg2, kg2, dq, dkT, dvT, dy, o):
    t = pa.shape[0]
    tq = _tile(t, 512, LANES)
    scale = HEAD_DIM ** -0.5

    def body(pa_ref, cs_ref, sn_ref, qg_ref, kg_ref, dq_ref, dkT_ref, dvT_ref, dy_ref, o_ref, dpa_ref, dqg_ref, dkg_ref):
        lane = lax.broadcasted_iota(jnp.int32, (tq, LANES), 1)
        c, s = cs_ref[...], sn_ref[...]

        @pl.when(pl.program_id(0) == 0)
        def _():
            dqg_ref[...] = jnp.zeros_like(dqg_ref)
            dkg_ref[...] = jnp.zeros_like(dkg_ref)

        def norm_rope_bwd(xg, g2, dout):
            r = lax.rsqrt(_head_sums(xg * xg, lane) * (1.0 / HEAD_DIM) + EPS)
            xh = xg * r
            dxn = dout * c + _swap16(dout * s, lane)
            gd = dxn * g2
            dx = r * (gd - xh * (_head_sums(gd * xh, lane) * (1.0 / HEAD_DIM)))
            return dx, jnp.sum(dxn * xh, axis=0, keepdims=True)

        for gi in range(4):
            sl = slice(gi * LANES, (gi + 1) * LANES)
            dx, dg = norm_rope_bwd(pa_ref[:, sl].astype(F32), qg_ref[...], dq_ref[:, sl] * scale)
            dpa_ref[:, sl] = dx.astype(BF16)
            dqg_ref[...] += dg
        dx, dg = norm_rope_bwd(pa_ref[:, 512:640].astype(F32), kg_ref[...], dkT_ref[...].T)
        dpa_ref[:, 512:640] = dx.astype(BF16)
        dkg_ref[...] += dg
        dpa_ref[:, 640:768] = dvT_ref[...].T.astype(BF16)
        z = pa_ref[:, 768:1280].astype(F32)
        _, dsilu = _silu_and_grad(z)
        dpa_ref[:, 768:1280] = (dy_ref[...] * o_ref[...] * dsilu).astype(BF16)

    row = lambda w: pl.BlockSpec((tq, w), lambda i: (i, 0))
    col = pl.BlockSpec((LANES, tq), lambda i: (0, i))
    vec = pl.BlockSpec((1, LANES), lambda i: (0, 0))
    return pl.pallas_call(
        body, name="attn_prep_bwd", grid=(t // tq,),
        in_specs=[row(W_A), row(LANES), row(LANES), vec, vec, row(BW), col, col, row(BW), row(BW)],
        out_specs=(row(W_A), vec, vec),
        out_shape=(SDS((t, W_A), BF16), SDS((1, LANES), F32), SDS((1, LANES), F32)),
        compiler_params=_params("arbitrary"),
    )(pa, cs, sn, qg2, kg2, dq, dkT, dvT, dy, o)


def _layer_norm(v, g, b):
    mu = jnp.mean(v, axis=-1, keepdims=True)
    xc = v - mu
    rs = lax.rsqrt(jnp.mean(xc * xc, axis=-1, keepdims=True) + EPS)
    xh = xc * rs
    return xh * g + b, xh, rs


def _gmlp_fwd(pb, lng, lnb, ws, bs):
    t = pb.shape[0]
    tb = _tile(t, 512, CHUNK)

    def body(pb_ref, g_ref, b_ref, ws_ref, bs_ref, y_ref):
        vln, _, _ = _layer_norm(pb_ref[:, BW:2 * BW].astype(F32), g_ref[...], b_ref[...])
        vb = vln.astype(BF16)
        for gi in range(B_GROUPS):
            w = ws_ref[gi].astype(BF16)
            cs_ = slice(gi * CHUNK, (gi + 1) * CHUNK)
            for n in range(tb // CHUNK):
                rs_ = slice(n * CHUNK, (n + 1) * CHUNK)
                mixed = _dot(w, vb[rs_, cs_], NN) + bs_ref[gi]
                z = pb_ref[rs_, 2 * BW + gi * CHUNK:2 * BW + (gi + 1) * CHUNK].astype(F32)
                y_ref[rs_, cs_] = (pb_ref[rs_, cs_].astype(F32) * mixed * (z * _sigmoid(z))).astype(BF16)

    return pl.pallas_call(
        body, name="gmlp_fwd", grid=(t // tb,),
        in_specs=[pl.BlockSpec((tb, W_B), lambda i: (i, 0)),
                  pl.BlockSpec((1, BW), lambda i: (0, 0)), pl.BlockSpec((1, BW), lambda i: (0, 0)),
                  pl.BlockSpec((B_GROUPS, CHUNK, CHUNK), lambda i: (0, 0, 0)),
                  pl.BlockSpec((B_GROUPS, CHUNK, 1), lambda i: (0, 0, 0))],
        out_specs=pl.BlockSpec((tb, BW), lambda i: (i, 0)),
        out_shape=SDS((t, BW), BF16), compiler_params=_params("parallel"),
    )(pb, lng, lnb, ws, bs)


def _gmlp_bwd(pb, lng, lnb, ws, bs, dy):
    t = pb.shape[0]
    tb = _tile(t, 256, CHUNK)

    def body(pb_ref, g_ref, b_ref, ws_ref, bs_ref, dy_ref, dpb_ref, dws_ref, dbs_ref, dg_ref, db_ref, dvln_ref):
        @pl.when(pl.program_id(0) == 0)
        def _():
            dws_ref[...] = jnp.zeros_like(dws_ref)
            dbs_ref[...] = jnp.zeros_like(dbs_ref)
            dg_ref[...] = jnp.zeros_like(dg_ref)
            db_ref[...] = jnp.zeros_like(db_ref)

        vln, xh, rs = _layer_norm(pb_ref[:, BW:2 * BW].astype(F32), g_ref[...], b_ref[...])
        vb = vln.astype(BF16)
        for gi in range(B_GROUPS):
            w = ws_ref[gi].astype(BF16)
            cs_ = slice(gi * CHUNK, (gi + 1) * CHUNK)
            for n in range(tb // CHUNK):
                rs_ = slice(n * CHUNK, (n + 1) * CHUNK)
                vbc = vb[rs_, cs_]
                mixed = _dot(w, vbc, NN) + bs_ref[gi]
                zs = slice(2 * BW + gi * CHUNK, 2 * BW + (gi + 1) * CHUNK)
                z = pb_ref[rs_, zs].astype(F32)
                u = pb_ref[rs_, cs_].astype(F32)
                sil, dsil = _silu_and_grad(z)
                dyc = dy_ref[rs_, cs_]
                dmixed = dyc * u * sil
                dpb_ref[rs_, cs_] = (dyc * mixed * sil).astype(BF16)
                dpb_ref[rs_, zs] = (dyc * u * mixed * dsil).astype(BF16)
                dmb = dmixed.astype(BF16)
                dws_ref[gi] += _dot(dmb, vbc, NT)
                dbs_ref[gi] += jnp.sum(dmixed, axis=-1, keepdims=True)
                dvln_ref[rs_, cs_] = _dot(w, dmb, TN)
        dvln = dvln_ref[...]
        dg_ref[...] += jnp.sum(dvln * xh, axis=0, keepdims=True)
        db_ref[...] += jnp.sum(dvln, axis=0, keepdims=True)
        gd = dvln * g_ref[...]
        dv = rs * (gd - jnp.mean(gd, axis=-1, keepdims=True) - xh * jnp.mean(gd * xh, axis=-1, keepdims=True))
        dpb_ref[:, BW:2 * BW] = dv.astype(BF16)

    vec = pl.BlockSpec((1, BW), lambda i: (0, 0))
    wsb = pl.BlockSpec((B_GROUPS, CHUNK, CHUNK), lambda i: (0, 0, 0))
    bsb = pl.BlockSpec((B_GROUPS, CHUNK, 1), lambda i: (0, 0, 0))
    return pl.pallas_call(
        body, name="gmlp_bwd", grid=(t // tb,),
        in_specs=[pl.BlockSpec((tb, W_B), lambda i: (i, 0)), vec, vec, wsb, bsb, pl.BlockSpec((tb, BW), lambda i: (i, 0))],
        out_specs=(pl.BlockSpec((tb, W_B), lambda i: (i, 0)), wsb, bsb, vec, vec),
        out_shape=(SDS((t, W_B), BF16), SDS((B_GROUPS, CHUNK, CHUNK), F32), SDS((B_GROUPS, CHUNK, 1), F32),
                   SDS((1, BW), F32), SDS((1, BW), F32)),
        scratch_shapes=[pltpu.VMEM((tb, BW), F32)],
        compiler_params=_params("arbitrary"),
    )(pb, lng, lnb, ws, bs, dy)


def _mem_softmax(qh, kh):
    s = _dot(qh, kh, NT) * (M_HEAD_DIM ** -0.5)
    e = jnp.exp(s - jnp.max(s, axis=-1, keepdims=True))
    return e / jnp.sum(e, axis=-1, keepdims=True)


def _mem_attn_fwd(pm, kv):
    t = pm.shape[0]
    tq = _tile(t, 512, 8)
    ml = kv.shape[0]

    def body(pm_ref, kv_ref, y_ref):
        for h in range(M_HEADS):
            hs = slice(h * M_HEAD_DIM, (h + 1) * M_HEAD_DIM)
            kh = kv_ref[:, hs].astype(BF16)
            vh = kv_ref[:, BW + h * M_HEAD_DIM:BW + (h + 1) * M_HEAD_DIM].astype(BF16)
            p = _mem_softmax(pm_ref[:, hs].astype(BF16), kh)
            o = _dot(p.astype(BF16), vh, NN)
            z = pm_ref[:, BW + h * M_HEAD_DIM:BW + (h + 1) * M_HEAD_DIM].astype(F32)
            y_ref[:, hs] = (o * (z * _sigmoid(z))).astype(BF16)

    return pl.pallas_call(
        body, name="mem_attn_fwd", grid=(t // tq,),
        in_specs=[pl.BlockSpec((tq, W_M), lambda i: (i, 0)), pl.BlockSpec((ml, 2 * BW), lambda i: (0, 0))],
        out_specs=pl.BlockSpec((tq, BW), lambda i: (i, 0)),
        out_shape=SDS((t, BW), BF16), compiler_params=_params("parallel"),
    )(pm, kv)


def _mem_attn_bwd(pm, kv, dy):
    t = pm.shape[0]
    tq = _tile(t, 512, 8)
    ml = kv.shape[0]
    scale = M_HEAD_DIM ** -0.5

    def body(pm_ref, kv_ref, dy_ref, dpm_ref, dkv_ref):
        @pl.when(pl.program_id(0) == 0)
        def _():
            dkv_ref[...] = jnp.zeros_like(dkv_ref)

        for h in range(M_HEADS):
            hs = slice(h * M_HEAD_DIM, (h + 1) * M_HEAD_DIM)
            zs = slice(BW + h * M_HEAD_DIM, BW + (h + 1) * M_HEAD_DIM)
            kh = kv_ref[:, hs].astype(BF16)
            vh = kv_ref[:, zs].astype(BF16)
            qh = pm_ref[:, hs].astype(BF16)
            p = _mem_softmax(qh, kh)
            pb = p.astype(BF16)
            o = _dot(pb, vh, NN)
            sil, dsil = _silu_and_grad(pm_ref[:, zs].astype(F32))
            dyh = dy_ref[:, hs]
            do = dyh * sil
            dpm_ref[:, zs] = (dyh * o * dsil).astype(BF16)
            delta = jnp.sum(do * o, axis=-1, keepdims=True)
            do_b = do.astype(BF16)
            dp = _dot(do_b, vh, NT)
            dr_b = (p * (dp - delta) * scale).astype(BF16)
            dpm_ref[:, hs] = _dot(dr_b, kh, NN).astype(BF16)
            dkv_ref[:, hs] += _dot(dr_b, qh, TN)
            dkv_ref[:, zs] += _dot(pb, do_b, TN)

    kvb = pl.BlockSpec((ml, 2 * BW), lambda i: (0, 0))
    return pl.pallas_call(
        body, name="mem_attn_bwd", grid=(t // tq,),
        in_specs=[pl.BlockSpec((tq, W_M), lambda i: (i, 0)), kvb, pl.BlockSpec((tq, BW), lambda i: (i, 0))],
        out_specs=(pl.BlockSpec((tq, W_M), lambda i: (i, 0)), kvb),
        out_shape=(SDS((t, W_M), BF16), SDS((ml, 2 * BW), F32)),
        compiler_params=_params("arbitrary"),
    )(pm, kv, dy)


def _merge_fwd(ya, yb, ym, pg, wbr, exchange=None):
    t = ya.shape[0]
    tm = _tile(t, 512, 8)

    def body(ya_ref, yb_ref, ym_ref, pg_ref, w_ref, m_ref):
        acc = None
        for n, y_ref in enumerate((ya_ref, yb_ref, ym_ref)):
            up = _dot(y_ref[...], w_ref[n], NN)
            term = _sigmoid(pg_ref[:, n * D_MODEL:(n + 1) * D_MODEL].astype(F32)) * up
            acc = term if acc is None else acc + term
        m_ref[...] = acc.astype(BF16)

    yb_spec = pl.BlockSpec((tm, BW), lambda i: (i, 0))
    return _host_call(
        body, (ya, yb, ym, pg, wbr),
        [yb_spec, yb_spec, yb_spec, pl.BlockSpec((tm, W_G), lambda i: (i, 0)),
         pl.BlockSpec((3, BW, D_MODEL), lambda i: (0, 0, 0))],
        (pl.BlockSpec((tm, D_MODEL), lambda i: (i, 0)),), (SDS((t, D_MODEL), BF16),),
        name="merge_fwd", grid=(t // tm,), semantics=("parallel",), exchange=exchange)


def _merge_bwd(ya, yb, ym, pg, wbr, dm, exchange=None):
    t = ya.shape[0]
    tm = _tile(t, 256, 8)

    def body(ya_ref, yb_ref, ym_ref, pg_ref, w_ref, dm_ref, dya_ref, dyb_ref, dym_ref, dpg_ref, dw_ref):
        @pl.when(pl.program_id(0) == 0)
        def _():
            dw_ref[...] = jnp.zeros_like(dw_ref)

        dmf = dm_ref[...]
        for n, (y_ref, dy_ref) in enumerate(((ya_ref, dya_ref), (yb_ref, dyb_ref), (ym_ref, dym_ref))):
            cs_ = slice(n * D_MODEL, (n + 1) * D_MODEL)
            y = y_ref[...]
            w = w_ref[n]
            up = _dot(y, w, NN)
            gt = _sigmoid(pg_ref[:, cs_].astype(F32))
            dpg_ref[:, cs_] = (dmf * up * gt * (1.0 - gt)).astype(BF16)
            dup = (dmf * gt).astype(BF16)
            dy_ref[...] = _dot(dup, w, NT)
            dw_ref[n] += _dot(y, dup, TN)

    y_spec = pl.BlockSpec((tm, BW), lambda i: (i, 0))
    w_spec = pl.BlockSpec((3, BW, D_MODEL), lambda i: (0, 0, 0))
    return _host_call(
        body, (ya, yb, ym, pg, wbr, dm),
        [y_spec, y_spec, y_spec, pl.BlockSpec((tm, W_G), lambda i: (i, 0)), w_spec,
         pl.BlockSpec((tm, D_MODEL), lambda i: (i, 0))],
        (y_spec, y_spec, y_spec, pl.BlockSpec((tm, W_G), lambda i: (i, 0)), w_spec),
        (SDS((t, BW), F32), SDS((t, BW), F32), SDS((t, BW), F32), SDS((t, W_G), BF16), SDS((3, BW, D_MODEL), F32)),
        name="merge_bwd", grid=(t // tm,), semantics=("arbitrary",), exchange=exchange)


def _loss_head(x, g, target):
    t, d = x.shape
    tm = _tile(t, 512, 8)

    def body(x_ref, g_ref, t_ref, loss_ref, dx_ref, dg_ref):
        @pl.when(pl.program_id(0) == 0)
        def _():
            loss_ref[...] = jnp.zeros_like(loss_ref)
            dg_ref[...] = jnp.zeros_like(dg_ref)

        xf = x_ref[...]
        r = lax.rsqrt(jnp.mean(xf * xf, axis=-1, keepdims=True) + EPS)
        xh = xf * r
        err = xh * g_ref[...] - t_ref[...]
        per_tok = jnp.mean(err * err, axis=-1, keepdims=True)
        loss_ref[...] += 0.5 * jnp.sum(per_tok, axis=0, keepdims=True)
        dy = err * (1.0 / d)
        gd = dy * g_ref[...]
        dx_ref[...] = r * (gd - xh * jnp.mean(gd * xh, axis=-1, keepdims=True))
        dg_ref[...] += jnp.sum(dy * xh, axis=0, keepdims=True)

    row = pl.BlockSpec((tm, d), lambda i: (i, 0))
    vec = pl.BlockSpec((1, d), lambda i: (0, 0))
    return pl.pallas_call(
        body, name="loss_head", grid=(t // tm,),
        in_specs=[row, vec, row],
        out_specs=(pl.BlockSpec((1, 1), lambda i: (0, 0)), row, vec),
        out_shape=(SDS((1, 1), F32), SDS((t, d), F32), SDS((1, d), F32)),
        compiler_params=_params("arbitrary"),
    )(x, g, target)


def _layer_fwd(x, mem, w, tabs, next_shards=None):
    cs, sn = tabs
    riding = next_shards is not None
    h, pa, pb, pm, pg = _norm_in_proj(x, w["norm_g"], w["w_in_t"])
    q, qT, k, kT, vT = _attn_prep_fwd(pa, cs, sn, w["qg2"], w["kg2"])
    (o, lse, ya), gathered = _attn_fwd(qT, k, vT, pa, exchange=_gather_first_hop(next_shards) if riding else None)
    yb = _gmlp_fwd(pb, w["ln_g"], w["ln_b"], w["w_s"], w["b_s"])
    memn = _rmsnorm_fwd(mem, w["mem_g"], "mem_rmsnorm_fwd")
    kv = _mm(memn, w["w_kv"], "nn", "mem_kv")
    ym = _mem_attn_fwd(pm, kv)
    (merged,), gathered = _merge_fwd(ya, yb, ym, pg, w["w_br"], exchange=_gather_forward(gathered) if riding else None)
    x_next = _mm(merged, w["w_out"], "nn", "out_proj", add=x)
    saved = dict(x=x, h=h, pa=pa, pb=pb, pm=pm, pg=pg, q=q, qT=qT, k=k, kT=kT, vT=vT, o=o, lse=lse, ya=ya, yb=yb, ym=ym,
                 memn=memn, kv=kv, merged=merged)
    return x_next, saved, gathered


def _layer_bwd(dx_out, mem, w, s, tabs, pending=None, core=None, scatter_own=False):
    cs, sn = tabs
    t = dx_out.shape[0]
    riding = pending is not None
    dmerged = _mm(dx_out, w["w_out"], "nt", "d_merged")
    shard_rows = D_MODEL // N_DEV
    d_w_out = _mm_tn_by_owner(s["merged"], dx_out, "d_w_out")
    (dya, dyb, dym, dpg, d_w_br), recv = _merge_bwd(s["ya"], s["yb"], s["ym"], s["pg"], w["w_br"], dmerged,
                                                    exchange=_scatter_to_sibling(pending) if riding else None)
    parts = _pair_sums(pending, recv, core) if riding else None
    grp = A_HEADS // A_KV_HEADS
    lse_cols = s["lse"].reshape(A_KV_HEADS, grp, t).transpose(0, 2, 1)
    (dq, dkT, dvT), from_chips = _attn_bwd(s["q"], s["qT"], s["k"], s["kT"], s["vT"], s["pa"], s["o"], lse_cols, dya,
                                           exchange=_scatter_to_chips(parts) if riding else None)
    dpa, d_qg2, d_kg2 = _attn_prep_bwd(s["pa"], cs, sn, w["qg2"], w["kg2"], dq, dkT.reshape(-1, t), dvT.reshape(-1, t),
                                       dya, s["o"])
    dpb, d_w_s, d_b_s, d_ln_g, d_ln_b = _gmlp_bwd(s["pb"], w["ln_g"], w["ln_b"], w["w_s"], w["b_s"], dyb)
    dpm, dkv = _mem_attn_bwd(s["pm"], s["kv"], dym)
    d_w_kv = _mm_tn_by_owner(s["memn"], dkv, "d_w_kv")
    dmemn = _mm(dkv, w["w_kv"], "nt", "d_memn")
    d_mem_g = _rmsnorm_dg(mem, w["mem_g"], dmemn, "mem_rmsnorm_bwd")
    dps = (dpa, dpb, dpm, dpg)
    d_w_in = _split_w_in_t(*[_mm(dp, s["h"], "tn", "d_w_" + nm, out_dtype=BF16, tm_cap=1024, tn_cap=D_MODEL, tk_cap=4096)
                             for nm, dp in zip("abmg", dps)])
    grads = dict(w_in=d_w_in[:, None], w_mem_kv=d_w_kv.reshape(N_DEV, 1, shard_rows, -1), w_br=_split_w_br(d_w_br),
                 w_out=d_w_out.reshape(N_DEV, 1, shard_rows, -1))
    own = None
    if scatter_own:
        dests = [grads[n] for n in BIG]
        own_parts = _pair_sums(dests, _exchange_call(_scatter_to_sibling(dests), "rs_sibling_swap"), core)
        own = _scatter_to_chips(own_parts)
    (dx_in, d_norm_g), own_from_chips = _dh_rmsnorm_bwd(dps, w["w_in_t"], s["x"], w["norm_g"], dx_out, exchange=own)
    grads.update(norm_g=d_norm_g[0], q_norm_g=d_qg2[0, :HEAD_DIM] + d_qg2[0, HEAD_DIM:],
                 k_norm_g=d_kg2[0, :HEAD_DIM] + d_kg2[0, HEAD_DIM:], sg_ln_g=d_ln_g[0], sg_ln_b=d_ln_b[0],
                 w_s=d_w_s, b_s=d_b_s[:, :, 0], mem_norm_g=d_mem_g[0])
    return (dx_in, grads, ((parts, from_chips) if riding else None),
            ((own_parts, own_from_chips) if scatter_own else None))


def _layer_weights(l, w_in_t, w_kv, w_br, w_out, small):
    tile2 = lambda g: jnp.tile(g.reshape(1, -1), (1, 2))
    return dict(
        w_in_t=w_in_t, w_kv=w_kv, w_br=w_br, w_out=w_out,
        norm_g=small["norm_g"][l].reshape(1, -1), qg2=tile2(small["q_norm_g"][l]), kg2=tile2(small["k_norm_g"][l]),
        ln_g=small["sg_ln_g"][l].reshape(1, -1), ln_b=small["sg_ln_b"][l].reshape(1, -1),
        w_s=small["w_s"][l], b_s=small["b_s"][l][:, :, None], mem_g=small["mem_norm_g"][l].reshape(1, -1))


def _position():
    x, y, c = lax.axis_index("x"), lax.axis_index("y"), lax.axis_index("c")
    return x, y, c, [(1 - x, y), (x, 1 - y), (1 - x, 1 - y)]


def _gather_first_hop(shards):
    n = len(shards)

    def make(x_refs, out_refs, sems):
        send_sems, recv_sems, local_sems = sems
        x, y, c, chips = _position()
        me = 4 * x + 2 * y + c
        peers = [(x, y, 1 - c)] + [(cx, cy, c) for cx, cy in chips]
        copies = [pltpu.make_async_copy(x_refs[t], out_refs[t].at[me], local_sems.at[t]) for t in range(n)]
        copies += [pltpu.make_async_remote_copy(
            src_ref=x_refs[t], dst_ref=out_refs[t].at[me], send_sem=send_sems.at[t, k], recv_sem=recv_sems.at[t, k],
            device_id=peer, device_id_type=MESH_ID) for t in range(n) for k, peer in enumerate(peers)]
        return copies

    return _Exchange(shards, [SDS((N_DEV,) + a.shape, a.dtype) for a in shards],
                     [pltpu.SemaphoreType.DMA((n, 4)), pltpu.SemaphoreType.DMA((n, 4)), pltpu.SemaphoreType.DMA((n,))], make)


def _gather_forward(gathered):
    n = len(gathered)

    def make(in_refs, out_refs, sems):
        send_sems, recv_sems = sems
        x, y, c, chips = _position()
        return [pltpu.make_async_remote_copy(
            src_ref=in_refs[t].at[4 * cx + 2 * cy + c], dst_ref=out_refs[t].at[4 * cx + 2 * cy + c],
            send_sem=send_sems.at[t, j], recv_sem=recv_sems.at[t, j], device_id=(x, y, 1 - c), device_id_type=MESH_ID)
            for t in range(n) for j, (cx, cy) in enumerate(chips)]

    return _Exchange(gathered, [SDS(a.shape, a.dtype) for a in gathered],
                     [pltpu.SemaphoreType.DMA((n, 3)), pltpu.SemaphoreType.DMA((n, 3))], make,
                     aliases={t: t for t in range(n)})


def _all_gather(shards, name):
    return _exchange_call(_gather_forward(_exchange_call(_gather_first_hop(shards), name + "_hop1")), name + "_hop2")


def _shard_segments(d):
    shard = IN_WIDTH // N_DEV
    lo, hi = d * shard, (d + 1) * shard
    out = []
    for g in range(4):
        a, b = max(lo, GROUP_OFFS[g]), min(hi, GROUP_OFFS[g + 1])
        if a < b:
            out.append((g, a - GROUP_OFFS[g], b - GROUP_OFFS[g], a - lo))
    return out


def _split_w_in_t(da, db, dm, dg):
    cols = da.shape[1]
    shard = IN_WIDTH // N_DEV
    tc = _tile(cols, 256)

    def body(a_ref, b_ref, m_ref, g_ref, o_ref):
        ins = (a_ref, b_ref, m_ref, g_ref)
        for d in range(N_DEV):
            for g, lo, hi, off in _shard_segments(d):
                o_ref[_core_major(d), off:off + hi - lo, :] = ins[g][lo:hi, :]

    return pl.pallas_call(
        body, name="split_w_in", grid=(cols // tc,),
        in_specs=[pl.BlockSpec((a.shape[0], tc), lambda j: (0, j)) for a in (da, db, dm, dg)],
        out_specs=pl.BlockSpec((N_DEV, shard, tc), lambda j: (0, 0, j)),
        out_shape=SDS((N_DEV, shard, cols), da.dtype), compiler_params=_params("parallel"),
    )(da, db, dm, dg)


def _assemble_w_br(gathered):
    _, nb, rows, shard = gathered.shape

    def body(g_ref, o_ref):
        o_ref[...] = g_ref[...]

    return pl.pallas_call(
        body, name="assemble_w_br", grid=(N_DEV,),
        in_specs=[pl.BlockSpec((None, nb, rows, shard), lambda d: (d, 0, 0, 0))],
        out_specs=pl.BlockSpec((nb, rows, shard), lambda d: (0, 0, d)),
        out_shape=SDS((nb, rows, N_DEV * shard), gathered.dtype), compiler_params=_params("parallel"),
    )(gathered)


def _split_w_br(dw):
    nb, rows, cols = dw.shape
    shard = cols // N_DEV

    def body(d_ref, o_ref):
        o_ref[...] = d_ref[...].astype(BF16)

    return pl.pallas_call(
        body, name="split_w_br", grid=(N_DEV,),
        in_specs=[pl.BlockSpec((nb, rows, shard), lambda d: (0, 0, d))],
        out_specs=pl.BlockSpec((None, nb, rows, shard), lambda d: (_core_major(d), 0, 0, 0)),
        out_shape=SDS((N_DEV, nb, rows, shard), BF16), compiler_params=_params("parallel"),
    )(dw)


def _scatter_to_sibling(dests):
    n = len(dests)

    def make(d_refs, recv_refs, sems):
        send_sems, recv_sems = sems
        x, y, c, _ = _position()
        return [pltpu.make_async_remote_copy(
            src_ref=d_refs[t].at[pl.ds((1 - c) * 4, 4)], dst_ref=recv_refs[t], send_sem=send_sems.at[t],
            recv_sem=recv_sems.at[t], device_id=(x, y, 1 - c), device_id_type=MESH_ID) for t in range(n)]

    return _Exchange(dests, [SDS((4,) + a.shape[1:], a.dtype) for a in dests],
                     [pltpu.SemaphoreType.DMA((n,)), pltpu.SemaphoreType.DMA((n,))], make)


def _scatter_to_chips(parts):
    n = len(parts)

    def make(p_refs, recv_refs, sems):
        send_sems, recv_sems = sems
        _, _, c, chips = _position()
        return [pltpu.make_async_remote_copy(
            src_ref=p_refs[t].at[2 * cx + cy], dst_ref=recv_refs[t].at[k], send_sem=send_sems.at[t, k],
            recv_sem=recv_sems.at[t, k], device_id=(cx, cy, c), device_id_type=MESH_ID)
            for t in range(n) for k, (cx, cy) in enumerate(chips)]

    return _Exchange(parts, [SDS((3,) + a.shape[1:], a.dtype) for a in parts],
                     [pltpu.SemaphoreType.DMA((n, 3)), pltpu.SemaphoreType.DMA((n, 3))], make)


def _pair_sums(dests, recv, core):
    return [_pair_sum(d, r, core, "rs_pair_sum_" + n) for n, d, r in zip(BIG, dests, recv)]


def _pair_sum(dest, recv, core, name):
    _, na, r, cdim = dest.shape

    def body(core_ref, a_ref, b_ref, o_ref):
        o_ref[...] = (a_ref[...].astype(F32) + b_ref[...].astype(F32)).astype(o_ref.dtype)

    blk = pl.BlockSpec((None, na, r, cdim), lambda j, core_ref: (j, 0, 0, 0))
    return pl.pallas_call(
        body, name=name, out_shape=SDS(recv.shape, BF16),
        grid_spec=pltpu.PrefetchScalarGridSpec(
            num_scalar_prefetch=1, grid=(4,),
            in_specs=[pl.BlockSpec((None, na, r, cdim), lambda j, core_ref: (core_ref[0] * 4 + j, 0, 0, 0)), blk],
            out_specs=blk),
        compiler_params=_params("parallel"),
    )(core, dest, recv)


def _adamw_math(w, g, m, v):
    m = ADAM_B1 * m + (1.0 - ADAM_B1) * g
    v = ADAM_B2 * v + (1.0 - ADAM_B2) * (g * g)
    m_hat = m / (1.0 - ADAM_B1 ** ADAM_STEP)
    v_hat = v / (1.0 - ADAM_B2 ** ADAM_STEP)
    delta = -ADAM_LR * (m_hat / (jnp.sqrt(v_hat) + ADAM_EPS) + ADAM_WD * w)
    return delta, m, v


def _sum_and_adamw(parts, w, m, v, name):
    n, r, ln = parts.shape
    tr = _tile(r, 512, 16)

    def body(p_ref, w_ref, m_ref, v_ref, g_out, d_out, m_out, v_out):
        g = p_ref[0].astype(F32)
        for j in range(1, n):
            g = g + p_ref[j].astype(F32)
        delta, nm, nv = _adamw_math(w_ref[...], g, m_ref[...], v_ref[...])
        g_out[...] = g
        d_out[...] = delta
        m_out[...] = nm
        v_out[...] = nv

    blk = pl.BlockSpec((tr, ln), lambda i: (i, 0))
    return pl.pallas_call(
        body, name=name, grid=(r // tr,),
        in_specs=[pl.BlockSpec((n, tr, ln), lambda i: (0, i, 0)), blk, blk, blk],
        out_specs=(blk, blk, blk, blk), out_shape=tuple(SDS((r, ln), F32) for _ in range(4)),
        compiler_params=_params("parallel"),
    )(parts, w, m, v)


BIG = ("w_in", "w_mem_kv", "w_br", "w_out")
SMALL = ("norm_g", "q_norm_g", "k_norm_g", "sg_ln_g", "sg_ln_b", "w_s", "b_s", "mem_norm_g", "final_g")


def _pack(arrs, row_unit=16):
    flat = jnp.concatenate([a.reshape(-1) for a in arrs])
    pad = (-flat.shape[0]) % (row_unit * LANES)
    if pad:
        flat = jnp.concatenate([flat, jnp.zeros((pad,), flat.dtype)])
    return flat.reshape(-1, LANES)


def _unpack(buf, shapes):
    flat = buf.reshape(-1)
    out, off = [], 0
    for shp in shapes:
        n = 1
        for s_ in shp:
            n *= s_
        out.append(flat[off:off + n].reshape(shp))
        off += n
    return out


def _shard_sum_adamw(part, from_chips, chip, w, m, v, layer, prev, name):
    _, na, r, cdim = part.shape
    flat = w.ndim == 3
    assert not flat or na == 1
    tr = _tile(r, max(8, (256 * 1024) // (na * cdim)), 8)
    n_prev = 0 if prev is None else len(prev)

    def body(chip_ref, p_ref, f_ref, w_ref, m_ref, v_ref, *rest):
        g_out, d_out, m_out, v_out = rest[n_prev:]
        g = p_ref[...].astype(F32)
        for j in range(3):
            g = g + f_ref[j].astype(F32)
        delta, nm, nv = _adamw_math(w_ref[...], g, m_ref[...], v_ref[...])
        g_out[...] = g
        d_out[...] = delta
        m_out[...] = nm
        v_out[...] = nv

    a_blk = None if flat else na
    if flat:
        lay = pl.BlockSpec((None, tr, cdim), lambda i, chip_ref: (layer, i, 0))
    else:
        lay = pl.BlockSpec((None, na, tr, cdim), lambda i, chip_ref: (layer, 0, i, 0))
    return pl.pallas_call(
        body, name=name, out_shape=tuple(SDS(w.shape, F32) for _ in range(4)),
        grid_spec=pltpu.PrefetchScalarGridSpec(
            num_scalar_prefetch=1, grid=(r // tr,),
            in_specs=[pl.BlockSpec((None, a_blk, tr, cdim), lambda i, chip_ref: (chip_ref[0], 0, i, 0)),
                      pl.BlockSpec((3, a_blk, tr, cdim), lambda i, chip_ref: (0, 0, i, 0)), lay, lay, lay]
            + [HBM] * n_prev,
            out_specs=(lay, lay, lay, lay)),
        input_output_aliases={6 + j: j for j in range(n_prev)},
        compiler_params=_params("parallel"),
    )(chip, part, from_chips, w, m, v, *(prev or ()))


def kernel(x, mem, norm_g, w_in, q_norm_g, k_norm_g, sg_ln_g, sg_ln_b, w_s, b_s, mem_norm_g, w_mem_kv, w_br, w_out, final_g, loss_target, m_norm_g, m_w_in, m_q_norm_g, m_k_norm_g, m_sg_ln_g, m_sg_ln_b, m_w_s, m_b_s, m_mem_norm_g, m_w_mem_kv, m_w_br, m_w_out, m_final_g, v_norm_g, v_w_in, v_q_norm_g, v_k_norm_g, v_sg_ln_g, v_sg_ln_b, v_w_s, v_b_s, v_mem_norm_g, v_w_mem_kv, v_w_br, v_w_out, v_final_g):
    wts = dict(norm_g=norm_g, w_in=w_in, q_norm_g=q_norm_g, k_norm_g=k_norm_g, sg_ln_g=sg_ln_g, sg_ln_b=sg_ln_b,
               w_s=w_s, b_s=b_s, mem_norm_g=mem_norm_g, w_mem_kv=w_mem_kv, w_br=w_br, w_out=w_out, final_g=final_g)
    mom1 = dict(norm_g=m_norm_g, w_in=m_w_in, q_norm_g=m_q_norm_g, k_norm_g=m_k_norm_g, sg_ln_g=m_sg_ln_g,
                sg_ln_b=m_sg_ln_b, w_s=m_w_s, b_s=m_b_s, mem_norm_g=m_mem_norm_g, w_mem_kv=m_w_mem_kv, w_br=m_w_br,
                w_out=m_w_out, final_g=m_final_g)
    mom2 = dict(norm_g=v_norm_g, w_in=v_w_in, q_norm_g=v_q_norm_g, k_norm_g=v_k_norm_g, sg_ln_g=v_sg_ln_g,
                sg_ln_b=v_sg_ln_b, w_s=v_w_s, b_s=v_b_s, mem_norm_g=v_mem_norm_g, w_mem_kv=v_w_mem_kv, w_br=v_w_br,
                w_out=v_w_out, final_g=v_final_g)
    dp = w_in.shape[0]
    core = lax.axis_index("c").astype(jnp.int32).reshape(1)
    chip = (2 * lax.axis_index("x") + lax.axis_index("y")).astype(jnp.int32).reshape(1)

    t_in = lambda a: jnp.swapaxes(a, 1, 2)
    wts, mom1, mom2 = [dict(d_, w_in=t_in(d_["w_in"])) for d_ in (wts, mom1, mom2)]

    shard_bf = {n: wts[n].astype(BF16) for n in BIG}
    shards = lambda l: [shard_bf[n][l] for n in BIG]
    x_l, mem_l = x[0], mem[0]
    tabs = _rope_tables(x_l.shape[0])

    gathered = _all_gather(shards(0), "weights_all_gather")
    layers, saved = [], []
    for l in range(dp):
        g_in, g_kv, g_br, g_out = gathered
        layers.append(_layer_weights(l, g_in.reshape(IN_WIDTH, -1), g_kv.reshape(D_MODEL, -1), _assemble_w_br(g_br),
                                     g_out.reshape(D_MODEL, -1), wts))
        x_l, s, gathered = _layer_fwd(x_l, mem_l, layers[l], tabs, next_shards=shards(l + 1) if l + 1 < dp else None)
        saved.append(s)
    loss_local, dx, d_final_g = _loss_head(x_l, final_g.reshape(1, -1), loss_target[0])
    loss = lax.psum(loss_local[0, 0], AXES)

    def finish(l, parts, from_chips, prev):
        return {n: _shard_sum_adamw(p, f, chip, wts[n], mom1[n], mom2[n], l, None if prev is None else prev[n],
                                    "sum_adamw_" + n)
                for n, p, f in zip(BIG, parts, from_chips)}

    grads, updated, pending = [None] * dp, None, None
    for l in reversed(range(dp)):
        dx, grads[l], scattered, own = _layer_bwd(dx, mem_l, layers[l], saved[l], tabs, pending=pending, core=core,
                                                  scatter_own=(l == 0))
        if scattered is not None:
            updated = finish(l + 1, *scattered, updated)
        pending = [grads[l][n] for n in BIG]
    updated = finish(0, *own, updated)
    grad_x = dx
    big_out = [{n: (t_in(updated[n][k]) if n == "w_in" else updated[n][k]) for n in BIG} for k in range(4)]

    small_g = {n: jnp.stack([g[n] for g in grads]) for n in SMALL if n != "final_g"}
    small_g["final_g"] = d_final_g
    (all_small,) = _all_gather([_pack([small_g[n] for n in SMALL])], "small_all_gather")
    small_bufs = _sum_and_adamw(
        all_small, _pack([wts[n] for n in SMALL]), _pack([mom1[n] for n in SMALL]), _pack([mom2[n] for n in SMALL]),
        "small_sum_adamw")

    outs = []
    for big_vals, small_buf in zip(big_out, small_bufs):
        vals = dict(big_vals)
        vals.update(zip(SMALL, _unpack(small_buf, [wts[n].shape for n in SMALL])))
        outs.append(vals)
    order = ("norm_g", "w_in", "q_norm_g", "k_norm_g", "sg_ln_g", "sg_ln_b", "w_s", "b_s", "mem_norm_g", "w_mem_kv",
             "w_br", "w_out", "final_g")
    result = [loss, grad_x[None]]
    for vals in outs:
        result += [vals[n] for n in order]
    return tuple(result)
```

```python
import functools

import jax
import jax.numpy as jnp
from jax import lax
from jax.experimental import pallas as pl
from jax.experimental.pallas import tpu as pltpu

F32 = jnp.float32
BF16 = jnp.bfloat16
SDS = jax.ShapeDtypeStruct
MESH_ID = pl.DeviceIdType.MESH
AXES = ("x", "y", "c")
N_DEV = 8

D_MODEL = 1024
DEPTH = 4
GRID_W = 64
CHUNK = 128
ROPE_THETA = 10000.0
EPS = 1e-6
HEAD_DIM = 64
A_HEADS = 8
A_KV_HEADS = 2
B_GROUPS = 4
M_HEADS = 4
M_HEAD_DIM = 128
BW = 512
W_A = 1280
W_B = 1536
W_M = 1024
W_G = 3072
IN_WIDTH = W_A + W_B + W_M + W_G
GROUP_OFFS = (0, W_A, W_A + W_B, W_A + W_B + W_M, IN_WIDTH)

ADAM_LR = 0.001
ADAM_B1 = 0.9
ADAM_B2 = 0.999
ADAM_EPS = 1e-08
ADAM_WD = 0.01
ADAM_STEP = 10

LANES = 128
KEY_CHUNK = 2048
VMEM_LIMIT = 52 * 1024 * 1024


def _tile(n, cap, unit=LANES):
    if n <= cap:
        return n
    t = (cap // unit) * unit
    while n % t:
        t -= unit
    return t


def _params(*sem):
    return pltpu.CompilerParams(dimension_semantics=sem, vmem_limit_bytes=VMEM_LIMIT)


def _sigmoid(z):
    return 0.5 * jnp.tanh(0.5 * z) + 0.5


def _silu_and_grad(z):
    s = _sigmoid(z)
    return z * s, s * (1.0 + z * (1.0 - s))


def _dot(a, b, dims):
    return lax.dot_general(a, b, (dims, ((), ())), preferred_element_type=F32)


NN = ((1,), (0,))
NT = ((1,), (1,))
TN = ((0,), (0,))
HBM = pl.BlockSpec(memory_space=pl.ANY)


class _Exchange:
    def __init__(self, ins, out_shapes, sems, make, aliases=None):
        self.ins, self.out_shapes, self.sems, self.make = list(ins), list(out_shapes), list(sems), make
        self.aliases = dict(aliases or {})

    def start(self, in_refs, out_refs, sems):
        for cp in self.make(in_refs, out_refs, sems):
            cp.start()

    def finish(self, in_refs, out_refs, sems):
        for cp in self.make(in_refs, out_refs, sems):
            cp.wait()


def _host_call(body, ins, in_specs, out_specs, out_shape, *, name, grid, semantics, scratch=(), exchange=None):
    ins, out_shape, scratch = list(ins), list(out_shape), list(scratch)
    if exchange is None:
        res = pl.pallas_call(
            body, name=name, grid=grid, in_specs=list(in_specs), out_specs=tuple(out_specs), out_shape=tuple(out_shape),
            scratch_shapes=scratch, compiler_params=_params(*semantics))(*ins)
        return tuple(res), ()
    n_in, n_out, n_scr = len(ins), len(out_shape), len(scratch)
    x_in, x_out = len(exchange.ins), len(exchange.out_shapes)

    def carrying(*refs):
        o0 = n_in + x_in
        s0 = o0 + n_out + x_out
        c_in, c_out, c_sems = refs[n_in:o0], refs[o0 + n_out:s0], refs[s0 + n_scr:]
        ids = [pl.program_id(a) for a in range(len(grid))]
        first = functools.reduce(jnp.logical_and, [i == 0 for i in ids])
        last = functools.reduce(jnp.logical_and, [i == g - 1 for i, g in zip(ids, grid)])

        @pl.when(first)
        def _():
            exchange.start(c_in, c_out, c_sems)

        body(*refs[:n_in], *refs[o0:o0 + n_out], *refs[s0:s0 + n_scr])

        @pl.when(last)
        def _():
            exchange.finish(c_in, c_out, c_sems)

    res = pl.pallas_call(
        carrying, name=name, grid=grid, in_specs=list(in_specs) + [HBM] * x_in,
        out_specs=tuple(out_specs) + (HBM,) * x_out, out_shape=tuple(out_shape) + tuple(exchange.out_shapes),
        scratch_shapes=scratch + exchange.sems,
        input_output_aliases={n_in + i: n_out + o for i, o in exchange.aliases.items()},
        compiler_params=_params(*(["arbitrary"] * len(grid))))(*ins, *exchange.ins)
    return tuple(res[:n_out]), tuple(res[n_out:])


def _exchange_call(exchange, name):
    x_in = len(exchange.ins)

    def body(*refs):
        x_out = len(exchange.out_shapes)
        c_in, c_out, c_sems = refs[:x_in], refs[x_in:x_in + x_out], refs[x_in + x_out:]
        exchange.start(c_in, c_out, c_sems)
        exchange.finish(c_in, c_out, c_sems)

    return pl.pallas_call(
        body, name=name, in_specs=[HBM] * x_in, out_specs=tuple([HBM] * len(exchange.out_shapes)),
        out_shape=tuple(exchange.out_shapes), scratch_shapes=exchange.sems, input_output_aliases=exchange.aliases,
    )(*exchange.ins)


def _core_major(d):
    return (d % 2) * 4 + d // 2


def _mm(a, b, mode, name, add=None, out_dtype=F32, tm_cap=512, tn_cap=1536, tk_cap=1536):
    if mode == "nn":
        (m, k), (k2, n) = a.shape, b.shape
    elif mode == "nt":
        (m, k), (n, k2) = a.shape, b.shape
    else:
        (k, m), (k2, n) = a.shape, b.shape
    assert k == k2, (a.shape, b.shape, mode)
    tm = _tile(m, tm_cap, 8 if mode != "tn" else LANES)
    tn = _tile(n, tn_cap)
    tk = _tile(k, tk_cap, LANES if mode != "tn" else 16)
    nk = k // tk
    dims = {"nn": NN, "nt": NT, "tn": TN}[mode]
    a_spec = {"nn": pl.BlockSpec((tm, tk), lambda i, j, kk: (i, kk)),
              "nt": pl.BlockSpec((tm, tk), lambda i, j, kk: (i, kk)),
              "tn": pl.BlockSpec((tk, tm), lambda i, j, kk: (kk, i))}[mode]
    b_spec = {"nn": pl.BlockSpec((tk, tn), lambda i, j, kk: (kk, j)),
              "nt": pl.BlockSpec((tn, tk), lambda i, j, kk: (j, kk)),
              "tn": pl.BlockSpec((tk, tn), lambda i, j, kk: (kk, j))}[mode]
    o_spec = pl.BlockSpec((tm, tn), lambda i, j, kk: (i, j))
    has_add = add is not None

    def body(*refs):
        a_ref, b_ref = refs[0], refs[1]
        add_ref = refs[2] if has_add else None
        o_ref = refs[3] if has_add else refs[2]
        part = _dot(a_ref[...].astype(BF16), b_ref[...].astype(BF16), dims)
        if nk == 1:
            if has_add:
                part = part + add_ref[...]
            o_ref[...] = part.astype(out_dtype)
        else:
            acc = refs[-1]
            kk = pl.program_id(2)

            @pl.when(kk == 0)
            def _():
                acc[...] = part + add_ref[...] if has_add else part

            @pl.when(kk > 0)
            def _():
                acc[...] += part

            @pl.when(kk == nk - 1)
            def _():
                o_ref[...] = acc[...].astype(out_dtype)

    ins = [a, b] + ([add] if has_add else [])
    in_specs = [a_spec, b_spec] + ([o_spec] if has_add else [])
    return pl.pallas_call(
        body, name=name, grid=(m // tm, n // tn, nk), in_specs=in_specs, out_specs=o_spec,
        out_shape=SDS((m, n), out_dtype),
        scratch_shapes=[pltpu.VMEM((tm, tn), F32)] if nk > 1 else [],
        compiler_params=_params("parallel", "parallel", "arbitrary"),
    )(*ins)


def _mm_tn_by_owner(a, b, name):
    (k, m), (k2, n) = a.shape, b.shape
    assert k == k2 and m % N_DEV == 0, (a.shape, b.shape)
    tk = _tile(k, 2048, 16)
    nk = k // tk
    rows = m // N_DEV

    def body(a_ref, b_ref, o_ref, acc):
        kk = pl.program_id(0)
        part = _dot(a_ref[...].astype(BF16), b_ref[...].astype(BF16), TN)

        @pl.when(kk == 0)
        def _():
            acc[...] = part

        @pl.when(kk > 0)
        def _():
            acc[...] += part

        @pl.when(kk == nk - 1)
        def _():
            for d in range(N_DEV):
                pos = _core_major(d)
                o_ref[pos * rows:(pos + 1) * rows, :] = acc[d * rows:(d + 1) * rows, :].astype(BF16)

    return pl.pallas_call(
        body, name=name, grid=(nk,),
        in_specs=[pl.BlockSpec((tk, m), lambda kk: (kk, 0)), pl.BlockSpec((tk, n), lambda kk: (kk, 0))],
        out_specs=pl.BlockSpec((m, n), lambda kk: (0, 0)), out_shape=SDS((m, n), BF16),
        scratch_shapes=[pltpu.VMEM((m, n), F32)], compiler_params=_params("arbitrary"),
    )(a, b)


def _rmsnorm_fwd(x, g, name):
    t, d = x.shape
    tm = _tile(t, 512, 8)

    def body(x_ref, g_ref, h_ref):
        xf = x_ref[...]
        r = lax.rsqrt(jnp.mean(xf * xf, axis=-1, keepdims=True) + EPS)
        h_ref[...] = (xf * r * g_ref[...]).astype(BF16)

    return pl.pallas_call(
        body, name=name, grid=(t // tm,),
        in_specs=[pl.BlockSpec((tm, d), lambda i: (i, 0)), pl.BlockSpec((1, d), lambda i: (0, 0))],
        out_specs=pl.BlockSpec((tm, d), lambda i: (i, 0)),
        out_shape=SDS((t, d), BF16), compiler_params=_params("parallel"),
    )(x, g)


def _rmsnorm_bwd_math(xf, g, dh):
    r = lax.rsqrt(jnp.mean(xf * xf, axis=-1, keepdims=True) + EPS)
    xh = xf * r
    gd = dh * g
    dx = r * (gd - xh * jnp.mean(gd * xh, axis=-1, keepdims=True))
    return dx, jnp.sum(dh * xh, axis=0, keepdims=True)


def _rmsnorm_dg(x, g, dh, name):
    t, d = x.shape
    tm = _tile(t, 512, 8)

    def body(x_ref, g_ref, dh_ref, dg_ref):
        @pl.when(pl.program_id(0) == 0)
        def _():
            dg_ref[...] = jnp.zeros_like(dg_ref)

        dg_ref[...] += _rmsnorm_bwd_math(x_ref[...], g_ref[...], dh_ref[...])[1]

    row = pl.BlockSpec((tm, d), lambda i: (i, 0))
    vec = pl.BlockSpec((1, d), lambda i: (0, 0))
    return pl.pallas_call(
        body, name=name, grid=(t // tm,), in_specs=[row, vec, row], out_specs=vec, out_shape=SDS((1, d), F32),
        compiler_params=_params("arbitrary"),
    )(x, g, dh)


def _group_rows(p):
    return slice(GROUP_OFFS[p], GROUP_OFFS[p + 1])


def _norm_in_proj(x, g, w_t):
    t, d = x.shape
    tm = _tile(t, 512, 16)
    widths = [GROUP_OFFS[p + 1] - GROUP_OFFS[p] for p in range(4)]

    def body(x_ref, g_ref, w_ref, h_ref, *outs):
        xf = x_ref[...]
        r = lax.rsqrt(jnp.mean(xf * xf, axis=-1, keepdims=True) + EPS)
        hb = (xf * r * g_ref[...]).astype(BF16)
        h_ref[...] = hb
        for p, o_ref in enumerate(outs):
            o_ref[...] = _dot(hb, w_ref[_group_rows(p), :], NT).astype(BF16)

    row = lambda wd: pl.BlockSpec((tm, wd), lambda i: (i, 0))
    return pl.pallas_call(
        body, name="norm_in_proj", grid=(t // tm,),
        in_specs=[row(d), pl.BlockSpec((1, d), lambda i: (0, 0)),
                  pl.BlockSpec(w_t.shape, lambda i: (0, 0), pipeline_mode=pl.Buffered(1))],
        out_specs=(row(d),) + tuple(row(wd) for wd in widths),
        out_shape=(SDS((t, d), BF16),) + tuple(SDS((t, wd), BF16) for wd in widths),
        compiler_params=_params("parallel"),
    )(x, g, w_t)


def _dh_rmsnorm_bwd(dps, w_t, x, g, dres, exchange=None):
    t, d = x.shape
    tm = _tile(t, 256, 16)
    n = len(dps)

    def body(*refs):
        dp_refs, w_ref = refs[:n], refs[n]
        x_ref, g_ref, dres_ref, dx_ref, dg_ref = refs[n + 1:]
        dh = None
        for p, dp_ref in enumerate(dp_refs):
            part = _dot(dp_ref[...], w_ref[_group_rows(p), :], NN)
            dh = part if dh is None else dh + part
        dx, dg = _rmsnorm_bwd_math(x_ref[...], g_ref[...], dh)
        dx_ref[...] = dx + dres_ref[...]

        @pl.when(pl.program_id(0) == 0)
        def _():
            dg_ref[...] = jnp.zeros_like(dg_ref)

        dg_ref[...] += dg

    row = pl.BlockSpec((tm, d), lambda i: (i, 0))
    vec = pl.BlockSpec((1, d), lambda i: (0, 0))
    return _host_call(
        body, list(dps) + [w_t, x, g, dres],
        [pl.BlockSpec((tm, a.shape[1]), lambda i: (i, 0)) for a in dps]
        + [pl.BlockSpec(w_t.shape, lambda i: (0, 0), pipeline_mode=pl.Buffered(1)), row, vec, row],
        (row, vec), (SDS((t, d), F32), SDS((1, d), F32)),
        name="dh_rmsnorm_bwd", grid=(t // tm,), semantics=("arbitrary",), exchange=exchange)


def _rope_tables(t):
    rows = t // GRID_W
    row = jnp.repeat(jnp.arange(rows, dtype=F32), GRID_W)
    col = jnp.tile(jnp.arange(GRID_W, dtype=F32), rows)
    n_freq = HEAD_DIM // 4
    inv = ROPE_THETA ** (-jnp.arange(n_freq, dtype=F32) / n_freq)
    ang = jnp.stack([row[:, None] * inv, col[:, None] * inv], axis=1)
    cos, sin = jnp.cos(ang), jnp.sin(ang)
    c64 = jnp.concatenate([cos[:, 0], cos[:, 0], cos[:, 1], cos[:, 1]], axis=-1)
    s64 = jnp.concatenate([-sin[:, 0], sin[:, 0], -sin[:, 1], sin[:, 1]], axis=-1)
    return jnp.tile(c64, (1, 2)), jnp.tile(s64, (1, 2))


def _head_sums(v, lane):
    lo = jnp.sum(jnp.where(lane < HEAD_DIM, v, 0.0), axis=-1, keepdims=True)
    hi = jnp.sum(jnp.where(lane < HEAD_DIM, 0.0, v), axis=-1, keepdims=True)
    return jnp.where(lane < HEAD_DIM, lo, hi)


def _swap16(v, lane):
    return jnp.where((lane % 32) < 16, pltpu.roll(v, LANES - 16, 1), pltpu.roll(v, 16, 1))


def _attn_prep_fwd(pa, cs, sn, qg2, kg2):
    t = pa.shape[0]
    tq = _tile(t, 512, LANES)
    scale = HEAD_DIM ** -0.5

    def body(pa_ref, cs_ref, sn_ref, qg_ref, kg_ref, q_ref, qT_ref, k_ref, kT_ref, vT_ref):
        lane = lax.broadcasted_iota(jnp.int32, (tq, LANES), 1)
        c, s = cs_ref[...], sn_ref[...]

        def norm_rope(xg, g2):
            r = lax.rsqrt(_head_sums(xg * xg, lane) * (1.0 / HEAD_DIM) + EPS)
            xn = xg * r * g2
            return xn * c + _swap16(xn, lane) * s

        for gi in range(4):
            sl = slice(gi * LANES, (gi + 1) * LANES)
            qr = norm_rope(pa_ref[:, sl].astype(F32), qg_ref[...]) * scale
            q_ref[:, sl] = qr.astype(BF16)
            qT_ref[sl, :] = qr.T.astype(BF16)
        kr = norm_rope(pa_ref[:, 512:640].astype(F32), kg_ref[...])
        kT_ref[...] = kr.T.astype(BF16)
        vT_ref[...] = pa_ref[:, 640:768].astype(F32).T.astype(BF16)
        kr = kr.astype(BF16)
        for kvh in range(A_KV_HEADS):
            k_ref[kvh] = kr[:, kvh * HEAD_DIM:(kvh + 1) * HEAD_DIM]

    row = lambda w: pl.BlockSpec((tq, w), lambda i: (i, 0))
    col = lambda r: pl.BlockSpec((r, tq), lambda i: (0, i))
    vec = pl.BlockSpec((1, LANES), lambda i: (0, 0))
    hm = pl.BlockSpec((A_KV_HEADS, tq, HEAD_DIM), lambda i: (0, i, 0))
    return pl.pallas_call(
        body, name="attn_prep_fwd", grid=(t // tq,),
        in_specs=[row(W_A), row(LANES), row(LANES), vec, vec],
        out_specs=(row(BW), col(BW), hm, col(LANES), col(LANES)),
        out_shape=(SDS((t, BW), BF16), SDS((BW, t), BF16), SDS((A_KV_HEADS, t, HEAD_DIM), BF16),
                   SDS((LANES, t), BF16), SDS((LANES, t), BF16)),
        compiler_params=_params("parallel"),
    )(pa, cs, sn, qg2, kg2)


def _attn_fwd(qT, k, vT, pa, exchange=None):
    t = qT.shape[1]
    tq = _tile(t, 256, LANES)
    grp = A_HEADS // A_KV_HEADS

    ck = _tile(t, KEY_CHUNK, LANES)

    def body(qT_ref, k_ref, vT_ref, pa_ref, o_ref, lse_ref, y_ref):
        def finish_pair(pair, o_pair):
            o2 = jnp.concatenate(o_pair, axis=0).T
            cols = slice(pair * LANES, (pair + 1) * LANES)
            o_ref[:, cols] = o2
            z = pa_ref[:, 768 + pair * LANES:768 + (pair + 1) * LANES].astype(F32)
            y_ref[:, cols] = (o2 * (z * _sigmoid(z))).astype(BF16)

        n_c = t // ck
        items = [(h, c) for h in range(A_HEADS) for c in range(n_c)]

        def scores(h, c):
            return _dot(k_ref[h // grp, c * ck:(c + 1) * ck, :], qT_ref[h * HEAD_DIM:(h + 1) * HEAD_DIM, :], NN)

        def weighted_values(h, c, pT):
            kvh = h // grp
            return _dot(vT_ref[kvh * HEAD_DIM:(kvh + 1) * HEAD_DIM, c * ck:(c + 1) * ck], pT, NN)

        o_pair, sT, pending, acc = [], scores(*items[0]), None, None
        for i in range(len(items) + 1):
            item = items[i] if i < len(items) else None
            sT_next = scores(*items[i + 1]) if i + 1 < len(items) else None
            if pending is not None:
                (ph, pc, pT, alpha, l_run) = pending
                o_c = weighted_values(ph, pc, pT)
                acc = o_c if alpha is None else acc * alpha + o_c
                if pc == n_c - 1:
                    o_pair.append(acc / l_run)
                    if ph % 2:
                        finish_pair(ph // 2, o_pair)
                        o_pair = []
                pending = None
            if item is not None:
                h, c = item
                m_c = jnp.max(sT, axis=0, keepdims=True)
                if c == 0:
                    m_new, alpha = m_c, None
                else:
                    m_new = jnp.maximum(m, m_c)
                    alpha = jnp.exp(m - m_new)
                pT = jnp.exp(sT - m_new)
                l_c = jnp.sum(pT, axis=0, keepdims=True)
                l = l_c if c == 0 else l * alpha + l_c
                m = m_new
                if c == n_c - 1:
                    lse_ref[h:h + 1, :] = m + jnp.log(l)
                pending = (h, c, pT.astype(BF16), alpha, l)
            sT = sT_next

    row = lambda w: pl.BlockSpec((tq, w), lambda i: (i, 0))
    return _host_call(
        body, (qT, k, vT, pa),
        [pl.BlockSpec((BW, tq), lambda i: (0, i)), pl.BlockSpec((A_KV_HEADS, t, HEAD_DIM), lambda i: (0, 0, 0)),
         pl.BlockSpec((A_KV_HEADS * HEAD_DIM, t), lambda i: (0, 0)), row(W_A)],
        (row(BW), pl.BlockSpec((A_HEADS, tq), lambda i: (0, i)), row(BW)),
        (SDS((t, BW), F32), SDS((A_HEADS, t), F32), SDS((t, BW), BF16)),
        name="attn_fwd", grid=(t // tq,), semantics=("parallel",), exchange=exchange)


def _attn_bwd(q, qT, k, kT, vT, pa, o, lse, dy, exchange=None):
    t = q.shape[0]
    tq = _tile(t, 256, LANES)
    grp = A_HEADS // A_KV_HEADS
    gw = grp * HEAD_DIM

    def body(q_ref, qT_ref, k_ref, kT_ref, vT_ref, z_ref, o_ref, lse_ref, dy_ref, dq_ref, dkT_ref, dvT_ref):
        @pl.when(pl.program_id(1) == 0)
        def _():
            dkT_ref[...] = jnp.zeros_like(dkT_ref)
            dvT_ref[...] = jnp.zeros_like(dvT_ref)

        z = z_ref[...].astype(F32)
        do = dy_ref[...] * (z * _sigmoid(z))
        doo = do * o_ref[...]
        doT = do.T
        kk, kT, vT = k_ref[...], kT_ref[...], vT_ref[...]
        for j in range(grp):
            hs = slice(j * HEAD_DIM, (j + 1) * HEAD_DIM)
            delta = jnp.sum(doo[:, hs], axis=-1, keepdims=True)
            s = _dot(q_ref[:, hs], kT, NN)
            p = jnp.exp(s - lse_ref[:, j:j + 1])
            dp = _dot(do[:, hs].astype(BF16), vT, NN)
            ds_b = (p * (dp - delta)).astype(BF16)
            dq_ref[:, hs] = _dot(ds_b, kk, NN)
            dkT_ref[...] += _dot(qT_ref[hs, :], ds_b, NN)
            dvT_ref[...] += _dot(doT[hs, :].astype(BF16), p.astype(BF16), NN)

    grp_blk = pl.BlockSpec((tq, gw), lambda g, i: (i, g))
    kvT_blk = pl.BlockSpec((HEAD_DIM, t), lambda g, i: (g, 0))
    acc_blk = pl.BlockSpec((None, HEAD_DIM, t), lambda g, i: (g, 0, 0))
    return _host_call(
        body, (q, qT, k, kT, vT, pa, o, lse, dy),
        [grp_blk, pl.BlockSpec((gw, tq), lambda g, i: (g, i)), pl.BlockSpec((None, t, HEAD_DIM), lambda g, i: (g, 0, 0)),
         kvT_blk, kvT_blk, pl.BlockSpec((tq, gw), lambda g, i: (i, 768 // gw + g)), grp_blk,
         pl.BlockSpec((None, tq, grp), lambda g, i: (g, i, 0)), grp_blk],
        (grp_blk, acc_blk, acc_blk),
        (SDS((t, BW), F32), SDS((A_KV_HEADS, HEAD_DIM, t), F32), SDS((A_KV_HEADS, HEAD_DIM, t), F32)),
        name="attn_bwd", grid=(A_KV_HEADS, t // tq), semantics=("arbitrary", "arbitrary"), exchange=exchange)


def _attn_prep_bwd(pa, cs, sn, qg2, kg2, dq, dkT, dvT, dy, o):
    t = pa.shape[0]
    tq = _tile(t, 512, LANES)
    scale = HEAD_DIM ** -0.5

    def body(pa_ref, cs_ref, sn_ref, qg_ref, kg_ref, dq_ref, dkT_ref, dvT_ref, dy_ref, o_ref, dpa_ref, dqg_ref, dkg_ref):
        lane = lax.broadcasted_iota(jnp.int32, (tq, LANES), 1)
        c, s = cs_ref[...], sn_ref[...]

        @pl.when(pl.program_id(0) == 0)
        def _():
            dqg_ref[...] = jnp.zeros_like(dqg_ref)
            dkg_ref[...] = jnp.zeros_like(dkg_ref)

        def norm_rope_bwd(xg, g2, dout):
            r = lax.rsqrt(_head_sums(xg * xg, lane) * (1.0 / HEAD_DIM) + EPS)
            xh = xg * r
            dxn = dout * c + _swap16(dout * s, lane)
            gd = dxn * g2
            dx = r * (gd - xh * (_head_sums(gd * xh, lane) * (1.0 / HEAD_DIM)))
            return dx, jnp.sum(dxn * xh, axis=0, keepdims=True)

        for gi in range(4):
            sl = slice(gi * LANES, (gi + 1) * LANES)
            dx, dg = norm_rope_bwd(pa_ref[:, sl].astype(F32), qg_ref[...], dq_ref[:, sl] * scale)
            dpa_ref[:, sl] = dx.astype(BF16)
            dqg_ref[...] += dg
        dx, dg = norm_rope_bwd(pa_ref[:, 512:640].astype(F32), kg_ref[...], dkT_ref[...].T)
        dpa_ref[:, 512:640] = dx.astype(BF16)
        dkg_ref[...] += dg
        dpa_ref[:, 640:768] = dvT_ref[...].T.astype(BF16)
        z = pa_ref[:, 768:1280].astype(F32)
        _, dsilu = _silu_and_grad(z)
        dpa_ref[:, 768:1280] = (dy_ref[...] * o_ref[...] * dsilu).astype(BF16)

    row = lambda w: pl.BlockSpec((tq, w), lambda i: (i, 0))
    col = pl.BlockSpec((LANES, tq), lambda i: (0, i))
    vec = pl.BlockSpec((1, LANES), lambda i: (0, 0))
    return pl.pallas_call(
        body, name="attn_prep_bwd", grid=(t // tq,),
        in_specs=[row(W_A), row(LANES), row(LANES), vec, vec, row(BW), col, col, row(BW), row(BW)],
        out_specs=(row(W_A), vec, vec),
        out_shape=(SDS((t, W_A), BF16), SDS((1, LANES), F32), SDS((1, LANES), F32)),
        compiler_params=_params("arbitrary"),
    )(pa, cs, sn, qg2, kg2, dq, dkT, dvT, dy, o)


def _layer_norm(v, g, b):
    mu = jnp.mean(v, axis=-1, keepdims=True)
    xc = v - mu
    rs = lax.rsqrt(jnp.mean(xc * xc, axis=-1, keepdims=True) + EPS)
    xh = xc * rs
    return xh * g + b, xh, rs


def _gmlp_fwd(pb, lng, lnb, ws, bs):
    t = pb.shape[0]
    tb = _tile(t, 512, CHUNK)

    def body(pb_ref, g_ref, b_ref, ws_ref, bs_ref, y_ref):
        vln, _, _ = _layer_norm(pb_ref[:, BW:2 * BW].astype(F32), g_ref[...], b_ref[...])
        vb = vln.astype(BF16)
        for gi in range(B_GROUPS):
            w = ws_ref[gi].astype(BF16)
            cs_ = slice(gi * CHUNK, (gi + 1) * CHUNK)
            for n in range(tb // CHUNK):
                rs_ = slice(n * CHUNK, (n + 1) * CHUNK)
                mixed = _dot(w, vb[rs_, cs_], NN) + bs_ref[gi]
                z = pb_ref[rs_, 2 * BW + gi * CHUNK:2 * BW + (gi + 1) * CHUNK].astype(F32)
                y_ref[rs_, cs_] = (pb_ref[rs_, cs_].astype(F32) * mixed * (z * _sigmoid(z))).astype(BF16)

    return pl.pallas_call(
        body, name="gmlp_fwd", grid=(t // tb,),
        in_specs=[pl.BlockSpec((tb, W_B), lambda i: (i, 0)),
                  pl.BlockSpec((1, BW), lambda i: (0, 0)), pl.BlockSpec((1, BW), lambda i: (0, 0)),
                  pl.BlockSpec((B_GROUPS, CHUNK, CHUNK), lambda i: (0, 0, 0)),
                  pl.BlockSpec((B_GROUPS, CHUNK, 1), lambda i: (0, 0, 0))],
        out_specs=pl.BlockSpec((tb, BW), lambda i: (i, 0)),
        out_shape=SDS((t, BW), BF16), compiler_params=_params("parallel"),
    )(pb, lng, lnb, ws, bs)


def _gmlp_bwd(pb, lng, lnb, ws, bs, dy):
    t = pb.shape[0]
    tb = _tile(t, 256, CHUNK)

    def body(pb_ref, g_ref, b_ref, ws_ref, bs_ref, dy_ref, dpb_ref, dws_ref, dbs_ref, dg_ref, db_ref, dvln_ref):
        @pl.when(pl.program_id(0) == 0)
        def _():
            dws_ref[...] = jnp.zeros_like(dws_ref)
            dbs_ref[...] = jnp.zeros_like(dbs_ref)
            dg_ref[...] = jnp.zeros_like(dg_ref)
            db_ref[...] = jnp.zeros_like(db_ref)

        vln, xh, rs = _layer_norm(pb_ref[:, BW:2 * BW].astype(F32), g_ref[...], b_ref[...])
        vb = vln.astype(BF16)
        for gi in range(B_GROUPS):
            w = ws_ref[gi].astype(BF16)
            cs_ = slice(gi * CHUNK, (gi + 1) * CHUNK)
            for n in range(tb // CHUNK):
                rs_ = slice(n * CHUNK, (n + 1) * CHUNK)
                vbc = vb[rs_, cs_]
                mixed = _dot(w, vbc, NN) + bs_ref[gi]
                zs = slice(2 * BW + gi * CHUNK, 2 * BW + (gi + 1) * CHUNK)
                z = pb_ref[rs_, zs].astype(F32)
                u = pb_ref[rs_, cs_].astype(F32)
                sil, dsil = _silu_and_grad(z)
                dyc = dy_ref[rs_, cs_]
                dmixed = dyc * u * sil
                dpb_ref[rs_, cs_] = (dyc * mixed * sil).astype(BF16)
                dpb_ref[rs_, zs] = (dyc * u * mixed * dsil).astype(BF16)
                dmb = dmixed.astype(BF16)
                dws_ref[gi] += _dot(dmb, vbc, NT)
                dbs_ref[gi] += jnp.sum(dmixed, axis=-1, keepdims=True)
                dvln_ref[rs_, cs_] = _dot(w, dmb, TN)
        dvln = dvln_ref[...]
        dg_ref[...] += jnp.sum(dvln * xh, axis=0, keepdims=True)
        db_ref[...] += jnp.sum(dvln, axis=0, keepdims=True)
        gd = dvln * g_ref[...]
        dv = rs * (gd - jnp.mean(gd, axis=-1, keepdims=True) - xh * jnp.mean(gd * xh, axis=-1, keepdims=True))
        dpb_ref[:, BW:2 * BW] = dv.astype(BF16)

    vec = pl.BlockSpec((1, BW), lambda i: (0, 0))
    wsb = pl.BlockSpec((B_GROUPS, CHUNK, CHUNK), lambda i: (0, 0, 0))
    bsb = pl.BlockSpec((B_GROUPS, CHUNK, 1), lambda i: (0, 0, 0))
    return pl.pallas_call(
        body, name="gmlp_bwd", grid=(t // tb,),
        in_specs=[pl.BlockSpec((tb, W_B), lambda i: (i, 0)), vec, vec, wsb, bsb, pl.BlockSpec((tb, BW), lambda i: (i, 0))],
        out_specs=(pl.BlockSpec((tb, W_B), lambda i: (i, 0)), wsb, bsb, vec, vec),
        out_shape=(SDS((t, W_B), BF16), SDS((B_GROUPS, CHUNK, CHUNK), F32), SDS((B_GROUPS, CHUNK, 1), F32),
                   SDS((1, BW), F32), SDS((1, BW), F32)),
        scratch_shapes=[pltpu.VMEM((tb, BW), F32)],
        compiler_params=_params("arbitrary"),
    )(pb, lng, lnb, ws, bs, dy)


def _mem_softmax(qh, kh):
    s = _dot(qh, kh, NT) * (M_HEAD_DIM ** -0.5)
    e = jnp.exp(s - jnp.max(s, axis=-1, keepdims=True))
    return e / jnp.sum(e, axis=-1, keepdims=True)


def _mem_attn_fwd(pm, kv):
    t = pm.shape[0]
    tq = _tile(t, 512, 8)
    ml = kv.shape[0]

    def body(pm_ref, kv_ref, y_ref):
        for h in range(M_HEADS):
            hs = slice(h * M_HEAD_DIM, (h + 1) * M_HEAD_DIM)
            kh = kv_ref[:, hs].astype(BF16)
            vh = kv_ref[:, BW + h * M_HEAD_DIM:BW + (h + 1) * M_HEAD_DIM].astype(BF16)
            p = _mem_softmax(pm_ref[:, hs].astype(BF16), kh)
            o = _dot(p.astype(BF16), vh, NN)
            z = pm_ref[:, BW + h * M_HEAD_DIM:BW + (h + 1) * M_HEAD_DIM].astype(F32)
            y_ref[:, hs] = (o * (z * _sigmoid(z))).astype(BF16)

    return pl.pallas_call(
        body, name="mem_attn_fwd", grid=(t // tq,),
        in_specs=[pl.BlockSpec((tq, W_M), lambda i: (i, 0)), pl.BlockSpec((ml, 2 * BW), lambda i: (0, 0))],
        out_specs=pl.BlockSpec((tq, BW), lambda i: (i, 0)),
        out_shape=SDS((t, BW), BF16), compiler_params=_params("parallel"),
    )(pm, kv)


def _mem_attn_bwd(pm, kv, dy):
    t = pm.shape[0]
    tq = _tile(t, 512, 8)
    ml = kv.shape[0]
    scale = M_HEAD_DIM ** -0.5

    def body(pm_ref, kv_ref, dy_ref, dpm_ref, dkv_ref):
        @pl.when(pl.program_id(0) == 0)
        def _():
            dkv_ref[...] = jnp.zeros_like(dkv_ref)

        for h in range(M_HEADS):
            hs = slice(h * M_HEAD_DIM, (h + 1) * M_HEAD_DIM)
            zs = slice(BW + h * M_HEAD_DIM, BW + (h + 1) * M_HEAD_DIM)
            kh = kv_ref[:, hs].astype(BF16)
            vh = kv_ref[:, zs].astype(BF16)
            qh = pm_ref[:, hs].astype(BF16)
            p = _mem_softmax(qh, kh)
            pb = p.astype(BF16)
            o = _dot(pb, vh, NN)
            sil, dsil = _silu_and_grad(pm_ref[:, zs].astype(F32))
            dyh = dy_ref[:, hs]
            do = dyh * sil
            dpm_ref[:, zs] = (dyh * o * dsil).astype(BF16)
            delta = jnp.sum(do * o, axis=-1, keepdims=True)
            do_b = do.astype(BF16)
            dp = _dot(do_b, vh, NT)
            dr_b = (p * (dp - delta) * scale).astype(BF16)
            dpm_ref[:, hs] = _dot(dr_b, kh, NN).astype(BF16)
            dkv_ref[:, hs] += _dot(dr_b, qh, TN)
            dkv_ref[:, zs] += _dot(pb, do_b, TN)

    kvb = pl.BlockSpec((ml, 2 * BW), lambda i: (0, 0))
    return pl.pallas_call(
        body, name="mem_attn_bwd", grid=(t // tq,),
        in_specs=[pl.BlockSpec((tq, W_M), lambda i: (i, 0)), kvb, pl.BlockSpec((tq, BW), lambda i: (i, 0))],
        out_specs=(pl.BlockSpec((tq, W_M), lambda i: (i, 0)), kvb),
        out_shape=(SDS((t, W_M), BF16), SDS((ml, 2 * BW), F32)),
        compiler_params=_params("arbitrary"),
    )(pm, kv, dy)


def _merge_fwd(ya, yb, ym, pg, wbr, exchange=None):
    t = ya.shape[0]
    tm = _tile(t, 512, 8)

    def body(ya_ref, yb_ref, ym_ref, pg_ref, w_ref, m_ref):
        acc = None
        for n, y_ref in enumerate((ya_ref, yb_ref, ym_ref)):
            up = _dot(y_ref[...], w_ref[n], NN)
            term = _sigmoid(pg_ref[:, n * D_MODEL:(n + 1) * D_MODEL].astype(F32)) * up
            acc = term if acc is None else acc + term
        m_ref[...] = acc.astype(BF16)

    yb_spec = pl.BlockSpec((tm, BW), lambda i: (i, 0))
    return _host_call(
        body, (ya, yb, ym, pg, wbr),
        [yb_spec, yb_spec, yb_spec, pl.BlockSpec((tm, W_G), lambda i: (i, 0)),
         pl.BlockSpec((3, BW, D_MODEL), lambda i: (0, 0, 0))],
        (pl.BlockSpec((tm, D_MODEL), lambda i: (i, 0)),), (SDS((t, D_MODEL), BF16),),
        name="merge_fwd", grid=(t // tm,), semantics=("parallel",), exchange=exchange)


def _merge_bwd(ya, yb, ym, pg, wbr, dm, exchange=None):
    t = ya.shape[0]
    tm = _tile(t, 256, 8)

    def body(ya_ref, yb_ref, ym_ref, pg_ref, w_ref, dm_ref, dya_ref, dyb_ref, dym_ref, dpg_ref, dw_ref):
        @pl.when(pl.program_id(0) == 0)
        def _():
            dw_ref[...] = jnp.zeros_like(dw_ref)

        dmf = dm_ref[...]
        for n, (y_ref, dy_ref) in enumerate(((ya_ref, dya_ref), (yb_ref, dyb_ref), (ym_ref, dym_ref))):
            cs_ = slice(n * D_MODEL, (n + 1) * D_MODEL)
            y = y_ref[...]
            w = w_ref[n]
            up = _dot(y, w, NN)
            gt = _sigmoid(pg_ref[:, cs_].astype(F32))
            dpg_ref[:, cs_] = (dmf * up * gt * (1.0 - gt)).astype(BF16)
            dup = (dmf * gt).astype(BF16)
            dy_ref[...] = _dot(dup, w, NT)
            dw_ref[n] += _dot(y, dup, TN)

    y_spec = pl.BlockSpec((tm, BW), lambda i: (i, 0))
    w_spec = pl.BlockSpec((3, BW, D_MODEL), lambda i: (0, 0, 0))
    return _host_call(
        body, (ya, yb, ym, pg, wbr, dm),
        [y_spec, y_spec, y_spec, pl.BlockSpec((tm, W_G), lambda i: (i, 0)), w_spec,
         pl.BlockSpec((tm, D_MODEL), lambda i: (i, 0))],
        (y_spec, y_spec, y_spec, pl.BlockSpec((tm, W_G), lambda i: (i, 0)), w_spec),
        (SDS((t, BW), F32), SDS((t, BW), F32), SDS((t, BW), F32), SDS((t, W_G), BF16), SDS((3, BW, D_MODEL), F32)),
        name="merge_bwd", grid=(t // tm,), semantics=("arbitrary",), exchange=exchange)


def _loss_head(x, g, target):
    t, d = x.shape
    tm = _tile(t, 512, 8)

    def body(x_ref, g_ref, t_ref, loss_ref, dx_ref, dg_ref):
        @pl.when(pl.program_id(0) == 0)
        def _():
            loss_ref[...] = jnp.zeros_like(loss_ref)
            dg_ref[...] = jnp.zeros_like(dg_ref)

        xf = x_ref[...]
        r = lax.rsqrt(jnp.mean(xf * xf, axis=-1, keepdims=True) + EPS)
        xh = xf * r
        err = xh * g_ref[...] - t_ref[...]
        per_tok = jnp.mean(err * err, axis=-1, keepdims=True)
        loss_ref[...] += 0.5 * jnp.sum(per_tok, axis=0, keepdims=True)
        dy = err * (1.0 / d)
        gd = dy * g_ref[...]
        dx_ref[...] = r * (gd - xh * jnp.mean(gd * xh, axis=-1, keepdims=True))
        dg_ref[...] += jnp.sum(dy * xh, axis=0, keepdims=True)

    row = pl.BlockSpec((tm, d), lambda i: (i, 0))
    vec = pl.BlockSpec((1, d), lambda i: (0, 0))
    return pl.pallas_call(
        body, name="loss_head", grid=(t // tm,),
        in_specs=[row, vec, row],
        out_specs=(pl.BlockSpec((1, 1), lambda i: (0, 0)), row, vec),
        out_shape=(SDS((1, 1), F32), SDS((t, d), F32), SDS((1, d), F32)),
        compiler_params=_params("arbitrary"),
    )(x, g, target)


def _layer_fwd(x, mem, w, tabs, next_shards=None):
    cs, sn = tabs
    riding = next_shards is not None
    h, pa, pb, pm, pg = _norm_in_proj(x, w["norm_g"], w["w_in_t"])
    q, qT, k, kT, vT = _attn_prep_fwd(pa, cs, sn, w["qg2"], w["kg2"])
    (o, lse, ya), gathered = _attn_fwd(qT, k, vT, pa, exchange=_gather_first_hop(next_shards) if riding else None)
    yb = _gmlp_fwd(pb, w["ln_g"], w["ln_b"], w["w_s"], w["b_s"])
    memn = _rmsnorm_fwd(mem, w["mem_g"], "mem_rmsnorm_fwd")
    kv = _mm(memn, w["w_kv"], "nn", "mem_kv")
    ym = _mem_attn_fwd(pm, kv)
    (merged,), gathered = _merge_fwd(ya, yb, ym, pg, w["w_br"], exchange=_gather_forward(gathered) if riding else None)
    x_next = _mm(merged, w["w_out"], "nn", "out_proj", add=x)
    saved = dict(x=x, h=h, pa=pa, pb=pb, pm=pm, pg=pg, q=q, qT=qT, k=k, kT=kT, vT=vT, o=o, lse=lse, ya=ya, yb=yb, ym=ym,
                 memn=memn, kv=kv, merged=merged)
    return x_next, saved, gathered


def _layer_bwd(dx_out, mem, w, s, tabs, pending=None, core=None, scatter_own=False):
    cs, sn = tabs
    t = dx_out.shape[0]
    riding = pending is not None
    dmerged = _mm(dx_out, w["w_out"], "nt", "d_merged")
    shard_rows = D_MODEL // N_DEV
    d_w_out = _mm_tn_by_owner(s["merged"], dx_out, "d_w_out")
    (dya, dyb, dym, dpg, d_w_br), recv = _merge_bwd(s["ya"], s["yb"], s["ym"], s["pg"], w["w_br"], dmerged,
                                                    exchange=_scatter_to_sibling(pending) if riding else None)
    parts = _pair_sums(pending, recv, core) if riding else None
    grp = A_HEADS // A_KV_HEADS
    lse_cols = s["lse"].reshape(A_KV_HEADS, grp, t).transpose(0, 2, 1)
    (dq, dkT, dvT), from_chips = _attn_bwd(s["q"], s["qT"], s["k"], s["kT"], s["vT"], s["pa"], s["o"], lse_cols, dya,
                                           exchange=_scatter_to_chips(parts) if riding else None)
    dpa, d_qg2, d_kg2 = _attn_prep_bwd(s["pa"], cs, sn, w["qg2"], w["kg2"], dq, dkT.reshape(-1, t), dvT.reshape(-1, t),
                                       dya, s["o"])
    dpb, d_w_s, d_b_s, d_ln_g, d_ln_b = _gmlp_bwd(s["pb"], w["ln_g"], w["ln_b"], w["w_s"], w["b_s"], dyb)
    dpm, dkv = _mem_attn_bwd(s["pm"], s["kv"], dym)
    d_w_kv = _mm_tn_by_owner(s["memn"], dkv, "d_w_kv")
    dmemn = _mm(dkv, w["w_kv"], "nt", "d_memn")
    d_mem_g = _rmsnorm_dg(mem, w["mem_g"], dmemn, "mem_rmsnorm_bwd")
    dps = (dpa, dpb, dpm, dpg)
    d_w_in = _split_w_in_t(*[_mm(dp, s["h"], "tn", "d_w_" + nm, out_dtype=BF16, tm_cap=1024, tn_cap=D_MODEL, tk_cap=4096)
                             for nm, dp in zip("abmg", dps)])
    grads = dict(w_in=d_w_in[:, None], w_mem_kv=d_w_kv.reshape(N_DEV, 1, shard_rows, -1), w_br=_split_w_br(d_w_br),
                 w_out=d_w_out.reshape(N_DEV, 1, shard_rows, -1))
    own = None
    if scatter_own:
        dests = [grads[n] for n in BIG]
        own_parts = _pair_sums(dests, _exchange_call(_scatter_to_sibling(dests), "rs_sibling_swap"), core)
        own = _scatter_to_chips(own_parts)
    (dx_in, d_norm_g), own_from_chips = _dh_rmsnorm_bwd(dps, w["w_in_t"], s["x"], w["norm_g"], dx_out, exchange=own)
    grads.update(norm_g=d_norm_g[0], q_norm_g=d_qg2[0, :HEAD_DIM] + d_qg2[0, HEAD_DIM:],
                 k_norm_g=d_kg2[0, :HEAD_DIM] + d_kg2[0, HEAD_DIM:], sg_ln_g=d_ln_g[0], sg_ln_b=d_ln_b[0],
                 w_s=d_w_s, b_s=d_b_s[:, :, 0], mem_norm_g=d_mem_g[0])
    return (dx_in, grads, ((parts, from_chips) if riding else None),
            ((own_parts, own_from_chips) if scatter_own else None))


def _layer_weights(l, w_in_t, w_kv, w_br, w_out, small):
    tile2 = lambda g: jnp.tile(g.reshape(1, -1), (1, 2))
    return dict(
        w_in_t=w_in_t, w_kv=w_kv, w_br=w_br, w_out=w_out,
        norm_g=small["norm_g"][l].reshape(1, -1), qg2=tile2(small["q_norm_g"][l]), kg2=tile2(small["k_norm_g"][l]),
        ln_g=small["sg_ln_g"][l].reshape(1, -1), ln_b=small["sg_ln_b"][l].reshape(1, -1),
        w_s=small["w_s"][l], b_s=small["b_s"][l][:, :, None], mem_g=small["mem_norm_g"][l].reshape(1, -1))


def _position():
    x, y, c = lax.axis_index("x"), lax.axis_index("y"), lax.axis_index("c")
    return x, y, c, [(1 - x, y), (x, 1 - y), (1 - x, 1 - y)]


def _gather_first_hop(shards):
    n = len(shards)

    def make(x_refs, out_refs, sems):
        send_sems, recv_sems, local_sems = sems
        x, y, c, chips = _position()
        me = 4 * x + 2 * y + c
        peers = [(x, y, 1 - c)] + [(cx, cy, c) for cx, cy in chips]
        copies = [pltpu.make_async_copy(x_refs[t], out_refs[t].at[me], local_sems.at[t]) for t in range(n)]
        copies += [pltpu.make_async_remote_copy(
            src_ref=x_refs[t], dst_ref=out_refs[t].at[me], send_sem=send_sems.at[t, k], recv_sem=recv_sems.at[t, k],
            device_id=peer, device_id_type=MESH_ID) for t in range(n) for k, peer in enumerate(peers)]
        return copies

    return _Exchange(shards, [SDS((N_DEV,) + a.shape, a.dtype) for a in shards],
                     [pltpu.SemaphoreType.DMA((n, 4)), pltpu.SemaphoreType.DMA((n, 4)), pltpu.SemaphoreType.DMA((n,))], make)


def _gather_forward(gathered):
    n = len(gathered)

    def make(in_refs, out_refs, sems):
        send_sems, recv_sems = sems
        x, y, c, chips = _position()
        return [pltpu.make_async_remote_copy(
            src_ref=in_refs[t].at[4 * cx + 2 * cy + c], dst_ref=out_refs[t].at[4 * cx + 2 * cy + c],
            send_sem=send_sems.at[t, j], recv_sem=recv_sems.at[t, j], device_id=(x, y, 1 - c), device_id_type=MESH_ID)
            for t in range(n) for j, (cx, cy) in enumerate(chips)]

    return _Exchange(gathered, [SDS(a.shape, a.dtype) for a in gathered],
                     [pltpu.SemaphoreType.DMA((n, 3)), pltpu.SemaphoreType.DMA((n, 3))], make,
                     aliases={t: t for t in range(n)})


def _all_gather(shards, name):
    return _exchange_call(_gather_forward(_exchange_call(_gather_first_hop(shards), name + "_hop1")), name + "_hop2")


def _shard_segments(d):
    shard = IN_WIDTH // N_DEV
    lo, hi = d * shard, (d + 1) * shard
    out = []
    for g in range(4):
        a, b = max(lo, GROUP_OFFS[g]), min(hi, GROUP_OFFS[g + 1])
        if a < b:
            out.append((g, a - GROUP_OFFS[g], b - GROUP_OFFS[g], a - lo))
    return out


def _split_w_in_t(da, db, dm, dg):
    cols = da.shape[1]
    shard = IN_WIDTH // N_DEV
    tc = _tile(cols, 256)

    def body(a_ref, b_ref, m_ref, g_ref, o_ref):
        ins = (a_ref, b_ref, m_ref, g_ref)
        for d in range(N_DEV):
            for g, lo, hi, off in _shard_segments(d):
                o_ref[_core_major(d), off:off + hi - lo, :] = ins[g][lo:hi, :]

    return pl.pallas_call(
        body, name="split_w_in", grid=(cols // tc,),
        in_specs=[pl.BlockSpec((a.shape[0], tc), lambda j: (0, j)) for a in (da, db, dm, dg)],
        out_specs=pl.BlockSpec((N_DEV, shard, tc), lambda j: (0, 0, j)),
        out_shape=SDS((N_DEV, shard, cols), da.dtype), compiler_params=_params("parallel"),
    )(da, db, dm, dg)


def _assemble_w_br(gathered):
    _, nb, rows, shard = gathered.shape

    def body(g_ref, o_ref):
        for d in range(N_DEV):
            o_ref[:, :, d * shard:(d + 1) * shard] = g_ref[d]

    return pl.pallas_call(
        body, name="assemble_w_br", out_shape=SDS((nb, rows, N_DEV * shard), gathered.dtype),
        compiler_params=pltpu.CompilerParams(vmem_limit_bytes=VMEM_LIMIT),
    )(gathered)


def _split_w_br(dw):
    nb, rows, cols = dw.shape
    shard = cols // N_DEV

    def body(d_ref, o_ref):
        for d in range(N_DEV):
            o_ref[_core_major(d)] = d_ref[:, :, d * shard:(d + 1) * shard].astype(BF16)

    return pl.pallas_call(
        body, name="split_w_br", out_shape=SDS((N_DEV, nb, rows, shard), BF16),
        compiler_params=pltpu.CompilerParams(vmem_limit_bytes=VMEM_LIMIT),
    )(dw)


def _scatter_to_sibling(dests):
    n = len(dests)

    def make(d_refs, recv_refs, sems):
        send_sems, recv_sems = sems
        x, y, c, _ = _position()
        return [pltpu.make_async_remote_copy(
            src_ref=d_refs[t].at[pl.ds((1 - c) * 4, 4)], dst_ref=recv_refs[t], send_sem=send_sems.at[t],
            recv_sem=recv_sems.at[t], device_id=(x, y, 1 - c), device_id_type=MESH_ID) for t in range(n)]

    return _Exchange(dests, [SDS((4,) + a.shape[1:], a.dtype) for a in dests],
                     [pltpu.SemaphoreType.DMA((n,)), pltpu.SemaphoreType.DMA((n,))], make)


def _scatter_to_chips(parts):
    n = len(parts)

    def make(p_refs, recv_refs, sems):
        send_sems, recv_sems = sems
        _, _, c, chips = _position()
        return [pltpu.make_async_remote_copy(
            src_ref=p_refs[t].at[2 * cx + cy], dst_ref=recv_refs[t].at[k], send_sem=send_sems.at[t, k],
            recv_sem=recv_sems.at[t, k], device_id=(cx, cy, c), device_id_type=MESH_ID)
            for t in range(n) for k, (cx, cy) in enumerate(chips)]

    return _Exchange(parts, [SDS((3,) + a.shape[1:], a.dtype) for a in parts],
                     [pltpu.SemaphoreType.DMA((n, 3)), pltpu.SemaphoreType.DMA((n, 3))], make)


def _pair_sums(dests, recv, core):
    return [_pair_sum(d, r, core, "rs_pair_sum_" + n) for n, d, r in zip(BIG, dests, recv)]


def _pair_sum(dest, recv, core, name):
    _, na, r, cdim = dest.shape

    def body(core_ref, a_ref, b_ref, o_ref):
        o_ref[...] = (a_ref[...].astype(F32) + b_ref[...].astype(F32)).astype(o_ref.dtype)

    blk = pl.BlockSpec((None, na, r, cdim), lambda j, core_ref: (j, 0, 0, 0))
    return pl.pallas_call(
        body, name=name, out_shape=SDS(recv.shape, BF16),
        grid_spec=pltpu.PrefetchScalarGridSpec(
            num_scalar_prefetch=1, grid=(4,),
            in_specs=[pl.BlockSpec((None, na, r, cdim), lambda j, core_ref: (core_ref[0] * 4 + j, 0, 0, 0)), blk],
            out_specs=blk),
        compiler_params=_params("parallel"),
    )(core, dest, recv)


def _adamw_math(w, g, m, v):
    m = ADAM_B1 * m + (1.0 - ADAM_B1) * g
    v = ADAM_B2 * v + (1.0 - ADAM_B2) * (g * g)
    m_hat = m / (1.0 - ADAM_B1 ** ADAM_STEP)
    v_hat = v / (1.0 - ADAM_B2 ** ADAM_STEP)
    delta = -ADAM_LR * (m_hat / (jnp.sqrt(v_hat) + ADAM_EPS) + ADAM_WD * w)
    return delta, m, v


def _sum_and_adamw(parts, w, m, v, name):
    n, r, ln = parts.shape
    tr = _tile(r, 512, 16)

    def body(p_ref, w_ref, m_ref, v_ref, g_out, d_out, m_out, v_out):
        g = p_ref[0].astype(F32)
        for j in range(1, n):
            g = g + p_ref[j].astype(F32)
        delta, nm, nv = _adamw_math(w_ref[...], g, m_ref[...], v_ref[...])
        g_out[...] = g
        d_out[...] = delta
        m_out[...] = nm
        v_out[...] = nv

    blk = pl.BlockSpec((tr, ln), lambda i: (i, 0))
    return pl.pallas_call(
        body, name=name, grid=(r // tr,),
        in_specs=[pl.BlockSpec((n, tr, ln), lambda i: (0, i, 0)), blk, blk, blk],
        out_specs=(blk, blk, blk, blk), out_shape=tuple(SDS((r, ln), F32) for _ in range(4)),
        compiler_params=_params("parallel"),
    )(parts, w, m, v)


BIG = ("w_in", "w_mem_kv", "w_br", "w_out")
SMALL = ("norm_g", "q_norm_g", "k_norm_g", "sg_ln_g", "sg_ln_b", "w_s", "b_s", "mem_norm_g", "final_g")


def _pack(arrs, row_unit=16):
    flat = jnp.concatenate([a.reshape(-1) for a in arrs])
    pad = (-flat.shape[0]) % (row_unit * LANES)
    if pad:
        flat = jnp.concatenate([flat, jnp.zeros((pad,), flat.dtype)])
    return flat.reshape(-1, LANES)


def _unpack(buf, shapes):
    flat = buf.reshape(-1)
    out, off = [], 0
    for shp in shapes:
        n = 1
        for s_ in shp:
            n *= s_
        out.append(flat[off:off + n].reshape(shp))
        off += n
    return out


def _shard_sum_adamw(part, from_chips, chip, w, m, v, layer, prev, name):
    _, na, r, cdim = part.shape
    flat = w.ndim == 3
    assert not flat or na == 1
    tr = _tile(r, max(8, (256 * 1024) // (na * cdim)), 8)
    n_prev = 0 if prev is None else len(prev)

    def body(chip_ref, p_ref, f_ref, w_ref, m_ref, v_ref, *rest):
        g_out, d_out, m_out, v_out = rest[n_prev:]
        g = p_ref[...].astype(F32)
        for j in range(3):
            g = g + f_ref[j].astype(F32)
        delta, nm, nv = _adamw_math(w_ref[...], g, m_ref[...], v_ref[...])
        g_out[...] = g
        d_out[...] = delta
        m_out[...] = nm
        v_out[...] = nv

    a_blk = None if flat else na
    if flat:
        lay = pl.BlockSpec((None, tr, cdim), lambda i, chip_ref: (layer, i, 0))
    else:
        lay = pl.BlockSpec((None, na, tr, cdim), lambda i, chip_ref: (layer, 0, i, 0))
    return pl.pallas_call(
        body, name=name, out_shape=tuple(SDS(w.shape, F32) for _ in range(4)),
        grid_spec=pltpu.PrefetchScalarGridSpec(
            num_scalar_prefetch=1, grid=(r // tr,),
            in_specs=[pl.BlockSpec((None, a_blk, tr, cdim), lambda i, chip_ref: (chip_ref[0], 0, i, 0)),
                      pl.BlockSpec((3, a_blk, tr, cdim), lambda i, chip_ref: (0, 0, i, 0)), lay, lay, lay]
            + [HBM] * n_prev,
            out_specs=(lay, lay, lay, lay)),
        input_output_aliases={6 + j: j for j in range(n_prev)},
        compiler_params=_params("parallel"),
    )(chip, part, from_chips, w, m, v, *(prev or ()))


def kernel(x, mem, norm_g, w_in, q_norm_g, k_norm_g, sg_ln_g, sg_ln_b, w_s, b_s, mem_norm_g, w_mem_kv, w_br, w_out, final_g, loss_target, m_norm_g, m_w_in, m_q_norm_g, m_k_norm_g, m_sg_ln_g, m_sg_ln_b, m_w_s, m_b_s, m_mem_norm_g, m_w_mem_kv, m_w_br, m_w_out, m_final_g, v_norm_g, v_w_in, v_q_norm_g, v_k_norm_g, v_sg_ln_g, v_sg_ln_b, v_w_s, v_b_s, v_mem_norm_g, v_w_mem_kv, v_w_br, v_w_out, v_final_g):
    wts = dict(norm_g=norm_g, w_in=w_in, q_norm_g=q_norm_g, k_norm_g=k_norm_g, sg_ln_g=sg_ln_g, sg_ln_b=sg_ln_b,
               w_s=w_s, b_s=b_s, mem_norm_g=mem_norm_g, w_mem_kv=w_mem_kv, w_br=w_br, w_out=w_out, final_g=final_g)
    mom1 = dict(norm_g=m_norm_g, w_in=m_w_in, q_norm_g=m_q_norm_g, k_norm_g=m_k_norm_g, sg_ln_g=m_sg_ln_g,
                sg_ln_b=m_sg_ln_b, w_s=m_w_s, b_s=m_b_s, mem_norm_g=m_mem_norm_g, w_mem_kv=m_w_mem_kv, w_br=m_w_br,
                w_out=m_w_out, final_g=m_final_g)
    mom2 = dict(norm_g=v_norm_g, w_in=v_w_in, q_norm_g=v_q_norm_g, k_norm_g=v_k_norm_g, sg_ln_g=v_sg_ln_g,
                sg_ln_b=v_sg_ln_b, w_s=v_w_s, b_s=v_b_s, mem_norm_g=v_mem_norm_g, w_mem_kv=v_w_mem_kv, w_br=v_w_br,
                w_out=v_w_out, final_g=v_final_g)
    dp = w_in.shape[0]
    core = lax.axis_index("c").astype(jnp.int32).reshape(1)
    chip = (2 * lax.axis_index("x") + lax.axis_index("y")).astype(jnp.int32).reshape(1)

    t_in = lambda a: jnp.swapaxes(a, 1, 2)
    wts, mom1, mom2 = [dict(d_, w_in=t_in(d_["w_in"])) for d_ in (wts, mom1, mom2)]

    shard_bf = {n: wts[n].astype(BF16) for n in BIG}
    shards = lambda l: [shard_bf[n][l] for n in BIG]
    x_l, mem_l = x[0], mem[0]
    tabs = _rope_tables(x_l.shape[0])

    gathered = _all_gather(shards(0), "weights_all_gather")
    layers, saved = [], []
    for l in range(dp):
        g_in, g_kv, g_br, g_out = gathered
        layers.append(_layer_weights(l, g_in.reshape(IN_WIDTH, -1), g_kv.reshape(D_MODEL, -1), _assemble_w_br(g_br),
                                     g_out.reshape(D_MODEL, -1), wts))
        x_l, s, gathered = _layer_fwd(x_l, mem_l, layers[l], tabs, next_shards=shards(l + 1) if l + 1 < dp else None)
        saved.append(s)
    loss_local, dx, d_final_g = _loss_head(x_l, final_g.reshape(1, -1), loss_target[0])
    loss = lax.psum(loss_local[0, 0], AXES)

    def finish(l, parts, from_chips, prev):
        return {n: _shard_sum_adamw(p, f, chip, wts[n], mom1[n], mom2[n], l, None if prev is None else prev[n],
                                    "sum_adamw_" + n)
                for n, p, f in zip(BIG, parts, from_chips)}

    grads, updated, pending = [None] * dp, None, None
    for l in reversed(range(dp)):
        dx, grads[l], scattered, own = _layer_bwd(dx, mem_l, layers[l], saved[l], tabs, pending=pending, core=core,
                                                  scatter_own=(l == 0))
        if scattered is not None:
            updated = finish(l + 1, *scattered, updated)
        pending = [grads[l][n] for n in BIG]
    updated = finish(0, *own, updated)
    grad_x = dx
    big_out = [{n: (t_in(updated[n][k]) if n == "w_in" else updated[n][k]) for n in BIG} for k in range(4)]

    small_g = {n: jnp.stack([g[n] for g in grads]) for n in SMALL if n != "final_g"}
    small_g["final_g"] = d_final_g
    (all_small,) = _all_gather([_pack([small_g[n] for n in SMALL])], "small_all_gather")
    small_bufs = _sum_and_adamw(
        all_small, _pack([wts[n] for n in SMALL]), _pack([mom1[n] for n in SMALL]), _pack([mom2[n] for n in SMALL]),
        "small_sum_adamw")

    outs = []
    for big_vals, small_buf in zip(big_out, small_bufs):
        vals = dict(big_vals)
        vals.update(zip(SMALL, _unpack(small_buf, [wts[n].shape for n in SMALL])))
        outs.append(vals)
    order = ("norm_g", "w_in", "q_norm_g", "k_norm_g", "sg_ln_g", "sg_ln_b", "w_s", "b_s", "mem_norm_g", "w_mem_kv",
             "w_br", "w_out", "final_g")
    result = [loss, grad_x[None]]
    for vals in outs:
        result += [vals[n] for n in order]
    return tuple(result)
```

```python
import functools

import jax
import jax.numpy as jnp
from jax import lax
from jax.experimental import pallas as pl
from jax.experimental.pallas import tpu as pltpu

F32 = jnp.float32
BF16 = jnp.bfloat16
SDS = jax.ShapeDtypeStruct
MESH_ID = pl.DeviceIdType.MESH
AXES = ("x", "y", "c")
N_DEV = 8

D_MODEL = 1024
DEPTH = 4
GRID_W = 64
CHUNK = 128
ROPE_THETA = 10000.0
EPS = 1e-6
HEAD_DIM = 64
A_HEADS = 8
A_KV_HEADS = 2
B_GROUPS = 4
M_HEADS = 4
M_HEAD_DIM = 128
BW = 512
W_A = 1280
W_B = 1536
W_M = 1024
W_G = 3072
IN_WIDTH = W_A + W_B + W_M + W_G
GROUP_OFFS = (0, W_A, W_A + W_B, W_A + W_B + W_M, IN_WIDTH)

ADAM_LR = 0.001
ADAM_B1 = 0.9
ADAM_B2 = 0.999
ADAM_EPS = 1e-08
ADAM_WD = 0.01
ADAM_STEP = 10

LANES = 128
KEY_CHUNK = 2048
VMEM_LIMIT = 52 * 1024 * 1024


def _tile(n, cap, unit=LANES):
    if n <= cap:
        return n
    t = (cap // unit) * unit
    while n % t:
        t -= unit
    return t


def _params(*sem):
    return pltpu.CompilerParams(dimension_semantics=sem, vmem_limit_bytes=VMEM_LIMIT)


def _sigmoid(z):
    return 0.5 * jnp.tanh(0.5 * z) + 0.5


def _silu_and_grad(z):
    s = _sigmoid(z)
    return z * s, s * (1.0 + z * (1.0 - s))


def _dot(a, b, dims):
    return lax.dot_general(a, b, (dims, ((), ())), preferred_element_type=F32)


NN = ((1,), (0,))
NT = ((1,), (1,))
TN = ((0,), (0,))
HBM = pl.BlockSpec(memory_space=pl.ANY)


class _Exchange:
    def __init__(self, ins, out_shapes, sems, make, aliases=None):
        self.ins, self.out_shapes, self.sems, self.make = list(ins), list(out_shapes), list(sems), make
        self.aliases = dict(aliases or {})

    def start(self, in_refs, out_refs, sems):
        for cp in self.make(in_refs, out_refs, sems):
            cp.start()

    def finish(self, in_refs, out_refs, sems):
        for cp in self.make(in_refs, out_refs, sems):
            cp.wait()


def _host_call(body, ins, in_specs, out_specs, out_shape, *, name, grid, semantics, scratch=(), exchange=None):
    ins, out_shape, scratch = list(ins), list(out_shape), list(scratch)
    if exchange is None:
        res = pl.pallas_call(
            body, name=name, grid=grid, in_specs=list(in_specs), out_specs=tuple(out_specs), out_shape=tuple(out_shape),
            scratch_shapes=scratch, compiler_params=_params(*semantics))(*ins)
        return tuple(res), ()
    n_in, n_out, n_scr = len(ins), len(out_shape), len(scratch)
    x_in, x_out = len(exchange.ins), len(exchange.out_shapes)

    def carrying(*refs):
        o0 = n_in + x_in
        s0 = o0 + n_out + x_out
        c_in, c_out, c_sems = refs[n_in:o0], refs[o0 + n_out:s0], refs[s0 + n_scr:]
        ids = [pl.program_id(a) for a in range(len(grid))]
        first = functools.reduce(jnp.logical_and, [i == 0 for i in ids])
        last = functools.reduce(jnp.logical_and, [i == g - 1 for i, g in zip(ids, grid)])

        @pl.when(first)
        def _():
            exchange.start(c_in, c_out, c_sems)

        body(*refs[:n_in], *refs[o0:o0 + n_out], *refs[s0:s0 + n_scr])

        @pl.when(last)
        def _():
            exchange.finish(c_in, c_out, c_sems)

    res = pl.pallas_call(
        carrying, name=name, grid=grid, in_specs=list(in_specs) + [HBM] * x_in,
        out_specs=tuple(out_specs) + (HBM,) * x_out, out_shape=tuple(out_shape) + tuple(exchange.out_shapes),
        scratch_shapes=scratch + exchange.sems,
        input_output_aliases={n_in + i: n_out + o for i, o in exchange.aliases.items()},
        compiler_params=_params(*(["arbitrary"] * len(grid))))(*ins, *exchange.ins)
    return tuple(res[:n_out]), tuple(res[n_out:])


def _exchange_call(exchange, name):
    x_in = len(exchange.ins)

    def body(*refs):
        x_out = len(exchange.out_shapes)
        c_in, c_out, c_sems = refs[:x_in], refs[x_in:x_in + x_out], refs[x_in + x_out:]
        exchange.start(c_in, c_out, c_sems)
        exchange.finish(c_in, c_out, c_sems)

    return pl.pallas_call(
        body, name=name, in_specs=[HBM] * x_in, out_specs=tuple([HBM] * len(exchange.out_shapes)),
        out_shape=tuple(exchange.out_shapes), scratch_shapes=exchange.sems, input_output_aliases=exchange.aliases,
    )(*exchange.ins)


def _core_major(d):
    return (d % 2) * 4 + d // 2


def _mm(a, b, mode, name, add=None, out_dtype=F32, tm_cap=512, tn_cap=1536, tk_cap=1536):
    if mode == "nn":
        (m, k), (k2, n) = a.shape, b.shape
    elif mode == "nt":
        (m, k), (n, k2) = a.shape, b.shape
    else:
        (k, m), (k2, n) = a.shape, b.shape
    assert k == k2, (a.shape, b.shape, mode)
    tm = _tile(m, tm_cap, 8 if mode != "tn" else LANES)
    tn = _tile(n, tn_cap)
    tk = _tile(k, tk_cap, LANES if mode != "tn" else 16)
    nk = k // tk
    dims = {"nn": NN, "nt": NT, "tn": TN}[mode]
    a_spec = {"nn": pl.BlockSpec((tm, tk), lambda i, j, kk: (i, kk)),
              "nt": pl.BlockSpec((tm, tk), lambda i, j, kk: (i, kk)),
              "tn": pl.BlockSpec((tk, tm), lambda i, j, kk: (kk, i))}[mode]
    b_spec = {"nn": pl.BlockSpec((tk, tn), lambda i, j, kk: (kk, j)),
              "nt": pl.BlockSpec((tn, tk), lambda i, j, kk: (j, kk)),
              "tn": pl.BlockSpec((tk, tn), lambda i, j, kk: (kk, j))}[mode]
    o_spec = pl.BlockSpec((tm, tn), lambda i, j, kk: (i, j))
    has_add = add is not None

    def body(*refs):
        a_ref, b_ref = refs[0], refs[1]
        add_ref = refs[2] if has_add else None
        o_ref = refs[3] if has_add else refs[2]
        part = _dot(a_ref[...].astype(BF16), b_ref[...].astype(BF16), dims)
        if nk == 1:
            if has_add:
                part = part + add_ref[...]
            o_ref[...] = part.astype(out_dtype)
        else:
            acc = refs[-1]
            kk = pl.program_id(2)

            @pl.when(kk == 0)
            def _():
                acc[...] = part + add_ref[...] if has_add else part

            @pl.when(kk > 0)
            def _():
                acc[...] += part

            @pl.when(kk == nk - 1)
            def _():
                o_ref[...] = acc[...].astype(out_dtype)

    ins = [a, b] + ([add] if has_add else [])
    in_specs = [a_spec, b_spec] + ([o_spec] if has_add else [])
    return pl.pallas_call(
        body, name=name, grid=(m // tm, n // tn, nk), in_specs=in_specs, out_specs=o_spec,
        out_shape=SDS((m, n), out_dtype),
        scratch_shapes=[pltpu.VMEM((tm, tn), F32)] if nk > 1 else [],
        compiler_params=_params("parallel", "parallel", "arbitrary"),
    )(*ins)


def _mm_tn_by_owner(a, b, name):
    (k, m), (k2, n) = a.shape, b.shape
    assert k == k2 and m % N_DEV == 0, (a.shape, b.shape)
    tk = _tile(k, 2048, 16)
    nk = k // tk
    rows = m // N_DEV

    def body(a_ref, b_ref, o_ref, acc):
        kk = pl.program_id(0)
        part = _dot(a_ref[...].astype(BF16), b_ref[...].astype(BF16), TN)

        @pl.when(kk == 0)
        def _():
            acc[...] = part

        @pl.when(kk > 0)
        def _():
            acc[...] += part

        @pl.when(kk == nk - 1)
        def _():
            for d in range(N_DEV):
                pos = _core_major(d)
                o_ref[pos * rows:(pos + 1) * rows, :] = acc[d * rows:(d + 1) * rows, :].astype(BF16)

    return pl.pallas_call(
        body, name=name, grid=(nk,),
        in_specs=[pl.BlockSpec((tk, m), lambda kk: (kk, 0)), pl.BlockSpec((tk, n), lambda kk: (kk, 0))],
        out_specs=pl.BlockSpec((m, n), lambda kk: (0, 0)), out_shape=SDS((m, n), BF16),
        scratch_shapes=[pltpu.VMEM((m, n), F32)], compiler_params=_params("arbitrary"),
    )(a, b)


def _rmsnorm_fwd(x, g, name):
    t, d = x.shape
    tm = _tile(t, 512, 8)

    def body(x_ref, g_ref, h_ref):
        xf = x_ref[...]
        r = lax.rsqrt(jnp.mean(xf * xf, axis=-1, keepdims=True) + EPS)
        h_ref[...] = (xf * r * g_ref[...]).astype(BF16)

    return pl.pallas_call(
        body, name=name, grid=(t // tm,),
        in_specs=[pl.BlockSpec((tm, d), lambda i: (i, 0)), pl.BlockSpec((1, d), lambda i: (0, 0))],
        out_specs=pl.BlockSpec((tm, d), lambda i: (i, 0)),
        out_shape=SDS((t, d), BF16), compiler_params=_params("parallel"),
    )(x, g)


def _rmsnorm_bwd_math(xf, g, dh):
    r = lax.rsqrt(jnp.mean(xf * xf, axis=-1, keepdims=True) + EPS)
    xh = xf * r
    gd = dh * g
    dx = r * (gd - xh * jnp.mean(gd * xh, axis=-1, keepdims=True))
    return dx, jnp.sum(dh * xh, axis=0, keepdims=True)


def _rmsnorm_dg(x, g, dh, name):
    t, d = x.shape
    tm = _tile(t, 512, 8)

    def body(x_ref, g_ref, dh_ref, dg_ref):
        @pl.when(pl.program_id(0) == 0)
        def _():
            dg_ref[...] = jnp.zeros_like(dg_ref)

        dg_ref[...] += _rmsnorm_bwd_math(x_ref[...], g_ref[...], dh_ref[...])[1]

    row = pl.BlockSpec((tm, d), lambda i: (i, 0))
    vec = pl.BlockSpec((1, d), lambda i: (0, 0))
    return pl.pallas_call(
        body, name=name, grid=(t // tm,), in_specs=[row, vec, row], out_specs=vec, out_shape=SDS((1, d), F32),
        compiler_params=_params("arbitrary"),
    )(x, g, dh)


def _group_rows(p):
    return slice(GROUP_OFFS[p], GROUP_OFFS[p + 1])


def _norm_in_proj(x, g, w_t):
    t, d = x.shape
    tm = _tile(t, 512, 16)
    widths = [GROUP_OFFS[p + 1] - GROUP_OFFS[p] for p in range(4)]

    def body(x_ref, g_ref, w_ref, h_ref, *outs):
        xf = x_ref[...]
        r = lax.rsqrt(jnp.mean(xf * xf, axis=-1, keepdims=True) + EPS)
        hb = (xf * r * g_ref[...]).astype(BF16)
        h_ref[...] = hb
        for p, o_ref in enumerate(outs):
            o_ref[...] = _dot(hb, w_ref[_group_rows(p), :], NT).astype(BF16)

    row = lambda wd: pl.BlockSpec((tm, wd), lambda i: (i, 0))
    return pl.pallas_call(
        body, name="norm_in_proj", grid=(t // tm,),
        in_specs=[row(d), pl.BlockSpec((1, d), lambda i: (0, 0)),
                  pl.BlockSpec(w_t.shape, lambda i: (0, 0), pipeline_mode=pl.Buffered(1))],
        out_specs=(row(d),) + tuple(row(wd) for wd in widths),
        out_shape=(SDS((t, d), BF16),) + tuple(SDS((t, wd), BF16) for wd in widths),
        compiler_params=_params("parallel"),
    )(x, g, w_t)


def _dh_rmsnorm_bwd(dps, w_t, x, g, dres, exchange=None):
    t, d = x.shape
    tm = _tile(t, 256, 16)
    n = len(dps)

    def body(*refs):
        dp_refs, w_ref = refs[:n], refs[n]
        x_ref, g_ref, dres_ref, dx_ref, dg_ref = refs[n + 1:]
        dh = None
        for p, dp_ref in enumerate(dp_refs):
            part = _dot(dp_ref[...], w_ref[_group_rows(p), :], NN)
            dh = part if dh is None else dh + part
        dx, dg = _rmsnorm_bwd_math(x_ref[...], g_ref[...], dh)
        dx_ref[...] = dx + dres_ref[...]

        @pl.when(pl.program_id(0) == 0)
        def _():
            dg_ref[...] = jnp.zeros_like(dg_ref)

        dg_ref[...] += dg

    row = pl.BlockSpec((tm, d), lambda i: (i, 0))
    vec = pl.BlockSpec((1, d), lambda i: (0, 0))
    return _host_call(
        body, list(dps) + [w_t, x, g, dres],
        [pl.BlockSpec((tm, a.shape[1]), lambda i: (i, 0)) for a in dps]
        + [pl.BlockSpec(w_t.shape, lambda i: (0, 0), pipeline_mode=pl.Buffered(1)), row, vec, row],
        (row, vec), (SDS((t, d), F32), SDS((1, d), F32)),
        name="dh_rmsnorm_bwd", grid=(t // tm,), semantics=("arbitrary",), exchange=exchange)


def _rope_tables(t):
    rows = t // GRID_W
    row = jnp.repeat(jnp.arange(rows, dtype=F32), GRID_W)
    col = jnp.tile(jnp.arange(GRID_W, dtype=F32), rows)
    n_freq = HEAD_DIM // 4
    inv = ROPE_THETA ** (-jnp.arange(n_freq, dtype=F32) / n_freq)
    ang = jnp.stack([row[:, None] * inv, col[:, None] * inv], axis=1)
    cos, sin = jnp.cos(ang), jnp.sin(ang)
    c64 = jnp.concatenate([cos[:, 0], cos[:, 0], cos[:, 1], cos[:, 1]], axis=-1)
    s64 = jnp.concatenate([-sin[:, 0], sin[:, 0], -sin[:, 1], sin[:, 1]], axis=-1)
    return jnp.tile(c64, (1, 2)), jnp.tile(s64, (1, 2))


def _head_sums(v, lane):
    lo = jnp.sum(jnp.where(lane < HEAD_DIM, v, 0.0), axis=-1, keepdims=True)
    hi = jnp.sum(jnp.where(lane < HEAD_DIM, 0.0, v), axis=-1, keepdims=True)
    return jnp.where(lane < HEAD_DIM, lo, hi)


def _swap16(v, lane):
    return jnp.where((lane % 32) < 16, pltpu.roll(v, LANES - 16, 1), pltpu.roll(v, 16, 1))


def _attn_prep_fwd(pa, cs, sn, qg2, kg2):
    t = pa.shape[0]
    tq = _tile(t, 512, LANES)
    scale = HEAD_DIM ** -0.5

    def body(pa_ref, cs_ref, sn_ref, qg_ref, kg_ref, q_ref, qT_ref, k_ref, kT_ref, vT_ref):
        lane = lax.broadcasted_iota(jnp.int32, (tq, LANES), 1)
        c, s = cs_ref[...], sn_ref[...]

        def norm_rope(xg, g2):
            r = lax.rsqrt(_head_sums(xg * xg, lane) * (1.0 / HEAD_DIM) + EPS)
            xn = xg * r * g2
            return xn * c + _swap16(xn, lane) * s

        for gi in range(4):
            sl = slice(gi * LANES, (gi + 1) * LANES)
            qr = norm_rope(pa_ref[:, sl].astype(F32), qg_ref[...]) * scale
            q_ref[:, sl] = qr.astype(BF16)
            qT_ref[sl, :] = qr.T.astype(BF16)
        kr = norm_rope(pa_ref[:, 512:640].astype(F32), kg_ref[...])
        kT_ref[...] = kr.T.astype(BF16)
        vT_ref[...] = pa_ref[:, 640:768].astype(F32).T.astype(BF16)
        kr = kr.astype(BF16)
        for kvh in range(A_KV_HEADS):
            k_ref[kvh] = kr[:, kvh * HEAD_DIM:(kvh + 1) * HEAD_DIM]

    row = lambda w: pl.BlockSpec((tq, w), lambda i: (i, 0))
    col = lambda r: pl.BlockSpec((r, tq), lambda i: (0, i))
    vec = pl.BlockSpec((1, LANES), lambda i: (0, 0))
    hm = pl.BlockSpec((A_KV_HEADS, tq, HEAD_DIM), lambda i: (0, i, 0))
    return pl.pallas_call(
        body, name="attn_prep_fwd", grid=(t // tq,),
        in_specs=[row(W_A), row(LANES), row(LANES), vec, vec],
        out_specs=(row(BW), col(BW), hm, col(LANES), col(LANES)),
        out_shape=(SDS((t, BW), BF16), SDS((BW, t), BF16), SDS((A_KV_HEADS, t, HEAD_DIM), BF16),
                   SDS((LANES, t), BF16), SDS((LANES, t), BF16)),
        compiler_params=_params("parallel"),
    )(pa, cs, sn, qg2, kg2)


def _attn_fwd(qT, k, vT, pa, exchange=None):
    t = qT.shape[1]
    tq = _tile(t, 256, LANES)
    grp = A_HEADS // A_KV_HEADS

    ck = _tile(t, KEY_CHUNK, LANES)

    def body(qT_ref, k_ref, vT_ref, pa_ref, o_ref, lse_ref, y_ref):
        def finish_pair(pair, o_pair):
            o2 = jnp.concatenate(o_pair, axis=0).T
            cols = slice(pair * LANES, (pair + 1) * LANES)
            o_ref[:, cols] = o2
            z = pa_ref[:, 768 + pair * LANES:768 + (pair + 1) * LANES].astype(F32)
            y_ref[:, cols] = (o2 * (z * _sigmoid(z))).astype(BF16)

        n_c = t // ck
        items = [(h, c) for h in range(A_HEADS) for c in range(n_c)]

        def scores(h, c):
            return _dot(k_ref[h // grp, c * ck:(c + 1) * ck, :], qT_ref[h * HEAD_DIM:(h + 1) * HEAD_DIM, :], NN)

        def weighted_values(h, c, pT):
            kvh = h // grp
            return _dot(vT_ref[kvh * HEAD_DIM:(kvh + 1) * HEAD_DIM, c * ck:(c + 1) * ck], pT, NN)

        o_pair, sT, pending, acc = [], scores(*items[0]), None, None
        for i in range(len(items) + 1):
            item = items[i] if i < len(items) else None
            sT_next = scores(*items[i + 1]) if i + 1 < len(items) else None
            if pending is not None:
                (ph, pc, pT, alpha, l_run) = pending
                o_c = weighted_values(ph, pc, pT)
                acc = o_c if alpha is None else acc * alpha + o_c
                if pc == n_c - 1:
                    o_pair.append(acc / l_run)
                    if ph % 2:
                        finish_pair(ph // 2, o_pair)
                        o_pair = []
                pending = None
            if item is not None:
                h, c = item
                m_c = jnp.max(sT, axis=0, keepdims=True)
                if c == 0:
                    m_new, alpha = m_c, None
                else:
                    m_new = jnp.maximum(m, m_c)
                    alpha = jnp.exp(m - m_new)
                pT = jnp.exp(sT - m_new)
                l_c = jnp.sum(pT, axis=0, keepdims=True)
                l = l_c if c == 0 else l * alpha + l_c
                m = m_new
                if c == n_c - 1:
                    lse_ref[h:h + 1, :] = m + jnp.log(l)
                pending = (h, c, pT.astype(BF16), alpha, l)
            sT = sT_next

    row = lambda w: pl.BlockSpec((tq, w), lambda i: (i, 0))
    return _host_call(
        body, (qT, k, vT, pa),
        [pl.BlockSpec((BW, tq), lambda i: (0, i)), pl.BlockSpec((A_KV_HEADS, t, HEAD_DIM), lambda i: (0, 0, 0)),
         pl.BlockSpec((A_KV_HEADS * HEAD_DIM, t), lambda i: (0, 0)), row(W_A)],
        (row(BW), pl.BlockSpec((A_HEADS, tq), lambda i: (0, i)), row(BW)),
        (SDS((t, BW), F32), SDS((A_HEADS, t), F32), SDS((t, BW), BF16)),
        name="attn_fwd", grid=(t // tq,), semantics=("parallel",), exchange=exchange)


def _attn_bwd(q, qT, k, kT, vT, pa, o, lse, dy, exchange=None):
    t = q.shape[0]
    tq = _tile(t, 256, LANES)
    grp = A_HEADS // A_KV_HEADS
    gw = grp * HEAD_DIM

    def body(q_ref, qT_ref, k_ref, kT_ref, vT_ref, z_ref, o_ref, lse_ref, dy_ref, dq_ref, dkT_ref, dvT_ref):
        @pl.when(pl.program_id(1) == 0)
        def _():
            dkT_ref[...] = jnp.zeros_like(dkT_ref)
            dvT_ref[...] = jnp.zeros_like(dvT_ref)

        z = z_ref[...].astype(F32)
        do = dy_ref[...] * (z * _sigmoid(z))
        doo = do * o_ref[...]
        doT = do.T
        kk, kT, vT = k_ref[...], kT_ref[...], vT_ref[...]
        heads = [slice(j * HEAD_DIM, (j + 1) * HEAD_DIM) for j in range(grp)]

        def scores(hs):
            return _dot(q_ref[:, hs], kT, NN), _dot(do[:, hs].astype(BF16), vT, NN)

        s, dp = scores(heads[0])
        for j, hs in enumerate(heads):
            nxt = scores(heads[j + 1]) if j + 1 < grp else None
            delta = jnp.sum(doo[:, hs], axis=-1, keepdims=True)
            p = jnp.exp(s - lse_ref[:, j:j + 1])
            ds_b = (p * (dp - delta)).astype(BF16)
            dq_ref[:, hs] = _dot(ds_b, kk, NN)
            dkT_ref[...] += _dot(qT_ref[hs, :], ds_b, NN)
            dvT_ref[...] += _dot(doT[hs, :].astype(BF16), p.astype(BF16), NN)
            if nxt is not None:
                s, dp = nxt

    grp_blk = pl.BlockSpec((tq, gw), lambda g, i: (i, g))
    kvT_blk = pl.BlockSpec((HEAD_DIM, t), lambda g, i: (g, 0))
    acc_blk = pl.BlockSpec((None, HEAD_DIM, t), lambda g, i: (g, 0, 0))
    return _host_call(
        body, (q, qT, k, kT, vT, pa, o, lse, dy),
        [grp_blk, pl.BlockSpec((gw, tq), lambda g, i: (g, i)), pl.BlockSpec((None, t, HEAD_DIM), lambda g, i: (g, 0, 0)),
         kvT_blk, kvT_blk, pl.BlockSpec((tq, gw), lambda g, i: (i, 768 // gw + g)), grp_blk,
         pl.BlockSpec((None, tq, grp), lambda g, i: (g, i, 0)), grp_blk],
        (grp_blk, acc_blk, acc_blk),
        (SDS((t, BW), F32), SDS((A_KV_HEADS, HEAD_DIM, t), F32), SDS((A_KV_HEADS, HEAD_DIM, t), F32)),
        name="attn_bwd", grid=(A_KV_HEADS, t // tq), semantics=("arbitrary", "arbitrary"), exchange=exchange)


def _attn_prep_bwd(pa, cs, sn, qg2, kg2, dq, dkT, dvT, dy, o):
    t = pa.shape[0]
    tq = _tile(t, 512, LANES)
    scale = HEAD_DIM ** -0.5

    def body(pa_ref, cs_ref, sn_ref, qg_ref, kg_ref, dq_ref, dkT_ref, dvT_ref, dy_ref, o_ref, dpa_ref, dqg_ref, dkg_ref):
        lane = lax.broadcasted_iota(jnp.int32, (tq, LANES), 1)
        c, s = cs_ref[...], sn_ref[...]

        @pl.when(pl.program_id(0) == 0)
        def _():
            dqg_ref[...] = jnp.zeros_like(dqg_ref)
            dkg_ref[...] = jnp.zeros_like(dkg_ref)

        def norm_rope_bwd(xg, g2, dout):
            r = lax.rsqrt(_head_sums(xg * xg, lane) * (1.0 / HEAD_DIM) + EPS)
            xh = xg * r
            dxn = dout * c + _swap16(dout * s, lane)
            gd = dxn * g2
            dx = r * (gd - xh * (_head_sums(gd * xh, lane) * (1.0 / HEAD_DIM)))
            return dx, jnp.sum(dxn * xh, axis=0, keepdims=True)

        for gi in range(4):
            sl = slice(gi * LANES, (gi + 1) * LANES)
            dx, dg = norm_rope_bwd(pa_ref[:, sl].astype(F32), qg_ref[...], dq_ref[:, sl] * scale)
            dpa_ref[:, sl] = dx.astype(BF16)
            dqg_ref[...] += dg
        dx, dg = norm_rope_bwd(pa_ref[:, 512:640].astype(F32), kg_ref[...], dkT_ref[...].T)
        dpa_ref[:, 512:640] = dx.astype(BF16)
        dkg_ref[...] += dg
        dpa_ref[:, 640:768] = dvT_ref[...].T.astype(BF16)
        z = pa_ref[:, 768:1280].astype(F32)
        _, dsilu = _silu_and_grad(z)
        dpa_ref[:, 768:1280] = (dy_ref[...] * o_ref[...] * dsilu).astype(BF16)

    row = lambda w: pl.BlockSpec((tq, w), lambda i: (i, 0))
    col = pl.BlockSpec((LANES, tq), lambda i: (0, i))
    vec = pl.BlockSpec((1, LANES), lambda i: (0, 0))
    return pl.pallas_call(
        body, name="attn_prep_bwd", grid=(t // tq,),
        in_specs=[row(W_A), row(LANES), row(LANES), vec, vec, row(BW), col, col, row(BW), row(BW)],
        out_specs=(row(W_A), vec, vec),
        out_shape=(SDS((t, W_A), BF16), SDS((1, LANES), F32), SDS((1, LANES), F32)),
        compiler_params=_params("arbitrary"),
    )(pa, cs, sn, qg2, kg2, dq, dkT, dvT, dy, o)


def _layer_norm(v, g, b):
    mu = jnp.mean(v, axis=-1, keepdims=True)
    xc = v - mu
    rs = lax.rsqrt(jnp.mean(xc * xc, axis=-1, keepdims=True) + EPS)
    xh = xc * rs
    return xh * g + b, xh, rs


def _gmlp_fwd(pb, lng, lnb, ws, bs):
    t = pb.shape[0]
    tb = _tile(t, 512, CHUNK)

    def body(pb_ref, g_ref, b_ref, ws_ref, bs_ref, y_ref):
        vln, _, _ = _layer_norm(pb_ref[:, BW:2 * BW].astype(F32), g_ref[...], b_ref[...])
        vb = vln.astype(BF16)
        for gi in range(B_GROUPS):
            w = ws_ref[gi].astype(BF16)
            cs_ = slice(gi * CHUNK, (gi + 1) * CHUNK)
            for n in range(tb // CHUNK):
                rs_ = slice(n * CHUNK, (n + 1) * CHUNK)
                mixed = _dot(w, vb[rs_, cs_], NN) + bs_ref[gi]
                z = pb_ref[rs_, 2 * BW + gi * CHUNK:2 * BW + (gi + 1) * CHUNK].astype(F32)
                y_ref[rs_, cs_] = (pb_ref[rs_, cs_].astype(F32) * mixed * (z * _sigmoid(z))).astype(BF16)

    return pl.pallas_call(
        body, name="gmlp_fwd", grid=(t // tb,),
        in_specs=[pl.BlockSpec((tb, W_B), lambda i: (i, 0)),
                  pl.BlockSpec((1, BW), lambda i: (0, 0)), pl.BlockSpec((1, BW), lambda i: (0, 0)),
                  pl.BlockSpec((B_GROUPS, CHUNK, CHUNK), lambda i: (0, 0, 0)),
                  pl.BlockSpec((B_GROUPS, CHUNK, 1), lambda i: (0, 0, 0))],
        out_specs=pl.BlockSpec((tb, BW), lambda i: (i, 0)),
        out_shape=SDS((t, BW), BF16), compiler_params=_params("parallel"),
    )(pb, lng, lnb, ws, bs)


def _gmlp_bwd(pb, lng, lnb, ws, bs, dy):
    t = pb.shape[0]
    tb = _tile(t, 256, CHUNK)

    def body(pb_ref, g_ref, b_ref, ws_ref, bs_ref, dy_ref, dpb_ref, dws_ref, dbs_ref, dg_ref, db_ref, dvln_ref):
        @pl.when(pl.program_id(0) == 0)
        def _():
            dws_ref[...] = jnp.zeros_like(dws_ref)
            dbs_ref[...] = jnp.zeros_like(dbs_ref)
            dg_ref[...] = jnp.zeros_like(dg_ref)
            db_ref[...] = jnp.zeros_like(db_ref)

        vln, xh, rs = _layer_norm(pb_ref[:, BW:2 * BW].astype(F32), g_ref[...], b_ref[...])
        vb = vln.astype(BF16)
        for gi in range(B_GROUPS):
            w = ws_ref[gi].astype(BF16)
            cs_ = slice(gi * CHUNK, (gi + 1) * CHUNK)
            for n in range(tb // CHUNK):
                rs_ = slice(n * CHUNK, (n + 1) * CHUNK)
                vbc = vb[rs_, cs_]
                mixed = _dot(w, vbc, NN) + bs_ref[gi]
                zs = slice(2 * BW + gi * CHUNK, 2 * BW + (gi + 1) * CHUNK)
                z = pb_ref[rs_, zs].astype(F32)
                u = pb_ref[rs_, cs_].astype(F32)
                sil, dsil = _silu_and_grad(z)
                dyc = dy_ref[rs_, cs_]
                dmixed = dyc * u * sil
                dpb_ref[rs_, cs_] = (dyc * mixed * sil).astype(BF16)
                dpb_ref[rs_, zs] = (dyc * u * mixed * dsil).astype(BF16)
                dmb = dmixed.astype(BF16)
                dws_ref[gi] += _dot(dmb, vbc, NT)
                dbs_ref[gi] += jnp.sum(dmixed, axis=-1, keepdims=True)
                dvln_ref[rs_, cs_] = _dot(w, dmb, TN)
        dvln = dvln_ref[...]
        dg_ref[...] += jnp.sum(dvln * xh, axis=0, keepdims=True)
        db_ref[...] += jnp.sum(dvln, axis=0, keepdims=True)
        gd = dvln * g_ref[...]
        dv = rs * (gd - jnp.mean(gd, axis=-1, keepdims=True) - xh * jnp.mean(gd * xh, axis=-1, keepdims=True))
        dpb_ref[:, BW:2 * BW] = dv.astype(BF16)

    vec = pl.BlockSpec((1, BW), lambda i: (0, 0))
    wsb = pl.BlockSpec((B_GROUPS, CHUNK, CHUNK), lambda i: (0, 0, 0))
    bsb = pl.BlockSpec((B_GROUPS, CHUNK, 1), lambda i: (0, 0, 0))
    return pl.pallas_call(
        body, name="gmlp_bwd", grid=(t // tb,),
        in_specs=[pl.BlockSpec((tb, W_B), lambda i: (i, 0)), vec, vec, wsb, bsb, pl.BlockSpec((tb, BW), lambda i: (i, 0))],
        out_specs=(pl.BlockSpec((tb, W_B), lambda i: (i, 0)), wsb, bsb, vec, vec),
        out_shape=(SDS((t, W_B), BF16), SDS((B_GROUPS, CHUNK, CHUNK), F32), SDS((B_GROUPS, CHUNK, 1), F32),
                   SDS((1, BW), F32), SDS((1, BW), F32)),
        scratch_shapes=[pltpu.VMEM((tb, BW), F32)],
        compiler_params=_params("arbitrary"),
    )(pb, lng, lnb, ws, bs, dy)


def _mem_softmax(qh, kh):
    s = _dot(qh, kh, NT) * (M_HEAD_DIM ** -0.5)
    e = jnp.exp(s - jnp.max(s, axis=-1, keepdims=True))
    return e / jnp.sum(e, axis=-1, keepdims=True)


def _mem_attn_fwd(pm, kv):
    t = pm.shape[0]
    tq = _tile(t, 512, 8)
    ml = kv.shape[0]

    def body(pm_ref, kv_ref, y_ref):
        for h in range(M_HEADS):
            hs = slice(h * M_HEAD_DIM, (h + 1) * M_HEAD_DIM)
            kh = kv_ref[:, hs].astype(BF16)
            vh = kv_ref[:, BW + h * M_HEAD_DIM:BW + (h + 1) * M_HEAD_DIM].astype(BF16)
            p = _mem_softmax(pm_ref[:, hs].astype(BF16), kh)
            o = _dot(p.astype(BF16), vh, NN)
            z = pm_ref[:, BW + h * M_HEAD_DIM:BW + (h + 1) * M_HEAD_DIM].astype(F32)
            y_ref[:, hs] = (o * (z * _sigmoid(z))).astype(BF16)

    return pl.pallas_call(
        body, name="mem_attn_fwd", grid=(t // tq,),
        in_specs=[pl.BlockSpec((tq, W_M), lambda i: (i, 0)), pl.BlockSpec((ml, 2 * BW), lambda i: (0, 0))],
        out_specs=pl.BlockSpec((tq, BW), lambda i: (i, 0)),
        out_shape=SDS((t, BW), BF16), compiler_params=_params("parallel"),
    )(pm, kv)


def _mem_attn_bwd(pm, kv, dy):
    t = pm.shape[0]
    tq = _tile(t, 512, 8)
    ml = kv.shape[0]
    scale = M_HEAD_DIM ** -0.5

    def body(pm_ref, kv_ref, dy_ref, dpm_ref, dkv_ref):
        @pl.when(pl.program_id(0) == 0)
        def _():
            dkv_ref[...] = jnp.zeros_like(dkv_ref)

        for h in range(M_HEADS):
            hs = slice(h * M_HEAD_DIM, (h + 1) * M_HEAD_DIM)
            zs = slice(BW + h * M_HEAD_DIM, BW + (h + 1) * M_HEAD_DIM)
            kh = kv_ref[:, hs].astype(BF16)
            vh = kv_ref[:, zs].astype(BF16)
            qh = pm_ref[:, hs].astype(BF16)
            p = _mem_softmax(qh, kh)
            pb = p.astype(BF16)
            o = _dot(pb, vh, NN)
            sil, dsil = _silu_and_grad(pm_ref[:, zs].astype(F32))
            dyh = dy_ref[:, hs]
            do = dyh * sil
            dpm_ref[:, zs] = (dyh * o * dsil).astype(BF16)
            delta = jnp.sum(do * o, axis=-1, keepdims=True)
            do_b = do.astype(BF16)
            dp = _dot(do_b, vh, NT)
            dr_b = (p * (dp - delta) * scale).astype(BF16)
            dpm_ref[:, hs] = _dot(dr_b, kh, NN).astype(BF16)
            dkv_ref[:, hs] += _dot(dr_b, qh, TN)
            dkv_ref[:, zs] += _dot(pb, do_b, TN)

    kvb = pl.BlockSpec((ml, 2 * BW), lambda i: (0, 0))
    return pl.pallas_call(
        body, name="mem_attn_bwd", grid=(t // tq,),
        in_specs=[pl.BlockSpec((tq, W_M), lambda i: (i, 0)), kvb, pl.BlockSpec((tq, BW), lambda i: (i, 0))],
        out_specs=(pl.BlockSpec((tq, W_M), lambda i: (i, 0)), kvb),
        out_shape=(SDS((t, W_M), BF16), SDS((ml, 2 * BW), F32)),
        compiler_params=_params("arbitrary"),
    )(pm, kv, dy)


def _merge_fwd(ya, yb, ym, pg, wbr, exchange=None):
    t = ya.shape[0]
    tm = _tile(t, 512, 8)

    def body(ya_ref, yb_ref, ym_ref, pg_ref, w_ref, m_ref):
        ups = [_dot(y_ref[...], w_ref[n], NN) for n, y_ref in enumerate((ya_ref, yb_ref, ym_ref))]
        acc = None
        for n, up in enumerate(ups):
            term = _sigmoid(pg_ref[:, n * D_MODEL:(n + 1) * D_MODEL].astype(F32)) * up
            acc = term if acc is None else acc + term
        m_ref[...] = acc.astype(BF16)

    yb_spec = pl.BlockSpec((tm, BW), lambda i: (i, 0))
    return _host_call(
        body, (ya, yb, ym, pg, wbr),
        [yb_spec, yb_spec, yb_spec, pl.BlockSpec((tm, W_G), lambda i: (i, 0)),
         pl.BlockSpec((3, BW, D_MODEL), lambda i: (0, 0, 0))],
        (pl.BlockSpec((tm, D_MODEL), lambda i: (i, 0)),), (SDS((t, D_MODEL), BF16),),
        name="merge_fwd", grid=(t // tm,), semantics=("parallel",), exchange=exchange)


def _merge_bwd(ya, yb, ym, pg, wbr, dm, exchange=None):
    t = ya.shape[0]
    tm = _tile(t, 512, 16)

    def body(ya_ref, yb_ref, ym_ref, pg_ref, w_ref, dm_ref, dya_ref, dyb_ref, dym_ref, dpg_ref, dw_ref):
        @pl.when(pl.program_id(0) == 0)
        def _():
            dw_ref[...] = jnp.zeros_like(dw_ref)

        dmf = dm_ref[...]
        branches = ((ya_ref, dya_ref), (yb_ref, dyb_ref), (ym_ref, dym_ref))

        def gate(n):
            gt = _sigmoid(pg_ref[:, n * D_MODEL:(n + 1) * D_MODEL].astype(F32))
            return gt, (dmf * gt).astype(BF16)

        gt, dup = gate(0)
        for n, (y_ref, dy_ref) in enumerate(branches):
            nxt = gate(n + 1) if n + 1 < len(branches) else None
            y, w = y_ref[...], w_ref[n]
            dy_ref[...] = _dot(dup, w, NT)
            dw_ref[n] += _dot(y, dup, TN)
            up = _dot(y, w, NN)
            dpg_ref[:, n * D_MODEL:(n + 1) * D_MODEL] = (dmf * up * gt * (1.0 - gt)).astype(BF16)
            if nxt is not None:
                gt, dup = nxt

    y_spec = pl.BlockSpec((tm, BW), lambda i: (i, 0))
    w_spec = pl.BlockSpec((3, BW, D_MODEL), lambda i: (0, 0, 0))
    return _host_call(
        body, (ya, yb, ym, pg, wbr, dm),
        [y_spec, y_spec, y_spec, pl.BlockSpec((tm, W_G), lambda i: (i, 0)), w_spec,
         pl.BlockSpec((tm, D_MODEL), lambda i: (i, 0))],
        (y_spec, y_spec, y_spec, pl.BlockSpec((tm, W_G), lambda i: (i, 0)), w_spec),
        (SDS((t, BW), F32), SDS((t, BW), F32), SDS((t, BW), F32), SDS((t, W_G), BF16), SDS((3, BW, D_MODEL), F32)),
        name="merge_bwd", grid=(t // tm,), semantics=("arbitrary",), exchange=exchange)


def _loss_head(x, g, target):
    t, d = x.shape
    tm = _tile(t, 512, 8)

    def body(x_ref, g_ref, t_ref, loss_ref, dx_ref, dg_ref):
        @pl.when(pl.program_id(0) == 0)
        def _():
            loss_ref[...] = jnp.zeros_like(loss_ref)
            dg_ref[...] = jnp.zeros_like(dg_ref)

        xf = x_ref[...]
        r = lax.rsqrt(jnp.mean(xf * xf, axis=-1, keepdims=True) + EPS)
        xh = xf * r
        err = xh * g_ref[...] - t_ref[...]
        per_tok = jnp.mean(err * err, axis=-1, keepdims=True)
        loss_ref[...] += 0.5 * jnp.sum(per_tok, axis=0, keepdims=True)
        dy = err * (1.0 / d)
        gd = dy * g_ref[...]
        dx_ref[...] = r * (gd - xh * jnp.mean(gd * xh, axis=-1, keepdims=True))
        dg_ref[...] += jnp.sum(dy * xh, axis=0, keepdims=True)

    row = pl.BlockSpec((tm, d), lambda i: (i, 0))
    vec = pl.BlockSpec((1, d), lambda i: (0, 0))
    return pl.pallas_call(
        body, name="loss_head", grid=(t // tm,),
        in_specs=[row, vec, row],
        out_specs=(pl.BlockSpec((1, 1), lambda i: (0, 0)), row, vec),
        out_shape=(SDS((1, 1), F32), SDS((t, d), F32), SDS((1, d), F32)),
        compiler_params=_params("arbitrary"),
    )(x, g, target)


def _layer_fwd(x, mem, w, tabs, next_shards=None):
    cs, sn = tabs
    riding = next_shards is not None
    h, pa, pb, pm, pg = _norm_in_proj(x, w["norm_g"], w["w_in_t"])
    q, qT, k, kT, vT = _attn_prep_fwd(pa, cs, sn, w["qg2"], w["kg2"])
    (o, lse, ya), gathered = _attn_fwd(qT, k, vT, pa, exchange=_gather_first_hop(next_shards) if riding else None)
    yb = _gmlp_fwd(pb, w["ln_g"], w["ln_b"], w["w_s"], w["b_s"])
    memn = _rmsnorm_fwd(mem, w["mem_g"], "mem_rmsnorm_fwd")
    kv = _mm(memn, w["w_kv"], "nn", "mem_kv")
    ym = _mem_attn_fwd(pm, kv)
    (merged,), gathered = _merge_fwd(ya, yb, ym, pg, w["w_br"], exchange=_gather_forward(gathered) if riding else None)
    x_next = _mm(merged, w["w_out"], "nn", "out_proj", add=x)
    saved = dict(x=x, h=h, pa=pa, pb=pb, pm=pm, pg=pg, q=q, qT=qT, k=k, kT=kT, vT=vT, o=o, lse=lse, ya=ya, yb=yb, ym=ym,
                 memn=memn, kv=kv, merged=merged)
    return x_next, saved, gathered


def _layer_bwd(dx_out, mem, w, s, tabs, pending=None, core=None, scatter_own=False):
    cs, sn = tabs
    t = dx_out.shape[0]
    riding = pending is not None
    dmerged = _mm(dx_out, w["w_out"], "nt", "d_merged")
    shard_rows = D_MODEL // N_DEV
    d_w_out = _mm_tn_by_owner(s["merged"], dx_out, "d_w_out")
    (dya, dyb, dym, dpg, d_w_br), recv = _merge_bwd(s["ya"], s["yb"], s["ym"], s["pg"], w["w_br"], dmerged,
                                                    exchange=_scatter_to_sibling(pending) if riding else None)
    parts = _pair_sums(pending, recv, core) if riding else None
    grp = A_HEADS // A_KV_HEADS
    lse_cols = s["lse"].reshape(A_KV_HEADS, grp, t).transpose(0, 2, 1)
    (dq, dkT, dvT), from_chips = _attn_bwd(s["q"], s["qT"], s["k"], s["kT"], s["vT"], s["pa"], s["o"], lse_cols, dya,
                                           exchange=_scatter_to_chips(parts) if riding else None)
    dpa, d_qg2, d_kg2 = _attn_prep_bwd(s["pa"], cs, sn, w["qg2"], w["kg2"], dq, dkT.reshape(-1, t), dvT.reshape(-1, t),
                                       dya, s["o"])
    dpb, d_w_s, d_b_s, d_ln_g, d_ln_b = _gmlp_bwd(s["pb"], w["ln_g"], w["ln_b"], w["w_s"], w["b_s"], dyb)
    dpm, dkv = _mem_attn_bwd(s["pm"], s["kv"], dym)
    d_w_kv = _mm_tn_by_owner(s["memn"], dkv, "d_w_kv")
    dmemn = _mm(dkv, w["w_kv"], "nt", "d_memn")
    d_mem_g = _rmsnorm_dg(mem, w["mem_g"], dmemn, "mem_rmsnorm_bwd")
    dps = (dpa, dpb, dpm, dpg)
    d_w_in = _split_w_in_t(*[_mm(dp, s["h"], "tn", "d_w_" + nm, out_dtype=BF16, tm_cap=1024, tn_cap=D_MODEL, tk_cap=4096)
                             for nm, dp in zip("abmg", dps)])
    grads = dict(w_in=d_w_in[:, None], w_mem_kv=d_w_kv.reshape(N_DEV, 1, shard_rows, -1), w_br=_split_w_br(d_w_br),
                 w_out=d_w_out.reshape(N_DEV, 1, shard_rows, -1))
    own = None
    if scatter_own:
        dests = [grads[n] for n in BIG]
        own_parts = _pair_sums(dests, _exchange_call(_scatter_to_sibling(dests), "rs_sibling_swap"), core)
        own = _scatter_to_chips(own_parts)
    (dx_in, d_norm_g), own_from_chips = _dh_rmsnorm_bwd(dps, w["w_in_t"], s["x"], w["norm_g"], dx_out, exchange=own)
    grads.update(norm_g=d_norm_g[0], q_norm_g=d_qg2[0, :HEAD_DIM] + d_qg2[0, HEAD_DIM:],
                 k_norm_g=d_kg2[0, :HEAD_DIM] + d_kg2[0, HEAD_DIM:], sg_ln_g=d_ln_g[0], sg_ln_b=d_ln_b[0],
                 w_s=d_w_s, b_s=d_b_s[:, :, 0], mem_norm_g=d_mem_g[0])
    return (dx_in, grads, ((parts, from_chips) if riding else None),
            ((own_parts, own_from_chips) if scatter_own else None))


def _layer_weights(l, w_in_t, w_kv, w_br, w_out, small):
    tile2 = lambda g: jnp.tile(g.reshape(1, -1), (1, 2))
    return dict(
        w_in_t=w_in_t, w_kv=w_kv, w_br=w_br, w_out=w_out,
        norm_g=small["norm_g"][l].reshape(1, -1), qg2=tile2(small["q_norm_g"][l]), kg2=tile2(small["k_norm_g"][l]),
        ln_g=small["sg_ln_g"][l].reshape(1, -1), ln_b=small["sg_ln_b"][l].reshape(1, -1),
        w_s=small["w_s"][l], b_s=small["b_s"][l][:, :, None], mem_g=small["mem_norm_g"][l].reshape(1, -1))


def _position():
    x, y, c = lax.axis_index("x"), lax.axis_index("y"), lax.axis_index("c")
    return x, y, c, [(1 - x, y), (x, 1 - y), (1 - x, 1 - y)]


def _gather_first_hop(shards):
    n = len(shards)

    def make(x_refs, out_refs, sems):
        send_sems, recv_sems, local_sems = sems
        x, y, c, chips = _position()
        me = 4 * x + 2 * y + c
        peers = [(x, y, 1 - c)] + [(cx, cy, c) for cx, cy in chips]
        copies = [pltpu.make_async_copy(x_refs[t], out_refs[t].at[me], local_sems.at[t]) for t in range(n)]
        copies += [pltpu.make_async_remote_copy(
            src_ref=x_refs[t], dst_ref=out_refs[t].at[me], send_sem=send_sems.at[t, k], recv_sem=recv_sems.at[t, k],
            device_id=peer, device_id_type=MESH_ID) for t in range(n) for k, peer in enumerate(peers)]
        return copies

    return _Exchange(shards, [SDS((N_DEV,) + a.shape, a.dtype) for a in shards],
                     [pltpu.SemaphoreType.DMA((n, 4)), pltpu.SemaphoreType.DMA((n, 4)), pltpu.SemaphoreType.DMA((n,))], make)


def _gather_forward(gathered):
    n = len(gathered)

    def make(in_refs, out_refs, sems):
        send_sems, recv_sems = sems
        x, y, c, chips = _position()
        return [pltpu.make_async_remote_copy(
            src_ref=in_refs[t].at[4 * cx + 2 * cy + c], dst_ref=out_refs[t].at[4 * cx + 2 * cy + c],
            send_sem=send_sems.at[t, j], recv_sem=recv_sems.at[t, j], device_id=(x, y, 1 - c), device_id_type=MESH_ID)
            for t in range(n) for j, (cx, cy) in enumerate(chips)]

    return _Exchange(gathered, [SDS(a.shape, a.dtype) for a in gathered],
                     [pltpu.SemaphoreType.DMA((n, 3)), pltpu.SemaphoreType.DMA((n, 3))], make,
                     aliases={t: t for t in range(n)})


def _all_gather(shards, name):
    return _exchange_call(_gather_forward(_exchange_call(_gather_first_hop(shards), name + "_hop1")), name + "_hop2")


def _shard_segments(d):
    shard = IN_WIDTH // N_DEV
    lo, hi = d * shard, (d + 1) * shard
    out = []
    for g in range(4):
        a, b = max(lo, GROUP_OFFS[g]), min(hi, GROUP_OFFS[g + 1])
        if a < b:
            out.append((g, a - GROUP_OFFS[g], b - GROUP_OFFS[g], a - lo))
    return out


def _split_w_in_t(da, db, dm, dg):
    cols = da.shape[1]
    shard = IN_WIDTH // N_DEV
    tc = _tile(cols, 256)

    def body(a_ref, b_ref, m_ref, g_ref, o_ref):
        ins = (a_ref, b_ref, m_ref, g_ref)
        for d in range(N_DEV):
            for g, lo, hi, off in _shard_segments(d):
                o_ref[_core_major(d), off:off + hi - lo, :] = ins[g][lo:hi, :]

    return pl.pallas_call(
        body, name="split_w_in", grid=(cols // tc,),
        in_specs=[pl.BlockSpec((a.shape[0], tc), lambda j: (0, j)) for a in (da, db, dm, dg)],
        out_specs=pl.BlockSpec((N_DEV, shard, tc), lambda j: (0, 0, j)),
        out_shape=SDS((N_DEV, shard, cols), da.dtype), compiler_params=_params("parallel"),
    )(da, db, dm, dg)


def _assemble_w_br(gathered):
    _, nb, rows, shard = gathered.shape

    def body(g_ref, o_ref):
        for d in range(N_DEV):
            o_ref[:, :, d * shard:(d + 1) * shard] = g_ref[d]

    return pl.pallas_call(
        body, name="assemble_w_br", out_shape=SDS((nb, rows, N_DEV * shard), gathered.dtype),
        compiler_params=pltpu.CompilerParams(vmem_limit_bytes=VMEM_LIMIT),
    )(gathered)


def _split_w_br(dw):
    nb, rows, cols = dw.shape
    shard = cols // N_DEV

    def body(d_ref, o_ref):
        for d in range(N_DEV):
            o_ref[_core_major(d)] = d_ref[:, :, d * shard:(d + 1) * shard].astype(BF16)

    return pl.pallas_call(
        body, name="split_w_br", out_shape=SDS((N_DEV, nb, rows, shard), BF16),
        compiler_params=pltpu.CompilerParams(vmem_limit_bytes=VMEM_LIMIT),
    )(dw)


def _scatter_to_sibling(dests):
    n = len(dests)

    def make(d_refs, recv_refs, sems):
        send_sems, recv_sems = sems
        x, y, c, _ = _position()
        return [pltpu.make_async_remote_copy(
            src_ref=d_refs[t].at[pl.ds((1 - c) * 4, 4)], dst_ref=recv_refs[t], send_sem=send_sems.at[t],
            recv_sem=recv_sems.at[t], device_id=(x, y, 1 - c), device_id_type=MESH_ID) for t in range(n)]

    return _Exchange(dests, [SDS((4,) + a.shape[1:], a.dtype) for a in dests],
                     [pltpu.SemaphoreType.DMA((n,)), pltpu.SemaphoreType.DMA((n,))], make)


def _scatter_to_chips(parts):
    n = len(parts)

    def make(p_refs, recv_refs, sems):
        send_sems, recv_sems = sems
        _, _, c, chips = _position()
        return [pltpu.make_async_remote_copy(
            src_ref=p_refs[t].at[2 * cx + cy], dst_ref=recv_refs[t].at[k], send_sem=send_sems.at[t, k],
            recv_sem=recv_sems.at[t, k], device_id=(cx, cy, c), device_id_type=MESH_ID)
            for t in range(n) for k, (cx, cy) in enumerate(chips)]

    return _Exchange(parts, [SDS((3,) + a.shape[1:], a.dtype) for a in parts],
                     [pltpu.SemaphoreType.DMA((n, 3)), pltpu.SemaphoreType.DMA((n, 3))], make)


def _pair_sums(dests, recv, core):
    return [_pair_sum(d, r, core, "rs_pair_sum_" + n) for n, d, r in zip(BIG, dests, recv)]


def _pair_sum(dest, recv, core, name):
    _, na, r, cdim = dest.shape

    def body(core_ref, a_ref, b_ref, o_ref):
        o_ref[...] = (a_ref[...].astype(F32) + b_ref[...].astype(F32)).astype(o_ref.dtype)

    blk = pl.BlockSpec((None, na, r, cdim), lambda j, core_ref: (j, 0, 0, 0))
    return pl.pallas_call(
        body, name=name, out_shape=SDS(recv.shape, BF16),
        grid_spec=pltpu.PrefetchScalarGridSpec(
            num_scalar_prefetch=1, grid=(4,),
            in_specs=[pl.BlockSpec((None, na, r, cdim), lambda j, core_ref: (core_ref[0] * 4 + j, 0, 0, 0)), blk],
            out_specs=blk),
        compiler_params=_params("parallel"),
    )(core, dest, recv)


def _adamw_math(w, g, m, v):
    m = ADAM_B1 * m + (1.0 - ADAM_B1) * g
    v = ADAM_B2 * v + (1.0 - ADAM_B2) * (g * g)
    m_hat = m / (1.0 - ADAM_B1 ** ADAM_STEP)
    v_hat = v / (1.0 - ADAM_B2 ** ADAM_STEP)
    delta = -ADAM_LR * (m_hat / (jnp.sqrt(v_hat) + ADAM_EPS) + ADAM_WD * w)
    return delta, m, v


def _sum_and_adamw(parts, w, m, v, name):
    n, r, ln = parts.shape
    tr = _tile(r, 512, 16)

    def body(p_ref, w_ref, m_ref, v_ref, g_out, d_out, m_out, v_out):
        g = p_ref[0].astype(F32)
        for j in range(1, n):
            g = g + p_ref[j].astype(F32)
        delta, nm, nv = _adamw_math(w_ref[...], g, m_ref[...], v_ref[...])
        g_out[...] = g
        d_out[...] = delta
        m_out[...] = nm
        v_out[...] = nv

    blk = pl.BlockSpec((tr, ln), lambda i: (i, 0))
    return pl.pallas_call(
        body, name=name, grid=(r // tr,),
        in_specs=[pl.BlockSpec((n, tr, ln), lambda i: (0, i, 0)), blk, blk, blk],
        out_specs=(blk, blk, blk, blk), out_shape=tuple(SDS((r, ln), F32) for _ in range(4)),
        compiler_params=_params("parallel"),
    )(parts, w, m, v)


BIG = ("w_in", "w_mem_kv", "w_br", "w_out")
SMALL = ("norm_g", "q_norm_g", "k_norm_g", "sg_ln_g", "sg_ln_b", "w_s", "b_s", "mem_norm_g", "final_g")


def _pack(arrs, row_unit=16):
    flat = jnp.concatenate([a.reshape(-1) for a in arrs])
    pad = (-flat.shape[0]) % (row_unit * LANES)
    if pad:
        flat = jnp.concatenate([flat, jnp.zeros((pad,), flat.dtype)])
    return flat.reshape(-1, LANES)


def _unpack(buf, shapes):
    flat = buf.reshape(-1)
    out, off = [], 0
    for shp in shapes:
        n = 1
        for s_ in shp:
            n *= s_
        out.append(flat[off:off + n].reshape(shp))
        off += n
    return out


def _shard_sum_adamw(part, from_chips, chip, w, m, v, layer, prev, name):
    _, na, r, cdim = part.shape
    flat = w.ndim == 3
    assert not flat or na == 1
    tr = _tile(r, max(8, (256 * 1024) // (na * cdim)), 8)
    n_prev = 0 if prev is None else len(prev)

    def body(chip_ref, p_ref, f_ref, w_ref, m_ref, v_ref, *rest):
        g_out, d_out, m_out, v_out = rest[n_prev:]
        g = p_ref[...].astype(F32)
        for j in range(3):
            g = g + f_ref[j].astype(F32)
        delta, nm, nv = _adamw_math(w_ref[...], g, m_ref[...], v_ref[...])
        g_out[...] = g
        d_out[...] = delta
        m_out[...] = nm
        v_out[...] = nv

    a_blk = None if flat else na
    if flat:
        lay = pl.BlockSpec((None, tr, cdim), lambda i, chip_ref: (layer, i, 0))
    else:
        lay = pl.BlockSpec((None, na, tr, cdim), lambda i, chip_ref: (layer, 0, i, 0))
    return pl.pallas_call(
        body, name=name, out_shape=tuple(SDS(w.shape, F32) for _ in range(4)),
        grid_spec=pltpu.PrefetchScalarGridSpec(
            num_scalar_prefetch=1, grid=(r // tr,),
            in_specs=[pl.BlockSpec((None, a_blk, tr, cdim), lambda i, chip_ref: (chip_ref[0], 0, i, 0)),
                      pl.BlockSpec((3, a_blk, tr, cdim), lambda i, chip_ref: (0, 0, i, 0)), lay, lay, lay]
            + [HBM] * n_prev,
            out_specs=(lay, lay, lay, lay)),
        input_output_aliases={6 + j: j for j in range(n_prev)},
        compiler_params=_params("parallel"),
    )(chip, part, from_chips, w, m, v, *(prev or ()))


def kernel(x, mem, norm_g, w_in, q_norm_g, k_norm_g, sg_ln_g, sg_ln_b, w_s, b_s, mem_norm_g, w_mem_kv, w_br, w_out, final_g, loss_target, m_norm_g, m_w_in, m_q_norm_g, m_k_norm_g, m_sg_ln_g, m_sg_ln_b, m_w_s, m_b_s, m_mem_norm_g, m_w_mem_kv, m_w_br, m_w_out, m_final_g, v_norm_g, v_w_in, v_q_norm_g, v_k_norm_g, v_sg_ln_g, v_sg_ln_b, v_w_s, v_b_s, v_mem_norm_g, v_w_mem_kv, v_w_br, v_w_out, v_final_g):
    wts = dict(norm_g=norm_g, w_in=w_in, q_norm_g=q_norm_g, k_norm_g=k_norm_g, sg_ln_g=sg_ln_g, sg_ln_b=sg_ln_b,
               w_s=w_s, b_s=b_s, mem_norm_g=mem_norm_g, w_mem_kv=w_mem_kv, w_br=w_br, w_out=w_out, final_g=final_g)
    mom1 = dict(norm_g=m_norm_g, w_in=m_w_in, q_norm_g=m_q_norm_g, k_norm_g=m_k_norm_g, sg_ln_g=m_sg_ln_g,
                sg_ln_b=m_sg_ln_b, w_s=m_w_s, b_s=m_b_s, mem_norm_g=m_mem_norm_g, w_mem_kv=m_w_mem_kv, w_br=m_w_br,
                w_out=m_w_out, final_g=m_final_g)
    mom2 = dict(norm_g=v_norm_g, w_in=v_w_in, q_norm_g=v_q_norm_g, k_norm_g=v_k_norm_g, sg_ln_g=v_sg_ln_g,
                sg_ln_b=v_sg_ln_b, w_s=v_w_s, b_s=v_b_s, mem_norm_g=v_mem_norm_g, w_mem_kv=v_w_mem_kv, w_br=v_w_br,
                w_out=v_w_out, final_g=v_final_g)
    dp = w_in.shape[0]
    core = lax.axis_index("c").astype(jnp.int32).reshape(1)
    chip = (2 * lax.axis_index("x") + lax.axis_index("y")).astype(jnp.int32).reshape(1)

    t_in = lambda a: jnp.swapaxes(a, 1, 2)
    wts, mom1, mom2 = [dict(d_, w_in=t_in(d_["w_in"])) for d_ in (wts, mom1, mom2)]

    shard_bf = {n: wts[n].astype(BF16) for n in BIG}
    shards = lambda l: [shard_bf[n][l] for n in BIG]
    x_l, mem_l = x[0], mem[0]
    tabs = _rope_tables(x_l.shape[0])

    gathered = _all_gather(shards(0), "weights_all_gather")
    layers, saved = [], []
    for l in range(dp):
        g_in, g_kv, g_br, g_out = gathered
        layers.append(_layer_weights(l, g_in.reshape(IN_WIDTH, -1), g_kv.reshape(D_MODEL, -1), _assemble_w_br(g_br),
                                     g_out.reshape(D_MODEL, -1), wts))
        x_l, s, gathered = _layer_fwd(x_l, mem_l, layers[l], tabs, next_shards=shards(l + 1) if l + 1 < dp else None)
        saved.append(s)
    loss_local, dx, d_final_g = _loss_head(x_l, final_g.reshape(1, -1), loss_target[0])
    loss = lax.psum(loss_local[0, 0], AXES)

    def finish(l, parts, from_chips, prev):
        return {n: _shard_sum_adamw(p, f, chip, wts[n], mom1[n], mom2[n], l, None if prev is None else prev[n],
                                    "sum_adamw_" + n)
                for n, p, f in zip(BIG, parts, from_chips)}

    grads, updated, pending = [None] * dp, None, None
    for l in reversed(range(dp)):
        dx, grads[l], scattered, own = _layer_bwd(dx, mem_l, layers[l], saved[l], tabs, pending=pending, core=core,
                                                  scatter_own=(l == 0))
        if scattered is not None:
            updated = finish(l + 1, *scattered, updated)
        pending = [grads[l][n] for n in BIG]
    updated = finish(0, *own, updated)
    grad_x = dx
    big_out = [{n: (t_in(updated[n][k]) if n == "w_in" else updated[n][k]) for n in BIG} for k in range(4)]

    small_g = {n: jnp.stack([g[n] for g in grads]) for n in SMALL if n != "final_g"}
    small_g["final_g"] = d_final_g
    (all_small,) = _all_gather([_pack([small_g[n] for n in SMALL])], "small_all_gather")
    small_bufs = _sum_and_adamw(
        all_small, _pack([wts[n] for n in SMALL]), _pack([mom1[n] for n in SMALL]), _pack([mom2[n] for n in SMALL]),
        "small_sum_adamw")

    outs = []
    for big_vals, small_buf in zip(big_out, small_bufs):
        vals = dict(big_vals)
        vals.update(zip(SMALL, _unpack(small_buf, [wts[n].shape for n in SMALL])))
        outs.append(vals)
    order = ("norm_g", "w_in", "q_norm_g", "k_norm_g", "sg_ln_g", "sg_ln_b", "w_s", "b_s", "mem_norm_g", "w_mem_kv",
             "w_br", "w_out", "final_g")
    result = [loss, grad_x[None]]
    for vals in outs:
        result += [vals[n] for n in order]
    return tuple(result)
```

```python
import functools

import jax
import jax.numpy as jnp
from jax import lax
from jax.experimental import pallas as pl
from jax.experimental.pallas import tpu as pltpu

F32 = jnp.float32
BF16 = jnp.bfloat16
SDS = jax.ShapeDtypeStruct
MESH_ID = pl.DeviceIdType.MESH
AXES = ("x", "y", "c")
N_DEV = 8

D_MODEL = 1024
DEPTH = 4
GRID_W = 64
CHUNK = 128
ROPE_THETA = 10000.0
EPS = 1e-6
HEAD_DIM = 64
A_HEADS = 8
A_KV_HEADS = 2
B_GROUPS = 4
M_HEADS = 4
M_HEAD_DIM = 128
BW = 512
W_A = 1280
W_B = 1536
W_M = 1024
W_G = 3072
IN_WIDTH = W_A + W_B + W_M + W_G
GROUP_OFFS = (0, W_A, W_A + W_B, W_A + W_B + W_M, IN_WIDTH)

ADAM_LR = 0.001
ADAM_B1 = 0.9
ADAM_B2 = 0.999
ADAM_EPS = 1e-08
ADAM_WD = 0.01
ADAM_STEP = 10

LANES = 128
KEY_CHUNK = 2048
VMEM_LIMIT = 52 * 1024 * 1024


def _tile(n, cap, unit=LANES):
    if n <= cap:
        return n
    t = (cap // unit) * unit
    while n % t:
        t -= unit
    return t


def _params(*sem):
    return pltpu.CompilerParams(dimension_semantics=sem, vmem_limit_bytes=VMEM_LIMIT)


def _sigmoid(z):
    return 0.5 * jnp.tanh(0.5 * z) + 0.5


def _silu_and_grad(z):
    s = _sigmoid(z)
    return z * s, s * (1.0 + z * (1.0 - s))


def _dot(a, b, dims):
    return lax.dot_general(a, b, (dims, ((), ())), preferred_element_type=F32)


NN = ((1,), (0,))
NT = ((1,), (1,))
TN = ((0,), (0,))
HBM = pl.BlockSpec(memory_space=pl.ANY)


class _Exchange:
    def __init__(self, ins, out_shapes, sems, make, aliases=None):
        self.ins, self.out_shapes, self.sems, self.make = list(ins), list(out_shapes), list(sems), make
        self.aliases = dict(aliases or {})

    def start(self, in_refs, out_refs, sems):
        for cp in self.make(in_refs, out_refs, sems):
            cp.start()

    def finish(self, in_refs, out_refs, sems):
        for cp in self.make(in_refs, out_refs, sems):
            cp.wait()


def _both(a, b):
    if a is None or b is None:
        return a if b is None else b
    n_in, n_out, n_sem = len(a.ins), len(a.out_shapes), len(a.sems)

    def make(in_refs, out_refs, sems):
        return (a.make(in_refs[:n_in], out_refs[:n_out], sems[:n_sem])
                + b.make(in_refs[n_in:], out_refs[n_out:], sems[n_sem:]))

    aliases = dict(a.aliases)
    aliases.update({n_in + i: n_out + o for i, o in b.aliases.items()})
    return _Exchange(a.ins + b.ins, a.out_shapes + b.out_shapes, a.sems + b.sems, make, aliases)


def _host_call(body, ins, in_specs, out_specs, out_shape, *, name, grid, semantics, scratch=(), exchange=None):
    ins, out_shape, scratch = list(ins), list(out_shape), list(scratch)
    if exchange is None:
        res = pl.pallas_call(
            body, name=name, grid=grid, in_specs=list(in_specs), out_specs=tuple(out_specs), out_shape=tuple(out_shape),
            scratch_shapes=scratch, compiler_params=_params(*semantics))(*ins)
        return tuple(res), ()
    n_in, n_out, n_scr = len(ins), len(out_shape), len(scratch)
    x_in, x_out = len(exchange.ins), len(exchange.out_shapes)

    def carrying(*refs):
        o0 = n_in + x_in
        s0 = o0 + n_out + x_out
        c_in, c_out, c_sems = refs[n_in:o0], refs[o0 + n_out:s0], refs[s0 + n_scr:]
        ids = [pl.program_id(a) for a in range(len(grid))]
        first = functools.reduce(jnp.logical_and, [i == 0 for i in ids])
        last = functools.reduce(jnp.logical_and, [i == g - 1 for i, g in zip(ids, grid)])

        @pl.when(first)
        def _():
            exchange.start(c_in, c_out, c_sems)

        body(*refs[:n_in], *refs[o0:o0 + n_out], *refs[s0:s0 + n_scr])

        @pl.when(last)
        def _():
            exchange.finish(c_in, c_out, c_sems)

    res = pl.pallas_call(
        carrying, name=name, grid=grid, in_specs=list(in_specs) + [HBM] * x_in,
        out_specs=tuple(out_specs) + (HBM,) * x_out, out_shape=tuple(out_shape) + tuple(exchange.out_shapes),
        scratch_shapes=scratch + exchange.sems,
        input_output_aliases={n_in + i: n_out + o for i, o in exchange.aliases.items()},
        compiler_params=_params(*(["arbitrary"] * len(grid))))(*ins, *exchange.ins)
    return tuple(res[:n_out]), tuple(res[n_out:])


def _exchange_call(exchange, name):
    x_in = len(exchange.ins)

    def body(*refs):
        x_out = len(exchange.out_shapes)
        c_in, c_out, c_sems = refs[:x_in], refs[x_in:x_in + x_out], refs[x_in + x_out:]
        exchange.start(c_in, c_out, c_sems)
        exchange.finish(c_in, c_out, c_sems)

    return pl.pallas_call(
        body, name=name, in_specs=[HBM] * x_in, out_specs=tuple([HBM] * len(exchange.out_shapes)),
        out_shape=tuple(exchange.out_shapes), scratch_shapes=exchange.sems, input_output_aliases=exchange.aliases,
    )(*exchange.ins)


def _core_major(d):
    return (d % 2) * 4 + d // 2


def _mm(a, b, mode, name, add=None, out_dtype=F32, tm_cap=512, tn_cap=1536, tk_cap=1536):
    if mode == "nn":
        (m, k), (k2, n) = a.shape, b.shape
    elif mode == "nt":
        (m, k), (n, k2) = a.shape, b.shape
    else:
        (k, m), (k2, n) = a.shape, b.shape
    assert k == k2, (a.shape, b.shape, mode)
    tm = _tile(m, tm_cap, 8 if mode != "tn" else LANES)
    tn = _tile(n, tn_cap)
    tk = _tile(k, tk_cap, LANES if mode != "tn" else 16)
    nk = k // tk
    dims = {"nn": NN, "nt": NT, "tn": TN}[mode]
    a_spec = {"nn": pl.BlockSpec((tm, tk), lambda i, j, kk: (i, kk)),
              "nt": pl.BlockSpec((tm, tk), lambda i, j, kk: (i, kk)),
              "tn": pl.BlockSpec((tk, tm), lambda i, j, kk: (kk, i))}[mode]
    b_spec = {"nn": pl.BlockSpec((tk, tn), lambda i, j, kk: (kk, j)),
              "nt": pl.BlockSpec((tn, tk), lambda i, j, kk: (j, kk)),
              "tn": pl.BlockSpec((tk, tn), lambda i, j, kk: (kk, j))}[mode]
    o_spec = pl.BlockSpec((tm, tn), lambda i, j, kk: (i, j))
    has_add = add is not None

    def body(*refs):
        a_ref, b_ref = refs[0], refs[1]
        add_ref = refs[2] if has_add else None
        o_ref = refs[3] if has_add else refs[2]
        part = _dot(a_ref[...].astype(BF16), b_ref[...].astype(BF16), dims)
        if nk == 1:
            if has_add:
                part = part + add_ref[...]
            o_ref[...] = part.astype(out_dtype)
        else:
            acc = refs[-1]
            kk = pl.program_id(2)

            @pl.when(kk == 0)
            def _():
                acc[...] = part + add_ref[...] if has_add else part

            @pl.when(kk > 0)
            def _():
                acc[...] += part

            @pl.when(kk == nk - 1)
            def _():
                o_ref[...] = acc[...].astype(out_dtype)

    ins = [a, b] + ([add] if has_add else [])
    in_specs = [a_spec, b_spec] + ([o_spec] if has_add else [])
    return pl.pallas_call(
        body, name=name, grid=(m // tm, n // tn, nk), in_specs=in_specs, out_specs=o_spec,
        out_shape=SDS((m, n), out_dtype),
        scratch_shapes=[pltpu.VMEM((tm, tn), F32)] if nk > 1 else [],
        compiler_params=_params("parallel", "parallel", "arbitrary"),
    )(*ins)


def _mm_tn_by_owner(a, b, name):
    (k, m), (k2, n) = a.shape, b.shape
    assert k == k2 and m % N_DEV == 0, (a.shape, b.shape)
    tk = _tile(k, 2048, 16)
    nk = k // tk
    rows = m // N_DEV

    def body(a_ref, b_ref, o_ref, acc):
        kk = pl.program_id(0)
        part = _dot(a_ref[...].astype(BF16), b_ref[...].astype(BF16), TN)

        @pl.when(kk == 0)
        def _():
            acc[...] = part

        @pl.when(kk > 0)
        def _():
            acc[...] += part

        @pl.when(kk == nk - 1)
        def _():
            for d in range(N_DEV):
                pos = _core_major(d)
                o_ref[pos * rows:(pos + 1) * rows, :] = acc[d * rows:(d + 1) * rows, :].astype(BF16)

    return pl.pallas_call(
        body, name=name, grid=(nk,),
        in_specs=[pl.BlockSpec((tk, m), lambda kk: (kk, 0)), pl.BlockSpec((tk, n), lambda kk: (kk, 0))],
        out_specs=pl.BlockSpec((m, n), lambda kk: (0, 0)), out_shape=SDS((m, n), BF16),
        scratch_shapes=[pltpu.VMEM((m, n), F32)], compiler_params=_params("arbitrary"),
    )(a, b)


def _rmsnorm_fwd(x, g, name):
    t, d = x.shape
    tm = _tile(t, 512, 8)

    def body(x_ref, g_ref, h_ref):
        xf = x_ref[...]
        r = lax.rsqrt(jnp.mean(xf * xf, axis=-1, keepdims=True) + EPS)
        h_ref[...] = (xf * r * g_ref[...]).astype(BF16)

    return pl.pallas_call(
        body, name=name, grid=(t // tm,),
        in_specs=[pl.BlockSpec((tm, d), lambda i: (i, 0)), pl.BlockSpec((1, d), lambda i: (0, 0))],
        out_specs=pl.BlockSpec((tm, d), lambda i: (i, 0)),
        out_shape=SDS((t, d), BF16), compiler_params=_params("parallel"),
    )(x, g)


def _rmsnorm_bwd_math(xf, g, dh):
    r = lax.rsqrt(jnp.mean(xf * xf, axis=-1, keepdims=True) + EPS)
    xh = xf * r
    gd = dh * g
    dx = r * (gd - xh * jnp.mean(gd * xh, axis=-1, keepdims=True))
    return dx, jnp.sum(dh * xh, axis=0, keepdims=True)


def _rmsnorm_dg(x, g, dh, name):
    t, d = x.shape
    tm = _tile(t, 512, 8)

    def body(x_ref, g_ref, dh_ref, dg_ref):
        @pl.when(pl.program_id(0) == 0)
        def _():
            dg_ref[...] = jnp.zeros_like(dg_ref)

        dg_ref[...] += _rmsnorm_bwd_math(x_ref[...], g_ref[...], dh_ref[...])[1]

    row = pl.BlockSpec((tm, d), lambda i: (i, 0))
    vec = pl.BlockSpec((1, d), lambda i: (0, 0))
    return pl.pallas_call(
        body, name=name, grid=(t // tm,), in_specs=[row, vec, row], out_specs=vec, out_shape=SDS((1, d), F32),
        compiler_params=_params("arbitrary"),
    )(x, g, dh)


def _group_rows(p):
    return slice(GROUP_OFFS[p], GROUP_OFFS[p + 1])


def _norm_in_proj(x, g, w_t):
    t, d = x.shape
    tm = _tile(t, 512, 16)
    widths = [GROUP_OFFS[p + 1] - GROUP_OFFS[p] for p in range(4)]

    def body(x_ref, g_ref, w_ref, h_ref, *outs):
        xf = x_ref[...]
        r = lax.rsqrt(jnp.mean(xf * xf, axis=-1, keepdims=True) + EPS)
        hb = (xf * r * g_ref[...]).astype(BF16)
        h_ref[...] = hb
        for p, o_ref in enumerate(outs):
            o_ref[...] = _dot(hb, w_ref[_group_rows(p), :], NT).astype(BF16)

    row = lambda wd: pl.BlockSpec((tm, wd), lambda i: (i, 0))
    return pl.pallas_call(
        body, name="norm_in_proj", grid=(t // tm,),
        in_specs=[row(d), pl.BlockSpec((1, d), lambda i: (0, 0)),
                  pl.BlockSpec(w_t.shape, lambda i: (0, 0), pipeline_mode=pl.Buffered(1))],
        out_specs=(row(d),) + tuple(row(wd) for wd in widths),
        out_shape=(SDS((t, d), BF16),) + tuple(SDS((t, wd), BF16) for wd in widths),
        compiler_params=_params("parallel"),
    )(x, g, w_t)


def _dh_rmsnorm_bwd(dps, w_t, x, g, dres, exchange=None):
    t, d = x.shape
    tm = _tile(t, 256, 16)
    n = len(dps)

    def body(*refs):
        dp_refs, w_ref = refs[:n], refs[n]
        x_ref, g_ref, dres_ref, dx_ref, dg_ref = refs[n + 1:]
        dh = None
        for p, dp_ref in enumerate(dp_refs):
            part = _dot(dp_ref[...], w_ref[_group_rows(p), :], NN)
            dh = part if dh is None else dh + part
        dx, dg = _rmsnorm_bwd_math(x_ref[...], g_ref[...], dh)
        dx_ref[...] = dx + dres_ref[...]

        @pl.when(pl.program_id(0) == 0)
        def _():
            dg_ref[...] = jnp.zeros_like(dg_ref)

        dg_ref[...] += dg

    row = pl.BlockSpec((tm, d), lambda i: (i, 0))
    vec = pl.BlockSpec((1, d), lambda i: (0, 0))
    return _host_call(
        body, list(dps) + [w_t, x, g, dres],
        [pl.BlockSpec((tm, a.shape[1]), lambda i: (i, 0)) for a in dps]
        + [pl.BlockSpec(w_t.shape, lambda i: (0, 0), pipeline_mode=pl.Buffered(1)), row, vec, row],
        (row, vec), (SDS((t, d), F32), SDS((1, d), F32)),
        name="dh_rmsnorm_bwd", grid=(t // tm,), semantics=("arbitrary",), exchange=exchange)


def _rope_tables(t):
    rows = t // GRID_W
    row = jnp.repeat(jnp.arange(rows, dtype=F32), GRID_W)
    col = jnp.tile(jnp.arange(GRID_W, dtype=F32), rows)
    n_freq = HEAD_DIM // 4
    inv = ROPE_THETA ** (-jnp.arange(n_freq, dtype=F32) / n_freq)
    ang = jnp.stack([row[:, None] * inv, col[:, None] * inv], axis=1)
    cos, sin = jnp.cos(ang), jnp.sin(ang)
    c64 = jnp.concatenate([cos[:, 0], cos[:, 0], cos[:, 1], cos[:, 1]], axis=-1)
    s64 = jnp.concatenate([-sin[:, 0], sin[:, 0], -sin[:, 1], sin[:, 1]], axis=-1)
    return jnp.tile(c64, (1, 2)), jnp.tile(s64, (1, 2))


def _head_sums(v, lane):
    lo = jnp.sum(jnp.where(lane < HEAD_DIM, v, 0.0), axis=-1, keepdims=True)
    hi = jnp.sum(jnp.where(lane < HEAD_DIM, 0.0, v), axis=-1, keepdims=True)
    return jnp.where(lane < HEAD_DIM, lo, hi)


def _swap16(v, lane):
    return jnp.where((lane % 32) < 16, pltpu.roll(v, LANES - 16, 1), pltpu.roll(v, 16, 1))


def _attn_prep_fwd(pa, cs, sn, qg2, kg2):
    t = pa.shape[0]
    tq = _tile(t, 512, LANES)
    scale = HEAD_DIM ** -0.5

    def body(pa_ref, cs_ref, sn_ref, qg_ref, kg_ref, q_ref, qT_ref, k_ref, kT_ref, vT_ref):
        lane = lax.broadcasted_iota(jnp.int32, (tq, LANES), 1)
        c, s = cs_ref[...], sn_ref[...]

        def norm_rope(xg, g2):
            r = lax.rsqrt(_head_sums(xg * xg, lane) * (1.0 / HEAD_DIM) + EPS)
            xn = xg * r * g2
            return xn * c + _swap16(xn, lane) * s

        for gi in range(4):
            sl = slice(gi * LANES, (gi + 1) * LANES)
            qr = norm_rope(pa_ref[:, sl].astype(F32), qg_ref[...]) * scale
            q_ref[:, sl] = qr.astype(BF16)
            qT_ref[sl, :] = qr.T.astype(BF16)
        kr = norm_rope(pa_ref[:, 512:640].astype(F32), kg_ref[...])
        kT_ref[...] = kr.T.astype(BF16)
        vT_ref[...] = pa_ref[:, 640:768].astype(F32).T.astype(BF16)
        kr = kr.astype(BF16)
        for kvh in range(A_KV_HEADS):
            k_ref[kvh] = kr[:, kvh * HEAD_DIM:(kvh + 1) * HEAD_DIM]

    row = lambda w: pl.BlockSpec((tq, w), lambda i: (i, 0))
    col = lambda r: pl.BlockSpec((r, tq), lambda i: (0, i))
    vec = pl.BlockSpec((1, LANES), lambda i: (0, 0))
    hm = pl.BlockSpec((A_KV_HEADS, tq, HEAD_DIM), lambda i: (0, i, 0))
    return pl.pallas_call(
        body, name="attn_prep_fwd", grid=(t // tq,),
        in_specs=[row(W_A), row(LANES), row(LANES), vec, vec],
        out_specs=(row(BW), col(BW), hm, col(LANES), col(LANES)),
        out_shape=(SDS((t, BW), BF16), SDS((BW, t), BF16), SDS((A_KV_HEADS, t, HEAD_DIM), BF16),
                   SDS((LANES, t), BF16), SDS((LANES, t), BF16)),
        compiler_params=_params("parallel"),
    )(pa, cs, sn, qg2, kg2)


def _attn_fwd(qT, k, vT, pa, exchange=None):
    t = qT.shape[1]
    tq = _tile(t, 256, LANES)
    grp = A_HEADS // A_KV_HEADS

    ck = _tile(t, KEY_CHUNK, LANES)

    def body(qT_ref, k_ref, vT_ref, pa_ref, o_ref, lse_ref, y_ref):
        def finish_pair(pair, o_pair):
            o2 = jnp.concatenate(o_pair, axis=0).T
            cols = slice(pair * LANES, (pair + 1) * LANES)
            o_ref[:, cols] = o2
            z = pa_ref[:, 768 + pair * LANES:768 + (pair + 1) * LANES].astype(F32)
            y_ref[:, cols] = (o2 * (z * _sigmoid(z))).astype(BF16)

        n_c = t // ck
        items = [(h, c) for h in range(A_HEADS) for c in range(n_c)]

        def scores(h, c):
            return _dot(k_ref[h // grp, c * ck:(c + 1) * ck, :], qT_ref[h * HEAD_DIM:(h + 1) * HEAD_DIM, :], NN)

        def weighted_values(h, c, pT):
            kvh = h // grp
            return _dot(vT_ref[kvh * HEAD_DIM:(kvh + 1) * HEAD_DIM, c * ck:(c + 1) * ck], pT, NN)

        o_pair, sT, pending, acc = [], scores(*items[0]), None, None
        for i in range(len(items) + 1):
            item = items[i] if i < len(items) else None
            sT_next = scores(*items[i + 1]) if i + 1 < len(items) else None
            if pending is not None:
                (ph, pc, pT, alpha, l_run) = pending
                o_c = weighted_values(ph, pc, pT)
                acc = o_c if alpha is None else acc * alpha + o_c
                if pc == n_c - 1:
                    o_pair.append(acc / l_run)
                    if ph % 2:
                        finish_pair(ph // 2, o_pair)
                        o_pair = []
                pending = None
            if item is not None:
                h, c = item
                m_c = jnp.max(sT, axis=0, keepdims=True)
                if c == 0:
                    m_new, alpha = m_c, None
                else:
                    m_new = jnp.maximum(m, m_c)
                    alpha = jnp.exp(m - m_new)
                pT = jnp.exp(sT - m_new)
                l_c = jnp.sum(pT, axis=0, keepdims=True)
                l = l_c if c == 0 else l * alpha + l_c
                m = m_new
                if c == n_c - 1:
                    lse_ref[h:h + 1, :] = m + jnp.log(l)
                pending = (h, c, pT.astype(BF16), alpha, l)
            sT = sT_next

    row = lambda w: pl.BlockSpec((tq, w), lambda i: (i, 0))
    return _host_call(
        body, (qT, k, vT, pa),
        [pl.BlockSpec((BW, tq), lambda i: (0, i)), pl.BlockSpec((A_KV_HEADS, t, HEAD_DIM), lambda i: (0, 0, 0)),
         pl.BlockSpec((A_KV_HEADS * HEAD_DIM, t), lambda i: (0, 0)), row(W_A)],
        (row(BW), pl.BlockSpec((A_HEADS, tq), lambda i: (0, i)), row(BW)),
        (SDS((t, BW), F32), SDS((A_HEADS, t), F32), SDS((t, BW), BF16)),
        name="attn_fwd", grid=(t // tq,), semantics=("parallel",), exchange=exchange)


def _attn_bwd(q, qT, k, kT, vT, pa, o, lse, dy, exchange=None):
    t = q.shape[0]
    tq = _tile(t, 256, LANES)
    grp = A_HEADS // A_KV_HEADS
    gw = grp * HEAD_DIM

    def body(q_ref, qT_ref, k_ref, kT_ref, vT_ref, z_ref, o_ref, lse_ref, dy_ref, dq_ref, dkT_ref, dvT_ref):
        @pl.when(pl.program_id(1) == 0)
        def _():
            dkT_ref[...] = jnp.zeros_like(dkT_ref)
            dvT_ref[...] = jnp.zeros_like(dvT_ref)

        z = z_ref[...].astype(F32)
        do = dy_ref[...] * (z * _sigmoid(z))
        doo = do * o_ref[...]
        doT = do.T
        kk, kT, vT = k_ref[...], kT_ref[...], vT_ref[...]
        heads = [slice(j * HEAD_DIM, (j + 1) * HEAD_DIM) for j in range(grp)]

        def scores(hs):
            return _dot(q_ref[:, hs], kT, NN), _dot(do[:, hs].astype(BF16), vT, NN)

        s, dp = scores(heads[0])
        for j, hs in enumerate(heads):
            nxt = scores(heads[j + 1]) if j + 1 < grp else None
            delta = jnp.sum(doo[:, hs], axis=-1, keepdims=True)
            p = jnp.exp(s - lse_ref[:, j:j + 1])
            ds_b = (p * (dp - delta)).astype(BF16)
            dq_ref[:, hs] = _dot(ds_b, kk, NN)
            dkT_ref[...] += _dot(qT_ref[hs, :], ds_b, NN)
            dvT_ref[...] += _dot(doT[hs, :].astype(BF16), p.astype(BF16), NN)
            if nxt is not None:
                s, dp = nxt

    grp_blk = pl.BlockSpec((tq, gw), lambda g, i: (i, g))
    kvT_blk = pl.BlockSpec((HEAD_DIM, t), lambda g, i: (g, 0))
    acc_blk = pl.BlockSpec((None, HEAD_DIM, t), lambda g, i: (g, 0, 0))
    return _host_call(
        body, (q, qT, k, kT, vT, pa, o, lse, dy),
        [grp_blk, pl.BlockSpec((gw, tq), lambda g, i: (g, i)), pl.BlockSpec((None, t, HEAD_DIM), lambda g, i: (g, 0, 0)),
         kvT_blk, kvT_blk, pl.BlockSpec((tq, gw), lambda g, i: (i, 768 // gw + g)), grp_blk,
         pl.BlockSpec((None, tq, grp), lambda g, i: (g, i, 0)), grp_blk],
        (grp_blk, acc_blk, acc_blk),
        (SDS((t, BW), F32), SDS((A_KV_HEADS, HEAD_DIM, t), F32), SDS((A_KV_HEADS, HEAD_DIM, t), F32)),
        name="attn_bwd", grid=(A_KV_HEADS, t // tq), semantics=("arbitrary", "arbitrary"), exchange=exchange)


def _attn_prep_bwd(pa, cs, sn, qg2, kg2, dq, dkT, dvT, dy, o, exchange=None):
    t = pa.shape[0]
    tq = _tile(t, 512, LANES)
    scale = HEAD_DIM ** -0.5

    def body(pa_ref, cs_ref, sn_ref, qg_ref, kg_ref, dq_ref, dkT_ref, dvT_ref, dy_ref, o_ref, dpa_ref, dqg_ref, dkg_ref):
        lane = lax.broadcasted_iota(jnp.int32, (tq, LANES), 1)
        c, s = cs_ref[...], sn_ref[...]

        @pl.when(pl.program_id(0) == 0)
        def _():
            dqg_ref[...] = jnp.zeros_like(dqg_ref)
            dkg_ref[...] = jnp.zeros_like(dkg_ref)

        def norm_rope_bwd(xg, g2, dout):
            r = lax.rsqrt(_head_sums(xg * xg, lane) * (1.0 / HEAD_DIM) + EPS)
            xh = xg * r
            dxn = dout * c + _swap16(dout * s, lane)
            gd = dxn * g2
            dx = r * (gd - xh * (_head_sums(gd * xh, lane) * (1.0 / HEAD_DIM)))
            return dx, jnp.sum(dxn * xh, axis=0, keepdims=True)

        for gi in range(4):
            sl = slice(gi * LANES, (gi + 1) * LANES)
            dx, dg = norm_rope_bwd(pa_ref[:, sl].astype(F32), qg_ref[...], dq_ref[:, sl] * scale)
            dpa_ref[:, sl] = dx.astype(BF16)
            dqg_ref[...] += dg
        dx, dg = norm_rope_bwd(pa_ref[:, 512:640].astype(F32), kg_ref[...], dkT_ref[...].T)
        dpa_ref[:, 512:640] = dx.astype(BF16)
        dkg_ref[...] += dg
        dpa_ref[:, 640:768] = dvT_ref[...].T.astype(BF16)
        z = pa_ref[:, 768:1280].astype(F32)
        _, dsilu = _silu_and_grad(z)
        dpa_ref[:, 768:1280] = (dy_ref[...] * o_ref[...] * dsilu).astype(BF16)

    row = lambda w: pl.BlockSpec((tq, w), lambda i: (i, 0))
    col = pl.BlockSpec((LANES, tq), lambda i: (0, i))
    vec = pl.BlockSpec((1, LANES), lambda i: (0, 0))
    return _host_call(
        body, (pa, cs, sn, qg2, kg2, dq, dkT, dvT, dy, o),
        [row(W_A), row(LANES), row(LANES), vec, vec, row(BW), col, col, row(BW), row(BW)],
        (row(W_A), vec, vec), (SDS((t, W_A), BF16), SDS((1, LANES), F32), SDS((1, LANES), F32)),
        name="attn_prep_bwd", grid=(t // tq,), semantics=("arbitrary",), exchange=exchange)


def _layer_norm(v, g, b):
    mu = jnp.mean(v, axis=-1, keepdims=True)
    xc = v - mu
    rs = lax.rsqrt(jnp.mean(xc * xc, axis=-1, keepdims=True) + EPS)
    xh = xc * rs
    return xh * g + b, xh, rs


def _gmlp_fwd(pb, lng, lnb, ws, bs):
    t = pb.shape[0]
    tb = _tile(t, 512, CHUNK)

    def body(pb_ref, g_ref, b_ref, ws_ref, bs_ref, y_ref):
        vln, _, _ = _layer_norm(pb_ref[:, BW:2 * BW].astype(F32), g_ref[...], b_ref[...])
        vb = vln.astype(BF16)
        for gi in range(B_GROUPS):
            w = ws_ref[gi].astype(BF16)
            cs_ = slice(gi * CHUNK, (gi + 1) * CHUNK)
            for n in range(tb // CHUNK):
                rs_ = slice(n * CHUNK, (n + 1) * CHUNK)
                mixed = _dot(w, vb[rs_, cs_], NN) + bs_ref[gi]
                z = pb_ref[rs_, 2 * BW + gi * CHUNK:2 * BW + (gi + 1) * CHUNK].astype(F32)
                y_ref[rs_, cs_] = (pb_ref[rs_, cs_].astype(F32) * mixed * (z * _sigmoid(z))).astype(BF16)

    return pl.pallas_call(
        body, name="gmlp_fwd", grid=(t // tb,),
        in_specs=[pl.BlockSpec((tb, W_B), lambda i: (i, 0)),
                  pl.BlockSpec((1, BW), lambda i: (0, 0)), pl.BlockSpec((1, BW), lambda i: (0, 0)),
                  pl.BlockSpec((B_GROUPS, CHUNK, CHUNK), lambda i: (0, 0, 0)),
                  pl.BlockSpec((B_GROUPS, CHUNK, 1), lambda i: (0, 0, 0))],
        out_specs=pl.BlockSpec((tb, BW), lambda i: (i, 0)),
        out_shape=SDS((t, BW), BF16), compiler_params=_params("parallel"),
    )(pb, lng, lnb, ws, bs)


def _gmlp_bwd(pb, lng, lnb, ws, bs, dy):
    t = pb.shape[0]
    tb = _tile(t, 256, CHUNK)

    def body(pb_ref, g_ref, b_ref, ws_ref, bs_ref, dy_ref, dpb_ref, dws_ref, dbs_ref, dg_ref, db_ref, dvln_ref):
        @pl.when(pl.program_id(0) == 0)
        def _():
            dws_ref[...] = jnp.zeros_like(dws_ref)
            dbs_ref[...] = jnp.zeros_like(dbs_ref)
            dg_ref[...] = jnp.zeros_like(dg_ref)
            db_ref[...] = jnp.zeros_like(db_ref)

        vln, xh, rs = _layer_norm(pb_ref[:, BW:2 * BW].astype(F32), g_ref[...], b_ref[...])
        vb = vln.astype(BF16)
        for gi in range(B_GROUPS):
            w = ws_ref[gi].astype(BF16)
            cs_ = slice(gi * CHUNK, (gi + 1) * CHUNK)
            for n in range(tb // CHUNK):
                rs_ = slice(n * CHUNK, (n + 1) * CHUNK)
                vbc = vb[rs_, cs_]
                mixed = _dot(w, vbc, NN) + bs_ref[gi]
                zs = slice(2 * BW + gi * CHUNK, 2 * BW + (gi + 1) * CHUNK)
                z = pb_ref[rs_, zs].astype(F32)
                u = pb_ref[rs_, cs_].astype(F32)
                sil, dsil = _silu_and_grad(z)
                dyc = dy_ref[rs_, cs_]
                dmixed = dyc * u * sil
                dpb_ref[rs_, cs_] = (dyc * mixed * sil).astype(BF16)
                dpb_ref[rs_, zs] = (dyc * u * mixed * dsil).astype(BF16)
                dmb = dmixed.astype(BF16)
                dws_ref[gi] += _dot(dmb, vbc, NT)
                dbs_ref[gi] += jnp.sum(dmixed, axis=-1, keepdims=True)
                dvln_ref[rs_, cs_] = _dot(w, dmb, TN)
        dvln = dvln_ref[...]
        dg_ref[...] += jnp.sum(dvln * xh, axis=0, keepdims=True)
        db_ref[...] += jnp.sum(dvln, axis=0, keepdims=True)
        gd = dvln * g_ref[...]
        dv = rs * (gd - jnp.mean(gd, axis=-1, keepdims=True) - xh * jnp.mean(gd * xh, axis=-1, keepdims=True))
        dpb_ref[:, BW:2 * BW] = dv.astype(BF16)

    vec = pl.BlockSpec((1, BW), lambda i: (0, 0))
    wsb = pl.BlockSpec((B_GROUPS, CHUNK, CHUNK), lambda i: (0, 0, 0))
    bsb = pl.BlockSpec((B_GROUPS, CHUNK, 1), lambda i: (0, 0, 0))
    return pl.pallas_call(
        body, name="gmlp_bwd", grid=(t // tb,),
        in_specs=[pl.BlockSpec((tb, W_B), lambda i: (i, 0)), vec, vec, wsb, bsb, pl.BlockSpec((tb, BW), lambda i: (i, 0))],
        out_specs=(pl.BlockSpec((tb, W_B), lambda i: (i, 0)), wsb, bsb, vec, vec),
        out_shape=(SDS((t, W_B), BF16), SDS((B_GROUPS, CHUNK, CHUNK), F32), SDS((B_GROUPS, CHUNK, 1), F32),
                   SDS((1, BW), F32), SDS((1, BW), F32)),
        scratch_shapes=[pltpu.VMEM((tb, BW), F32)],
        compiler_params=_params("arbitrary"),
    )(pb, lng, lnb, ws, bs, dy)


def _mem_softmax(qh, kh):
    s = _dot(qh, kh, NT) * (M_HEAD_DIM ** -0.5)
    e = jnp.exp(s - jnp.max(s, axis=-1, keepdims=True))
    return e / jnp.sum(e, axis=-1, keepdims=True)


def _mem_attn_fwd(pm, kv):
    t = pm.shape[0]
    tq = _tile(t, 512, 8)
    ml = kv.shape[0]

    def body(pm_ref, kv_ref, y_ref):
        for h in range(M_HEADS):
            hs = slice(h * M_HEAD_DIM, (h + 1) * M_HEAD_DIM)
            kh = kv_ref[:, hs].astype(BF16)
            vh = kv_ref[:, BW + h * M_HEAD_DIM:BW + (h + 1) * M_HEAD_DIM].astype(BF16)
            p = _mem_softmax(pm_ref[:, hs].astype(BF16), kh)
            o = _dot(p.astype(BF16), vh, NN)
            z = pm_ref[:, BW + h * M_HEAD_DIM:BW + (h + 1) * M_HEAD_DIM].astype(F32)
            y_ref[:, hs] = (o * (z * _sigmoid(z))).astype(BF16)

    return pl.pallas_call(
        body, name="mem_attn_fwd", grid=(t // tq,),
        in_specs=[pl.BlockSpec((tq, W_M), lambda i: (i, 0)), pl.BlockSpec((ml, 2 * BW), lambda i: (0, 0))],
        out_specs=pl.BlockSpec((tq, BW), lambda i: (i, 0)),
        out_shape=SDS((t, BW), BF16), compiler_params=_params("parallel"),
    )(pm, kv)


def _mem_attn_bwd(pm, kv, dy):
    t = pm.shape[0]
    tq = _tile(t, 512, 8)
    ml = kv.shape[0]
    scale = M_HEAD_DIM ** -0.5

    def body(pm_ref, kv_ref, dy_ref, dpm_ref, dkv_ref):
        @pl.when(pl.program_id(0) == 0)
        def _():
            dkv_ref[...] = jnp.zeros_like(dkv_ref)

        for h in range(M_HEADS):
            hs = slice(h * M_HEAD_DIM, (h + 1) * M_HEAD_DIM)
            zs = slice(BW + h * M_HEAD_DIM, BW + (h + 1) * M_HEAD_DIM)
            kh = kv_ref[:, hs].astype(BF16)
            vh = kv_ref[:, zs].astype(BF16)
            qh = pm_ref[:, hs].astype(BF16)
            p = _mem_softmax(qh, kh)
            pb = p.astype(BF16)
            o = _dot(pb, vh, NN)
            sil, dsil = _silu_and_grad(pm_ref[:, zs].astype(F32))
            dyh = dy_ref[:, hs]
            do = dyh * sil
            dpm_ref[:, zs] = (dyh * o * dsil).astype(BF16)
            delta = jnp.sum(do * o, axis=-1, keepdims=True)
            do_b = do.astype(BF16)
            dp = _dot(do_b, vh, NT)
            dr_b = (p * (dp - delta) * scale).astype(BF16)
            dpm_ref[:, hs] = _dot(dr_b, kh, NN).astype(BF16)
            dkv_ref[:, hs] += _dot(dr_b, qh, TN)
            dkv_ref[:, zs] += _dot(pb, do_b, TN)

    kvb = pl.BlockSpec((ml, 2 * BW), lambda i: (0, 0))
    return pl.pallas_call(
        body, name="mem_attn_bwd", grid=(t // tq,),
        in_specs=[pl.BlockSpec((tq, W_M), lambda i: (i, 0)), kvb, pl.BlockSpec((tq, BW), lambda i: (i, 0))],
        out_specs=(pl.BlockSpec((tq, W_M), lambda i: (i, 0)), kvb),
        out_shape=(SDS((t, W_M), BF16), SDS((ml, 2 * BW), F32)),
        compiler_params=_params("arbitrary"),
    )(pm, kv, dy)


def _merge_fwd(ya, yb, ym, pg, wbr, exchange=None):
    t = ya.shape[0]
    tm = _tile(t, 512, 8)

    def body(ya_ref, yb_ref, ym_ref, pg_ref, w_ref, m_ref):
        ups = [_dot(y_ref[...], w_ref[n], NN) for n, y_ref in enumerate((ya_ref, yb_ref, ym_ref))]
        acc = None
        for n, up in enumerate(ups):
            term = _sigmoid(pg_ref[:, n * D_MODEL:(n + 1) * D_MODEL].astype(F32)) * up
            acc = term if acc is None else acc + term
        m_ref[...] = acc.astype(BF16)

    yb_spec = pl.BlockSpec((tm, BW), lambda i: (i, 0))
    return _host_call(
        body, (ya, yb, ym, pg, wbr),
        [yb_spec, yb_spec, yb_spec, pl.BlockSpec((tm, W_G), lambda i: (i, 0)),
         pl.BlockSpec((3, BW, D_MODEL), lambda i: (0, 0, 0))],
        (pl.BlockSpec((tm, D_MODEL), lambda i: (i, 0)),), (SDS((t, D_MODEL), BF16),),
        name="merge_fwd", grid=(t // tm,), semantics=("parallel",), exchange=exchange)


def _merge_bwd(ya, yb, ym, pg, wbr, dm, exchange=None):
    t = ya.shape[0]
    tm = _tile(t, 512, 16)

    def body(ya_ref, yb_ref, ym_ref, pg_ref, w_ref, dm_ref, dya_ref, dyb_ref, dym_ref, dpg_ref, dw_ref):
        @pl.when(pl.program_id(0) == 0)
        def _():
            dw_ref[...] = jnp.zeros_like(dw_ref)

        dmf = dm_ref[...]
        branches = ((ya_ref, dya_ref), (yb_ref, dyb_ref), (ym_ref, dym_ref))

        def gate(n):
            gt = _sigmoid(pg_ref[:, n * D_MODEL:(n + 1) * D_MODEL].astype(F32))
            return gt, (dmf * gt).astype(BF16)

        gt, dup = gate(0)
        for n, (y_ref, dy_ref) in enumerate(branches):
            nxt = gate(n + 1) if n + 1 < len(branches) else None
            y, w = y_ref[...], w_ref[n]
            dy_ref[...] = _dot(dup, w, NT)
            dw_ref[n] += _dot(y, dup, TN)
            up = _dot(y, w, NN)
            dpg_ref[:, n * D_MODEL:(n + 1) * D_MODEL] = (dmf * up * gt * (1.0 - gt)).astype(BF16)
            if nxt is not None:
                gt, dup = nxt

    y_spec = pl.BlockSpec((tm, BW), lambda i: (i, 0))
    w_spec = pl.BlockSpec((3, BW, D_MODEL), lambda i: (0, 0, 0))
    return _host_call(
        body, (ya, yb, ym, pg, wbr, dm),
        [y_spec, y_spec, y_spec, pl.BlockSpec((tm, W_G), lambda i: (i, 0)), w_spec,
         pl.BlockSpec((tm, D_MODEL), lambda i: (i, 0))],
        (y_spec, y_spec, y_spec, pl.BlockSpec((tm, W_G), lambda i: (i, 0)), w_spec),
        (SDS((t, BW), F32), SDS((t, BW), F32), SDS((t, BW), F32), SDS((t, W_G), BF16), SDS((3, BW, D_MODEL), F32)),
        name="merge_bwd", grid=(t // tm,), semantics=("arbitrary",), exchange=exchange)


def _loss_head(x, g, target):
    t, d = x.shape
    tm = _tile(t, 512, 8)

    def body(x_ref, g_ref, t_ref, loss_ref, dx_ref, dg_ref):
        @pl.when(pl.program_id(0) == 0)
        def _():
            loss_ref[...] = jnp.zeros_like(loss_ref)
            dg_ref[...] = jnp.zeros_like(dg_ref)

        xf = x_ref[...]
        r = lax.rsqrt(jnp.mean(xf * xf, axis=-1, keepdims=True) + EPS)
        xh = xf * r
        err = xh * g_ref[...] - t_ref[...]
        per_tok = jnp.mean(err * err, axis=-1, keepdims=True)
        loss_ref[...] += 0.5 * jnp.sum(per_tok, axis=0, keepdims=True)
        dy = err * (1.0 / d)
        gd = dy * g_ref[...]
        dx_ref[...] = r * (gd - xh * jnp.mean(gd * xh, axis=-1, keepdims=True))
        dg_ref[...] += jnp.sum(dy * xh, axis=0, keepdims=True)

    row = pl.BlockSpec((tm, d), lambda i: (i, 0))
    vec = pl.BlockSpec((1, d), lambda i: (0, 0))
    return pl.pallas_call(
        body, name="loss_head", grid=(t // tm,),
        in_specs=[row, vec, row],
        out_specs=(pl.BlockSpec((1, 1), lambda i: (0, 0)), row, vec),
        out_shape=(SDS((1, 1), F32), SDS((t, d), F32), SDS((1, d), F32)),
        compiler_params=_params("arbitrary"),
    )(x, g, target)


def _layer_fwd(x, mem, w, tabs, next_shards=None):
    cs, sn = tabs
    riding = next_shards is not None
    h, pa, pb, pm, pg = _norm_in_proj(x, w["norm_g"], w["w_in_t"])
    q, qT, k, kT, vT = _attn_prep_fwd(pa, cs, sn, w["qg2"], w["kg2"])
    (o, lse, ya), gathered = _attn_fwd(qT, k, vT, pa, exchange=_gather_first_hop(next_shards) if riding else None)
    yb = _gmlp_fwd(pb, w["ln_g"], w["ln_b"], w["w_s"], w["b_s"])
    memn = _rmsnorm_fwd(mem, w["mem_g"], "mem_rmsnorm_fwd")
    kv = _mm(memn, w["w_kv"], "nn", "mem_kv")
    ym = _mem_attn_fwd(pm, kv)
    (merged,), gathered = _merge_fwd(ya, yb, ym, pg, w["w_br"], exchange=_gather_forward(gathered) if riding else None)
    x_next = _mm(merged, w["w_out"], "nn", "out_proj", add=x)
    saved = dict(x=x, h=h, pa=pa, pb=pb, pm=pm, pg=pg, q=q, qT=qT, k=k, kT=kT, vT=vT, o=o, lse=lse, ya=ya, yb=yb, ym=ym,
                 memn=memn, kv=kv, merged=merged)
    return x_next, saved, gathered


def _layer_bwd(dx_out, mem, w, s, tabs, pending=None, core=None, scatter_own=False):
    cs, sn = tabs
    t = dx_out.shape[0]
    riding = pending is not None
    dmerged = _mm(dx_out, w["w_out"], "nt", "d_merged")
    shard_rows = D_MODEL // N_DEV
    d_w_out = _mm_tn_by_owner(s["merged"], dx_out, "d_w_out")
    (dya, dyb, dym, dpg, d_w_br), recv = _merge_bwd(s["ya"], s["yb"], s["ym"], s["pg"], w["w_br"], dmerged,
                                                    exchange=_scatter_to_sibling(pending) if riding else None)
    parts = _pair_sums(BIG, pending, recv, core) if riding else None
    grads = dict(w_br=_split_w_br(d_w_br), w_out=d_w_out.reshape(N_DEV, 1, shard_rows, -1))
    early = [grads[n] for n in BIG[2:]]
    grp = A_HEADS // A_KV_HEADS
    lse_cols = s["lse"].reshape(A_KV_HEADS, grp, t).transpose(0, 2, 1)
    (dq, dkT, dvT), landed = _attn_bwd(
        s["q"], s["qT"], s["k"], s["kT"], s["vT"], s["pa"], s["o"], lse_cols, dya,
        exchange=_both(_scatter_to_chips(parts) if riding else None, _scatter_to_sibling(early) if scatter_own else None))
    from_chips, early_recv = (landed[:len(BIG)], landed[len(BIG):]) if riding else (None, landed)
    early_parts = _pair_sums(BIG[2:], early, early_recv, core) if scatter_own else None
    (dpa, d_qg2, d_kg2), early_from_chips = _attn_prep_bwd(
        s["pa"], cs, sn, w["qg2"], w["kg2"], dq, dkT.reshape(-1, t), dvT.reshape(-1, t), dya, s["o"],
        exchange=_scatter_to_chips(early_parts) if scatter_own else None)
    dpb, d_w_s, d_b_s, d_ln_g, d_ln_b = _gmlp_bwd(s["pb"], w["ln_g"], w["ln_b"], w["w_s"], w["b_s"], dyb)
    dpm, dkv = _mem_attn_bwd(s["pm"], s["kv"], dym)
    d_w_kv = _mm_tn_by_owner(s["memn"], dkv, "d_w_kv")
    dmemn = _mm(dkv, w["w_kv"], "nt", "d_memn")
    d_mem_g = _rmsnorm_dg(mem, w["mem_g"], dmemn, "mem_rmsnorm_bwd")
    dps = (dpa, dpb, dpm, dpg)
    d_w_in = _split_w_in_t(*[_mm(dp, s["h"], "tn", "d_w_" + nm, out_dtype=BF16, tm_cap=1024, tn_cap=D_MODEL, tk_cap=4096)
                             for nm, dp in zip("abmg", dps)])
    grads.update(w_in=d_w_in[:, None], w_mem_kv=d_w_kv.reshape(N_DEV, 1, shard_rows, -1))
    own = None
    if scatter_own:
        late = [grads[n] for n in BIG[:2]]
        late_parts = _pair_sums(BIG[:2], late, _exchange_call(_scatter_to_sibling(late), "rs_sibling_swap"), core)
        own = _scatter_to_chips(late_parts)
    (dx_in, d_norm_g), late_from_chips = _dh_rmsnorm_bwd(dps, w["w_in_t"], s["x"], w["norm_g"], dx_out, exchange=own)
    if scatter_own:
        own_parts, own_from_chips = late_parts + early_parts, tuple(late_from_chips) + tuple(early_from_chips)
    grads.update(norm_g=d_norm_g[0], q_norm_g=d_qg2[0, :HEAD_DIM] + d_qg2[0, HEAD_DIM:],
                 k_norm_g=d_kg2[0, :HEAD_DIM] + d_kg2[0, HEAD_DIM:], sg_ln_g=d_ln_g[0], sg_ln_b=d_ln_b[0],
                 w_s=d_w_s, b_s=d_b_s[:, :, 0], mem_norm_g=d_mem_g[0])
    return (dx_in, grads, ((parts, from_chips) if riding else None),
            ((own_parts, own_from_chips) if scatter_own else None))


def _layer_weights(l, w_in_t, w_kv, w_br, w_out, small):
    tile2 = lambda g: jnp.tile(g.reshape(1, -1), (1, 2))
    return dict(
        w_in_t=w_in_t, w_kv=w_kv, w_br=w_br, w_out=w_out,
        norm_g=small["norm_g"][l].reshape(1, -1), qg2=tile2(small["q_norm_g"][l]), kg2=tile2(small["k_norm_g"][l]),
        ln_g=small["sg_ln_g"][l].reshape(1, -1), ln_b=small["sg_ln_b"][l].reshape(1, -1),
        w_s=small["w_s"][l], b_s=small["b_s"][l][:, :, None], mem_g=small["mem_norm_g"][l].reshape(1, -1))


def _position():
    x, y, c = lax.axis_index("x"), lax.axis_index("y"), lax.axis_index("c")
    return x, y, c, [(1 - x, y), (x, 1 - y), (1 - x, 1 - y)]


def _gather_first_hop(shards):
    n = len(shards)

    def make(x_refs, out_refs, sems):
        send_sems, recv_sems, local_sems = sems
        x, y, c, chips = _position()
        me = 4 * x + 2 * y + c
        peers = [(x, y, 1 - c)] + [(cx, cy, c) for cx, cy in chips]
        copies = [pltpu.make_async_copy(x_refs[t], out_refs[t].at[me], local_sems.at[t]) for t in range(n)]
        copies += [pltpu.make_async_remote_copy(
            src_ref=x_refs[t], dst_ref=out_refs[t].at[me], send_sem=send_sems.at[t, k], recv_sem=recv_sems.at[t, k],
            device_id=peer, device_id_type=MESH_ID) for t in range(n) for k, peer in enumerate(peers)]
        return copies

    return _Exchange(shards, [SDS((N_DEV,) + a.shape, a.dtype) for a in shards],
                     [pltpu.SemaphoreType.DMA((n, 4)), pltpu.SemaphoreType.DMA((n, 4)), pltpu.SemaphoreType.DMA((n,))], make)


def _gather_forward(gathered):
    n = len(gathered)

    def make(in_refs, out_refs, sems):
        send_sems, recv_sems = sems
        x, y, c, chips = _position()
        return [pltpu.make_async_remote_copy(
            src_ref=in_refs[t].at[4 * cx + 2 * cy + c], dst_ref=out_refs[t].at[4 * cx + 2 * cy + c],
            send_sem=send_sems.at[t, j], recv_sem=recv_sems.at[t, j], device_id=(x, y, 1 - c), device_id_type=MESH_ID)
            for t in range(n) for j, (cx, cy) in enumerate(chips)]

    return _Exchange(gathered, [SDS(a.shape, a.dtype) for a in gathered],
                     [pltpu.SemaphoreType.DMA((n, 3)), pltpu.SemaphoreType.DMA((n, 3))], make,
                     aliases={t: t for t in range(n)})


def _all_gather(shards, name):
    return _exchange_call(_gather_forward(_exchange_call(_gather_first_hop(shards), name + "_hop1")), name + "_hop2")


def _shard_segments(d):
    shard = IN_WIDTH // N_DEV
    lo, hi = d * shard, (d + 1) * shard
    out = []
    for g in range(4):
        a, b = max(lo, GROUP_OFFS[g]), min(hi, GROUP_OFFS[g + 1])
        if a < b:
            out.append((g, a - GROUP_OFFS[g], b - GROUP_OFFS[g], a - lo))
    return out


def _split_w_in_t(da, db, dm, dg):
    cols = da.shape[1]
    shard = IN_WIDTH // N_DEV
    tc = _tile(cols, 256)

    def body(a_ref, b_ref, m_ref, g_ref, o_ref):
        ins = (a_ref, b_ref, m_ref, g_ref)
        for d in range(N_DEV):
            for g, lo, hi, off in _shard_segments(d):
                o_ref[_core_major(d), off:off + hi - lo, :] = ins[g][lo:hi, :]

    return pl.pallas_call(
        body, name="split_w_in", grid=(cols // tc,),
        in_specs=[pl.BlockSpec((a.shape[0], tc), lambda j: (0, j)) for a in (da, db, dm, dg)],
        out_specs=pl.BlockSpec((N_DEV, shard, tc), lambda j: (0, 0, j)),
        out_shape=SDS((N_DEV, shard, cols), da.dtype), compiler_params=_params("parallel"),
    )(da, db, dm, dg)


def _assemble_w_br(gathered):
    _, nb, rows, shard = gathered.shape

    def body(g_ref, o_ref):
        for d in range(N_DEV):
            o_ref[:, :, d * shard:(d + 1) * shard] = g_ref[d]

    return pl.pallas_call(
        body, name="assemble_w_br", out_shape=SDS((nb, rows, N_DEV * shard), gathered.dtype),
        compiler_params=pltpu.CompilerParams(vmem_limit_bytes=VMEM_LIMIT),
    )(gathered)


def _split_w_br(dw):
    nb, rows, cols = dw.shape
    shard = cols // N_DEV

    def body(d_ref, o_ref):
        for d in range(N_DEV):
            o_ref[_core_major(d)] = d_ref[:, :, d * shard:(d + 1) * shard].astype(BF16)

    return pl.pallas_call(
        body, name="split_w_br", out_shape=SDS((N_DEV, nb, rows, shard), BF16),
        compiler_params=pltpu.CompilerParams(vmem_limit_bytes=VMEM_LIMIT),
    )(dw)


def _scatter_to_sibling(dests):
    n = len(dests)

    def make(d_refs, recv_refs, sems):
        send_sems, recv_sems = sems
        x, y, c, _ = _position()
        return [pltpu.make_async_remote_copy(
            src_ref=d_refs[t].at[pl.ds((1 - c) * 4, 4)], dst_ref=recv_refs[t], send_sem=send_sems.at[t],
            recv_sem=recv_sems.at[t], device_id=(x, y, 1 - c), device_id_type=MESH_ID) for t in range(n)]

    return _Exchange(dests, [SDS((4,) + a.shape[1:], a.dtype) for a in dests],
                     [pltpu.SemaphoreType.DMA((n,)), pltpu.SemaphoreType.DMA((n,))], make)


def _scatter_to_chips(parts):
    n = len(parts)

    def make(p_refs, recv_refs, sems):
        send_sems, recv_sems = sems
        _, _, c, chips = _position()
        return [pltpu.make_async_remote_copy(
            src_ref=p_refs[t].at[2 * cx + cy], dst_ref=recv_refs[t].at[k], send_sem=send_sems.at[t, k],
            recv_sem=recv_sems.at[t, k], device_id=(cx, cy, c), device_id_type=MESH_ID)
            for t in range(n) for k, (cx, cy) in enumerate(chips)]

    return _Exchange(parts, [SDS((3,) + a.shape[1:], a.dtype) for a in parts],
                     [pltpu.SemaphoreType.DMA((n, 3)), pltpu.SemaphoreType.DMA((n, 3))], make)


def _pair_sums(names, dests, recv, core):
    return [_pair_sum(d, r, core, "rs_pair_sum_" + n) for n, d, r in zip(names, dests, recv)]


def _pair_sum(dest, recv, core, name):
    _, na, r, cdim = dest.shape

    def body(core_ref, a_ref, b_ref, o_ref):
        o_ref[...] = (a_ref[...].astype(F32) + b_ref[...].astype(F32)).astype(o_ref.dtype)

    blk = pl.BlockSpec((None, na, r, cdim), lambda j, core_ref: (j, 0, 0, 0))
    return pl.pallas_call(
        body, name=name, out_shape=SDS(recv.shape, BF16),
        grid_spec=pltpu.PrefetchScalarGridSpec(
            num_scalar_prefetch=1, grid=(4,),
            in_specs=[pl.BlockSpec((None, na, r, cdim), lambda j, core_ref: (core_ref[0] * 4 + j, 0, 0, 0)), blk],
            out_specs=blk),
        compiler_params=_params("parallel"),
    )(core, dest, recv)


def _adamw_math(w, g, m, v):
    m = ADAM_B1 * m + (1.0 - ADAM_B1) * g
    v = ADAM_B2 * v + (1.0 - ADAM_B2) * (g * g)
    m_hat = m / (1.0 - ADAM_B1 ** ADAM_STEP)
    v_hat = v / (1.0 - ADAM_B2 ** ADAM_STEP)
    delta = -ADAM_LR * (m_hat / (jnp.sqrt(v_hat) + ADAM_EPS) + ADAM_WD * w)
    return delta, m, v


def _sum_and_adamw(parts, w, m, v, name):
    n, r, ln = parts.shape
    tr = _tile(r, 512, 16)

    def body(p_ref, w_ref, m_ref, v_ref, g_out, d_out, m_out, v_out):
        g = p_ref[0].astype(F32)
        for j in range(1, n):
            g = g + p_ref[j].astype(F32)
        delta, nm, nv = _adamw_math(w_ref[...], g, m_ref[...], v_ref[...])
        g_out[...] = g
        d_out[...] = delta
        m_out[...] = nm
        v_out[...] = nv

    blk = pl.BlockSpec((tr, ln), lambda i: (i, 0))
    return pl.pallas_call(
        body, name=name, grid=(r // tr,),
        in_specs=[pl.BlockSpec((n, tr, ln), lambda i: (0, i, 0)), blk, blk, blk],
        out_specs=(blk, blk, blk, blk), out_shape=tuple(SDS((r, ln), F32) for _ in range(4)),
        compiler_params=_params("parallel"),
    )(parts, w, m, v)


BIG = ("w_in", "w_mem_kv", "w_br", "w_out")
SMALL = ("norm_g", "q_norm_g", "k_norm_g", "sg_ln_g", "sg_ln_b", "w_s", "b_s", "mem_norm_g", "final_g")


def _pack(arrs, row_unit=16):
    flat = jnp.concatenate([a.reshape(-1) for a in arrs])
    pad = (-flat.shape[0]) % (row_unit * LANES)
    if pad:
        flat = jnp.concatenate([flat, jnp.zeros((pad,), flat.dtype)])
    return flat.reshape(-1, LANES)


def _unpack(buf, shapes):
    flat = buf.reshape(-1)
    out, off = [], 0
    for shp in shapes:
        n = 1
        for s_ in shp:
            n *= s_
        out.append(flat[off:off + n].reshape(shp))
        off += n
    return out


def _shard_sum_adamw(part, from_chips, chip, w, m, v, layer, prev, name):
    _, na, r, cdim = part.shape
    flat = w.ndim == 3
    assert not flat or na == 1
    tr = _tile(r, max(8, (256 * 1024) // (na * cdim)), 8)
    n_prev = 0 if prev is None else len(prev)

    def body(chip_ref, p_ref, f_ref, w_ref, m_ref, v_ref, *rest):
        g_out, d_out, m_out, v_out = rest[n_prev:]
        g = p_ref[...].astype(F32)
        for j in range(3):
            g = g + f_ref[j].astype(F32)
        delta, nm, nv = _adamw_math(w_ref[...], g, m_ref[...], v_ref[...])
        g_out[...] = g
        d_out[...] = delta
        m_out[...] = nm
        v_out[...] = nv

    a_blk = None if flat else na
    if flat:
        lay = pl.BlockSpec((None, tr, cdim), lambda i, chip_ref: (layer, i, 0))
    else:
        lay = pl.BlockSpec((None, na, tr, cdim), lambda i, chip_ref: (layer, 0, i, 0))
    return pl.pallas_call(
        body, name=name, out_shape=tuple(SDS(w.shape, F32) for _ in range(4)),
        grid_spec=pltpu.PrefetchScalarGridSpec(
            num_scalar_prefetch=1, grid=(r // tr,),
            in_specs=[pl.BlockSpec((None, a_blk, tr, cdim), lambda i, chip_ref: (chip_ref[0], 0, i, 0)),
                      pl.BlockSpec((3, a_blk, tr, cdim), lambda i, chip_ref: (0, 0, i, 0)), lay, lay, lay]
            + [HBM] * n_prev,
            out_specs=(lay, lay, lay, lay)),
        input_output_aliases={6 + j: j for j in range(n_prev)},
        compiler_params=_params("parallel"),
    )(chip, part, from_chips, w, m, v, *(prev or ()))


def kernel(x, mem, norm_g, w_in, q_norm_g, k_norm_g, sg_ln_g, sg_ln_b, w_s, b_s, mem_norm_g, w_mem_kv, w_br, w_out, final_g, loss_target, m_norm_g, m_w_in, m_q_norm_g, m_k_norm_g, m_sg_ln_g, m_sg_ln_b, m_w_s, m_b_s, m_mem_norm_g, m_w_mem_kv, m_w_br, m_w_out, m_final_g, v_norm_g, v_w_in, v_q_norm_g, v_k_norm_g, v_sg_ln_g, v_sg_ln_b, v_w_s, v_b_s, v_mem_norm_g, v_w_mem_kv, v_w_br, v_w_out, v_final_g):
    wts = dict(norm_g=norm_g, w_in=w_in, q_norm_g=q_norm_g, k_norm_g=k_norm_g, sg_ln_g=sg_ln_g, sg_ln_b=sg_ln_b,
               w_s=w_s, b_s=b_s, mem_norm_g=mem_norm_g, w_mem_kv=w_mem_kv, w_br=w_br, w_out=w_out, final_g=final_g)
    mom1 = dict(norm_g=m_norm_g, w_in=m_w_in, q_norm_g=m_q_norm_g, k_norm_g=m_k_norm_g, sg_ln_g=m_sg_ln_g,
                sg_ln_b=m_sg_ln_b, w_s=m_w_s, b_s=m_b_s, mem_norm_g=m_mem_norm_g, w_mem_kv=m_w_mem_kv, w_br=m_w_br,
                w_out=m_w_out, final_g=m_final_g)
    mom2 = dict(norm_g=v_norm_g, w_in=v_w_in, q_norm_g=v_q_norm_g, k_norm_g=v_k_norm_g, sg_ln_g=v_sg_ln_g,
                sg_ln_b=v_sg_ln_b, w_s=v_w_s, b_s=v_b_s, mem_norm_g=v_mem_norm_g, w_mem_kv=v_w_mem_kv, w_br=v_w_br,
                w_out=v_w_out, final_g=v_final_g)
    dp = w_in.shape[0]
    core = lax.axis_index("c").astype(jnp.int32).reshape(1)
    chip = (2 * lax.axis_index("x") + lax.axis_index("y")).astype(jnp.int32).reshape(1)

    t_in = lambda a: jnp.swapaxes(a, 1, 2)
    wts, mom1, mom2 = [dict(d_, w_in=t_in(d_["w_in"])) for d_ in (wts, mom1, mom2)]

    shard_bf = {n: wts[n].astype(BF16) for n in BIG}
    shards = lambda l: [shard_bf[n][l] for n in BIG]
    x_l, mem_l = x[0], mem[0]
    tabs = _rope_tables(x_l.shape[0])

    gathered = _all_gather(shards(0), "weights_all_gather")
    layers, saved = [], []
    for l in range(dp):
        g_in, g_kv, g_br, g_out = gathered
        layers.append(_layer_weights(l, g_in.reshape(IN_WIDTH, -1), g_kv.reshape(D_MODEL, -1), _assemble_w_br(g_br),
                                     g_out.reshape(D_MODEL, -1), wts))
        x_l, s, gathered = _layer_fwd(x_l, mem_l, layers[l], tabs, next_shards=shards(l + 1) if l + 1 < dp else None)
        saved.append(s)
    loss_local, dx, d_final_g = _loss_head(x_l, final_g.reshape(1, -1), loss_target[0])
    loss = lax.psum(loss_local[0, 0], AXES)

    def finish(l, parts, from_chips, prev):
        return {n: _shard_sum_adamw(p, f, chip, wts[n], mom1[n], mom2[n], l, None if prev is None else prev[n],
                                    "sum_adamw_" + n)
                for n, p, f in zip(BIG, parts, from_chips)}

    grads, updated, pending = [None] * dp, None, None
    for l in reversed(range(dp)):
        dx, grads[l], scattered, own = _layer_bwd(dx, mem_l, layers[l], saved[l], tabs, pending=pending, core=core,
                                                  scatter_own=(l == 0))
        if scattered is not None:
            updated = finish(l + 1, *scattered, updated)
        pending = [grads[l][n] for n in BIG]
    updated = finish(0, *own, updated)
    grad_x = dx
    big_out = [{n: (t_in(updated[n][k]) if n == "w_in" else updated[n][k]) for n in BIG} for k in range(4)]

    small_g = {n: jnp.stack([g[n] for g in grads]) for n in SMALL if n != "final_g"}
    small_g["final_g"] = d_final_g
    (all_small,) = _all_gather([_pack([small_g[n] for n in SMALL])], "small_all_gather")
    small_bufs = _sum_and_adamw(
        all_small, _pack([wts[n] for n in SMALL]), _pack([mom1[n] for n in SMALL]), _pack([mom2[n] for n in SMALL]),
        "small_sum_adamw")

    outs = []
    for big_vals, small_buf in zip(big_out, small_bufs):
        vals = dict(big_vals)
        vals.update(zip(SMALL, _unpack(small_buf, [wts[n].shape for n in SMALL])))
        outs.append(vals)
    order = ("norm_g", "w_in", "q_norm_g", "k_norm_g", "sg_ln_g", "sg_ln_b", "w_s", "b_s", "mem_norm_g", "w_mem_kv",
             "w_br", "w_out", "final_g")
    result = [loss, grad_x[None]]
    for vals in outs:
        result += [vals[n] for n in order]
    return tuple(result)
```

```python
import functools

import jax
import jax.numpy as jnp
from jax import lax
from jax.experimental import pallas as pl
from jax.experimental.pallas import tpu as pltpu

F32 = jnp.float32
BF16 = jnp.bfloat16
SDS = jax.ShapeDtypeStruct
MESH_ID = pl.DeviceIdType.MESH
AXES = ("x", "y", "c")
N_DEV = 8

D_MODEL = 1024
DEPTH = 4
GRID_W = 64
CHUNK = 128
ROPE_THETA = 10000.0
EPS = 1e-6
HEAD_DIM = 64
A_HEADS = 8
A_KV_HEADS = 2
B_GROUPS = 4
M_HEADS = 4
M_HEAD_DIM = 128
BW = 512
W_A = 1280
W_B = 1536
W_M = 1024
W_G = 3072
IN_WIDTH = W_A + W_B + W_M + W_G
GROUP_OFFS = (0, W_A, W_A + W_B, W_A + W_B + W_M, IN_WIDTH)

ADAM_LR = 0.001
ADAM_B1 = 0.9
ADAM_B2 = 0.999
ADAM_EPS = 1e-08
ADAM_WD = 0.01
ADAM_STEP = 10

LANES = 128
KEY_CHUNK = 2048
VMEM_LIMIT = 52 * 1024 * 1024


def _tile(n, cap, unit=LANES):
    if n <= cap:
        return n
    t = (cap // unit) * unit
    while n % t:
        t -= unit
    return t


def _params(*sem):
    return pltpu.CompilerParams(dimension_semantics=sem, vmem_limit_bytes=VMEM_LIMIT)


def _sigmoid(z):
    return 0.5 * jnp.tanh(0.5 * z) + 0.5


def _silu_and_grad(z):
    s = _sigmoid(z)
    return z * s, s * (1.0 + z * (1.0 - s))


def _dot(a, b, dims):
    return lax.dot_general(a, b, (dims, ((), ())), preferred_element_type=F32)


NN = ((1,), (0,))
NT = ((1,), (1,))
TN = ((0,), (0,))
HBM = pl.BlockSpec(memory_space=pl.ANY)


class _Exchange:
    def __init__(self, ins, out_shapes, sems, make, aliases=None):
        self.ins, self.out_shapes, self.sems, self.make = list(ins), list(out_shapes), list(sems), make
        self.aliases = dict(aliases or {})

    def start(self, in_refs, out_refs, sems):
        for cp in self.make(in_refs, out_refs, sems):
            cp.start()

    def finish(self, in_refs, out_refs, sems):
        for cp in self.make(in_refs, out_refs, sems):
            cp.wait()


def _both(a, b):
    if a is None or b is None:
        return a if b is None else b
    n_in, n_out, n_sem = len(a.ins), len(a.out_shapes), len(a.sems)

    def make(in_refs, out_refs, sems):
        return (a.make(in_refs[:n_in], out_refs[:n_out], sems[:n_sem])
                + b.make(in_refs[n_in:], out_refs[n_out:], sems[n_sem:]))

    aliases = dict(a.aliases)
    aliases.update({n_in + i: n_out + o for i, o in b.aliases.items()})
    return _Exchange(a.ins + b.ins, a.out_shapes + b.out_shapes, a.sems + b.sems, make, aliases)


def _host_call(body, ins, in_specs, out_specs, out_shape, *, name, grid, semantics, scratch=(), exchange=None):
    ins, out_shape, scratch = list(ins), list(out_shape), list(scratch)
    if exchange is None:
        res = pl.pallas_call(
            body, name=name, grid=grid, in_specs=list(in_specs), out_specs=tuple(out_specs), out_shape=tuple(out_shape),
            scratch_shapes=scratch, compiler_params=_params(*semantics))(*ins)
        return tuple(res), ()
    n_in, n_out, n_scr = len(ins), len(out_shape), len(scratch)
    x_in, x_out = len(exchange.ins), len(exchange.out_shapes)

    def carrying(*refs):
        o0 = n_in + x_in
        s0 = o0 + n_out + x_out
        c_in, c_out, c_sems = refs[n_in:o0], refs[o0 + n_out:s0], refs[s0 + n_scr:]
        ids = [pl.program_id(a) for a in range(len(grid))]
        first = functools.reduce(jnp.logical_and, [i == 0 for i in ids])
        last = functools.reduce(jnp.logical_and, [i == g - 1 for i, g in zip(ids, grid)])

        @pl.when(first)
        def _():
            exchange.start(c_in, c_out, c_sems)

        body(*refs[:n_in], *refs[o0:o0 + n_out], *refs[s0:s0 + n_scr])

        @pl.when(last)
        def _():
            exchange.finish(c_in, c_out, c_sems)

    res = pl.pallas_call(
        carrying, name=name, grid=grid, in_specs=list(in_specs) + [HBM] * x_in,
        out_specs=tuple(out_specs) + (HBM,) * x_out, out_shape=tuple(out_shape) + tuple(exchange.out_shapes),
        scratch_shapes=scratch + exchange.sems,
        input_output_aliases={n_in + i: n_out + o for i, o in exchange.aliases.items()},
        compiler_params=_params(*(["arbitrary"] * len(grid))))(*ins, *exchange.ins)
    return tuple(res[:n_out]), tuple(res[n_out:])


def _exchange_call(exchange, name):
    x_in = len(exchange.ins)

    def body(*refs):
        x_out = len(exchange.out_shapes)
        c_in, c_out, c_sems = refs[:x_in], refs[x_in:x_in + x_out], refs[x_in + x_out:]
        exchange.start(c_in, c_out, c_sems)
        exchange.finish(c_in, c_out, c_sems)

    return pl.pallas_call(
        body, name=name, in_specs=[HBM] * x_in, out_specs=tuple([HBM] * len(exchange.out_shapes)),
        out_shape=tuple(exchange.out_shapes), scratch_shapes=exchange.sems, input_output_aliases=exchange.aliases,
    )(*exchange.ins)


def _core_major(d):
    return (d % 2) * 4 + d // 2


def _mm(a, b, mode, name, add=None, out_dtype=F32, tm_cap=512, tn_cap=1536, tk_cap=1536):
    (m, k), (n, k2) = a.shape, (b.shape[::-1] if mode == "nn" else b.shape)
    assert k == k2 and mode in ("nn", "nt"), (a.shape, b.shape, mode)
    tm = _tile(m, tm_cap, 8)
    tn = _tile(n, tn_cap)
    tk = _tile(k, tk_cap)
    nk = k // tk
    dims = {"nn": NN, "nt": NT}[mode]
    a_spec = pl.BlockSpec((tm, tk), lambda i, j, kk: (i, kk))
    b_spec = {"nn": pl.BlockSpec((tk, tn), lambda i, j, kk: (kk, j)),
              "nt": pl.BlockSpec((tn, tk), lambda i, j, kk: (j, kk))}[mode]
    o_spec = pl.BlockSpec((tm, tn), lambda i, j, kk: (i, j))
    has_add = add is not None

    def body(*refs):
        a_ref, b_ref = refs[0], refs[1]
        add_ref = refs[2] if has_add else None
        o_ref = refs[3] if has_add else refs[2]
        part = _dot(a_ref[...].astype(BF16), b_ref[...].astype(BF16), dims)
        if nk == 1:
            if has_add:
                part = part + add_ref[...]
            o_ref[...] = part.astype(out_dtype)
        else:
            acc = refs[-1]
            kk = pl.program_id(2)

            @pl.when(kk == 0)
            def _():
                acc[...] = part + add_ref[...] if has_add else part

            @pl.when(kk > 0)
            def _():
                acc[...] += part

            @pl.when(kk == nk - 1)
            def _():
                o_ref[...] = acc[...].astype(out_dtype)

    ins = [a, b] + ([add] if has_add else [])
    in_specs = [a_spec, b_spec] + ([o_spec] if has_add else [])
    return pl.pallas_call(
        body, name=name, grid=(m // tm, n // tn, nk), in_specs=in_specs, out_specs=o_spec,
        out_shape=SDS((m, n), out_dtype),
        scratch_shapes=[pltpu.VMEM((tm, tn), F32)] if nk > 1 else [],
        compiler_params=_params("parallel", "parallel", "arbitrary"),
    )(*ins)


def _mm_tn_by_owner(a, b, name):
    (k, m), (k2, n) = a.shape, b.shape
    assert k == k2 and m % N_DEV == 0, (a.shape, b.shape)
    tk = _tile(k, 2048, 16)
    nk = k // tk
    rows = m // N_DEV

    def body(a_ref, b_ref, o_ref, acc):
        kk = pl.program_id(0)
        part = _dot(a_ref[...].astype(BF16), b_ref[...].astype(BF16), TN)

        @pl.when(kk == 0)
        def _():
            acc[...] = part

        @pl.when(kk > 0)
        def _():
            acc[...] += part

        @pl.when(kk == nk - 1)
        def _():
            for d in range(N_DEV):
                pos = _core_major(d)
                o_ref[pos * rows:(pos + 1) * rows, :] = acc[d * rows:(d + 1) * rows, :].astype(BF16)

    return pl.pallas_call(
        body, name=name, grid=(nk,),
        in_specs=[pl.BlockSpec((tk, m), lambda kk: (kk, 0)), pl.BlockSpec((tk, n), lambda kk: (kk, 0))],
        out_specs=pl.BlockSpec((m, n), lambda kk: (0, 0)), out_shape=SDS((m, n), BF16),
        scratch_shapes=[pltpu.VMEM((m, n), F32)], compiler_params=_params("arbitrary"),
    )(a, b)


def _rmsnorm_fwd(x, g, name):
    t, d = x.shape
    tm = _tile(t, 512, 8)

    def body(x_ref, g_ref, h_ref):
        xf = x_ref[...]
        r = lax.rsqrt(jnp.mean(xf * xf, axis=-1, keepdims=True) + EPS)
        h_ref[...] = (xf * r * g_ref[...]).astype(BF16)

    return pl.pallas_call(
        body, name=name, grid=(t // tm,),
        in_specs=[pl.BlockSpec((tm, d), lambda i: (i, 0)), pl.BlockSpec((1, d), lambda i: (0, 0))],
        out_specs=pl.BlockSpec((tm, d), lambda i: (i, 0)),
        out_shape=SDS((t, d), BF16), compiler_params=_params("parallel"),
    )(x, g)


def _rmsnorm_bwd_math(xf, g, dh):
    r = lax.rsqrt(jnp.mean(xf * xf, axis=-1, keepdims=True) + EPS)
    xh = xf * r
    gd = dh * g
    dx = r * (gd - xh * jnp.mean(gd * xh, axis=-1, keepdims=True))
    return dx, jnp.sum(dh * xh, axis=0, keepdims=True)


def _rmsnorm_dg(x, g, dh, name):
    t, d = x.shape
    tm = _tile(t, 512, 8)

    def body(x_ref, g_ref, dh_ref, dg_ref):
        @pl.when(pl.program_id(0) == 0)
        def _():
            dg_ref[...] = jnp.zeros_like(dg_ref)

        dg_ref[...] += _rmsnorm_bwd_math(x_ref[...], g_ref[...], dh_ref[...])[1]

    row = pl.BlockSpec((tm, d), lambda i: (i, 0))
    vec = pl.BlockSpec((1, d), lambda i: (0, 0))
    return pl.pallas_call(
        body, name=name, grid=(t // tm,), in_specs=[row, vec, row], out_specs=vec, out_shape=SDS((1, d), F32),
        compiler_params=_params("arbitrary"),
    )(x, g, dh)


def _group_rows(p):
    return slice(GROUP_OFFS[p], GROUP_OFFS[p + 1])


def _d_w_in_t(dps, h):
    t, d = h.shape
    tm = 256
    tiles = [a.shape[1] // tm for a in dps]
    starts = [sum(tiles[:p]) for p in range(len(dps))]

    def body(*refs):
        a_refs, h_ref, outs = refs[:len(dps)], refs[len(dps)], refs[len(dps) + 1:]
        i = pl.program_id(0)
        for p, (a_ref, o_ref) in enumerate(zip(a_refs, outs)):
            @pl.when((i >= starts[p]) & (i < starts[p] + tiles[p]))
            def _():
                o_ref[...] = _dot(a_ref[...], h_ref[...], TN).astype(BF16)

    def tile_of(p):
        return lambda i: jnp.clip(i - starts[p], 0, tiles[p] - 1)

    return pl.pallas_call(
        body, name="d_w_in", grid=(sum(tiles),),
        in_specs=[pl.BlockSpec((t, tm), lambda i, p=p: (0, tile_of(p)(i))) for p in range(len(dps))]
        + [pl.BlockSpec((t, d), lambda i: (0, 0), pipeline_mode=pl.Buffered(1))],
        out_specs=tuple(pl.BlockSpec((tm, d), lambda i, p=p: (tile_of(p)(i), 0)) for p in range(len(dps))),
        out_shape=tuple(SDS((a.shape[1], d), BF16) for a in dps), compiler_params=_params("arbitrary"),
    )(*dps, h)


def _norm_in_proj(x, g, w_t):
    t, d = x.shape
    tm = _tile(t, 512, 16)
    widths = [GROUP_OFFS[p + 1] - GROUP_OFFS[p] for p in range(4)]

    def body(x_ref, g_ref, w_ref, h_ref, *outs):
        xf = x_ref[...]
        r = lax.rsqrt(jnp.mean(xf * xf, axis=-1, keepdims=True) + EPS)
        hb = (xf * r * g_ref[...]).astype(BF16)
        h_ref[...] = hb
        for p, o_ref in enumerate(outs):
            o_ref[...] = _dot(hb, w_ref[_group_rows(p), :], NT).astype(BF16)

    row = lambda wd: pl.BlockSpec((tm, wd), lambda i: (i, 0))
    return pl.pallas_call(
        body, name="norm_in_proj", grid=(t // tm,),
        in_specs=[row(d), pl.BlockSpec((1, d), lambda i: (0, 0)),
                  pl.BlockSpec(w_t.shape, lambda i: (0, 0), pipeline_mode=pl.Buffered(1))],
        out_specs=(row(d),) + tuple(row(wd) for wd in widths),
        out_shape=(SDS((t, d), BF16),) + tuple(SDS((t, wd), BF16) for wd in widths),
        compiler_params=_params("parallel"),
    )(x, g, w_t)


def _dh_rmsnorm_bwd(dps, w_t, x, g, dres, exchange=None):
    t, d = x.shape
    tm = _tile(t, 256, 16)
    n = len(dps)

    def body(*refs):
        dp_refs, w_ref = refs[:n], refs[n]
        x_ref, g_ref, dres_ref, dx_ref, dg_ref = refs[n + 1:]
        dh = None
        for p, dp_ref in enumerate(dp_refs):
            part = _dot(dp_ref[...], w_ref[_group_rows(p), :], NN)
            dh = part if dh is None else dh + part
        dx, dg = _rmsnorm_bwd_math(x_ref[...], g_ref[...], dh)
        dx_ref[...] = dx + dres_ref[...]

        @pl.when(pl.program_id(0) == 0)
        def _():
            dg_ref[...] = jnp.zeros_like(dg_ref)

        dg_ref[...] += dg

    row = pl.BlockSpec((tm, d), lambda i: (i, 0))
    vec = pl.BlockSpec((1, d), lambda i: (0, 0))
    return _host_call(
        body, list(dps) + [w_t, x, g, dres],
        [pl.BlockSpec((tm, a.shape[1]), lambda i: (i, 0)) for a in dps]
        + [pl.BlockSpec(w_t.shape, lambda i: (0, 0), pipeline_mode=pl.Buffered(1)), row, vec, row],
        (row, vec), (SDS((t, d), F32), SDS((1, d), F32)),
        name="dh_rmsnorm_bwd", grid=(t // tm,), semantics=("arbitrary",), exchange=exchange)


def _rope_tables(t):
    rows = t // GRID_W
    row = jnp.repeat(jnp.arange(rows, dtype=F32), GRID_W)
    col = jnp.tile(jnp.arange(GRID_W, dtype=F32), rows)
    n_freq = HEAD_DIM // 4
    inv = ROPE_THETA ** (-jnp.arange(n_freq, dtype=F32) / n_freq)
    ang = jnp.stack([row[:, None] * inv, col[:, None] * inv], axis=1)
    cos, sin = jnp.cos(ang), jnp.sin(ang)
    c64 = jnp.concatenate([cos[:, 0], cos[:, 0], cos[:, 1], cos[:, 1]], axis=-1)
    s64 = jnp.concatenate([-sin[:, 0], sin[:, 0], -sin[:, 1], sin[:, 1]], axis=-1)
    return jnp.tile(c64, (1, 2)), jnp.tile(s64, (1, 2))


def _head_sums(v, lane):
    lo = jnp.sum(jnp.where(lane < HEAD_DIM, v, 0.0), axis=-1, keepdims=True)
    hi = jnp.sum(jnp.where(lane < HEAD_DIM, 0.0, v), axis=-1, keepdims=True)
    return jnp.where(lane < HEAD_DIM, lo, hi)


def _swap16(v, lane):
    return jnp.where((lane % 32) < 16, pltpu.roll(v, LANES - 16, 1), pltpu.roll(v, 16, 1))


def _attn_prep_fwd(pa, cs, sn, qg2, kg2):
    t = pa.shape[0]
    tq = _tile(t, 512, LANES)
    scale = HEAD_DIM ** -0.5

    def body(pa_ref, cs_ref, sn_ref, qg_ref, kg_ref, q_ref, qT_ref, k_ref, kT_ref, vT_ref):
        lane = lax.broadcasted_iota(jnp.int32, (tq, LANES), 1)
        c, s = cs_ref[...], sn_ref[...]

        def norm_rope(xg, g2):
            r = lax.rsqrt(_head_sums(xg * xg, lane) * (1.0 / HEAD_DIM) + EPS)
            xn = xg * r * g2
            return xn * c + _swap16(xn, lane) * s

        for gi in range(4):
            sl = slice(gi * LANES, (gi + 1) * LANES)
            qr = norm_rope(pa_ref[:, sl].astype(F32), qg_ref[...]) * scale
            q_ref[:, sl] = qr.astype(BF16)
            qT_ref[sl, :] = qr.T.astype(BF16)
        kr = norm_rope(pa_ref[:, 512:640].astype(F32), kg_ref[...])
        kT_ref[...] = kr.T.astype(BF16)
        vT_ref[...] = pa_ref[:, 640:768].astype(F32).T.astype(BF16)
        kr = kr.astype(BF16)
        for kvh in range(A_KV_HEADS):
            k_ref[kvh] = kr[:, kvh * HEAD_DIM:(kvh + 1) * HEAD_DIM]

    row = lambda w: pl.BlockSpec((tq, w), lambda i: (i, 0))
    col = lambda r: pl.BlockSpec((r, tq), lambda i: (0, i))
    vec = pl.BlockSpec((1, LANES), lambda i: (0, 0))
    hm = pl.BlockSpec((A_KV_HEADS, tq, HEAD_DIM), lambda i: (0, i, 0))
    return pl.pallas_call(
        body, name="attn_prep_fwd", grid=(t // tq,),
        in_specs=[row(W_A), row(LANES), row(LANES), vec, vec],
        out_specs=(row(BW), col(BW), hm, col(LANES), col(LANES)),
        out_shape=(SDS((t, BW), BF16), SDS((BW, t), BF16), SDS((A_KV_HEADS, t, HEAD_DIM), BF16),
                   SDS((LANES, t), BF16), SDS((LANES, t), BF16)),
        compiler_params=_params("parallel"),
    )(pa, cs, sn, qg2, kg2)


def _attn_fwd(qT, k, vT, pa, exchange=None):
    t = qT.shape[1]
    tq = _tile(t, 256, LANES)
    grp = A_HEADS // A_KV_HEADS

    ck = _tile(t, KEY_CHUNK, LANES)

    def body(qT_ref, k_ref, vT_ref, pa_ref, o_ref, lse_ref, y_ref):
        def finish_pair(pair, o_pair):
            o2 = jnp.concatenate(o_pair, axis=0).T
            cols = slice(pair * LANES, (pair + 1) * LANES)
            o_ref[:, cols] = o2
            z = pa_ref[:, 768 + pair * LANES:768 + (pair + 1) * LANES].astype(F32)
            y_ref[:, cols] = (o2 * (z * _sigmoid(z))).astype(BF16)

        n_c = t // ck
        items = [(h, c) for h in range(A_HEADS) for c in range(n_c)]

        def scores(h, c):
            return _dot(k_ref[h // grp, c * ck:(c + 1) * ck, :], qT_ref[h * HEAD_DIM:(h + 1) * HEAD_DIM, :], NN)

        def weighted_values(h, c, pT):
            kvh = h // grp
            return _dot(vT_ref[kvh * HEAD_DIM:(kvh + 1) * HEAD_DIM, c * ck:(c + 1) * ck], pT, NN)

        o_pair, sT, pending, acc = [], scores(*items[0]), None, None
        for i in range(len(items) + 1):
            item = items[i] if i < len(items) else None
            sT_next = scores(*items[i + 1]) if i + 1 < len(items) else None
            if pending is not None:
                (ph, pc, pT, alpha, l_run) = pending
                o_c = weighted_values(ph, pc, pT)
                acc = o_c if alpha is None else acc * alpha + o_c
                if pc == n_c - 1:
                    o_pair.append(acc / l_run)
                    if ph % 2:
                        finish_pair(ph // 2, o_pair)
                        o_pair = []
                pending = None
            if item is not None:
                h, c = item
                m_c = jnp.max(sT, axis=0, keepdims=True)
                if c == 0:
                    m_new, alpha = m_c, None
                else:
                    m_new = jnp.maximum(m, m_c)
                    alpha = jnp.exp(m - m_new)
                pT = jnp.exp(sT - m_new)
                l_c = jnp.sum(pT, axis=0, keepdims=True)
                l = l_c if c == 0 else l * alpha + l_c
                m = m_new
                if c == n_c - 1:
                    lse_ref[h:h + 1, :] = m + jnp.log(l)
                pending = (h, c, pT.astype(BF16), alpha, l)
            sT = sT_next

    row = lambda w: pl.BlockSpec((tq, w), lambda i: (i, 0))
    return _host_call(
        body, (qT, k, vT, pa),
        [pl.BlockSpec((BW, tq), lambda i: (0, i)), pl.BlockSpec((A_KV_HEADS, t, HEAD_DIM), lambda i: (0, 0, 0)),
         pl.BlockSpec((A_KV_HEADS * HEAD_DIM, t), lambda i: (0, 0)), row(W_A)],
        (row(BW), pl.BlockSpec((A_HEADS, tq), lambda i: (0, i)), row(BW)),
        (SDS((t, BW), F32), SDS((A_HEADS, t), F32), SDS((t, BW), BF16)),
        name="attn_fwd", grid=(t // tq,), semantics=("parallel",), exchange=exchange)


def _attn_bwd(q, qT, k, kT, vT, pa, o, lse, dy, exchange=None):
    t = q.shape[0]
    tq = _tile(t, 256, LANES)
    grp = A_HEADS // A_KV_HEADS
    gw = grp * HEAD_DIM

    def body(q_ref, qT_ref, k_ref, kT_ref, vT_ref, z_ref, o_ref, lse_ref, dy_ref, dq_ref, dkT_ref, dvT_ref):
        @pl.when(pl.program_id(1) == 0)
        def _():
            dkT_ref[...] = jnp.zeros_like(dkT_ref)
            dvT_ref[...] = jnp.zeros_like(dvT_ref)

        z = z_ref[...].astype(F32)
        do = dy_ref[...] * (z * _sigmoid(z))
        doo = do * o_ref[...]
        doT = do.T
        kk, kT, vT = k_ref[...], kT_ref[...], vT_ref[...]
        heads = [slice(j * HEAD_DIM, (j + 1) * HEAD_DIM) for j in range(grp)]

        def scores(hs):
            return _dot(q_ref[:, hs], kT, NN), _dot(do[:, hs].astype(BF16), vT, NN)

        s, dp = scores(heads[0])
        for j, hs in enumerate(heads):
            nxt = scores(heads[j + 1]) if j + 1 < grp else None
            delta = jnp.sum(doo[:, hs], axis=-1, keepdims=True)
            p = jnp.exp(s - lse_ref[:, j:j + 1])
            ds_b = (p * (dp - delta)).astype(BF16)
            dq_ref[:, hs] = _dot(ds_b, kk, NN)
            dkT_ref[...] += _dot(qT_ref[hs, :], ds_b, NN)
            dvT_ref[...] += _dot(doT[hs, :].astype(BF16), p.astype(BF16), NN)
            if nxt is not None:
                s, dp = nxt

    grp_blk = pl.BlockSpec((tq, gw), lambda g, i: (i, g))
    kvT_blk = pl.BlockSpec((HEAD_DIM, t), lambda g, i: (g, 0))
    acc_blk = pl.BlockSpec((None, HEAD_DIM, t), lambda g, i: (g, 0, 0))
    return _host_call(
        body, (q, qT, k, kT, vT, pa, o, lse, dy),
        [grp_blk, pl.BlockSpec((gw, tq), lambda g, i: (g, i)), pl.BlockSpec((None, t, HEAD_DIM), lambda g, i: (g, 0, 0)),
         kvT_blk, kvT_blk, pl.BlockSpec((tq, gw), lambda g, i: (i, 768 // gw + g)), grp_blk,
         pl.BlockSpec((None, tq, grp), lambda g, i: (g, i, 0)), grp_blk],
        (grp_blk, acc_blk, acc_blk),
        (SDS((t, BW), F32), SDS((A_KV_HEADS, HEAD_DIM, t), F32), SDS((A_KV_HEADS, HEAD_DIM, t), F32)),
        name="attn_bwd", grid=(A_KV_HEADS, t // tq), semantics=("arbitrary", "arbitrary"), exchange=exchange)


def _attn_prep_bwd(pa, cs, sn, qg2, kg2, dq, dkT, dvT, dy, o, exchange=None):
    t = pa.shape[0]
    tq = _tile(t, 512, LANES)
    scale = HEAD_DIM ** -0.5

    def body(pa_ref, cs_ref, sn_ref, qg_ref, kg_ref, dq_ref, dkT_ref, dvT_ref, dy_ref, o_ref, dpa_ref, dqg_ref, dkg_ref):
        lane = lax.broadcasted_iota(jnp.int32, (tq, LANES), 1)
        c, s = cs_ref[...], sn_ref[...]

        @pl.when(pl.program_id(0) == 0)
        def _():
            dqg_ref[...] = jnp.zeros_like(dqg_ref)
            dkg_ref[...] = jnp.zeros_like(dkg_ref)

        def norm_rope_bwd(xg, g2, dout):
            r = lax.rsqrt(_head_sums(xg * xg, lane) * (1.0 / HEAD_DIM) + EPS)
            xh = xg * r
            dxn = dout * c + _swap16(dout * s, lane)
            gd = dxn * g2
            dx = r * (gd - xh * (_head_sums(gd * xh, lane) * (1.0 / HEAD_DIM)))
            return dx, jnp.sum(dxn * xh, axis=0, keepdims=True)

        for gi in range(4):
            sl = slice(gi * LANES, (gi + 1) * LANES)
            dx, dg = norm_rope_bwd(pa_ref[:, sl].astype(F32), qg_ref[...], dq_ref[:, sl] * scale)
            dpa_ref[:, sl] = dx.astype(BF16)
            dqg_ref[...] += dg
        dx, dg = norm_rope_bwd(pa_ref[:, 512:640].astype(F32), kg_ref[...], dkT_ref[...].T)
        dpa_ref[:, 512:640] = dx.astype(BF16)
        dkg_ref[...] += dg
        dpa_ref[:, 640:768] = dvT_ref[...].T.astype(BF16)
        z = pa_ref[:, 768:1280].astype(F32)
        _, dsilu = _silu_and_grad(z)
        dpa_ref[:, 768:1280] = (dy_ref[...] * o_ref[...] * dsilu).astype(BF16)

    row = lambda w: pl.BlockSpec((tq, w), lambda i: (i, 0))
    col = pl.BlockSpec((LANES, tq), lambda i: (0, i))
    vec = pl.BlockSpec((1, LANES), lambda i: (0, 0))
    return _host_call(
        body, (pa, cs, sn, qg2, kg2, dq, dkT, dvT, dy, o),
        [row(W_A), row(LANES), row(LANES), vec, vec, row(BW), col, col, row(BW), row(BW)],
        (row(W_A), vec, vec), (SDS((t, W_A), BF16), SDS((1, LANES), F32), SDS((1, LANES), F32)),
        name="attn_prep_bwd", grid=(t // tq,), semantics=("arbitrary",), exchange=exchange)


def _layer_norm(v, g, b):
    mu = jnp.mean(v, axis=-1, keepdims=True)
    xc = v - mu
    rs = lax.rsqrt(jnp.mean(xc * xc, axis=-1, keepdims=True) + EPS)
    xh = xc * rs
    return xh * g + b, xh, rs


def _gmlp_fwd(pb, lng, lnb, ws, bs):
    t = pb.shape[0]
    tb = _tile(t, 512, CHUNK)

    def body(pb_ref, g_ref, b_ref, ws_ref, bs_ref, y_ref):
        vln, _, _ = _layer_norm(pb_ref[:, BW:2 * BW].astype(F32), g_ref[...], b_ref[...])
        vb = vln.astype(BF16)
        for gi in range(B_GROUPS):
            w = ws_ref[gi].astype(BF16)
            cs_ = slice(gi * CHUNK, (gi + 1) * CHUNK)
            for n in range(tb // CHUNK):
                rs_ = slice(n * CHUNK, (n + 1) * CHUNK)
                mixed = _dot(w, vb[rs_, cs_], NN) + bs_ref[gi]
                z = pb_ref[rs_, 2 * BW + gi * CHUNK:2 * BW + (gi + 1) * CHUNK].astype(F32)
                y_ref[rs_, cs_] = (pb_ref[rs_, cs_].astype(F32) * mixed * (z * _sigmoid(z))).astype(BF16)

    return pl.pallas_call(
        body, name="gmlp_fwd", grid=(t // tb,),
        in_specs=[pl.BlockSpec((tb, W_B), lambda i: (i, 0)),
                  pl.BlockSpec((1, BW), lambda i: (0, 0)), pl.BlockSpec((1, BW), lambda i: (0, 0)),
                  pl.BlockSpec((B_GROUPS, CHUNK, CHUNK), lambda i: (0, 0, 0)),
                  pl.BlockSpec((B_GROUPS, CHUNK, 1), lambda i: (0, 0, 0))],
        out_specs=pl.BlockSpec((tb, BW), lambda i: (i, 0)),
        out_shape=SDS((t, BW), BF16), compiler_params=_params("parallel"),
    )(pb, lng, lnb, ws, bs)


def _gmlp_bwd(pb, lng, lnb, ws, bs, dy):
    t = pb.shape[0]
    tb = _tile(t, 256, CHUNK)

    def body(pb_ref, g_ref, b_ref, ws_ref, bs_ref, dy_ref, dpb_ref, dws_ref, dbs_ref, dg_ref, db_ref, dvln_ref):
        @pl.when(pl.program_id(0) == 0)
        def _():
            dws_ref[...] = jnp.zeros_like(dws_ref)
            dbs_ref[...] = jnp.zeros_like(dbs_ref)
            dg_ref[...] = jnp.zeros_like(dg_ref)
            db_ref[...] = jnp.zeros_like(db_ref)

        vln, xh, rs = _layer_norm(pb_ref[:, BW:2 * BW].astype(F32), g_ref[...], b_ref[...])
        vb = vln.astype(BF16)
        for gi in range(B_GROUPS):
            w = ws_ref[gi].astype(BF16)
            cs_ = slice(gi * CHUNK, (gi + 1) * CHUNK)
            for n in range(tb // CHUNK):
                rs_ = slice(n * CHUNK, (n + 1) * CHUNK)
                vbc = vb[rs_, cs_]
                mixed = _dot(w, vbc, NN) + bs_ref[gi]
                zs = slice(2 * BW + gi * CHUNK, 2 * BW + (gi + 1) * CHUNK)
                z = pb_ref[rs_, zs].astype(F32)
                u = pb_ref[rs_, cs_].astype(F32)
                sil, dsil = _silu_and_grad(z)
                dyc = dy_ref[rs_, cs_]
                dmixed = dyc * u * sil
                dpb_ref[rs_, cs_] = (dyc * mixed * sil).astype(BF16)
                dpb_ref[rs_, zs] = (dyc * u * mixed * dsil).astype(BF16)
                dmb = dmixed.astype(BF16)
                dws_ref[gi] += _dot(dmb, vbc, NT)
                dbs_ref[gi] += jnp.sum(dmixed, axis=-1, keepdims=True)
                dvln_ref[rs_, cs_] = _dot(w, dmb, TN)
        dvln = dvln_ref[...]
        dg_ref[...] += jnp.sum(dvln * xh, axis=0, keepdims=True)
        db_ref[...] += jnp.sum(dvln, axis=0, keepdims=True)
        gd = dvln * g_ref[...]
        dv = rs * (gd - jnp.mean(gd, axis=-1, keepdims=True) - xh * jnp.mean(gd * xh, axis=-1, keepdims=True))
        dpb_ref[:, BW:2 * BW] = dv.astype(BF16)

    vec = pl.BlockSpec((1, BW), lambda i: (0, 0))
    wsb = pl.BlockSpec((B_GROUPS, CHUNK, CHUNK), lambda i: (0, 0, 0))
    bsb = pl.BlockSpec((B_GROUPS, CHUNK, 1), lambda i: (0, 0, 0))
    return pl.pallas_call(
        body, name="gmlp_bwd", grid=(t // tb,),
        in_specs=[pl.BlockSpec((tb, W_B), lambda i: (i, 0)), vec, vec, wsb, bsb, pl.BlockSpec((tb, BW), lambda i: (i, 0))],
        out_specs=(pl.BlockSpec((tb, W_B), lambda i: (i, 0)), wsb, bsb, vec, vec),
        out_shape=(SDS((t, W_B), BF16), SDS((B_GROUPS, CHUNK, CHUNK), F32), SDS((B_GROUPS, CHUNK, 1), F32),
                   SDS((1, BW), F32), SDS((1, BW), F32)),
        scratch_shapes=[pltpu.VMEM((tb, BW), F32)],
        compiler_params=_params("arbitrary"),
    )(pb, lng, lnb, ws, bs, dy)


def _mem_softmax(qh, kh):
    s = _dot(qh, kh, NT) * (M_HEAD_DIM ** -0.5)
    e = jnp.exp(s - jnp.max(s, axis=-1, keepdims=True))
    return e / jnp.sum(e, axis=-1, keepdims=True)


def _mem_attn_fwd(pm, kv):
    t = pm.shape[0]
    tq = _tile(t, 512, 8)
    ml = kv.shape[0]

    def body(pm_ref, kv_ref, y_ref):
        for h in range(M_HEADS):
            hs = slice(h * M_HEAD_DIM, (h + 1) * M_HEAD_DIM)
            kh = kv_ref[:, hs].astype(BF16)
            vh = kv_ref[:, BW + h * M_HEAD_DIM:BW + (h + 1) * M_HEAD_DIM].astype(BF16)
            p = _mem_softmax(pm_ref[:, hs].astype(BF16), kh)
            o = _dot(p.astype(BF16), vh, NN)
            z = pm_ref[:, BW + h * M_HEAD_DIM:BW + (h + 1) * M_HEAD_DIM].astype(F32)
            y_ref[:, hs] = (o * (z * _sigmoid(z))).astype(BF16)

    return pl.pallas_call(
        body, name="mem_attn_fwd", grid=(t // tq,),
        in_specs=[pl.BlockSpec((tq, W_M), lambda i: (i, 0)), pl.BlockSpec((ml, 2 * BW), lambda i: (0, 0))],
        out_specs=pl.BlockSpec((tq, BW), lambda i: (i, 0)),
        out_shape=SDS((t, BW), BF16), compiler_params=_params("parallel"),
    )(pm, kv)


def _mem_attn_bwd(pm, kv, dy):
    t = pm.shape[0]
    tq = _tile(t, 512, 8)
    ml = kv.shape[0]
    scale = M_HEAD_DIM ** -0.5

    def body(pm_ref, kv_ref, dy_ref, dpm_ref, dkv_ref):
        @pl.when(pl.program_id(0) == 0)
        def _():
            dkv_ref[...] = jnp.zeros_like(dkv_ref)

        for h in range(M_HEADS):
            hs = slice(h * M_HEAD_DIM, (h + 1) * M_HEAD_DIM)
            zs = slice(BW + h * M_HEAD_DIM, BW + (h + 1) * M_HEAD_DIM)
            kh = kv_ref[:, hs].astype(BF16)
            vh = kv_ref[:, zs].astype(BF16)
            qh = pm_ref[:, hs].astype(BF16)
            p = _mem_softmax(qh, kh)
            pb = p.astype(BF16)
            o = _dot(pb, vh, NN)
            sil, dsil = _silu_and_grad(pm_ref[:, zs].astype(F32))
            dyh = dy_ref[:, hs]
            do = dyh * sil
            dpm_ref[:, zs] = (dyh * o * dsil).astype(BF16)
            delta = jnp.sum(do * o, axis=-1, keepdims=True)
            do_b = do.astype(BF16)
            dp = _dot(do_b, vh, NT)
            dr_b = (p * (dp - delta) * scale).astype(BF16)
            dpm_ref[:, hs] = _dot(dr_b, kh, NN).astype(BF16)
            dkv_ref[:, hs] += _dot(dr_b, qh, TN)
            dkv_ref[:, zs] += _dot(pb, do_b, TN)

    kvb = pl.BlockSpec((ml, 2 * BW), lambda i: (0, 0))
    return pl.pallas_call(
        body, name="mem_attn_bwd", grid=(t // tq,),
        in_specs=[pl.BlockSpec((tq, W_M), lambda i: (i, 0)), kvb, pl.BlockSpec((tq, BW), lambda i: (i, 0))],
        out_specs=(pl.BlockSpec((tq, W_M), lambda i: (i, 0)), kvb),
        out_shape=(SDS((t, W_M), BF16), SDS((ml, 2 * BW), F32)),
        compiler_params=_params("arbitrary"),
    )(pm, kv, dy)


def _merge_fwd(ya, yb, ym, pg, wbr, exchange=None):
    t = ya.shape[0]
    tm = _tile(t, 512, 8)

    def body(ya_ref, yb_ref, ym_ref, pg_ref, w_ref, m_ref):
        ups = [_dot(y_ref[...], w_ref[n], NN) for n, y_ref in enumerate((ya_ref, yb_ref, ym_ref))]
        acc = None
        for n, up in enumerate(ups):
            term = _sigmoid(pg_ref[:, n * D_MODEL:(n + 1) * D_MODEL].astype(F32)) * up
            acc = term if acc is None else acc + term
        m_ref[...] = acc.astype(BF16)

    yb_spec = pl.BlockSpec((tm, BW), lambda i: (i, 0))
    return _host_call(
        body, (ya, yb, ym, pg, wbr),
        [yb_spec, yb_spec, yb_spec, pl.BlockSpec((tm, W_G), lambda i: (i, 0)),
         pl.BlockSpec((3, BW, D_MODEL), lambda i: (0, 0, 0))],
        (pl.BlockSpec((tm, D_MODEL), lambda i: (i, 0)),), (SDS((t, D_MODEL), BF16),),
        name="merge_fwd", grid=(t // tm,), semantics=("parallel",), exchange=exchange)


def _merge_bwd(ya, yb, ym, pg, wbr, dm, exchange=None):
    t = ya.shape[0]
    tm = _tile(t, 512, 16)

    def body(ya_ref, yb_ref, ym_ref, pg_ref, w_ref, dm_ref, dya_ref, dyb_ref, dym_ref, dpg_ref, dw_ref):
        @pl.when(pl.program_id(0) == 0)
        def _():
            dw_ref[...] = jnp.zeros_like(dw_ref)

        dmf = dm_ref[...]
        branches = ((ya_ref, dya_ref), (yb_ref, dyb_ref), (ym_ref, dym_ref))

        def gate(n):
            gt = _sigmoid(pg_ref[:, n * D_MODEL:(n + 1) * D_MODEL].astype(F32))
            return gt, (dmf * gt).astype(BF16)

        gt, dup = gate(0)
        for n, (y_ref, dy_ref) in enumerate(branches):
            nxt = gate(n + 1) if n + 1 < len(branches) else None
            y, w = y_ref[...], w_ref[n]
            dy_ref[...] = _dot(dup, w, NT)
            dw_ref[n] += _dot(y, dup, TN)
            up = _dot(y, w, NN)
            dpg_ref[:, n * D_MODEL:(n + 1) * D_MODEL] = (dmf * up * gt * (1.0 - gt)).astype(BF16)
            if nxt is not None:
                gt, dup = nxt

    y_spec = pl.BlockSpec((tm, BW), lambda i: (i, 0))
    w_spec = pl.BlockSpec((3, BW, D_MODEL), lambda i: (0, 0, 0))
    return _host_call(
        body, (ya, yb, ym, pg, wbr, dm),
        [y_spec, y_spec, y_spec, pl.BlockSpec((tm, W_G), lambda i: (i, 0)), w_spec,
         pl.BlockSpec((tm, D_MODEL), lambda i: (i, 0))],
        (y_spec, y_spec, y_spec, pl.BlockSpec((tm, W_G), lambda i: (i, 0)), w_spec),
        (SDS((t, BW), F32), SDS((t, BW), F32), SDS((t, BW), F32), SDS((t, W_G), BF16), SDS((3, BW, D_MODEL), F32)),
        name="merge_bwd", grid=(t // tm,), semantics=("arbitrary",), exchange=exchange)


def _loss_head(x, g, target):
    t, d = x.shape
    tm = _tile(t, 512, 8)

    def body(x_ref, g_ref, t_ref, loss_ref, dx_ref, dg_ref):
        @pl.when(pl.program_id(0) == 0)
        def _():
            loss_ref[...] = jnp.zeros_like(loss_ref)
            dg_ref[...] = jnp.zeros_like(dg_ref)

        xf = x_ref[...]
        r = lax.rsqrt(jnp.mean(xf * xf, axis=-1, keepdims=True) + EPS)
        xh = xf * r
        err = xh * g_ref[...] - t_ref[...]
        per_tok = jnp.mean(err * err, axis=-1, keepdims=True)
        loss_ref[...] += 0.5 * jnp.sum(per_tok, axis=0, keepdims=True)
        dy = err * (1.0 / d)
        gd = dy * g_ref[...]
        dx_ref[...] = r * (gd - xh * jnp.mean(gd * xh, axis=-1, keepdims=True))
        dg_ref[...] += jnp.sum(dy * xh, axis=0, keepdims=True)

    row = pl.BlockSpec((tm, d), lambda i: (i, 0))
    vec = pl.BlockSpec((1, d), lambda i: (0, 0))
    return pl.pallas_call(
        body, name="loss_head", grid=(t // tm,),
        in_specs=[row, vec, row],
        out_specs=(pl.BlockSpec((1, 1), lambda i: (0, 0)), row, vec),
        out_shape=(SDS((1, 1), F32), SDS((t, d), F32), SDS((1, d), F32)),
        compiler_params=_params("arbitrary"),
    )(x, g, target)


def _layer_fwd(x, mem, w, tabs, next_shards=None):
    cs, sn = tabs
    riding = next_shards is not None
    h, pa, pb, pm, pg = _norm_in_proj(x, w["norm_g"], w["w_in_t"])
    q, qT, k, kT, vT = _attn_prep_fwd(pa, cs, sn, w["qg2"], w["kg2"])
    (o, lse, ya), gathered = _attn_fwd(qT, k, vT, pa, exchange=_gather_first_hop(next_shards) if riding else None)
    yb = _gmlp_fwd(pb, w["ln_g"], w["ln_b"], w["w_s"], w["b_s"])
    memn = _rmsnorm_fwd(mem, w["mem_g"], "mem_rmsnorm_fwd")
    kv = _mm(memn, w["w_kv"], "nn", "mem_kv")
    ym = _mem_attn_fwd(pm, kv)
    (merged,), gathered = _merge_fwd(ya, yb, ym, pg, w["w_br"], exchange=_gather_forward(gathered) if riding else None)
    x_next = _mm(merged, w["w_out"], "nn", "out_proj", add=x)
    saved = dict(x=x, h=h, pa=pa, pb=pb, pm=pm, pg=pg, q=q, qT=qT, k=k, kT=kT, vT=vT, o=o, lse=lse, ya=ya, yb=yb, ym=ym,
                 memn=memn, kv=kv, merged=merged)
    return x_next, saved, gathered


def _layer_bwd(dx_out, mem, w, s, tabs, pending=None, core=None, scatter_own=False):
    cs, sn = tabs
    t = dx_out.shape[0]
    riding = pending is not None
    dmerged = _mm(dx_out, w["w_out"], "nt", "d_merged")
    shard_rows = D_MODEL // N_DEV
    d_w_out = _mm_tn_by_owner(s["merged"], dx_out, "d_w_out")
    (dya, dyb, dym, dpg, d_w_br), recv = _merge_bwd(s["ya"], s["yb"], s["ym"], s["pg"], w["w_br"], dmerged,
                                                    exchange=_scatter_to_sibling(pending) if riding else None)
    parts = _pair_sums(BIG, pending, recv, core) if riding else None
    grads = dict(w_br=_split_w_br(d_w_br), w_out=d_w_out.reshape(N_DEV, 1, shard_rows, -1))
    early = [grads[n] for n in BIG[2:]]
    grp = A_HEADS // A_KV_HEADS
    lse_cols = s["lse"].reshape(A_KV_HEADS, grp, t).transpose(0, 2, 1)
    (dq, dkT, dvT), landed = _attn_bwd(
        s["q"], s["qT"], s["k"], s["kT"], s["vT"], s["pa"], s["o"], lse_cols, dya,
        exchange=_both(_scatter_to_chips(parts) if riding else None, _scatter_to_sibling(early) if scatter_own else None))
    from_chips, early_recv = (landed[:len(BIG)], landed[len(BIG):]) if riding else (None, landed)
    early_parts = _pair_sums(BIG[2:], early, early_recv, core) if scatter_own else None
    (dpa, d_qg2, d_kg2), early_from_chips = _attn_prep_bwd(
        s["pa"], cs, sn, w["qg2"], w["kg2"], dq, dkT.reshape(-1, t), dvT.reshape(-1, t), dya, s["o"],
        exchange=_scatter_to_chips(early_parts) if scatter_own else None)
    dpb, d_w_s, d_b_s, d_ln_g, d_ln_b = _gmlp_bwd(s["pb"], w["ln_g"], w["ln_b"], w["w_s"], w["b_s"], dyb)
    dpm, dkv = _mem_attn_bwd(s["pm"], s["kv"], dym)
    d_w_kv = _mm_tn_by_owner(s["memn"], dkv, "d_w_kv")
    dmemn = _mm(dkv, w["w_kv"], "nt", "d_memn")
    d_mem_g = _rmsnorm_dg(mem, w["mem_g"], dmemn, "mem_rmsnorm_bwd")
    dps = (dpa, dpb, dpm, dpg)
    d_w_in = _split_w_in_t(*_d_w_in_t(dps, s["h"]))
    grads.update(w_in=d_w_in[:, None], w_mem_kv=d_w_kv.reshape(N_DEV, 1, shard_rows, -1))
    own = None
    if scatter_own:
        late = [grads[n] for n in BIG[:2]]
        late_parts = _pair_sums(BIG[:2], late, _exchange_call(_scatter_to_sibling(late), "rs_sibling_swap"), core)
        own = _scatter_to_chips(late_parts)
    (dx_in, d_norm_g), late_from_chips = _dh_rmsnorm_bwd(dps, w["w_in_t"], s["x"], w["norm_g"], dx_out, exchange=own)
    if scatter_own:
        own_parts, own_from_chips = late_parts + early_parts, tuple(late_from_chips) + tuple(early_from_chips)
    grads.update(norm_g=d_norm_g[0], q_norm_g=d_qg2[0, :HEAD_DIM] + d_qg2[0, HEAD_DIM:],
                 k_norm_g=d_kg2[0, :HEAD_DIM] + d_kg2[0, HEAD_DIM:], sg_ln_g=d_ln_g[0], sg_ln_b=d_ln_b[0],
                 w_s=d_w_s, b_s=d_b_s[:, :, 0], mem_norm_g=d_mem_g[0])
    return (dx_in, grads, ((parts, from_chips) if riding else None),
            ((own_parts, own_from_chips) if scatter_own else None))


def _layer_weights(l, w_in_t, w_kv, w_br, w_out, small):
    tile2 = lambda g: jnp.tile(g.reshape(1, -1), (1, 2))
    return dict(
        w_in_t=w_in_t, w_kv=w_kv, w_br=w_br, w_out=w_out,
        norm_g=small["norm_g"][l].reshape(1, -1), qg2=tile2(small["q_norm_g"][l]), kg2=tile2(small["k_norm_g"][l]),
        ln_g=small["sg_ln_g"][l].reshape(1, -1), ln_b=small["sg_ln_b"][l].reshape(1, -1),
        w_s=small["w_s"][l], b_s=small["b_s"][l][:, :, None], mem_g=small["mem_norm_g"][l].reshape(1, -1))


def _position():
    x, y, c = lax.axis_index("x"), lax.axis_index("y"), lax.axis_index("c")
    return x, y, c, [(1 - x, y), (x, 1 - y), (1 - x, 1 - y)]


def _gather_first_hop(shards):
    n = len(shards)

    def make(x_refs, out_refs, sems):
        send_sems, recv_sems, local_sems = sems
        x, y, c, chips = _position()
        me = 4 * x + 2 * y + c
        peers = [(x, y, 1 - c)] + [(cx, cy, c) for cx, cy in chips]
        copies = [pltpu.make_async_copy(x_refs[t], out_refs[t].at[me], local_sems.at[t]) for t in range(n)]
        copies += [pltpu.make_async_remote_copy(
            src_ref=x_refs[t], dst_ref=out_refs[t].at[me], send_sem=send_sems.at[t, k], recv_sem=recv_sems.at[t, k],
            device_id=peer, device_id_type=MESH_ID) for t in range(n) for k, peer in enumerate(peers)]
        return copies

    return _Exchange(shards, [SDS((N_DEV,) + a.shape, a.dtype) for a in shards],
                     [pltpu.SemaphoreType.DMA((n, 4)), pltpu.SemaphoreType.DMA((n, 4)), pltpu.SemaphoreType.DMA((n,))], make)


def _gather_forward(gathered):
    n = len(gathered)

    def make(in_refs, out_refs, sems):
        send_sems, recv_sems = sems
        x, y, c, chips = _position()
        return [pltpu.make_async_remote_copy(
            src_ref=in_refs[t].at[4 * cx + 2 * cy + c], dst_ref=out_refs[t].at[4 * cx + 2 * cy + c],
            send_sem=send_sems.at[t, j], recv_sem=recv_sems.at[t, j], device_id=(x, y, 1 - c), device_id_type=MESH_ID)
            for t in range(n) for j, (cx, cy) in enumerate(chips)]

    return _Exchange(gathered, [SDS(a.shape, a.dtype) for a in gathered],
                     [pltpu.SemaphoreType.DMA((n, 3)), pltpu.SemaphoreType.DMA((n, 3))], make,
                     aliases={t: t for t in range(n)})


def _all_gather(shards, name):
    return _exchange_call(_gather_forward(_exchange_call(_gather_first_hop(shards), name + "_hop1")), name + "_hop2")


def _shard_segments(d):
    shard = IN_WIDTH // N_DEV
    lo, hi = d * shard, (d + 1) * shard
    out = []
    for g in range(4):
        a, b = max(lo, GROUP_OFFS[g]), min(hi, GROUP_OFFS[g + 1])
        if a < b:
            out.append((g, a - GROUP_OFFS[g], b - GROUP_OFFS[g], a - lo))
    return out


def _split_w_in_t(da, db, dm, dg):
    cols = da.shape[1]
    shard = IN_WIDTH // N_DEV
    tc = _tile(cols, 256)

    def body(a_ref, b_ref, m_ref, g_ref, o_ref):
        ins = (a_ref, b_ref, m_ref, g_ref)
        for d in range(N_DEV):
            for g, lo, hi, off in _shard_segments(d):
                o_ref[_core_major(d), off:off + hi - lo, :] = ins[g][lo:hi, :]

    return pl.pallas_call(
        body, name="split_w_in", grid=(cols // tc,),
        in_specs=[pl.BlockSpec((a.shape[0], tc), lambda j: (0, j)) for a in (da, db, dm, dg)],
        out_specs=pl.BlockSpec((N_DEV, shard, tc), lambda j: (0, 0, j)),
        out_shape=SDS((N_DEV, shard, cols), da.dtype), compiler_params=_params("parallel"),
    )(da, db, dm, dg)


def _assemble_w_br(gathered):
    _, nb, rows, shard = gathered.shape

    def body(g_ref, o_ref):
        for d in range(N_DEV):
            o_ref[:, :, d * shard:(d + 1) * shard] = g_ref[d]

    return pl.pallas_call(
        body, name="assemble_w_br", out_shape=SDS((nb, rows, N_DEV * shard), gathered.dtype),
        compiler_params=pltpu.CompilerParams(vmem_limit_bytes=VMEM_LIMIT),
    )(gathered)


def _split_w_br(dw):
    nb, rows, cols = dw.shape
    shard = cols // N_DEV

    def body(d_ref, o_ref):
        for d in range(N_DEV):
            o_ref[_core_major(d)] = d_ref[:, :, d * shard:(d + 1) * shard].astype(BF16)

    return pl.pallas_call(
        body, name="split_w_br", out_shape=SDS((N_DEV, nb, rows, shard), BF16),
        compiler_params=pltpu.CompilerParams(vmem_limit_bytes=VMEM_LIMIT),
    )(dw)


def _scatter_to_sibling(dests):
    n = len(dests)

    def make(d_refs, recv_refs, sems):
        send_sems, recv_sems = sems
        x, y, c, _ = _position()
        return [pltpu.make_async_remote_copy(
            src_ref=d_refs[t].at[pl.ds((1 - c) * 4, 4)], dst_ref=recv_refs[t], send_sem=send_sems.at[t],
            recv_sem=recv_sems.at[t], device_id=(x, y, 1 - c), device_id_type=MESH_ID) for t in range(n)]

    return _Exchange(dests, [SDS((4,) + a.shape[1:], a.dtype) for a in dests],
                     [pltpu.SemaphoreType.DMA((n,)), pltpu.SemaphoreType.DMA((n,))], make)


def _scatter_to_chips(parts):
    n = len(parts)

    def make(p_refs, recv_refs, sems):
        send_sems, recv_sems = sems
        _, _, c, chips = _position()
        return [pltpu.make_async_remote_copy(
            src_ref=p_refs[t].at[2 * cx + cy], dst_ref=recv_refs[t].at[k], send_sem=send_sems.at[t, k],
            recv_sem=recv_sems.at[t, k], device_id=(cx, cy, c), device_id_type=MESH_ID)
            for t in range(n) for k, (cx, cy) in enumerate(chips)]

    return _Exchange(parts, [SDS((3,) + a.shape[1:], a.dtype) for a in parts],
                     [pltpu.SemaphoreType.DMA((n, 3)), pltpu.SemaphoreType.DMA((n, 3))], make)


def _pair_sums(names, dests, recv, core):
    return [_pair_sum(d, r, core, "rs_pair_sum_" + n) for n, d, r in zip(names, dests, recv)]


def _pair_sum(dest, recv, core, name):
    _, na, r, cdim = dest.shape

    def body(core_ref, a_ref, b_ref, o_ref):
        o_ref[...] = (a_ref[...].astype(F32) + b_ref[...].astype(F32)).astype(o_ref.dtype)

    blk = pl.BlockSpec((None, na, r, cdim), lambda j, core_ref: (j, 0, 0, 0))
    return pl.pallas_call(
        body, name=name, out_shape=SDS(recv.shape, BF16),
        grid_spec=pltpu.PrefetchScalarGridSpec(
            num_scalar_prefetch=1, grid=(4,),
            in_specs=[pl.BlockSpec((None, na, r, cdim), lambda j, core_ref: (core_ref[0] * 4 + j, 0, 0, 0)), blk],
            out_specs=blk),
        compiler_params=_params("parallel"),
    )(core, dest, recv)


def _adamw_math(w, g, m, v):
    m = ADAM_B1 * m + (1.0 - ADAM_B1) * g
    v = ADAM_B2 * v + (1.0 - ADAM_B2) * (g * g)
    m_hat = m / (1.0 - ADAM_B1 ** ADAM_STEP)
    v_hat = v / (1.0 - ADAM_B2 ** ADAM_STEP)
    delta = -ADAM_LR * (m_hat / (jnp.sqrt(v_hat) + ADAM_EPS) + ADAM_WD * w)
    return delta, m, v


def _sum_and_adamw(parts, w, m, v, name):
    n, r, ln = parts.shape
    tr = _tile(r, 512, 16)

    def body(p_ref, w_ref, m_ref, v_ref, g_out, d_out, m_out, v_out):
        g = p_ref[0].astype(F32)
        for j in range(1, n):
            g = g + p_ref[j].astype(F32)
        delta, nm, nv = _adamw_math(w_ref[...], g, m_ref[...], v_ref[...])
        g_out[...] = g
        d_out[...] = delta
        m_out[...] = nm
        v_out[...] = nv

    blk = pl.BlockSpec((tr, ln), lambda i: (i, 0))
    return pl.pallas_call(
        body, name=name, grid=(r // tr,),
        in_specs=[pl.BlockSpec((n, tr, ln), lambda i: (0, i, 0)), blk, blk, blk],
        out_specs=(blk, blk, blk, blk), out_shape=tuple(SDS((r, ln), F32) for _ in range(4)),
        compiler_params=_params("parallel"),
    )(parts, w, m, v)


BIG = ("w_in", "w_mem_kv", "w_br", "w_out")
SMALL = ("norm_g", "q_norm_g", "k_norm_g", "sg_ln_g", "sg_ln_b", "w_s", "b_s", "mem_norm_g", "final_g")


def _pack(arrs, row_unit=16):
    flat = jnp.concatenate([a.reshape(-1) for a in arrs])
    pad = (-flat.shape[0]) % (row_unit * LANES)
    if pad:
        flat = jnp.concatenate([flat, jnp.zeros((pad,), flat.dtype)])
    return flat.reshape(-1, LANES)


def _unpack(buf, shapes):
    flat = buf.reshape(-1)
    out, off = [], 0
    for shp in shapes:
        n = 1
        for s_ in shp:
            n *= s_
        out.append(flat[off:off + n].reshape(shp))
        off += n
    return out


def _shard_sum_adamw(part, from_chips, chip, w, m, v, layer, prev, name):
    _, na, r, cdim = part.shape
    flat = w.ndim == 3
    assert not flat or na == 1
    tr = _tile(r, max(8, (256 * 1024) // (na * cdim)), 8)
    n_prev = 0 if prev is None else len(prev)

    def body(chip_ref, p_ref, f_ref, w_ref, m_ref, v_ref, *rest):
        g_out, d_out, m_out, v_out = rest[n_prev:]
        g = p_ref[...].astype(F32)
        for j in range(3):
            g = g + f_ref[j].astype(F32)
        delta, nm, nv = _adamw_math(w_ref[...], g, m_ref[...], v_ref[...])
        g_out[...] = g
        d_out[...] = delta
        m_out[...] = nm
        v_out[...] = nv

    a_blk = None if flat else na
    if flat:
        lay = pl.BlockSpec((None, tr, cdim), lambda i, chip_ref: (layer, i, 0))
    else:
        lay = pl.BlockSpec((None, na, tr, cdim), lambda i, chip_ref: (layer, 0, i, 0))
    return pl.pallas_call(
        body, name=name, out_shape=tuple(SDS(w.shape, F32) for _ in range(4)),
        grid_spec=pltpu.PrefetchScalarGridSpec(
            num_scalar_prefetch=1, grid=(r // tr,),
            in_specs=[pl.BlockSpec((None, a_blk, tr, cdim), lambda i, chip_ref: (chip_ref[0], 0, i, 0)),
                      pl.BlockSpec((3, a_blk, tr, cdim), lambda i, chip_ref: (0, 0, i, 0)), lay, lay, lay]
            + [HBM] * n_prev,
            out_specs=(lay, lay, lay, lay)),
        input_output_aliases={6 + j: j for j in range(n_prev)},
        compiler_params=_params("parallel"),
    )(chip, part, from_chips, w, m, v, *(prev or ()))


def kernel(x, mem, norm_g, w_in, q_norm_g, k_norm_g, sg_ln_g, sg_ln_b, w_s, b_s, mem_norm_g, w_mem_kv, w_br, w_out, final_g, loss_target, m_norm_g, m_w_in, m_q_norm_g, m_k_norm_g, m_sg_ln_g, m_sg_ln_b, m_w_s, m_b_s, m_mem_norm_g, m_w_mem_kv, m_w_br, m_w_out, m_final_g, v_norm_g, v_w_in, v_q_norm_g, v_k_norm_g, v_sg_ln_g, v_sg_ln_b, v_w_s, v_b_s, v_mem_norm_g, v_w_mem_kv, v_w_br, v_w_out, v_final_g):
    wts = dict(norm_g=norm_g, w_in=w_in, q_norm_g=q_norm_g, k_norm_g=k_norm_g, sg_ln_g=sg_ln_g, sg_ln_b=sg_ln_b,
               w_s=w_s, b_s=b_s, mem_norm_g=mem_norm_g, w_mem_kv=w_mem_kv, w_br=w_br, w_out=w_out, final_g=final_g)
    mom1 = dict(norm_g=m_norm_g, w_in=m_w_in, q_norm_g=m_q_norm_g, k_norm_g=m_k_norm_g, sg_ln_g=m_sg_ln_g,
                sg_ln_b=m_sg_ln_b, w_s=m_w_s, b_s=m_b_s, mem_norm_g=m_mem_norm_g, w_mem_kv=m_w_mem_kv, w_br=m_w_br,
                w_out=m_w_out, final_g=m_final_g)
    mom2 = dict(norm_g=v_norm_g, w_in=v_w_in, q_norm_g=v_q_norm_g, k_norm_g=v_k_norm_g, sg_ln_g=v_sg_ln_g,
                sg_ln_b=v_sg_ln_b, w_s=v_w_s, b_s=v_b_s, mem_norm_g=v_mem_norm_g, w_mem_kv=v_w_mem_kv, w_br=v_w_br,
                w_out=v_w_out, final_g=v_final_g)
    dp = w_in.shape[0]
    core = lax.axis_index("c").astype(jnp.int32).reshape(1)
    chip = (2 * lax.axis_index("x") + lax.axis_index("y")).astype(jnp.int32).reshape(1)

    t_in = lambda a: jnp.swapaxes(a, 1, 2)
    wts, mom1, mom2 = [dict(d_, w_in=t_in(d_["w_in"])) for d_ in (wts, mom1, mom2)]

    shard_bf = {n: wts[n].astype(BF16) for n in BIG}
    shards = lambda l: [shard_bf[n][l] for n in BIG]
    x_l, mem_l = x[0], mem[0]
    tabs = _rope_tables(x_l.shape[0])

    gathered = _all_gather(shards(0), "weights_all_gather")
    layers, saved = [], []
    for l in range(dp):
        g_in, g_kv, g_br, g_out = gathered
        layers.append(_layer_weights(l, g_in.reshape(IN_WIDTH, -1), g_kv.reshape(D_MODEL, -1), _assemble_w_br(g_br),
                                     g_out.reshape(D_MODEL, -1), wts))
        x_l, s, gathered = _layer_fwd(x_l, mem_l, layers[l], tabs, next_shards=shards(l + 1) if l + 1 < dp else None)
        saved.append(s)
    loss_local, dx, d_final_g = _loss_head(x_l, final_g.reshape(1, -1), loss_target[0])
    loss = lax.psum(loss_local[0, 0], AXES)

    def finish(l, parts, from_chips, prev):
        return {n: _shard_sum_adamw(p, f, chip, wts[n], mom1[n], mom2[n], l, None if prev is None else prev[n],
                                    "sum_adamw_" + n)
                for n, p, f in zip(BIG, parts, from_chips)}

    grads, updated, pending = [None] * dp, None, None
    for l in reversed(range(dp)):
        dx, grads[l], scattered, own = _layer_bwd(dx, mem_l, layers[l], saved[l], tabs, pending=pending, core=core,
                                                  scatter_own=(l == 0))
        if scattered is not None:
            updated = finish(l + 1, *scattered, updated)
        pending = [grads[l][n] for n in BIG]
    updated = finish(0, *own, updated)
    grad_x = dx
    big_out = [{n: (t_in(updated[n][k]) if n == "w_in" else updated[n][k]) for n in BIG} for k in range(4)]

    small_g = {n: jnp.stack([g[n] for g in grads]) for n in SMALL if n != "final_g"}
    small_g["final_g"] = d_final_g
    (all_small,) = _all_gather([_pack([small_g[n] for n in SMALL])], "small_all_gather")
    small_bufs = _sum_and_adamw(
        all_small, _pack([wts[n] for n in SMALL]), _pack([mom1[n] for n in SMALL]), _pack([mom2[n] for n in SMALL]),
        "small_sum_adamw")

    outs = []
    for big_vals, small_buf in zip(big_out, small_bufs):
        vals = dict(big_vals)
        vals.update(zip(SMALL, _unpack(small_buf, [wts[n].shape for n in SMALL])))
        outs.append(vals)
    order = ("norm_g", "w_in", "q_norm_g", "k_norm_g", "sg_ln_g", "sg_ln_b", "w_s", "b_s", "mem_norm_g", "w_mem_kv",
             "w_br", "w_out", "final_g")
    result = [loss, grad_x[None]]
    for vals in outs:
        result += [vals[n] for n in order]
    return tuple(result)
```

```python
import functools

import jax
import jax.numpy as jnp
from jax import lax
from jax.experimental import pallas as pl
from jax.experimental.pallas import tpu as pltpu

F32 = jnp.float32
BF16 = jnp.bfloat16
SDS = jax.ShapeDtypeStruct
MESH_ID = pl.DeviceIdType.MESH
AXES = ("x", "y", "c")
N_DEV = 8

D_MODEL = 1024
DEPTH = 4
GRID_W = 64
CHUNK = 128
ROPE_THETA = 10000.0
EPS = 1e-6
HEAD_DIM = 64
A_HEADS = 8
A_KV_HEADS = 2
B_GROUPS = 4
M_HEADS = 4
M_HEAD_DIM = 128
BW = 512
W_A = 1280
W_B = 1536
W_M = 1024
W_G = 3072
IN_WIDTH = W_A + W_B + W_M + W_G
GROUP_OFFS = (0, W_A, W_A + W_B, W_A + W_B + W_M, IN_WIDTH)

ADAM_LR = 0.001
ADAM_B1 = 0.9
ADAM_B2 = 0.999
ADAM_EPS = 1e-08
ADAM_WD = 0.01
ADAM_STEP = 10

LANES = 128
KEY_CHUNK = 2048
VMEM_LIMIT = 52 * 1024 * 1024


def _tile(n, cap, unit=LANES):
    if n <= cap:
        return n
    t = (cap // unit) * unit
    while n % t:
        t -= unit
    return t


def _params(*sem):
    return pltpu.CompilerParams(dimension_semantics=sem, vmem_limit_bytes=VMEM_LIMIT)


def _sigmoid(z):
    return 0.5 * jnp.tanh(0.5 * z) + 0.5


def _silu_and_grad(z):
    s = _sigmoid(z)
    return z * s, s * (1.0 + z * (1.0 - s))


def _dot(a, b, dims):
    return lax.dot_general(a, b, (dims, ((), ())), preferred_element_type=F32)


NN = ((1,), (0,))
NT = ((1,), (1,))
TN = ((0,), (0,))
HBM = pl.BlockSpec(memory_space=pl.ANY)


class _Exchange:
    def __init__(self, ins, out_shapes, sems, make, aliases=None):
        self.ins, self.out_shapes, self.sems, self.make = list(ins), list(out_shapes), list(sems), make
        self.aliases = dict(aliases or {})

    def start(self, in_refs, out_refs, sems):
        for cp in self.make(in_refs, out_refs, sems):
            cp.start()

    def finish(self, in_refs, out_refs, sems):
        for cp in self.make(in_refs, out_refs, sems):
            cp.wait()


def _both(a, b):
    if a is None or b is None:
        return a if b is None else b
    n_in, n_out, n_sem = len(a.ins), len(a.out_shapes), len(a.sems)

    def make(in_refs, out_refs, sems):
        return (a.make(in_refs[:n_in], out_refs[:n_out], sems[:n_sem])
                + b.make(in_refs[n_in:], out_refs[n_out:], sems[n_sem:]))

    aliases = dict(a.aliases)
    aliases.update({n_in + i: n_out + o for i, o in b.aliases.items()})
    return _Exchange(a.ins + b.ins, a.out_shapes + b.out_shapes, a.sems + b.sems, make, aliases)


def _host_call(body, ins, in_specs, out_specs, out_shape, *, name, grid, semantics, scratch=(), exchange=None):
    ins, out_shape, scratch = list(ins), list(out_shape), list(scratch)
    if exchange is None:
        res = pl.pallas_call(
            body, name=name, grid=grid, in_specs=list(in_specs), out_specs=tuple(out_specs), out_shape=tuple(out_shape),
            scratch_shapes=scratch, compiler_params=_params(*semantics))(*ins)
        return tuple(res), ()
    n_in, n_out, n_scr = len(ins), len(out_shape), len(scratch)
    x_in, x_out = len(exchange.ins), len(exchange.out_shapes)

    def carrying(*refs):
        o0 = n_in + x_in
        s0 = o0 + n_out + x_out
        c_in, c_out, c_sems = refs[n_in:o0], refs[o0 + n_out:s0], refs[s0 + n_scr:]
        ids = [pl.program_id(a) for a in range(len(grid))]
        first = functools.reduce(jnp.logical_and, [i == 0 for i in ids])
        last = functools.reduce(jnp.logical_and, [i == g - 1 for i, g in zip(ids, grid)])

        @pl.when(first)
        def _():
            exchange.start(c_in, c_out, c_sems)

        body(*refs[:n_in], *refs[o0:o0 + n_out], *refs[s0:s0 + n_scr])

        @pl.when(last)
        def _():
            exchange.finish(c_in, c_out, c_sems)

    res = pl.pallas_call(
        carrying, name=name, grid=grid, in_specs=list(in_specs) + [HBM] * x_in,
        out_specs=tuple(out_specs) + (HBM,) * x_out, out_shape=tuple(out_shape) + tuple(exchange.out_shapes),
        scratch_shapes=scratch + exchange.sems,
        input_output_aliases={n_in + i: n_out + o for i, o in exchange.aliases.items()},
        compiler_params=_params(*(["arbitrary"] * len(grid))))(*ins, *exchange.ins)
    return tuple(res[:n_out]), tuple(res[n_out:])


def _exchange_call(exchange, name):
    x_in = len(exchange.ins)

    def body(*refs):
        x_out = len(exchange.out_shapes)
        c_in, c_out, c_sems = refs[:x_in], refs[x_in:x_in + x_out], refs[x_in + x_out:]
        exchange.start(c_in, c_out, c_sems)
        exchange.finish(c_in, c_out, c_sems)

    return pl.pallas_call(
        body, name=name, in_specs=[HBM] * x_in, out_specs=tuple([HBM] * len(exchange.out_shapes)),
        out_shape=tuple(exchange.out_shapes), scratch_shapes=exchange.sems, input_output_aliases=exchange.aliases,
    )(*exchange.ins)


def _mm(a, b, mode, name, add=None, out_dtype=F32, tm_cap=512, tn_cap=1536, tk_cap=1536):
    (m, k), (n, k2) = a.shape, (b.shape[::-1] if mode == "nn" else b.shape)
    assert k == k2 and mode in ("nn", "nt"), (a.shape, b.shape, mode)
    tm = _tile(m, tm_cap, 8)
    tn = _tile(n, tn_cap)
    tk = _tile(k, tk_cap)
    nk = k // tk
    dims = {"nn": NN, "nt": NT}[mode]
    a_spec = pl.BlockSpec((tm, tk), lambda i, j, kk: (i, kk))
    b_spec = {"nn": pl.BlockSpec((tk, tn), lambda i, j, kk: (kk, j)),
              "nt": pl.BlockSpec((tn, tk), lambda i, j, kk: (j, kk))}[mode]
    o_spec = pl.BlockSpec((tm, tn), lambda i, j, kk: (i, j))
    has_add = add is not None

    def body(*refs):
        a_ref, b_ref = refs[0], refs[1]
        add_ref = refs[2] if has_add else None
        o_ref = refs[3] if has_add else refs[2]
        part = _dot(a_ref[...].astype(BF16), b_ref[...].astype(BF16), dims)
        if nk == 1:
            if has_add:
                part = part + add_ref[...]
            o_ref[...] = part.astype(out_dtype)
        else:
            acc = refs[-1]
            kk = pl.program_id(2)

            @pl.when(kk == 0)
            def _():
                acc[...] = part + add_ref[...] if has_add else part

            @pl.when(kk > 0)
            def _():
                acc[...] += part

            @pl.when(kk == nk - 1)
            def _():
                o_ref[...] = acc[...].astype(out_dtype)

    ins = [a, b] + ([add] if has_add else [])
    in_specs = [a_spec, b_spec] + ([o_spec] if has_add else [])
    return pl.pallas_call(
        body, name=name, grid=(m // tm, n // tn, nk), in_specs=in_specs, out_specs=o_spec,
        out_shape=SDS((m, n), out_dtype),
        scratch_shapes=[pltpu.VMEM((tm, tn), F32)] if nk > 1 else [],
        compiler_params=_params("parallel", "parallel", "arbitrary"),
    )(*ins)


def _mm_tn(a, b, name):
    (k, m), (k2, n) = a.shape, b.shape
    assert k == k2, (a.shape, b.shape)
    tk = _tile(k, 2048, 16)
    nk = k // tk

    def body(a_ref, b_ref, o_ref, acc):
        kk = pl.program_id(0)
        part = _dot(a_ref[...].astype(BF16), b_ref[...].astype(BF16), TN)

        @pl.when(kk == 0)
        def _():
            acc[...] = part

        @pl.when(kk > 0)
        def _():
            acc[...] += part

        @pl.when(kk == nk - 1)
        def _():
            o_ref[...] = acc[...].astype(BF16)

    return pl.pallas_call(
        body, name=name, grid=(nk,),
        in_specs=[pl.BlockSpec((tk, m), lambda kk: (kk, 0)), pl.BlockSpec((tk, n), lambda kk: (kk, 0))],
        out_specs=pl.BlockSpec((m, n), lambda kk: (0, 0)), out_shape=SDS((m, n), BF16),
        scratch_shapes=[pltpu.VMEM((m, n), F32)], compiler_params=_params("arbitrary"),
    )(a, b)


def _rmsnorm_fwd(x, g, name):
    t, d = x.shape
    tm = _tile(t, 512, 8)

    def body(x_ref, g_ref, h_ref):
        xf = x_ref[...]
        r = lax.rsqrt(jnp.mean(xf * xf, axis=-1, keepdims=True) + EPS)
        h_ref[...] = (xf * r * g_ref[...]).astype(BF16)

    return pl.pallas_call(
        body, name=name, grid=(t // tm,),
        in_specs=[pl.BlockSpec((tm, d), lambda i: (i, 0)), pl.BlockSpec((1, d), lambda i: (0, 0))],
        out_specs=pl.BlockSpec((tm, d), lambda i: (i, 0)),
        out_shape=SDS((t, d), BF16), compiler_params=_params("parallel"),
    )(x, g)


def _rmsnorm_bwd_math(xf, g, dh):
    r = lax.rsqrt(jnp.mean(xf * xf, axis=-1, keepdims=True) + EPS)
    xh = xf * r
    gd = dh * g
    dx = r * (gd - xh * jnp.mean(gd * xh, axis=-1, keepdims=True))
    return dx, jnp.sum(dh * xh, axis=0, keepdims=True)


def _rmsnorm_dg(x, g, dh, name):
    t, d = x.shape
    tm = _tile(t, 512, 8)

    def body(x_ref, g_ref, dh_ref, dg_ref):
        @pl.when(pl.program_id(0) == 0)
        def _():
            dg_ref[...] = jnp.zeros_like(dg_ref)

        dg_ref[...] += _rmsnorm_bwd_math(x_ref[...], g_ref[...], dh_ref[...])[1]

    row = pl.BlockSpec((tm, d), lambda i: (i, 0))
    vec = pl.BlockSpec((1, d), lambda i: (0, 0))
    return pl.pallas_call(
        body, name=name, grid=(t // tm,), in_specs=[row, vec, row], out_specs=vec, out_shape=SDS((1, d), F32),
        compiler_params=_params("arbitrary"),
    )(x, g, dh)


def _group_rows(p):
    return slice(GROUP_OFFS[p], GROUP_OFFS[p + 1])


def _d_w_in_t(dps, h):
    t, d = h.shape
    tm = 256
    assert all(off % tm == 0 for off in GROUP_OFFS)
    tiles = [a.shape[1] // tm for a in dps]
    starts = [sum(tiles[:p]) for p in range(len(dps))]

    def body(*refs):
        a_refs, h_ref, o_ref = refs[:len(dps)], refs[len(dps)], refs[len(dps) + 1]
        i = pl.program_id(0)
        for p, a_ref in enumerate(a_refs):
            @pl.when((i >= starts[p]) & (i < starts[p] + tiles[p]))
            def _():
                o_ref[...] = _dot(a_ref[...], h_ref[...], TN).astype(BF16)

    def tile_of(p):
        return lambda i: jnp.clip(i - starts[p], 0, tiles[p] - 1)

    return pl.pallas_call(
        body, name="d_w_in", grid=(sum(tiles),),
        in_specs=[pl.BlockSpec((t, tm), lambda i, p=p: (0, tile_of(p)(i))) for p in range(len(dps))]
        + [pl.BlockSpec((t, d), lambda i: (0, 0), pipeline_mode=pl.Buffered(1))],
        out_specs=pl.BlockSpec((tm, d), lambda i: (i, 0)),
        out_shape=SDS((sum(tiles) * tm, d), BF16), compiler_params=_params("arbitrary"),
    )(*dps, h)


def _norm_in_proj(x, g, w_t):
    t, d = x.shape
    tm = _tile(t, 512, 16)
    widths = [GROUP_OFFS[p + 1] - GROUP_OFFS[p] for p in range(4)]

    def body(x_ref, g_ref, w_ref, h_ref, *outs):
        xf = x_ref[...]
        r = lax.rsqrt(jnp.mean(xf * xf, axis=-1, keepdims=True) + EPS)
        hb = (xf * r * g_ref[...]).astype(BF16)
        h_ref[...] = hb
        for p, o_ref in enumerate(outs):
            o_ref[...] = _dot(hb, w_ref[_group_rows(p), :], NT).astype(BF16)

    row = lambda wd: pl.BlockSpec((tm, wd), lambda i: (i, 0))
    return pl.pallas_call(
        body, name="norm_in_proj", grid=(t // tm,),
        in_specs=[row(d), pl.BlockSpec((1, d), lambda i: (0, 0)),
                  pl.BlockSpec(w_t.shape, lambda i: (0, 0), pipeline_mode=pl.Buffered(1))],
        out_specs=(row(d),) + tuple(row(wd) for wd in widths),
        out_shape=(SDS((t, d), BF16),) + tuple(SDS((t, wd), BF16) for wd in widths),
        compiler_params=_params("parallel"),
    )(x, g, w_t)


def _dh_rmsnorm_bwd(dps, w_t, x, g, dres, exchange=None):
    t, d = x.shape
    tm = _tile(t, 256, 16)
    n = len(dps)

    def body(*refs):
        dp_refs, w_ref = refs[:n], refs[n]
        x_ref, g_ref, dres_ref, dx_ref, dg_ref = refs[n + 1:]
        dh = None
        for p, dp_ref in enumerate(dp_refs):
            part = _dot(dp_ref[...], w_ref[_group_rows(p), :], NN)
            dh = part if dh is None else dh + part
        dx, dg = _rmsnorm_bwd_math(x_ref[...], g_ref[...], dh)
        dx_ref[...] = dx + dres_ref[...]

        @pl.when(pl.program_id(0) == 0)
        def _():
            dg_ref[...] = jnp.zeros_like(dg_ref)

        dg_ref[...] += dg

    row = pl.BlockSpec((tm, d), lambda i: (i, 0))
    vec = pl.BlockSpec((1, d), lambda i: (0, 0))
    return _host_call(
        body, list(dps) + [w_t, x, g, dres],
        [pl.BlockSpec((tm, a.shape[1]), lambda i: (i, 0)) for a in dps]
        + [pl.BlockSpec(w_t.shape, lambda i: (0, 0), pipeline_mode=pl.Buffered(1)), row, vec, row],
        (row, vec), (SDS((t, d), F32), SDS((1, d), F32)),
        name="dh_rmsnorm_bwd", grid=(t // tm,), semantics=("arbitrary",), exchange=exchange)


def _rope_tables(t):
    rows = t // GRID_W
    row = jnp.repeat(jnp.arange(rows, dtype=F32), GRID_W)
    col = jnp.tile(jnp.arange(GRID_W, dtype=F32), rows)
    n_freq = HEAD_DIM // 4
    inv = ROPE_THETA ** (-jnp.arange(n_freq, dtype=F32) / n_freq)
    ang = jnp.stack([row[:, None] * inv, col[:, None] * inv], axis=1)
    cos, sin = jnp.cos(ang), jnp.sin(ang)
    c64 = jnp.concatenate([cos[:, 0], cos[:, 0], cos[:, 1], cos[:, 1]], axis=-1)
    s64 = jnp.concatenate([-sin[:, 0], sin[:, 0], -sin[:, 1], sin[:, 1]], axis=-1)
    return jnp.tile(c64, (1, 2)), jnp.tile(s64, (1, 2))


def _head_sums(v, lane):
    lo = jnp.sum(jnp.where(lane < HEAD_DIM, v, 0.0), axis=-1, keepdims=True)
    hi = jnp.sum(jnp.where(lane < HEAD_DIM, 0.0, v), axis=-1, keepdims=True)
    return jnp.where(lane < HEAD_DIM, lo, hi)


def _swap16(v, lane):
    return jnp.where((lane % 32) < 16, pltpu.roll(v, LANES - 16, 1), pltpu.roll(v, 16, 1))


def _attn_prep_fwd(pa, cs, sn, qg2, kg2):
    t = pa.shape[0]
    tq = _tile(t, 512, LANES)
    scale = HEAD_DIM ** -0.5

    def body(pa_ref, cs_ref, sn_ref, qg_ref, kg_ref, q_ref, qT_ref, k_ref, kT_ref, vT_ref):
        lane = lax.broadcasted_iota(jnp.int32, (tq, LANES), 1)
        c, s = cs_ref[...], sn_ref[...]

        def norm_rope(xg, g2):
            r = lax.rsqrt(_head_sums(xg * xg, lane) * (1.0 / HEAD_DIM) + EPS)
            xn = xg * r * g2
            return xn * c + _swap16(xn, lane) * s

        for gi in range(4):
            sl = slice(gi * LANES, (gi + 1) * LANES)
            qr = norm_rope(pa_ref[:, sl].astype(F32), qg_ref[...]) * scale
            q_ref[:, sl] = qr.astype(BF16)
            qT_ref[sl, :] = qr.T.astype(BF16)
        kr = norm_rope(pa_ref[:, 512:640].astype(F32), kg_ref[...])
        kT_ref[...] = kr.T.astype(BF16)
        vT_ref[...] = pa_ref[:, 640:768].astype(F32).T.astype(BF16)
        kr = kr.astype(BF16)
        for kvh in range(A_KV_HEADS):
            k_ref[kvh] = kr[:, kvh * HEAD_DIM:(kvh + 1) * HEAD_DIM]

    row = lambda w: pl.BlockSpec((tq, w), lambda i: (i, 0))
    col = lambda r: pl.BlockSpec((r, tq), lambda i: (0, i))
    vec = pl.BlockSpec((1, LANES), lambda i: (0, 0))
    hm = pl.BlockSpec((A_KV_HEADS, tq, HEAD_DIM), lambda i: (0, i, 0))
    return pl.pallas_call(
        body, name="attn_prep_fwd", grid=(t // tq,),
        in_specs=[row(W_A), row(LANES), row(LANES), vec, vec],
        out_specs=(row(BW), col(BW), hm, col(LANES), col(LANES)),
        out_shape=(SDS((t, BW), BF16), SDS((BW, t), BF16), SDS((A_KV_HEADS, t, HEAD_DIM), BF16),
                   SDS((LANES, t), BF16), SDS((LANES, t), BF16)),
        compiler_params=_params("parallel"),
    )(pa, cs, sn, qg2, kg2)


def _attn_fwd(qT, k, vT, pa, exchange=None):
    t = qT.shape[1]
    tq = _tile(t, 256, LANES)
    grp = A_HEADS // A_KV_HEADS

    ck = _tile(t, KEY_CHUNK, LANES)

    def body(qT_ref, k_ref, vT_ref, pa_ref, o_ref, lse_ref, y_ref):
        def finish_pair(pair, o_pair):
            o2 = jnp.concatenate(o_pair, axis=0).T
            cols = slice(pair * LANES, (pair + 1) * LANES)
            o_ref[:, cols] = o2
            z = pa_ref[:, 768 + pair * LANES:768 + (pair + 1) * LANES].astype(F32)
            y_ref[:, cols] = (o2 * (z * _sigmoid(z))).astype(BF16)

        n_c = t // ck
        items = [(h, c) for h in range(A_HEADS) for c in range(n_c)]

        def scores(h, c):
            return _dot(k_ref[h // grp, c * ck:(c + 1) * ck, :], qT_ref[h * HEAD_DIM:(h + 1) * HEAD_DIM, :], NN)

        def weighted_values(h, c, pT):
            kvh = h // grp
            return _dot(vT_ref[kvh * HEAD_DIM:(kvh + 1) * HEAD_DIM, c * ck:(c + 1) * ck], pT, NN)

        o_pair, sT, pending, acc = [], scores(*items[0]), None, None
        for i in range(len(items) + 1):
            item = items[i] if i < len(items) else None
            sT_next = scores(*items[i + 1]) if i + 1 < len(items) else None
            if pending is not None:
                (ph, pc, pT, alpha, l_run) = pending
                o_c = weighted_values(ph, pc, pT)
                acc = o_c if alpha is None else acc * alpha + o_c
                if pc == n_c - 1:
                    o_pair.append(acc / l_run)
                    if ph % 2:
                        finish_pair(ph // 2, o_pair)
                        o_pair = []
                pending = None
            if item is not None:
                h, c = item
                m_c = jnp.max(sT, axis=0, keepdims=True)
                if c == 0:
                    m_new, alpha = m_c, None
                else:
                    m_new = jnp.maximum(m, m_c)
                    alpha = jnp.exp(m - m_new)
                pT = jnp.exp(sT - m_new)
                l_c = jnp.sum(pT, axis=0, keepdims=True)
                l = l_c if c == 0 else l * alpha + l_c
                m = m_new
                if c == n_c - 1:
                    lse_ref[h:h + 1, :] = m + jnp.log(l)
                pending = (h, c, pT.astype(BF16), alpha, l)
            sT = sT_next

    row = lambda w: pl.BlockSpec((tq, w), lambda i: (i, 0))
    return _host_call(
        body, (qT, k, vT, pa),
        [pl.BlockSpec((BW, tq), lambda i: (0, i)), pl.BlockSpec((A_KV_HEADS, t, HEAD_DIM), lambda i: (0, 0, 0)),
         pl.BlockSpec((A_KV_HEADS * HEAD_DIM, t), lambda i: (0, 0)), row(W_A)],
        (row(BW), pl.BlockSpec((A_HEADS, tq), lambda i: (0, i)), row(BW)),
        (SDS((t, BW), F32), SDS((A_HEADS, t), F32), SDS((t, BW), BF16)),
        name="attn_fwd", grid=(t // tq,), semantics=("parallel",), exchange=exchange)


def _attn_bwd(q, qT, k, kT, vT, pa, o, lse, dy, exchange=None):
    t = q.shape[0]
    tq = _tile(t, 256, LANES)
    grp = A_HEADS // A_KV_HEADS
    gw = grp * HEAD_DIM

    def body(q_ref, qT_ref, k_ref, kT_ref, vT_ref, z_ref, o_ref, lse_ref, dy_ref, dq_ref, dkT_ref, dvT_ref):
        @pl.when(pl.program_id(1) == 0)
        def _():
            dkT_ref[...] = jnp.zeros_like(dkT_ref)
            dvT_ref[...] = jnp.zeros_like(dvT_ref)

        z = z_ref[...].astype(F32)
        do = dy_ref[...] * (z * _sigmoid(z))
        doo = do * o_ref[...]
        doT = do.T
        kk, kT, vT = k_ref[...], kT_ref[...], vT_ref[...]
        heads = [slice(j * HEAD_DIM, (j + 1) * HEAD_DIM) for j in range(grp)]

        def scores(hs):
            return _dot(q_ref[:, hs], kT, NN), _dot(do[:, hs].astype(BF16), vT, NN)

        s, dp = scores(heads[0])
        for j, hs in enumerate(heads):
            nxt = scores(heads[j + 1]) if j + 1 < grp else None
            delta = jnp.sum(doo[:, hs], axis=-1, keepdims=True)
            p = jnp.exp(s - lse_ref[:, j:j + 1])
            ds_b = (p * (dp - delta)).astype(BF16)
            dq_ref[:, hs] = _dot(ds_b, kk, NN)
            dkT_ref[...] += _dot(qT_ref[hs, :], ds_b, NN)
            dvT_ref[...] += _dot(doT[hs, :].astype(BF16), p.astype(BF16), NN)
            if nxt is not None:
                s, dp = nxt

    grp_blk = pl.BlockSpec((tq, gw), lambda g, i: (i, g))
    kvT_blk = pl.BlockSpec((HEAD_DIM, t), lambda g, i: (g, 0))
    acc_blk = pl.BlockSpec((None, HEAD_DIM, t), lambda g, i: (g, 0, 0))
    return _host_call(
        body, (q, qT, k, kT, vT, pa, o, lse, dy),
        [grp_blk, pl.BlockSpec((gw, tq), lambda g, i: (g, i)), pl.BlockSpec((None, t, HEAD_DIM), lambda g, i: (g, 0, 0)),
         kvT_blk, kvT_blk, pl.BlockSpec((tq, gw), lambda g, i: (i, 768 // gw + g)), grp_blk,
         pl.BlockSpec((None, tq, grp), lambda g, i: (g, i, 0)), grp_blk],
        (grp_blk, acc_blk, acc_blk),
        (SDS((t, BW), F32), SDS((A_KV_HEADS, HEAD_DIM, t), F32), SDS((A_KV_HEADS, HEAD_DIM, t), F32)),
        name="attn_bwd", grid=(A_KV_HEADS, t // tq), semantics=("arbitrary", "arbitrary"), exchange=exchange)


def _attn_prep_bwd(pa, cs, sn, qg2, kg2, dq, dkT, dvT, dy, o, exchange=None):
    t = pa.shape[0]
    tq = _tile(t, 512, LANES)
    scale = HEAD_DIM ** -0.5

    def body(pa_ref, cs_ref, sn_ref, qg_ref, kg_ref, dq_ref, dkT_ref, dvT_ref, dy_ref, o_ref, dpa_ref, dqg_ref, dkg_ref):
        lane = lax.broadcasted_iota(jnp.int32, (tq, LANES), 1)
        c, s = cs_ref[...], sn_ref[...]

        @pl.when(pl.program_id(0) == 0)
        def _():
            dqg_ref[...] = jnp.zeros_like(dqg_ref)
            dkg_ref[...] = jnp.zeros_like(dkg_ref)

        def norm_rope_bwd(xg, g2, dout):
            r = lax.rsqrt(_head_sums(xg * xg, lane) * (1.0 / HEAD_DIM) + EPS)
            xh = xg * r
            dxn = dout * c + _swap16(dout * s, lane)
            gd = dxn * g2
            dx = r * (gd - xh * (_head_sums(gd * xh, lane) * (1.0 / HEAD_DIM)))
            return dx, jnp.sum(dxn * xh, axis=0, keepdims=True)

        for gi in range(4):
            sl = slice(gi * LANES, (gi + 1) * LANES)
            dx, dg = norm_rope_bwd(pa_ref[:, sl].astype(F32), qg_ref[...], dq_ref[:, sl] * scale)
            dpa_ref[:, sl] = dx.astype(BF16)
            dqg_ref[...] += dg
        dx, dg = norm_rope_bwd(pa_ref[:, 512:640].astype(F32), kg_ref[...], dkT_ref[...].T)
        dpa_ref[:, 512:640] = dx.astype(BF16)
        dkg_ref[...] += dg
        dpa_ref[:, 640:768] = dvT_ref[...].T.astype(BF16)
        z = pa_ref[:, 768:1280].astype(F32)
        _, dsilu = _silu_and_grad(z)
        dpa_ref[:, 768:1280] = (dy_ref[...] * o_ref[...] * dsilu).astype(BF16)

    row = lambda w: pl.BlockSpec((tq, w), lambda i: (i, 0))
    col = pl.BlockSpec((LANES, tq), lambda i: (0, i))
    vec = pl.BlockSpec((1, LANES), lambda i: (0, 0))
    return _host_call(
        body, (pa, cs, sn, qg2, kg2, dq, dkT, dvT, dy, o),
        [row(W_A), row(LANES), row(LANES), vec, vec, row(BW), col, col, row(BW), row(BW)],
        (row(W_A), vec, vec), (SDS((t, W_A), BF16), SDS((1, LANES), F32), SDS((1, LANES), F32)),
        name="attn_prep_bwd", grid=(t // tq,), semantics=("arbitrary",), exchange=exchange)


def _layer_norm(v, g, b):
    mu = jnp.mean(v, axis=-1, keepdims=True)
    xc = v - mu
    rs = lax.rsqrt(jnp.mean(xc * xc, axis=-1, keepdims=True) + EPS)
    xh = xc * rs
    return xh * g + b, xh, rs


def _gmlp_fwd(pb, lng, lnb, ws, bs):
    t = pb.shape[0]
    tb = _tile(t, 512, CHUNK)

    def body(pb_ref, g_ref, b_ref, ws_ref, bs_ref, y_ref):
        vln, _, _ = _layer_norm(pb_ref[:, BW:2 * BW].astype(F32), g_ref[...], b_ref[...])
        vb = vln.astype(BF16)
        for gi in range(B_GROUPS):
            w = ws_ref[gi].astype(BF16)
            cs_ = slice(gi * CHUNK, (gi + 1) * CHUNK)
            for n in range(tb // CHUNK):
                rs_ = slice(n * CHUNK, (n + 1) * CHUNK)
                mixed = _dot(w, vb[rs_, cs_], NN) + bs_ref[gi]
                z = pb_ref[rs_, 2 * BW + gi * CHUNK:2 * BW + (gi + 1) * CHUNK].astype(F32)
                y_ref[rs_, cs_] = (pb_ref[rs_, cs_].astype(F32) * mixed * (z * _sigmoid(z))).astype(BF16)

    return pl.pallas_call(
        body, name="gmlp_fwd", grid=(t // tb,),
        in_specs=[pl.BlockSpec((tb, W_B), lambda i: (i, 0)),
                  pl.BlockSpec((1, BW), lambda i: (0, 0)), pl.BlockSpec((1, BW), lambda i: (0, 0)),
                  pl.BlockSpec((B_GROUPS, CHUNK, CHUNK), lambda i: (0, 0, 0)),
                  pl.BlockSpec((B_GROUPS, CHUNK, 1), lambda i: (0, 0, 0))],
        out_specs=pl.BlockSpec((tb, BW), lambda i: (i, 0)),
        out_shape=SDS((t, BW), BF16), compiler_params=_params("parallel"),
    )(pb, lng, lnb, ws, bs)


def _gmlp_bwd(pb, lng, lnb, ws, bs, dy):
    t = pb.shape[0]
    tb = _tile(t, 256, CHUNK)

    def body(pb_ref, g_ref, b_ref, ws_ref, bs_ref, dy_ref, dpb_ref, dws_ref, dbs_ref, dg_ref, db_ref, dvln_ref):
        @pl.when(pl.program_id(0) == 0)
        def _():
            dws_ref[...] = jnp.zeros_like(dws_ref)
            dbs_ref[...] = jnp.zeros_like(dbs_ref)
            dg_ref[...] = jnp.zeros_like(dg_ref)
            db_ref[...] = jnp.zeros_like(db_ref)

        vln, xh, rs = _layer_norm(pb_ref[:, BW:2 * BW].astype(F32), g_ref[...], b_ref[...])
        vb = vln.astype(BF16)
        for gi in range(B_GROUPS):
            w = ws_ref[gi].astype(BF16)
            cs_ = slice(gi * CHUNK, (gi + 1) * CHUNK)
            for n in range(tb // CHUNK):
                rs_ = slice(n * CHUNK, (n + 1) * CHUNK)
                vbc = vb[rs_, cs_]
                mixed = _dot(w, vbc, NN) + bs_ref[gi]
                zs = slice(2 * BW + gi * CHUNK, 2 * BW + (gi + 1) * CHUNK)
                z = pb_ref[rs_, zs].astype(F32)
                u = pb_ref[rs_, cs_].astype(F32)
                sil, dsil = _silu_and_grad(z)
                dyc = dy_ref[rs_, cs_]
                dmixed = dyc * u * sil
                dpb_ref[rs_, cs_] = (dyc * mixed * sil).astype(BF16)
                dpb_ref[rs_, zs] = (dyc * u * mixed * dsil).astype(BF16)
                dmb = dmixed.astype(BF16)
                dws_ref[gi] += _dot(dmb, vbc, NT)
                dbs_ref[gi] += jnp.sum(dmixed, axis=-1, keepdims=True)
                dvln_ref[rs_, cs_] = _dot(w, dmb, TN)
        dvln = dvln_ref[...]
        dg_ref[...] += jnp.sum(dvln * xh, axis=0, keepdims=True)
        db_ref[...] += jnp.sum(dvln, axis=0, keepdims=True)
        gd = dvln * g_ref[...]
        dv = rs * (gd - jnp.mean(gd, axis=-1, keepdims=True) - xh * jnp.mean(gd * xh, axis=-1, keepdims=True))
        dpb_ref[:, BW:2 * BW] = dv.astype(BF16)

    vec = pl.BlockSpec((1, BW), lambda i: (0, 0))
    wsb = pl.BlockSpec((B_GROUPS, CHUNK, CHUNK), lambda i: (0, 0, 0))
    bsb = pl.BlockSpec((B_GROUPS, CHUNK, 1), lambda i: (0, 0, 0))
    return pl.pallas_call(
        body, name="gmlp_bwd", grid=(t // tb,),
        in_specs=[pl.BlockSpec((tb, W_B), lambda i: (i, 0)), vec, vec, wsb, bsb, pl.BlockSpec((tb, BW), lambda i: (i, 0))],
        out_specs=(pl.BlockSpec((tb, W_B), lambda i: (i, 0)), wsb, bsb, vec, vec),
        out_shape=(SDS((t, W_B), BF16), SDS((B_GROUPS, CHUNK, CHUNK), F32), SDS((B_GROUPS, CHUNK, 1), F32),
                   SDS((1, BW), F32), SDS((1, BW), F32)),
        scratch_shapes=[pltpu.VMEM((tb, BW), F32)],
        compiler_params=_params("arbitrary"),
    )(pb, lng, lnb, ws, bs, dy)


def _mem_softmax(qh, kh):
    s = _dot(qh, kh, NT) * (M_HEAD_DIM ** -0.5)
    e = jnp.exp(s - jnp.max(s, axis=-1, keepdims=True))
    return e / jnp.sum(e, axis=-1, keepdims=True)


def _mem_attn_fwd(pm, kv):
    t = pm.shape[0]
    tq = _tile(t, 512, 8)
    ml = kv.shape[0]

    def body(pm_ref, kv_ref, y_ref):
        for h in range(M_HEADS):
            hs = slice(h * M_HEAD_DIM, (h + 1) * M_HEAD_DIM)
            kh = kv_ref[:, hs].astype(BF16)
            vh = kv_ref[:, BW + h * M_HEAD_DIM:BW + (h + 1) * M_HEAD_DIM].astype(BF16)
            p = _mem_softmax(pm_ref[:, hs].astype(BF16), kh)
            o = _dot(p.astype(BF16), vh, NN)
            z = pm_ref[:, BW + h * M_HEAD_DIM:BW + (h + 1) * M_HEAD_DIM].astype(F32)
            y_ref[:, hs] = (o * (z * _sigmoid(z))).astype(BF16)

    return pl.pallas_call(
        body, name="mem_attn_fwd", grid=(t // tq,),
        in_specs=[pl.BlockSpec((tq, W_M), lambda i: (i, 0)), pl.BlockSpec((ml, 2 * BW), lambda i: (0, 0))],
        out_specs=pl.BlockSpec((tq, BW), lambda i: (i, 0)),
        out_shape=SDS((t, BW), BF16), compiler_params=_params("parallel"),
    )(pm, kv)


def _mem_attn_bwd(pm, kv, dy):
    t = pm.shape[0]
    tq = _tile(t, 512, 8)
    ml = kv.shape[0]
    scale = M_HEAD_DIM ** -0.5

    def body(pm_ref, kv_ref, dy_ref, dpm_ref, dkv_ref):
        @pl.when(pl.program_id(0) == 0)
        def _():
            dkv_ref[...] = jnp.zeros_like(dkv_ref)

        for h in range(M_HEADS):
            hs = slice(h * M_HEAD_DIM, (h + 1) * M_HEAD_DIM)
            zs = slice(BW + h * M_HEAD_DIM, BW + (h + 1) * M_HEAD_DIM)
            kh = kv_ref[:, hs].astype(BF16)
            vh = kv_ref[:, zs].astype(BF16)
            qh = pm_ref[:, hs].astype(BF16)
            p = _mem_softmax(qh, kh)
            pb = p.astype(BF16)
            o = _dot(pb, vh, NN)
            sil, dsil = _silu_and_grad(pm_ref[:, zs].astype(F32))
            dyh = dy_ref[:, hs]
            do = dyh * sil
            dpm_ref[:, zs] = (dyh * o * dsil).astype(BF16)
            delta = jnp.sum(do * o, axis=-1, keepdims=True)
            do_b = do.astype(BF16)
            dp = _dot(do_b, vh, NT)
            dr_b = (p * (dp - delta) * scale).astype(BF16)
            dpm_ref[:, hs] = _dot(dr_b, kh, NN).astype(BF16)
            dkv_ref[:, hs] += _dot(dr_b, qh, TN)
            dkv_ref[:, zs] += _dot(pb, do_b, TN)

    kvb = pl.BlockSpec((ml, 2 * BW), lambda i: (0, 0))
    return pl.pallas_call(
        body, name="mem_attn_bwd", grid=(t // tq,),
        in_specs=[pl.BlockSpec((tq, W_M), lambda i: (i, 0)), kvb, pl.BlockSpec((tq, BW), lambda i: (i, 0))],
        out_specs=(pl.BlockSpec((tq, W_M), lambda i: (i, 0)), kvb),
        out_shape=(SDS((t, W_M), BF16), SDS((ml, 2 * BW), F32)),
        compiler_params=_params("arbitrary"),
    )(pm, kv, dy)


def _merge_fwd(ya, yb, ym, pg, wbr, exchange=None):
    t = ya.shape[0]
    tm = _tile(t, 512, 8)

    def body(ya_ref, yb_ref, ym_ref, pg_ref, w_ref, m_ref):
        ups = [_dot(y_ref[...], w_ref[n], NN) for n, y_ref in enumerate((ya_ref, yb_ref, ym_ref))]
        acc = None
        for n, up in enumerate(ups):
            term = _sigmoid(pg_ref[:, n * D_MODEL:(n + 1) * D_MODEL].astype(F32)) * up
            acc = term if acc is None else acc + term
        m_ref[...] = acc.astype(BF16)

    yb_spec = pl.BlockSpec((tm, BW), lambda i: (i, 0))
    return _host_call(
        body, (ya, yb, ym, pg, wbr),
        [yb_spec, yb_spec, yb_spec, pl.BlockSpec((tm, W_G), lambda i: (i, 0)),
         pl.BlockSpec((3, BW, D_MODEL), lambda i: (0, 0, 0))],
        (pl.BlockSpec((tm, D_MODEL), lambda i: (i, 0)),), (SDS((t, D_MODEL), BF16),),
        name="merge_fwd", grid=(t // tm,), semantics=("parallel",), exchange=exchange)


def _merge_bwd(ya, yb, ym, pg, wbr, dm, exchange=None):
    t = ya.shape[0]
    tm = _tile(t, 512, 16)

    def body(ya_ref, yb_ref, ym_ref, pg_ref, w_ref, dm_ref, dya_ref, dyb_ref, dym_ref, dpg_ref, dw_ref):
        @pl.when(pl.program_id(0) == 0)
        def _():
            dw_ref[...] = jnp.zeros_like(dw_ref)

        dmf = dm_ref[...]
        branches = ((ya_ref, dya_ref), (yb_ref, dyb_ref), (ym_ref, dym_ref))

        def gate(n):
            gt = _sigmoid(pg_ref[:, n * D_MODEL:(n + 1) * D_MODEL].astype(F32))
            return gt, (dmf * gt).astype(BF16)

        gt, dup = gate(0)
        for n, (y_ref, dy_ref) in enumerate(branches):
            nxt = gate(n + 1) if n + 1 < len(branches) else None
            y, w = y_ref[...], w_ref[n]
            dy_ref[...] = _dot(dup, w, NT)
            dw_ref[n] += _dot(y, dup, TN)
            up = _dot(y, w, NN)
            dpg_ref[:, n * D_MODEL:(n + 1) * D_MODEL] = (dmf * up * gt * (1.0 - gt)).astype(BF16)
            if nxt is not None:
                gt, dup = nxt

    y_spec = pl.BlockSpec((tm, BW), lambda i: (i, 0))
    w_spec = pl.BlockSpec((3, BW, D_MODEL), lambda i: (0, 0, 0))
    return _host_call(
        body, (ya, yb, ym, pg, wbr, dm),
        [y_spec, y_spec, y_spec, pl.BlockSpec((tm, W_G), lambda i: (i, 0)), w_spec,
         pl.BlockSpec((tm, D_MODEL), lambda i: (i, 0))],
        (y_spec, y_spec, y_spec, pl.BlockSpec((tm, W_G), lambda i: (i, 0)), w_spec),
        (SDS((t, BW), F32), SDS((t, BW), F32), SDS((t, BW), F32), SDS((t, W_G), BF16), SDS((3, BW, D_MODEL), F32)),
        name="merge_bwd", grid=(t // tm,), semantics=("arbitrary",), exchange=exchange)


def _loss_head(x, g, target):
    t, d = x.shape
    tm = _tile(t, 512, 8)

    def body(x_ref, g_ref, t_ref, loss_ref, dx_ref, dg_ref):
        @pl.when(pl.program_id(0) == 0)
        def _():
            loss_ref[...] = jnp.zeros_like(loss_ref)
            dg_ref[...] = jnp.zeros_like(dg_ref)

        xf = x_ref[...]
        r = lax.rsqrt(jnp.mean(xf * xf, axis=-1, keepdims=True) + EPS)
        xh = xf * r
        err = xh * g_ref[...] - t_ref[...]
        per_tok = jnp.mean(err * err, axis=-1, keepdims=True)
        loss_ref[...] += 0.5 * jnp.sum(per_tok, axis=0, keepdims=True)
        dy = err * (1.0 / d)
        gd = dy * g_ref[...]
        dx_ref[...] = r * (gd - xh * jnp.mean(gd * xh, axis=-1, keepdims=True))
        dg_ref[...] += jnp.sum(dy * xh, axis=0, keepdims=True)

    row = pl.BlockSpec((tm, d), lambda i: (i, 0))
    vec = pl.BlockSpec((1, d), lambda i: (0, 0))
    return pl.pallas_call(
        body, name="loss_head", grid=(t // tm,),
        in_specs=[row, vec, row],
        out_specs=(pl.BlockSpec((1, 1), lambda i: (0, 0)), row, vec),
        out_shape=(SDS((1, 1), F32), SDS((t, d), F32), SDS((1, d), F32)),
        compiler_params=_params("arbitrary"),
    )(x, g, target)


def _layer_fwd(x, mem, w, tabs, next_shards=None):
    cs, sn = tabs
    riding = next_shards is not None
    h, pa, pb, pm, pg = _norm_in_proj(x, w["norm_g"], w["w_in_t"])
    q, qT, k, kT, vT = _attn_prep_fwd(pa, cs, sn, w["qg2"], w["kg2"])
    (o, lse, ya), gathered = _attn_fwd(qT, k, vT, pa, exchange=_gather_first_hop(next_shards) if riding else None)
    yb = _gmlp_fwd(pb, w["ln_g"], w["ln_b"], w["w_s"], w["b_s"])
    memn = _rmsnorm_fwd(mem, w["mem_g"], "mem_rmsnorm_fwd")
    kv = _mm(memn, w["w_kv"], "nn", "mem_kv")
    ym = _mem_attn_fwd(pm, kv)
    (merged,), gathered = _merge_fwd(ya, yb, ym, pg, w["w_br"], exchange=_gather_forward(gathered) if riding else None)
    x_next = _mm(merged, w["w_out"], "nn", "out_proj", add=x)
    saved = dict(x=x, h=h, pa=pa, pb=pb, pm=pm, pg=pg, q=q, qT=qT, k=k, kT=kT, vT=vT, o=o, lse=lse, ya=ya, yb=yb, ym=ym,
                 memn=memn, kv=kv, merged=merged)
    return x_next, saved, gathered


def _layer_bwd(dx_out, mem, w, s, tabs, pending=None, core=None, scatter_own=False):
    cs, sn = tabs
    t = dx_out.shape[0]
    riding = pending is not None
    dmerged = _mm(dx_out, w["w_out"], "nt", "d_merged")
    by_owner = lambda a: a.reshape((N_DEV // 2, 2, -1) + a.shape[-2:])
    row_shards = lambda a: by_owner(a.reshape(N_DEV, a.shape[0] // N_DEV, a.shape[1]))
    d_w_out = _mm_tn(s["merged"], dx_out, "d_w_out")
    (dya, dyb, dym, dpg, d_w_br), recv = _merge_bwd(s["ya"], s["yb"], s["ym"], s["pg"], w["w_br"], dmerged,
                                                    exchange=_scatter_to_sibling(pending) if riding else None)
    parts = _pair_sums(BIG, pending, recv, core) if riding else None
    grads = dict(w_br=by_owner(_split_w_br(d_w_br)), w_out=row_shards(d_w_out))
    early = [grads[n] for n in BIG[2:]]
    grp = A_HEADS // A_KV_HEADS
    lse_cols = s["lse"].reshape(A_KV_HEADS, grp, t).transpose(0, 2, 1)
    (dq, dkT, dvT), landed = _attn_bwd(
        s["q"], s["qT"], s["k"], s["kT"], s["vT"], s["pa"], s["o"], lse_cols, dya,
        exchange=_both(_scatter_to_chips(parts) if riding else None, _scatter_to_sibling(early) if scatter_own else None))
    from_chips, early_recv = (landed[:len(BIG)], landed[len(BIG):]) if riding else (None, landed)
    early_parts = _pair_sums(BIG[2:], early, early_recv, core) if scatter_own else None
    (dpa, d_qg2, d_kg2), early_from_chips = _attn_prep_bwd(
        s["pa"], cs, sn, w["qg2"], w["kg2"], dq, dkT.reshape(-1, t), dvT.reshape(-1, t), dya, s["o"],
        exchange=_scatter_to_chips(early_parts) if scatter_own else None)
    dpb, d_w_s, d_b_s, d_ln_g, d_ln_b = _gmlp_bwd(s["pb"], w["ln_g"], w["ln_b"], w["w_s"], w["b_s"], dyb)
    dpm, dkv = _mem_attn_bwd(s["pm"], s["kv"], dym)
    d_w_kv = _mm_tn(s["memn"], dkv, "d_w_kv")
    dmemn = _mm(dkv, w["w_kv"], "nt", "d_memn")
    d_mem_g = _rmsnorm_dg(mem, w["mem_g"], dmemn, "mem_rmsnorm_bwd")
    dps = (dpa, dpb, dpm, dpg)
    grads.update(w_in=row_shards(_d_w_in_t(dps, s["h"])), w_mem_kv=row_shards(d_w_kv))
    own = None
    if scatter_own:
        late = [grads[n] for n in BIG[:2]]
        late_parts = _pair_sums(BIG[:2], late, _exchange_call(_scatter_to_sibling(late), "rs_sibling_swap"), core)
        own = _scatter_to_chips(late_parts)
    (dx_in, d_norm_g), late_from_chips = _dh_rmsnorm_bwd(dps, w["w_in_t"], s["x"], w["norm_g"], dx_out, exchange=own)
    if scatter_own:
        own_parts, own_from_chips = late_parts + early_parts, tuple(late_from_chips) + tuple(early_from_chips)
    grads.update(norm_g=d_norm_g[0], q_norm_g=d_qg2[0, :HEAD_DIM] + d_qg2[0, HEAD_DIM:],
                 k_norm_g=d_kg2[0, :HEAD_DIM] + d_kg2[0, HEAD_DIM:], sg_ln_g=d_ln_g[0], sg_ln_b=d_ln_b[0],
                 w_s=d_w_s, b_s=d_b_s[:, :, 0], mem_norm_g=d_mem_g[0])
    return (dx_in, grads, ((parts, from_chips) if riding else None),
            ((own_parts, own_from_chips) if scatter_own else None))


def _layer_weights(l, w_in_t, w_kv, w_br, w_out, small):
    tile2 = lambda g: jnp.tile(g.reshape(1, -1), (1, 2))
    return dict(
        w_in_t=w_in_t, w_kv=w_kv, w_br=w_br, w_out=w_out,
        norm_g=small["norm_g"][l].reshape(1, -1), qg2=tile2(small["q_norm_g"][l]), kg2=tile2(small["k_norm_g"][l]),
        ln_g=small["sg_ln_g"][l].reshape(1, -1), ln_b=small["sg_ln_b"][l].reshape(1, -1),
        w_s=small["w_s"][l], b_s=small["b_s"][l][:, :, None], mem_g=small["mem_norm_g"][l].reshape(1, -1))


def _position():
    x, y, c = lax.axis_index("x"), lax.axis_index("y"), lax.axis_index("c")
    return x, y, c, [(1 - x, y), (x, 1 - y), (1 - x, 1 - y)]


def _gather_first_hop(shards):
    n = len(shards)

    def make(x_refs, out_refs, sems):
        send_sems, recv_sems, local_sems = sems
        x, y, c, chips = _position()
        me = 4 * x + 2 * y + c
        peers = [(x, y, 1 - c)] + [(cx, cy, c) for cx, cy in chips]
        copies = [pltpu.make_async_copy(x_refs[t], out_refs[t].at[me], local_sems.at[t]) for t in range(n)]
        copies += [pltpu.make_async_remote_copy(
            src_ref=x_refs[t], dst_ref=out_refs[t].at[me], send_sem=send_sems.at[t, k], recv_sem=recv_sems.at[t, k],
            device_id=peer, device_id_type=MESH_ID) for t in range(n) for k, peer in enumerate(peers)]
        return copies

    return _Exchange(shards, [SDS((N_DEV,) + a.shape, a.dtype) for a in shards],
                     [pltpu.SemaphoreType.DMA((n, 4)), pltpu.SemaphoreType.DMA((n, 4)), pltpu.SemaphoreType.DMA((n,))], make)


def _gather_forward(gathered):
    n = len(gathered)

    def make(in_refs, out_refs, sems):
        send_sems, recv_sems = sems
        x, y, c, chips = _position()
        return [pltpu.make_async_remote_copy(
            src_ref=in_refs[t].at[4 * cx + 2 * cy + c], dst_ref=out_refs[t].at[4 * cx + 2 * cy + c],
            send_sem=send_sems.at[t, j], recv_sem=recv_sems.at[t, j], device_id=(x, y, 1 - c), device_id_type=MESH_ID)
            for t in range(n) for j, (cx, cy) in enumerate(chips)]

    return _Exchange(gathered, [SDS(a.shape, a.dtype) for a in gathered],
                     [pltpu.SemaphoreType.DMA((n, 3)), pltpu.SemaphoreType.DMA((n, 3))], make,
                     aliases={t: t for t in range(n)})


def _all_gather(shards, name):
    return _exchange_call(_gather_forward(_exchange_call(_gather_first_hop(shards), name + "_hop1")), name + "_hop2")


def _assemble_w_br(gathered):
    _, nb, rows, shard = gathered.shape

    def body(g_ref, o_ref):
        for d in range(N_DEV):
            o_ref[:, :, d * shard:(d + 1) * shard] = g_ref[d]

    return pl.pallas_call(
        body, name="assemble_w_br", out_shape=SDS((nb, rows, N_DEV * shard), gathered.dtype),
        compiler_params=pltpu.CompilerParams(vmem_limit_bytes=VMEM_LIMIT),
    )(gathered)


def _split_w_br(dw):
    nb, rows, cols = dw.shape
    shard = cols // N_DEV

    def body(d_ref, o_ref):
        for d in range(N_DEV):
            o_ref[d] = d_ref[:, :, d * shard:(d + 1) * shard].astype(BF16)

    return pl.pallas_call(
        body, name="split_w_br", out_shape=SDS((N_DEV, nb, rows, shard), BF16),
        compiler_params=pltpu.CompilerParams(vmem_limit_bytes=VMEM_LIMIT),
    )(dw)


def _scatter_to_sibling(dests):
    n = len(dests)

    def make(d_refs, recv_refs, sems):
        send_sems, recv_sems = sems
        x, y, c, _ = _position()
        return [pltpu.make_async_remote_copy(
            src_ref=d_refs[t].at[:, 1 - c], dst_ref=recv_refs[t], send_sem=send_sems.at[t],
            recv_sem=recv_sems.at[t], device_id=(x, y, 1 - c), device_id_type=MESH_ID) for t in range(n)]

    return _Exchange(dests, [SDS(a.shape[:1] + a.shape[2:], a.dtype) for a in dests],
                     [pltpu.SemaphoreType.DMA((n,)), pltpu.SemaphoreType.DMA((n,))], make)


def _scatter_to_chips(parts):
    n = len(parts)

    def make(p_refs, recv_refs, sems):
        send_sems, recv_sems = sems
        _, _, c, chips = _position()
        return [pltpu.make_async_remote_copy(
            src_ref=p_refs[t].at[2 * cx + cy], dst_ref=recv_refs[t].at[k], send_sem=send_sems.at[t, k],
            recv_sem=recv_sems.at[t, k], device_id=(cx, cy, c), device_id_type=MESH_ID)
            for t in range(n) for k, (cx, cy) in enumerate(chips)]

    return _Exchange(parts, [SDS((3,) + a.shape[1:], a.dtype) for a in parts],
                     [pltpu.SemaphoreType.DMA((n, 3)), pltpu.SemaphoreType.DMA((n, 3))], make)


def _pair_sums(names, dests, recv, core):
    return [_pair_sum(d, r, core, "rs_pair_sum_" + n) for n, d, r in zip(names, dests, recv)]


def _pair_sum(dest, recv, core, name):
    _, _, na, r, cdim = dest.shape

    def body(core_ref, a_ref, b_ref, o_ref):
        o_ref[...] = (a_ref[...].astype(F32) + b_ref[...].astype(F32)).astype(o_ref.dtype)

    blk = pl.BlockSpec((None, na, r, cdim), lambda j, core_ref: (j, 0, 0, 0))
    return pl.pallas_call(
        body, name=name, out_shape=SDS(recv.shape, BF16),
        grid_spec=pltpu.PrefetchScalarGridSpec(
            num_scalar_prefetch=1, grid=(4,),
            in_specs=[pl.BlockSpec((None, None, na, r, cdim), lambda j, core_ref: (j, core_ref[0], 0, 0, 0)), blk],
            out_specs=blk),
        compiler_params=_params("parallel"),
    )(core, dest, recv)


def _adamw_math(w, g, m, v):
    m = ADAM_B1 * m + (1.0 - ADAM_B1) * g
    v = ADAM_B2 * v + (1.0 - ADAM_B2) * (g * g)
    m_hat = m / (1.0 - ADAM_B1 ** ADAM_STEP)
    v_hat = v / (1.0 - ADAM_B2 ** ADAM_STEP)
    delta = -ADAM_LR * (m_hat / (jnp.sqrt(v_hat) + ADAM_EPS) + ADAM_WD * w)
    return delta, m, v


def _sum_and_adamw(parts, w, m, v, name):
    n, r, ln = parts.shape
    tr = _tile(r, 512, 16)

    def body(p_ref, w_ref, m_ref, v_ref, g_out, d_out, m_out, v_out):
        g = p_ref[0].astype(F32)
        for j in range(1, n):
            g = g + p_ref[j].astype(F32)
        delta, nm, nv = _adamw_math(w_ref[...], g, m_ref[...], v_ref[...])
        g_out[...] = g
        d_out[...] = delta
        m_out[...] = nm
        v_out[...] = nv

    blk = pl.BlockSpec((tr, ln), lambda i: (i, 0))
    return pl.pallas_call(
        body, name=name, grid=(r // tr,),
        in_specs=[pl.BlockSpec((n, tr, ln), lambda i: (0, i, 0)), blk, blk, blk],
        out_specs=(blk, blk, blk, blk), out_shape=tuple(SDS((r, ln), F32) for _ in range(4)),
        compiler_params=_params("parallel"),
    )(parts, w, m, v)


BIG = ("w_in", "w_mem_kv", "w_br", "w_out")
SMALL = ("norm_g", "q_norm_g", "k_norm_g", "sg_ln_g", "sg_ln_b", "w_s", "b_s", "mem_norm_g", "final_g")


def _pack(arrs, row_unit=16):
    flat = jnp.concatenate([a.reshape(-1) for a in arrs])
    pad = (-flat.shape[0]) % (row_unit * LANES)
    if pad:
        flat = jnp.concatenate([flat, jnp.zeros((pad,), flat.dtype)])
    return flat.reshape(-1, LANES)


def _unpack(buf, shapes):
    flat = buf.reshape(-1)
    out, off = [], 0
    for shp in shapes:
        n = 1
        for s_ in shp:
            n *= s_
        out.append(flat[off:off + n].reshape(shp))
        off += n
    return out


def _shard_sum_adamw(part, from_chips, chip, w, m, v, layer, prev, name):
    _, na, r, cdim = part.shape
    flat = w.ndim == 3
    assert not flat or na == 1
    tr = _tile(r, max(8, (256 * 1024) // (na * cdim)), 8)
    n_prev = 0 if prev is None else len(prev)

    def body(chip_ref, p_ref, f_ref, w_ref, m_ref, v_ref, *rest):
        g_out, d_out, m_out, v_out = rest[n_prev:]
        g = p_ref[...].astype(F32)
        for j in range(3):
            g = g + f_ref[j].astype(F32)
        delta, nm, nv = _adamw_math(w_ref[...], g, m_ref[...], v_ref[...])
        g_out[...] = g
        d_out[...] = delta
        m_out[...] = nm
        v_out[...] = nv

    a_blk = None if flat else na
    if flat:
        lay = pl.BlockSpec((None, tr, cdim), lambda i, chip_ref: (layer, i, 0))
    else:
        lay = pl.BlockSpec((None, na, tr, cdim), lambda i, chip_ref: (layer, 0, i, 0))
    return pl.pallas_call(
        body, name=name, out_shape=tuple(SDS(w.shape, F32) for _ in range(4)),
        grid_spec=pltpu.PrefetchScalarGridSpec(
            num_scalar_prefetch=1, grid=(r // tr,),
            in_specs=[pl.BlockSpec((None, a_blk, tr, cdim), lambda i, chip_ref: (chip_ref[0], 0, i, 0)),
                      pl.BlockSpec((3, a_blk, tr, cdim), lambda i, chip_ref: (0, 0, i, 0)), lay, lay, lay]
            + [HBM] * n_prev,
            out_specs=(lay, lay, lay, lay)),
        input_output_aliases={6 + j: j for j in range(n_prev)},
        compiler_params=_params("parallel"),
    )(chip, part, from_chips, w, m, v, *(prev or ()))


def kernel(x, mem, norm_g, w_in, q_norm_g, k_norm_g, sg_ln_g, sg_ln_b, w_s, b_s, mem_norm_g, w_mem_kv, w_br, w_out, final_g, loss_target, m_norm_g, m_w_in, m_q_norm_g, m_k_norm_g, m_sg_ln_g, m_sg_ln_b, m_w_s, m_b_s, m_mem_norm_g, m_w_mem_kv, m_w_br, m_w_out, m_final_g, v_norm_g, v_w_in, v_q_norm_g, v_k_norm_g, v_sg_ln_g, v_sg_ln_b, v_w_s, v_b_s, v_mem_norm_g, v_w_mem_kv, v_w_br, v_w_out, v_final_g):
    wts = dict(norm_g=norm_g, w_in=w_in, q_norm_g=q_norm_g, k_norm_g=k_norm_g, sg_ln_g=sg_ln_g, sg_ln_b=sg_ln_b,
               w_s=w_s, b_s=b_s, mem_norm_g=mem_norm_g, w_mem_kv=w_mem_kv, w_br=w_br, w_out=w_out, final_g=final_g)
    mom1 = dict(norm_g=m_norm_g, w_in=m_w_in, q_norm_g=m_q_norm_g, k_norm_g=m_k_norm_g, sg_ln_g=m_sg_ln_g,
                sg_ln_b=m_sg_ln_b, w_s=m_w_s, b_s=m_b_s, mem_norm_g=m_mem_norm_g, w_mem_kv=m_w_mem_kv, w_br=m_w_br,
                w_out=m_w_out, final_g=m_final_g)
    mom2 = dict(norm_g=v_norm_g, w_in=v_w_in, q_norm_g=v_q_norm_g, k_norm_g=v_k_norm_g, sg_ln_g=v_sg_ln_g,
                sg_ln_b=v_sg_ln_b, w_s=v_w_s, b_s=v_b_s, mem_norm_g=v_mem_norm_g, w_mem_kv=v_w_mem_kv, w_br=v_w_br,
                w_out=v_w_out, final_g=v_final_g)
    dp = w_in.shape[0]
    core = lax.axis_index("c").astype(jnp.int32).reshape(1)
    chip = (2 * lax.axis_index("x") + lax.axis_index("y")).astype(jnp.int32).reshape(1)

    t_in = lambda a: jnp.swapaxes(a, 1, 2)
    wts, mom1, mom2 = [dict(d_, w_in=t_in(d_["w_in"])) for d_ in (wts, mom1, mom2)]

    shard_bf = {n: wts[n].astype(BF16) for n in BIG}
    shards = lambda l: [shard_bf[n][l] for n in BIG]
    x_l, mem_l = x[0], mem[0]
    tabs = _rope_tables(x_l.shape[0])

    gathered = _all_gather(shards(0), "weights_all_gather")
    layers, saved = [], []
    for l in range(dp):
        g_in, g_kv, g_br, g_out = gathered
        layers.append(_layer_weights(l, g_in.reshape(IN_WIDTH, -1), g_kv.reshape(D_MODEL, -1), _assemble_w_br(g_br),
                                     g_out.reshape(D_MODEL, -1), wts))
        x_l, s, gathered = _layer_fwd(x_l, mem_l, layers[l], tabs, next_shards=shards(l + 1) if l + 1 < dp else None)
        saved.append(s)
    loss_local, dx, d_final_g = _loss_head(x_l, final_g.reshape(1, -1), loss_target[0])
    loss = lax.psum(loss_local[0, 0], AXES)

    def finish(l, parts, from_chips, prev):
        return {n: _shard_sum_adamw(p, f, chip, wts[n], mom1[n], mom2[n], l, None if prev is None else prev[n],
                                    "sum_adamw_" + n)
                for n, p, f in zip(BIG, parts, from_chips)}

    grads, updated, pending = [None] * dp, None, None
    for l in reversed(range(dp)):
        dx, grads[l], scattered, own = _layer_bwd(dx, mem_l, layers[l], saved[l], tabs, pending=pending, core=core,
                                                  scatter_own=(l == 0))
        if scattered is not None:
            updated = finish(l + 1, *scattered, updated)
        pending = [grads[l][n] for n in BIG]
    updated = finish(0, *own, updated)
    grad_x = dx
    big_out = [{n: (t_in(updated[n][k]) if n == "w_in" else updated[n][k]) for n in BIG} for k in range(4)]

    small_g = {n: jnp.stack([g[n] for g in grads]) for n in SMALL if n != "final_g"}
    small_g["final_g"] = d_final_g
    (all_small,) = _all_gather([_pack([small_g[n] for n in SMALL])], "small_all_gather")
    small_bufs = _sum_and_adamw(
        all_small, _pack([wts[n] for n in SMALL]), _pack([mom1[n] for n in SMALL]), _pack([mom2[n] for n in SMALL]),
        "small_sum_adamw")

    outs = []
    for big_vals, small_buf in zip(big_out, small_bufs):
        vals = dict(big_vals)
        vals.update(zip(SMALL, _unpack(small_buf, [wts[n].shape for n in SMALL])))
        outs.append(vals)
    order = ("norm_g", "w_in", "q_norm_g", "k_norm_g", "sg_ln_g", "sg_ln_b", "w_s", "b_s", "mem_norm_g", "w_mem_kv",
             "w_br", "w_out", "final_g")
    result = [loss, grad_x[None]]
    for vals in outs:
        result += [vals[n] for n in order]
    return tuple(result)
```

```python
import functools

import jax
import jax.numpy as jnp
from jax import lax
from jax.experimental import pallas as pl
from jax.experimental.pallas import tpu as pltpu

F32 = jnp.float32
BF16 = jnp.bfloat16
SDS = jax.ShapeDtypeStruct
MESH_ID = pl.DeviceIdType.MESH
AXES = ("x", "y", "c")
N_DEV = 8

D_MODEL = 1024
DEPTH = 4
GRID_W = 64
CHUNK = 128
ROPE_THETA = 10000.0
EPS = 1e-6
HEAD_DIM = 64
A_HEADS = 8
A_KV_HEADS = 2
B_GROUPS = 4
M_HEADS = 4
M_HEAD_DIM = 128
BW = 512
W_A = 1280
W_B = 1536
W_M = 1024
W_G = 3072
IN_WIDTH = W_A + W_B + W_M + W_G
GROUP_OFFS = (0, W_A, W_A + W_B, W_A + W_B + W_M, IN_WIDTH)

ADAM_LR = 0.001
ADAM_B1 = 0.9
ADAM_B2 = 0.999
ADAM_EPS = 1e-08
ADAM_WD = 0.01
ADAM_STEP = 10

LANES = 128
KEY_CHUNK = 2048
VMEM_LIMIT = 52 * 1024 * 1024


def _tile(n, cap, unit=LANES):
    if n <= cap:
        return n
    t = (cap // unit) * unit
    while n % t:
        t -= unit
    return t


def _params(*sem):
    return pltpu.CompilerParams(dimension_semantics=sem, vmem_limit_bytes=VMEM_LIMIT)


def _sigmoid(z):
    return 0.5 * jnp.tanh(0.5 * z) + 0.5


def _silu_and_grad(z):
    s = _sigmoid(z)
    return z * s, s * (1.0 + z * (1.0 - s))


def _dot(a, b, dims):
    return lax.dot_general(a, b, (dims, ((), ())), preferred_element_type=F32)


NN = ((1,), (0,))
NT = ((1,), (1,))
TN = ((0,), (0,))
HBM = pl.BlockSpec(memory_space=pl.ANY)


class _Exchange:
    def __init__(self, ins, out_shapes, sems, make, aliases=None):
        self.ins, self.out_shapes, self.sems, self.make = list(ins), list(out_shapes), list(sems), make
        self.aliases = dict(aliases or {})

    def start(self, in_refs, out_refs, sems):
        for cp in self.make(in_refs, out_refs, sems):
            cp.start()

    def finish(self, in_refs, out_refs, sems):
        for cp in self.make(in_refs, out_refs, sems):
            cp.wait()


def _both(a, b):
    if a is None or b is None:
        return a if b is None else b
    n_in, n_out, n_sem = len(a.ins), len(a.out_shapes), len(a.sems)

    def make(in_refs, out_refs, sems):
        return (a.make(in_refs[:n_in], out_refs[:n_out], sems[:n_sem])
                + b.make(in_refs[n_in:], out_refs[n_out:], sems[n_sem:]))

    aliases = dict(a.aliases)
    aliases.update({n_in + i: n_out + o for i, o in b.aliases.items()})
    return _Exchange(a.ins + b.ins, a.out_shapes + b.out_shapes, a.sems + b.sems, make, aliases)


def _host_call(body, ins, in_specs, out_specs, out_shape, *, name, grid, semantics, scratch=(), exchange=None):
    ins, out_shape, scratch = list(ins), list(out_shape), list(scratch)
    if exchange is None:
        res = pl.pallas_call(
            body, name=name, grid=grid, in_specs=list(in_specs), out_specs=tuple(out_specs), out_shape=tuple(out_shape),
            scratch_shapes=scratch, compiler_params=_params(*semantics))(*ins)
        return tuple(res), ()
    n_in, n_out, n_scr = len(ins), len(out_shape), len(scratch)
    x_in, x_out = len(exchange.ins), len(exchange.out_shapes)

    def carrying(*refs):
        o0 = n_in + x_in
        s0 = o0 + n_out + x_out
        c_in, c_out, c_sems = refs[n_in:o0], refs[o0 + n_out:s0], refs[s0 + n_scr:]
        ids = [pl.program_id(a) for a in range(len(grid))]
        first = functools.reduce(jnp.logical_and, [i == 0 for i in ids])
        last = functools.reduce(jnp.logical_and, [i == g - 1 for i, g in zip(ids, grid)])

        @pl.when(first)
        def _():
            exchange.start(c_in, c_out, c_sems)

        body(*refs[:n_in], *refs[o0:o0 + n_out], *refs[s0:s0 + n_scr])

        @pl.when(last)
        def _():
            exchange.finish(c_in, c_out, c_sems)

    res = pl.pallas_call(
        carrying, name=name, grid=grid, in_specs=list(in_specs) + [HBM] * x_in,
        out_specs=tuple(out_specs) + (HBM,) * x_out, out_shape=tuple(out_shape) + tuple(exchange.out_shapes),
        scratch_shapes=scratch + exchange.sems,
        input_output_aliases={n_in + i: n_out + o for i, o in exchange.aliases.items()},
        compiler_params=_params(*(["arbitrary"] * len(grid))))(*ins, *exchange.ins)
    return tuple(res[:n_out]), tuple(res[n_out:])


def _exchange_call(exchange, name):
    x_in = len(exchange.ins)

    def body(*refs):
        x_out = len(exchange.out_shapes)
        c_in, c_out, c_sems = refs[:x_in], refs[x_in:x_in + x_out], refs[x_in + x_out:]
        exchange.start(c_in, c_out, c_sems)
        exchange.finish(c_in, c_out, c_sems)

    return pl.pallas_call(
        body, name=name, in_specs=[HBM] * x_in, out_specs=tuple([HBM] * len(exchange.out_shapes)),
        out_shape=tuple(exchange.out_shapes), scratch_shapes=exchange.sems, input_output_aliases=exchange.aliases,
    )(*exchange.ins)


def _mm(a, b, mode, name):
    (m, k), (n, k2) = a.shape, (b.shape[::-1] if mode == "nn" else b.shape)
    assert k == k2 and mode in ("nn", "nt"), (a.shape, b.shape, mode)
    dims = {"nn": NN, "nt": NT}[mode]

    def body(a_ref, b_ref, o_ref):
        o_ref[...] = _dot(a_ref[...].astype(BF16), b_ref[...].astype(BF16), dims)

    return pl.pallas_call(body, name=name, out_shape=SDS((m, n), F32),
                          compiler_params=pltpu.CompilerParams(vmem_limit_bytes=VMEM_LIMIT))(a, b)


def _mm_tn(a, b, name):
    (k, m), (k2, n) = a.shape, b.shape
    assert k == k2, (a.shape, b.shape)
    tk = _tile(k, 2048, 16)
    nk = k // tk

    def body(a_ref, b_ref, o_ref, acc):
        kk = pl.program_id(0)
        part = _dot(a_ref[...].astype(BF16), b_ref[...].astype(BF16), TN)

        @pl.when(kk == 0)
        def _():
            acc[...] = part

        @pl.when(kk > 0)
        def _():
            acc[...] += part

        @pl.when(kk == nk - 1)
        def _():
            o_ref[...] = acc[...].astype(BF16)

    return pl.pallas_call(
        body, name=name, grid=(nk,),
        in_specs=[pl.BlockSpec((tk, m), lambda kk: (kk, 0)), pl.BlockSpec((tk, n), lambda kk: (kk, 0))],
        out_specs=pl.BlockSpec((m, n), lambda kk: (0, 0)), out_shape=SDS((m, n), BF16),
        scratch_shapes=[pltpu.VMEM((m, n), F32)], compiler_params=_params("arbitrary"),
    )(a, b)


def _rmsnorm_fwd(x, g, name):
    t, d = x.shape
    tm = _tile(t, 512, 8)

    def body(x_ref, g_ref, h_ref):
        xf = x_ref[...]
        r = lax.rsqrt(jnp.mean(xf * xf, axis=-1, keepdims=True) + EPS)
        h_ref[...] = (xf * r * g_ref[...]).astype(BF16)

    return pl.pallas_call(
        body, name=name, grid=(t // tm,),
        in_specs=[pl.BlockSpec((tm, d), lambda i: (i, 0)), pl.BlockSpec((1, d), lambda i: (0, 0))],
        out_specs=pl.BlockSpec((tm, d), lambda i: (i, 0)),
        out_shape=SDS((t, d), BF16), compiler_params=_params("parallel"),
    )(x, g)


def _rmsnorm_bwd_math(xf, g, dh):
    r = lax.rsqrt(jnp.mean(xf * xf, axis=-1, keepdims=True) + EPS)
    xh = xf * r
    gd = dh * g
    dx = r * (gd - xh * jnp.mean(gd * xh, axis=-1, keepdims=True))
    return dx, jnp.sum(dh * xh, axis=0, keepdims=True)


def _rmsnorm_dg(x, g, dh, name):
    t, d = x.shape
    tm = _tile(t, 512, 8)

    def body(x_ref, g_ref, dh_ref, dg_ref):
        @pl.when(pl.program_id(0) == 0)
        def _():
            dg_ref[...] = jnp.zeros_like(dg_ref)

        dg_ref[...] += _rmsnorm_bwd_math(x_ref[...], g_ref[...], dh_ref[...])[1]

    row = pl.BlockSpec((tm, d), lambda i: (i, 0))
    vec = pl.BlockSpec((1, d), lambda i: (0, 0))
    return pl.pallas_call(
        body, name=name, grid=(t // tm,), in_specs=[row, vec, row], out_specs=vec, out_shape=SDS((1, d), F32),
        compiler_params=_params("arbitrary"),
    )(x, g, dh)


def _group_rows(p):
    return slice(GROUP_OFFS[p], GROUP_OFFS[p + 1])


def _d_w_in_t(dps, h):
    t, d = h.shape
    tm = 256
    assert all(off % tm == 0 for off in GROUP_OFFS)
    tiles = [a.shape[1] // tm for a in dps]
    starts = [sum(tiles[:p]) for p in range(len(dps))]

    def body(*refs):
        a_refs, h_ref, o_ref = refs[:len(dps)], refs[len(dps)], refs[len(dps) + 1]
        i = pl.program_id(0)
        for p, a_ref in enumerate(a_refs):
            @pl.when((i >= starts[p]) & (i < starts[p] + tiles[p]))
            def _():
                o_ref[...] = _dot(a_ref[...], h_ref[...], TN).astype(BF16)

    def tile_of(p):
        return lambda i: jnp.clip(i - starts[p], 0, tiles[p] - 1)

    return pl.pallas_call(
        body, name="d_w_in", grid=(sum(tiles),),
        in_specs=[pl.BlockSpec((t, tm), lambda i, p=p: (0, tile_of(p)(i))) for p in range(len(dps))]
        + [pl.BlockSpec((t, d), lambda i: (0, 0), pipeline_mode=pl.Buffered(1))],
        out_specs=pl.BlockSpec((tm, d), lambda i: (i, 0)),
        out_shape=SDS((sum(tiles) * tm, d), BF16), compiler_params=_params("arbitrary"),
    )(*dps, h)


def _norm_in_proj(x, g, w_t):
    t, d = x.shape
    tm = _tile(t, 512, 16)
    widths = [GROUP_OFFS[p + 1] - GROUP_OFFS[p] for p in range(4)]

    def body(x_ref, g_ref, w_ref, h_ref, *outs):
        xf = x_ref[...]
        r = lax.rsqrt(jnp.mean(xf * xf, axis=-1, keepdims=True) + EPS)
        hb = (xf * r * g_ref[...]).astype(BF16)
        h_ref[...] = hb
        for p, o_ref in enumerate(outs):
            o_ref[...] = _dot(hb, w_ref[_group_rows(p), :], NT).astype(BF16)

    row = lambda wd: pl.BlockSpec((tm, wd), lambda i: (i, 0))
    return pl.pallas_call(
        body, name="norm_in_proj", grid=(t // tm,),
        in_specs=[row(d), pl.BlockSpec((1, d), lambda i: (0, 0)),
                  pl.BlockSpec(w_t.shape, lambda i: (0, 0), pipeline_mode=pl.Buffered(1))],
        out_specs=(row(d),) + tuple(row(wd) for wd in widths),
        out_shape=(SDS((t, d), BF16),) + tuple(SDS((t, wd), BF16) for wd in widths),
        compiler_params=_params("parallel"),
    )(x, g, w_t)


def _dh_rmsnorm_bwd(dps, w_t, x, g, dres, exchange=None):
    t, d = x.shape
    tm = _tile(t, 256, 16)
    n = len(dps)

    def body(*refs):
        dp_refs, w_ref = refs[:n], refs[n]
        x_ref, g_ref, dres_ref, dx_ref, dg_ref = refs[n + 1:]
        dh = None
        for p, dp_ref in enumerate(dp_refs):
            part = _dot(dp_ref[...], w_ref[_group_rows(p), :], NN)
            dh = part if dh is None else dh + part
        dx, dg = _rmsnorm_bwd_math(x_ref[...], g_ref[...], dh)
        dx_ref[...] = dx + dres_ref[...]

        @pl.when(pl.program_id(0) == 0)
        def _():
            dg_ref[...] = jnp.zeros_like(dg_ref)

        dg_ref[...] += dg

    row = pl.BlockSpec((tm, d), lambda i: (i, 0))
    vec = pl.BlockSpec((1, d), lambda i: (0, 0))
    return _host_call(
        body, list(dps) + [w_t, x, g, dres],
        [pl.BlockSpec((tm, a.shape[1]), lambda i: (i, 0)) for a in dps]
        + [pl.BlockSpec(w_t.shape, lambda i: (0, 0), pipeline_mode=pl.Buffered(1)), row, vec, row],
        (row, vec), (SDS((t, d), F32), SDS((1, d), F32)),
        name="dh_rmsnorm_bwd", grid=(t // tm,), semantics=("arbitrary",), exchange=exchange)


def _rope_tables(t):
    rows = t // GRID_W
    row = jnp.repeat(jnp.arange(rows, dtype=F32), GRID_W)
    col = jnp.tile(jnp.arange(GRID_W, dtype=F32), rows)
    n_freq = HEAD_DIM // 4
    inv = ROPE_THETA ** (-jnp.arange(n_freq, dtype=F32) / n_freq)
    ang = jnp.stack([row[:, None] * inv, col[:, None] * inv], axis=1)
    cos, sin = jnp.cos(ang), jnp.sin(ang)
    c64 = jnp.concatenate([cos[:, 0], cos[:, 0], cos[:, 1], cos[:, 1]], axis=-1)
    s64 = jnp.concatenate([-sin[:, 0], sin[:, 0], -sin[:, 1], sin[:, 1]], axis=-1)
    return jnp.tile(c64, (1, 2)), jnp.tile(s64, (1, 2))


def _head_sums(v, lane):
    lo = jnp.sum(jnp.where(lane < HEAD_DIM, v, 0.0), axis=-1, keepdims=True)
    hi = jnp.sum(jnp.where(lane < HEAD_DIM, 0.0, v), axis=-1, keepdims=True)
    return jnp.where(lane < HEAD_DIM, lo, hi)


def _swap16(v, lane):
    return jnp.where((lane % 32) < 16, pltpu.roll(v, LANES - 16, 1), pltpu.roll(v, 16, 1))


def _attn_prep_fwd(pa, cs, sn, qg2, kg2):
    t = pa.shape[0]
    tq = _tile(t, 512, LANES)
    scale = HEAD_DIM ** -0.5

    def body(pa_ref, cs_ref, sn_ref, qg_ref, kg_ref, q_ref, qT_ref, k_ref, kT_ref, vT_ref):
        lane = lax.broadcasted_iota(jnp.int32, (tq, LANES), 1)
        c, s = cs_ref[...], sn_ref[...]

        def norm_rope(xg, g2):
            r = lax.rsqrt(_head_sums(xg * xg, lane) * (1.0 / HEAD_DIM) + EPS)
            xn = xg * r * g2
            return xn * c + _swap16(xn, lane) * s

        for gi in range(4):
            sl = slice(gi * LANES, (gi + 1) * LANES)
            qr = norm_rope(pa_ref[:, sl].astype(F32), qg_ref[...]) * scale
            q_ref[:, sl] = qr.astype(BF16)
            qT_ref[sl, :] = qr.T.astype(BF16)
        kr = norm_rope(pa_ref[:, 512:640].astype(F32), kg_ref[...])
        kT_ref[...] = kr.T.astype(BF16)
        vT_ref[...] = pa_ref[:, 640:768].astype(F32).T.astype(BF16)
        kr = kr.astype(BF16)
        for kvh in range(A_KV_HEADS):
            k_ref[kvh] = kr[:, kvh * HEAD_DIM:(kvh + 1) * HEAD_DIM]

    row = lambda w: pl.BlockSpec((tq, w), lambda i: (i, 0))
    col = lambda r: pl.BlockSpec((r, tq), lambda i: (0, i))
    vec = pl.BlockSpec((1, LANES), lambda i: (0, 0))
    hm = pl.BlockSpec((A_KV_HEADS, tq, HEAD_DIM), lambda i: (0, i, 0))
    return pl.pallas_call(
        body, name="attn_prep_fwd", grid=(t // tq,),
        in_specs=[row(W_A), row(LANES), row(LANES), vec, vec],
        out_specs=(row(BW), col(BW), hm, col(LANES), col(LANES)),
        out_shape=(SDS((t, BW), BF16), SDS((BW, t), BF16), SDS((A_KV_HEADS, t, HEAD_DIM), BF16),
                   SDS((LANES, t), BF16), SDS((LANES, t), BF16)),
        compiler_params=_params("parallel"),
    )(pa, cs, sn, qg2, kg2)


def _attn_fwd(qT, k, vT, pa, exchange=None):
    t = qT.shape[1]
    tq = _tile(t, 256, LANES)
    grp = A_HEADS // A_KV_HEADS

    ck = _tile(t, KEY_CHUNK, LANES)

    def body(qT_ref, k_ref, vT_ref, pa_ref, o_ref, lse_ref, y_ref):
        def finish_pair(pair, o_pair):
            o2 = jnp.concatenate(o_pair, axis=0).T
            cols = slice(pair * LANES, (pair + 1) * LANES)
            o_ref[:, cols] = o2
            z = pa_ref[:, 768 + pair * LANES:768 + (pair + 1) * LANES].astype(F32)
            y_ref[:, cols] = (o2 * (z * _sigmoid(z))).astype(BF16)

        n_c = t // ck
        items = [(h, c) for h in range(A_HEADS) for c in range(n_c)]

        def scores(h, c):
            return _dot(k_ref[h // grp, c * ck:(c + 1) * ck, :], qT_ref[h * HEAD_DIM:(h + 1) * HEAD_DIM, :], NN)

        def weighted_values(h, c, pT):
            kvh = h // grp
            return _dot(vT_ref[kvh * HEAD_DIM:(kvh + 1) * HEAD_DIM, c * ck:(c + 1) * ck], pT, NN)

        o_pair, sT, pending, acc = [], scores(*items[0]), None, None
        for i in range(len(items) + 1):
            item = items[i] if i < len(items) else None
            sT_next = scores(*items[i + 1]) if i + 1 < len(items) else None
            if pending is not None:
                (ph, pc, pT, alpha, l_run) = pending
                o_c = weighted_values(ph, pc, pT)
                acc = o_c if alpha is None else acc * alpha + o_c
                if pc == n_c - 1:
                    o_pair.append(acc / l_run)
                    if ph % 2:
                        finish_pair(ph // 2, o_pair)
                        o_pair = []
                pending = None
            if item is not None:
                h, c = item
                m_c = jnp.max(sT, axis=0, keepdims=True)
                if c == 0:
                    m_new, alpha = m_c, None
                else:
                    m_new = jnp.maximum(m, m_c)
                    alpha = jnp.exp(m - m_new)
                pT = jnp.exp(sT - m_new)
                l_c = jnp.sum(pT, axis=0, keepdims=True)
                l = l_c if c == 0 else l * alpha + l_c
                m = m_new
                if c == n_c - 1:
                    lse_ref[h:h + 1, :] = m + jnp.log(l)
                pending = (h, c, pT.astype(BF16), alpha, l)
            sT = sT_next

    row = lambda w: pl.BlockSpec((tq, w), lambda i: (i, 0))
    return _host_call(
        body, (qT, k, vT, pa),
        [pl.BlockSpec((BW, tq), lambda i: (0, i)), pl.BlockSpec((A_KV_HEADS, t, HEAD_DIM), lambda i: (0, 0, 0)),
         pl.BlockSpec((A_KV_HEADS * HEAD_DIM, t), lambda i: (0, 0)), row(W_A)],
        (row(BW), pl.BlockSpec((A_HEADS, tq), lambda i: (0, i)), row(BW)),
        (SDS((t, BW), F32), SDS((A_HEADS, t), F32), SDS((t, BW), BF16)),
        name="attn_fwd", grid=(t // tq,), semantics=("parallel",), exchange=exchange)


def _attn_bwd(q, qT, k, kT, vT, pa, o, lse, dy, exchange=None):
    t = q.shape[0]
    tq = _tile(t, 256, LANES)
    grp = A_HEADS // A_KV_HEADS
    gw = grp * HEAD_DIM

    def body(q_ref, qT_ref, k_ref, kT_ref, vT_ref, z_ref, o_ref, lse_ref, dy_ref, dq_ref, dkT_ref, dvT_ref):
        @pl.when(pl.program_id(1) == 0)
        def _():
            dkT_ref[...] = jnp.zeros_like(dkT_ref)
            dvT_ref[...] = jnp.zeros_like(dvT_ref)

        z = z_ref[...].astype(F32)
        do = dy_ref[...] * (z * _sigmoid(z))
        doo = do * o_ref[...]
        doT = do.T
        kk, kT, vT = k_ref[...], kT_ref[...], vT_ref[...]
        heads = [slice(j * HEAD_DIM, (j + 1) * HEAD_DIM) for j in range(grp)]

        def scores(hs):
            return _dot(q_ref[:, hs], kT, NN), _dot(do[:, hs].astype(BF16), vT, NN)

        s, dp = scores(heads[0])
        for j, hs in enumerate(heads):
            nxt = scores(heads[j + 1]) if j + 1 < grp else None
            delta = jnp.sum(doo[:, hs], axis=-1, keepdims=True)
            p = jnp.exp(s - lse_ref[:, j:j + 1])
            ds_b = (p * (dp - delta)).astype(BF16)
            dq_ref[:, hs] = _dot(ds_b, kk, NN)
            dkT_ref[...] += _dot(qT_ref[hs, :], ds_b, NN)
            dvT_ref[...] += _dot(doT[hs, :].astype(BF16), p.astype(BF16), NN)
            if nxt is not None:
                s, dp = nxt

    grp_blk = pl.BlockSpec((tq, gw), lambda g, i: (i, g))
    kvT_blk = pl.BlockSpec((HEAD_DIM, t), lambda g, i: (g, 0))
    acc_blk = pl.BlockSpec((None, HEAD_DIM, t), lambda g, i: (g, 0, 0))
    return _host_call(
        body, (q, qT, k, kT, vT, pa, o, lse, dy),
        [grp_blk, pl.BlockSpec((gw, tq), lambda g, i: (g, i)), pl.BlockSpec((None, t, HEAD_DIM), lambda g, i: (g, 0, 0)),
         kvT_blk, kvT_blk, pl.BlockSpec((tq, gw), lambda g, i: (i, 768 // gw + g)), grp_blk,
         pl.BlockSpec((None, tq, grp), lambda g, i: (g, i, 0)), grp_blk],
        (grp_blk, acc_blk, acc_blk),
        (SDS((t, BW), F32), SDS((A_KV_HEADS, HEAD_DIM, t), F32), SDS((A_KV_HEADS, HEAD_DIM, t), F32)),
        name="attn_bwd", grid=(A_KV_HEADS, t // tq), semantics=("arbitrary", "arbitrary"), exchange=exchange)


def _attn_prep_bwd(pa, cs, sn, qg2, kg2, dq, dkT, dvT, dy, o, exchange=None):
    t = pa.shape[0]
    tq = _tile(t, 512, LANES)
    scale = HEAD_DIM ** -0.5

    def body(pa_ref, cs_ref, sn_ref, qg_ref, kg_ref, dq_ref, dkT_ref, dvT_ref, dy_ref, o_ref, dpa_ref, dqg_ref, dkg_ref):
        lane = lax.broadcasted_iota(jnp.int32, (tq, LANES), 1)
        c, s = cs_ref[...], sn_ref[...]

        @pl.when(pl.program_id(0) == 0)
        def _():
            dqg_ref[...] = jnp.zeros_like(dqg_ref)
            dkg_ref[...] = jnp.zeros_like(dkg_ref)

        def norm_rope_bwd(xg, g2, dout):
            r = lax.rsqrt(_head_sums(xg * xg, lane) * (1.0 / HEAD_DIM) + EPS)
            xh = xg * r
            dxn = dout * c + _swap16(dout * s, lane)
            gd = dxn * g2
            dx = r * (gd - xh * (_head_sums(gd * xh, lane) * (1.0 / HEAD_DIM)))
            return dx, jnp.sum(dxn * xh, axis=0, keepdims=True)

        for gi in range(4):
            sl = slice(gi * LANES, (gi + 1) * LANES)
            dx, dg = norm_rope_bwd(pa_ref[:, sl].astype(F32), qg_ref[...], dq_ref[:, sl] * scale)
            dpa_ref[:, sl] = dx.astype(BF16)
            dqg_ref[...] += dg
        dx, dg = norm_rope_bwd(pa_ref[:, 512:640].astype(F32), kg_ref[...], dkT_ref[...].T)
        dpa_ref[:, 512:640] = dx.astype(BF16)
        dkg_ref[...] += dg
        dpa_ref[:, 640:768] = dvT_ref[...].T.astype(BF16)
        z = pa_ref[:, 768:1280].astype(F32)
        _, dsilu = _silu_and_grad(z)
        dpa_ref[:, 768:1280] = (dy_ref[...] * o_ref[...] * dsilu).astype(BF16)

    row = lambda w: pl.BlockSpec((tq, w), lambda i: (i, 0))
    col = pl.BlockSpec((LANES, tq), lambda i: (0, i))
    vec = pl.BlockSpec((1, LANES), lambda i: (0, 0))
    return _host_call(
        body, (pa, cs, sn, qg2, kg2, dq, dkT, dvT, dy, o),
        [row(W_A), row(LANES), row(LANES), vec, vec, row(BW), col, col, row(BW), row(BW)],
        (row(W_A), vec, vec), (SDS((t, W_A), BF16), SDS((1, LANES), F32), SDS((1, LANES), F32)),
        name="attn_prep_bwd", grid=(t // tq,), semantics=("arbitrary",), exchange=exchange)


def _layer_norm(v, g, b):
    mu = jnp.mean(v, axis=-1, keepdims=True)
    xc = v - mu
    rs = lax.rsqrt(jnp.mean(xc * xc, axis=-1, keepdims=True) + EPS)
    xh = xc * rs
    return xh * g + b, xh, rs


def _gmlp_fwd(pb, lng, lnb, ws, bs):
    t = pb.shape[0]
    tb = _tile(t, 512, CHUNK)

    def body(pb_ref, g_ref, b_ref, ws_ref, bs_ref, y_ref):
        vln, _, _ = _layer_norm(pb_ref[:, BW:2 * BW].astype(F32), g_ref[...], b_ref[...])
        vb = vln.astype(BF16)
        for gi in range(B_GROUPS):
            w = ws_ref[gi].astype(BF16)
            cs_ = slice(gi * CHUNK, (gi + 1) * CHUNK)
            for n in range(tb // CHUNK):
                rs_ = slice(n * CHUNK, (n + 1) * CHUNK)
                mixed = _dot(w, vb[rs_, cs_], NN) + bs_ref[gi]
                z = pb_ref[rs_, 2 * BW + gi * CHUNK:2 * BW + (gi + 1) * CHUNK].astype(F32)
                y_ref[rs_, cs_] = (pb_ref[rs_, cs_].astype(F32) * mixed * (z * _sigmoid(z))).astype(BF16)

    return pl.pallas_call(
        body, name="gmlp_fwd", grid=(t // tb,),
        in_specs=[pl.BlockSpec((tb, W_B), lambda i: (i, 0)),
                  pl.BlockSpec((1, BW), lambda i: (0, 0)), pl.BlockSpec((1, BW), lambda i: (0, 0)),
                  pl.BlockSpec((B_GROUPS, CHUNK, CHUNK), lambda i: (0, 0, 0)),
                  pl.BlockSpec((B_GROUPS, CHUNK, 1), lambda i: (0, 0, 0))],
        out_specs=pl.BlockSpec((tb, BW), lambda i: (i, 0)),
        out_shape=SDS((t, BW), BF16), compiler_params=_params("parallel"),
    )(pb, lng, lnb, ws, bs)


def _gmlp_bwd(pb, lng, lnb, ws, bs, dy):
    t = pb.shape[0]
    tb = _tile(t, 256, CHUNK)

    def body(pb_ref, g_ref, b_ref, ws_ref, bs_ref, dy_ref, dpb_ref, dws_ref, dbs_ref, dg_ref, db_ref, dvln_ref):
        @pl.when(pl.program_id(0) == 0)
        def _():
            dws_ref[...] = jnp.zeros_like(dws_ref)
            dbs_ref[...] = jnp.zeros_like(dbs_ref)
            dg_ref[...] = jnp.zeros_like(dg_ref)
            db_ref[...] = jnp.zeros_like(db_ref)

        vln, xh, rs = _layer_norm(pb_ref[:, BW:2 * BW].astype(F32), g_ref[...], b_ref[...])
        vb = vln.astype(BF16)
        for gi in range(B_GROUPS):
            w = ws_ref[gi].astype(BF16)
            cs_ = slice(gi * CHUNK, (gi + 1) * CHUNK)
            for n in range(tb // CHUNK):
                rs_ = slice(n * CHUNK, (n + 1) * CHUNK)
                vbc = vb[rs_, cs_]
                mixed = _dot(w, vbc, NN) + bs_ref[gi]
                zs = slice(2 * BW + gi * CHUNK, 2 * BW + (gi + 1) * CHUNK)
                z = pb_ref[rs_, zs].astype(F32)
                u = pb_ref[rs_, cs_].astype(F32)
                sil, dsil = _silu_and_grad(z)
                dyc = dy_ref[rs_, cs_]
                dmixed = dyc * u * sil
                dpb_ref[rs_, cs_] = (dyc * mixed * sil).astype(BF16)
                dpb_ref[rs_, zs] = (dyc * u * mixed * dsil).astype(BF16)
                dmb = dmixed.astype(BF16)
                dws_ref[gi] += _dot(dmb, vbc, NT)
                dbs_ref[gi] += jnp.sum(dmixed, axis=-1, keepdims=True)
                dvln_ref[rs_, cs_] = _dot(w, dmb, TN)
        dvln = dvln_ref[...]
        dg_ref[...] += jnp.sum(dvln * xh, axis=0, keepdims=True)
        db_ref[...] += jnp.sum(dvln, axis=0, keepdims=True)
        gd = dvln * g_ref[...]
        dv = rs * (gd - jnp.mean(gd, axis=-1, keepdims=True) - xh * jnp.mean(gd * xh, axis=-1, keepdims=True))
        dpb_ref[:, BW:2 * BW] = dv.astype(BF16)

    vec = pl.BlockSpec((1, BW), lambda i: (0, 0))
    wsb = pl.BlockSpec((B_GROUPS, CHUNK, CHUNK), lambda i: (0, 0, 0))
    bsb = pl.BlockSpec((B_GROUPS, CHUNK, 1), lambda i: (0, 0, 0))
    return pl.pallas_call(
        body, name="gmlp_bwd", grid=(t // tb,),
        in_specs=[pl.BlockSpec((tb, W_B), lambda i: (i, 0)), vec, vec, wsb, bsb, pl.BlockSpec((tb, BW), lambda i: (i, 0))],
        out_specs=(pl.BlockSpec((tb, W_B), lambda i: (i, 0)), wsb, bsb, vec, vec),
        out_shape=(SDS((t, W_B), BF16), SDS((B_GROUPS, CHUNK, CHUNK), F32), SDS((B_GROUPS, CHUNK, 1), F32),
                   SDS((1, BW), F32), SDS((1, BW), F32)),
        scratch_shapes=[pltpu.VMEM((tb, BW), F32)],
        compiler_params=_params("arbitrary"),
    )(pb, lng, lnb, ws, bs, dy)


def _mem_scores(pm_ref, kv_ref, h):
    hs = slice(h * M_HEAD_DIM, (h + 1) * M_HEAD_DIM)
    return _dot(pm_ref[:, hs].astype(BF16), kv_ref[:, hs].astype(BF16), NT) * (M_HEAD_DIM ** -0.5)


def _softmax_rows(s):
    e = jnp.exp(s - jnp.max(s, axis=-1, keepdims=True))
    return e / jnp.sum(e, axis=-1, keepdims=True)


def _mem_attn_fwd(pm, kv):
    t = pm.shape[0]
    tq = _tile(t, 512, 8)
    ml = kv.shape[0]

    def body(pm_ref, kv_ref, y_ref):
        s_next = _mem_scores(pm_ref, kv_ref, 0)
        for h in range(M_HEADS):
            hs = slice(h * M_HEAD_DIM, (h + 1) * M_HEAD_DIM)
            vh = kv_ref[:, BW + h * M_HEAD_DIM:BW + (h + 1) * M_HEAD_DIM].astype(BF16)
            s, s_next = s_next, (_mem_scores(pm_ref, kv_ref, h + 1) if h + 1 < M_HEADS else None)
            p = _softmax_rows(s)
            o = _dot(p.astype(BF16), vh, NN)
            z = pm_ref[:, BW + h * M_HEAD_DIM:BW + (h + 1) * M_HEAD_DIM].astype(F32)
            y_ref[:, hs] = (o * (z * _sigmoid(z))).astype(BF16)

    return pl.pallas_call(
        body, name="mem_attn_fwd", grid=(t // tq,),
        in_specs=[pl.BlockSpec((tq, W_M), lambda i: (i, 0)), pl.BlockSpec((ml, 2 * BW), lambda i: (0, 0))],
        out_specs=pl.BlockSpec((tq, BW), lambda i: (i, 0)),
        out_shape=SDS((t, BW), BF16), compiler_params=_params("parallel"),
    )(pm, kv)


def _mem_attn_bwd(pm, kv, dy):
    t = pm.shape[0]
    tq = _tile(t, 512, 8)
    ml = kv.shape[0]
    scale = M_HEAD_DIM ** -0.5

    def body(pm_ref, kv_ref, dy_ref, dpm_ref, dkv_ref):
        @pl.when(pl.program_id(0) == 0)
        def _():
            dkv_ref[...] = jnp.zeros_like(dkv_ref)

        s_next = _mem_scores(pm_ref, kv_ref, 0)
        for h in range(M_HEADS):
            hs = slice(h * M_HEAD_DIM, (h + 1) * M_HEAD_DIM)
            zs = slice(BW + h * M_HEAD_DIM, BW + (h + 1) * M_HEAD_DIM)
            kh = kv_ref[:, hs].astype(BF16)
            vh = kv_ref[:, zs].astype(BF16)
            qh = pm_ref[:, hs].astype(BF16)
            s, s_next = s_next, (_mem_scores(pm_ref, kv_ref, h + 1) if h + 1 < M_HEADS else None)
            p = _softmax_rows(s)
            pb = p.astype(BF16)
            o = _dot(pb, vh, NN)
            sil, dsil = _silu_and_grad(pm_ref[:, zs].astype(F32))
            dyh = dy_ref[:, hs]
            do = dyh * sil
            dpm_ref[:, zs] = (dyh * o * dsil).astype(BF16)
            delta = jnp.sum(do * o, axis=-1, keepdims=True)
            do_b = do.astype(BF16)
            dp = _dot(do_b, vh, NT)
            dr_b = (p * (dp - delta) * scale).astype(BF16)
            dpm_ref[:, hs] = _dot(dr_b, kh, NN).astype(BF16)
            dkv_ref[:, hs] += _dot(dr_b, qh, TN)
            dkv_ref[:, zs] += _dot(pb, do_b, TN)

    kvb = pl.BlockSpec((ml, 2 * BW), lambda i: (0, 0))
    return pl.pallas_call(
        body, name="mem_attn_bwd", grid=(t // tq,),
        in_specs=[pl.BlockSpec((tq, W_M), lambda i: (i, 0)), kvb, pl.BlockSpec((tq, BW), lambda i: (i, 0))],
        out_specs=(pl.BlockSpec((tq, W_M), lambda i: (i, 0)), kvb),
        out_shape=(SDS((t, W_M), BF16), SDS((ml, 2 * BW), F32)),
        compiler_params=_params("arbitrary"),
    )(pm, kv, dy)


def _merge_fwd(ya, yb, ym, pg, wbr, x, w_out, exchange=None):
    t = ya.shape[0]
    tm = _tile(t, 512, 8)

    def body(ya_ref, yb_ref, ym_ref, pg_ref, w_ref, x_ref, wo_ref, m_ref, xn_ref):
        ups = [_dot(y_ref[...], w_ref[n], NN) for n, y_ref in enumerate((ya_ref, yb_ref, ym_ref))]
        acc = None
        for n, up in enumerate(ups):
            term = _sigmoid(pg_ref[:, n * D_MODEL:(n + 1) * D_MODEL].astype(F32)) * up
            acc = term if acc is None else acc + term
        merged = acc.astype(BF16)
        m_ref[...] = merged
        xn_ref[...] = x_ref[...] + _dot(merged, wo_ref[...], NN)

    yb_spec = pl.BlockSpec((tm, BW), lambda i: (i, 0))
    row = pl.BlockSpec((tm, D_MODEL), lambda i: (i, 0))
    return _host_call(
        body, (ya, yb, ym, pg, wbr, x, w_out),
        [yb_spec, yb_spec, yb_spec, pl.BlockSpec((tm, W_G), lambda i: (i, 0)),
         pl.BlockSpec((3, BW, D_MODEL), lambda i: (0, 0, 0)), row, pl.BlockSpec(w_out.shape, lambda i: (0, 0))],
        (row, row), (SDS((t, D_MODEL), BF16), SDS((t, D_MODEL), F32)),
        name="merge_fwd", grid=(t // tm,), semantics=("parallel",), exchange=exchange)


def _merge_bwd(ya, yb, ym, pg, wbr, dx_out, w_out, exchange=None):
    t = ya.shape[0]
    tm = _tile(t, 512, 16)

    def body(ya_ref, yb_ref, ym_ref, pg_ref, w_ref, dx_ref, wo_ref, dya_ref, dyb_ref, dym_ref, dpg_ref, dw_ref):
        @pl.when(pl.program_id(0) == 0)
        def _():
            dw_ref[...] = jnp.zeros_like(dw_ref)

        dmf = _dot(dx_ref[...].astype(BF16), wo_ref[...], NT)
        branches = ((ya_ref, dya_ref), (yb_ref, dyb_ref), (ym_ref, dym_ref))

        def gate(n):
            gt = _sigmoid(pg_ref[:, n * D_MODEL:(n + 1) * D_MODEL].astype(F32))
            return gt, (dmf * gt).astype(BF16)

        gt, dup = gate(0)
        for n, (y_ref, dy_ref) in enumerate(branches):
            nxt = gate(n + 1) if n + 1 < len(branches) else None
            y, w = y_ref[...], w_ref[n]
            dy_ref[...] = _dot(dup, w, NT)
            dw_ref[n] += _dot(y, dup, TN)
            up = _dot(y, w, NN)
            dpg_ref[:, n * D_MODEL:(n + 1) * D_MODEL] = (dmf * up * gt * (1.0 - gt)).astype(BF16)
            if nxt is not None:
                gt, dup = nxt

    y_spec = pl.BlockSpec((tm, BW), lambda i: (i, 0))
    w_spec = pl.BlockSpec((3, BW, D_MODEL), lambda i: (0, 0, 0))
    return _host_call(
        body, (ya, yb, ym, pg, wbr, dx_out, w_out),
        [y_spec, y_spec, y_spec, pl.BlockSpec((tm, W_G), lambda i: (i, 0)), w_spec,
         pl.BlockSpec((tm, D_MODEL), lambda i: (i, 0)), pl.BlockSpec(w_out.shape, lambda i: (0, 0))],
        (y_spec, y_spec, y_spec, pl.BlockSpec((tm, W_G), lambda i: (i, 0)), w_spec),
        (SDS((t, BW), F32), SDS((t, BW), F32), SDS((t, BW), F32), SDS((t, W_G), BF16), SDS((3, BW, D_MODEL), F32)),
        name="merge_bwd", grid=(t // tm,), semantics=("arbitrary",), exchange=exchange)


def _loss_head(x, g, target):
    t, d = x.shape
    tm = _tile(t, 512, 8)

    def body(x_ref, g_ref, t_ref, loss_ref, dx_ref, dg_ref):
        @pl.when(pl.program_id(0) == 0)
        def _():
            loss_ref[...] = jnp.zeros_like(loss_ref)
            dg_ref[...] = jnp.zeros_like(dg_ref)

        xf = x_ref[...]
        r = lax.rsqrt(jnp.mean(xf * xf, axis=-1, keepdims=True) + EPS)
        xh = xf * r
        err = xh * g_ref[...] - t_ref[...]
        per_tok = jnp.mean(err * err, axis=-1, keepdims=True)
        loss_ref[...] += 0.5 * jnp.sum(per_tok, axis=0, keepdims=True)
        dy = err * (1.0 / d)
        gd = dy * g_ref[...]
        dx_ref[...] = r * (gd - xh * jnp.mean(gd * xh, axis=-1, keepdims=True))
        dg_ref[...] += jnp.sum(dy * xh, axis=0, keepdims=True)

    row = pl.BlockSpec((tm, d), lambda i: (i, 0))
    vec = pl.BlockSpec((1, d), lambda i: (0, 0))
    return pl.pallas_call(
        body, name="loss_head", grid=(t // tm,),
        in_specs=[row, vec, row],
        out_specs=(pl.BlockSpec((1, 1), lambda i: (0, 0)), row, vec),
        out_shape=(SDS((1, 1), F32), SDS((t, d), F32), SDS((1, d), F32)),
        compiler_params=_params("arbitrary"),
    )(x, g, target)


def _layer_fwd(x, mem, w, tabs, next_shards=None):
    cs, sn = tabs
    riding = next_shards is not None
    h, pa, pb, pm, pg = _norm_in_proj(x, w["norm_g"], w["w_in_t"])
    q, qT, k, kT, vT = _attn_prep_fwd(pa, cs, sn, w["qg2"], w["kg2"])
    (o, lse, ya), gathered = _attn_fwd(qT, k, vT, pa, exchange=_gather_first_hop(next_shards) if riding else None)
    yb = _gmlp_fwd(pb, w["ln_g"], w["ln_b"], w["w_s"], w["b_s"])
    memn = _rmsnorm_fwd(mem, w["mem_g"], "mem_rmsnorm_fwd")
    kv = _mm(memn, w["w_kv"], "nn", "mem_kv")
    ym = _mem_attn_fwd(pm, kv)
    (merged, x_next), gathered = _merge_fwd(ya, yb, ym, pg, w["w_br"], x, w["w_out"],
                                            exchange=_gather_forward(gathered) if riding else None)
    saved = dict(x=x, h=h, pa=pa, pb=pb, pm=pm, pg=pg, q=q, qT=qT, k=k, kT=kT, vT=vT, o=o, lse=lse, ya=ya, yb=yb, ym=ym,
                 memn=memn, kv=kv, merged=merged)
    return x_next, saved, gathered


def _layer_bwd(dx_out, mem, w, s, tabs, pending=None, core=None, scatter_own=False):
    cs, sn = tabs
    t = dx_out.shape[0]
    riding = pending is not None
    by_owner = lambda a: a.reshape((N_DEV // 2, 2, -1) + a.shape[-2:])
    row_shards = lambda a: by_owner(a.reshape(N_DEV, a.shape[0] // N_DEV, a.shape[1]))
    d_w_out = _mm_tn(s["merged"], dx_out, "d_w_out")
    (dya, dyb, dym, dpg, d_w_br), recv = _merge_bwd(s["ya"], s["yb"], s["ym"], s["pg"], w["w_br"], dx_out, w["w_out"],
                                                    exchange=_scatter_to_sibling(pending) if riding else None)
    parts = _pair_sums(BIG, pending, recv, core) if riding else None
    grads = dict(w_br=by_owner(_split_w_br(d_w_br)), w_out=row_shards(d_w_out))
    early = [grads[n] for n in BIG[2:]]
    grp = A_HEADS // A_KV_HEADS
    lse_cols = s["lse"].reshape(A_KV_HEADS, grp, t).transpose(0, 2, 1)
    (dq, dkT, dvT), landed = _attn_bwd(
        s["q"], s["qT"], s["k"], s["kT"], s["vT"], s["pa"], s["o"], lse_cols, dya,
        exchange=_both(_scatter_to_chips(parts) if riding else None, _scatter_to_sibling(early) if scatter_own else None))
    from_chips, early_recv = (landed[:len(BIG)], landed[len(BIG):]) if riding else (None, landed)
    early_parts = _pair_sums(BIG[2:], early, early_recv, core) if scatter_own else None
    (dpa, d_qg2, d_kg2), early_from_chips = _attn_prep_bwd(
        s["pa"], cs, sn, w["qg2"], w["kg2"], dq, dkT.reshape(-1, t), dvT.reshape(-1, t), dya, s["o"],
        exchange=_scatter_to_chips(early_parts) if scatter_own else None)
    dpb, d_w_s, d_b_s, d_ln_g, d_ln_b = _gmlp_bwd(s["pb"], w["ln_g"], w["ln_b"], w["w_s"], w["b_s"], dyb)
    dpm, dkv = _mem_attn_bwd(s["pm"], s["kv"], dym)
    d_w_kv = _mm_tn(s["memn"], dkv, "d_w_kv")
    dmemn = _mm(dkv, w["w_kv"], "nt", "d_memn")
    d_mem_g = _rmsnorm_dg(mem, w["mem_g"], dmemn, "mem_rmsnorm_bwd")
    dps = (dpa, dpb, dpm, dpg)
    grads.update(w_in=row_shards(_d_w_in_t(dps, s["h"])), w_mem_kv=row_shards(d_w_kv))
    own = None
    if scatter_own:
        late = [grads[n] for n in BIG[:2]]
        late_parts = _pair_sums(BIG[:2], late, _exchange_call(_scatter_to_sibling(late), "rs_sibling_swap"), core)
        own = _scatter_to_chips(late_parts)
    (dx_in, d_norm_g), late_from_chips = _dh_rmsnorm_bwd(dps, w["w_in_t"], s["x"], w["norm_g"], dx_out, exchange=own)
    if scatter_own:
        own_parts, own_from_chips = late_parts + early_parts, tuple(late_from_chips) + tuple(early_from_chips)
    grads.update(norm_g=d_norm_g[0], q_norm_g=d_qg2[0, :HEAD_DIM] + d_qg2[0, HEAD_DIM:],
                 k_norm_g=d_kg2[0, :HEAD_DIM] + d_kg2[0, HEAD_DIM:], sg_ln_g=d_ln_g[0], sg_ln_b=d_ln_b[0],
                 w_s=d_w_s, b_s=d_b_s[:, :, 0], mem_norm_g=d_mem_g[0])
    return (dx_in, grads, ((parts, from_chips) if riding else None),
            ((own_parts, own_from_chips) if scatter_own else None))


def _layer_weights(l, w_in_t, w_kv, w_br, w_out, small):
    tile2 = lambda g: jnp.tile(g.reshape(1, -1), (1, 2))
    return dict(
        w_in_t=w_in_t, w_kv=w_kv, w_br=w_br, w_out=w_out,
        norm_g=small["norm_g"][l].reshape(1, -1), qg2=tile2(small["q_norm_g"][l]), kg2=tile2(small["k_norm_g"][l]),
        ln_g=small["sg_ln_g"][l].reshape(1, -1), ln_b=small["sg_ln_b"][l].reshape(1, -1),
        w_s=small["w_s"][l], b_s=small["b_s"][l][:, :, None], mem_g=small["mem_norm_g"][l].reshape(1, -1))


def _position():
    x, y, c = lax.axis_index("x"), lax.axis_index("y"), lax.axis_index("c")
    return x, y, c, [(1 - x, y), (x, 1 - y), (1 - x, 1 - y)]


def _gather_first_hop(shards):
    n = len(shards)

    def make(x_refs, out_refs, sems):
        send_sems, recv_sems, local_sems = sems
        x, y, c, chips = _position()
        me = 4 * x + 2 * y + c
        peers = [(x, y, 1 - c)] + [(cx, cy, c) for cx, cy in chips]
        copies = [pltpu.make_async_copy(x_refs[t], out_refs[t].at[me], local_sems.at[t]) for t in range(n)]
        copies += [pltpu.make_async_remote_copy(
            src_ref=x_refs[t], dst_ref=out_refs[t].at[me], send_sem=send_sems.at[t, k], recv_sem=recv_sems.at[t, k],
            device_id=peer, device_id_type=MESH_ID) for t in range(n) for k, peer in enumerate(peers)]
        return copies

    return _Exchange(shards, [SDS((N_DEV,) + a.shape, a.dtype) for a in shards],
                     [pltpu.SemaphoreType.DMA((n, 4)), pltpu.SemaphoreType.DMA((n, 4)), pltpu.SemaphoreType.DMA((n,))], make)


def _gather_forward(gathered):
    n = len(gathered)

    def make(in_refs, out_refs, sems):
        send_sems, recv_sems = sems
        x, y, c, chips = _position()
        return [pltpu.make_async_remote_copy(
            src_ref=in_refs[t].at[4 * cx + 2 * cy + c], dst_ref=out_refs[t].at[4 * cx + 2 * cy + c],
            send_sem=send_sems.at[t, j], recv_sem=recv_sems.at[t, j], device_id=(x, y, 1 - c), device_id_type=MESH_ID)
            for t in range(n) for j, (cx, cy) in enumerate(chips)]

    return _Exchange(gathered, [SDS(a.shape, a.dtype) for a in gathered],
                     [pltpu.SemaphoreType.DMA((n, 3)), pltpu.SemaphoreType.DMA((n, 3))], make,
                     aliases={t: t for t in range(n)})


def _all_gather(shards, name):
    return _exchange_call(_gather_forward(_exchange_call(_gather_first_hop(shards), name + "_hop1")), name + "_hop2")


def _assemble_w_br(gathered):
    _, nb, rows, shard = gathered.shape

    def body(g_ref, o_ref):
        for d in range(N_DEV):
            o_ref[:, :, d * shard:(d + 1) * shard] = g_ref[d]

    return pl.pallas_call(
        body, name="assemble_w_br", out_shape=SDS((nb, rows, N_DEV * shard), gathered.dtype),
        compiler_params=pltpu.CompilerParams(vmem_limit_bytes=VMEM_LIMIT),
    )(gathered)


def _split_w_br(dw):
    nb, rows, cols = dw.shape
    shard = cols // N_DEV

    def body(d_ref, o_ref):
        for d in range(N_DEV):
            o_ref[d] = d_ref[:, :, d * shard:(d + 1) * shard].astype(BF16)

    return pl.pallas_call(
        body, name="split_w_br", out_shape=SDS((N_DEV, nb, rows, shard), BF16),
        compiler_params=pltpu.CompilerParams(vmem_limit_bytes=VMEM_LIMIT),
    )(dw)


def _scatter_to_sibling(dests):
    n = len(dests)

    def make(d_refs, recv_refs, sems):
        send_sems, recv_sems = sems
        x, y, c, _ = _position()
        return [pltpu.make_async_remote_copy(
            src_ref=d_refs[t].at[:, 1 - c], dst_ref=recv_refs[t], send_sem=send_sems.at[t],
            recv_sem=recv_sems.at[t], device_id=(x, y, 1 - c), device_id_type=MESH_ID) for t in range(n)]

    return _Exchange(dests, [SDS(a.shape[:1] + a.shape[2:], a.dtype) for a in dests],
                     [pltpu.SemaphoreType.DMA((n,)), pltpu.SemaphoreType.DMA((n,))], make)


def _scatter_to_chips(parts):
    n = len(parts)

    def make(p_refs, recv_refs, sems):
        send_sems, recv_sems = sems
        _, _, c, chips = _position()
        return [pltpu.make_async_remote_copy(
            src_ref=p_refs[t].at[2 * cx + cy], dst_ref=recv_refs[t].at[k], send_sem=send_sems.at[t, k],
            recv_sem=recv_sems.at[t, k], device_id=(cx, cy, c), device_id_type=MESH_ID)
            for t in range(n) for k, (cx, cy) in enumerate(chips)]

    return _Exchange(parts, [SDS((3,) + a.shape[1:], a.dtype) for a in parts],
                     [pltpu.SemaphoreType.DMA((n, 3)), pltpu.SemaphoreType.DMA((n, 3))], make)


def _pair_sums(names, dests, recv, core):
    return [_pair_sum(d, r, core, "rs_pair_sum_" + n) for n, d, r in zip(names, dests, recv)]


def _pair_sum(dest, recv, core, name):
    _, _, na, r, cdim = dest.shape

    def body(core_ref, a_ref, b_ref, o_ref):
        o_ref[...] = (a_ref[...].astype(F32) + b_ref[...].astype(F32)).astype(o_ref.dtype)

    blk = pl.BlockSpec((None, na, r, cdim), lambda j, core_ref: (j, 0, 0, 0))
    return pl.pallas_call(
        body, name=name, out_shape=SDS(recv.shape, BF16),
        grid_spec=pltpu.PrefetchScalarGridSpec(
            num_scalar_prefetch=1, grid=(4,),
            in_specs=[pl.BlockSpec((None, None, na, r, cdim), lambda j, core_ref: (j, core_ref[0], 0, 0, 0)), blk],
            out_specs=blk),
        compiler_params=_params("parallel"),
    )(core, dest, recv)


def _adamw_math(w, g, m, v):
    m = ADAM_B1 * m + (1.0 - ADAM_B1) * g
    v = ADAM_B2 * v + (1.0 - ADAM_B2) * (g * g)
    m_hat = m / (1.0 - ADAM_B1 ** ADAM_STEP)
    v_hat = v / (1.0 - ADAM_B2 ** ADAM_STEP)
    delta = -ADAM_LR * (m_hat / (jnp.sqrt(v_hat) + ADAM_EPS) + ADAM_WD * w)
    return delta, m, v


def _sum_and_adamw(parts, w, m, v, name):
    n, r, ln = parts.shape
    tr = _tile(r, 512, 16)

    def body(p_ref, w_ref, m_ref, v_ref, g_out, d_out, m_out, v_out):
        g = p_ref[0].astype(F32)
        for j in range(1, n):
            g = g + p_ref[j].astype(F32)
        delta, nm, nv = _adamw_math(w_ref[...], g, m_ref[...], v_ref[...])
        g_out[...] = g
        d_out[...] = delta
        m_out[...] = nm
        v_out[...] = nv

    blk = pl.BlockSpec((tr, ln), lambda i: (i, 0))
    return pl.pallas_call(
        body, name=name, grid=(r // tr,),
        in_specs=[pl.BlockSpec((n, tr, ln), lambda i: (0, i, 0)), blk, blk, blk],
        out_specs=(blk, blk, blk, blk), out_shape=tuple(SDS((r, ln), F32) for _ in range(4)),
        compiler_params=_params("parallel"),
    )(parts, w, m, v)


BIG = ("w_in", "w_mem_kv", "w_br", "w_out")
SMALL = ("norm_g", "q_norm_g", "k_norm_g", "sg_ln_g", "sg_ln_b", "w_s", "b_s", "mem_norm_g", "final_g")


def _pack(arrs, row_unit=16):
    flat = jnp.concatenate([a.reshape(-1) for a in arrs])
    pad = (-flat.shape[0]) % (row_unit * LANES)
    if pad:
        flat = jnp.concatenate([flat, jnp.zeros((pad,), flat.dtype)])
    return flat.reshape(-1, LANES)


def _unpack(buf, shapes):
    flat = buf.reshape(-1)
    out, off = [], 0
    for shp in shapes:
        n = 1
        for s_ in shp:
            n *= s_
        out.append(flat[off:off + n].reshape(shp))
        off += n
    return out


def _shard_sum_adamw(part, from_chips, chip, w, m, v, layer, prev, name):
    _, na, r, cdim = part.shape
    flat = w.ndim == 3
    assert not flat or na == 1
    tr = _tile(r, max(8, (256 * 1024) // (na * cdim)), 8)
    n_prev = 0 if prev is None else len(prev)

    def body(chip_ref, p_ref, f_ref, w_ref, m_ref, v_ref, *rest):
        g_out, d_out, m_out, v_out = rest[n_prev:]
        g = p_ref[...].astype(F32)
        for j in range(3):
            g = g + f_ref[j].astype(F32)
        delta, nm, nv = _adamw_math(w_ref[...], g, m_ref[...], v_ref[...])
        g_out[...] = g
        d_out[...] = delta
        m_out[...] = nm
        v_out[...] = nv

    a_blk = None if flat else na
    if flat:
        lay = pl.BlockSpec((None, tr, cdim), lambda i, chip_ref: (layer, i, 0))
    else:
        lay = pl.BlockSpec((None, na, tr, cdim), lambda i, chip_ref: (layer, 0, i, 0))
    return pl.pallas_call(
        body, name=name, out_shape=tuple(SDS(w.shape, F32) for _ in range(4)),
        grid_spec=pltpu.PrefetchScalarGridSpec(
            num_scalar_prefetch=1, grid=(r // tr,),
            in_specs=[pl.BlockSpec((None, a_blk, tr, cdim), lambda i, chip_ref: (chip_ref[0], 0, i, 0)),
                      pl.BlockSpec((3, a_blk, tr, cdim), lambda i, chip_ref: (0, 0, i, 0)), lay, lay, lay]
            + [HBM] * n_prev,
            out_specs=(lay, lay, lay, lay)),
        input_output_aliases={6 + j: j for j in range(n_prev)},
        compiler_params=_params("parallel"),
    )(chip, part, from_chips, w, m, v, *(prev or ()))


def kernel(x, mem, norm_g, w_in, q_norm_g, k_norm_g, sg_ln_g, sg_ln_b, w_s, b_s, mem_norm_g, w_mem_kv, w_br, w_out, final_g, loss_target, m_norm_g, m_w_in, m_q_norm_g, m_k_norm_g, m_sg_ln_g, m_sg_ln_b, m_w_s, m_b_s, m_mem_norm_g, m_w_mem_kv, m_w_br, m_w_out, m_final_g, v_norm_g, v_w_in, v_q_norm_g, v_k_norm_g, v_sg_ln_g, v_sg_ln_b, v_w_s, v_b_s, v_mem_norm_g, v_w_mem_kv, v_w_br, v_w_out, v_final_g):
    wts = dict(norm_g=norm_g, w_in=w_in, q_norm_g=q_norm_g, k_norm_g=k_norm_g, sg_ln_g=sg_ln_g, sg_ln_b=sg_ln_b,
               w_s=w_s, b_s=b_s, mem_norm_g=mem_norm_g, w_mem_kv=w_mem_kv, w_br=w_br, w_out=w_out, final_g=final_g)
    mom1 = dict(norm_g=m_norm_g, w_in=m_w_in, q_norm_g=m_q_norm_g, k_norm_g=m_k_norm_g, sg_ln_g=m_sg_ln_g,
                sg_ln_b=m_sg_ln_b, w_s=m_w_s, b_s=m_b_s, mem_norm_g=m_mem_norm_g, w_mem_kv=m_w_mem_kv, w_br=m_w_br,
                w_out=m_w_out, final_g=m_final_g)
    mom2 = dict(norm_g=v_norm_g, w_in=v_w_in, q_norm_g=v_q_norm_g, k_norm_g=v_k_norm_g, sg_ln_g=v_sg_ln_g,
                sg_ln_b=v_sg_ln_b, w_s=v_w_s, b_s=v_b_s, mem_norm_g=v_mem_norm_g, w_mem_kv=v_w_mem_kv, w_br=v_w_br,
                w_out=v_w_out, final_g=v_final_g)
    dp = w_in.shape[0]
    core = lax.axis_index("c").astype(jnp.int32).reshape(1)
    chip = (2 * lax.axis_index("x") + lax.axis_index("y")).astype(jnp.int32).reshape(1)

    t_in = lambda a: jnp.swapaxes(a, 1, 2)
    wts, mom1, mom2 = [dict(d_, w_in=t_in(d_["w_in"])) for d_ in (wts, mom1, mom2)]

    shard_bf = {n: wts[n].astype(BF16) for n in BIG}
    shards = lambda l: [shard_bf[n][l] for n in BIG]
    x_l, mem_l = x[0], mem[0]
    tabs = _rope_tables(x_l.shape[0])

    gathered = _all_gather(shards(0), "weights_all_gather")
    layers, saved = [], []
    for l in range(dp):
        g_in, g_kv, g_br, g_out = gathered
        layers.append(_layer_weights(l, g_in.reshape(IN_WIDTH, -1), g_kv.reshape(D_MODEL, -1), _assemble_w_br(g_br),
                                     g_out.reshape(D_MODEL, -1), wts))
        x_l, s, gathered = _layer_fwd(x_l, mem_l, layers[l], tabs, next_shards=shards(l + 1) if l + 1 < dp else None)
        saved.append(s)
    loss_local, dx, d_final_g = _loss_head(x_l, final_g.reshape(1, -1), loss_target[0])
    loss = lax.psum(loss_local[0, 0], AXES)

    def finish(l, parts, from_chips, prev):
        return {n: _shard_sum_adamw(p, f, chip, wts[n], mom1[n], mom2[n], l, None if prev is None else prev[n],
                                    "sum_adamw_" + n)
                for n, p, f in zip(BIG, parts, from_chips)}

    grads, updated, pending = [None] * dp, None, None
    for l in reversed(range(dp)):
        dx, grads[l], scattered, own = _layer_bwd(dx, mem_l, layers[l], saved[l], tabs, pending=pending, core=core,
                                                  scatter_own=(l == 0))
        if scattered is not None:
            updated = finish(l + 1, *scattered, updated)
        pending = [grads[l][n] for n in BIG]
    updated = finish(0, *own, updated)
    grad_x = dx
    big_out = [{n: (t_in(updated[n][k]) if n == "w_in" else updated[n][k]) for n in BIG} for k in range(4)]

    small_g = {n: jnp.stack([g[n] for g in grads]) for n in SMALL if n != "final_g"}
    small_g["final_g"] = d_final_g
    (all_small,) = _all_gather([_pack([small_g[n] for n in SMALL])], "small_all_gather")
    small_bufs = _sum_and_adamw(
        all_small, _pack([wts[n] for n in SMALL]), _pack([mom1[n] for n in SMALL]), _pack([mom2[n] for n in SMALL]),
        "small_sum_adamw")

    outs = []
    for big_vals, small_buf in zip(big_out, small_bufs):
        vals = dict(big_vals)
        vals.update(zip(SMALL, _unpack(small_buf, [wts[n].shape for n in SMALL])))
        outs.append(vals)
    order = ("norm_g", "w_in", "q_norm_g", "k_norm_g", "sg_ln_g", "sg_ln_b", "w_s", "b_s", "mem_norm_g", "w_mem_kv",
             "w_br", "w_out", "final_g")
    result = [loss, grad_x[None]]
    for vals in outs:
        result += [vals[n] for n in order]
    return tuple(result)
```

```python
import functools

import jax
import jax.numpy as jnp
from jax import lax
from jax.experimental import pallas as pl
from jax.experimental.pallas import tpu as pltpu

F32 = jnp.float32
BF16 = jnp.bfloat16
SDS = jax.ShapeDtypeStruct
MESH_ID = pl.DeviceIdType.MESH
AXES = ("x", "y", "c")
N_DEV = 8

D_MODEL = 1024
DEPTH = 4
GRID_W = 64
CHUNK = 128
ROPE_THETA = 10000.0
EPS = 1e-6
HEAD_DIM = 64
A_HEADS = 8
A_KV_HEADS = 2
B_GROUPS = 4
M_HEADS = 4
M_HEAD_DIM = 128
BW = 512
W_A = 1280
W_B = 1536
W_M = 1024
W_G = 3072
IN_WIDTH = W_A + W_B + W_M + W_G
GROUP_OFFS = (0, W_A, W_A + W_B, W_A + W_B + W_M, IN_WIDTH)

ADAM_LR = 0.001
ADAM_B1 = 0.9
ADAM_B2 = 0.999
ADAM_EPS = 1e-08
ADAM_WD = 0.01
ADAM_STEP = 10

LANES = 128
KEY_CHUNK = 2048
VMEM_LIMIT = 52 * 1024 * 1024


def _tile(n, cap, unit=LANES):
    if n <= cap:
        return n
    t = (cap // unit) * unit
    while n % t:
        t -= unit
    return t


def _params(*sem):
    return pltpu.CompilerParams(dimension_semantics=sem, vmem_limit_bytes=VMEM_LIMIT)


def _sigmoid(z):
    return 0.5 * jnp.tanh(0.5 * z) + 0.5


def _silu_and_grad(z):
    s = _sigmoid(z)
    return z * s, s * (1.0 + z * (1.0 - s))


def _dot(a, b, dims):
    return lax.dot_general(a, b, (dims, ((), ())), preferred_element_type=F32)


NN = ((1,), (0,))
NT = ((1,), (1,))
TN = ((0,), (0,))
HBM = pl.BlockSpec(memory_space=pl.ANY)


class _Exchange:
    def __init__(self, ins, out_shapes, sems, make, aliases=None):
        self.ins, self.out_shapes, self.sems, self.make = list(ins), list(out_shapes), list(sems), make
        self.aliases = dict(aliases or {})

    def start(self, in_refs, out_refs, sems):
        for cp in self.make(in_refs, out_refs, sems):
            cp.start()

    def finish(self, in_refs, out_refs, sems):
        for cp in self.make(in_refs, out_refs, sems):
            cp.wait()


def _both(a, b):
    if a is None or b is None:
        return a if b is None else b
    n_in, n_out, n_sem = len(a.ins), len(a.out_shapes), len(a.sems)

    def make(in_refs, out_refs, sems):
        return (a.make(in_refs[:n_in], out_refs[:n_out], sems[:n_sem])
                + b.make(in_refs[n_in:], out_refs[n_out:], sems[n_sem:]))

    aliases = dict(a.aliases)
    aliases.update({n_in + i: n_out + o for i, o in b.aliases.items()})
    return _Exchange(a.ins + b.ins, a.out_shapes + b.out_shapes, a.sems + b.sems, make, aliases)


def _host_call(body, ins, in_specs, out_specs, out_shape, *, name, grid, semantics, scratch=(), exchange=None):
    ins, out_shape, scratch = list(ins), list(out_shape), list(scratch)
    if exchange is None:
        res = pl.pallas_call(
            body, name=name, grid=grid, in_specs=list(in_specs), out_specs=tuple(out_specs), out_shape=tuple(out_shape),
            scratch_shapes=scratch, compiler_params=_params(*semantics))(*ins)
        return tuple(res), ()
    n_in, n_out, n_scr = len(ins), len(out_shape), len(scratch)
    x_in, x_out = len(exchange.ins), len(exchange.out_shapes)

    def carrying(*refs):
        o0 = n_in + x_in
        s0 = o0 + n_out + x_out
        c_in, c_out, c_sems = refs[n_in:o0], refs[o0 + n_out:s0], refs[s0 + n_scr:]
        ids = [pl.program_id(a) for a in range(len(grid))]
        first = functools.reduce(jnp.logical_and, [i == 0 for i in ids])
        last = functools.reduce(jnp.logical_and, [i == g - 1 for i, g in zip(ids, grid)])

        @pl.when(first)
        def _():
            exchange.start(c_in, c_out, c_sems)

        body(*refs[:n_in], *refs[o0:o0 + n_out], *refs[s0:s0 + n_scr])

        @pl.when(last)
        def _():
            exchange.finish(c_in, c_out, c_sems)

    res = pl.pallas_call(
        carrying, name=name, grid=grid, in_specs=list(in_specs) + [HBM] * x_in,
        out_specs=tuple(out_specs) + (HBM,) * x_out, out_shape=tuple(out_shape) + tuple(exchange.out_shapes),
        scratch_shapes=scratch + exchange.sems,
        input_output_aliases={n_in + i: n_out + o for i, o in exchange.aliases.items()},
        compiler_params=_params(*(["arbitrary"] * len(grid))))(*ins, *exchange.ins)
    return tuple(res[:n_out]), tuple(res[n_out:])


def _exchange_call(exchange, name):
    x_in = len(exchange.ins)

    def body(*refs):
        x_out = len(exchange.out_shapes)
        c_in, c_out, c_sems = refs[:x_in], refs[x_in:x_in + x_out], refs[x_in + x_out:]
        exchange.start(c_in, c_out, c_sems)
        exchange.finish(c_in, c_out, c_sems)

    return pl.pallas_call(
        body, name=name, in_specs=[HBM] * x_in, out_specs=tuple([HBM] * len(exchange.out_shapes)),
        out_shape=tuple(exchange.out_shapes), scratch_shapes=exchange.sems, input_output_aliases=exchange.aliases,
    )(*exchange.ins)


def _mm(a, b, mode, name):
    (m, k), (n, k2) = a.shape, (b.shape[::-1] if mode == "nn" else b.shape)
    assert k == k2 and mode in ("nn", "nt"), (a.shape, b.shape, mode)
    dims = {"nn": NN, "nt": NT}[mode]

    def body(a_ref, b_ref, o_ref):
        o_ref[...] = _dot(a_ref[...].astype(BF16), b_ref[...].astype(BF16), dims)

    return pl.pallas_call(body, name=name, out_shape=SDS((m, n), F32),
                          compiler_params=pltpu.CompilerParams(vmem_limit_bytes=VMEM_LIMIT))(a, b)


def _mm_tn(a, b, name):
    (k, m), (k2, n) = a.shape, b.shape
    assert k == k2, (a.shape, b.shape)
    tk = _tile(k, 2048, 16)
    nk = k // tk

    def body(a_ref, b_ref, o_ref, acc):
        kk = pl.program_id(0)
        part = _dot(a_ref[...].astype(BF16), b_ref[...].astype(BF16), TN)

        @pl.when(kk == 0)
        def _():
            acc[...] = part

        @pl.when(kk > 0)
        def _():
            acc[...] += part

        @pl.when(kk == nk - 1)
        def _():
            o_ref[...] = acc[...].astype(BF16)

    return pl.pallas_call(
        body, name=name, grid=(nk,),
        in_specs=[pl.BlockSpec((tk, m), lambda kk: (kk, 0)), pl.BlockSpec((tk, n), lambda kk: (kk, 0))],
        out_specs=pl.BlockSpec((m, n), lambda kk: (0, 0)), out_shape=SDS((m, n), BF16),
        scratch_shapes=[pltpu.VMEM((m, n), F32)], compiler_params=_params("arbitrary"),
    )(a, b)


def _rmsnorm_fwd(x, g, name):
    t, d = x.shape
    tm = _tile(t, 512, 8)

    def body(x_ref, g_ref, h_ref):
        xf = x_ref[...]
        r = lax.rsqrt(jnp.mean(xf * xf, axis=-1, keepdims=True) + EPS)
        h_ref[...] = (xf * r * g_ref[...]).astype(BF16)

    return pl.pallas_call(
        body, name=name, grid=(t // tm,),
        in_specs=[pl.BlockSpec((tm, d), lambda i: (i, 0)), pl.BlockSpec((1, d), lambda i: (0, 0))],
        out_specs=pl.BlockSpec((tm, d), lambda i: (i, 0)),
        out_shape=SDS((t, d), BF16), compiler_params=_params("parallel"),
    )(x, g)


def _rmsnorm_bwd_math(xf, g, dh):
    r = lax.rsqrt(jnp.mean(xf * xf, axis=-1, keepdims=True) + EPS)
    xh = xf * r
    gd = dh * g
    dx = r * (gd - xh * jnp.mean(gd * xh, axis=-1, keepdims=True))
    return dx, jnp.sum(dh * xh, axis=0, keepdims=True)


def _rmsnorm_dg(x, g, dh, name):
    t, d = x.shape
    tm = _tile(t, 512, 8)

    def body(x_ref, g_ref, dh_ref, dg_ref):
        @pl.when(pl.program_id(0) == 0)
        def _():
            dg_ref[...] = jnp.zeros_like(dg_ref)

        dg_ref[...] += _rmsnorm_bwd_math(x_ref[...], g_ref[...], dh_ref[...])[1]

    row = pl.BlockSpec((tm, d), lambda i: (i, 0))
    vec = pl.BlockSpec((1, d), lambda i: (0, 0))
    return pl.pallas_call(
        body, name=name, grid=(t // tm,), in_specs=[row, vec, row], out_specs=vec, out_shape=SDS((1, d), F32),
        compiler_params=_params("arbitrary"),
    )(x, g, dh)


def _group_rows(p):
    return slice(GROUP_OFFS[p], GROUP_OFFS[p + 1])


def _d_w_in_t(dps, h):
    t, d = h.shape
    tm = 256
    assert all(off % tm == 0 for off in GROUP_OFFS)
    tiles = [a.shape[1] // tm for a in dps]
    starts = [sum(tiles[:p]) for p in range(len(dps))]

    def body(*refs):
        a_refs, h_ref, o_ref = refs[:len(dps)], refs[len(dps)], refs[len(dps) + 1]
        i = pl.program_id(0)
        for p, a_ref in enumerate(a_refs):
            @pl.when((i >= starts[p]) & (i < starts[p] + tiles[p]))
            def _():
                o_ref[...] = _dot(a_ref[...], h_ref[...], TN).astype(BF16)

    def tile_of(p):
        return lambda i: jnp.clip(i - starts[p], 0, tiles[p] - 1)

    return pl.pallas_call(
        body, name="d_w_in", grid=(sum(tiles),),
        in_specs=[pl.BlockSpec((t, tm), lambda i, p=p: (0, tile_of(p)(i))) for p in range(len(dps))]
        + [pl.BlockSpec((t, d), lambda i: (0, 0), pipeline_mode=pl.Buffered(1))],
        out_specs=pl.BlockSpec((tm, d), lambda i: (i, 0)),
        out_shape=SDS((sum(tiles) * tm, d), BF16), compiler_params=_params("arbitrary"),
    )(*dps, h)


def _norm_in_proj(x, g, w_t, cs, sn, qg2, kg2):
    t, d = x.shape
    tm = _tile(t, 512, LANES)
    widths = [GROUP_OFFS[p + 1] - GROUP_OFFS[p] for p in range(4)]
    scale = HEAD_DIM ** -0.5

    def body(x_ref, g_ref, w_ref, cs_ref, sn_ref, qg_ref, kg_ref, h_ref, pa_ref, pb_ref, pm_ref, pg_ref,
             q_ref, qT_ref, k_ref, kT_ref, vT_ref):
        xf = x_ref[...]
        r = lax.rsqrt(jnp.mean(xf * xf, axis=-1, keepdims=True) + EPS)
        hb = (xf * r * g_ref[...]).astype(BF16)
        h_ref[...] = hb
        groups = [_dot(hb, w_ref[_group_rows(p), :], NT).astype(BF16) for p in range(4)]
        for o_ref, val in zip((pa_ref, pb_ref, pm_ref, pg_ref), groups):
            o_ref[...] = val
        pa = groups[0]
        lane = lax.broadcasted_iota(jnp.int32, (tm, LANES), 1)
        c, s = cs_ref[...], sn_ref[...]

        def norm_rope(xg, g2):
            rr = lax.rsqrt(_head_sums(xg * xg, lane) * (1.0 / HEAD_DIM) + EPS)
            xn = xg * rr * g2
            return xn * c + _swap16(xn, lane) * s

        for gi in range(4):
            sl = slice(gi * LANES, (gi + 1) * LANES)
            qr = norm_rope(pa[:, sl].astype(F32), qg_ref[...]) * scale
            q_ref[:, sl] = qr.astype(BF16)
            qT_ref[sl, :] = qr.T.astype(BF16)
        kr = norm_rope(pa[:, 512:640].astype(F32), kg_ref[...])
        kT_ref[...] = kr.T.astype(BF16)
        vT_ref[...] = pa[:, 640:768].astype(F32).T.astype(BF16)
        kr = kr.astype(BF16)
        for kvh in range(A_KV_HEADS):
            k_ref[kvh] = kr[:, kvh * HEAD_DIM:(kvh + 1) * HEAD_DIM]

    row = lambda wd: pl.BlockSpec((tm, wd), lambda i: (i, 0))
    col = lambda rws: pl.BlockSpec((rws, tm), lambda i: (0, i))
    vec = pl.BlockSpec((1, LANES), lambda i: (0, 0))
    hm = pl.BlockSpec((A_KV_HEADS, tm, HEAD_DIM), lambda i: (0, i, 0))
    return pl.pallas_call(
        body, name="norm_in_proj", grid=(t // tm,),
        in_specs=[row(d), pl.BlockSpec((1, d), lambda i: (0, 0)),
                  pl.BlockSpec(w_t.shape, lambda i: (0, 0), pipeline_mode=pl.Buffered(1)), row(LANES), row(LANES), vec, vec],
        out_specs=(row(d),) + tuple(row(wd) for wd in widths) + (row(BW), col(BW), hm, col(LANES), col(LANES)),
        out_shape=(SDS((t, d), BF16),) + tuple(SDS((t, wd), BF16) for wd in widths)
        + (SDS((t, BW), BF16), SDS((BW, t), BF16), SDS((A_KV_HEADS, t, HEAD_DIM), BF16),
           SDS((LANES, t), BF16), SDS((LANES, t), BF16)),
        compiler_params=_params("parallel"),
    )(x, g, w_t, cs, sn, qg2, kg2)


def _dh_rmsnorm_bwd(dps, w_t, x, g, dres, exchange=None):
    t, d = x.shape
    tm = _tile(t, 256, 16)
    n = len(dps)

    def body(*refs):
        dp_refs, w_ref = refs[:n], refs[n]
        x_ref, g_ref, dres_ref, dx_ref, dg_ref = refs[n + 1:]
        dh = None
        for p, dp_ref in enumerate(dp_refs):
            part = _dot(dp_ref[...], w_ref[_group_rows(p), :], NN)
            dh = part if dh is None else dh + part
        dx, dg = _rmsnorm_bwd_math(x_ref[...], g_ref[...], dh)
        dx_ref[...] = dx + dres_ref[...]

        @pl.when(pl.program_id(0) == 0)
        def _():
            dg_ref[...] = jnp.zeros_like(dg_ref)

        dg_ref[...] += dg

    row = pl.BlockSpec((tm, d), lambda i: (i, 0))
    vec = pl.BlockSpec((1, d), lambda i: (0, 0))
    return _host_call(
        body, list(dps) + [w_t, x, g, dres],
        [pl.BlockSpec((tm, a.shape[1]), lambda i: (i, 0)) for a in dps]
        + [pl.BlockSpec(w_t.shape, lambda i: (0, 0), pipeline_mode=pl.Buffered(1)), row, vec, row],
        (row, vec), (SDS((t, d), F32), SDS((1, d), F32)),
        name="dh_rmsnorm_bwd", grid=(t // tm,), semantics=("arbitrary",), exchange=exchange)


def _rope_tables(t):
    rows = t // GRID_W
    row = jnp.repeat(jnp.arange(rows, dtype=F32), GRID_W)
    col = jnp.tile(jnp.arange(GRID_W, dtype=F32), rows)
    n_freq = HEAD_DIM // 4
    inv = ROPE_THETA ** (-jnp.arange(n_freq, dtype=F32) / n_freq)
    ang = jnp.stack([row[:, None] * inv, col[:, None] * inv], axis=1)
    cos, sin = jnp.cos(ang), jnp.sin(ang)
    c64 = jnp.concatenate([cos[:, 0], cos[:, 0], cos[:, 1], cos[:, 1]], axis=-1)
    s64 = jnp.concatenate([-sin[:, 0], sin[:, 0], -sin[:, 1], sin[:, 1]], axis=-1)
    return jnp.tile(c64, (1, 2)), jnp.tile(s64, (1, 2))


def _head_sums(v, lane):
    lo = jnp.sum(jnp.where(lane < HEAD_DIM, v, 0.0), axis=-1, keepdims=True)
    hi = jnp.sum(jnp.where(lane < HEAD_DIM, 0.0, v), axis=-1, keepdims=True)
    return jnp.where(lane < HEAD_DIM, lo, hi)


def _swap16(v, lane):
    return jnp.where((lane % 32) < 16, pltpu.roll(v, LANES - 16, 1), pltpu.roll(v, 16, 1))


def _attn_fwd(qT, k, vT, pa, exchange=None):
    t = qT.shape[1]
    tq = _tile(t, 256, LANES)
    grp = A_HEADS // A_KV_HEADS

    ck = _tile(t, KEY_CHUNK, LANES)

    def body(qT_ref, k_ref, vT_ref, pa_ref, o_ref, lse_ref, y_ref):
        def finish_pair(pair, o_pair):
            o2 = jnp.concatenate(o_pair, axis=0).T
            cols = slice(pair * LANES, (pair + 1) * LANES)
            o_ref[:, cols] = o2
            z = pa_ref[:, 768 + pair * LANES:768 + (pair + 1) * LANES].astype(F32)
            y_ref[:, cols] = (o2 * (z * _sigmoid(z))).astype(BF16)

        n_c = t // ck
        items = [(h, c) for h in range(A_HEADS) for c in range(n_c)]

        def scores(h, c):
            return _dot(k_ref[h // grp, c * ck:(c + 1) * ck, :], qT_ref[h * HEAD_DIM:(h + 1) * HEAD_DIM, :], NN)

        def weighted_values(h, c, pT):
            kvh = h // grp
            return _dot(vT_ref[kvh * HEAD_DIM:(kvh + 1) * HEAD_DIM, c * ck:(c + 1) * ck], pT, NN)

        o_pair, sT, pending, acc = [], scores(*items[0]), None, None
        for i in range(len(items) + 1):
            item = items[i] if i < len(items) else None
            sT_next = scores(*items[i + 1]) if i + 1 < len(items) else None
            if pending is not None:
                (ph, pc, pT, alpha, l_run) = pending
                o_c = weighted_values(ph, pc, pT)
                acc = o_c if alpha is None else acc * alpha + o_c
                if pc == n_c - 1:
                    o_pair.append(acc / l_run)
                    if ph % 2:
                        finish_pair(ph // 2, o_pair)
                        o_pair = []
                pending = None
            if item is not None:
                h, c = item
                m_c = jnp.max(sT, axis=0, keepdims=True)
                if c == 0:
                    m_new, alpha = m_c, None
                else:
                    m_new = jnp.maximum(m, m_c)
                    alpha = jnp.exp(m - m_new)
                pT = jnp.exp(sT - m_new)
                l_c = jnp.sum(pT, axis=0, keepdims=True)
                l = l_c if c == 0 else l * alpha + l_c
                m = m_new
                if c == n_c - 1:
                    lse_ref[h:h + 1, :] = m + jnp.log(l)
                pending = (h, c, pT.astype(BF16), alpha, l)
            sT = sT_next

    row = lambda w: pl.BlockSpec((tq, w), lambda i: (i, 0))
    return _host_call(
        body, (qT, k, vT, pa),
        [pl.BlockSpec((BW, tq), lambda i: (0, i)), pl.BlockSpec((A_KV_HEADS, t, HEAD_DIM), lambda i: (0, 0, 0)),
         pl.BlockSpec((A_KV_HEADS * HEAD_DIM, t), lambda i: (0, 0)), row(W_A)],
        (row(BW), pl.BlockSpec((A_HEADS, tq), lambda i: (0, i)), row(BW)),
        (SDS((t, BW), F32), SDS((A_HEADS, t), F32), SDS((t, BW), BF16)),
        name="attn_fwd", grid=(t // tq,), semantics=("parallel",), exchange=exchange)


def _attn_bwd(q, qT, k, kT, vT, pa, o, lse, dy, exchange=None):
    t = q.shape[0]
    tq = _tile(t, 256, LANES)
    grp = A_HEADS // A_KV_HEADS
    gw = grp * HEAD_DIM

    def body(q_ref, qT_ref, k_ref, kT_ref, vT_ref, z_ref, o_ref, lse_ref, dy_ref, dq_ref, dkT_ref, dvT_ref):
        @pl.when(pl.program_id(1) == 0)
        def _():
            dkT_ref[...] = jnp.zeros_like(dkT_ref)
            dvT_ref[...] = jnp.zeros_like(dvT_ref)

        z = z_ref[...].astype(F32)
        do = dy_ref[...] * (z * _sigmoid(z))
        doo = do * o_ref[...]
        doT = do.T
        kk, kT, vT = k_ref[...], kT_ref[...], vT_ref[...]
        heads = [slice(j * HEAD_DIM, (j + 1) * HEAD_DIM) for j in range(grp)]

        def scores(hs):
            return _dot(q_ref[:, hs], kT, NN), _dot(do[:, hs].astype(BF16), vT, NN)

        s, dp = scores(heads[0])
        for j, hs in enumerate(heads):
            nxt = scores(heads[j + 1]) if j + 1 < grp else None
            delta = jnp.sum(doo[:, hs], axis=-1, keepdims=True)
            p = jnp.exp(s - lse_ref[:, j:j + 1])
            ds_b = (p * (dp - delta)).astype(BF16)
            dq_ref[:, hs] = _dot(ds_b, kk, NN)
            dkT_ref[...] += _dot(qT_ref[hs, :], ds_b, NN)
            dvT_ref[...] += _dot(doT[hs, :].astype(BF16), p.astype(BF16), NN)
            if nxt is not None:
                s, dp = nxt

    grp_blk = pl.BlockSpec((tq, gw), lambda g, i: (i, g))
    kvT_blk = pl.BlockSpec((HEAD_DIM, t), lambda g, i: (g, 0))
    acc_blk = pl.BlockSpec((None, HEAD_DIM, t), lambda g, i: (g, 0, 0))
    return _host_call(
        body, (q, qT, k, kT, vT, pa, o, lse, dy),
        [grp_blk, pl.BlockSpec((gw, tq), lambda g, i: (g, i)), pl.BlockSpec((None, t, HEAD_DIM), lambda g, i: (g, 0, 0)),
         kvT_blk, kvT_blk, pl.BlockSpec((tq, gw), lambda g, i: (i, 768 // gw + g)), grp_blk,
         pl.BlockSpec((None, tq, grp), lambda g, i: (g, i, 0)), grp_blk],
        (grp_blk, acc_blk, acc_blk),
        (SDS((t, BW), F32), SDS((A_KV_HEADS, HEAD_DIM, t), F32), SDS((A_KV_HEADS, HEAD_DIM, t), F32)),
        name="attn_bwd", grid=(A_KV_HEADS, t // tq), semantics=("arbitrary", "arbitrary"), exchange=exchange)


def _attn_prep_bwd(pa, cs, sn, qg2, kg2, dq, dkT, dvT, dy, o, exchange=None):
    t = pa.shape[0]
    tq = _tile(t, 512, LANES)
    scale = HEAD_DIM ** -0.5

    def body(pa_ref, cs_ref, sn_ref, qg_ref, kg_ref, dq_ref, dkT_ref, dvT_ref, dy_ref, o_ref, dpa_ref, dqg_ref, dkg_ref):
        lane = lax.broadcasted_iota(jnp.int32, (tq, LANES), 1)
        c, s = cs_ref[...], sn_ref[...]

        @pl.when(pl.program_id(0) == 0)
        def _():
            dqg_ref[...] = jnp.zeros_like(dqg_ref)
            dkg_ref[...] = jnp.zeros_like(dkg_ref)

        def norm_rope_bwd(xg, g2, dout):
            r = lax.rsqrt(_head_sums(xg * xg, lane) * (1.0 / HEAD_DIM) + EPS)
            xh = xg * r
            dxn = dout * c + _swap16(dout * s, lane)
            gd = dxn * g2
            dx = r * (gd - xh * (_head_sums(gd * xh, lane) * (1.0 / HEAD_DIM)))
            return dx, jnp.sum(dxn * xh, axis=0, keepdims=True)

        for gi in range(4):
            sl = slice(gi * LANES, (gi + 1) * LANES)
            dx, dg = norm_rope_bwd(pa_ref[:, sl].astype(F32), qg_ref[...], dq_ref[:, sl] * scale)
            dpa_ref[:, sl] = dx.astype(BF16)
            dqg_ref[...] += dg
        dx, dg = norm_rope_bwd(pa_ref[:, 512:640].astype(F32), kg_ref[...], dkT_ref[...].T)
        dpa_ref[:, 512:640] = dx.astype(BF16)
        dkg_ref[...] += dg
        dpa_ref[:, 640:768] = dvT_ref[...].T.astype(BF16)
        z = pa_ref[:, 768:1280].astype(F32)
        _, dsilu = _silu_and_grad(z)
        dpa_ref[:, 768:1280] = (dy_ref[...] * o_ref[...] * dsilu).astype(BF16)

    row = lambda w: pl.BlockSpec((tq, w), lambda i: (i, 0))
    col = pl.BlockSpec((LANES, tq), lambda i: (0, i))
    vec = pl.BlockSpec((1, LANES), lambda i: (0, 0))
    return _host_call(
        body, (pa, cs, sn, qg2, kg2, dq, dkT, dvT, dy, o),
        [row(W_A), row(LANES), row(LANES), vec, vec, row(BW), col, col, row(BW), row(BW)],
        (row(W_A), vec, vec), (SDS((t, W_A), BF16), SDS((1, LANES), F32), SDS((1, LANES), F32)),
        name="attn_prep_bwd", grid=(t // tq,), semantics=("arbitrary",), exchange=exchange)


def _layer_norm(v, g, b):
    mu = jnp.mean(v, axis=-1, keepdims=True)
    xc = v - mu
    rs = lax.rsqrt(jnp.mean(xc * xc, axis=-1, keepdims=True) + EPS)
    xh = xc * rs
    return xh * g + b, xh, rs


def _gmlp_fwd(pb, lng, lnb, ws, bs):
    t = pb.shape[0]
    tb = _tile(t, 512, CHUNK)

    def body(pb_ref, g_ref, b_ref, ws_ref, bs_ref, y_ref):
        vln, _, _ = _layer_norm(pb_ref[:, BW:2 * BW].astype(F32), g_ref[...], b_ref[...])
        vb = vln.astype(BF16)
        for gi in range(B_GROUPS):
            w = ws_ref[gi].astype(BF16)
            cs_ = slice(gi * CHUNK, (gi + 1) * CHUNK)
            for n in range(tb // CHUNK):
                rs_ = slice(n * CHUNK, (n + 1) * CHUNK)
                mixed = _dot(w, vb[rs_, cs_], NN) + bs_ref[gi]
                z = pb_ref[rs_, 2 * BW + gi * CHUNK:2 * BW + (gi + 1) * CHUNK].astype(F32)
                y_ref[rs_, cs_] = (pb_ref[rs_, cs_].astype(F32) * mixed * (z * _sigmoid(z))).astype(BF16)

    return pl.pallas_call(
        body, name="gmlp_fwd", grid=(t // tb,),
        in_specs=[pl.BlockSpec((tb, W_B), lambda i: (i, 0)),
                  pl.BlockSpec((1, BW), lambda i: (0, 0)), pl.BlockSpec((1, BW), lambda i: (0, 0)),
                  pl.BlockSpec((B_GROUPS, CHUNK, CHUNK), lambda i: (0, 0, 0)),
                  pl.BlockSpec((B_GROUPS, CHUNK, 1), lambda i: (0, 0, 0))],
        out_specs=pl.BlockSpec((tb, BW), lambda i: (i, 0)),
        out_shape=SDS((t, BW), BF16), compiler_params=_params("parallel"),
    )(pb, lng, lnb, ws, bs)


def _gmlp_bwd(pb, lng, lnb, ws, bs, dy):
    t = pb.shape[0]
    tb = _tile(t, 256, CHUNK)

    def body(pb_ref, g_ref, b_ref, ws_ref, bs_ref, dy_ref, dpb_ref, dws_ref, dbs_ref, dg_ref, db_ref, dvln_ref):
        @pl.when(pl.program_id(0) == 0)
        def _():
            dws_ref[...] = jnp.zeros_like(dws_ref)
            dbs_ref[...] = jnp.zeros_like(dbs_ref)
            dg_ref[...] = jnp.zeros_like(dg_ref)
            db_ref[...] = jnp.zeros_like(db_ref)

        vln, xh, rs = _layer_norm(pb_ref[:, BW:2 * BW].astype(F32), g_ref[...], b_ref[...])
        vb = vln.astype(BF16)
        for gi in range(B_GROUPS):
            w = ws_ref[gi].astype(BF16)
            cs_ = slice(gi * CHUNK, (gi + 1) * CHUNK)
            for n in range(tb // CHUNK):
                rs_ = slice(n * CHUNK, (n + 1) * CHUNK)
                vbc = vb[rs_, cs_]
                mixed = _dot(w, vbc, NN) + bs_ref[gi]
                zs = slice(2 * BW + gi * CHUNK, 2 * BW + (gi + 1) * CHUNK)
                z = pb_ref[rs_, zs].astype(F32)
                u = pb_ref[rs_, cs_].astype(F32)
                sil, dsil = _silu_and_grad(z)
                dyc = dy_ref[rs_, cs_]
                dmixed = dyc * u * sil
                dpb_ref[rs_, cs_] = (dyc * mixed * sil).astype(BF16)
                dpb_ref[rs_, zs] = (dyc * u * mixed * dsil).astype(BF16)
                dmb = dmixed.astype(BF16)
                dws_ref[gi] += _dot(dmb, vbc, NT)
                dbs_ref[gi] += jnp.sum(dmixed, axis=-1, keepdims=True)
                dvln_ref[rs_, cs_] = _dot(w, dmb, TN)
        dvln = dvln_ref[...]
        dg_ref[...] += jnp.sum(dvln * xh, axis=0, keepdims=True)
        db_ref[...] += jnp.sum(dvln, axis=0, keepdims=True)
        gd = dvln * g_ref[...]
        dv = rs * (gd - jnp.mean(gd, axis=-1, keepdims=True) - xh * jnp.mean(gd * xh, axis=-1, keepdims=True))
        dpb_ref[:, BW:2 * BW] = dv.astype(BF16)

    vec = pl.BlockSpec((1, BW), lambda i: (0, 0))
    wsb = pl.BlockSpec((B_GROUPS, CHUNK, CHUNK), lambda i: (0, 0, 0))
    bsb = pl.BlockSpec((B_GROUPS, CHUNK, 1), lambda i: (0, 0, 0))
    return pl.pallas_call(
        body, name="gmlp_bwd", grid=(t // tb,),
        in_specs=[pl.BlockSpec((tb, W_B), lambda i: (i, 0)), vec, vec, wsb, bsb, pl.BlockSpec((tb, BW), lambda i: (i, 0))],
        out_specs=(pl.BlockSpec((tb, W_B), lambda i: (i, 0)), wsb, bsb, vec, vec),
        out_shape=(SDS((t, W_B), BF16), SDS((B_GROUPS, CHUNK, CHUNK), F32), SDS((B_GROUPS, CHUNK, 1), F32),
                   SDS((1, BW), F32), SDS((1, BW), F32)),
        scratch_shapes=[pltpu.VMEM((tb, BW), F32)],
        compiler_params=_params("arbitrary"),
    )(pb, lng, lnb, ws, bs, dy)


def _mem_scores(pm_ref, kv_ref, h):
    hs = slice(h * M_HEAD_DIM, (h + 1) * M_HEAD_DIM)
    return _dot(pm_ref[:, hs].astype(BF16), kv_ref[:, hs].astype(BF16), NT) * (M_HEAD_DIM ** -0.5)


def _softmax_rows(s):
    e = jnp.exp(s - jnp.max(s, axis=-1, keepdims=True))
    return e / jnp.sum(e, axis=-1, keepdims=True)


def _mem_attn_fwd(pm, kv):
    t = pm.shape[0]
    tq = _tile(t, 512, 8)
    ml = kv.shape[0]

    def body(pm_ref, kv_ref, y_ref):
        s_next = _mem_scores(pm_ref, kv_ref, 0)
        for h in range(M_HEADS):
            hs = slice(h * M_HEAD_DIM, (h + 1) * M_HEAD_DIM)
            vh = kv_ref[:, BW + h * M_HEAD_DIM:BW + (h + 1) * M_HEAD_DIM].astype(BF16)
            s, s_next = s_next, (_mem_scores(pm_ref, kv_ref, h + 1) if h + 1 < M_HEADS else None)
            p = _softmax_rows(s)
            o = _dot(p.astype(BF16), vh, NN)
            z = pm_ref[:, BW + h * M_HEAD_DIM:BW + (h + 1) * M_HEAD_DIM].astype(F32)
            y_ref[:, hs] = (o * (z * _sigmoid(z))).astype(BF16)

    return pl.pallas_call(
        body, name="mem_attn_fwd", grid=(t // tq,),
        in_specs=[pl.BlockSpec((tq, W_M), lambda i: (i, 0)), pl.BlockSpec((ml, 2 * BW), lambda i: (0, 0))],
        out_specs=pl.BlockSpec((tq, BW), lambda i: (i, 0)),
        out_shape=SDS((t, BW), BF16), compiler_params=_params("parallel"),
    )(pm, kv)


def _mem_attn_bwd(pm, kv, dy):
    t = pm.shape[0]
    tq = _tile(t, 512, 8)
    ml = kv.shape[0]
    scale = M_HEAD_DIM ** -0.5

    def body(pm_ref, kv_ref, dy_ref, dpm_ref, dkv_ref):
        @pl.when(pl.program_id(0) == 0)
        def _():
            dkv_ref[...] = jnp.zeros_like(dkv_ref)

        s_next = _mem_scores(pm_ref, kv_ref, 0)
        for h in range(M_HEADS):
            hs = slice(h * M_HEAD_DIM, (h + 1) * M_HEAD_DIM)
            zs = slice(BW + h * M_HEAD_DIM, BW + (h + 1) * M_HEAD_DIM)
            kh = kv_ref[:, hs].astype(BF16)
            vh = kv_ref[:, zs].astype(BF16)
            qh = pm_ref[:, hs].astype(BF16)
            s, s_next = s_next, (_mem_scores(pm_ref, kv_ref, h + 1) if h + 1 < M_HEADS else None)
            p = _softmax_rows(s)
            pb = p.astype(BF16)
            o = _dot(pb, vh, NN)
            sil, dsil = _silu_and_grad(pm_ref[:, zs].astype(F32))
            dyh = dy_ref[:, hs]
            do = dyh * sil
            dpm_ref[:, zs] = (dyh * o * dsil).astype(BF16)
            delta = jnp.sum(do * o, axis=-1, keepdims=True)
            do_b = do.astype(BF16)
            dp = _dot(do_b, vh, NT)
            dr_b = (p * (dp - delta) * scale).astype(BF16)
            dpm_ref[:, hs] = _dot(dr_b, kh, NN).astype(BF16)
            dkv_ref[:, hs] += _dot(dr_b, qh, TN)
            dkv_ref[:, zs] += _dot(pb, do_b, TN)

    kvb = pl.BlockSpec((ml, 2 * BW), lambda i: (0, 0))
    return pl.pallas_call(
        body, name="mem_attn_bwd", grid=(t // tq,),
        in_specs=[pl.BlockSpec((tq, W_M), lambda i: (i, 0)), kvb, pl.BlockSpec((tq, BW), lambda i: (i, 0))],
        out_specs=(pl.BlockSpec((tq, W_M), lambda i: (i, 0)), kvb),
        out_shape=(SDS((t, W_M), BF16), SDS((ml, 2 * BW), F32)),
        compiler_params=_params("arbitrary"),
    )(pm, kv, dy)


def _merge_fwd(ya, yb, ym, pg, wbr, x, w_out, exchange=None):
    t = ya.shape[0]
    tm = _tile(t, 512, 8)

    def body(ya_ref, yb_ref, ym_ref, pg_ref, w_ref, x_ref, wo_ref, m_ref, xn_ref):
        ups = [_dot(y_ref[...], w_ref[n], NN) for n, y_ref in enumerate((ya_ref, yb_ref, ym_ref))]
        acc = None
        for n, up in enumerate(ups):
            term = _sigmoid(pg_ref[:, n * D_MODEL:(n + 1) * D_MODEL].astype(F32)) * up
            acc = term if acc is None else acc + term
        merged = acc.astype(BF16)
        m_ref[...] = merged
        xn_ref[...] = x_ref[...] + _dot(merged, wo_ref[...], NN)

    yb_spec = pl.BlockSpec((tm, BW), lambda i: (i, 0))
    row = pl.BlockSpec((tm, D_MODEL), lambda i: (i, 0))
    return _host_call(
        body, (ya, yb, ym, pg, wbr, x, w_out),
        [yb_spec, yb_spec, yb_spec, pl.BlockSpec((tm, W_G), lambda i: (i, 0)),
         pl.BlockSpec((3, BW, D_MODEL), lambda i: (0, 0, 0)), row, pl.BlockSpec(w_out.shape, lambda i: (0, 0))],
        (row, row), (SDS((t, D_MODEL), BF16), SDS((t, D_MODEL), F32)),
        name="merge_fwd", grid=(t // tm,), semantics=("parallel",), exchange=exchange)


def _merge_bwd(ya, yb, ym, pg, wbr, dx_out, w_out, exchange=None):
    t = ya.shape[0]
    tm = _tile(t, 512, 16)

    def body(ya_ref, yb_ref, ym_ref, pg_ref, w_ref, dx_ref, wo_ref, dya_ref, dyb_ref, dym_ref, dpg_ref, dw_ref):
        @pl.when(pl.program_id(0) == 0)
        def _():
            dw_ref[...] = jnp.zeros_like(dw_ref)

        dmf = _dot(dx_ref[...].astype(BF16), wo_ref[...], NT)
        branches = ((ya_ref, dya_ref), (yb_ref, dyb_ref), (ym_ref, dym_ref))

        def gate(n):
            gt = _sigmoid(pg_ref[:, n * D_MODEL:(n + 1) * D_MODEL].astype(F32))
            return gt, (dmf * gt).astype(BF16)

        gt, dup = gate(0)
        for n, (y_ref, dy_ref) in enumerate(branches):
            nxt = gate(n + 1) if n + 1 < len(branches) else None
            y, w = y_ref[...], w_ref[n]
            dy_ref[...] = _dot(dup, w, NT)
            dw_ref[n] += _dot(y, dup, TN)
            up = _dot(y, w, NN)
            dpg_ref[:, n * D_MODEL:(n + 1) * D_MODEL] = (dmf * up * gt * (1.0 - gt)).astype(BF16)
            if nxt is not None:
                gt, dup = nxt

    y_spec = pl.BlockSpec((tm, BW), lambda i: (i, 0))
    w_spec = pl.BlockSpec((3, BW, D_MODEL), lambda i: (0, 0, 0))
    return _host_call(
        body, (ya, yb, ym, pg, wbr, dx_out, w_out),
        [y_spec, y_spec, y_spec, pl.BlockSpec((tm, W_G), lambda i: (i, 0)), w_spec,
         pl.BlockSpec((tm, D_MODEL), lambda i: (i, 0)), pl.BlockSpec(w_out.shape, lambda i: (0, 0))],
        (y_spec, y_spec, y_spec, pl.BlockSpec((tm, W_G), lambda i: (i, 0)), w_spec),
        (SDS((t, BW), F32), SDS((t, BW), F32), SDS((t, BW), F32), SDS((t, W_G), BF16), SDS((3, BW, D_MODEL), F32)),
        name="merge_bwd", grid=(t // tm,), semantics=("arbitrary",), exchange=exchange)


def _loss_head(x, g, target):
    t, d = x.shape
    tm = _tile(t, 512, 8)

    def body(x_ref, g_ref, t_ref, loss_ref, dx_ref, dg_ref):
        @pl.when(pl.program_id(0) == 0)
        def _():
            loss_ref[...] = jnp.zeros_like(loss_ref)
            dg_ref[...] = jnp.zeros_like(dg_ref)

        xf = x_ref[...]
        r = lax.rsqrt(jnp.mean(xf * xf, axis=-1, keepdims=True) + EPS)
        xh = xf * r
        err = xh * g_ref[...] - t_ref[...]
        per_tok = jnp.mean(err * err, axis=-1, keepdims=True)
        loss_ref[...] += 0.5 * jnp.sum(per_tok, axis=0, keepdims=True)
        dy = err * (1.0 / d)
        gd = dy * g_ref[...]
        dx_ref[...] = r * (gd - xh * jnp.mean(gd * xh, axis=-1, keepdims=True))
        dg_ref[...] += jnp.sum(dy * xh, axis=0, keepdims=True)

    row = pl.BlockSpec((tm, d), lambda i: (i, 0))
    vec = pl.BlockSpec((1, d), lambda i: (0, 0))
    return pl.pallas_call(
        body, name="loss_head", grid=(t // tm,),
        in_specs=[row, vec, row],
        out_specs=(pl.BlockSpec((1, 1), lambda i: (0, 0)), row, vec),
        out_shape=(SDS((1, 1), F32), SDS((t, d), F32), SDS((1, d), F32)),
        compiler_params=_params("arbitrary"),
    )(x, g, target)


def _layer_fwd(x, mem, w, tabs, next_shards=None):
    cs, sn = tabs
    riding = next_shards is not None
    h, pa, pb, pm, pg, q, qT, k, kT, vT = _norm_in_proj(x, w["norm_g"], w["w_in_t"], cs, sn, w["qg2"], w["kg2"])
    (o, lse, ya), gathered = _attn_fwd(qT, k, vT, pa, exchange=_gather_first_hop(next_shards) if riding else None)
    yb = _gmlp_fwd(pb, w["ln_g"], w["ln_b"], w["w_s"], w["b_s"])
    memn = _rmsnorm_fwd(mem, w["mem_g"], "mem_rmsnorm_fwd")
    kv = _mm(memn, w["w_kv"], "nn", "mem_kv")
    ym = _mem_attn_fwd(pm, kv)
    (merged, x_next), gathered = _merge_fwd(ya, yb, ym, pg, w["w_br"], x, w["w_out"],
                                            exchange=_gather_forward(gathered) if riding else None)
    saved = dict(x=x, h=h, pa=pa, pb=pb, pm=pm, pg=pg, q=q, qT=qT, k=k, kT=kT, vT=vT, o=o, lse=lse, ya=ya, yb=yb, ym=ym,
                 memn=memn, kv=kv, merged=merged)
    return x_next, saved, gathered


def _layer_bwd(dx_out, mem, w, s, tabs, pending=None, core=None, scatter_own=False):
    cs, sn = tabs
    t = dx_out.shape[0]
    riding = pending is not None
    by_owner = lambda a: a.reshape((N_DEV // 2, 2, -1) + a.shape[-2:])
    row_shards = lambda a: by_owner(a.reshape(N_DEV, a.shape[0] // N_DEV, a.shape[1]))
    d_w_out = _mm_tn(s["merged"], dx_out, "d_w_out")
    (dya, dyb, dym, dpg, d_w_br), recv = _merge_bwd(s["ya"], s["yb"], s["ym"], s["pg"], w["w_br"], dx_out, w["w_out"],
                                                    exchange=_scatter_to_sibling(pending) if riding else None)
    parts = _pair_sums(BIG, pending, recv, core) if riding else None
    grads = dict(w_br=by_owner(_split_w_br(d_w_br)), w_out=row_shards(d_w_out))
    early = [grads[n] for n in BIG[2:]]
    grp = A_HEADS // A_KV_HEADS
    lse_cols = s["lse"].reshape(A_KV_HEADS, grp, t).transpose(0, 2, 1)
    (dq, dkT, dvT), landed = _attn_bwd(
        s["q"], s["qT"], s["k"], s["kT"], s["vT"], s["pa"], s["o"], lse_cols, dya,
        exchange=_both(_scatter_to_chips(parts) if riding else None, _scatter_to_sibling(early) if scatter_own else None))
    from_chips, early_recv = (landed[:len(BIG)], landed[len(BIG):]) if riding else (None, landed)
    early_parts = _pair_sums(BIG[2:], early, early_recv, core) if scatter_own else None
    (dpa, d_qg2, d_kg2), early_from_chips = _attn_prep_bwd(
        s["pa"], cs, sn, w["qg2"], w["kg2"], dq, dkT.reshape(-1, t), dvT.reshape(-1, t), dya, s["o"],
        exchange=_scatter_to_chips(early_parts) if scatter_own else None)
    dpb, d_w_s, d_b_s, d_ln_g, d_ln_b = _gmlp_bwd(s["pb"], w["ln_g"], w["ln_b"], w["w_s"], w["b_s"], dyb)
    dpm, dkv = _mem_attn_bwd(s["pm"], s["kv"], dym)
    d_w_kv = _mm_tn(s["memn"], dkv, "d_w_kv")
    dmemn = _mm(dkv, w["w_kv"], "nt", "d_memn")
    d_mem_g = _rmsnorm_dg(mem, w["mem_g"], dmemn, "mem_rmsnorm_bwd")
    dps = (dpa, dpb, dpm, dpg)
    grads.update(w_in=row_shards(_d_w_in_t(dps, s["h"])), w_mem_kv=row_shards(d_w_kv))
    own = None
    if scatter_own:
        late = [grads[n] for n in BIG[:2]]
        late_parts = _pair_sums(BIG[:2], late, _exchange_call(_scatter_to_sibling(late), "rs_sibling_swap"), core)
        own = _scatter_to_chips(late_parts)
    (dx_in, d_norm_g), late_from_chips = _dh_rmsnorm_bwd(dps, w["w_in_t"], s["x"], w["norm_g"], dx_out, exchange=own)
    if scatter_own:
        own_parts, own_from_chips = late_parts + early_parts, tuple(late_from_chips) + tuple(early_from_chips)
    grads.update(norm_g=d_norm_g[0], q_norm_g=d_qg2[0, :HEAD_DIM] + d_qg2[0, HEAD_DIM:],
                 k_norm_g=d_kg2[0, :HEAD_DIM] + d_kg2[0, HEAD_DIM:], sg_ln_g=d_ln_g[0], sg_ln_b=d_ln_b[0],
                 w_s=d_w_s, b_s=d_b_s[:, :, 0], mem_norm_g=d_mem_g[0])
    return (dx_in, grads, ((parts, from_chips) if riding else None),
            ((own_parts, own_from_chips) if scatter_own else None))


def _layer_weights(l, w_in_t, w_kv, w_br, w_out, small):
    tile2 = lambda g: jnp.tile(g.reshape(1, -1), (1, 2))
    return dict(
        w_in_t=w_in_t, w_kv=w_kv, w_br=w_br, w_out=w_out,
        norm_g=small["norm_g"][l].reshape(1, -1), qg2=tile2(small["q_norm_g"][l]), kg2=tile2(small["k_norm_g"][l]),
        ln_g=small["sg_ln_g"][l].reshape(1, -1), ln_b=small["sg_ln_b"][l].reshape(1, -1),
        w_s=small["w_s"][l], b_s=small["b_s"][l][:, :, None], mem_g=small["mem_norm_g"][l].reshape(1, -1))


def _position():
    x, y, c = lax.axis_index("x"), lax.axis_index("y"), lax.axis_index("c")
    return x, y, c, [(1 - x, y), (x, 1 - y), (1 - x, 1 - y)]


def _gather_first_hop(shards):
    n = len(shards)

    def make(x_refs, out_refs, sems):
        send_sems, recv_sems, local_sems = sems
        x, y, c, chips = _position()
        me = 4 * x + 2 * y + c
        peers = [(x, y, 1 - c)] + [(cx, cy, c) for cx, cy in chips]
        copies = [pltpu.make_async_copy(x_refs[t], out_refs[t].at[me], local_sems.at[t]) for t in range(n)]
        copies += [pltpu.make_async_remote_copy(
            src_ref=x_refs[t], dst_ref=out_refs[t].at[me], send_sem=send_sems.at[t, k], recv_sem=recv_sems.at[t, k],
            device_id=peer, device_id_type=MESH_ID) for t in range(n) for k, peer in enumerate(peers)]
        return copies

    return _Exchange(shards, [SDS((N_DEV,) + a.shape, a.dtype) for a in shards],
                     [pltpu.SemaphoreType.DMA((n, 4)), pltpu.SemaphoreType.DMA((n, 4)), pltpu.SemaphoreType.DMA((n,))], make)


def _gather_forward(gathered):
    n = len(gathered)

    def make(in_refs, out_refs, sems):
        send_sems, recv_sems = sems
        x, y, c, chips = _position()
        return [pltpu.make_async_remote_copy(
            src_ref=in_refs[t].at[4 * cx + 2 * cy + c], dst_ref=out_refs[t].at[4 * cx + 2 * cy + c],
            send_sem=send_sems.at[t, j], recv_sem=recv_sems.at[t, j], device_id=(x, y, 1 - c), device_id_type=MESH_ID)
            for t in range(n) for j, (cx, cy) in enumerate(chips)]

    return _Exchange(gathered, [SDS(a.shape, a.dtype) for a in gathered],
                     [pltpu.SemaphoreType.DMA((n, 3)), pltpu.SemaphoreType.DMA((n, 3))], make,
                     aliases={t: t for t in range(n)})


def _all_gather(shards, name):
    return _exchange_call(_gather_forward(_exchange_call(_gather_first_hop(shards), name + "_hop1")), name + "_hop2")


def _assemble_w_br(gathered):
    _, nb, rows, shard = gathered.shape

    def body(g_ref, o_ref):
        for d in range(N_DEV):
            o_ref[:, :, d * shard:(d + 1) * shard] = g_ref[d]

    return pl.pallas_call(
        body, name="assemble_w_br", out_shape=SDS((nb, rows, N_DEV * shard), gathered.dtype),
        compiler_params=pltpu.CompilerParams(vmem_limit_bytes=VMEM_LIMIT),
    )(gathered)


def _split_w_br(dw):
    nb, rows, cols = dw.shape
    shard = cols // N_DEV

    def body(d_ref, o_ref):
        for d in range(N_DEV):
            o_ref[d] = d_ref[:, :, d * shard:(d + 1) * shard].astype(BF16)

    return pl.pallas_call(
        body, name="split_w_br", out_shape=SDS((N_DEV, nb, rows, shard), BF16),
        compiler_params=pltpu.CompilerParams(vmem_limit_bytes=VMEM_LIMIT),
    )(dw)


def _scatter_to_sibling(dests):
    n = len(dests)

    def make(d_refs, recv_refs, sems):
        send_sems, recv_sems = sems
        x, y, c, _ = _position()
        return [pltpu.make_async_remote_copy(
            src_ref=d_refs[t].at[:, 1 - c], dst_ref=recv_refs[t], send_sem=send_sems.at[t],
            recv_sem=recv_sems.at[t], device_id=(x, y, 1 - c), device_id_type=MESH_ID) for t in range(n)]

    return _Exchange(dests, [SDS(a.shape[:1] + a.shape[2:], a.dtype) for a in dests],
                     [pltpu.SemaphoreType.DMA((n,)), pltpu.SemaphoreType.DMA((n,))], make)


def _scatter_to_chips(parts):
    n = len(parts)

    def make(p_refs, recv_refs, sems):
        send_sems, recv_sems = sems
        _, _, c, chips = _position()
        return [pltpu.make_async_remote_copy(
            src_ref=p_refs[t].at[2 * cx + cy], dst_ref=recv_refs[t].at[k], send_sem=send_sems.at[t, k],
            recv_sem=recv_sems.at[t, k], device_id=(cx, cy, c), device_id_type=MESH_ID)
            for t in range(n) for k, (cx, cy) in enumerate(chips)]

    return _Exchange(parts, [SDS((3,) + a.shape[1:], a.dtype) for a in parts],
                     [pltpu.SemaphoreType.DMA((n, 3)), pltpu.SemaphoreType.DMA((n, 3))], make)


def _pair_sums(names, dests, recv, core):
    return [_pair_sum(d, r, core, "rs_pair_sum_" + n) for n, d, r in zip(names, dests, recv)]


def _pair_sum(dest, recv, core, name):
    _, _, na, r, cdim = dest.shape

    def body(core_ref, a_ref, b_ref, o_ref):
        o_ref[...] = (a_ref[...].astype(F32) + b_ref[...].astype(F32)).astype(o_ref.dtype)

    blk = pl.BlockSpec((None, na, r, cdim), lambda j, core_ref: (j, 0, 0, 0))
    return pl.pallas_call(
        body, name=name, out_shape=SDS(recv.shape, BF16),
        grid_spec=pltpu.PrefetchScalarGridSpec(
            num_scalar_prefetch=1, grid=(4,),
            in_specs=[pl.BlockSpec((None, None, na, r, cdim), lambda j, core_ref: (j, core_ref[0], 0, 0, 0)), blk],
            out_specs=blk),
        compiler_params=_params("parallel"),
    )(core, dest, recv)


def _adamw_math(w, g, m, v):
    m = ADAM_B1 * m + (1.0 - ADAM_B1) * g
    v = ADAM_B2 * v + (1.0 - ADAM_B2) * (g * g)
    m_hat = m / (1.0 - ADAM_B1 ** ADAM_STEP)
    v_hat = v / (1.0 - ADAM_B2 ** ADAM_STEP)
    delta = -ADAM_LR * (m_hat / (jnp.sqrt(v_hat) + ADAM_EPS) + ADAM_WD * w)
    return delta, m, v


def _sum_and_adamw(parts, w, m, v, name):
    n, r, ln = parts.shape
    tr = _tile(r, 512, 16)

    def body(p_ref, w_ref, m_ref, v_ref, g_out, d_out, m_out, v_out):
        g = p_ref[0].astype(F32)
        for j in range(1, n):
            g = g + p_ref[j].astype(F32)
        delta, nm, nv = _adamw_math(w_ref[...], g, m_ref[...], v_ref[...])
        g_out[...] = g
        d_out[...] = delta
        m_out[...] = nm
        v_out[...] = nv

    blk = pl.BlockSpec((tr, ln), lambda i: (i, 0))
    return pl.pallas_call(
        body, name=name, grid=(r // tr,),
        in_specs=[pl.BlockSpec((n, tr, ln), lambda i: (0, i, 0)), blk, blk, blk],
        out_specs=(blk, blk, blk, blk), out_shape=tuple(SDS((r, ln), F32) for _ in range(4)),
        compiler_params=_params("parallel"),
    )(parts, w, m, v)


BIG = ("w_in", "w_mem_kv", "w_br", "w_out")
SMALL = ("norm_g", "q_norm_g", "k_norm_g", "sg_ln_g", "sg_ln_b", "w_s", "b_s", "mem_norm_g", "final_g")


def _pack(arrs, row_unit=16):
    flat = jnp.concatenate([a.reshape(-1) for a in arrs])
    pad = (-flat.shape[0]) % (row_unit * LANES)
    if pad:
        flat = jnp.concatenate([flat, jnp.zeros((pad,), flat.dtype)])
    return flat.reshape(-1, LANES)


def _unpack(buf, shapes):
    flat = buf.reshape(-1)
    out, off = [], 0
    for shp in shapes:
        n = 1
        for s_ in shp:
            n *= s_
        out.append(flat[off:off + n].reshape(shp))
        off += n
    return out


def _shard_sum_adamw(part, from_chips, chip, w, m, v, layer, prev, name):
    _, na, r, cdim = part.shape
    flat = w.ndim == 3
    assert not flat or na == 1
    tr = _tile(r, max(8, (256 * 1024) // (na * cdim)), 8)
    n_prev = 0 if prev is None else len(prev)

    def body(chip_ref, p_ref, f_ref, w_ref, m_ref, v_ref, *rest):
        g_out, d_out, m_out, v_out = rest[n_prev:]
        g = p_ref[...].astype(F32)
        for j in range(3):
            g = g + f_ref[j].astype(F32)
        delta, nm, nv = _adamw_math(w_ref[...], g, m_ref[...], v_ref[...])
        g_out[...] = g
        d_out[...] = delta
        m_out[...] = nm
        v_out[...] = nv

    a_blk = None if flat else na
    if flat:
        lay = pl.BlockSpec((None, tr, cdim), lambda i, chip_ref: (layer, i, 0))
    else:
        lay = pl.BlockSpec((None, na, tr, cdim), lambda i, chip_ref: (layer, 0, i, 0))
    return pl.pallas_call(
        body, name=name, out_shape=tuple(SDS(w.shape, F32) for _ in range(4)),
        grid_spec=pltpu.PrefetchScalarGridSpec(
            num_scalar_prefetch=1, grid=(r // tr,),
            in_specs=[pl.BlockSpec((None, a_blk, tr, cdim), lambda i, chip_ref: (chip_ref[0], 0, i, 0)),
                      pl.BlockSpec((3, a_blk, tr, cdim), lambda i, chip_ref: (0, 0, i, 0)), lay, lay, lay]
            + [HBM] * n_prev,
            out_specs=(lay, lay, lay, lay)),
        input_output_aliases={6 + j: j for j in range(n_prev)},
        compiler_params=_params("parallel"),
    )(chip, part, from_chips, w, m, v, *(prev or ()))


def kernel(x, mem, norm_g, w_in, q_norm_g, k_norm_g, sg_ln_g, sg_ln_b, w_s, b_s, mem_norm_g, w_mem_kv, w_br, w_out, final_g, loss_target, m_norm_g, m_w_in, m_q_norm_g, m_k_norm_g, m_sg_ln_g, m_sg_ln_b, m_w_s, m_b_s, m_mem_norm_g, m_w_mem_kv, m_w_br, m_w_out, m_final_g, v_norm_g, v_w_in, v_q_norm_g, v_k_norm_g, v_sg_ln_g, v_sg_ln_b, v_w_s, v_b_s, v_mem_norm_g, v_w_mem_kv, v_w_br, v_w_out, v_final_g):
    wts = dict(norm_g=norm_g, w_in=w_in, q_norm_g=q_norm_g, k_norm_g=k_norm_g, sg_ln_g=sg_ln_g, sg_ln_b=sg_ln_b,
               w_s=w_s, b_s=b_s, mem_norm_g=mem_norm_g, w_mem_kv=w_mem_kv, w_br=w_br, w_out=w_out, final_g=final_g)
    mom1 = dict(norm_g=m_norm_g, w_in=m_w_in, q_norm_g=m_q_norm_g, k_norm_g=m_k_norm_g, sg_ln_g=m_sg_ln_g,
                sg_ln_b=m_sg_ln_b, w_s=m_w_s, b_s=m_b_s, mem_norm_g=m_mem_norm_g, w_mem_kv=m_w_mem_kv, w_br=m_w_br,
                w_out=m_w_out, final_g=m_final_g)
    mom2 = dict(norm_g=v_norm_g, w_in=v_w_in, q_norm_g=v_q_norm_g, k_norm_g=v_k_norm_g, sg_ln_g=v_sg_ln_g,
                sg_ln_b=v_sg_ln_b, w_s=v_w_s, b_s=v_b_s, mem_norm_g=v_mem_norm_g, w_mem_kv=v_w_mem_kv, w_br=v_w_br,
                w_out=v_w_out, final_g=v_final_g)
    dp = w_in.shape[0]
    core = lax.axis_index("c").astype(jnp.int32).reshape(1)
    chip = (2 * lax.axis_index("x") + lax.axis_index("y")).astype(jnp.int32).reshape(1)

    t_in = lambda a: jnp.swapaxes(a, 1, 2)
    wts, mom1, mom2 = [dict(d_, w_in=t_in(d_["w_in"])) for d_ in (wts, mom1, mom2)]

    shard_bf = {n: wts[n].astype(BF16) for n in BIG}
    shards = lambda l: [shard_bf[n][l] for n in BIG]
    x_l, mem_l = x[0], mem[0]
    tabs = _rope_tables(x_l.shape[0])

    gathered = _all_gather(shards(0), "weights_all_gather")
    layers, saved = [], []
    for l in range(dp):
        g_in, g_kv, g_br, g_out = gathered
        layers.append(_layer_weights(l, g_in.reshape(IN_WIDTH, -1), g_kv.reshape(D_MODEL, -1), _assemble_w_br(g_br),
                                     g_out.reshape(D_MODEL, -1), wts))
        x_l, s, gathered = _layer_fwd(x_l, mem_l, layers[l], tabs, next_shards=shards(l + 1) if l + 1 < dp else None)
        saved.append(s)
    loss_local, dx, d_final_g = _loss_head(x_l, final_g.reshape(1, -1), loss_target[0])
    loss = lax.psum(loss_local[0, 0], AXES)

    def finish(l, parts, from_chips, prev):
        return {n: _shard_sum_adamw(p, f, chip, wts[n], mom1[n], mom2[n], l, None if prev is None else prev[n],
                                    "sum_adamw_" + n)
                for n, p, f in zip(BIG, parts, from_chips)}

    grads, updated, pending = [None] * dp, None, None
    for l in reversed(range(dp)):
        dx, grads[l], scattered, own = _layer_bwd(dx, mem_l, layers[l], saved[l], tabs, pending=pending, core=core,
                                                  scatter_own=(l == 0))
        if scattered is not None:
            updated = finish(l + 1, *scattered, updated)
        pending = [grads[l][n] for n in BIG]
    updated = finish(0, *own, updated)
    grad_x = dx
    big_out = [{n: (t_in(updated[n][k]) if n == "w_in" else updated[n][k]) for n in BIG} for k in range(4)]

    small_g = {n: jnp.stack([g[n] for g in grads]) for n in SMALL if n != "final_g"}
    small_g["final_g"] = d_final_g
    (all_small,) = _all_gather([_pack([small_g[n] for n in SMALL])], "small_all_gather")
    small_bufs = _sum_and_adamw(
        all_small, _pack([wts[n] for n in SMALL]), _pack([mom1[n] for n in SMALL]), _pack([mom2[n] for n in SMALL]),
        "small_sum_adamw")

    outs = []
    for big_vals, small_buf in zip(big_out, small_bufs):
        vals = dict(big_vals)
        vals.update(zip(SMALL, _unpack(small_buf, [wts[n].shape for n in SMALL])))
        outs.append(vals)
    order = ("norm_g", "w_in", "q_norm_g", "k_norm_g", "sg_ln_g", "sg_ln_b", "w_s", "b_s", "mem_norm_g", "w_mem_kv",
             "w_br", "w_out", "final_g")
    result = [loss, grad_x[None]]
    for vals in outs:
        result += [vals[n] for n in order]
    return tuple(result)
```

```python
import functools

import jax
import jax.numpy as jnp
from jax import lax
from jax.experimental import pallas as pl
from jax.experimental.pallas import tpu as pltpu

F32 = jnp.float32
BF16 = jnp.bfloat16
SDS = jax.ShapeDtypeStruct
MESH_ID = pl.DeviceIdType.MESH
AXES = ("x", "y", "c")
N_DEV = 8

D_MODEL = 1024
DEPTH = 4
GRID_W = 64
CHUNK = 128
ROPE_THETA = 10000.0
EPS = 1e-6
HEAD_DIM = 64
A_HEADS = 8
A_KV_HEADS = 2
B_GROUPS = 4
M_HEADS = 4
M_HEAD_DIM = 128
BW = 512
W_A = 1280
W_B = 1536
W_M = 1024
W_G = 3072
IN_WIDTH = W_A + W_B + W_M + W_G
GROUP_OFFS = (0, W_A, W_A + W_B, W_A + W_B + W_M, IN_WIDTH)

ADAM_LR = 0.001
ADAM_B1 = 0.9
ADAM_B2 = 0.999
ADAM_EPS = 1e-08
ADAM_WD = 0.01
ADAM_STEP = 10

LANES = 128
KEY_CHUNK = 2048
VMEM_LIMIT = 52 * 1024 * 1024


def _tile(n, cap, unit=LANES):
    if n <= cap:
        return n
    t = (cap // unit) * unit
    while n % t:
        t -= unit
    return t


def _params(*sem):
    return pltpu.CompilerParams(dimension_semantics=sem, vmem_limit_bytes=VMEM_LIMIT)


def _sigmoid(z):
    return 0.5 * jnp.tanh(0.5 * z) + 0.5


def _silu_and_grad(z):
    s = _sigmoid(z)
    return z * s, s * (1.0 + z * (1.0 - s))


def _dot(a, b, dims):
    return lax.dot_general(a, b, (dims, ((), ())), preferred_element_type=F32)


NN = ((1,), (0,))
NT = ((1,), (1,))
TN = ((0,), (0,))
HBM = pl.BlockSpec(memory_space=pl.ANY)


class _Exchange:
    def __init__(self, ins, out_shapes, sems, make, aliases=None):
        self.ins, self.out_shapes, self.sems, self.make = list(ins), list(out_shapes), list(sems), make
        self.aliases = dict(aliases or {})

    def start(self, in_refs, out_refs, sems):
        for cp in self.make(in_refs, out_refs, sems):
            cp.start()

    def finish(self, in_refs, out_refs, sems):
        for cp in self.make(in_refs, out_refs, sems):
            cp.wait()


def _both(a, b):
    if a is None or b is None:
        return a if b is None else b
    n_in, n_out, n_sem = len(a.ins), len(a.out_shapes), len(a.sems)

    def make(in_refs, out_refs, sems):
        return (a.make(in_refs[:n_in], out_refs[:n_out], sems[:n_sem])
                + b.make(in_refs[n_in:], out_refs[n_out:], sems[n_sem:]))

    aliases = dict(a.aliases)
    aliases.update({n_in + i: n_out + o for i, o in b.aliases.items()})
    return _Exchange(a.ins + b.ins, a.out_shapes + b.out_shapes, a.sems + b.sems, make, aliases)


def _host_call(body, ins, in_specs, out_specs, out_shape, *, name, grid, semantics, scratch=(), exchange=None):
    ins, out_shape, scratch = list(ins), list(out_shape), list(scratch)
    if exchange is None:
        res = pl.pallas_call(
            body, name=name, grid=grid, in_specs=list(in_specs), out_specs=tuple(out_specs), out_shape=tuple(out_shape),
            scratch_shapes=scratch, compiler_params=_params(*semantics))(*ins)
        return tuple(res), ()
    n_in, n_out, n_scr = len(ins), len(out_shape), len(scratch)
    x_in, x_out = len(exchange.ins), len(exchange.out_shapes)

    def carrying(*refs):
        o0 = n_in + x_in
        s0 = o0 + n_out + x_out
        c_in, c_out, c_sems = refs[n_in:o0], refs[o0 + n_out:s0], refs[s0 + n_scr:]
        ids = [pl.program_id(a) for a in range(len(grid))]
        first = functools.reduce(jnp.logical_and, [i == 0 for i in ids])
        last = functools.reduce(jnp.logical_and, [i == g - 1 for i, g in zip(ids, grid)])

        @pl.when(first)
        def _():
            exchange.start(c_in, c_out, c_sems)

        body(*refs[:n_in], *refs[o0:o0 + n_out], *refs[s0:s0 + n_scr])

        @pl.when(last)
        def _():
            exchange.finish(c_in, c_out, c_sems)

    res = pl.pallas_call(
        carrying, name=name, grid=grid, in_specs=list(in_specs) + [HBM] * x_in,
        out_specs=tuple(out_specs) + (HBM,) * x_out, out_shape=tuple(out_shape) + tuple(exchange.out_shapes),
        scratch_shapes=scratch + exchange.sems,
        input_output_aliases={n_in + i: n_out + o for i, o in exchange.aliases.items()},
        compiler_params=_params(*(["arbitrary"] * len(grid))))(*ins, *exchange.ins)
    return tuple(res[:n_out]), tuple(res[n_out:])


def _exchange_call(exchange, name):
    x_in = len(exchange.ins)

    def body(*refs):
        x_out = len(exchange.out_shapes)
        c_in, c_out, c_sems = refs[:x_in], refs[x_in:x_in + x_out], refs[x_in + x_out:]
        exchange.start(c_in, c_out, c_sems)
        exchange.finish(c_in, c_out, c_sems)

    return pl.pallas_call(
        body, name=name, in_specs=[HBM] * x_in, out_specs=tuple([HBM] * len(exchange.out_shapes)),
        out_shape=tuple(exchange.out_shapes), scratch_shapes=exchange.sems, input_output_aliases=exchange.aliases,
    )(*exchange.ins)


def _mm(a, b, mode, name):
    (m, k), (n, k2) = a.shape, (b.shape[::-1] if mode == "nn" else b.shape)
    assert k == k2 and mode in ("nn", "nt"), (a.shape, b.shape, mode)
    dims = {"nn": NN, "nt": NT}[mode]

    def body(a_ref, b_ref, o_ref):
        o_ref[...] = _dot(a_ref[...].astype(BF16), b_ref[...].astype(BF16), dims)

    return pl.pallas_call(body, name=name, out_shape=SDS((m, n), F32),
                          compiler_params=pltpu.CompilerParams(vmem_limit_bytes=VMEM_LIMIT))(a, b)


def _mm_tn(a, b, name):
    (k, m), (k2, n) = a.shape, b.shape
    assert k == k2, (a.shape, b.shape)
    tk = _tile(k, 2048, 16)
    nk = k // tk

    def body(a_ref, b_ref, o_ref, acc):
        kk = pl.program_id(0)
        part = _dot(a_ref[...].astype(BF16), b_ref[...].astype(BF16), TN)

        @pl.when(kk == 0)
        def _():
            acc[...] = part

        @pl.when(kk > 0)
        def _():
            acc[...] += part

        @pl.when(kk == nk - 1)
        def _():
            o_ref[...] = acc[...].astype(BF16)

    return pl.pallas_call(
        body, name=name, grid=(nk,),
        in_specs=[pl.BlockSpec((tk, m), lambda kk: (kk, 0)), pl.BlockSpec((tk, n), lambda kk: (kk, 0))],
        out_specs=pl.BlockSpec((m, n), lambda kk: (0, 0)), out_shape=SDS((m, n), BF16),
        scratch_shapes=[pltpu.VMEM((m, n), F32)], compiler_params=_params("arbitrary"),
    )(a, b)


def _rmsnorm_fwd(x, g, name):
    t, d = x.shape
    tm = _tile(t, 512, 8)

    def body(x_ref, g_ref, h_ref):
        xf = x_ref[...]
        r = lax.rsqrt(jnp.mean(xf * xf, axis=-1, keepdims=True) + EPS)
        h_ref[...] = (xf * r * g_ref[...]).astype(BF16)

    return pl.pallas_call(
        body, name=name, grid=(t // tm,),
        in_specs=[pl.BlockSpec((tm, d), lambda i: (i, 0)), pl.BlockSpec((1, d), lambda i: (0, 0))],
        out_specs=pl.BlockSpec((tm, d), lambda i: (i, 0)),
        out_shape=SDS((t, d), BF16), compiler_params=_params("parallel"),
    )(x, g)


def _rmsnorm_bwd_math(xf, g, dh):
    r = lax.rsqrt(jnp.mean(xf * xf, axis=-1, keepdims=True) + EPS)
    xh = xf * r
    gd = dh * g
    dx = r * (gd - xh * jnp.mean(gd * xh, axis=-1, keepdims=True))
    return dx, jnp.sum(dh * xh, axis=0, keepdims=True)


def _rmsnorm_dg(x, g, dh, name):
    t, d = x.shape
    tm = _tile(t, 512, 8)

    def body(x_ref, g_ref, dh_ref, dg_ref):
        @pl.when(pl.program_id(0) == 0)
        def _():
            dg_ref[...] = jnp.zeros_like(dg_ref)

        dg_ref[...] += _rmsnorm_bwd_math(x_ref[...], g_ref[...], dh_ref[...])[1]

    row = pl.BlockSpec((tm, d), lambda i: (i, 0))
    vec = pl.BlockSpec((1, d), lambda i: (0, 0))
    return pl.pallas_call(
        body, name=name, grid=(t // tm,), in_specs=[row, vec, row], out_specs=vec, out_shape=SDS((1, d), F32),
        compiler_params=_params("arbitrary"),
    )(x, g, dh)


def _group_rows(p):
    return slice(GROUP_OFFS[p], GROUP_OFFS[p + 1])


def _d_w_in_t(dps, h):
    t, d = h.shape
    tm = 256
    assert all(off % tm == 0 for off in GROUP_OFFS)
    tiles = [a.shape[1] // tm for a in dps]
    starts = [sum(tiles[:p]) for p in range(len(dps))]

    def body(*refs):
        a_refs, h_ref, o_ref = refs[:len(dps)], refs[len(dps)], refs[len(dps) + 1]
        i = pl.program_id(0)
        for p, a_ref in enumerate(a_refs):
            @pl.when((i >= starts[p]) & (i < starts[p] + tiles[p]))
            def _():
                o_ref[...] = _dot(a_ref[...], h_ref[...], TN).astype(BF16)

    def tile_of(p):
        return lambda i: jnp.clip(i - starts[p], 0, tiles[p] - 1)

    return pl.pallas_call(
        body, name="d_w_in", grid=(sum(tiles),),
        in_specs=[pl.BlockSpec((t, tm), lambda i, p=p: (0, tile_of(p)(i))) for p in range(len(dps))]
        + [pl.BlockSpec((t, d), lambda i: (0, 0), pipeline_mode=pl.Buffered(1))],
        out_specs=pl.BlockSpec((tm, d), lambda i: (i, 0)),
        out_shape=SDS((sum(tiles) * tm, d), BF16), compiler_params=_params("arbitrary"),
    )(*dps, h)


def _norm_in_proj(x, g, w_t, cs, sn, qg2, kg2, lng, lnb, ws, bs, kv):
    t, d = x.shape
    tm = _tile(t, 512, LANES)
    widths = [GROUP_OFFS[p + 1] - GROUP_OFFS[p] for p in range(4)]
    scale = HEAD_DIM ** -0.5

    def body(x_ref, g_ref, w_ref, cs_ref, sn_ref, qg_ref, kg_ref, lng_ref, lnb_ref, ws_ref, bs_ref, kv_ref,
             h_ref, pa_ref, pb_ref, pm_ref, pg_ref, q_ref, qT_ref, k_ref, kT_ref, vT_ref, yb_ref, ym_ref):
        xf = x_ref[...]
        r = lax.rsqrt(jnp.mean(xf * xf, axis=-1, keepdims=True) + EPS)
        hb = (xf * r * g_ref[...]).astype(BF16)
        h_ref[...] = hb
        groups = [_dot(hb, w_ref[_group_rows(p), :], NT).astype(BF16) for p in range(4)]
        for o_ref, val in zip((pa_ref, pb_ref, pm_ref, pg_ref), groups):
            o_ref[...] = val
        pa = groups[0]
        lane = lax.broadcasted_iota(jnp.int32, (tm, LANES), 1)
        c, s = cs_ref[...], sn_ref[...]

        def norm_rope(xg, g2):
            rr = lax.rsqrt(_head_sums(xg * xg, lane) * (1.0 / HEAD_DIM) + EPS)
            xn = xg * rr * g2
            return xn * c + _swap16(xn, lane) * s

        for gi in range(4):
            sl = slice(gi * LANES, (gi + 1) * LANES)
            qr = norm_rope(pa[:, sl].astype(F32), qg_ref[...]) * scale
            q_ref[:, sl] = qr.astype(BF16)
            qT_ref[sl, :] = qr.T.astype(BF16)
        kr = norm_rope(pa[:, 512:640].astype(F32), kg_ref[...])
        kT_ref[...] = kr.T.astype(BF16)
        vT_ref[...] = pa[:, 640:768].astype(F32).T.astype(BF16)
        kr = kr.astype(BF16)
        for kvh in range(A_KV_HEADS):
            k_ref[kvh] = kr[:, kvh * HEAD_DIM:(kvh + 1) * HEAD_DIM]
        _gmlp_tile(groups[1], lng_ref, lnb_ref, ws_ref, bs_ref, yb_ref)
        _mem_attn_tile(groups[2], kv_ref, ym_ref)

    row = lambda wd: pl.BlockSpec((tm, wd), lambda i: (i, 0))
    col = lambda rws: pl.BlockSpec((rws, tm), lambda i: (0, i))
    whole = lambda a: pl.BlockSpec(a.shape, lambda i: (0,) * a.ndim)
    hm = pl.BlockSpec((A_KV_HEADS, tm, HEAD_DIM), lambda i: (0, i, 0))
    return pl.pallas_call(
        body, name="norm_in_proj", grid=(t // tm,),
        in_specs=[row(d), whole(g), pl.BlockSpec(w_t.shape, lambda i: (0, 0), pipeline_mode=pl.Buffered(1)),
                  row(LANES), row(LANES)] + [whole(a) for a in (qg2, kg2, lng, lnb, ws, bs, kv)],
        out_specs=(row(d),) + tuple(row(wd) for wd in widths)
        + (row(BW), col(BW), hm, col(LANES), col(LANES), row(BW), row(BW)),
        out_shape=(SDS((t, d), BF16),) + tuple(SDS((t, wd), BF16) for wd in widths)
        + (SDS((t, BW), BF16), SDS((BW, t), BF16), SDS((A_KV_HEADS, t, HEAD_DIM), BF16),
           SDS((LANES, t), BF16), SDS((LANES, t), BF16), SDS((t, BW), BF16), SDS((t, BW), BF16)),
        compiler_params=_params("parallel"),
    )(x, g, w_t, cs, sn, qg2, kg2, lng, lnb, ws, bs, kv)


def _dh_rmsnorm_bwd(dps, w_t, x, g, dres, exchange=None):
    t, d = x.shape
    tm = _tile(t, 256, 16)
    n = len(dps)

    def body(*refs):
        dp_refs, w_ref = refs[:n], refs[n]
        x_ref, g_ref, dres_ref, dx_ref, dg_ref = refs[n + 1:]
        dh = None
        for p, dp_ref in enumerate(dp_refs):
            part = _dot(dp_ref[...], w_ref[_group_rows(p), :], NN)
            dh = part if dh is None else dh + part
        dx, dg = _rmsnorm_bwd_math(x_ref[...], g_ref[...], dh)
        dx_ref[...] = dx + dres_ref[...]

        @pl.when(pl.program_id(0) == 0)
        def _():
            dg_ref[...] = jnp.zeros_like(dg_ref)

        dg_ref[...] += dg

    row = pl.BlockSpec((tm, d), lambda i: (i, 0))
    vec = pl.BlockSpec((1, d), lambda i: (0, 0))
    return _host_call(
        body, list(dps) + [w_t, x, g, dres],
        [pl.BlockSpec((tm, a.shape[1]), lambda i: (i, 0)) for a in dps]
        + [pl.BlockSpec(w_t.shape, lambda i: (0, 0), pipeline_mode=pl.Buffered(1)), row, vec, row],
        (row, vec), (SDS((t, d), F32), SDS((1, d), F32)),
        name="dh_rmsnorm_bwd", grid=(t // tm,), semantics=("arbitrary",), exchange=exchange)


def _rope_tables(t):
    rows = t // GRID_W
    row = jnp.repeat(jnp.arange(rows, dtype=F32), GRID_W)
    col = jnp.tile(jnp.arange(GRID_W, dtype=F32), rows)
    n_freq = HEAD_DIM // 4
    inv = ROPE_THETA ** (-jnp.arange(n_freq, dtype=F32) / n_freq)
    ang = jnp.stack([row[:, None] * inv, col[:, None] * inv], axis=1)
    cos, sin = jnp.cos(ang), jnp.sin(ang)
    c64 = jnp.concatenate([cos[:, 0], cos[:, 0], cos[:, 1], cos[:, 1]], axis=-1)
    s64 = jnp.concatenate([-sin[:, 0], sin[:, 0], -sin[:, 1], sin[:, 1]], axis=-1)
    return jnp.tile(c64, (1, 2)), jnp.tile(s64, (1, 2))


def _head_sums(v, lane):
    lo = jnp.sum(jnp.where(lane < HEAD_DIM, v, 0.0), axis=-1, keepdims=True)
    hi = jnp.sum(jnp.where(lane < HEAD_DIM, 0.0, v), axis=-1, keepdims=True)
    return jnp.where(lane < HEAD_DIM, lo, hi)


def _swap16(v, lane):
    return jnp.where((lane % 32) < 16, pltpu.roll(v, LANES - 16, 1), pltpu.roll(v, 16, 1))


def _attn_fwd(qT, k, vT, pa, exchange=None):
    t = qT.shape[1]
    tq = _tile(t, 256, LANES)
    grp = A_HEADS // A_KV_HEADS

    ck = _tile(t, KEY_CHUNK, LANES)

    def body(qT_ref, k_ref, vT_ref, pa_ref, o_ref, lse_ref, y_ref):
        def finish_pair(pair, o_pair):
            o2 = jnp.concatenate(o_pair, axis=0).T
            cols = slice(pair * LANES, (pair + 1) * LANES)
            o_ref[:, cols] = o2
            z = pa_ref[:, 768 + pair * LANES:768 + (pair + 1) * LANES].astype(F32)
            y_ref[:, cols] = (o2 * (z * _sigmoid(z))).astype(BF16)

        n_c = t // ck
        items = [(h, c) for h in range(A_HEADS) for c in range(n_c)]

        def scores(h, c):
            return _dot(k_ref[h // grp, c * ck:(c + 1) * ck, :], qT_ref[h * HEAD_DIM:(h + 1) * HEAD_DIM, :], NN)

        def weighted_values(h, c, pT):
            kvh = h // grp
            return _dot(vT_ref[kvh * HEAD_DIM:(kvh + 1) * HEAD_DIM, c * ck:(c + 1) * ck], pT, NN)

        o_pair, sT, pending, acc = [], scores(*items[0]), None, None
        for i in range(len(items) + 1):
            item = items[i] if i < len(items) else None
            sT_next = scores(*items[i + 1]) if i + 1 < len(items) else None
            if pending is not None:
                (ph, pc, pT, alpha, l_run) = pending
                o_c = weighted_values(ph, pc, pT)
                acc = o_c if alpha is None else acc * alpha + o_c
                if pc == n_c - 1:
                    o_pair.append(acc / l_run)
                    if ph % 2:
                        finish_pair(ph // 2, o_pair)
                        o_pair = []
                pending = None
            if item is not None:
                h, c = item
                m_c = jnp.max(sT, axis=0, keepdims=True)
                if c == 0:
                    m_new, alpha = m_c, None
                else:
                    m_new = jnp.maximum(m, m_c)
                    alpha = jnp.exp(m - m_new)
                pT = jnp.exp(sT - m_new)
                l_c = jnp.sum(pT, axis=0, keepdims=True)
                l = l_c if c == 0 else l * alpha + l_c
                m = m_new
                if c == n_c - 1:
                    lse_ref[h:h + 1, :] = m + jnp.log(l)
                pending = (h, c, pT.astype(BF16), alpha, l)
            sT = sT_next

    row = lambda w: pl.BlockSpec((tq, w), lambda i: (i, 0))
    return _host_call(
        body, (qT, k, vT, pa),
        [pl.BlockSpec((BW, tq), lambda i: (0, i)), pl.BlockSpec((A_KV_HEADS, t, HEAD_DIM), lambda i: (0, 0, 0)),
         pl.BlockSpec((A_KV_HEADS * HEAD_DIM, t), lambda i: (0, 0)), row(W_A)],
        (row(BW), pl.BlockSpec((A_HEADS, tq), lambda i: (0, i)), row(BW)),
        (SDS((t, BW), F32), SDS((A_HEADS, t), F32), SDS((t, BW), BF16)),
        name="attn_fwd", grid=(t // tq,), semantics=("parallel",), exchange=exchange)


def _attn_bwd(q, qT, k, kT, vT, pa, o, lse, dy, exchange=None):
    t = q.shape[0]
    tq = _tile(t, 256, LANES)
    grp = A_HEADS // A_KV_HEADS
    gw = grp * HEAD_DIM

    def body(q_ref, qT_ref, k_ref, kT_ref, vT_ref, z_ref, o_ref, lse_ref, dy_ref, dq_ref, dkT_ref, dvT_ref):
        @pl.when(pl.program_id(1) == 0)
        def _():
            dkT_ref[...] = jnp.zeros_like(dkT_ref)
            dvT_ref[...] = jnp.zeros_like(dvT_ref)

        z = z_ref[...].astype(F32)
        do = dy_ref[...] * (z * _sigmoid(z))
        doo = do * o_ref[...]
        doT = do.T
        kk, kT, vT = k_ref[...], kT_ref[...], vT_ref[...]
        heads = [slice(j * HEAD_DIM, (j + 1) * HEAD_DIM) for j in range(grp)]

        def scores(hs):
            return _dot(q_ref[:, hs], kT, NN), _dot(do[:, hs].astype(BF16), vT, NN)

        s, dp = scores(heads[0])
        for j, hs in enumerate(heads):
            nxt = scores(heads[j + 1]) if j + 1 < grp else None
            delta = jnp.sum(doo[:, hs], axis=-1, keepdims=True)
            p = jnp.exp(s - lse_ref[:, j:j + 1])
            ds_b = (p * (dp - delta)).astype(BF16)
            dq_ref[:, hs] = _dot(ds_b, kk, NN)
            dkT_ref[...] += _dot(qT_ref[hs, :], ds_b, NN)
            dvT_ref[...] += _dot(doT[hs, :].astype(BF16), p.astype(BF16), NN)
            if nxt is not None:
                s, dp = nxt

    grp_blk = pl.BlockSpec((tq, gw), lambda g, i: (i, g))
    kvT_blk = pl.BlockSpec((HEAD_DIM, t), lambda g, i: (g, 0))
    acc_blk = pl.BlockSpec((None, HEAD_DIM, t), lambda g, i: (g, 0, 0))
    return _host_call(
        body, (q, qT, k, kT, vT, pa, o, lse, dy),
        [grp_blk, pl.BlockSpec((gw, tq), lambda g, i: (g, i)), pl.BlockSpec((None, t, HEAD_DIM), lambda g, i: (g, 0, 0)),
         kvT_blk, kvT_blk, pl.BlockSpec((tq, gw), lambda g, i: (i, 768 // gw + g)), grp_blk,
         pl.BlockSpec((None, tq, grp), lambda g, i: (g, i, 0)), grp_blk],
        (grp_blk, acc_blk, acc_blk),
        (SDS((t, BW), F32), SDS((A_KV_HEADS, HEAD_DIM, t), F32), SDS((A_KV_HEADS, HEAD_DIM, t), F32)),
        name="attn_bwd", grid=(A_KV_HEADS, t // tq), semantics=("arbitrary", "arbitrary"), exchange=exchange)


def _attn_prep_bwd(pa, cs, sn, qg2, kg2, dq, dkT, dvT, dy, o, exchange=None):
    t = pa.shape[0]
    tq = _tile(t, 512, LANES)
    scale = HEAD_DIM ** -0.5

    def body(pa_ref, cs_ref, sn_ref, qg_ref, kg_ref, dq_ref, dkT_ref, dvT_ref, dy_ref, o_ref, dpa_ref, dqg_ref, dkg_ref):
        lane = lax.broadcasted_iota(jnp.int32, (tq, LANES), 1)
        c, s = cs_ref[...], sn_ref[...]

        @pl.when(pl.program_id(0) == 0)
        def _():
            dqg_ref[...] = jnp.zeros_like(dqg_ref)
            dkg_ref[...] = jnp.zeros_like(dkg_ref)

        def norm_rope_bwd(xg, g2, dout):
            r = lax.rsqrt(_head_sums(xg * xg, lane) * (1.0 / HEAD_DIM) + EPS)
            xh = xg * r
            dxn = dout * c + _swap16(dout * s, lane)
            gd = dxn * g2
            dx = r * (gd - xh * (_head_sums(gd * xh, lane) * (1.0 / HEAD_DIM)))
            return dx, jnp.sum(dxn * xh, axis=0, keepdims=True)

        for gi in range(4):
            sl = slice(gi * LANES, (gi + 1) * LANES)
            dx, dg = norm_rope_bwd(pa_ref[:, sl].astype(F32), qg_ref[...], dq_ref[:, sl] * scale)
            dpa_ref[:, sl] = dx.astype(BF16)
            dqg_ref[...] += dg
        dx, dg = norm_rope_bwd(pa_ref[:, 512:640].astype(F32), kg_ref[...], dkT_ref[...].T)
        dpa_ref[:, 512:640] = dx.astype(BF16)
        dkg_ref[...] += dg
        dpa_ref[:, 640:768] = dvT_ref[...].T.astype(BF16)
        z = pa_ref[:, 768:1280].astype(F32)
        _, dsilu = _silu_and_grad(z)
        dpa_ref[:, 768:1280] = (dy_ref[...] * o_ref[...] * dsilu).astype(BF16)

    row = lambda w: pl.BlockSpec((tq, w), lambda i: (i, 0))
    col = pl.BlockSpec((LANES, tq), lambda i: (0, i))
    vec = pl.BlockSpec((1, LANES), lambda i: (0, 0))
    return _host_call(
        body, (pa, cs, sn, qg2, kg2, dq, dkT, dvT, dy, o),
        [row(W_A), row(LANES), row(LANES), vec, vec, row(BW), col, col, row(BW), row(BW)],
        (row(W_A), vec, vec), (SDS((t, W_A), BF16), SDS((1, LANES), F32), SDS((1, LANES), F32)),
        name="attn_prep_bwd", grid=(t // tq,), semantics=("arbitrary",), exchange=exchange)


def _layer_norm(v, g, b):
    mu = jnp.mean(v, axis=-1, keepdims=True)
    xc = v - mu
    rs = lax.rsqrt(jnp.mean(xc * xc, axis=-1, keepdims=True) + EPS)
    xh = xc * rs
    return xh * g + b, xh, rs


def _gmlp_tile(pb, g_ref, b_ref, ws_ref, bs_ref, y_ref):
    vln, _, _ = _layer_norm(pb[:, BW:2 * BW].astype(F32), g_ref[...], b_ref[...])
    vb = vln.astype(BF16)
    for gi in range(B_GROUPS):
        w = ws_ref[gi].astype(BF16)
        cs_ = slice(gi * CHUNK, (gi + 1) * CHUNK)
        for n in range(pb.shape[0] // CHUNK):
            rs_ = slice(n * CHUNK, (n + 1) * CHUNK)
            mixed = _dot(w, vb[rs_, cs_], NN) + bs_ref[gi]
            z = pb[rs_, 2 * BW + gi * CHUNK:2 * BW + (gi + 1) * CHUNK].astype(F32)
            y_ref[rs_, cs_] = (pb[rs_, cs_].astype(F32) * mixed * (z * _sigmoid(z))).astype(BF16)


def _gmlp_bwd(pb, lng, lnb, ws, bs, dy):
    t = pb.shape[0]
    tb = _tile(t, 256, CHUNK)

    def body(pb_ref, g_ref, b_ref, ws_ref, bs_ref, dy_ref, dpb_ref, dws_ref, dbs_ref, dg_ref, db_ref, dvln_ref):
        @pl.when(pl.program_id(0) == 0)
        def _():
            dws_ref[...] = jnp.zeros_like(dws_ref)
            dbs_ref[...] = jnp.zeros_like(dbs_ref)
            dg_ref[...] = jnp.zeros_like(dg_ref)
            db_ref[...] = jnp.zeros_like(db_ref)

        vln, xh, rs = _layer_norm(pb_ref[:, BW:2 * BW].astype(F32), g_ref[...], b_ref[...])
        vb = vln.astype(BF16)
        for gi in range(B_GROUPS):
            w = ws_ref[gi].astype(BF16)
            cs_ = slice(gi * CHUNK, (gi + 1) * CHUNK)
            for n in range(tb // CHUNK):
                rs_ = slice(n * CHUNK, (n + 1) * CHUNK)
                vbc = vb[rs_, cs_]
                mixed = _dot(w, vbc, NN) + bs_ref[gi]
                zs = slice(2 * BW + gi * CHUNK, 2 * BW + (gi + 1) * CHUNK)
                z = pb_ref[rs_, zs].astype(F32)
                u = pb_ref[rs_, cs_].astype(F32)
                sil, dsil = _silu_and_grad(z)
                dyc = dy_ref[rs_, cs_]
                dmixed = dyc * u * sil
                dpb_ref[rs_, cs_] = (dyc * mixed * sil).astype(BF16)
                dpb_ref[rs_, zs] = (dyc * u * mixed * dsil).astype(BF16)
                dmb = dmixed.astype(BF16)
                dws_ref[gi] += _dot(dmb, vbc, NT)
                dbs_ref[gi] += jnp.sum(dmixed, axis=-1, keepdims=True)
                dvln_ref[rs_, cs_] = _dot(w, dmb, TN)
        dvln = dvln_ref[...]
        dg_ref[...] += jnp.sum(dvln * xh, axis=0, keepdims=True)
        db_ref[...] += jnp.sum(dvln, axis=0, keepdims=True)
        gd = dvln * g_ref[...]
        dv = rs * (gd - jnp.mean(gd, axis=-1, keepdims=True) - xh * jnp.mean(gd * xh, axis=-1, keepdims=True))
        dpb_ref[:, BW:2 * BW] = dv.astype(BF16)

    vec = pl.BlockSpec((1, BW), lambda i: (0, 0))
    wsb = pl.BlockSpec((B_GROUPS, CHUNK, CHUNK), lambda i: (0, 0, 0))
    bsb = pl.BlockSpec((B_GROUPS, CHUNK, 1), lambda i: (0, 0, 0))
    return pl.pallas_call(
        body, name="gmlp_bwd", grid=(t // tb,),
        in_specs=[pl.BlockSpec((tb, W_B), lambda i: (i, 0)), vec, vec, wsb, bsb, pl.BlockSpec((tb, BW), lambda i: (i, 0))],
        out_specs=(pl.BlockSpec((tb, W_B), lambda i: (i, 0)), wsb, bsb, vec, vec),
        out_shape=(SDS((t, W_B), BF16), SDS((B_GROUPS, CHUNK, CHUNK), F32), SDS((B_GROUPS, CHUNK, 1), F32),
                   SDS((1, BW), F32), SDS((1, BW), F32)),
        scratch_shapes=[pltpu.VMEM((tb, BW), F32)],
        compiler_params=_params("arbitrary"),
    )(pb, lng, lnb, ws, bs, dy)


def _mem_scores(pm_ref, kv_ref, h):
    hs = slice(h * M_HEAD_DIM, (h + 1) * M_HEAD_DIM)
    return _dot(pm_ref[:, hs].astype(BF16), kv_ref[:, hs].astype(BF16), NT) * (M_HEAD_DIM ** -0.5)


def _softmax_rows(s):
    e = jnp.exp(s - jnp.max(s, axis=-1, keepdims=True))
    return e / jnp.sum(e, axis=-1, keepdims=True)


def _mem_attn_tile(pm, kv_ref, y_ref):
    s_next = _mem_scores(pm, kv_ref, 0)
    for h in range(M_HEADS):
        hs = slice(h * M_HEAD_DIM, (h + 1) * M_HEAD_DIM)
        vh = kv_ref[:, BW + h * M_HEAD_DIM:BW + (h + 1) * M_HEAD_DIM].astype(BF16)
        s, s_next = s_next, (_mem_scores(pm, kv_ref, h + 1) if h + 1 < M_HEADS else None)
        p = _softmax_rows(s)
        o = _dot(p.astype(BF16), vh, NN)
        z = pm[:, BW + h * M_HEAD_DIM:BW + (h + 1) * M_HEAD_DIM].astype(F32)
        y_ref[:, hs] = (o * (z * _sigmoid(z))).astype(BF16)


def _mem_attn_bwd(pm, kv, dy):
    t = pm.shape[0]
    tq = _tile(t, 512, 8)
    ml = kv.shape[0]
    scale = M_HEAD_DIM ** -0.5

    def body(pm_ref, kv_ref, dy_ref, dpm_ref, dkv_ref):
        @pl.when(pl.program_id(0) == 0)
        def _():
            dkv_ref[...] = jnp.zeros_like(dkv_ref)

        s_next = _mem_scores(pm_ref, kv_ref, 0)
        for h in range(M_HEADS):
            hs = slice(h * M_HEAD_DIM, (h + 1) * M_HEAD_DIM)
            zs = slice(BW + h * M_HEAD_DIM, BW + (h + 1) * M_HEAD_DIM)
            kh = kv_ref[:, hs].astype(BF16)
            vh = kv_ref[:, zs].astype(BF16)
            qh = pm_ref[:, hs].astype(BF16)
            s, s_next = s_next, (_mem_scores(pm_ref, kv_ref, h + 1) if h + 1 < M_HEADS else None)
            p = _softmax_rows(s)
            pb = p.astype(BF16)
            o = _dot(pb, vh, NN)
            sil, dsil = _silu_and_grad(pm_ref[:, zs].astype(F32))
            dyh = dy_ref[:, hs]
            do = dyh * sil
            dpm_ref[:, zs] = (dyh * o * dsil).astype(BF16)
            delta = jnp.sum(do * o, axis=-1, keepdims=True)
            do_b = do.astype(BF16)
            dp = _dot(do_b, vh, NT)
            dr_b = (p * (dp - delta) * scale).astype(BF16)
            dpm_ref[:, hs] = _dot(dr_b, kh, NN).astype(BF16)
            dkv_ref[:, hs] += _dot(dr_b, qh, TN)
            dkv_ref[:, zs] += _dot(pb, do_b, TN)

    kvb = pl.BlockSpec((ml, 2 * BW), lambda i: (0, 0))
    return pl.pallas_call(
        body, name="mem_attn_bwd", grid=(t // tq,),
        in_specs=[pl.BlockSpec((tq, W_M), lambda i: (i, 0)), kvb, pl.BlockSpec((tq, BW), lambda i: (i, 0))],
        out_specs=(pl.BlockSpec((tq, W_M), lambda i: (i, 0)), kvb),
        out_shape=(SDS((t, W_M), BF16), SDS((ml, 2 * BW), F32)),
        compiler_params=_params("arbitrary"),
    )(pm, kv, dy)


def _merge_fwd(ya, yb, ym, pg, wbr, x, w_out, exchange=None):
    t = ya.shape[0]
    tm = _tile(t, 512, 8)

    def body(ya_ref, yb_ref, ym_ref, pg_ref, w_ref, x_ref, wo_ref, m_ref, xn_ref):
        ups = [_dot(y_ref[...], w_ref[n], NN) for n, y_ref in enumerate((ya_ref, yb_ref, ym_ref))]
        acc = None
        for n, up in enumerate(ups):
            term = _sigmoid(pg_ref[:, n * D_MODEL:(n + 1) * D_MODEL].astype(F32)) * up
            acc = term if acc is None else acc + term
        merged = acc.astype(BF16)
        m_ref[...] = merged
        xn_ref[...] = x_ref[...] + _dot(merged, wo_ref[...], NN)

    yb_spec = pl.BlockSpec((tm, BW), lambda i: (i, 0))
    row = pl.BlockSpec((tm, D_MODEL), lambda i: (i, 0))
    return _host_call(
        body, (ya, yb, ym, pg, wbr, x, w_out),
        [yb_spec, yb_spec, yb_spec, pl.BlockSpec((tm, W_G), lambda i: (i, 0)),
         pl.BlockSpec((3, BW, D_MODEL), lambda i: (0, 0, 0)), row, pl.BlockSpec(w_out.shape, lambda i: (0, 0))],
        (row, row), (SDS((t, D_MODEL), BF16), SDS((t, D_MODEL), F32)),
        name="merge_fwd", grid=(t // tm,), semantics=("parallel",), exchange=exchange)


def _merge_bwd(ya, yb, ym, pg, wbr, dx_out, w_out, exchange=None):
    t = ya.shape[0]
    tm = _tile(t, 512, 16)

    def body(ya_ref, yb_ref, ym_ref, pg_ref, w_ref, dx_ref, wo_ref, dya_ref, dyb_ref, dym_ref, dpg_ref, dw_ref):
        @pl.when(pl.program_id(0) == 0)
        def _():
            dw_ref[...] = jnp.zeros_like(dw_ref)

        dmf = _dot(dx_ref[...].astype(BF16), wo_ref[...], NT)
        branches = ((ya_ref, dya_ref), (yb_ref, dyb_ref), (ym_ref, dym_ref))

        def gate(n):
            gt = _sigmoid(pg_ref[:, n * D_MODEL:(n + 1) * D_MODEL].astype(F32))
            return gt, (dmf * gt).astype(BF16)

        gt, dup = gate(0)
        for n, (y_ref, dy_ref) in enumerate(branches):
            nxt = gate(n + 1) if n + 1 < len(branches) else None
            y, w = y_ref[...], w_ref[n]
            dy_ref[...] = _dot(dup, w, NT)
            dw_ref[n] += _dot(y, dup, TN)
            up = _dot(y, w, NN)
            dpg_ref[:, n * D_MODEL:(n + 1) * D_MODEL] = (dmf * up * gt * (1.0 - gt)).astype(BF16)
            if nxt is not None:
                gt, dup = nxt

    y_spec = pl.BlockSpec((tm, BW), lambda i: (i, 0))
    w_spec = pl.BlockSpec((3, BW, D_MODEL), lambda i: (0, 0, 0))
    return _host_call(
        body, (ya, yb, ym, pg, wbr, dx_out, w_out),
        [y_spec, y_spec, y_spec, pl.BlockSpec((tm, W_G), lambda i: (i, 0)), w_spec,
         pl.BlockSpec((tm, D_MODEL), lambda i: (i, 0)), pl.BlockSpec(w_out.shape, lambda i: (0, 0))],
        (y_spec, y_spec, y_spec, pl.BlockSpec((tm, W_G), lambda i: (i, 0)), w_spec),
        (SDS((t, BW), F32), SDS((t, BW), F32), SDS((t, BW), F32), SDS((t, W_G), BF16), SDS((3, BW, D_MODEL), F32)),
        name="merge_bwd", grid=(t // tm,), semantics=("arbitrary",), exchange=exchange)


def _loss_head(x, g, target):
    t, d = x.shape
    tm = _tile(t, 512, 8)

    def body(x_ref, g_ref, t_ref, loss_ref, dx_ref, dg_ref):
        @pl.when(pl.program_id(0) == 0)
        def _():
            loss_ref[...] = jnp.zeros_like(loss_ref)
            dg_ref[...] = jnp.zeros_like(dg_ref)

        xf = x_ref[...]
        r = lax.rsqrt(jnp.mean(xf * xf, axis=-1, keepdims=True) + EPS)
        xh = xf * r
        err = xh * g_ref[...] - t_ref[...]
        per_tok = jnp.mean(err * err, axis=-1, keepdims=True)
        loss_ref[...] += 0.5 * jnp.sum(per_tok, axis=0, keepdims=True)
        dy = err * (1.0 / d)
        gd = dy * g_ref[...]
        dx_ref[...] = r * (gd - xh * jnp.mean(gd * xh, axis=-1, keepdims=True))
        dg_ref[...] += jnp.sum(dy * xh, axis=0, keepdims=True)

    row = pl.BlockSpec((tm, d), lambda i: (i, 0))
    vec = pl.BlockSpec((1, d), lambda i: (0, 0))
    return pl.pallas_call(
        body, name="loss_head", grid=(t // tm,),
        in_specs=[row, vec, row],
        out_specs=(pl.BlockSpec((1, 1), lambda i: (0, 0)), row, vec),
        out_shape=(SDS((1, 1), F32), SDS((t, d), F32), SDS((1, d), F32)),
        compiler_params=_params("arbitrary"),
    )(x, g, target)


def _layer_fwd(x, mem, w, tabs, next_shards=None):
    cs, sn = tabs
    riding = next_shards is not None
    memn = _rmsnorm_fwd(mem, w["mem_g"], "mem_rmsnorm_fwd")
    kv = _mm(memn, w["w_kv"], "nn", "mem_kv")
    h, pa, pb, pm, pg, q, qT, k, kT, vT, yb, ym = _norm_in_proj(
        x, w["norm_g"], w["w_in_t"], cs, sn, w["qg2"], w["kg2"], w["ln_g"], w["ln_b"], w["w_s"], w["b_s"], kv)
    (o, lse, ya), gathered = _attn_fwd(qT, k, vT, pa, exchange=_gather_first_hop(next_shards) if riding else None)
    (merged, x_next), gathered = _merge_fwd(ya, yb, ym, pg, w["w_br"], x, w["w_out"],
                                            exchange=_gather_forward(gathered) if riding else None)
    saved = dict(x=x, h=h, pa=pa, pb=pb, pm=pm, pg=pg, q=q, qT=qT, k=k, kT=kT, vT=vT, o=o, lse=lse, ya=ya, yb=yb, ym=ym,
                 memn=memn, kv=kv, merged=merged)
    return x_next, saved, gathered


def _layer_bwd(dx_out, mem, w, s, tabs, pending=None, core=None, scatter_own=False):
    cs, sn = tabs
    t = dx_out.shape[0]
    riding = pending is not None
    by_owner = lambda a: a.reshape((N_DEV // 2, 2, -1) + a.shape[-2:])
    row_shards = lambda a: by_owner(a.reshape(N_DEV, a.shape[0] // N_DEV, a.shape[1]))
    d_w_out = _mm_tn(s["merged"], dx_out, "d_w_out")
    (dya, dyb, dym, dpg, d_w_br), recv = _merge_bwd(s["ya"], s["yb"], s["ym"], s["pg"], w["w_br"], dx_out, w["w_out"],
                                                    exchange=_scatter_to_sibling(pending) if riding else None)
    parts = _pair_sums(BIG, pending, recv, core) if riding else None
    grads = dict(w_br=by_owner(_split_w_br(d_w_br)), w_out=row_shards(d_w_out))
    early = [grads[n] for n in BIG[2:]]
    grp = A_HEADS // A_KV_HEADS
    lse_cols = s["lse"].reshape(A_KV_HEADS, grp, t).transpose(0, 2, 1)
    (dq, dkT, dvT), landed = _attn_bwd(
        s["q"], s["qT"], s["k"], s["kT"], s["vT"], s["pa"], s["o"], lse_cols, dya,
        exchange=_both(_scatter_to_chips(parts) if riding else None, _scatter_to_sibling(early) if scatter_own else None))
    from_chips, early_recv = (landed[:len(BIG)], landed[len(BIG):]) if riding else (None, landed)
    early_parts = _pair_sums(BIG[2:], early, early_recv, core) if scatter_own else None
    (dpa, d_qg2, d_kg2), early_from_chips = _attn_prep_bwd(
        s["pa"], cs, sn, w["qg2"], w["kg2"], dq, dkT.reshape(-1, t), dvT.reshape(-1, t), dya, s["o"],
        exchange=_scatter_to_chips(early_parts) if scatter_own else None)
    dpb, d_w_s, d_b_s, d_ln_g, d_ln_b = _gmlp_bwd(s["pb"], w["ln_g"], w["ln_b"], w["w_s"], w["b_s"], dyb)
    dpm, dkv = _mem_attn_bwd(s["pm"], s["kv"], dym)
    d_w_kv = _mm_tn(s["memn"], dkv, "d_w_kv")
    dmemn = _mm(dkv, w["w_kv"], "nt", "d_memn")
    d_mem_g = _rmsnorm_dg(mem, w["mem_g"], dmemn, "mem_rmsnorm_bwd")
    dps = (dpa, dpb, dpm, dpg)
    grads.update(w_in=row_shards(_d_w_in_t(dps, s["h"])), w_mem_kv=row_shards(d_w_kv))
    own = None
    if scatter_own:
        late = [grads[n] for n in BIG[:2]]
        late_parts = _pair_sums(BIG[:2], late, _exchange_call(_scatter_to_sibling(late), "rs_sibling_swap"), core)
        own = _scatter_to_chips(late_parts)
    (dx_in, d_norm_g), late_from_chips = _dh_rmsnorm_bwd(dps, w["w_in_t"], s["x"], w["norm_g"], dx_out, exchange=own)
    if scatter_own:
        own_parts, own_from_chips = late_parts + early_parts, tuple(late_from_chips) + tuple(early_from_chips)
    grads.update(norm_g=d_norm_g[0], q_norm_g=d_qg2[0, :HEAD_DIM] + d_qg2[0, HEAD_DIM:],
                 k_norm_g=d_kg2[0, :HEAD_DIM] + d_kg2[0, HEAD_DIM:], sg_ln_g=d_ln_g[0], sg_ln_b=d_ln_b[0],
                 w_s=d_w_s, b_s=d_b_s[:, :, 0], mem_norm_g=d_mem_g[0])
    return (dx_in, grads, ((parts, from_chips) if riding else None),
            ((own_parts, own_from_chips) if scatter_own else None))


def _layer_weights(l, w_in_t, w_kv, w_br, w_out, small):
    tile2 = lambda g: jnp.tile(g.reshape(1, -1), (1, 2))
    return dict(
        w_in_t=w_in_t, w_kv=w_kv, w_br=w_br, w_out=w_out,
        norm_g=small["norm_g"][l].reshape(1, -1), qg2=tile2(small["q_norm_g"][l]), kg2=tile2(small["k_norm_g"][l]),
        ln_g=small["sg_ln_g"][l].reshape(1, -1), ln_b=small["sg_ln_b"][l].reshape(1, -1),
        w_s=small["w_s"][l], b_s=small["b_s"][l][:, :, None], mem_g=small["mem_norm_g"][l].reshape(1, -1))


def _position():
    x, y, c = lax.axis_index("x"), lax.axis_index("y"), lax.axis_index("c")
    return x, y, c, [(1 - x, y), (x, 1 - y), (1 - x, 1 - y)]


def _gather_first_hop(shards):
    n = len(shards)

    def make(x_refs, out_refs, sems):
        send_sems, recv_sems, local_sems = sems
        x, y, c, chips = _position()
        me = 4 * x + 2 * y + c
        peers = [(x, y, 1 - c)] + [(cx, cy, c) for cx, cy in chips]
        copies = [pltpu.make_async_copy(x_refs[t], out_refs[t].at[me], local_sems.at[t]) for t in range(n)]
        copies += [pltpu.make_async_remote_copy(
            src_ref=x_refs[t], dst_ref=out_refs[t].at[me], send_sem=send_sems.at[t, k], recv_sem=recv_sems.at[t, k],
            device_id=peer, device_id_type=MESH_ID) for t in range(n) for k, peer in enumerate(peers)]
        return copies

    return _Exchange(shards, [SDS((N_DEV,) + a.shape, a.dtype) for a in shards],
                     [pltpu.SemaphoreType.DMA((n, 4)), pltpu.SemaphoreType.DMA((n, 4)), pltpu.SemaphoreType.DMA((n,))], make)


def _gather_forward(gathered):
    n = len(gathered)

    def make(in_refs, out_refs, sems):
        send_sems, recv_sems = sems
        x, y, c, chips = _position()
        return [pltpu.make_async_remote_copy(
            src_ref=in_refs[t].at[4 * cx + 2 * cy + c], dst_ref=out_refs[t].at[4 * cx + 2 * cy + c],
            send_sem=send_sems.at[t, j], recv_sem=recv_sems.at[t, j], device_id=(x, y, 1 - c), device_id_type=MESH_ID)
            for t in range(n) for j, (cx, cy) in enumerate(chips)]

    return _Exchange(gathered, [SDS(a.shape, a.dtype) for a in gathered],
                     [pltpu.SemaphoreType.DMA((n, 3)), pltpu.SemaphoreType.DMA((n, 3))], make,
                     aliases={t: t for t in range(n)})


def _all_gather(shards, name):
    return _exchange_call(_gather_forward(_exchange_call(_gather_first_hop(shards), name + "_hop1")), name + "_hop2")


def _assemble_w_br(gathered):
    _, nb, rows, shard = gathered.shape

    def body(g_ref, o_ref):
        for d in range(N_DEV):
            o_ref[:, :, d * shard:(d + 1) * shard] = g_ref[d]

    return pl.pallas_call(
        body, name="assemble_w_br", out_shape=SDS((nb, rows, N_DEV * shard), gathered.dtype),
        compiler_params=pltpu.CompilerParams(vmem_limit_bytes=VMEM_LIMIT),
    )(gathered)


def _split_w_br(dw):
    nb, rows, cols = dw.shape
    shard = cols // N_DEV

    def body(d_ref, o_ref):
        for d in range(N_DEV):
            o_ref[d] = d_ref[:, :, d * shard:(d + 1) * shard].astype(BF16)

    return pl.pallas_call(
        body, name="split_w_br", out_shape=SDS((N_DEV, nb, rows, shard), BF16),
        compiler_params=pltpu.CompilerParams(vmem_limit_bytes=VMEM_LIMIT),
    )(dw)


def _scatter_to_sibling(dests):
    n = len(dests)

    def make(d_refs, recv_refs, sems):
        send_sems, recv_sems = sems
        x, y, c, _ = _position()
        return [pltpu.make_async_remote_copy(
            src_ref=d_refs[t].at[:, 1 - c], dst_ref=recv_refs[t], send_sem=send_sems.at[t],
            recv_sem=recv_sems.at[t], device_id=(x, y, 1 - c), device_id_type=MESH_ID) for t in range(n)]

    return _Exchange(dests, [SDS(a.shape[:1] + a.shape[2:], a.dtype) for a in dests],
                     [pltpu.SemaphoreType.DMA((n,)), pltpu.SemaphoreType.DMA((n,))], make)


def _scatter_to_chips(parts):
    n = len(parts)

    def make(p_refs, recv_refs, sems):
        send_sems, recv_sems = sems
        _, _, c, chips = _position()
        return [pltpu.make_async_remote_copy(
            src_ref=p_refs[t].at[2 * cx + cy], dst_ref=recv_refs[t].at[k], send_sem=send_sems.at[t, k],
            recv_sem=recv_sems.at[t, k], device_id=(cx, cy, c), device_id_type=MESH_ID)
            for t in range(n) for k, (cx, cy) in enumerate(chips)]

    return _Exchange(parts, [SDS((3,) + a.shape[1:], a.dtype) for a in parts],
                     [pltpu.SemaphoreType.DMA((n, 3)), pltpu.SemaphoreType.DMA((n, 3))], make)


def _pair_sums(names, dests, recv, core):
    return [_pair_sum(d, r, core, "rs_pair_sum_" + n) for n, d, r in zip(names, dests, recv)]


def _pair_sum(dest, recv, core, name):
    _, _, na, r, cdim = dest.shape

    def body(core_ref, a_ref, b_ref, o_ref):
        o_ref[...] = (a_ref[...].astype(F32) + b_ref[...].astype(F32)).astype(o_ref.dtype)

    blk = pl.BlockSpec((None, na, r, cdim), lambda j, core_ref: (j, 0, 0, 0))
    return pl.pallas_call(
        body, name=name, out_shape=SDS(recv.shape, BF16),
        grid_spec=pltpu.PrefetchScalarGridSpec(
            num_scalar_prefetch=1, grid=(4,),
            in_specs=[pl.BlockSpec((None, None, na, r, cdim), lambda j, core_ref: (j, core_ref[0], 0, 0, 0)), blk],
            out_specs=blk),
        compiler_params=_params("parallel"),
    )(core, dest, recv)


def _adamw_math(w, g, m, v):
    m = ADAM_B1 * m + (1.0 - ADAM_B1) * g
    v = ADAM_B2 * v + (1.0 - ADAM_B2) * (g * g)
    m_hat = m / (1.0 - ADAM_B1 ** ADAM_STEP)
    v_hat = v / (1.0 - ADAM_B2 ** ADAM_STEP)
    delta = -ADAM_LR * (m_hat / (jnp.sqrt(v_hat) + ADAM_EPS) + ADAM_WD * w)
    return delta, m, v


def _sum_and_adamw(parts, w, m, v, name):
    n, r, ln = parts.shape
    tr = _tile(r, 512, 16)

    def body(p_ref, w_ref, m_ref, v_ref, g_out, d_out, m_out, v_out):
        g = p_ref[0].astype(F32)
        for j in range(1, n):
            g = g + p_ref[j].astype(F32)
        delta, nm, nv = _adamw_math(w_ref[...], g, m_ref[...], v_ref[...])
        g_out[...] = g
        d_out[...] = delta
        m_out[...] = nm
        v_out[...] = nv

    blk = pl.BlockSpec((tr, ln), lambda i: (i, 0))
    return pl.pallas_call(
        body, name=name, grid=(r // tr,),
        in_specs=[pl.BlockSpec((n, tr, ln), lambda i: (0, i, 0)), blk, blk, blk],
        out_specs=(blk, blk, blk, blk), out_shape=tuple(SDS((r, ln), F32) for _ in range(4)),
        compiler_params=_params("parallel"),
    )(parts, w, m, v)


BIG = ("w_in", "w_mem_kv", "w_br", "w_out")
SMALL = ("norm_g", "q_norm_g", "k_norm_g", "sg_ln_g", "sg_ln_b", "w_s", "b_s", "mem_norm_g", "final_g")


def _pack(arrs, row_unit=16):
    flat = jnp.concatenate([a.reshape(-1) for a in arrs])
    pad = (-flat.shape[0]) % (row_unit * LANES)
    if pad:
        flat = jnp.concatenate([flat, jnp.zeros((pad,), flat.dtype)])
    return flat.reshape(-1, LANES)


def _unpack(buf, shapes):
    flat = buf.reshape(-1)
    out, off = [], 0
    for shp in shapes:
        n = 1
        for s_ in shp:
            n *= s_
        out.append(flat[off:off + n].reshape(shp))
        off += n
    return out


def _shard_sum_adamw(part, from_chips, chip, w, m, v, layer, prev, name):
    _, na, r, cdim = part.shape
    flat = w.ndim == 3
    assert not flat or na == 1
    tr = _tile(r, max(8, (256 * 1024) // (na * cdim)), 8)
    n_prev = 0 if prev is None else len(prev)

    def body(chip_ref, p_ref, f_ref, w_ref, m_ref, v_ref, *rest):
        g_out, d_out, m_out, v_out = rest[n_prev:]
        g = p_ref[...].astype(F32)
        for j in range(3):
            g = g + f_ref[j].astype(F32)
        delta, nm, nv = _adamw_math(w_ref[...], g, m_ref[...], v_ref[...])
        g_out[...] = g
        d_out[...] = delta
        m_out[...] = nm
        v_out[...] = nv

    a_blk = None if flat else na
    if flat:
        lay = pl.BlockSpec((None, tr, cdim), lambda i, chip_ref: (layer, i, 0))
    else:
        lay = pl.BlockSpec((None, na, tr, cdim), lambda i, chip_ref: (layer, 0, i, 0))
    return pl.pallas_call(
        body, name=name, out_shape=tuple(SDS(w.shape, F32) for _ in range(4)),
        grid_spec=pltpu.PrefetchScalarGridSpec(
            num_scalar_prefetch=1, grid=(r // tr,),
            in_specs=[pl.BlockSpec((None, a_blk, tr, cdim), lambda i, chip_ref: (chip_ref[0], 0, i, 0)),
                      pl.BlockSpec((3, a_blk, tr, cdim), lambda i, chip_ref: (0, 0, i, 0)), lay, lay, lay]
            + [HBM] * n_prev,
            out_specs=(lay, lay, lay, lay)),
        input_output_aliases={6 + j: j for j in range(n_prev)},
        compiler_params=_params("parallel"),
    )(chip, part, from_chips, w, m, v, *(prev or ()))


def kernel(x, mem, norm_g, w_in, q_norm_g, k_norm_g, sg_ln_g, sg_ln_b, w_s, b_s, mem_norm_g, w_mem_kv, w_br, w_out, final_g, loss_target, m_norm_g, m_w_in, m_q_norm_g, m_k_norm_g, m_sg_ln_g, m_sg_ln_b, m_w_s, m_b_s, m_mem_norm_g, m_w_mem_kv, m_w_br, m_w_out, m_final_g, v_norm_g, v_w_in, v_q_norm_g, v_k_norm_g, v_sg_ln_g, v_sg_ln_b, v_w_s, v_b_s, v_mem_norm_g, v_w_mem_kv, v_w_br, v_w_out, v_final_g):
    wts = dict(norm_g=norm_g, w_in=w_in, q_norm_g=q_norm_g, k_norm_g=k_norm_g, sg_ln_g=sg_ln_g, sg_ln_b=sg_ln_b,
               w_s=w_s, b_s=b_s, mem_norm_g=mem_norm_g, w_mem_kv=w_mem_kv, w_br=w_br, w_out=w_out, final_g=final_g)
    mom1 = dict(norm_g=m_norm_g, w_in=m_w_in, q_norm_g=m_q_norm_g, k_norm_g=m_k_norm_g, sg_ln_g=m_sg_ln_g,
                sg_ln_b=m_sg_ln_b, w_s=m_w_s, b_s=m_b_s, mem_norm_g=m_mem_norm_g, w_mem_kv=m_w_mem_kv, w_br=m_w_br,
                w_out=m_w_out, final_g=m_final_g)
    mom2 = dict(norm_g=v_norm_g, w_in=v_w_in, q_norm_g=v_q_norm_g, k_norm_g=v_k_norm_g, sg_ln_g=v_sg_ln_g,
                sg_ln_b=v_sg_ln_b, w_s=v_w_s, b_s=v_b_s, mem_norm_g=v_mem_norm_g, w_mem_kv=v_w_mem_kv, w_br=v_w_br,
                w_out=v_w_out, final_g=v_final_g)
    dp = w_in.shape[0]
    core = lax.axis_index("c").astype(jnp.int32).reshape(1)
    chip = (2 * lax.axis_index("x") + lax.axis_index("y")).astype(jnp.int32).reshape(1)

    t_in = lambda a: jnp.swapaxes(a, 1, 2)
    wts, mom1, mom2 = [dict(d_, w_in=t_in(d_["w_in"])) for d_ in (wts, mom1, mom2)]

    shard_bf = {n: wts[n].astype(BF16) for n in BIG}
    shards = lambda l: [shard_bf[n][l] for n in BIG]
    x_l, mem_l = x[0], mem[0]
    tabs = _rope_tables(x_l.shape[0])

    gathered = _all_gather(shards(0), "weights_all_gather")
    layers, saved = [], []
    for l in range(dp):
        g_in, g_kv, g_br, g_out = gathered
        layers.append(_layer_weights(l, g_in.reshape(IN_WIDTH, -1), g_kv.reshape(D_MODEL, -1), _assemble_w_br(g_br),
                                     g_out.reshape(D_MODEL, -1), wts))
        x_l, s, gathered = _layer_fwd(x_l, mem_l, layers[l], tabs, next_shards=shards(l + 1) if l + 1 < dp else None)
        saved.append(s)
    loss_local, dx, d_final_g = _loss_head(x_l, final_g.reshape(1, -1), loss_target[0])
    loss = lax.psum(loss_local[0, 0], AXES)

    def finish(l, parts, from_chips, prev):
        return {n: _shard_sum_adamw(p, f, chip, wts[n], mom1[n], mom2[n], l, None if prev is None else prev[n],
                                    "sum_adamw_" + n)
                for n, p, f in zip(BIG, parts, from_chips)}

    grads, updated, pending = [None] * dp, None, None
    for l in reversed(range(dp)):
        dx, grads[l], scattered, own = _layer_bwd(dx, mem_l, layers[l], saved[l], tabs, pending=pending, core=core,
                                                  scatter_own=(l == 0))
        if scattered is not None:
            updated = finish(l + 1, *scattered, updated)
        pending = [grads[l][n] for n in BIG]
    updated = finish(0, *own, updated)
    grad_x = dx
    big_out = [{n: (t_in(updated[n][k]) if n == "w_in" else updated[n][k]) for n in BIG} for k in range(4)]

    small_g = {n: jnp.stack([g[n] for g in grads]) for n in SMALL if n != "final_g"}
    small_g["final_g"] = d_final_g
    (all_small,) = _all_gather([_pack([small_g[n] for n in SMALL])], "small_all_gather")
    small_bufs = _sum_and_adamw(
        all_small, _pack([wts[n] for n in SMALL]), _pack([mom1[n] for n in SMALL]), _pack([mom2[n] for n in SMALL]),
        "small_sum_adamw")

    outs = []
    for big_vals, small_buf in zip(big_out, small_bufs):
        vals = dict(big_vals)
        vals.update(zip(SMALL, _unpack(small_buf, [wts[n].shape for n in SMALL])))
        outs.append(vals)
    order = ("norm_g", "w_in", "q_norm_g", "k_norm_g", "sg_ln_g", "sg_ln_b", "w_s", "b_s", "mem_norm_g", "w_mem_kv",
             "w_br", "w_out", "final_g")
    result = [loss, grad_x[None]]
    for vals in outs:
        result += [vals[n] for n in order]
    return tuple(result)
```

```python
import functools

import jax
import jax.numpy as jnp
from jax import lax
from jax.experimental import pallas as pl
from jax.experimental.pallas import tpu as pltpu

F32 = jnp.float32
BF16 = jnp.bfloat16
SDS = jax.ShapeDtypeStruct
MESH_ID = pl.DeviceIdType.MESH
AXES = ("x", "y", "c")
N_DEV = 8

D_MODEL = 1024
DEPTH = 4
GRID_W = 64
CHUNK = 128
ROPE_THETA = 10000.0
EPS = 1e-6
HEAD_DIM = 64
A_HEADS = 8
A_KV_HEADS = 2
B_GROUPS = 4
M_HEADS = 4
M_HEAD_DIM = 128
BW = 512
W_A = 1280
W_B = 1536
W_M = 1024
W_G = 3072
IN_WIDTH = W_A + W_B + W_M + W_G
GROUP_OFFS = (0, W_A, W_A + W_B, W_A + W_B + W_M, IN_WIDTH)

ADAM_LR = 0.001
ADAM_B1 = 0.9
ADAM_B2 = 0.999
ADAM_EPS = 1e-08
ADAM_WD = 0.01
ADAM_STEP = 10

LANES = 128
KEY_CHUNK = 2048
VMEM_LIMIT = 52 * 1024 * 1024


def _tile(n, cap, unit=LANES):
    if n <= cap:
        return n
    t = (cap // unit) * unit
    while n % t:
        t -= unit
    return t


def _params(*sem):
    return pltpu.CompilerParams(dimension_semantics=sem, vmem_limit_bytes=VMEM_LIMIT)


def _sigmoid(z):
    return 0.5 * jnp.tanh(0.5 * z) + 0.5


def _silu_and_grad(z):
    s = _sigmoid(z)
    return z * s, s * (1.0 + z * (1.0 - s))


def _dot(a, b, dims):
    return lax.dot_general(a, b, (dims, ((), ())), preferred_element_type=F32)


NN = ((1,), (0,))
NT = ((1,), (1,))
TN = ((0,), (0,))
HBM = pl.BlockSpec(memory_space=pl.ANY)


class _Exchange:
    def __init__(self, ins, out_shapes, sems, make, aliases=None):
        self.ins, self.out_shapes, self.sems, self.make = list(ins), list(out_shapes), list(sems), make
        self.aliases = dict(aliases or {})

    def start(self, in_refs, out_refs, sems):
        for cp in self.make(in_refs, out_refs, sems):
            cp.start()

    def finish(self, in_refs, out_refs, sems):
        for cp in self.make(in_refs, out_refs, sems):
            cp.wait()


def _both(a, b):
    if a is None or b is None:
        return a if b is None else b
    n_in, n_out, n_sem = len(a.ins), len(a.out_shapes), len(a.sems)

    def make(in_refs, out_refs, sems):
        return (a.make(in_refs[:n_in], out_refs[:n_out], sems[:n_sem])
                + b.make(in_refs[n_in:], out_refs[n_out:], sems[n_sem:]))

    aliases = dict(a.aliases)
    aliases.update({n_in + i: n_out + o for i, o in b.aliases.items()})
    return _Exchange(a.ins + b.ins, a.out_shapes + b.out_shapes, a.sems + b.sems, make, aliases)


def _host_call(body, ins, in_specs, out_specs, out_shape, *, name, grid, semantics, scratch=(), exchange=None):
    ins, out_shape, scratch = list(ins), list(out_shape), list(scratch)
    if exchange is None:
        res = pl.pallas_call(
            body, name=name, grid=grid, in_specs=list(in_specs), out_specs=tuple(out_specs), out_shape=tuple(out_shape),
            scratch_shapes=scratch, compiler_params=_params(*semantics))(*ins)
        return tuple(res), ()
    n_in, n_out, n_scr = len(ins), len(out_shape), len(scratch)
    x_in, x_out = len(exchange.ins), len(exchange.out_shapes)

    def carrying(*refs):
        o0 = n_in + x_in
        s0 = o0 + n_out + x_out
        c_in, c_out, c_sems = refs[n_in:o0], refs[o0 + n_out:s0], refs[s0 + n_scr:]
        ids = [pl.program_id(a) for a in range(len(grid))]
        first = functools.reduce(jnp.logical_and, [i == 0 for i in ids])
        last = functools.reduce(jnp.logical_and, [i == g - 1 for i, g in zip(ids, grid)])

        @pl.when(first)
        def _():
            exchange.start(c_in, c_out, c_sems)

        body(*refs[:n_in], *refs[o0:o0 + n_out], *refs[s0:s0 + n_scr])

        @pl.when(last)
        def _():
            exchange.finish(c_in, c_out, c_sems)

    res = pl.pallas_call(
        carrying, name=name, grid=grid, in_specs=list(in_specs) + [HBM] * x_in,
        out_specs=tuple(out_specs) + (HBM,) * x_out, out_shape=tuple(out_shape) + tuple(exchange.out_shapes),
        scratch_shapes=scratch + exchange.sems,
        input_output_aliases={n_in + i: n_out + o for i, o in exchange.aliases.items()},
        compiler_params=_params(*(["arbitrary"] * len(grid))))(*ins, *exchange.ins)
    return tuple(res[:n_out]), tuple(res[n_out:])


def _exchange_call(exchange, name):
    x_in = len(exchange.ins)

    def body(*refs):
        x_out = len(exchange.out_shapes)
        c_in, c_out, c_sems = refs[:x_in], refs[x_in:x_in + x_out], refs[x_in + x_out:]
        exchange.start(c_in, c_out, c_sems)
        exchange.finish(c_in, c_out, c_sems)

    return pl.pallas_call(
        body, name=name, in_specs=[HBM] * x_in, out_specs=tuple([HBM] * len(exchange.out_shapes)),
        out_shape=tuple(exchange.out_shapes), scratch_shapes=exchange.sems, input_output_aliases=exchange.aliases,
    )(*exchange.ins)


def _mm(a, b, mode, name):
    (m, k), (n, k2) = a.shape, (b.shape[::-1] if mode == "nn" else b.shape)
    assert k == k2 and mode in ("nn", "nt"), (a.shape, b.shape, mode)
    dims = {"nn": NN, "nt": NT}[mode]

    def body(a_ref, b_ref, o_ref):
        o_ref[...] = _dot(a_ref[...].astype(BF16), b_ref[...].astype(BF16), dims)

    return pl.pallas_call(body, name=name, out_shape=SDS((m, n), F32),
                          compiler_params=pltpu.CompilerParams(vmem_limit_bytes=VMEM_LIMIT))(a, b)


def _mm_tn(a, b, name):
    (k, m), (k2, n) = a.shape, b.shape
    assert k == k2, (a.shape, b.shape)
    tk = _tile(k, 2048, 16)
    nk = k // tk

    def body(a_ref, b_ref, o_ref, acc):
        kk = pl.program_id(0)
        part = _dot(a_ref[...].astype(BF16), b_ref[...].astype(BF16), TN)

        @pl.when(kk == 0)
        def _():
            acc[...] = part

        @pl.when(kk > 0)
        def _():
            acc[...] += part

        @pl.when(kk == nk - 1)
        def _():
            o_ref[...] = acc[...].astype(BF16)

    return pl.pallas_call(
        body, name=name, grid=(nk,),
        in_specs=[pl.BlockSpec((tk, m), lambda kk: (kk, 0)), pl.BlockSpec((tk, n), lambda kk: (kk, 0))],
        out_specs=pl.BlockSpec((m, n), lambda kk: (0, 0)), out_shape=SDS((m, n), BF16),
        scratch_shapes=[pltpu.VMEM((m, n), F32)], compiler_params=_params("arbitrary"),
    )(a, b)


def _rmsnorm_fwd(x, g, name):
    t, d = x.shape
    tm = _tile(t, 512, 8)

    def body(x_ref, g_ref, h_ref):
        xf = x_ref[...]
        r = lax.rsqrt(jnp.mean(xf * xf, axis=-1, keepdims=True) + EPS)
        h_ref[...] = (xf * r * g_ref[...]).astype(BF16)

    return pl.pallas_call(
        body, name=name, grid=(t // tm,),
        in_specs=[pl.BlockSpec((tm, d), lambda i: (i, 0)), pl.BlockSpec((1, d), lambda i: (0, 0))],
        out_specs=pl.BlockSpec((tm, d), lambda i: (i, 0)),
        out_shape=SDS((t, d), BF16), compiler_params=_params("parallel"),
    )(x, g)


def _rmsnorm_bwd_math(xf, g, dh):
    r = lax.rsqrt(jnp.mean(xf * xf, axis=-1, keepdims=True) + EPS)
    xh = xf * r
    gd = dh * g
    dx = r * (gd - xh * jnp.mean(gd * xh, axis=-1, keepdims=True))
    return dx, jnp.sum(dh * xh, axis=0, keepdims=True)


def _rmsnorm_dg(x, g, dh, name):
    t, d = x.shape
    tm = _tile(t, 512, 8)

    def body(x_ref, g_ref, dh_ref, dg_ref):
        @pl.when(pl.program_id(0) == 0)
        def _():
            dg_ref[...] = jnp.zeros_like(dg_ref)

        dg_ref[...] += _rmsnorm_bwd_math(x_ref[...], g_ref[...], dh_ref[...])[1]

    row = pl.BlockSpec((tm, d), lambda i: (i, 0))
    vec = pl.BlockSpec((1, d), lambda i: (0, 0))
    return pl.pallas_call(
        body, name=name, grid=(t // tm,), in_specs=[row, vec, row], out_specs=vec, out_shape=SDS((1, d), F32),
        compiler_params=_params("arbitrary"),
    )(x, g, dh)


def _group_rows(p):
    return slice(GROUP_OFFS[p], GROUP_OFFS[p + 1])


def _d_w_in_t(dps, h):
    t, d = h.shape
    tm = 256
    assert all(off % tm == 0 for off in GROUP_OFFS)
    tiles = [a.shape[1] // tm for a in dps]
    starts = [sum(tiles[:p]) for p in range(len(dps))]

    def body(*refs):
        a_refs, h_ref, o_ref = refs[:len(dps)], refs[len(dps)], refs[len(dps) + 1]
        i = pl.program_id(0)
        for p, a_ref in enumerate(a_refs):
            @pl.when((i >= starts[p]) & (i < starts[p] + tiles[p]))
            def _():
                o_ref[...] = _dot(a_ref[...], h_ref[...], TN).astype(BF16)

    def tile_of(p):
        return lambda i: jnp.clip(i - starts[p], 0, tiles[p] - 1)

    return pl.pallas_call(
        body, name="d_w_in", grid=(sum(tiles),),
        in_specs=[pl.BlockSpec((t, tm), lambda i, p=p: (0, tile_of(p)(i))) for p in range(len(dps))]
        + [pl.BlockSpec((t, d), lambda i: (0, 0), pipeline_mode=pl.Buffered(1))],
        out_specs=pl.BlockSpec((tm, d), lambda i: (i, 0)),
        out_shape=SDS((sum(tiles) * tm, d), BF16), compiler_params=_params("arbitrary"),
    )(*dps, h)


def _norm_in_proj(x, g, w_t, cs, sn, qg2, kg2, lng, lnb, ws, bs, kv):
    t, d = x.shape
    tm = _tile(t, 512, LANES)
    widths = [GROUP_OFFS[p + 1] - GROUP_OFFS[p] for p in range(4)]
    scale = HEAD_DIM ** -0.5

    def body(x_ref, g_ref, w_ref, cs_ref, sn_ref, qg_ref, kg_ref, lng_ref, lnb_ref, ws_ref, bs_ref, kv_ref,
             h_ref, pa_ref, pb_ref, pm_ref, pg_ref, q_ref, qT_ref, k_ref, kT_ref, vT_ref, yb_ref, ym_ref):
        xf = x_ref[...]
        r = lax.rsqrt(jnp.mean(xf * xf, axis=-1, keepdims=True) + EPS)
        hb = (xf * r * g_ref[...]).astype(BF16)
        h_ref[...] = hb
        groups = [_dot(hb, w_ref[_group_rows(p), :], NT).astype(BF16) for p in range(4)]
        for o_ref, val in zip((pa_ref, pb_ref, pm_ref, pg_ref), groups):
            o_ref[...] = val
        pa = groups[0]
        lane = lax.broadcasted_iota(jnp.int32, (tm, LANES), 1)
        c, s = cs_ref[...], sn_ref[...]

        def norm_rope(xg, g2):
            rr = lax.rsqrt(_head_sums(xg * xg, lane) * (1.0 / HEAD_DIM) + EPS)
            xn = xg * rr * g2
            return xn * c + _swap16(xn, lane) * s

        for gi in range(4):
            sl = slice(gi * LANES, (gi + 1) * LANES)
            qr = norm_rope(pa[:, sl].astype(F32), qg_ref[...]) * scale
            q_ref[:, sl] = qr.astype(BF16)
            qT_ref[sl, :] = qr.T.astype(BF16)
        kr = norm_rope(pa[:, 512:640].astype(F32), kg_ref[...])
        kT_ref[...] = kr.T.astype(BF16)
        vT_ref[...] = pa[:, 640:768].astype(F32).T.astype(BF16)
        kr = kr.astype(BF16)
        for kvh in range(A_KV_HEADS):
            k_ref[kvh] = kr[:, kvh * HEAD_DIM:(kvh + 1) * HEAD_DIM]
        _gmlp_tile(groups[1], lng_ref, lnb_ref, ws_ref, bs_ref, yb_ref)
        _mem_attn_tile(groups[2], kv_ref, ym_ref)

    row = lambda wd: pl.BlockSpec((tm, wd), lambda i: (i, 0))
    col = lambda rws: pl.BlockSpec((rws, tm), lambda i: (0, i))
    whole = lambda a: pl.BlockSpec(a.shape, lambda i: (0,) * a.ndim)
    hm = pl.BlockSpec((A_KV_HEADS, tm, HEAD_DIM), lambda i: (0, i, 0))
    return pl.pallas_call(
        body, name="norm_in_proj", grid=(t // tm,),
        in_specs=[row(d), whole(g), pl.BlockSpec(w_t.shape, lambda i: (0, 0), pipeline_mode=pl.Buffered(1)),
                  row(LANES), row(LANES)] + [whole(a) for a in (qg2, kg2, lng, lnb, ws, bs, kv)],
        out_specs=(row(d),) + tuple(row(wd) for wd in widths)
        + (row(BW), col(BW), hm, col(LANES), col(LANES), row(BW), row(BW)),
        out_shape=(SDS((t, d), BF16),) + tuple(SDS((t, wd), BF16) for wd in widths)
        + (SDS((t, BW), BF16), SDS((BW, t), BF16), SDS((A_KV_HEADS, t, HEAD_DIM), BF16),
           SDS((LANES, t), BF16), SDS((LANES, t), BF16), SDS((t, BW), BF16), SDS((t, BW), BF16)),
        compiler_params=_params("parallel"),
    )(x, g, w_t, cs, sn, qg2, kg2, lng, lnb, ws, bs, kv)


def _dh_rmsnorm_bwd(dps, w_t, x, g, dres, exchange=None):
    t, d = x.shape
    tm = _tile(t, 256, 16)
    n = len(dps)

    def body(*refs):
        dp_refs, w_ref = refs[:n], refs[n]
        x_ref, g_ref, dres_ref, dx_ref, dg_ref = refs[n + 1:]
        dh = None
        for p, dp_ref in enumerate(dp_refs):
            part = _dot(dp_ref[...], w_ref[_group_rows(p), :], NN)
            dh = part if dh is None else dh + part
        dx, dg = _rmsnorm_bwd_math(x_ref[...], g_ref[...], dh)
        dx_ref[...] = dx + dres_ref[...]

        @pl.when(pl.program_id(0) == 0)
        def _():
            dg_ref[...] = jnp.zeros_like(dg_ref)

        dg_ref[...] += dg

    row = pl.BlockSpec((tm, d), lambda i: (i, 0))
    vec = pl.BlockSpec((1, d), lambda i: (0, 0))
    return _host_call(
        body, list(dps) + [w_t, x, g, dres],
        [pl.BlockSpec((tm, a.shape[1]), lambda i: (i, 0)) for a in dps]
        + [pl.BlockSpec(w_t.shape, lambda i: (0, 0), pipeline_mode=pl.Buffered(1)), row, vec, row],
        (row, vec), (SDS((t, d), F32), SDS((1, d), F32)),
        name="dh_rmsnorm_bwd", grid=(t // tm,), semantics=("arbitrary",), exchange=exchange)


def _rope_tables(t):
    rows = t // GRID_W
    row = jnp.repeat(jnp.arange(rows, dtype=F32), GRID_W)
    col = jnp.tile(jnp.arange(GRID_W, dtype=F32), rows)
    n_freq = HEAD_DIM // 4
    inv = ROPE_THETA ** (-jnp.arange(n_freq, dtype=F32) / n_freq)
    ang = jnp.stack([row[:, None] * inv, col[:, None] * inv], axis=1)
    cos, sin = jnp.cos(ang), jnp.sin(ang)
    c64 = jnp.concatenate([cos[:, 0], cos[:, 0], cos[:, 1], cos[:, 1]], axis=-1)
    s64 = jnp.concatenate([-sin[:, 0], sin[:, 0], -sin[:, 1], sin[:, 1]], axis=-1)
    return jnp.tile(c64, (1, 2)), jnp.tile(s64, (1, 2))


def _head_sums(v, lane):
    lo = jnp.sum(jnp.where(lane < HEAD_DIM, v, 0.0), axis=-1, keepdims=True)
    hi = jnp.sum(jnp.where(lane < HEAD_DIM, 0.0, v), axis=-1, keepdims=True)
    return jnp.where(lane < HEAD_DIM, lo, hi)


def _swap16(v, lane):
    return jnp.where((lane % 32) < 16, pltpu.roll(v, LANES - 16, 1), pltpu.roll(v, 16, 1))


def _attn_fwd(qT, k, vT, pa, exchange=None):
    t = qT.shape[1]
    tq = _tile(t, 256, LANES)
    grp = A_HEADS // A_KV_HEADS

    ck = _tile(t, KEY_CHUNK, LANES)

    def body(qT_ref, k_ref, vT_ref, pa_ref, o_ref, lse_ref, y_ref):
        def finish_pair(pair, o_pair):
            o2 = jnp.concatenate(o_pair, axis=0).T
            cols = slice(pair * LANES, (pair + 1) * LANES)
            o_ref[:, cols] = o2
            z = pa_ref[:, 768 + pair * LANES:768 + (pair + 1) * LANES].astype(F32)
            y_ref[:, cols] = (o2 * (z * _sigmoid(z))).astype(BF16)

        n_c = t // ck
        items = [(h, c) for h in range(A_HEADS) for c in range(n_c)]

        def scores(h, c):
            return _dot(k_ref[h // grp, c * ck:(c + 1) * ck, :], qT_ref[h * HEAD_DIM:(h + 1) * HEAD_DIM, :], NN)

        def weighted_values(h, c, pT):
            kvh = h // grp
            return _dot(vT_ref[kvh * HEAD_DIM:(kvh + 1) * HEAD_DIM, c * ck:(c + 1) * ck], pT, NN)

        o_pair, sT, pending, acc = [], scores(*items[0]), None, None
        for i in range(len(items) + 1):
            item = items[i] if i < len(items) else None
            sT_next = scores(*items[i + 1]) if i + 1 < len(items) else None
            if pending is not None:
                (ph, pc, pT, alpha, l_run) = pending
                o_c = weighted_values(ph, pc, pT)
                acc = o_c if alpha is None else acc * alpha + o_c
                if pc == n_c - 1:
                    o_pair.append(acc / l_run)
                    if ph % 2:
                        finish_pair(ph // 2, o_pair)
                        o_pair = []
                pending = None
            if item is not None:
                h, c = item
                m_c = jnp.max(sT, axis=0, keepdims=True)
                if c == 0:
                    m_new, alpha = m_c, None
                else:
                    m_new = jnp.maximum(m, m_c)
                    alpha = jnp.exp(m - m_new)
                pT = jnp.exp(sT - m_new)
                l_c = jnp.sum(pT, axis=0, keepdims=True)
                l = l_c if c == 0 else l * alpha + l_c
                m = m_new
                if c == n_c - 1:
                    lse_ref[h:h + 1, :] = m + jnp.log(l)
                pending = (h, c, pT.astype(BF16), alpha, l)
            sT = sT_next

    row = lambda w: pl.BlockSpec((tq, w), lambda i: (i, 0))
    return _host_call(
        body, (qT, k, vT, pa),
        [pl.BlockSpec((BW, tq), lambda i: (0, i)), pl.BlockSpec((A_KV_HEADS, t, HEAD_DIM), lambda i: (0, 0, 0)),
         pl.BlockSpec((A_KV_HEADS * HEAD_DIM, t), lambda i: (0, 0)), row(W_A)],
        (row(BW), pl.BlockSpec((A_HEADS, tq), lambda i: (0, i)), row(BW)),
        (SDS((t, BW), F32), SDS((A_HEADS, t), F32), SDS((t, BW), BF16)),
        name="attn_fwd", grid=(t // tq,), semantics=("parallel",), exchange=exchange)


def _attn_bwd(q, qT, k, kT, vT, pa, o, lse, dy, exchange=None):
    t = q.shape[0]
    tq = _tile(t, 256, LANES)
    grp = A_HEADS // A_KV_HEADS
    gw = grp * HEAD_DIM

    def body(q_ref, qT_ref, k_ref, kT_ref, vT_ref, z_ref, o_ref, lse_ref, dy_ref, dq_ref, dkT_ref, dvT_ref):
        @pl.when(pl.program_id(1) == 0)
        def _():
            dkT_ref[...] = jnp.zeros_like(dkT_ref)
            dvT_ref[...] = jnp.zeros_like(dvT_ref)

        z = z_ref[...].astype(F32)
        do = dy_ref[...] * (z * _sigmoid(z))
        doo = do * o_ref[...]
        doT = do.T
        kk, kT, vT = k_ref[...], kT_ref[...], vT_ref[...]
        heads = [slice(j * HEAD_DIM, (j + 1) * HEAD_DIM) for j in range(grp)]

        def scores(hs):
            return _dot(q_ref[:, hs], kT, NN), _dot(do[:, hs].astype(BF16), vT, NN)

        s, dp = scores(heads[0])
        for j, hs in enumerate(heads):
            nxt = scores(heads[j + 1]) if j + 1 < grp else None
            delta = jnp.sum(doo[:, hs], axis=-1, keepdims=True)
            p = jnp.exp(s - lse_ref[:, j:j + 1])
            ds_b = (p * (dp - delta)).astype(BF16)
            dq_ref[:, hs] = _dot(ds_b, kk, NN)
            dkT_ref[...] += _dot(qT_ref[hs, :], ds_b, NN)
            dvT_ref[...] += _dot(doT[hs, :].astype(BF16), p.astype(BF16), NN)
            if nxt is not None:
                s, dp = nxt

    grp_blk = pl.BlockSpec((tq, gw), lambda g, i: (i, g))
    kvT_blk = pl.BlockSpec((HEAD_DIM, t), lambda g, i: (g, 0))
    acc_blk = pl.BlockSpec((None, HEAD_DIM, t), lambda g, i: (g, 0, 0))
    return _host_call(
        body, (q, qT, k, kT, vT, pa, o, lse, dy),
        [grp_blk, pl.BlockSpec((gw, tq), lambda g, i: (g, i)), pl.BlockSpec((None, t, HEAD_DIM), lambda g, i: (g, 0, 0)),
         kvT_blk, kvT_blk, pl.BlockSpec((tq, gw), lambda g, i: (i, 768 // gw + g)), grp_blk,
         pl.BlockSpec((None, tq, grp), lambda g, i: (g, i, 0)), grp_blk],
        (grp_blk, acc_blk, acc_blk),
        (SDS((t, BW), F32), SDS((A_KV_HEADS, HEAD_DIM, t), F32), SDS((A_KV_HEADS, HEAD_DIM, t), F32)),
        name="attn_bwd", grid=(A_KV_HEADS, t // tq), semantics=("arbitrary", "arbitrary"), exchange=exchange)


def _attn_prep_bwd(pa, cs, sn, qg2, kg2, dq, dkT, dvT, dy, o, exchange=None):
    t = pa.shape[0]
    tq = _tile(t, 512, LANES)
    scale = HEAD_DIM ** -0.5

    def body(pa_ref, cs_ref, sn_ref, qg_ref, kg_ref, dq_ref, dkT_ref, dvT_ref, dy_ref, o_ref, dpa_ref, dqg_ref, dkg_ref):
        lane = lax.broadcasted_iota(jnp.int32, (tq, LANES), 1)
        c, s = cs_ref[...], sn_ref[...]

        @pl.when(pl.program_id(0) == 0)
        def _():
            dqg_ref[...] = jnp.zeros_like(dqg_ref)
            dkg_ref[...] = jnp.zeros_like(dkg_ref)

        def norm_rope_bwd(xg, g2, dout):
            r = lax.rsqrt(_head_sums(xg * xg, lane) * (1.0 / HEAD_DIM) + EPS)
            xh = xg * r
            dxn = dout * c + _swap16(dout * s, lane)
            gd = dxn * g2
            dx = r * (gd - xh * (_head_sums(gd * xh, lane) * (1.0 / HEAD_DIM)))
            return dx, jnp.sum(dxn * xh, axis=0, keepdims=True)

        for gi in range(4):
            sl = slice(gi * LANES, (gi + 1) * LANES)
            dx, dg = norm_rope_bwd(pa_ref[:, sl].astype(F32), qg_ref[...], dq_ref[:, sl] * scale)
            dpa_ref[:, sl] = dx.astype(BF16)
            dqg_ref[...] += dg
        dx, dg = norm_rope_bwd(pa_ref[:, 512:640].astype(F32), kg_ref[...], dkT_ref[...].T)
        dpa_ref[:, 512:640] = dx.astype(BF16)
        dkg_ref[...] += dg
        dpa_ref[:, 640:768] = dvT_ref[...].T.astype(BF16)
        z = pa_ref[:, 768:1280].astype(F32)
        _, dsilu = _silu_and_grad(z)
        dpa_ref[:, 768:1280] = (dy_ref[...] * o_ref[...] * dsilu).astype(BF16)

    row = lambda w: pl.BlockSpec((tq, w), lambda i: (i, 0))
    col = pl.BlockSpec((LANES, tq), lambda i: (0, i))
    vec = pl.BlockSpec((1, LANES), lambda i: (0, 0))
    return _host_call(
        body, (pa, cs, sn, qg2, kg2, dq, dkT, dvT, dy, o),
        [row(W_A), row(LANES), row(LANES), vec, vec, row(BW), col, col, row(BW), row(BW)],
        (row(W_A), vec, vec), (SDS((t, W_A), BF16), SDS((1, LANES), F32), SDS((1, LANES), F32)),
        name="attn_prep_bwd", grid=(t // tq,), semantics=("arbitrary",), exchange=exchange)


def _layer_norm(v, g, b):
    mu = jnp.mean(v, axis=-1, keepdims=True)
    xc = v - mu
    rs = lax.rsqrt(jnp.mean(xc * xc, axis=-1, keepdims=True) + EPS)
    xh = xc * rs
    return xh * g + b, xh, rs


def _gmlp_tile(pb, g_ref, b_ref, ws_ref, bs_ref, y_ref):
    vln, _, _ = _layer_norm(pb[:, BW:2 * BW].astype(F32), g_ref[...], b_ref[...])
    vb = vln.astype(BF16)
    for gi in range(B_GROUPS):
        w = ws_ref[gi].astype(BF16)
        cs_ = slice(gi * CHUNK, (gi + 1) * CHUNK)
        for n in range(pb.shape[0] // CHUNK):
            rs_ = slice(n * CHUNK, (n + 1) * CHUNK)
            mixed = _dot(w, vb[rs_, cs_], NN) + bs_ref[gi]
            z = pb[rs_, 2 * BW + gi * CHUNK:2 * BW + (gi + 1) * CHUNK].astype(F32)
            y_ref[rs_, cs_] = (pb[rs_, cs_].astype(F32) * mixed * (z * _sigmoid(z))).astype(BF16)


def _gmlp_bwd(pb, lng, lnb, ws, bs, dy):
    t = pb.shape[0]
    tb = _tile(t, 256, CHUNK)

    def body(pb_ref, g_ref, b_ref, ws_ref, bs_ref, dy_ref, dpb_ref, dws_ref, dbs_ref, dg_ref, db_ref, dvln_ref):
        @pl.when(pl.program_id(0) == 0)
        def _():
            dws_ref[...] = jnp.zeros_like(dws_ref)
            dbs_ref[...] = jnp.zeros_like(dbs_ref)
            dg_ref[...] = jnp.zeros_like(dg_ref)
            db_ref[...] = jnp.zeros_like(db_ref)

        vln, xh, rs = _layer_norm(pb_ref[:, BW:2 * BW].astype(F32), g_ref[...], b_ref[...])
        vb = vln.astype(BF16)
        for gi in range(B_GROUPS):
            w = ws_ref[gi].astype(BF16)
            cs_ = slice(gi * CHUNK, (gi + 1) * CHUNK)
            for n in range(tb // CHUNK):
                rs_ = slice(n * CHUNK, (n + 1) * CHUNK)
                vbc = vb[rs_, cs_]
                mixed = _dot(w, vbc, NN) + bs_ref[gi]
                zs = slice(2 * BW + gi * CHUNK, 2 * BW + (gi + 1) * CHUNK)
                z = pb_ref[rs_, zs].astype(F32)
                u = pb_ref[rs_, cs_].astype(F32)
                sil, dsil = _silu_and_grad(z)
                dyc = dy_ref[rs_, cs_]
                dmixed = dyc * u * sil
                dpb_ref[rs_, cs_] = (dyc * mixed * sil).astype(BF16)
                dpb_ref[rs_, zs] = (dyc * u * mixed * dsil).astype(BF16)
                dmb = dmixed.astype(BF16)
                dws_ref[gi] += _dot(dmb, vbc, NT)
                dbs_ref[gi] += jnp.sum(dmixed, axis=-1, keepdims=True)
                dvln_ref[rs_, cs_] = _dot(w, dmb, TN)
        dvln = dvln_ref[...]
        dg_ref[...] += jnp.sum(dvln * xh, axis=0, keepdims=True)
        db_ref[...] += jnp.sum(dvln, axis=0, keepdims=True)
        gd = dvln * g_ref[...]
        dv = rs * (gd - jnp.mean(gd, axis=-1, keepdims=True) - xh * jnp.mean(gd * xh, axis=-1, keepdims=True))
        dpb_ref[:, BW:2 * BW] = dv.astype(BF16)

    vec = pl.BlockSpec((1, BW), lambda i: (0, 0))
    wsb = pl.BlockSpec((B_GROUPS, CHUNK, CHUNK), lambda i: (0, 0, 0))
    bsb = pl.BlockSpec((B_GROUPS, CHUNK, 1), lambda i: (0, 0, 0))
    return pl.pallas_call(
        body, name="gmlp_bwd", grid=(t // tb,),
        in_specs=[pl.BlockSpec((tb, W_B), lambda i: (i, 0)), vec, vec, wsb, bsb, pl.BlockSpec((tb, BW), lambda i: (i, 0))],
        out_specs=(pl.BlockSpec((tb, W_B), lambda i: (i, 0)), wsb, bsb, vec, vec),
        out_shape=(SDS((t, W_B), BF16), SDS((B_GROUPS, CHUNK, CHUNK), F32), SDS((B_GROUPS, CHUNK, 1), F32),
                   SDS((1, BW), F32), SDS((1, BW), F32)),
        scratch_shapes=[pltpu.VMEM((tb, BW), F32)],
        compiler_params=_params("arbitrary"),
    )(pb, lng, lnb, ws, bs, dy)


def _mem_scores(pm_ref, kv_ref, h):
    hs = slice(h * M_HEAD_DIM, (h + 1) * M_HEAD_DIM)
    return _dot(pm_ref[:, hs].astype(BF16), kv_ref[:, hs].astype(BF16), NT) * (M_HEAD_DIM ** -0.5)


def _softmax_rows(s):
    e = jnp.exp(s - jnp.max(s, axis=-1, keepdims=True))
    return e / jnp.sum(e, axis=-1, keepdims=True)


def _mem_attn_tile(pm, kv_ref, y_ref):
    s_next = _mem_scores(pm, kv_ref, 0)
    for h in range(M_HEADS):
        hs = slice(h * M_HEAD_DIM, (h + 1) * M_HEAD_DIM)
        vh = kv_ref[:, BW + h * M_HEAD_DIM:BW + (h + 1) * M_HEAD_DIM].astype(BF16)
        s, s_next = s_next, (_mem_scores(pm, kv_ref, h + 1) if h + 1 < M_HEADS else None)
        p = _softmax_rows(s)
        o = _dot(p.astype(BF16), vh, NN)
        z = pm[:, BW + h * M_HEAD_DIM:BW + (h + 1) * M_HEAD_DIM].astype(F32)
        y_ref[:, hs] = (o * (z * _sigmoid(z))).astype(BF16)


def _mem_attn_bwd(pm, kv, dy):
    t = pm.shape[0]
    tq = _tile(t, 512, 8)
    ml = kv.shape[0]
    scale = M_HEAD_DIM ** -0.5

    def body(pm_ref, kv_ref, dy_ref, dpm_ref, dkv_ref):
        @pl.when(pl.program_id(0) == 0)
        def _():
            dkv_ref[...] = jnp.zeros_like(dkv_ref)

        s_next = _mem_scores(pm_ref, kv_ref, 0)
        for h in range(M_HEADS):
            hs = slice(h * M_HEAD_DIM, (h + 1) * M_HEAD_DIM)
            zs = slice(BW + h * M_HEAD_DIM, BW + (h + 1) * M_HEAD_DIM)
            kh = kv_ref[:, hs].astype(BF16)
            vh = kv_ref[:, zs].astype(BF16)
            qh = pm_ref[:, hs].astype(BF16)
            s, s_next = s_next, (_mem_scores(pm_ref, kv_ref, h + 1) if h + 1 < M_HEADS else None)
            p = _softmax_rows(s)
            pb = p.astype(BF16)
            o = _dot(pb, vh, NN)
            sil, dsil = _silu_and_grad(pm_ref[:, zs].astype(F32))
            dyh = dy_ref[:, hs]
            do = dyh * sil
            dpm_ref[:, zs] = (dyh * o * dsil).astype(BF16)
            delta = jnp.sum(do * o, axis=-1, keepdims=True)
            do_b = do.astype(BF16)
            dp = _dot(do_b, vh, NT)
            dr_b = (p * (dp - delta) * scale).astype(BF16)
            dpm_ref[:, hs] = _dot(dr_b, kh, NN).astype(BF16)
            dkv_ref[:, hs] += _dot(dr_b, qh, TN)
            dkv_ref[:, zs] += _dot(pb, do_b, TN)

    kvb = pl.BlockSpec((ml, 2 * BW), lambda i: (0, 0))
    return pl.pallas_call(
        body, name="mem_attn_bwd", grid=(t // tq,),
        in_specs=[pl.BlockSpec((tq, W_M), lambda i: (i, 0)), kvb, pl.BlockSpec((tq, BW), lambda i: (i, 0))],
        out_specs=(pl.BlockSpec((tq, W_M), lambda i: (i, 0)), kvb),
        out_shape=(SDS((t, W_M), BF16), SDS((ml, 2 * BW), F32)),
        compiler_params=_params("arbitrary"),
    )(pm, kv, dy)


def _merge_fwd(ya, yb, ym, pg, wbr, x, w_out, exchange=None):
    t = ya.shape[0]
    tm = _tile(t, 512, 8)

    def body(ya_ref, yb_ref, ym_ref, pg_ref, w_ref, x_ref, wo_ref, m_ref, xn_ref):
        ups = [_dot(y_ref[...], w_ref[n], NN) for n, y_ref in enumerate((ya_ref, yb_ref, ym_ref))]
        acc = None
        for n, up in enumerate(ups):
            term = _sigmoid(pg_ref[:, n * D_MODEL:(n + 1) * D_MODEL].astype(F32)) * up
            acc = term if acc is None else acc + term
        merged = acc.astype(BF16)
        m_ref[...] = merged
        xn_ref[...] = x_ref[...] + _dot(merged, wo_ref[...], NN)

    yb_spec = pl.BlockSpec((tm, BW), lambda i: (i, 0))
    row = pl.BlockSpec((tm, D_MODEL), lambda i: (i, 0))
    return _host_call(
        body, (ya, yb, ym, pg, wbr, x, w_out),
        [yb_spec, yb_spec, yb_spec, pl.BlockSpec((tm, W_G), lambda i: (i, 0)),
         pl.BlockSpec((3, BW, D_MODEL), lambda i: (0, 0, 0)), row, pl.BlockSpec(w_out.shape, lambda i: (0, 0))],
        (row, row), (SDS((t, D_MODEL), BF16), SDS((t, D_MODEL), F32)),
        name="merge_fwd", grid=(t // tm,), semantics=("parallel",), exchange=exchange)


def _merge_bwd(ya, yb, ym, pg, wbr, dx_out, w_out, exchange=None):
    t = ya.shape[0]
    tm = _tile(t, 512, 16)

    def body(ya_ref, yb_ref, ym_ref, pg_ref, w_ref, dx_ref, wo_ref, dya_ref, dyb_ref, dym_ref, dpg_ref, dw_ref):
        @pl.when(pl.program_id(0) == 0)
        def _():
            dw_ref[...] = jnp.zeros_like(dw_ref)

        dmf = _dot(dx_ref[...].astype(BF16), wo_ref[...], NT)
        branches = ((ya_ref, dya_ref), (yb_ref, dyb_ref), (ym_ref, dym_ref))

        def gate(n):
            gt = _sigmoid(pg_ref[:, n * D_MODEL:(n + 1) * D_MODEL].astype(F32))
            return gt, (dmf * gt).astype(BF16)

        gt, dup = gate(0)
        for n, (y_ref, dy_ref) in enumerate(branches):
            nxt = gate(n + 1) if n + 1 < len(branches) else None
            y, w = y_ref[...], w_ref[n]
            dy_ref[...] = _dot(dup, w, NT)
            dw_ref[n] += _dot(y, dup, TN)
            up = _dot(y, w, NN)
            dpg_ref[:, n * D_MODEL:(n + 1) * D_MODEL] = (dmf * up * gt * (1.0 - gt)).astype(BF16)
            if nxt is not None:
                gt, dup = nxt

    y_spec = pl.BlockSpec((tm, BW), lambda i: (i, 0))
    w_spec = pl.BlockSpec((3, BW, D_MODEL), lambda i: (0, 0, 0))
    return _host_call(
        body, (ya, yb, ym, pg, wbr, dx_out, w_out),
        [y_spec, y_spec, y_spec, pl.BlockSpec((tm, W_G), lambda i: (i, 0)), w_spec,
         pl.BlockSpec((tm, D_MODEL), lambda i: (i, 0)), pl.BlockSpec(w_out.shape, lambda i: (0, 0))],
        (y_spec, y_spec, y_spec, pl.BlockSpec((tm, W_G), lambda i: (i, 0)), w_spec),
        (SDS((t, BW), F32), SDS((t, BW), F32), SDS((t, BW), F32), SDS((t, W_G), BF16), SDS((3, BW, D_MODEL), F32)),
        name="merge_bwd", grid=(t // tm,), semantics=("arbitrary",), exchange=exchange)


def _loss_head(x, g, target):
    t, d = x.shape
    tm = _tile(t, 512, 8)

    def body(x_ref, g_ref, t_ref, loss_ref, dx_ref, dg_ref):
        @pl.when(pl.program_id(0) == 0)
        def _():
            loss_ref[...] = jnp.zeros_like(loss_ref)
            dg_ref[...] = jnp.zeros_like(dg_ref)

        xf = x_ref[...]
        r = lax.rsqrt(jnp.mean(xf * xf, axis=-1, keepdims=True) + EPS)
        xh = xf * r
        err = xh * g_ref[...] - t_ref[...]
        per_tok = jnp.mean(err * err, axis=-1, keepdims=True)
        loss_ref[...] += 0.5 * jnp.sum(per_tok, axis=0, keepdims=True)
        dy = err * (1.0 / d)
        gd = dy * g_ref[...]
        dx_ref[...] = r * (gd - xh * jnp.mean(gd * xh, axis=-1, keepdims=True))
        dg_ref[...] += jnp.sum(dy * xh, axis=0, keepdims=True)

    row = pl.BlockSpec((tm, d), lambda i: (i, 0))
    vec = pl.BlockSpec((1, d), lambda i: (0, 0))
    return pl.pallas_call(
        body, name="loss_head", grid=(t // tm,),
        in_specs=[row, vec, row],
        out_specs=(pl.BlockSpec((1, 1), lambda i: (0, 0)), row, vec),
        out_shape=(SDS((1, 1), F32), SDS((t, d), F32), SDS((1, d), F32)),
        compiler_params=_params("arbitrary"),
    )(x, g, target)


def _layer_fwd(x, mem, w, tabs, next_shards=None):
    cs, sn = tabs
    riding = next_shards is not None
    memn = _rmsnorm_fwd(mem, w["mem_g"], "mem_rmsnorm_fwd")
    kv = _mm(memn, w["w_kv"], "nn", "mem_kv")
    h, pa, pb, pm, pg, q, qT, k, kT, vT, yb, ym = _norm_in_proj(
        x, w["norm_g"], w["w_in_t"], cs, sn, w["qg2"], w["kg2"], w["ln_g"], w["ln_b"], w["w_s"], w["b_s"], kv)
    (o, lse, ya), gathered = _attn_fwd(qT, k, vT, pa, exchange=_gather_first_hop(next_shards) if riding else None)
    (merged, x_next), gathered = _merge_fwd(ya, yb, ym, pg, w["w_br"], x, w["w_out"],
                                            exchange=_gather_forward(gathered) if riding else None)
    saved = dict(x=x, h=h, pa=pa, pb=pb, pm=pm, pg=pg, q=q, qT=qT, k=k, kT=kT, vT=vT, o=o, lse=lse, ya=ya, yb=yb, ym=ym,
                 memn=memn, kv=kv, merged=merged)
    return x_next, saved, gathered


def _layer_bwd(dx_out, mem, w, s, tabs, pending=None, core=None, scatter_own=False, small_early=None):
    cs, sn = tabs
    t = dx_out.shape[0]
    riding = pending is not None
    by_owner = lambda a: a.reshape((N_DEV // 2, 2, -1) + a.shape[-2:])
    row_shards = lambda a: by_owner(a.reshape(N_DEV, a.shape[0] // N_DEV, a.shape[1]))
    d_w_out = _mm_tn(s["merged"], dx_out, "d_w_out")
    (dya, dyb, dym, dpg, d_w_br), recv = _merge_bwd(s["ya"], s["yb"], s["ym"], s["pg"], w["w_br"], dx_out, w["w_out"],
                                                    exchange=_scatter_to_sibling(pending) if riding else None)
    parts = _pair_sums(BIG, pending, recv, core) if riding else None
    grads = dict(w_br=by_owner(_split_w_br(d_w_br)), w_out=row_shards(d_w_out))
    early = [grads[n] for n in BIG[2:]]
    grp = A_HEADS // A_KV_HEADS
    lse_cols = s["lse"].reshape(A_KV_HEADS, grp, t).transpose(0, 2, 1)
    gathering = small_early is not None
    (dq, dkT, dvT), landed = _attn_bwd(
        s["q"], s["qT"], s["k"], s["kT"], s["vT"], s["pa"], s["o"], lse_cols, dya,
        exchange=_both(_both(_scatter_to_chips(parts) if riding else None, _scatter_to_sibling(early) if scatter_own else None),
                       _gather_first_hop([small_early]) if gathering else None))
    landed = list(landed)
    from_chips = [landed.pop(0) for _ in BIG] if riding else None
    early_recv = [landed.pop(0) for _ in early] if scatter_own else None
    early_parts = _pair_sums(BIG[2:], early, early_recv, core) if scatter_own else None
    (dpa, d_qg2, d_kg2), landed2 = _attn_prep_bwd(
        s["pa"], cs, sn, w["qg2"], w["kg2"], dq, dkT.reshape(-1, t), dvT.reshape(-1, t), dya, s["o"],
        exchange=_both(_scatter_to_chips(early_parts) if scatter_own else None,
                       _gather_forward(landed) if gathering else None))
    landed2 = list(landed2)
    early_from_chips = [landed2.pop(0) for _ in early] if scatter_own else None
    small_all = landed2.pop(0) if gathering else None
    dpb, d_w_s, d_b_s, d_ln_g, d_ln_b = _gmlp_bwd(s["pb"], w["ln_g"], w["ln_b"], w["w_s"], w["b_s"], dyb)
    dpm, dkv = _mem_attn_bwd(s["pm"], s["kv"], dym)
    d_w_kv = _mm_tn(s["memn"], dkv, "d_w_kv")
    dmemn = _mm(dkv, w["w_kv"], "nt", "d_memn")
    d_mem_g = _rmsnorm_dg(mem, w["mem_g"], dmemn, "mem_rmsnorm_bwd")
    dps = (dpa, dpb, dpm, dpg)
    grads.update(w_in=row_shards(_d_w_in_t(dps, s["h"])), w_mem_kv=row_shards(d_w_kv))
    own = None
    if scatter_own:
        late = [grads[n] for n in BIG[:2]]
        late_parts = _pair_sums(BIG[:2], late, _exchange_call(_scatter_to_sibling(late), "rs_sibling_swap"), core)
        own = _scatter_to_chips(late_parts)
    (dx_in, d_norm_g), late_from_chips = _dh_rmsnorm_bwd(dps, w["w_in_t"], s["x"], w["norm_g"], dx_out, exchange=own)
    if scatter_own:
        own_parts, own_from_chips = late_parts + early_parts, tuple(late_from_chips) + tuple(early_from_chips)
    grads.update(norm_g=d_norm_g[0], q_norm_g=d_qg2[0, :HEAD_DIM] + d_qg2[0, HEAD_DIM:],
                 k_norm_g=d_kg2[0, :HEAD_DIM] + d_kg2[0, HEAD_DIM:], sg_ln_g=d_ln_g[0], sg_ln_b=d_ln_b[0],
                 w_s=d_w_s, b_s=d_b_s[:, :, 0], mem_norm_g=d_mem_g[0])
    return (dx_in, grads, ((parts, from_chips) if riding else None),
            ((own_parts, own_from_chips) if scatter_own else None), small_all)


def _layer_weights(l, w_in_t, w_kv, w_br, w_out, small):
    tile2 = lambda g: jnp.tile(g.reshape(1, -1), (1, 2))
    return dict(
        w_in_t=w_in_t, w_kv=w_kv, w_br=w_br, w_out=w_out,
        norm_g=small["norm_g"][l].reshape(1, -1), qg2=tile2(small["q_norm_g"][l]), kg2=tile2(small["k_norm_g"][l]),
        ln_g=small["sg_ln_g"][l].reshape(1, -1), ln_b=small["sg_ln_b"][l].reshape(1, -1),
        w_s=small["w_s"][l], b_s=small["b_s"][l][:, :, None], mem_g=small["mem_norm_g"][l].reshape(1, -1))


def _position():
    x, y, c = lax.axis_index("x"), lax.axis_index("y"), lax.axis_index("c")
    return x, y, c, [(1 - x, y), (x, 1 - y), (1 - x, 1 - y)]


def _gather_first_hop(shards):
    n = len(shards)

    def make(x_refs, out_refs, sems):
        send_sems, recv_sems, local_sems = sems
        x, y, c, chips = _position()
        me = 4 * x + 2 * y + c
        peers = [(x, y, 1 - c)] + [(cx, cy, c) for cx, cy in chips]
        copies = [pltpu.make_async_copy(x_refs[t], out_refs[t].at[me], local_sems.at[t]) for t in range(n)]
        copies += [pltpu.make_async_remote_copy(
            src_ref=x_refs[t], dst_ref=out_refs[t].at[me], send_sem=send_sems.at[t, k], recv_sem=recv_sems.at[t, k],
            device_id=peer, device_id_type=MESH_ID) for t in range(n) for k, peer in enumerate(peers)]
        return copies

    return _Exchange(shards, [SDS((N_DEV,) + a.shape, a.dtype) for a in shards],
                     [pltpu.SemaphoreType.DMA((n, 4)), pltpu.SemaphoreType.DMA((n, 4)), pltpu.SemaphoreType.DMA((n,))], make)


def _gather_forward(gathered):
    n = len(gathered)

    def make(in_refs, out_refs, sems):
        send_sems, recv_sems = sems
        x, y, c, chips = _position()
        return [pltpu.make_async_remote_copy(
            src_ref=in_refs[t].at[4 * cx + 2 * cy + c], dst_ref=out_refs[t].at[4 * cx + 2 * cy + c],
            send_sem=send_sems.at[t, j], recv_sem=recv_sems.at[t, j], device_id=(x, y, 1 - c), device_id_type=MESH_ID)
            for t in range(n) for j, (cx, cy) in enumerate(chips)]

    return _Exchange(gathered, [SDS(a.shape, a.dtype) for a in gathered],
                     [pltpu.SemaphoreType.DMA((n, 3)), pltpu.SemaphoreType.DMA((n, 3))], make,
                     aliases={t: t for t in range(n)})


def _all_gather(shards, name):
    return _exchange_call(_gather_forward(_exchange_call(_gather_first_hop(shards), name + "_hop1")), name + "_hop2")


def _assemble_w_br(gathered):
    _, nb, rows, shard = gathered.shape

    def body(g_ref, o_ref):
        for d in range(N_DEV):
            o_ref[:, :, d * shard:(d + 1) * shard] = g_ref[d]

    return pl.pallas_call(
        body, name="assemble_w_br", out_shape=SDS((nb, rows, N_DEV * shard), gathered.dtype),
        compiler_params=pltpu.CompilerParams(vmem_limit_bytes=VMEM_LIMIT),
    )(gathered)


def _split_w_br(dw):
    nb, rows, cols = dw.shape
    shard = cols // N_DEV

    def body(d_ref, o_ref):
        for d in range(N_DEV):
            o_ref[d] = d_ref[:, :, d * shard:(d + 1) * shard].astype(BF16)

    return pl.pallas_call(
        body, name="split_w_br", out_shape=SDS((N_DEV, nb, rows, shard), BF16),
        compiler_params=pltpu.CompilerParams(vmem_limit_bytes=VMEM_LIMIT),
    )(dw)


def _scatter_to_sibling(dests):
    n = len(dests)

    def make(d_refs, recv_refs, sems):
        send_sems, recv_sems = sems
        x, y, c, _ = _position()
        return [pltpu.make_async_remote_copy(
            src_ref=d_refs[t].at[:, 1 - c], dst_ref=recv_refs[t], send_sem=send_sems.at[t],
            recv_sem=recv_sems.at[t], device_id=(x, y, 1 - c), device_id_type=MESH_ID) for t in range(n)]

    return _Exchange(dests, [SDS(a.shape[:1] + a.shape[2:], a.dtype) for a in dests],
                     [pltpu.SemaphoreType.DMA((n,)), pltpu.SemaphoreType.DMA((n,))], make)


def _scatter_to_chips(parts):
    n = len(parts)

    def make(p_refs, recv_refs, sems):
        send_sems, recv_sems = sems
        _, _, c, chips = _position()
        return [pltpu.make_async_remote_copy(
            src_ref=p_refs[t].at[2 * cx + cy], dst_ref=recv_refs[t].at[k], send_sem=send_sems.at[t, k],
            recv_sem=recv_sems.at[t, k], device_id=(cx, cy, c), device_id_type=MESH_ID)
            for t in range(n) for k, (cx, cy) in enumerate(chips)]

    return _Exchange(parts, [SDS((3,) + a.shape[1:], a.dtype) for a in parts],
                     [pltpu.SemaphoreType.DMA((n, 3)), pltpu.SemaphoreType.DMA((n, 3))], make)


def _pair_sums(names, dests, recv, core):
    return [_pair_sum(d, r, core, "rs_pair_sum_" + n) for n, d, r in zip(names, dests, recv)]


def _pair_sum(dest, recv, core, name):
    _, _, na, r, cdim = dest.shape

    def body(core_ref, a_ref, b_ref, o_ref):
        o_ref[...] = (a_ref[...].astype(F32) + b_ref[...].astype(F32)).astype(o_ref.dtype)

    blk = pl.BlockSpec((None, na, r, cdim), lambda j, core_ref: (j, 0, 0, 0))
    return pl.pallas_call(
        body, name=name, out_shape=SDS(recv.shape, BF16),
        grid_spec=pltpu.PrefetchScalarGridSpec(
            num_scalar_prefetch=1, grid=(4,),
            in_specs=[pl.BlockSpec((None, None, na, r, cdim), lambda j, core_ref: (j, core_ref[0], 0, 0, 0)), blk],
            out_specs=blk),
        compiler_params=_params("parallel"),
    )(core, dest, recv)


def _adamw_math(w, g, m, v):
    m = ADAM_B1 * m + (1.0 - ADAM_B1) * g
    v = ADAM_B2 * v + (1.0 - ADAM_B2) * (g * g)
    m_hat = m / (1.0 - ADAM_B1 ** ADAM_STEP)
    v_hat = v / (1.0 - ADAM_B2 ** ADAM_STEP)
    delta = -ADAM_LR * (m_hat / (jnp.sqrt(v_hat) + ADAM_EPS) + ADAM_WD * w)
    return delta, m, v


def _sum_and_adamw(parts, w, m, v, name):
    n, r, ln = parts.shape
    tr = _tile(r, 512, 16)

    def body(p_ref, w_ref, m_ref, v_ref, g_out, d_out, m_out, v_out):
        g = p_ref[0].astype(F32)
        for j in range(1, n):
            g = g + p_ref[j].astype(F32)
        delta, nm, nv = _adamw_math(w_ref[...], g, m_ref[...], v_ref[...])
        g_out[...] = g
        d_out[...] = delta
        m_out[...] = nm
        v_out[...] = nv

    blk = pl.BlockSpec((tr, ln), lambda i: (i, 0))
    return pl.pallas_call(
        body, name=name, grid=(r // tr,),
        in_specs=[pl.BlockSpec((n, tr, ln), lambda i: (0, i, 0)), blk, blk, blk],
        out_specs=(blk, blk, blk, blk), out_shape=tuple(SDS((r, ln), F32) for _ in range(4)),
        compiler_params=_params("parallel"),
    )(parts, w, m, v)


BIG = ("w_in", "w_mem_kv", "w_br", "w_out")
SMALL = ("norm_g", "q_norm_g", "k_norm_g", "sg_ln_g", "sg_ln_b", "w_s", "b_s", "mem_norm_g", "final_g")


def _pack(arrs, row_unit=16):
    flat = jnp.concatenate([a.reshape(-1) for a in arrs])
    pad = (-flat.shape[0]) % (row_unit * LANES)
    if pad:
        flat = jnp.concatenate([flat, jnp.zeros((pad,), flat.dtype)])
    return flat.reshape(-1, LANES)


def _unpack(buf, shapes):
    flat = buf.reshape(-1)
    out, off = [], 0
    for shp in shapes:
        n = 1
        for s_ in shp:
            n *= s_
        out.append(flat[off:off + n].reshape(shp))
        off += n
    return out


def _shard_sum_adamw(part, from_chips, chip, w, m, v, layer, prev, name):
    _, na, r, cdim = part.shape
    flat = w.ndim == 3
    assert not flat or na == 1
    tr = _tile(r, max(8, (256 * 1024) // (na * cdim)), 8)
    n_prev = 0 if prev is None else len(prev)

    def body(chip_ref, p_ref, f_ref, w_ref, m_ref, v_ref, *rest):
        g_out, d_out, m_out, v_out = rest[n_prev:]
        g = p_ref[...].astype(F32)
        for j in range(3):
            g = g + f_ref[j].astype(F32)
        delta, nm, nv = _adamw_math(w_ref[...], g, m_ref[...], v_ref[...])
        g_out[...] = g
        d_out[...] = delta
        m_out[...] = nm
        v_out[...] = nv

    a_blk = None if flat else na
    if flat:
        lay = pl.BlockSpec((None, tr, cdim), lambda i, chip_ref: (layer, i, 0))
    else:
        lay = pl.BlockSpec((None, na, tr, cdim), lambda i, chip_ref: (layer, 0, i, 0))
    return pl.pallas_call(
        body, name=name, out_shape=tuple(SDS(w.shape, F32) for _ in range(4)),
        grid_spec=pltpu.PrefetchScalarGridSpec(
            num_scalar_prefetch=1, grid=(r // tr,),
            in_specs=[pl.BlockSpec((None, a_blk, tr, cdim), lambda i, chip_ref: (chip_ref[0], 0, i, 0)),
                      pl.BlockSpec((3, a_blk, tr, cdim), lambda i, chip_ref: (0, 0, i, 0)), lay, lay, lay]
            + [HBM] * n_prev,
            out_specs=(lay, lay, lay, lay)),
        input_output_aliases={6 + j: j for j in range(n_prev)},
        compiler_params=_params("parallel"),
    )(chip, part, from_chips, w, m, v, *(prev or ()))


def kernel(x, mem, norm_g, w_in, q_norm_g, k_norm_g, sg_ln_g, sg_ln_b, w_s, b_s, mem_norm_g, w_mem_kv, w_br, w_out, final_g, loss_target, m_norm_g, m_w_in, m_q_norm_g, m_k_norm_g, m_sg_ln_g, m_sg_ln_b, m_w_s, m_b_s, m_mem_norm_g, m_w_mem_kv, m_w_br, m_w_out, m_final_g, v_norm_g, v_w_in, v_q_norm_g, v_k_norm_g, v_sg_ln_g, v_sg_ln_b, v_w_s, v_b_s, v_mem_norm_g, v_w_mem_kv, v_w_br, v_w_out, v_final_g):
    wts = dict(norm_g=norm_g, w_in=w_in, q_norm_g=q_norm_g, k_norm_g=k_norm_g, sg_ln_g=sg_ln_g, sg_ln_b=sg_ln_b,
               w_s=w_s, b_s=b_s, mem_norm_g=mem_norm_g, w_mem_kv=w_mem_kv, w_br=w_br, w_out=w_out, final_g=final_g)
    mom1 = dict(norm_g=m_norm_g, w_in=m_w_in, q_norm_g=m_q_norm_g, k_norm_g=m_k_norm_g, sg_ln_g=m_sg_ln_g,
                sg_ln_b=m_sg_ln_b, w_s=m_w_s, b_s=m_b_s, mem_norm_g=m_mem_norm_g, w_mem_kv=m_w_mem_kv, w_br=m_w_br,
                w_out=m_w_out, final_g=m_final_g)
    mom2 = dict(norm_g=v_norm_g, w_in=v_w_in, q_norm_g=v_q_norm_g, k_norm_g=v_k_norm_g, sg_ln_g=v_sg_ln_g,
                sg_ln_b=v_sg_ln_b, w_s=v_w_s, b_s=v_b_s, mem_norm_g=v_mem_norm_g, w_mem_kv=v_w_mem_kv, w_br=v_w_br,
                w_out=v_w_out, final_g=v_final_g)
    dp = w_in.shape[0]
    core = lax.axis_index("c").astype(jnp.int32).reshape(1)
    chip = (2 * lax.axis_index("x") + lax.axis_index("y")).astype(jnp.int32).reshape(1)

    t_in = lambda a: jnp.swapaxes(a, 1, 2)
    wts, mom1, mom2 = [dict(d_, w_in=t_in(d_["w_in"])) for d_ in (wts, mom1, mom2)]

    shard_bf = {n: wts[n].astype(BF16) for n in BIG}
    shards = lambda l: [shard_bf[n][l] for n in BIG]
    x_l, mem_l = x[0], mem[0]
    tabs = _rope_tables(x_l.shape[0])

    gathered = _all_gather(shards(0), "weights_all_gather")
    layers, saved = [], []
    for l in range(dp):
        g_in, g_kv, g_br, g_out = gathered
        layers.append(_layer_weights(l, g_in.reshape(IN_WIDTH, -1), g_kv.reshape(D_MODEL, -1), _assemble_w_br(g_br),
                                     g_out.reshape(D_MODEL, -1), wts))
        x_l, s, gathered = _layer_fwd(x_l, mem_l, layers[l], tabs, next_shards=shards(l + 1) if l + 1 < dp else None)
        saved.append(s)
    loss_local, dx, d_final_g = _loss_head(x_l, final_g.reshape(1, -1), loss_target[0])
    loss = lax.psum(loss_local[0, 0], AXES)

    def finish(l, parts, from_chips, prev):
        return {n: _shard_sum_adamw(p, f, chip, wts[n], mom1[n], mom2[n], l, None if prev is None else prev[n],
                                    "sum_adamw_" + n)
                for n, p, f in zip(BIG, parts, from_chips)}

    layered = [n for n in SMALL if n != "final_g"]
    early_pack = lambda per_layer, last: _pack([per_layer(n) for n in layered if dp > 1] + [last])
    late_pack = lambda first: _pack([first(n) for n in layered])

    grads, updated, pending, early_all = [None] * dp, None, None, None
    for l in reversed(range(dp)):
        small_early = None
        if l == 0:
            small_early = early_pack(lambda n: jnp.stack([grads[j][n] for j in range(1, dp)]), d_final_g)
        dx, grads[l], scattered, own, small_all = _layer_bwd(
            dx, mem_l, layers[l], saved[l], tabs, pending=pending, core=core, scatter_own=(l == 0), small_early=small_early)
        if scattered is not None:
            updated = finish(l + 1, *scattered, updated)
        if small_all is not None:
            early_all = small_all
        pending = [grads[l][n] for n in BIG]
    updated = finish(0, *own, updated)
    grad_x = dx
    big_out = [{n: (t_in(updated[n][k]) if n == "w_in" else updated[n][k]) for n in BIG} for k in range(4)]

    (late_all,) = _all_gather([late_pack(lambda n: grads[0][n])], "small_all_gather")
    early_bufs = _sum_and_adamw(early_all, *[early_pack(lambda n, d_=d_: d_[n][1:], d_["final_g"]) for d_ in (wts, mom1, mom2)],
                                "small_sum_adamw_early")
    late_bufs = _sum_and_adamw(late_all, *[late_pack(lambda n, d_=d_: d_[n][0]) for d_ in (wts, mom1, mom2)],
                               "small_sum_adamw_late")
    early_shapes = [(dp - 1,) + wts[n].shape[1:] for n in layered if dp > 1] + [wts["final_g"].shape]
    late_shapes = [wts[n].shape[1:] for n in layered]

    outs = []
    for big_vals, early_buf, late_buf in zip(big_out, early_bufs, late_bufs):
        vals = dict(big_vals)
        early_vals = _unpack(early_buf, early_shapes)
        vals["final_g"] = early_vals[-1]
        for j, (n, first) in enumerate(zip(layered, _unpack(late_buf, late_shapes))):
            vals[n] = jnp.concatenate([first[None], early_vals[j]]) if dp > 1 else first[None]
        outs.append(vals)
    order = ("norm_g", "w_in", "q_norm_g", "k_norm_g", "sg_ln_g", "sg_ln_b", "w_s", "b_s", "mem_norm_g", "w_mem_kv",
             "w_br", "w_out", "final_g")
    result = [loss, grad_x[None]]
    for vals in outs:
        result += [vals[n] for n in order]
    return tuple(result)
```

```python
import functools

import jax
import jax.numpy as jnp
from jax import lax
from jax.experimental import pallas as pl
from jax.experimental.pallas import tpu as pltpu

F32 = jnp.float32
BF16 = jnp.bfloat16
SDS = jax.ShapeDtypeStruct
MESH_ID = pl.DeviceIdType.MESH
AXES = ("x", "y", "c")
N_DEV = 8

D_MODEL = 1024
DEPTH = 4
GRID_W = 64
CHUNK = 128
ROPE_THETA = 10000.0
EPS = 1e-6
HEAD_DIM = 64
A_HEADS = 8
A_KV_HEADS = 2
B_GROUPS = 4
M_HEADS = 4
M_HEAD_DIM = 128
BW = 512
W_A = 1280
W_B = 1536
W_M = 1024
W_G = 3072
IN_WIDTH = W_A + W_B + W_M + W_G
GROUP_OFFS = (0, W_A, W_A + W_B, W_A + W_B + W_M, IN_WIDTH)

ADAM_LR = 0.001
ADAM_B1 = 0.9
ADAM_B2 = 0.999
ADAM_EPS = 1e-08
ADAM_WD = 0.01
ADAM_STEP = 10

LANES = 128
KEY_CHUNK = 2048
VMEM_LIMIT = 52 * 1024 * 1024


def _tile(n, cap, unit=LANES):
    if n <= cap:
        return n
    t = (cap // unit) * unit
    while n % t:
        t -= unit
    return t


def _params(*sem):
    return pltpu.CompilerParams(dimension_semantics=sem, vmem_limit_bytes=VMEM_LIMIT)


def _sigmoid(z):
    return 0.5 * jnp.tanh(0.5 * z) + 0.5


def _silu_and_grad(z):
    s = _sigmoid(z)
    return z * s, s * (1.0 + z * (1.0 - s))


def _dot(a, b, dims):
    return lax.dot_general(a, b, (dims, ((), ())), preferred_element_type=F32)


NN = ((1,), (0,))
NT = ((1,), (1,))
TN = ((0,), (0,))
HBM = pl.BlockSpec(memory_space=pl.ANY)


class _Exchange:
    def __init__(self, ins, out_shapes, sems, make, aliases=None):
        self.ins, self.out_shapes, self.sems, self.make = list(ins), list(out_shapes), list(sems), make
        self.aliases = dict(aliases or {})

    def start(self, in_refs, out_refs, sems):
        for cp in self.make(in_refs, out_refs, sems):
            cp.start()

    def finish(self, in_refs, out_refs, sems):
        for cp in self.make(in_refs, out_refs, sems):
            cp.wait()


def _both(a, b):
    if a is None or b is None:
        return a if b is None else b
    n_in, n_out, n_sem = len(a.ins), len(a.out_shapes), len(a.sems)

    def make(in_refs, out_refs, sems):
        return (a.make(in_refs[:n_in], out_refs[:n_out], sems[:n_sem])
                + b.make(in_refs[n_in:], out_refs[n_out:], sems[n_sem:]))

    aliases = dict(a.aliases)
    aliases.update({n_in + i: n_out + o for i, o in b.aliases.items()})
    return _Exchange(a.ins + b.ins, a.out_shapes + b.out_shapes, a.sems + b.sems, make, aliases)


def _host_call(body, ins, in_specs, out_specs, out_shape, *, name, grid, semantics, scratch=(), exchange=None):
    ins, out_shape, scratch = list(ins), list(out_shape), list(scratch)
    if exchange is None:
        res = pl.pallas_call(
            body, name=name, grid=grid, in_specs=list(in_specs), out_specs=tuple(out_specs), out_shape=tuple(out_shape),
            scratch_shapes=scratch, compiler_params=_params(*semantics))(*ins)
        return tuple(res), ()
    n_in, n_out, n_scr = len(ins), len(out_shape), len(scratch)
    x_in, x_out = len(exchange.ins), len(exchange.out_shapes)

    def carrying(*refs):
        o0 = n_in + x_in
        s0 = o0 + n_out + x_out
        c_in, c_out, c_sems = refs[n_in:o0], refs[o0 + n_out:s0], refs[s0 + n_scr:]
        ids = [pl.program_id(a) for a in range(len(grid))]
        first = functools.reduce(jnp.logical_and, [i == 0 for i in ids])
        last = functools.reduce(jnp.logical_and, [i == g - 1 for i, g in zip(ids, grid)])

        @pl.when(first)
        def _():
            exchange.start(c_in, c_out, c_sems)

        body(*refs[:n_in], *refs[o0:o0 + n_out], *refs[s0:s0 + n_scr])

        @pl.when(last)
        def _():
            exchange.finish(c_in, c_out, c_sems)

    res = pl.pallas_call(
        carrying, name=name, grid=grid, in_specs=list(in_specs) + [HBM] * x_in,
        out_specs=tuple(out_specs) + (HBM,) * x_out, out_shape=tuple(out_shape) + tuple(exchange.out_shapes),
        scratch_shapes=scratch + exchange.sems,
        input_output_aliases={n_in + i: n_out + o for i, o in exchange.aliases.items()},
        compiler_params=_params(*(["arbitrary"] * len(grid))))(*ins, *exchange.ins)
    return tuple(res[:n_out]), tuple(res[n_out:])


def _exchange_call(exchange, name):
    x_in = len(exchange.ins)

    def body(*refs):
        x_out = len(exchange.out_shapes)
        c_in, c_out, c_sems = refs[:x_in], refs[x_in:x_in + x_out], refs[x_in + x_out:]
        exchange.start(c_in, c_out, c_sems)
        exchange.finish(c_in, c_out, c_sems)

    return pl.pallas_call(
        body, name=name, in_specs=[HBM] * x_in, out_specs=tuple([HBM] * len(exchange.out_shapes)),
        out_shape=tuple(exchange.out_shapes), scratch_shapes=exchange.sems, input_output_aliases=exchange.aliases,
    )(*exchange.ins)


def _mm(a, b, mode, name):
    (m, k), (n, k2) = a.shape, (b.shape[::-1] if mode == "nn" else b.shape)
    assert k == k2 and mode in ("nn", "nt"), (a.shape, b.shape, mode)
    dims = {"nn": NN, "nt": NT}[mode]

    def body(a_ref, b_ref, o_ref):
        o_ref[...] = _dot(a_ref[...].astype(BF16), b_ref[...].astype(BF16), dims)

    return pl.pallas_call(body, name=name, out_shape=SDS((m, n), F32),
                          compiler_params=pltpu.CompilerParams(vmem_limit_bytes=VMEM_LIMIT))(a, b)


def _mm_tn(a, b, name):
    (k, m), (k2, n) = a.shape, b.shape
    assert k == k2, (a.shape, b.shape)
    tk = _tile(k, 2048, 16)
    nk = k // tk

    def body(a_ref, b_ref, o_ref, acc):
        kk = pl.program_id(0)
        part = _dot(a_ref[...].astype(BF16), b_ref[...].astype(BF16), TN)

        @pl.when(kk == 0)
        def _():
            acc[...] = part

        @pl.when(kk > 0)
        def _():
            acc[...] += part

        @pl.when(kk == nk - 1)
        def _():
            o_ref[...] = acc[...].astype(BF16)

    return pl.pallas_call(
        body, name=name, grid=(nk,),
        in_specs=[pl.BlockSpec((tk, m), lambda kk: (kk, 0)), pl.BlockSpec((tk, n), lambda kk: (kk, 0))],
        out_specs=pl.BlockSpec((m, n), lambda kk: (0, 0)), out_shape=SDS((m, n), BF16),
        scratch_shapes=[pltpu.VMEM((m, n), F32)], compiler_params=_params("arbitrary"),
    )(a, b)


def _rmsnorm_fwd(x, g, name):
    t, d = x.shape
    tm = _tile(t, 512, 8)

    def body(x_ref, g_ref, h_ref):
        xf = x_ref[...]
        r = lax.rsqrt(jnp.mean(xf * xf, axis=-1, keepdims=True) + EPS)
        h_ref[...] = (xf * r * g_ref[...]).astype(BF16)

    return pl.pallas_call(
        body, name=name, grid=(t // tm,),
        in_specs=[pl.BlockSpec((tm, d), lambda i: (i, 0)), pl.BlockSpec((1, d), lambda i: (0, 0))],
        out_specs=pl.BlockSpec((tm, d), lambda i: (i, 0)),
        out_shape=SDS((t, d), BF16), compiler_params=_params("parallel"),
    )(x, g)


def _rmsnorm_bwd_math(xf, g, dh):
    r = lax.rsqrt(jnp.mean(xf * xf, axis=-1, keepdims=True) + EPS)
    xh = xf * r
    gd = dh * g
    dx = r * (gd - xh * jnp.mean(gd * xh, axis=-1, keepdims=True))
    return dx, jnp.sum(dh * xh, axis=0, keepdims=True)


def _rmsnorm_dg(x, g, dh, name):
    t, d = x.shape
    tm = _tile(t, 512, 8)

    def body(x_ref, g_ref, dh_ref, dg_ref):
        @pl.when(pl.program_id(0) == 0)
        def _():
            dg_ref[...] = jnp.zeros_like(dg_ref)

        dg_ref[...] += _rmsnorm_bwd_math(x_ref[...], g_ref[...], dh_ref[...])[1]

    row = pl.BlockSpec((tm, d), lambda i: (i, 0))
    vec = pl.BlockSpec((1, d), lambda i: (0, 0))
    return pl.pallas_call(
        body, name=name, grid=(t // tm,), in_specs=[row, vec, row], out_specs=vec, out_shape=SDS((1, d), F32),
        compiler_params=_params("arbitrary"),
    )(x, g, dh)


def _group_rows(p):
    return slice(GROUP_OFFS[p], GROUP_OFFS[p + 1])


def _d_w_in_t(dps, h):
    t, d = h.shape
    tm = 256
    assert all(off % tm == 0 for off in GROUP_OFFS)
    tiles = [a.shape[1] // tm for a in dps]
    starts = [sum(tiles[:p]) for p in range(len(dps))]

    def body(*refs):
        a_refs, h_ref, o_ref = refs[:len(dps)], refs[len(dps)], refs[len(dps) + 1]
        i = pl.program_id(0)
        for p, a_ref in enumerate(a_refs):
            @pl.when((i >= starts[p]) & (i < starts[p] + tiles[p]))
            def _():
                o_ref[...] = _dot(a_ref[...], h_ref[...], TN).astype(BF16)

    def tile_of(p):
        return lambda i: jnp.clip(i - starts[p], 0, tiles[p] - 1)

    return pl.pallas_call(
        body, name="d_w_in", grid=(sum(tiles),),
        in_specs=[pl.BlockSpec((t, tm), lambda i, p=p: (0, tile_of(p)(i))) for p in range(len(dps))]
        + [pl.BlockSpec((t, d), lambda i: (0, 0), pipeline_mode=pl.Buffered(1))],
        out_specs=pl.BlockSpec((tm, d), lambda i: (i, 0)),
        out_shape=SDS((sum(tiles) * tm, d), BF16), compiler_params=_params("arbitrary"),
    )(*dps, h)


def _norm_in_proj(x, g, w_t, cs, sn, qg2, kg2, lng, lnb, ws, bs, kv):
    t, d = x.shape
    tm = _tile(t, 512, LANES)
    widths = [GROUP_OFFS[p + 1] - GROUP_OFFS[p] for p in range(4)]
    scale = HEAD_DIM ** -0.5

    def body(x_ref, g_ref, w_ref, cs_ref, sn_ref, qg_ref, kg_ref, lng_ref, lnb_ref, ws_ref, bs_ref, kv_ref,
             h_ref, pa_ref, pb_ref, pm_ref, pg_ref, q_ref, qT_ref, k_ref, kT_ref, vT_ref, yb_ref, ym_ref):
        xf = x_ref[...]
        r = lax.rsqrt(jnp.mean(xf * xf, axis=-1, keepdims=True) + EPS)
        hb = (xf * r * g_ref[...]).astype(BF16)
        h_ref[...] = hb
        groups = [_dot(hb, w_ref[_group_rows(p), :], NT).astype(BF16) for p in range(4)]
        for o_ref, val in zip((pa_ref, pb_ref, pm_ref, pg_ref), groups):
            o_ref[...] = val
        pa = groups[0]
        lane = lax.broadcasted_iota(jnp.int32, (tm, LANES), 1)
        c, s = cs_ref[...], sn_ref[...]

        def norm_rope(xg, g2):
            rr = lax.rsqrt(_head_sums(xg * xg, lane) * (1.0 / HEAD_DIM) + EPS)
            xn = xg * rr * g2
            return xn * c + _swap16(xn, lane) * s

        for gi in range(4):
            sl = slice(gi * LANES, (gi + 1) * LANES)
            qr = norm_rope(pa[:, sl].astype(F32), qg_ref[...]) * scale
            q_ref[:, sl] = qr.astype(BF16)
            qT_ref[sl, :] = qr.T.astype(BF16)
        kr = norm_rope(pa[:, 512:640].astype(F32), kg_ref[...])
        kT_ref[...] = kr.T.astype(BF16)
        vT_ref[...] = pa[:, 640:768].astype(F32).T.astype(BF16)
        kr = kr.astype(BF16)
        for kvh in range(A_KV_HEADS):
            k_ref[kvh] = kr[:, kvh * HEAD_DIM:(kvh + 1) * HEAD_DIM]
        _gmlp_tile(groups[1], lng_ref, lnb_ref, ws_ref, bs_ref, yb_ref)
        _mem_attn_tile(groups[2], kv_ref, ym_ref)

    row = lambda wd: pl.BlockSpec((tm, wd), lambda i: (i, 0))
    col = lambda rws: pl.BlockSpec((rws, tm), lambda i: (0, i))
    whole = lambda a: pl.BlockSpec(a.shape, lambda i: (0,) * a.ndim)
    hm = pl.BlockSpec((A_KV_HEADS, tm, HEAD_DIM), lambda i: (0, i, 0))
    return pl.pallas_call(
        body, name="norm_in_proj", grid=(t // tm,),
        in_specs=[row(d), whole(g), pl.BlockSpec(w_t.shape, lambda i: (0, 0), pipeline_mode=pl.Buffered(1)),
                  row(LANES), row(LANES)] + [whole(a) for a in (qg2, kg2, lng, lnb, ws, bs, kv)],
        out_specs=(row(d),) + tuple(row(wd) for wd in widths)
        + (row(BW), col(BW), hm, col(LANES), col(LANES), row(BW), row(BW)),
        out_shape=(SDS((t, d), BF16),) + tuple(SDS((t, wd), BF16) for wd in widths)
        + (SDS((t, BW), BF16), SDS((BW, t), BF16), SDS((A_KV_HEADS, t, HEAD_DIM), BF16),
           SDS((LANES, t), BF16), SDS((LANES, t), BF16), SDS((t, BW), BF16), SDS((t, BW), BF16)),
        compiler_params=_params("parallel"),
    )(x, g, w_t, cs, sn, qg2, kg2, lng, lnb, ws, bs, kv)


def _dh_rmsnorm_bwd(dps, w_t, x, g, dres, exchange=None):
    t, d = x.shape
    tm = _tile(t, 256, 16)
    n = len(dps)

    def body(*refs):
        dp_refs, w_ref = refs[:n], refs[n]
        x_ref, g_ref, dres_ref, dx_ref, dg_ref = refs[n + 1:]
        dh = None
        for p, dp_ref in enumerate(dp_refs):
            part = _dot(dp_ref[...], w_ref[_group_rows(p), :], NN)
            dh = part if dh is None else dh + part
        dx, dg = _rmsnorm_bwd_math(x_ref[...], g_ref[...], dh)
        dx_ref[...] = dx + dres_ref[...]

        @pl.when(pl.program_id(0) == 0)
        def _():
            dg_ref[...] = jnp.zeros_like(dg_ref)

        dg_ref[...] += dg

    row = pl.BlockSpec((tm, d), lambda i: (i, 0))
    vec = pl.BlockSpec((1, d), lambda i: (0, 0))
    return _host_call(
        body, list(dps) + [w_t, x, g, dres],
        [pl.BlockSpec((tm, a.shape[1]), lambda i: (i, 0)) for a in dps]
        + [pl.BlockSpec(w_t.shape, lambda i: (0, 0), pipeline_mode=pl.Buffered(1)), row, vec, row],
        (row, vec), (SDS((t, d), F32), SDS((1, d), F32)),
        name="dh_rmsnorm_bwd", grid=(t // tm,), semantics=("arbitrary",), exchange=exchange)


def _rope_tables(t):
    rows = t // GRID_W
    row = jnp.repeat(jnp.arange(rows, dtype=F32), GRID_W)
    col = jnp.tile(jnp.arange(GRID_W, dtype=F32), rows)
    n_freq = HEAD_DIM // 4
    inv = ROPE_THETA ** (-jnp.arange(n_freq, dtype=F32) / n_freq)
    ang = jnp.stack([row[:, None] * inv, col[:, None] * inv], axis=1)
    cos, sin = jnp.cos(ang), jnp.sin(ang)
    c64 = jnp.concatenate([cos[:, 0], cos[:, 0], cos[:, 1], cos[:, 1]], axis=-1)
    s64 = jnp.concatenate([-sin[:, 0], sin[:, 0], -sin[:, 1], sin[:, 1]], axis=-1)
    return jnp.tile(c64, (1, 2)), jnp.tile(s64, (1, 2))


def _head_sums(v, lane):
    lo = jnp.sum(jnp.where(lane < HEAD_DIM, v, 0.0), axis=-1, keepdims=True)
    hi = jnp.sum(jnp.where(lane < HEAD_DIM, 0.0, v), axis=-1, keepdims=True)
    return jnp.where(lane < HEAD_DIM, lo, hi)


def _swap16(v, lane):
    return jnp.where((lane % 32) < 16, pltpu.roll(v, LANES - 16, 1), pltpu.roll(v, 16, 1))


def _attn_fwd(qT, k, vT, pa, exchange=None):
    t = qT.shape[1]
    tq = _tile(t, 256, LANES)
    grp = A_HEADS // A_KV_HEADS

    ck = _tile(t, KEY_CHUNK, LANES)

    def body(qT_ref, k_ref, vT_ref, pa_ref, o_ref, lse_ref, y_ref):
        def finish_pair(pair, o_pair):
            o2 = jnp.concatenate(o_pair, axis=0).T
            cols = slice(pair * LANES, (pair + 1) * LANES)
            o_ref[:, cols] = o2
            z = pa_ref[:, 768 + pair * LANES:768 + (pair + 1) * LANES].astype(F32)
            y_ref[:, cols] = (o2 * (z * _sigmoid(z))).astype(BF16)

        n_c = t // ck
        items = [(h, c) for h in range(A_HEADS) for c in range(n_c)]

        def scores(h, c):
            return _dot(k_ref[h // grp, c * ck:(c + 1) * ck, :], qT_ref[h * HEAD_DIM:(h + 1) * HEAD_DIM, :], NN)

        def weighted_values(h, c, pT):
            kvh = h // grp
            return _dot(vT_ref[kvh * HEAD_DIM:(kvh + 1) * HEAD_DIM, c * ck:(c + 1) * ck], pT, NN)

        o_pair, lse_rows, sT, pending, acc = [], [], scores(*items[0]), None, None
        for i in range(len(items) + 1):
            item = items[i] if i < len(items) else None
            sT_next = scores(*items[i + 1]) if i + 1 < len(items) else None
            if pending is not None:
                (ph, pc, pT, alpha, l_run) = pending
                o_c = weighted_values(ph, pc, pT)
                acc = o_c if alpha is None else acc * alpha + o_c
                if pc == n_c - 1:
                    o_pair.append(acc / l_run)
                    if ph % 2:
                        finish_pair(ph // 2, o_pair)
                        o_pair = []
                pending = None
            if item is not None:
                h, c = item
                m_c = jnp.max(sT, axis=0, keepdims=True)
                if c == 0:
                    m_new, alpha = m_c, None
                else:
                    m_new = jnp.maximum(m, m_c)
                    alpha = jnp.exp(m - m_new)
                pT = jnp.exp(sT - m_new)
                l_c = jnp.sum(pT, axis=0, keepdims=True)
                l = l_c if c == 0 else l * alpha + l_c
                m = m_new
                if c == n_c - 1:
                    lse_rows.append(m + jnp.log(l))
                pending = (h, c, pT.astype(BF16), alpha, l)
            sT = sT_next
        lse_cols = jnp.concatenate(lse_rows, axis=0).T
        for kvh in range(A_KV_HEADS):
            lse_ref[kvh] = lse_cols[:, kvh * grp:(kvh + 1) * grp]

    row = lambda w: pl.BlockSpec((tq, w), lambda i: (i, 0))
    return _host_call(
        body, (qT, k, vT, pa),
        [pl.BlockSpec((BW, tq), lambda i: (0, i)), pl.BlockSpec((A_KV_HEADS, t, HEAD_DIM), lambda i: (0, 0, 0)),
         pl.BlockSpec((A_KV_HEADS * HEAD_DIM, t), lambda i: (0, 0)), row(W_A)],
        (row(BW), pl.BlockSpec((A_KV_HEADS, tq, grp), lambda i: (0, i, 0)), row(BW)),
        (SDS((t, BW), F32), SDS((A_KV_HEADS, t, grp), F32), SDS((t, BW), BF16)),
        name="attn_fwd", grid=(t // tq,), semantics=("parallel",), exchange=exchange)


def _attn_bwd(q, qT, k, kT, vT, pa, o, lse, dy, exchange=None):
    t = q.shape[0]
    tq = _tile(t, 256, LANES)
    grp = A_HEADS // A_KV_HEADS
    gw = grp * HEAD_DIM

    def body(q_ref, qT_ref, k_ref, kT_ref, vT_ref, z_ref, o_ref, lse_ref, dy_ref, dq_ref, dkT_ref, dvT_ref):
        @pl.when(pl.program_id(1) == 0)
        def _():
            dkT_ref[...] = jnp.zeros_like(dkT_ref)
            dvT_ref[...] = jnp.zeros_like(dvT_ref)

        z = z_ref[...].astype(F32)
        do = dy_ref[...] * (z * _sigmoid(z))
        doo = do * o_ref[...]
        doT = do.T
        kk, kT, vT = k_ref[...], kT_ref[...], vT_ref[...]
        heads = [slice(j * HEAD_DIM, (j + 1) * HEAD_DIM) for j in range(grp)]

        def scores(hs):
            return _dot(q_ref[:, hs], kT, NN), _dot(do[:, hs].astype(BF16), vT, NN)

        s, dp = scores(heads[0])
        for j, hs in enumerate(heads):
            nxt = scores(heads[j + 1]) if j + 1 < grp else None
            delta = jnp.sum(doo[:, hs], axis=-1, keepdims=True)
            p = jnp.exp(s - lse_ref[:, j:j + 1])
            ds_b = (p * (dp - delta)).astype(BF16)
            dq_ref[:, hs] = _dot(ds_b, kk, NN)
            dkT_ref[...] += _dot(qT_ref[hs, :], ds_b, NN)
            dvT_ref[...] += _dot(doT[hs, :].astype(BF16), p.astype(BF16), NN)
            if nxt is not None:
                s, dp = nxt

    grp_blk = pl.BlockSpec((tq, gw), lambda g, i: (i, g))
    kvT_blk = pl.BlockSpec((HEAD_DIM, t), lambda g, i: (g, 0))
    acc_blk = pl.BlockSpec((None, HEAD_DIM, t), lambda g, i: (g, 0, 0))
    return _host_call(
        body, (q, qT, k, kT, vT, pa, o, lse, dy),
        [grp_blk, pl.BlockSpec((gw, tq), lambda g, i: (g, i)), pl.BlockSpec((None, t, HEAD_DIM), lambda g, i: (g, 0, 0)),
         kvT_blk, kvT_blk, pl.BlockSpec((tq, gw), lambda g, i: (i, 768 // gw + g)), grp_blk,
         pl.BlockSpec((None, tq, grp), lambda g, i: (g, i, 0)), grp_blk],
        (grp_blk, acc_blk, acc_blk),
        (SDS((t, BW), F32), SDS((A_KV_HEADS, HEAD_DIM, t), F32), SDS((A_KV_HEADS, HEAD_DIM, t), F32)),
        name="attn_bwd", grid=(A_KV_HEADS, t // tq), semantics=("arbitrary", "arbitrary"), exchange=exchange)


def _attn_prep_bwd(pa, cs, sn, qg2, kg2, dq, dkT, dvT, dy, o, exchange=None):
    t = pa.shape[0]
    tq = _tile(t, 512, LANES)
    scale = HEAD_DIM ** -0.5

    def body(pa_ref, cs_ref, sn_ref, qg_ref, kg_ref, dq_ref, dkT_ref, dvT_ref, dy_ref, o_ref, dpa_ref, dqg_ref, dkg_ref):
        lane = lax.broadcasted_iota(jnp.int32, (tq, LANES), 1)
        c, s = cs_ref[...], sn_ref[...]

        @pl.when(pl.program_id(0) == 0)
        def _():
            dqg_ref[...] = jnp.zeros_like(dqg_ref)
            dkg_ref[...] = jnp.zeros_like(dkg_ref)

        def norm_rope_bwd(xg, g2, dout):
            r = lax.rsqrt(_head_sums(xg * xg, lane) * (1.0 / HEAD_DIM) + EPS)
            xh = xg * r
            dxn = dout * c + _swap16(dout * s, lane)
            gd = dxn * g2
            dx = r * (gd - xh * (_head_sums(gd * xh, lane) * (1.0 / HEAD_DIM)))
            return dx, jnp.sum(dxn * xh, axis=0, keepdims=True)

        for gi in range(4):
            sl = slice(gi * LANES, (gi + 1) * LANES)
            dx, dg = norm_rope_bwd(pa_ref[:, sl].astype(F32), qg_ref[...], dq_ref[:, sl] * scale)
            dpa_ref[:, sl] = dx.astype(BF16)
            dqg_ref[...] += dg
        dx, dg = norm_rope_bwd(pa_ref[:, 512:640].astype(F32), kg_ref[...], dkT_ref[...].T)
        dpa_ref[:, 512:640] = dx.astype(BF16)
        dkg_ref[...] += dg
        dpa_ref[:, 640:768] = dvT_ref[...].T.astype(BF16)
        z = pa_ref[:, 768:1280].astype(F32)
        _, dsilu = _silu_and_grad(z)
        dpa_ref[:, 768:1280] = (dy_ref[...] * o_ref[...] * dsilu).astype(BF16)

    row = lambda w: pl.BlockSpec((tq, w), lambda i: (i, 0))
    col = pl.BlockSpec((LANES, tq), lambda i: (0, i))
    vec = pl.BlockSpec((1, LANES), lambda i: (0, 0))
    return _host_call(
        body, (pa, cs, sn, qg2, kg2, dq, dkT, dvT, dy, o),
        [row(W_A), row(LANES), row(LANES), vec, vec, row(BW), col, col, row(BW), row(BW)],
        (row(W_A), vec, vec), (SDS((t, W_A), BF16), SDS((1, LANES), F32), SDS((1, LANES), F32)),
        name="attn_prep_bwd", grid=(t // tq,), semantics=("arbitrary",), exchange=exchange)


def _layer_norm(v, g, b):
    mu = jnp.mean(v, axis=-1, keepdims=True)
    xc = v - mu
    rs = lax.rsqrt(jnp.mean(xc * xc, axis=-1, keepdims=True) + EPS)
    xh = xc * rs
    return xh * g + b, xh, rs


def _gmlp_tile(pb, g_ref, b_ref, ws_ref, bs_ref, y_ref):
    vln, _, _ = _layer_norm(pb[:, BW:2 * BW].astype(F32), g_ref[...], b_ref[...])
    vb = vln.astype(BF16)
    for gi in range(B_GROUPS):
        w = ws_ref[gi].astype(BF16)
        cs_ = slice(gi * CHUNK, (gi + 1) * CHUNK)
        for n in range(pb.shape[0] // CHUNK):
            rs_ = slice(n * CHUNK, (n + 1) * CHUNK)
            mixed = _dot(w, vb[rs_, cs_], NN) + bs_ref[gi]
            z = pb[rs_, 2 * BW + gi * CHUNK:2 * BW + (gi + 1) * CHUNK].astype(F32)
            y_ref[rs_, cs_] = (pb[rs_, cs_].astype(F32) * mixed * (z * _sigmoid(z))).astype(BF16)


def _gmlp_bwd(pb, lng, lnb, ws, bs, dy):
    t = pb.shape[0]
    tb = _tile(t, 256, CHUNK)

    def body(pb_ref, g_ref, b_ref, ws_ref, bs_ref, dy_ref, dpb_ref, dws_ref, dbs_ref, dg_ref, db_ref, dvln_ref):
        @pl.when(pl.program_id(0) == 0)
        def _():
            dws_ref[...] = jnp.zeros_like(dws_ref)
            dbs_ref[...] = jnp.zeros_like(dbs_ref)
            dg_ref[...] = jnp.zeros_like(dg_ref)
            db_ref[...] = jnp.zeros_like(db_ref)

        vln, xh, rs = _layer_norm(pb_ref[:, BW:2 * BW].astype(F32), g_ref[...], b_ref[...])
        vb = vln.astype(BF16)
        for gi in range(B_GROUPS):
            w = ws_ref[gi].astype(BF16)
            cs_ = slice(gi * CHUNK, (gi + 1) * CHUNK)
            for n in range(tb // CHUNK):
                rs_ = slice(n * CHUNK, (n + 1) * CHUNK)
                vbc = vb[rs_, cs_]
                mixed = _dot(w, vbc, NN) + bs_ref[gi]
                zs = slice(2 * BW + gi * CHUNK, 2 * BW + (gi + 1) * CHUNK)
                z = pb_ref[rs_, zs].astype(F32)
                u = pb_ref[rs_, cs_].astype(F32)
                sil, dsil = _silu_and_grad(z)
                dyc = dy_ref[rs_, cs_]
                dmixed = dyc * u * sil
                dpb_ref[rs_, cs_] = (dyc * mixed * sil).astype(BF16)
                dpb_ref[rs_, zs] = (dyc * u * mixed * dsil).astype(BF16)
                dmb = dmixed.astype(BF16)
                dws_ref[gi] += _dot(dmb, vbc, NT)
                dbs_ref[gi] += jnp.sum(dmixed, axis=-1, keepdims=True)
                dvln_ref[rs_, cs_] = _dot(w, dmb, TN)
        dvln = dvln_ref[...]
        dg_ref[...] += jnp.sum(dvln * xh, axis=0, keepdims=True)
        db_ref[...] += jnp.sum(dvln, axis=0, keepdims=True)
        gd = dvln * g_ref[...]
        dv = rs * (gd - jnp.mean(gd, axis=-1, keepdims=True) - xh * jnp.mean(gd * xh, axis=-1, keepdims=True))
        dpb_ref[:, BW:2 * BW] = dv.astype(BF16)

    vec = pl.BlockSpec((1, BW), lambda i: (0, 0))
    wsb = pl.BlockSpec((B_GROUPS, CHUNK, CHUNK), lambda i: (0, 0, 0))
    bsb = pl.BlockSpec((B_GROUPS, CHUNK, 1), lambda i: (0, 0, 0))
    return pl.pallas_call(
        body, name="gmlp_bwd", grid=(t // tb,),
        in_specs=[pl.BlockSpec((tb, W_B), lambda i: (i, 0)), vec, vec, wsb, bsb, pl.BlockSpec((tb, BW), lambda i: (i, 0))],
        out_specs=(pl.BlockSpec((tb, W_B), lambda i: (i, 0)), wsb, bsb, vec, vec),
        out_shape=(SDS((t, W_B), BF16), SDS((B_GROUPS, CHUNK, CHUNK), F32), SDS((B_GROUPS, CHUNK, 1), F32),
                   SDS((1, BW), F32), SDS((1, BW), F32)),
        scratch_shapes=[pltpu.VMEM((tb, BW), F32)],
        compiler_params=_params("arbitrary"),
    )(pb, lng, lnb, ws, bs, dy)


def _mem_scores(pm_ref, kv_ref, h):
    hs = slice(h * M_HEAD_DIM, (h + 1) * M_HEAD_DIM)
    return _dot(pm_ref[:, hs].astype(BF16), kv_ref[:, hs].astype(BF16), NT) * (M_HEAD_DIM ** -0.5)


def _softmax_rows(s):
    e = jnp.exp(s - jnp.max(s, axis=-1, keepdims=True))
    return e / jnp.sum(e, axis=-1, keepdims=True)


def _mem_attn_tile(pm, kv_ref, y_ref):
    s_next = _mem_scores(pm, kv_ref, 0)
    for h in range(M_HEADS):
        hs = slice(h * M_HEAD_DIM, (h + 1) * M_HEAD_DIM)
        vh = kv_ref[:, BW + h * M_HEAD_DIM:BW + (h + 1) * M_HEAD_DIM].astype(BF16)
        s, s_next = s_next, (_mem_scores(pm, kv_ref, h + 1) if h + 1 < M_HEADS else None)
        p = _softmax_rows(s)
        o = _dot(p.astype(BF16), vh, NN)
        z = pm[:, BW + h * M_HEAD_DIM:BW + (h + 1) * M_HEAD_DIM].astype(F32)
        y_ref[:, hs] = (o * (z * _sigmoid(z))).astype(BF16)


def _mem_attn_bwd(pm, kv, dy):
    t = pm.shape[0]
    tq = _tile(t, 512, 8)
    ml = kv.shape[0]
    scale = M_HEAD_DIM ** -0.5

    def body(pm_ref, kv_ref, dy_ref, dpm_ref, dkv_ref):
        @pl.when(pl.program_id(0) == 0)
        def _():
            dkv_ref[...] = jnp.zeros_like(dkv_ref)

        s_next = _mem_scores(pm_ref, kv_ref, 0)
        for h in range(M_HEADS):
            hs = slice(h * M_HEAD_DIM, (h + 1) * M_HEAD_DIM)
            zs = slice(BW + h * M_HEAD_DIM, BW + (h + 1) * M_HEAD_DIM)
            kh = kv_ref[:, hs].astype(BF16)
            vh = kv_ref[:, zs].astype(BF16)
            qh = pm_ref[:, hs].astype(BF16)
            s, s_next = s_next, (_mem_scores(pm_ref, kv_ref, h + 1) if h + 1 < M_HEADS else None)
            p = _softmax_rows(s)
            pb = p.astype(BF16)
            o = _dot(pb, vh, NN)
            sil, dsil = _silu_and_grad(pm_ref[:, zs].astype(F32))
            dyh = dy_ref[:, hs]
            do = dyh * sil
            dpm_ref[:, zs] = (dyh * o * dsil).astype(BF16)
            delta = jnp.sum(do * o, axis=-1, keepdims=True)
            do_b = do.astype(BF16)
            dp = _dot(do_b, vh, NT)
            dr_b = (p * (dp - delta) * scale).astype(BF16)
            dpm_ref[:, hs] = _dot(dr_b, kh, NN).astype(BF16)
            dkv_ref[:, hs] += _dot(dr_b, qh, TN)
            dkv_ref[:, zs] += _dot(pb, do_b, TN)

    kvb = pl.BlockSpec((ml, 2 * BW), lambda i: (0, 0))
    return pl.pallas_call(
        body, name="mem_attn_bwd", grid=(t // tq,),
        in_specs=[pl.BlockSpec((tq, W_M), lambda i: (i, 0)), kvb, pl.BlockSpec((tq, BW), lambda i: (i, 0))],
        out_specs=(pl.BlockSpec((tq, W_M), lambda i: (i, 0)), kvb),
        out_shape=(SDS((t, W_M), BF16), SDS((ml, 2 * BW), F32)),
        compiler_params=_params("arbitrary"),
    )(pm, kv, dy)


def _merge_fwd(ya, yb, ym, pg, wbr, x, w_out, exchange=None):
    t = ya.shape[0]
    tm = _tile(t, 512, 8)

    def body(ya_ref, yb_ref, ym_ref, pg_ref, w_ref, x_ref, wo_ref, m_ref, xn_ref):
        ups = [_dot(y_ref[...], w_ref[n], NN) for n, y_ref in enumerate((ya_ref, yb_ref, ym_ref))]
        acc = None
        for n, up in enumerate(ups):
            term = _sigmoid(pg_ref[:, n * D_MODEL:(n + 1) * D_MODEL].astype(F32)) * up
            acc = term if acc is None else acc + term
        merged = acc.astype(BF16)
        m_ref[...] = merged
        xn_ref[...] = x_ref[...] + _dot(merged, wo_ref[...], NN)

    yb_spec = pl.BlockSpec((tm, BW), lambda i: (i, 0))
    row = pl.BlockSpec((tm, D_MODEL), lambda i: (i, 0))
    return _host_call(
        body, (ya, yb, ym, pg, wbr, x, w_out),
        [yb_spec, yb_spec, yb_spec, pl.BlockSpec((tm, W_G), lambda i: (i, 0)),
         pl.BlockSpec((3, BW, D_MODEL), lambda i: (0, 0, 0)), row, pl.BlockSpec(w_out.shape, lambda i: (0, 0))],
        (row, row), (SDS((t, D_MODEL), BF16), SDS((t, D_MODEL), F32)),
        name="merge_fwd", grid=(t // tm,), semantics=("parallel",), exchange=exchange)


def _merge_bwd(ya, yb, ym, pg, wbr, dx_out, w_out, exchange=None):
    t = ya.shape[0]
    tm = _tile(t, 512, 16)
    n_steps = t // tm
    shard = D_MODEL // N_DEV

    def body(ya_ref, yb_ref, ym_ref, pg_ref, w_ref, dx_ref, wo_ref, dya_ref, dyb_ref, dym_ref, dpg_ref, dwd_ref, dw_ref):
        @pl.when(pl.program_id(0) == 0)
        def _():
            dw_ref[...] = jnp.zeros_like(dw_ref)

        dmf = _dot(dx_ref[...].astype(BF16), wo_ref[...], NT)
        branches = ((ya_ref, dya_ref), (yb_ref, dyb_ref), (ym_ref, dym_ref))

        def gate(n):
            gt = _sigmoid(pg_ref[:, n * D_MODEL:(n + 1) * D_MODEL].astype(F32))
            return gt, (dmf * gt).astype(BF16)

        gt, dup = gate(0)
        for n, (y_ref, dy_ref) in enumerate(branches):
            nxt = gate(n + 1) if n + 1 < len(branches) else None
            y, w = y_ref[...], w_ref[n]
            dy_ref[...] = _dot(dup, w, NT)
            dw_ref[n] += _dot(y, dup, TN)
            up = _dot(y, w, NN)
            dpg_ref[:, n * D_MODEL:(n + 1) * D_MODEL] = (dmf * up * gt * (1.0 - gt)).astype(BF16)
            if nxt is not None:
                gt, dup = nxt

        @pl.when(pl.program_id(0) == n_steps - 1)
        def _():
            for d in range(N_DEV):
                dwd_ref[d] = dw_ref[:, :, d * shard:(d + 1) * shard].astype(BF16)

    y_spec = pl.BlockSpec((tm, BW), lambda i: (i, 0))
    w_spec = pl.BlockSpec((3, BW, D_MODEL), lambda i: (0, 0, 0))
    return _host_call(
        body, (ya, yb, ym, pg, wbr, dx_out, w_out),
        [y_spec, y_spec, y_spec, pl.BlockSpec((tm, W_G), lambda i: (i, 0)), w_spec,
         pl.BlockSpec((tm, D_MODEL), lambda i: (i, 0)), pl.BlockSpec(w_out.shape, lambda i: (0, 0))],
        (y_spec, y_spec, y_spec, pl.BlockSpec((tm, W_G), lambda i: (i, 0)),
         pl.BlockSpec((N_DEV, 3, BW, shard), lambda i: (0, 0, 0, 0))),
        (SDS((t, BW), F32), SDS((t, BW), F32), SDS((t, BW), F32), SDS((t, W_G), BF16), SDS((N_DEV, 3, BW, shard), BF16)),
        name="merge_bwd", grid=(n_steps,), semantics=("arbitrary",), scratch=[pltpu.VMEM((3, BW, D_MODEL), F32)],
        exchange=exchange)


def _loss_head(x, g, target):
    t, d = x.shape
    tm = _tile(t, 512, 8)

    def body(x_ref, g_ref, t_ref, loss_ref, dx_ref, dg_ref):
        @pl.when(pl.program_id(0) == 0)
        def _():
            loss_ref[...] = jnp.zeros_like(loss_ref)
            dg_ref[...] = jnp.zeros_like(dg_ref)

        xf = x_ref[...]
        r = lax.rsqrt(jnp.mean(xf * xf, axis=-1, keepdims=True) + EPS)
        xh = xf * r
        err = xh * g_ref[...] - t_ref[...]
        per_tok = jnp.mean(err * err, axis=-1, keepdims=True)
        loss_ref[...] += 0.5 * jnp.sum(per_tok, axis=0, keepdims=True)
        dy = err * (1.0 / d)
        gd = dy * g_ref[...]
        dx_ref[...] = r * (gd - xh * jnp.mean(gd * xh, axis=-1, keepdims=True))
        dg_ref[...] += jnp.sum(dy * xh, axis=0, keepdims=True)

    row = pl.BlockSpec((tm, d), lambda i: (i, 0))
    vec = pl.BlockSpec((1, d), lambda i: (0, 0))
    return pl.pallas_call(
        body, name="loss_head", grid=(t // tm,),
        in_specs=[row, vec, row],
        out_specs=(pl.BlockSpec((1, 1), lambda i: (0, 0)), row, vec),
        out_shape=(SDS((1, 1), F32), SDS((t, d), F32), SDS((1, d), F32)),
        compiler_params=_params("arbitrary"),
    )(x, g, target)


def _layer_fwd(x, mem, w, tabs, next_shards=None):
    cs, sn = tabs
    riding = next_shards is not None
    memn = _rmsnorm_fwd(mem, w["mem_g"], "mem_rmsnorm_fwd")
    kv = _mm(memn, w["w_kv"], "nn", "mem_kv")
    h, pa, pb, pm, pg, q, qT, k, kT, vT, yb, ym = _norm_in_proj(
        x, w["norm_g"], w["w_in_t"], cs, sn, w["qg2"], w["kg2"], w["ln_g"], w["ln_b"], w["w_s"], w["b_s"], kv)
    (o, lse, ya), gathered = _attn_fwd(qT, k, vT, pa, exchange=_gather_first_hop(next_shards) if riding else None)
    (merged, x_next), gathered = _merge_fwd(ya, yb, ym, pg, w["w_br"], x, w["w_out"],
                                            exchange=_gather_forward(gathered) if riding else None)
    saved = dict(x=x, h=h, pa=pa, pb=pb, pm=pm, pg=pg, q=q, qT=qT, k=k, kT=kT, vT=vT, o=o, lse=lse, ya=ya, yb=yb, ym=ym,
                 memn=memn, kv=kv, merged=merged)
    return x_next, saved, gathered


def _layer_bwd(dx_out, mem, w, s, tabs, pending=None, core=None, scatter_own=False):
    cs, sn = tabs
    t = dx_out.shape[0]
    riding = pending is not None
    by_owner = lambda a: a.reshape((N_DEV // 2, 2, -1) + a.shape[-2:])
    row_shards = lambda a: by_owner(a.reshape(N_DEV, a.shape[0] // N_DEV, a.shape[1]))
    d_w_out = _mm_tn(s["merged"], dx_out, "d_w_out")
    (dya, dyb, dym, dpg, d_w_br), recv = _merge_bwd(s["ya"], s["yb"], s["ym"], s["pg"], w["w_br"], dx_out, w["w_out"],
                                                    exchange=_scatter_to_sibling(pending) if riding else None)
    parts = _pair_sums(BIG, pending, recv, core) if riding else None
    grads = dict(w_br=by_owner(d_w_br), w_out=row_shards(d_w_out))
    early = [grads[n] for n in BIG[2:]]
    (dq, dkT, dvT), landed = _attn_bwd(
        s["q"], s["qT"], s["k"], s["kT"], s["vT"], s["pa"], s["o"], s["lse"], dya,
        exchange=_both(_scatter_to_chips(parts) if riding else None, _scatter_to_sibling(early) if scatter_own else None))
    from_chips, early_recv = (landed[:len(BIG)], landed[len(BIG):]) if riding else (None, landed)
    early_parts = _pair_sums(BIG[2:], early, early_recv, core) if scatter_own else None
    (dpa, d_qg2, d_kg2), early_from_chips = _attn_prep_bwd(
        s["pa"], cs, sn, w["qg2"], w["kg2"], dq, dkT.reshape(-1, t), dvT.reshape(-1, t), dya, s["o"],
        exchange=_scatter_to_chips(early_parts) if scatter_own else None)
    dpb, d_w_s, d_b_s, d_ln_g, d_ln_b = _gmlp_bwd(s["pb"], w["ln_g"], w["ln_b"], w["w_s"], w["b_s"], dyb)
    dpm, dkv = _mem_attn_bwd(s["pm"], s["kv"], dym)
    d_w_kv = _mm_tn(s["memn"], dkv, "d_w_kv")
    dmemn = _mm(dkv, w["w_kv"], "nt", "d_memn")
    d_mem_g = _rmsnorm_dg(mem, w["mem_g"], dmemn, "mem_rmsnorm_bwd")
    dps = (dpa, dpb, dpm, dpg)
    grads.update(w_in=row_shards(_d_w_in_t(dps, s["h"])), w_mem_kv=row_shards(d_w_kv))
    own = None
    if scatter_own:
        late = [grads[n] for n in BIG[:2]]
        late_parts = _pair_sums(BIG[:2], late, _exchange_call(_scatter_to_sibling(late), "rs_sibling_swap"), core)
        own = _scatter_to_chips(late_parts)
    (dx_in, d_norm_g), late_from_chips = _dh_rmsnorm_bwd(dps, w["w_in_t"], s["x"], w["norm_g"], dx_out, exchange=own)
    if scatter_own:
        own_parts, own_from_chips = late_parts + early_parts, tuple(late_from_chips) + tuple(early_from_chips)
    grads.update(norm_g=d_norm_g[0], q_norm_g=d_qg2[0, :HEAD_DIM] + d_qg2[0, HEAD_DIM:],
                 k_norm_g=d_kg2[0, :HEAD_DIM] + d_kg2[0, HEAD_DIM:], sg_ln_g=d_ln_g[0], sg_ln_b=d_ln_b[0],
                 w_s=d_w_s, b_s=d_b_s[:, :, 0], mem_norm_g=d_mem_g[0])
    return (dx_in, grads, ((parts, from_chips) if riding else None),
            ((own_parts, own_from_chips) if scatter_own else None))


def _layer_weights(l, w_in_t, w_kv, w_br, w_out, small):
    tile2 = lambda g: jnp.tile(g.reshape(1, -1), (1, 2))
    return dict(
        w_in_t=w_in_t, w_kv=w_kv, w_br=w_br, w_out=w_out,
        norm_g=small["norm_g"][l].reshape(1, -1), qg2=tile2(small["q_norm_g"][l]), kg2=tile2(small["k_norm_g"][l]),
        ln_g=small["sg_ln_g"][l].reshape(1, -1), ln_b=small["sg_ln_b"][l].reshape(1, -1),
        w_s=small["w_s"][l], b_s=small["b_s"][l][:, :, None], mem_g=small["mem_norm_g"][l].reshape(1, -1))


def _position():
    x, y, c = lax.axis_index("x"), lax.axis_index("y"), lax.axis_index("c")
    return x, y, c, [(1 - x, y), (x, 1 - y), (1 - x, 1 - y)]


def _gather_first_hop(shards):
    n = len(shards)

    def make(x_refs, out_refs, sems):
        send_sems, recv_sems, local_sems = sems
        x, y, c, chips = _position()
        me = 4 * x + 2 * y + c
        peers = [(x, y, 1 - c)] + [(cx, cy, c) for cx, cy in chips]
        copies = [pltpu.make_async_copy(x_refs[t], out_refs[t].at[me], local_sems.at[t]) for t in range(n)]
        copies += [pltpu.make_async_remote_copy(
            src_ref=x_refs[t], dst_ref=out_refs[t].at[me], send_sem=send_sems.at[t, k], recv_sem=recv_sems.at[t, k],
            device_id=peer, device_id_type=MESH_ID) for t in range(n) for k, peer in enumerate(peers)]
        return copies

    return _Exchange(shards, [SDS((N_DEV,) + a.shape, a.dtype) for a in shards],
                     [pltpu.SemaphoreType.DMA((n, 4)), pltpu.SemaphoreType.DMA((n, 4)), pltpu.SemaphoreType.DMA((n,))], make)


def _gather_forward(gathered):
    n = len(gathered)

    def make(in_refs, out_refs, sems):
        send_sems, recv_sems = sems
        x, y, c, chips = _position()
        return [pltpu.make_async_remote_copy(
            src_ref=in_refs[t].at[4 * cx + 2 * cy + c], dst_ref=out_refs[t].at[4 * cx + 2 * cy + c],
            send_sem=send_sems.at[t, j], recv_sem=recv_sems.at[t, j], device_id=(x, y, 1 - c), device_id_type=MESH_ID)
            for t in range(n) for j, (cx, cy) in enumerate(chips)]

    return _Exchange(gathered, [SDS(a.shape, a.dtype) for a in gathered],
                     [pltpu.SemaphoreType.DMA((n, 3)), pltpu.SemaphoreType.DMA((n, 3))], make,
                     aliases={t: t for t in range(n)})


def _all_gather(shards, name):
    return _exchange_call(_gather_forward(_exchange_call(_gather_first_hop(shards), name + "_hop1")), name + "_hop2")


def _assemble_w_br(gathered):
    _, nb, rows, shard = gathered.shape

    def body(g_ref, o_ref):
        for d in range(N_DEV):
            o_ref[:, :, d * shard:(d + 1) * shard] = g_ref[d]

    return pl.pallas_call(
        body, name="assemble_w_br", out_shape=SDS((nb, rows, N_DEV * shard), gathered.dtype),
        compiler_params=pltpu.CompilerParams(vmem_limit_bytes=VMEM_LIMIT),
    )(gathered)


def _scatter_to_sibling(dests):
    n = len(dests)

    def make(d_refs, recv_refs, sems):
        send_sems, recv_sems = sems
        x, y, c, _ = _position()
        return [pltpu.make_async_remote_copy(
            src_ref=d_refs[t].at[:, 1 - c], dst_ref=recv_refs[t], send_sem=send_sems.at[t],
            recv_sem=recv_sems.at[t], device_id=(x, y, 1 - c), device_id_type=MESH_ID) for t in range(n)]

    return _Exchange(dests, [SDS(a.shape[:1] + a.shape[2:], a.dtype) for a in dests],
                     [pltpu.SemaphoreType.DMA((n,)), pltpu.SemaphoreType.DMA((n,))], make)


def _scatter_to_chips(parts):
    n = len(parts)

    def make(p_refs, recv_refs, sems):
        send_sems, recv_sems = sems
        _, _, c, chips = _position()
        return [pltpu.make_async_remote_copy(
            src_ref=p_refs[t].at[2 * cx + cy], dst_ref=recv_refs[t].at[k], send_sem=send_sems.at[t, k],
            recv_sem=recv_sems.at[t, k], device_id=(cx, cy, c), device_id_type=MESH_ID)
            for t in range(n) for k, (cx, cy) in enumerate(chips)]

    return _Exchange(parts, [SDS((3,) + a.shape[1:], a.dtype) for a in parts],
                     [pltpu.SemaphoreType.DMA((n, 3)), pltpu.SemaphoreType.DMA((n, 3))], make)


def _pair_sums(names, dests, recv, core):
    return [_pair_sum(d, r, core, "rs_pair_sum_" + n) for n, d, r in zip(names, dests, recv)]


def _pair_sum(dest, recv, core, name):
    _, _, na, r, cdim = dest.shape

    def body(core_ref, a_ref, b_ref, o_ref):
        o_ref[...] = (a_ref[...].astype(F32) + b_ref[...].astype(F32)).astype(o_ref.dtype)

    blk = pl.BlockSpec((None, na, r, cdim), lambda j, core_ref: (j, 0, 0, 0))
    return pl.pallas_call(
        body, name=name, out_shape=SDS(recv.shape, BF16),
        grid_spec=pltpu.PrefetchScalarGridSpec(
            num_scalar_prefetch=1, grid=(4,),
            in_specs=[pl.BlockSpec((None, None, na, r, cdim), lambda j, core_ref: (j, core_ref[0], 0, 0, 0)), blk],
            out_specs=blk),
        compiler_params=_params("parallel"),
    )(core, dest, recv)


def _adamw_math(w, g, m, v):
    m = ADAM_B1 * m + (1.0 - ADAM_B1) * g
    v = ADAM_B2 * v + (1.0 - ADAM_B2) * (g * g)
    m_hat = m / (1.0 - ADAM_B1 ** ADAM_STEP)
    v_hat = v / (1.0 - ADAM_B2 ** ADAM_STEP)
    delta = -ADAM_LR * (m_hat / (jnp.sqrt(v_hat) + ADAM_EPS) + ADAM_WD * w)
    return delta, m, v


def _sum_and_adamw(parts, w, m, v, name):
    n, r, ln = parts.shape
    tr = _tile(r, 512, 16)

    def body(p_ref, w_ref, m_ref, v_ref, g_out, d_out, m_out, v_out):
        g = p_ref[0].astype(F32)
        for j in range(1, n):
            g = g + p_ref[j].astype(F32)
        delta, nm, nv = _adamw_math(w_ref[...], g, m_ref[...], v_ref[...])
        g_out[...] = g
        d_out[...] = delta
        m_out[...] = nm
        v_out[...] = nv

    blk = pl.BlockSpec((tr, ln), lambda i: (i, 0))
    return pl.pallas_call(
        body, name=name, grid=(r // tr,),
        in_specs=[pl.BlockSpec((n, tr, ln), lambda i: (0, i, 0)), blk, blk, blk],
        out_specs=(blk, blk, blk, blk), out_shape=tuple(SDS((r, ln), F32) for _ in range(4)),
        compiler_params=_params("parallel"),
    )(parts, w, m, v)


BIG = ("w_in", "w_mem_kv", "w_br", "w_out")
SMALL = ("norm_g", "q_norm_g", "k_norm_g", "sg_ln_g", "sg_ln_b", "w_s", "b_s", "mem_norm_g", "final_g")


def _pack(arrs, row_unit=16):
    flat = jnp.concatenate([a.reshape(-1) for a in arrs])
    pad = (-flat.shape[0]) % (row_unit * LANES)
    if pad:
        flat = jnp.concatenate([flat, jnp.zeros((pad,), flat.dtype)])
    return flat.reshape(-1, LANES)


def _unpack(buf, shapes):
    flat = buf.reshape(-1)
    out, off = [], 0
    for shp in shapes:
        n = 1
        for s_ in shp:
            n *= s_
        out.append(flat[off:off + n].reshape(shp))
        off += n
    return out


def _shard_sum_adamw(part, from_chips, chip, w, m, v, layer, prev, name):
    _, na, r, cdim = part.shape
    flat = w.ndim == 3
    assert not flat or na == 1
    tr = _tile(r, max(8, (256 * 1024) // (na * cdim)), 8)
    n_prev = 0 if prev is None else len(prev)

    def body(chip_ref, p_ref, f_ref, w_ref, m_ref, v_ref, *rest):
        g_out, d_out, m_out, v_out = rest[n_prev:]
        g = p_ref[...].astype(F32)
        for j in range(3):
            g = g + f_ref[j].astype(F32)
        delta, nm, nv = _adamw_math(w_ref[...], g, m_ref[...], v_ref[...])
        g_out[...] = g
        d_out[...] = delta
        m_out[...] = nm
        v_out[...] = nv

    a_blk = None if flat else na
    if flat:
        lay = pl.BlockSpec((None, tr, cdim), lambda i, chip_ref: (layer, i, 0))
    else:
        lay = pl.BlockSpec((None, na, tr, cdim), lambda i, chip_ref: (layer, 0, i, 0))
    return pl.pallas_call(
        body, name=name, out_shape=tuple(SDS(w.shape, F32) for _ in range(4)),
        grid_spec=pltpu.PrefetchScalarGridSpec(
            num_scalar_prefetch=1, grid=(r // tr,),
            in_specs=[pl.BlockSpec((None, a_blk, tr, cdim), lambda i, chip_ref: (chip_ref[0], 0, i, 0)),
                      pl.BlockSpec((3, a_blk, tr, cdim), lambda i, chip_ref: (0, 0, i, 0)), lay, lay, lay]
            + [HBM] * n_prev,
            out_specs=(lay, lay, lay, lay)),
        input_output_aliases={6 + j: j for j in range(n_prev)},
        compiler_params=_params("parallel"),
    )(chip, part, from_chips, w, m, v, *(prev or ()))


def kernel(x, mem, norm_g, w_in, q_norm_g, k_norm_g, sg_ln_g, sg_ln_b, w_s, b_s, mem_norm_g, w_mem_kv, w_br, w_out, final_g, loss_target, m_norm_g, m_w_in, m_q_norm_g, m_k_norm_g, m_sg_ln_g, m_sg_ln_b, m_w_s, m_b_s, m_mem_norm_g, m_w_mem_kv, m_w_br, m_w_out, m_final_g, v_norm_g, v_w_in, v_q_norm_g, v_k_norm_g, v_sg_ln_g, v_sg_ln_b, v_w_s, v_b_s, v_mem_norm_g, v_w_mem_kv, v_w_br, v_w_out, v_final_g):
    wts = dict(norm_g=norm_g, w_in=w_in, q_norm_g=q_norm_g, k_norm_g=k_norm_g, sg_ln_g=sg_ln_g, sg_ln_b=sg_ln_b,
               w_s=w_s, b_s=b_s, mem_norm_g=mem_norm_g, w_mem_kv=w_mem_kv, w_br=w_br, w_out=w_out, final_g=final_g)
    mom1 = dict(norm_g=m_norm_g, w_in=m_w_in, q_norm_g=m_q_norm_g, k_norm_g=m_k_norm_g, sg_ln_g=m_sg_ln_g,
                sg_ln_b=m_sg_ln_b, w_s=m_w_s, b_s=m_b_s, mem_norm_g=m_mem_norm_g, w_mem_kv=m_w_mem_kv, w_br=m_w_br,
                w_out=m_w_out, final_g=m_final_g)
    mom2 = dict(norm_g=v_norm_g, w_in=v_w_in, q_norm_g=v_q_norm_g, k_norm_g=v_k_norm_g, sg_ln_g=v_sg_ln_g,
                sg_ln_b=v_sg_ln_b, w_s=v_w_s, b_s=v_b_s, mem_norm_g=v_mem_norm_g, w_mem_kv=v_w_mem_kv, w_br=v_w_br,
                w_out=v_w_out, final_g=v_final_g)
    dp = w_in.shape[0]
    core = lax.axis_index("c").astype(jnp.int32).reshape(1)
    chip = (2 * lax.axis_index("x") + lax.axis_index("y")).astype(jnp.int32).reshape(1)

    t_in = lambda a: jnp.swapaxes(a, 1, 2)
    wts, mom1, mom2 = [dict(d_, w_in=t_in(d_["w_in"])) for d_ in (wts, mom1, mom2)]

    shard_bf = {n: wts[n].astype(BF16) for n in BIG}
    shards = lambda l: [shard_bf[n][l] for n in BIG]
    x_l, mem_l = x[0], mem[0]
    tabs = _rope_tables(x_l.shape[0])

    gathered = _all_gather(shards(0), "weights_all_gather")
    layers, saved = [], []
    for l in range(dp):
        g_in, g_kv, g_br, g_out = gathered
        layers.append(_layer_weights(l, g_in.reshape(IN_WIDTH, -1), g_kv.reshape(D_MODEL, -1), _assemble_w_br(g_br),
                                     g_out.reshape(D_MODEL, -1), wts))
        x_l, s, gathered = _layer_fwd(x_l, mem_l, layers[l], tabs, next_shards=shards(l + 1) if l + 1 < dp else None)
        saved.append(s)
    loss_local, dx, d_final_g = _loss_head(x_l, final_g.reshape(1, -1), loss_target[0])
    loss = lax.psum(loss_local[0, 0], AXES)

    def finish(l, parts, from_chips, prev):
        return {n: _shard_sum_adamw(p, f, chip, wts[n], mom1[n], mom2[n], l, None if prev is None else prev[n],
                                    "sum_adamw_" + n)
                for n, p, f in zip(BIG, parts, from_chips)}

    grads, updated, pending = [None] * dp, None, None
    for l in reversed(range(dp)):
        dx, grads[l], scattered, own = _layer_bwd(dx, mem_l, layers[l], saved[l], tabs, pending=pending, core=core,
                                                  scatter_own=(l == 0))
        if scattered is not None:
            updated = finish(l + 1, *scattered, updated)
        pending = [grads[l][n] for n in BIG]
    updated = finish(0, *own, updated)
    grad_x = dx
    big_out = [{n: (t_in(updated[n][k]) if n == "w_in" else updated[n][k]) for n in BIG} for k in range(4)]

    small_g = {n: jnp.stack([g[n] for g in grads]) for n in SMALL if n != "final_g"}
    small_g["final_g"] = d_final_g
    (all_small,) = _all_gather([_pack([small_g[n] for n in SMALL])], "small_all_gather")
    small_bufs = _sum_and_adamw(
        all_small, _pack([wts[n] for n in SMALL]), _pack([mom1[n] for n in SMALL]), _pack([mom2[n] for n in SMALL]),
        "small_sum_adamw")

    outs = []
    for big_vals, small_buf in zip(big_out, small_bufs):
        vals = dict(big_vals)
        vals.update(zip(SMALL, _unpack(small_buf, [wts[n].shape for n in SMALL])))
        outs.append(vals)
    order = ("norm_g", "w_in", "q_norm_g", "k_norm_g", "sg_ln_g", "sg_ln_b", "w_s", "b_s", "mem_norm_g", "w_mem_kv",
             "w_br", "w_out", "final_g")
    result = [loss, grad_x[None]]
    for vals in outs:
        result += [vals[n] for n in order]
    return tuple(result)
```

```python
import functools

import jax
import jax.numpy as jnp
from jax import lax
from jax.experimental import pallas as pl
from jax.experimental.pallas import tpu as pltpu

F32 = jnp.float32
BF16 = jnp.bfloat16
SDS = jax.ShapeDtypeStruct
MESH_ID = pl.DeviceIdType.MESH
AXES = ("x", "y", "c")
N_DEV = 8

D_MODEL = 1024
DEPTH = 4
GRID_W = 64
CHUNK = 128
ROPE_THETA = 10000.0
EPS = 1e-6
HEAD_DIM = 64
A_HEADS = 8
A_KV_HEADS = 2
B_GROUPS = 4
M_HEADS = 4
M_HEAD_DIM = 128
BW = 512
W_A = 1280
W_B = 1536
W_M = 1024
W_G = 3072
IN_WIDTH = W_A + W_B + W_M + W_G
GROUP_OFFS = (0, W_A, W_A + W_B, W_A + W_B + W_M, IN_WIDTH)

ADAM_LR = 0.001
ADAM_B1 = 0.9
ADAM_B2 = 0.999
ADAM_EPS = 1e-08
ADAM_WD = 0.01
ADAM_STEP = 10

LANES = 128
KEY_CHUNK = 2048
VMEM_LIMIT = 52 * 1024 * 1024


def _tile(n, cap, unit=LANES):
    if n <= cap:
        return n
    t = (cap // unit) * unit
    while n % t:
        t -= unit
    return t


def _params(*sem):
    return pltpu.CompilerParams(dimension_semantics=sem, vmem_limit_bytes=VMEM_LIMIT)


def _sigmoid(z):
    return 0.5 * jnp.tanh(0.5 * z) + 0.5


def _silu_and_grad(z):
    s = _sigmoid(z)
    return z * s, s * (1.0 + z * (1.0 - s))


def _dot(a, b, dims):
    return lax.dot_general(a, b, (dims, ((), ())), preferred_element_type=F32)


NN = ((1,), (0,))
NT = ((1,), (1,))
TN = ((0,), (0,))
HBM = pl.BlockSpec(memory_space=pl.ANY)


class _Exchange:
    def __init__(self, ins, out_shapes, sems, make, aliases=None):
        self.ins, self.out_shapes, self.sems, self.make = list(ins), list(out_shapes), list(sems), make
        self.aliases = dict(aliases or {})

    def start(self, in_refs, out_refs, sems):
        for cp in self.make(in_refs, out_refs, sems):
            cp.start()

    def finish(self, in_refs, out_refs, sems):
        for cp in self.make(in_refs, out_refs, sems):
            cp.wait()


def _both(a, b):
    if a is None or b is None:
        return a if b is None else b
    n_in, n_out, n_sem = len(a.ins), len(a.out_shapes), len(a.sems)

    def make(in_refs, out_refs, sems):
        return (a.make(in_refs[:n_in], out_refs[:n_out], sems[:n_sem])
                + b.make(in_refs[n_in:], out_refs[n_out:], sems[n_sem:]))

    aliases = dict(a.aliases)
    aliases.update({n_in + i: n_out + o for i, o in b.aliases.items()})
    return _Exchange(a.ins + b.ins, a.out_shapes + b.out_shapes, a.sems + b.sems, make, aliases)


def _host_call(body, ins, in_specs, out_specs, out_shape, *, name, grid, semantics, scratch=(), exchange=None):
    ins, out_shape, scratch = list(ins), list(out_shape), list(scratch)
    if exchange is None:
        res = pl.pallas_call(
            body, name=name, grid=grid, in_specs=list(in_specs), out_specs=tuple(out_specs), out_shape=tuple(out_shape),
            scratch_shapes=scratch, compiler_params=_params(*semantics))(*ins)
        return tuple(res), ()
    n_in, n_out, n_scr = len(ins), len(out_shape), len(scratch)
    x_in, x_out = len(exchange.ins), len(exchange.out_shapes)

    def carrying(*refs):
        o0 = n_in + x_in
        s0 = o0 + n_out + x_out
        c_in, c_out, c_sems = refs[n_in:o0], refs[o0 + n_out:s0], refs[s0 + n_scr:]
        ids = [pl.program_id(a) for a in range(len(grid))]
        first = functools.reduce(jnp.logical_and, [i == 0 for i in ids])
        last = functools.reduce(jnp.logical_and, [i == g - 1 for i, g in zip(ids, grid)])

        @pl.when(first)
        def _():
            exchange.start(c_in, c_out, c_sems)

        body(*refs[:n_in], *refs[o0:o0 + n_out], *refs[s0:s0 + n_scr])

        @pl.when(last)
        def _():
            exchange.finish(c_in, c_out, c_sems)

    res = pl.pallas_call(
        carrying, name=name, grid=grid, in_specs=list(in_specs) + [HBM] * x_in,
        out_specs=tuple(out_specs) + (HBM,) * x_out, out_shape=tuple(out_shape) + tuple(exchange.out_shapes),
        scratch_shapes=scratch + exchange.sems,
        input_output_aliases={n_in + i: n_out + o for i, o in exchange.aliases.items()},
        compiler_params=_params(*(["arbitrary"] * len(grid))))(*ins, *exchange.ins)
    return tuple(res[:n_out]), tuple(res[n_out:])


def _exchange_call(exchange, name):
    x_in = len(exchange.ins)

    def body(*refs):
        x_out = len(exchange.out_shapes)
        c_in, c_out, c_sems = refs[:x_in], refs[x_in:x_in + x_out], refs[x_in + x_out:]
        exchange.start(c_in, c_out, c_sems)
        exchange.finish(c_in, c_out, c_sems)

    return pl.pallas_call(
        body, name=name, in_specs=[HBM] * x_in, out_specs=tuple([HBM] * len(exchange.out_shapes)),
        out_shape=tuple(exchange.out_shapes), scratch_shapes=exchange.sems, input_output_aliases=exchange.aliases,
    )(*exchange.ins)


def _mm(a, b, mode, name):
    (m, k), (n, k2) = a.shape, (b.shape[::-1] if mode == "nn" else b.shape)
    assert k == k2 and mode in ("nn", "nt"), (a.shape, b.shape, mode)
    dims = {"nn": NN, "nt": NT}[mode]

    def body(a_ref, b_ref, o_ref):
        o_ref[...] = _dot(a_ref[...].astype(BF16), b_ref[...].astype(BF16), dims)

    return pl.pallas_call(body, name=name, out_shape=SDS((m, n), F32),
                          compiler_params=pltpu.CompilerParams(vmem_limit_bytes=VMEM_LIMIT))(a, b)


def _mm_tn(a, b, name):
    (k, m), (k2, n) = a.shape, b.shape
    assert k == k2, (a.shape, b.shape)
    tk = _tile(k, 2048, 16)
    nk = k // tk

    def body(a_ref, b_ref, o_ref, acc):
        kk = pl.program_id(0)
        part = _dot(a_ref[...].astype(BF16), b_ref[...].astype(BF16), TN)

        @pl.when(kk == 0)
        def _():
            acc[...] = part

        @pl.when(kk > 0)
        def _():
            acc[...] += part

        @pl.when(kk == nk - 1)
        def _():
            o_ref[...] = acc[...].astype(BF16)

    return pl.pallas_call(
        body, name=name, grid=(nk,),
        in_specs=[pl.BlockSpec((tk, m), lambda kk: (kk, 0)), pl.BlockSpec((tk, n), lambda kk: (kk, 0))],
        out_specs=pl.BlockSpec((m, n), lambda kk: (0, 0)), out_shape=SDS((m, n), BF16),
        scratch_shapes=[pltpu.VMEM((m, n), F32)], compiler_params=_params("arbitrary"),
    )(a, b)


def _rmsnorm_fwd(x, g, name):
    t, d = x.shape
    tm = _tile(t, 512, 8)

    def body(x_ref, g_ref, h_ref):
        xf = x_ref[...]
        r = lax.rsqrt(jnp.mean(xf * xf, axis=-1, keepdims=True) + EPS)
        h_ref[...] = (xf * r * g_ref[...]).astype(BF16)

    return pl.pallas_call(
        body, name=name, grid=(t // tm,),
        in_specs=[pl.BlockSpec((tm, d), lambda i: (i, 0)), pl.BlockSpec((1, d), lambda i: (0, 0))],
        out_specs=pl.BlockSpec((tm, d), lambda i: (i, 0)),
        out_shape=SDS((t, d), BF16), compiler_params=_params("parallel"),
    )(x, g)


def _rmsnorm_bwd_math(xf, g, dh):
    r = lax.rsqrt(jnp.mean(xf * xf, axis=-1, keepdims=True) + EPS)
    xh = xf * r
    gd = dh * g
    dx = r * (gd - xh * jnp.mean(gd * xh, axis=-1, keepdims=True))
    return dx, jnp.sum(dh * xh, axis=0, keepdims=True)


def _rmsnorm_dg(x, g, dh, name):
    t, d = x.shape
    tm = _tile(t, 512, 8)

    def body(x_ref, g_ref, dh_ref, dg_ref):
        @pl.when(pl.program_id(0) == 0)
        def _():
            dg_ref[...] = jnp.zeros_like(dg_ref)

        dg_ref[...] += _rmsnorm_bwd_math(x_ref[...], g_ref[...], dh_ref[...])[1]

    row = pl.BlockSpec((tm, d), lambda i: (i, 0))
    vec = pl.BlockSpec((1, d), lambda i: (0, 0))
    return pl.pallas_call(
        body, name=name, grid=(t // tm,), in_specs=[row, vec, row], out_specs=vec, out_shape=SDS((1, d), F32),
        compiler_params=_params("arbitrary"),
    )(x, g, dh)


def _group_rows(p):
    return slice(GROUP_OFFS[p], GROUP_OFFS[p + 1])


def _d_w_in_t(dps, h):
    t, d = h.shape
    tm = 256
    assert all(off % tm == 0 for off in GROUP_OFFS)
    tiles = [a.shape[1] // tm for a in dps]
    starts = [sum(tiles[:p]) for p in range(len(dps))]

    def body(*refs):
        a_refs, h_ref, o_ref = refs[:len(dps)], refs[len(dps)], refs[len(dps) + 1]
        i = pl.program_id(0)
        for p, a_ref in enumerate(a_refs):
            @pl.when((i >= starts[p]) & (i < starts[p] + tiles[p]))
            def _():
                o_ref[...] = _dot(a_ref[...], h_ref[...], TN).astype(BF16)

    def tile_of(p):
        return lambda i: jnp.clip(i - starts[p], 0, tiles[p] - 1)

    return pl.pallas_call(
        body, name="d_w_in", grid=(sum(tiles),),
        in_specs=[pl.BlockSpec((t, tm), lambda i, p=p: (0, tile_of(p)(i))) for p in range(len(dps))]
        + [pl.BlockSpec((t, d), lambda i: (0, 0), pipeline_mode=pl.Buffered(1))],
        out_specs=pl.BlockSpec((tm, d), lambda i: (i, 0)),
        out_shape=SDS((sum(tiles) * tm, d), BF16), compiler_params=_params("arbitrary"),
    )(*dps, h)


def _norm_in_proj(x, g, w_t, cs, sn, qg2, kg2, lng, lnb, ws, bs, kv):
    t, d = x.shape
    tm = _tile(t, 512, LANES)
    widths = [GROUP_OFFS[p + 1] - GROUP_OFFS[p] for p in range(4)]
    scale = HEAD_DIM ** -0.5

    def body(x_ref, g_ref, w_ref, cs_ref, sn_ref, qg_ref, kg_ref, lng_ref, lnb_ref, ws_ref, bs_ref, kv_ref,
             h_ref, pa_ref, pb_ref, pm_ref, pg_ref, q_ref, qT_ref, k_ref, kT_ref, vT_ref, yb_ref, ym_ref):
        xf = x_ref[...]
        r = lax.rsqrt(jnp.mean(xf * xf, axis=-1, keepdims=True) + EPS)
        hb = (xf * r * g_ref[...]).astype(BF16)
        h_ref[...] = hb
        groups = [_dot(hb, w_ref[_group_rows(p), :], NT).astype(BF16) for p in range(4)]
        for o_ref, val in zip((pa_ref, pb_ref, pm_ref, pg_ref), groups):
            o_ref[...] = val
        pa = groups[0]
        lane = lax.broadcasted_iota(jnp.int32, (tm, LANES), 1)
        c, s = cs_ref[...], sn_ref[...]

        def norm_rope(xg, g2):
            rr = lax.rsqrt(_head_sums(xg * xg, lane) * (1.0 / HEAD_DIM) + EPS)
            xn = xg * rr * g2
            return xn * c + _swap16(xn, lane) * s

        for gi in range(4):
            sl = slice(gi * LANES, (gi + 1) * LANES)
            qr = norm_rope(pa[:, sl].astype(F32), qg_ref[...]) * scale
            q_ref[:, sl] = qr.astype(BF16)
            qT_ref[sl, :] = qr.T.astype(BF16)
        kr = norm_rope(pa[:, 512:640].astype(F32), kg_ref[...])
        kT_ref[...] = kr.T.astype(BF16)
        vT_ref[...] = pa[:, 640:768].astype(F32).T.astype(BF16)
        kr = kr.astype(BF16)
        for kvh in range(A_KV_HEADS):
            k_ref[kvh] = kr[:, kvh * HEAD_DIM:(kvh + 1) * HEAD_DIM]
        _gmlp_tile(groups[1], lng_ref, lnb_ref, ws_ref, bs_ref, yb_ref)
        _mem_attn_tile(groups[2], kv_ref, ym_ref)

    row = lambda wd: pl.BlockSpec((tm, wd), lambda i: (i, 0))
    col = lambda rws: pl.BlockSpec((rws, tm), lambda i: (0, i))
    whole = lambda a: pl.BlockSpec(a.shape, lambda i: (0,) * a.ndim)
    hm = pl.BlockSpec((A_KV_HEADS, tm, HEAD_DIM), lambda i: (0, i, 0))
    return pl.pallas_call(
        body, name="norm_in_proj", grid=(t // tm,),
        in_specs=[row(d), whole(g), pl.BlockSpec(w_t.shape, lambda i: (0, 0), pipeline_mode=pl.Buffered(1)),
                  row(LANES), row(LANES)] + [whole(a) for a in (qg2, kg2, lng, lnb, ws, bs, kv)],
        out_specs=(row(d),) + tuple(row(wd) for wd in widths)
        + (row(BW), col(BW), hm, col(LANES), col(LANES), row(BW), row(BW)),
        out_shape=(SDS((t, d), BF16),) + tuple(SDS((t, wd), BF16) for wd in widths)
        + (SDS((t, BW), BF16), SDS((BW, t), BF16), SDS((A_KV_HEADS, t, HEAD_DIM), BF16),
           SDS((LANES, t), BF16), SDS((LANES, t), BF16), SDS((t, BW), BF16), SDS((t, BW), BF16)),
        compiler_params=_params("parallel"),
    )(x, g, w_t, cs, sn, qg2, kg2, lng, lnb, ws, bs, kv)


def _dh_rmsnorm_bwd(dps, w_t, x, g, dres, exchange=None):
    t, d = x.shape
    tm = _tile(t, 256, 16)
    n = len(dps)

    def body(*refs):
        dp_refs, w_ref = refs[:n], refs[n]
        x_ref, g_ref, dres_ref, dx_ref, dg_ref = refs[n + 1:]
        dh = None
        for p, dp_ref in enumerate(dp_refs):
            part = _dot(dp_ref[...], w_ref[_group_rows(p), :], NN)
            dh = part if dh is None else dh + part
        dx, dg = _rmsnorm_bwd_math(x_ref[...], g_ref[...], dh)
        dx_ref[...] = dx + dres_ref[...]

        @pl.when(pl.program_id(0) == 0)
        def _():
            dg_ref[...] = jnp.zeros_like(dg_ref)

        dg_ref[...] += dg

    row = pl.BlockSpec((tm, d), lambda i: (i, 0))
    vec = pl.BlockSpec((1, d), lambda i: (0, 0))
    return _host_call(
        body, list(dps) + [w_t, x, g, dres],
        [pl.BlockSpec((tm, a.shape[1]), lambda i: (i, 0)) for a in dps]
        + [pl.BlockSpec(w_t.shape, lambda i: (0, 0), pipeline_mode=pl.Buffered(1)), row, vec, row],
        (row, vec), (SDS((t, d), F32), SDS((1, d), F32)),
        name="dh_rmsnorm_bwd", grid=(t // tm,), semantics=("arbitrary",), exchange=exchange)


def _rope_tables(t):
    rows = t // GRID_W
    row = jnp.repeat(jnp.arange(rows, dtype=F32), GRID_W)
    col = jnp.tile(jnp.arange(GRID_W, dtype=F32), rows)
    n_freq = HEAD_DIM // 4
    inv = ROPE_THETA ** (-jnp.arange(n_freq, dtype=F32) / n_freq)
    ang = jnp.stack([row[:, None] * inv, col[:, None] * inv], axis=1)
    cos, sin = jnp.cos(ang), jnp.sin(ang)
    c64 = jnp.concatenate([cos[:, 0], cos[:, 0], cos[:, 1], cos[:, 1]], axis=-1)
    s64 = jnp.concatenate([-sin[:, 0], sin[:, 0], -sin[:, 1], sin[:, 1]], axis=-1)
    return jnp.tile(c64, (1, 2)), jnp.tile(s64, (1, 2))


def _head_sums(v, lane):
    lo = jnp.sum(jnp.where(lane < HEAD_DIM, v, 0.0), axis=-1, keepdims=True)
    hi = jnp.sum(jnp.where(lane < HEAD_DIM, 0.0, v), axis=-1, keepdims=True)
    return jnp.where(lane < HEAD_DIM, lo, hi)


def _swap16(v, lane):
    return jnp.where((lane % 32) < 16, pltpu.roll(v, LANES - 16, 1), pltpu.roll(v, 16, 1))


def _attn_fwd(qT, k, vT, pa, exchange=None):
    t = qT.shape[1]
    tq = _tile(t, 256, LANES)
    grp = A_HEADS // A_KV_HEADS

    ck = _tile(t, KEY_CHUNK, LANES)

    def body(qT_ref, k_ref, vT_ref, pa_ref, o_ref, lse_ref, y_ref):
        def finish_pair(pair, o_pair):
            o2 = jnp.concatenate(o_pair, axis=0).T
            cols = slice(pair * LANES, (pair + 1) * LANES)
            o_ref[:, cols] = o2
            z = pa_ref[:, 768 + pair * LANES:768 + (pair + 1) * LANES].astype(F32)
            y_ref[:, cols] = (o2 * (z * _sigmoid(z))).astype(BF16)

        n_c = t // ck
        items = [(h, c) for h in range(A_HEADS) for c in range(n_c)]

        def scores(h, c):
            return _dot(k_ref[h // grp, c * ck:(c + 1) * ck, :], qT_ref[h * HEAD_DIM:(h + 1) * HEAD_DIM, :], NN)

        def weighted_values(h, c, pT):
            kvh = h // grp
            return _dot(vT_ref[kvh * HEAD_DIM:(kvh + 1) * HEAD_DIM, c * ck:(c + 1) * ck], pT, NN)

        o_pair, lse_rows, sT, pending, acc = [], [], scores(*items[0]), None, None
        for i in range(len(items) + 1):
            item = items[i] if i < len(items) else None
            sT_next = scores(*items[i + 1]) if i + 1 < len(items) else None
            if pending is not None:
                (ph, pc, pT, alpha, l_run) = pending
                o_c = weighted_values(ph, pc, pT)
                acc = o_c if alpha is None else acc * alpha + o_c
                if pc == n_c - 1:
                    o_pair.append(acc / l_run)
                    if ph % 2:
                        finish_pair(ph // 2, o_pair)
                        o_pair = []
                pending = None
            if item is not None:
                h, c = item
                m_c = jnp.max(sT, axis=0, keepdims=True)
                if c == 0:
                    m_new, alpha = m_c, None
                else:
                    m_new = jnp.maximum(m, m_c)
                    alpha = jnp.exp(m - m_new)
                pT = jnp.exp(sT - m_new)
                l_c = jnp.sum(pT, axis=0, keepdims=True)
                l = l_c if c == 0 else l * alpha + l_c
                m = m_new
                if c == n_c - 1:
                    lse_rows.append(m + jnp.log(l))
                pending = (h, c, pT.astype(BF16), alpha, l)
            sT = sT_next
        lse_cols = jnp.concatenate(lse_rows, axis=0).T
        for kvh in range(A_KV_HEADS):
            lse_ref[kvh] = lse_cols[:, kvh * grp:(kvh + 1) * grp]

    row = lambda w: pl.BlockSpec((tq, w), lambda i: (i, 0))
    return _host_call(
        body, (qT, k, vT, pa),
        [pl.BlockSpec((BW, tq), lambda i: (0, i)), pl.BlockSpec((A_KV_HEADS, t, HEAD_DIM), lambda i: (0, 0, 0)),
         pl.BlockSpec((A_KV_HEADS * HEAD_DIM, t), lambda i: (0, 0)), row(W_A)],
        (row(BW), pl.BlockSpec((A_KV_HEADS, tq, grp), lambda i: (0, i, 0)), row(BW)),
        (SDS((t, BW), F32), SDS((A_KV_HEADS, t, grp), F32), SDS((t, BW), BF16)),
        name="attn_fwd", grid=(t // tq,), semantics=("parallel",), exchange=exchange)


def _attn_bwd(q, qT, k, kT, vT, pa, o, lse, dy, exchange=None):
    t = q.shape[0]
    tq = _tile(t, 256, LANES)
    grp = A_HEADS // A_KV_HEADS
    gw = grp * HEAD_DIM

    def body(q_ref, qT_ref, k_ref, kT_ref, vT_ref, z_ref, o_ref, lse_ref, dy_ref, dq_ref, dkT_ref, dvT_ref):
        @pl.when(pl.program_id(1) == 0)
        def _():
            dkT_ref[...] = jnp.zeros_like(dkT_ref)
            dvT_ref[...] = jnp.zeros_like(dvT_ref)

        z = z_ref[...].astype(F32)
        do = dy_ref[...] * (z * _sigmoid(z))
        doo = do * o_ref[...]
        doT = do.T
        kk, kT, vT = k_ref[...], kT_ref[...], vT_ref[...]
        heads = [slice(j * HEAD_DIM, (j + 1) * HEAD_DIM) for j in range(grp)]

        def scores(hs):
            return _dot(q_ref[:, hs], kT, NN), _dot(do[:, hs].astype(BF16), vT, NN)

        s, dp = scores(heads[0])
        for j, hs in enumerate(heads):
            nxt = scores(heads[j + 1]) if j + 1 < grp else None
            delta = jnp.sum(doo[:, hs], axis=-1, keepdims=True)
            p = jnp.exp(s - lse_ref[:, j:j + 1])
            ds_b = (p * (dp - delta)).astype(BF16)
            dq_ref[:, hs] = _dot(ds_b, kk, NN)
            dkT_ref[...] += _dot(qT_ref[hs, :], ds_b, NN)
            dvT_ref[...] += _dot(doT[hs, :].astype(BF16), p.astype(BF16), NN)
            if nxt is not None:
                s, dp = nxt

    grp_blk = pl.BlockSpec((tq, gw), lambda g, i: (i, g))
    kvT_blk = pl.BlockSpec((HEAD_DIM, t), lambda g, i: (g, 0))
    acc_blk = pl.BlockSpec((None, HEAD_DIM, t), lambda g, i: (g, 0, 0))
    return _host_call(
        body, (q, qT, k, kT, vT, pa, o, lse, dy),
        [grp_blk, pl.BlockSpec((gw, tq), lambda g, i: (g, i)), pl.BlockSpec((None, t, HEAD_DIM), lambda g, i: (g, 0, 0)),
         kvT_blk, kvT_blk, pl.BlockSpec((tq, gw), lambda g, i: (i, 768 // gw + g)), grp_blk,
         pl.BlockSpec((None, tq, grp), lambda g, i: (g, i, 0)), grp_blk],
        (grp_blk, acc_blk, acc_blk),
        (SDS((t, BW), F32), SDS((A_KV_HEADS, HEAD_DIM, t), F32), SDS((A_KV_HEADS, HEAD_DIM, t), F32)),
        name="attn_bwd", grid=(A_KV_HEADS, t // tq), semantics=("arbitrary", "arbitrary"), exchange=exchange)


def _attn_prep_bwd(pa, cs, sn, qg2, kg2, dq, dkT, dvT, dy, o, exchange=None):
    t = pa.shape[0]
    tq = _tile(t, 512, LANES)
    scale = HEAD_DIM ** -0.5

    def body(pa_ref, cs_ref, sn_ref, qg_ref, kg_ref, dq_ref, dkT_ref, dvT_ref, dy_ref, o_ref, dpa_ref, dqg_ref, dkg_ref):
        lane = lax.broadcasted_iota(jnp.int32, (tq, LANES), 1)
        c, s = cs_ref[...], sn_ref[...]

        @pl.when(pl.program_id(0) == 0)
        def _():
            dqg_ref[...] = jnp.zeros_like(dqg_ref)
            dkg_ref[...] = jnp.zeros_like(dkg_ref)

        def norm_rope_bwd(xg, g2, dout):
            r = lax.rsqrt(_head_sums(xg * xg, lane) * (1.0 / HEAD_DIM) + EPS)
            xh = xg * r
            dxn = dout * c + _swap16(dout * s, lane)
            gd = dxn * g2
            dx = r * (gd - xh * (_head_sums(gd * xh, lane) * (1.0 / HEAD_DIM)))
            return dx, jnp.sum(dxn * xh, axis=0, keepdims=True)

        for gi in range(4):
            sl = slice(gi * LANES, (gi + 1) * LANES)
            dx, dg = norm_rope_bwd(pa_ref[:, sl].astype(F32), qg_ref[...], dq_ref[:, sl] * scale)
            dpa_ref[:, sl] = dx.astype(BF16)
            dqg_ref[...] += dg
        dx, dg = norm_rope_bwd(pa_ref[:, 512:640].astype(F32), kg_ref[...], dkT_ref[...].T)
        dpa_ref[:, 512:640] = dx.astype(BF16)
        dkg_ref[...] += dg
        dpa_ref[:, 640:768] = dvT_ref[...].T.astype(BF16)
        z = pa_ref[:, 768:1280].astype(F32)
        _, dsilu = _silu_and_grad(z)
        dpa_ref[:, 768:1280] = (dy_ref[...] * o_ref[...] * dsilu).astype(BF16)

    row = lambda w: pl.BlockSpec((tq, w), lambda i: (i, 0))
    col = pl.BlockSpec((LANES, tq), lambda i: (0, i))
    vec = pl.BlockSpec((1, LANES), lambda i: (0, 0))
    return _host_call(
        body, (pa, cs, sn, qg2, kg2, dq, dkT, dvT, dy, o),
        [row(W_A), row(LANES), row(LANES), vec, vec, row(BW), col, col, row(BW), row(BW)],
        (row(W_A), vec, vec), (SDS((t, W_A), BF16), SDS((1, LANES), F32), SDS((1, LANES), F32)),
        name="attn_prep_bwd", grid=(t // tq,), semantics=("arbitrary",), exchange=exchange)


def _layer_norm(v, g, b):
    mu = jnp.mean(v, axis=-1, keepdims=True)
    xc = v - mu
    rs = lax.rsqrt(jnp.mean(xc * xc, axis=-1, keepdims=True) + EPS)
    xh = xc * rs
    return xh * g + b, xh, rs


def _gmlp_tile(pb, g_ref, b_ref, ws_ref, bs_ref, y_ref):
    vln, _, _ = _layer_norm(pb[:, BW:2 * BW].astype(F32), g_ref[...], b_ref[...])
    vb = vln.astype(BF16)
    for gi in range(B_GROUPS):
        w = ws_ref[gi].astype(BF16)
        cs_ = slice(gi * CHUNK, (gi + 1) * CHUNK)
        for n in range(pb.shape[0] // CHUNK):
            rs_ = slice(n * CHUNK, (n + 1) * CHUNK)
            mixed = _dot(w, vb[rs_, cs_], NN) + bs_ref[gi]
            z = pb[rs_, 2 * BW + gi * CHUNK:2 * BW + (gi + 1) * CHUNK].astype(F32)
            y_ref[rs_, cs_] = (pb[rs_, cs_].astype(F32) * mixed * (z * _sigmoid(z))).astype(BF16)


def _gmlp_bwd(pb, lng, lnb, ws, bs, dy):
    t = pb.shape[0]
    tb = _tile(t, 256, CHUNK)

    def body(pb_ref, g_ref, b_ref, ws_ref, bs_ref, dy_ref, dpb_ref, dws_ref, dbs_ref, dg_ref, db_ref, dvln_ref):
        @pl.when(pl.program_id(0) == 0)
        def _():
            dws_ref[...] = jnp.zeros_like(dws_ref)
            dbs_ref[...] = jnp.zeros_like(dbs_ref)
            dg_ref[...] = jnp.zeros_like(dg_ref)
            db_ref[...] = jnp.zeros_like(db_ref)

        vln, xh, rs = _layer_norm(pb_ref[:, BW:2 * BW].astype(F32), g_ref[...], b_ref[...])
        vb = vln.astype(BF16)
        for gi in range(B_GROUPS):
            w = ws_ref[gi].astype(BF16)
            cs_ = slice(gi * CHUNK, (gi + 1) * CHUNK)
            for n in range(tb // CHUNK):
                rs_ = slice(n * CHUNK, (n + 1) * CHUNK)
                vbc = vb[rs_, cs_]
                mixed = _dot(w, vbc, NN) + bs_ref[gi]
                zs = slice(2 * BW + gi * CHUNK, 2 * BW + (gi + 1) * CHUNK)
                z = pb_ref[rs_, zs].astype(F32)
                u = pb_ref[rs_, cs_].astype(F32)
                sil, dsil = _silu_and_grad(z)
                dyc = dy_ref[rs_, cs_]
                dmixed = dyc * u * sil
                dpb_ref[rs_, cs_] = (dyc * mixed * sil).astype(BF16)
                dpb_ref[rs_, zs] = (dyc * u * mixed * dsil).astype(BF16)
                dmb = dmixed.astype(BF16)
                dws_ref[gi] += _dot(dmb, vbc, NT)
                dbs_ref[gi] += jnp.sum(dmixed, axis=-1, keepdims=True)
                dvln_ref[rs_, cs_] = _dot(w, dmb, TN)
        dvln = dvln_ref[...]
        dg_ref[...] += jnp.sum(dvln * xh, axis=0, keepdims=True)
        db_ref[...] += jnp.sum(dvln, axis=0, keepdims=True)
        gd = dvln * g_ref[...]
        dv = rs * (gd - jnp.mean(gd, axis=-1, keepdims=True) - xh * jnp.mean(gd * xh, axis=-1, keepdims=True))
        dpb_ref[:, BW:2 * BW] = dv.astype(BF16)

    vec = pl.BlockSpec((1, BW), lambda i: (0, 0))
    wsb = pl.BlockSpec((B_GROUPS, CHUNK, CHUNK), lambda i: (0, 0, 0))
    bsb = pl.BlockSpec((B_GROUPS, CHUNK, 1), lambda i: (0, 0, 0))
    return pl.pallas_call(
        body, name="gmlp_bwd", grid=(t // tb,),
        in_specs=[pl.BlockSpec((tb, W_B), lambda i: (i, 0)), vec, vec, wsb, bsb, pl.BlockSpec((tb, BW), lambda i: (i, 0))],
        out_specs=(pl.BlockSpec((tb, W_B), lambda i: (i, 0)), wsb, bsb, vec, vec),
        out_shape=(SDS((t, W_B), BF16), SDS((B_GROUPS, CHUNK, CHUNK), F32), SDS((B_GROUPS, CHUNK, 1), F32),
                   SDS((1, BW), F32), SDS((1, BW), F32)),
        scratch_shapes=[pltpu.VMEM((tb, BW), F32)],
        compiler_params=_params("arbitrary"),
    )(pb, lng, lnb, ws, bs, dy)


def _mem_scores(pm_ref, kv_ref, h):
    hs = slice(h * M_HEAD_DIM, (h + 1) * M_HEAD_DIM)
    return _dot(pm_ref[:, hs].astype(BF16), kv_ref[:, hs].astype(BF16), NT) * (M_HEAD_DIM ** -0.5)


def _softmax_rows(s):
    e = jnp.exp(s - jnp.max(s, axis=-1, keepdims=True))
    return e / jnp.sum(e, axis=-1, keepdims=True)


def _mem_attn_tile(pm, kv_ref, y_ref):
    s_next = _mem_scores(pm, kv_ref, 0)
    for h in range(M_HEADS):
        hs = slice(h * M_HEAD_DIM, (h + 1) * M_HEAD_DIM)
        vh = kv_ref[:, BW + h * M_HEAD_DIM:BW + (h + 1) * M_HEAD_DIM].astype(BF16)
        s, s_next = s_next, (_mem_scores(pm, kv_ref, h + 1) if h + 1 < M_HEADS else None)
        p = _softmax_rows(s)
        o = _dot(p.astype(BF16), vh, NN)
        z = pm[:, BW + h * M_HEAD_DIM:BW + (h + 1) * M_HEAD_DIM].astype(F32)
        y_ref[:, hs] = (o * (z * _sigmoid(z))).astype(BF16)


def _mem_attn_bwd(pm, kv, dy):
    t = pm.shape[0]
    tq = _tile(t, 512, 8)
    ml = kv.shape[0]
    scale = M_HEAD_DIM ** -0.5

    def body(pm_ref, kv_ref, dy_ref, dpm_ref, dkv_ref):
        @pl.when(pl.program_id(0) == 0)
        def _():
            dkv_ref[...] = jnp.zeros_like(dkv_ref)

        s_next = _mem_scores(pm_ref, kv_ref, 0)
        for h in range(M_HEADS):
            hs = slice(h * M_HEAD_DIM, (h + 1) * M_HEAD_DIM)
            zs = slice(BW + h * M_HEAD_DIM, BW + (h + 1) * M_HEAD_DIM)
            kh = kv_ref[:, hs].astype(BF16)
            vh = kv_ref[:, zs].astype(BF16)
            qh = pm_ref[:, hs].astype(BF16)
            s, s_next = s_next, (_mem_scores(pm_ref, kv_ref, h + 1) if h + 1 < M_HEADS else None)
            p = _softmax_rows(s)
            pb = p.astype(BF16)
            o = _dot(pb, vh, NN)
            sil, dsil = _silu_and_grad(pm_ref[:, zs].astype(F32))
            dyh = dy_ref[:, hs]
            do = dyh * sil
            dpm_ref[:, zs] = (dyh * o * dsil).astype(BF16)
            delta = jnp.sum(do * o, axis=-1, keepdims=True)
            do_b = do.astype(BF16)
            dp = _dot(do_b, vh, NT)
            dr_b = (p * (dp - delta) * scale).astype(BF16)
            dpm_ref[:, hs] = _dot(dr_b, kh, NN).astype(BF16)
            dkv_ref[:, hs] += _dot(dr_b, qh, TN)
            dkv_ref[:, zs] += _dot(pb, do_b, TN)

    kvb = pl.BlockSpec((ml, 2 * BW), lambda i: (0, 0))
    return pl.pallas_call(
        body, name="mem_attn_bwd", grid=(t // tq,),
        in_specs=[pl.BlockSpec((tq, W_M), lambda i: (i, 0)), kvb, pl.BlockSpec((tq, BW), lambda i: (i, 0))],
        out_specs=(pl.BlockSpec((tq, W_M), lambda i: (i, 0)), kvb),
        out_shape=(SDS((t, W_M), BF16), SDS((ml, 2 * BW), F32)),
        compiler_params=_params("arbitrary"),
    )(pm, kv, dy)


def _merge_fwd(ya, yb, ym, pg, g_br, x, w_out, exchange=None):
    t = ya.shape[0]
    tm = _tile(t, 512, 8)
    shard = g_br.shape[-1]

    def body(ya_ref, yb_ref, ym_ref, pg_ref, g_ref, x_ref, wo_ref, m_ref, xn_ref, w_ref):
        @pl.when(pl.program_id(0) == 0)
        def _():
            for d in range(N_DEV):
                w_ref[:, :, d * shard:(d + 1) * shard] = g_ref[d]

        ups = [_dot(y_ref[...], w_ref[n], NN) for n, y_ref in enumerate((ya_ref, yb_ref, ym_ref))]
        acc = None
        for n, up in enumerate(ups):
            term = _sigmoid(pg_ref[:, n * D_MODEL:(n + 1) * D_MODEL].astype(F32)) * up
            acc = term if acc is None else acc + term
        merged = acc.astype(BF16)
        m_ref[...] = merged
        xn_ref[...] = x_ref[...] + _dot(merged, wo_ref[...], NN)

    yb_spec = pl.BlockSpec((tm, BW), lambda i: (i, 0))
    row = pl.BlockSpec((tm, D_MODEL), lambda i: (i, 0))
    return _host_call(
        body, (ya, yb, ym, pg, g_br, x, w_out),
        [yb_spec, yb_spec, yb_spec, pl.BlockSpec((tm, W_G), lambda i: (i, 0)),
         pl.BlockSpec(g_br.shape, lambda i: (0, 0, 0, 0)), row, pl.BlockSpec(w_out.shape, lambda i: (0, 0))],
        (row, row, pl.BlockSpec((3, BW, D_MODEL), lambda i: (0, 0, 0))),
        (SDS((t, D_MODEL), BF16), SDS((t, D_MODEL), F32), SDS((3, BW, D_MODEL), BF16)),
        name="merge_fwd", grid=(t // tm,), semantics=("arbitrary",), exchange=exchange)


def _merge_bwd(ya, yb, ym, pg, wbr, dx_out, w_out, exchange=None):
    t = ya.shape[0]
    tm = _tile(t, 512, 16)
    n_steps = t // tm
    shard = D_MODEL // N_DEV

    def body(ya_ref, yb_ref, ym_ref, pg_ref, w_ref, dx_ref, wo_ref, dya_ref, dyb_ref, dym_ref, dpg_ref, dwd_ref, dw_ref):
        @pl.when(pl.program_id(0) == 0)
        def _():
            dw_ref[...] = jnp.zeros_like(dw_ref)

        dmf = _dot(dx_ref[...].astype(BF16), wo_ref[...], NT)
        branches = ((ya_ref, dya_ref), (yb_ref, dyb_ref), (ym_ref, dym_ref))

        def gate(n):
            gt = _sigmoid(pg_ref[:, n * D_MODEL:(n + 1) * D_MODEL].astype(F32))
            return gt, (dmf * gt).astype(BF16)

        gt, dup = gate(0)
        for n, (y_ref, dy_ref) in enumerate(branches):
            nxt = gate(n + 1) if n + 1 < len(branches) else None
            y, w = y_ref[...], w_ref[n]
            dy_ref[...] = _dot(dup, w, NT)
            dw_ref[n] += _dot(y, dup, TN)
            up = _dot(y, w, NN)
            dpg_ref[:, n * D_MODEL:(n + 1) * D_MODEL] = (dmf * up * gt * (1.0 - gt)).astype(BF16)
            if nxt is not None:
                gt, dup = nxt

        @pl.when(pl.program_id(0) == n_steps - 1)
        def _():
            for d in range(N_DEV):
                dwd_ref[d] = dw_ref[:, :, d * shard:(d + 1) * shard].astype(BF16)

    y_spec = pl.BlockSpec((tm, BW), lambda i: (i, 0))
    w_spec = pl.BlockSpec((3, BW, D_MODEL), lambda i: (0, 0, 0))
    return _host_call(
        body, (ya, yb, ym, pg, wbr, dx_out, w_out),
        [y_spec, y_spec, y_spec, pl.BlockSpec((tm, W_G), lambda i: (i, 0)), w_spec,
         pl.BlockSpec((tm, D_MODEL), lambda i: (i, 0)), pl.BlockSpec(w_out.shape, lambda i: (0, 0))],
        (y_spec, y_spec, y_spec, pl.BlockSpec((tm, W_G), lambda i: (i, 0)),
         pl.BlockSpec((N_DEV, 3, BW, shard), lambda i: (0, 0, 0, 0))),
        (SDS((t, BW), F32), SDS((t, BW), F32), SDS((t, BW), F32), SDS((t, W_G), BF16), SDS((N_DEV, 3, BW, shard), BF16)),
        name="merge_bwd", grid=(n_steps,), semantics=("arbitrary",), scratch=[pltpu.VMEM((3, BW, D_MODEL), F32)],
        exchange=exchange)


def _loss_head(x, g, target):
    t, d = x.shape
    tm = _tile(t, 512, 8)

    def body(x_ref, g_ref, t_ref, loss_ref, dx_ref, dg_ref):
        @pl.when(pl.program_id(0) == 0)
        def _():
            loss_ref[...] = jnp.zeros_like(loss_ref)
            dg_ref[...] = jnp.zeros_like(dg_ref)

        xf = x_ref[...]
        r = lax.rsqrt(jnp.mean(xf * xf, axis=-1, keepdims=True) + EPS)
        xh = xf * r
        err = xh * g_ref[...] - t_ref[...]
        per_tok = jnp.mean(err * err, axis=-1, keepdims=True)
        loss_ref[...] += 0.5 * jnp.sum(per_tok, axis=0, keepdims=True)
        dy = err * (1.0 / d)
        gd = dy * g_ref[...]
        dx_ref[...] = r * (gd - xh * jnp.mean(gd * xh, axis=-1, keepdims=True))
        dg_ref[...] += jnp.sum(dy * xh, axis=0, keepdims=True)

    row = pl.BlockSpec((tm, d), lambda i: (i, 0))
    vec = pl.BlockSpec((1, d), lambda i: (0, 0))
    return pl.pallas_call(
        body, name="loss_head", grid=(t // tm,),
        in_specs=[row, vec, row],
        out_specs=(pl.BlockSpec((1, 1), lambda i: (0, 0)), row, vec),
        out_shape=(SDS((1, 1), F32), SDS((t, d), F32), SDS((1, d), F32)),
        compiler_params=_params("arbitrary"),
    )(x, g, target)


def _layer_fwd(x, mem, w, tabs, next_shards=None):
    cs, sn = tabs
    riding = next_shards is not None
    memn = _rmsnorm_fwd(mem, w["mem_g"], "mem_rmsnorm_fwd")
    kv = _mm(memn, w["w_kv"], "nn", "mem_kv")
    h, pa, pb, pm, pg, q, qT, k, kT, vT, yb, ym = _norm_in_proj(
        x, w["norm_g"], w["w_in_t"], cs, sn, w["qg2"], w["kg2"], w["ln_g"], w["ln_b"], w["w_s"], w["b_s"], kv)
    (o, lse, ya), gathered = _attn_fwd(qT, k, vT, pa, exchange=_gather_first_hop(next_shards) if riding else None)
    (merged, x_next, w_br), gathered = _merge_fwd(ya, yb, ym, pg, w["g_br"], x, w["w_out"],
                                                  exchange=_gather_forward(gathered) if riding else None)
    saved = dict(x=x, h=h, pa=pa, pb=pb, pm=pm, pg=pg, q=q, qT=qT, k=k, kT=kT, vT=vT, o=o, lse=lse, ya=ya, yb=yb, ym=ym,
                 memn=memn, kv=kv, merged=merged, w_br=w_br)
    return x_next, saved, gathered


def _layer_bwd(dx_out, mem, w, s, tabs, pending=None, core=None, scatter_own=False):
    cs, sn = tabs
    t = dx_out.shape[0]
    riding = pending is not None
    by_owner = lambda a: a.reshape((N_DEV // 2, 2, -1) + a.shape[-2:])
    row_shards = lambda a: by_owner(a.reshape(N_DEV, a.shape[0] // N_DEV, a.shape[1]))
    d_w_out = _mm_tn(s["merged"], dx_out, "d_w_out")
    (dya, dyb, dym, dpg, d_w_br), recv = _merge_bwd(s["ya"], s["yb"], s["ym"], s["pg"], s["w_br"], dx_out, w["w_out"],
                                                    exchange=_scatter_to_sibling(pending) if riding else None)
    parts = _pair_sums(BIG, pending, recv, core) if riding else None
    grads = dict(w_br=by_owner(d_w_br), w_out=row_shards(d_w_out))
    early = [grads[n] for n in BIG[2:]]
    (dq, dkT, dvT), landed = _attn_bwd(
        s["q"], s["qT"], s["k"], s["kT"], s["vT"], s["pa"], s["o"], s["lse"], dya,
        exchange=_both(_scatter_to_chips(parts) if riding else None, _scatter_to_sibling(early) if scatter_own else None))
    from_chips, early_recv = (landed[:len(BIG)], landed[len(BIG):]) if riding else (None, landed)
    early_parts = _pair_sums(BIG[2:], early, early_recv, core) if scatter_own else None
    (dpa, d_qg2, d_kg2), early_from_chips = _attn_prep_bwd(
        s["pa"], cs, sn, w["qg2"], w["kg2"], dq, dkT.reshape(-1, t), dvT.reshape(-1, t), dya, s["o"],
        exchange=_scatter_to_chips(early_parts) if scatter_own else None)
    dpb, d_w_s, d_b_s, d_ln_g, d_ln_b = _gmlp_bwd(s["pb"], w["ln_g"], w["ln_b"], w["w_s"], w["b_s"], dyb)
    dpm, dkv = _mem_attn_bwd(s["pm"], s["kv"], dym)
    d_w_kv = _mm_tn(s["memn"], dkv, "d_w_kv")
    dmemn = _mm(dkv, w["w_kv"], "nt", "d_memn")
    d_mem_g = _rmsnorm_dg(mem, w["mem_g"], dmemn, "mem_rmsnorm_bwd")
    dps = (dpa, dpb, dpm, dpg)
    grads.update(w_in=row_shards(_d_w_in_t(dps, s["h"])), w_mem_kv=row_shards(d_w_kv))
    own = None
    if scatter_own:
        late = [grads[n] for n in BIG[:2]]
        late_parts = _pair_sums(BIG[:2], late, _exchange_call(_scatter_to_sibling(late), "rs_sibling_swap"), core)
        own = _scatter_to_chips(late_parts)
    (dx_in, d_norm_g), late_from_chips = _dh_rmsnorm_bwd(dps, w["w_in_t"], s["x"], w["norm_g"], dx_out, exchange=own)
    if scatter_own:
        own_parts, own_from_chips = late_parts + early_parts, tuple(late_from_chips) + tuple(early_from_chips)
    grads.update(norm_g=d_norm_g[0], q_norm_g=d_qg2[0, :HEAD_DIM] + d_qg2[0, HEAD_DIM:],
                 k_norm_g=d_kg2[0, :HEAD_DIM] + d_kg2[0, HEAD_DIM:], sg_ln_g=d_ln_g[0], sg_ln_b=d_ln_b[0],
                 w_s=d_w_s, b_s=d_b_s[:, :, 0], mem_norm_g=d_mem_g[0])
    return (dx_in, grads, ((parts, from_chips) if riding else None),
            ((own_parts, own_from_chips) if scatter_own else None))


def _layer_weights(l, w_in_t, w_kv, g_br, w_out, small):
    tile2 = lambda g: jnp.tile(g.reshape(1, -1), (1, 2))
    return dict(
        w_in_t=w_in_t, w_kv=w_kv, g_br=g_br, w_out=w_out,
        norm_g=small["norm_g"][l].reshape(1, -1), qg2=tile2(small["q_norm_g"][l]), kg2=tile2(small["k_norm_g"][l]),
        ln_g=small["sg_ln_g"][l].reshape(1, -1), ln_b=small["sg_ln_b"][l].reshape(1, -1),
        w_s=small["w_s"][l], b_s=small["b_s"][l][:, :, None], mem_g=small["mem_norm_g"][l].reshape(1, -1))


def _position():
    x, y, c = lax.axis_index("x"), lax.axis_index("y"), lax.axis_index("c")
    return x, y, c, [(1 - x, y), (x, 1 - y), (1 - x, 1 - y)]


def _gather_first_hop(shards):
    n = len(shards)

    def make(x_refs, out_refs, sems):
        send_sems, recv_sems, local_sems = sems
        x, y, c, chips = _position()
        me = 4 * x + 2 * y + c
        peers = [(x, y, 1 - c)] + [(cx, cy, c) for cx, cy in chips]
        copies = [pltpu.make_async_copy(x_refs[t], out_refs[t].at[me], local_sems.at[t]) for t in range(n)]
        copies += [pltpu.make_async_remote_copy(
            src_ref=x_refs[t], dst_ref=out_refs[t].at[me], send_sem=send_sems.at[t, k], recv_sem=recv_sems.at[t, k],
            device_id=peer, device_id_type=MESH_ID) for t in range(n) for k, peer in enumerate(peers)]
        return copies

    return _Exchange(shards, [SDS((N_DEV,) + a.shape, a.dtype) for a in shards],
                     [pltpu.SemaphoreType.DMA((n, 4)), pltpu.SemaphoreType.DMA((n, 4)), pltpu.SemaphoreType.DMA((n,))], make)


def _gather_forward(gathered):
    n = len(gathered)

    def make(in_refs, out_refs, sems):
        send_sems, recv_sems = sems
        x, y, c, chips = _position()
        return [pltpu.make_async_remote_copy(
            src_ref=in_refs[t].at[4 * cx + 2 * cy + c], dst_ref=out_refs[t].at[4 * cx + 2 * cy + c],
            send_sem=send_sems.at[t, j], recv_sem=recv_sems.at[t, j], device_id=(x, y, 1 - c), device_id_type=MESH_ID)
            for t in range(n) for j, (cx, cy) in enumerate(chips)]

    return _Exchange(gathered, [SDS(a.shape, a.dtype) for a in gathered],
                     [pltpu.SemaphoreType.DMA((n, 3)), pltpu.SemaphoreType.DMA((n, 3))], make,
                     aliases={t: t for t in range(n)})


def _all_gather(shards, name):
    return _exchange_call(_gather_forward(_exchange_call(_gather_first_hop(shards), name + "_hop1")), name + "_hop2")


def _scatter_to_sibling(dests):
    n = len(dests)

    def make(d_refs, recv_refs, sems):
        send_sems, recv_sems = sems
        x, y, c, _ = _position()
        return [pltpu.make_async_remote_copy(
            src_ref=d_refs[t].at[:, 1 - c], dst_ref=recv_refs[t], send_sem=send_sems.at[t],
            recv_sem=recv_sems.at[t], device_id=(x, y, 1 - c), device_id_type=MESH_ID) for t in range(n)]

    return _Exchange(dests, [SDS(a.shape[:1] + a.shape[2:], a.dtype) for a in dests],
                     [pltpu.SemaphoreType.DMA((n,)), pltpu.SemaphoreType.DMA((n,))], make)


def _scatter_to_chips(parts):
    n = len(parts)

    def make(p_refs, recv_refs, sems):
        send_sems, recv_sems = sems
        _, _, c, chips = _position()
        return [pltpu.make_async_remote_copy(
            src_ref=p_refs[t].at[2 * cx + cy], dst_ref=recv_refs[t].at[k], send_sem=send_sems.at[t, k],
            recv_sem=recv_sems.at[t, k], device_id=(cx, cy, c), device_id_type=MESH_ID)
            for t in range(n) for k, (cx, cy) in enumerate(chips)]

    return _Exchange(parts, [SDS((3,) + a.shape[1:], a.dtype) for a in parts],
                     [pltpu.SemaphoreType.DMA((n, 3)), pltpu.SemaphoreType.DMA((n, 3))], make)


def _pair_sums(names, dests, recv, core):
    n = len(dests)

    def body(core_ref, *refs):
        for a_ref, b_ref, o_ref in zip(refs[:n], refs[n:2 * n], refs[2 * n:]):
            o_ref[...] = (a_ref[...].astype(F32) + b_ref[...].astype(F32)).astype(o_ref.dtype)

    mine = [pl.BlockSpec((None, None) + d.shape[2:], lambda j, core_ref: (j, core_ref[0], 0, 0, 0)) for d in dests]
    theirs = [pl.BlockSpec((None,) + r.shape[1:], lambda j, core_ref: (j, 0, 0, 0)) for r in recv]
    return list(pl.pallas_call(
        body, name="rs_pair_sum_" + "_".join(names), out_shape=tuple(SDS(r.shape, BF16) for r in recv),
        grid_spec=pltpu.PrefetchScalarGridSpec(
            num_scalar_prefetch=1, grid=(4,), in_specs=mine + theirs, out_specs=tuple(theirs)),
        compiler_params=_params("parallel"),
    )(core, *dests, *recv))


def _adamw_math(w, g, m, v):
    m = ADAM_B1 * m + (1.0 - ADAM_B1) * g
    v = ADAM_B2 * v + (1.0 - ADAM_B2) * (g * g)
    m_hat = m / (1.0 - ADAM_B1 ** ADAM_STEP)
    v_hat = v / (1.0 - ADAM_B2 ** ADAM_STEP)
    delta = -ADAM_LR * (m_hat / (jnp.sqrt(v_hat) + ADAM_EPS) + ADAM_WD * w)
    return delta, m, v


def _sum_and_adamw(parts, w, m, v, name):
    n, r, ln = parts.shape
    tr = _tile(r, 512, 16)

    def body(p_ref, w_ref, m_ref, v_ref, g_out, d_out, m_out, v_out):
        g = p_ref[0].astype(F32)
        for j in range(1, n):
            g = g + p_ref[j].astype(F32)
        delta, nm, nv = _adamw_math(w_ref[...], g, m_ref[...], v_ref[...])
        g_out[...] = g
        d_out[...] = delta
        m_out[...] = nm
        v_out[...] = nv

    blk = pl.BlockSpec((tr, ln), lambda i: (i, 0))
    return pl.pallas_call(
        body, name=name, grid=(r // tr,),
        in_specs=[pl.BlockSpec((n, tr, ln), lambda i: (0, i, 0)), blk, blk, blk],
        out_specs=(blk, blk, blk, blk), out_shape=tuple(SDS((r, ln), F32) for _ in range(4)),
        compiler_params=_params("parallel"),
    )(parts, w, m, v)


BIG = ("w_in", "w_mem_kv", "w_br", "w_out")
SMALL = ("norm_g", "q_norm_g", "k_norm_g", "sg_ln_g", "sg_ln_b", "w_s", "b_s", "mem_norm_g", "final_g")


def _pack(arrs, row_unit=16):
    flat = jnp.concatenate([a.reshape(-1) for a in arrs])
    pad = (-flat.shape[0]) % (row_unit * LANES)
    if pad:
        flat = jnp.concatenate([flat, jnp.zeros((pad,), flat.dtype)])
    return flat.reshape(-1, LANES)


def _unpack(buf, shapes):
    flat = buf.reshape(-1)
    out, off = [], 0
    for shp in shapes:
        n = 1
        for s_ in shp:
            n *= s_
        out.append(flat[off:off + n].reshape(shp))
        off += n
    return out


def _shard_sum_adamw(part, from_chips, chip, w, m, v, layer, prev, name):
    _, na, r, cdim = part.shape
    flat = w.ndim == 3
    assert not flat or na == 1
    tr = _tile(r, max(8, (256 * 1024) // (na * cdim)), 8)
    n_prev = 0 if prev is None else len(prev)

    def body(chip_ref, p_ref, f_ref, w_ref, m_ref, v_ref, *rest):
        g_out, d_out, m_out, v_out = rest[n_prev:]
        g = p_ref[...].astype(F32)
        for j in range(3):
            g = g + f_ref[j].astype(F32)
        delta, nm, nv = _adamw_math(w_ref[...], g, m_ref[...], v_ref[...])
        g_out[...] = g
        d_out[...] = delta
        m_out[...] = nm
        v_out[...] = nv

    a_blk = None if flat else na
    if flat:
        lay = pl.BlockSpec((None, tr, cdim), lambda i, chip_ref: (layer, i, 0))
    else:
        lay = pl.BlockSpec((None, na, tr, cdim), lambda i, chip_ref: (layer, 0, i, 0))
    return pl.pallas_call(
        body, name=name, out_shape=tuple(SDS(w.shape, F32) for _ in range(4)),
        grid_spec=pltpu.PrefetchScalarGridSpec(
            num_scalar_prefetch=1, grid=(r // tr,),
            in_specs=[pl.BlockSpec((None, a_blk, tr, cdim), lambda i, chip_ref: (chip_ref[0], 0, i, 0)),
                      pl.BlockSpec((3, a_blk, tr, cdim), lambda i, chip_ref: (0, 0, i, 0)), lay, lay, lay]
            + [HBM] * n_prev,
            out_specs=(lay, lay, lay, lay)),
        input_output_aliases={6 + j: j for j in range(n_prev)},
        compiler_params=_params("parallel"),
    )(chip, part, from_chips, w, m, v, *(prev or ()))


def kernel(x, mem, norm_g, w_in, q_norm_g, k_norm_g, sg_ln_g, sg_ln_b, w_s, b_s, mem_norm_g, w_mem_kv, w_br, w_out, final_g, loss_target, m_norm_g, m_w_in, m_q_norm_g, m_k_norm_g, m_sg_ln_g, m_sg_ln_b, m_w_s, m_b_s, m_mem_norm_g, m_w_mem_kv, m_w_br, m_w_out, m_final_g, v_norm_g, v_w_in, v_q_norm_g, v_k_norm_g, v_sg_ln_g, v_sg_ln_b, v_w_s, v_b_s, v_mem_norm_g, v_w_mem_kv, v_w_br, v_w_out, v_final_g):
    wts = dict(norm_g=norm_g, w_in=w_in, q_norm_g=q_norm_g, k_norm_g=k_norm_g, sg_ln_g=sg_ln_g, sg_ln_b=sg_ln_b,
               w_s=w_s, b_s=b_s, mem_norm_g=mem_norm_g, w_mem_kv=w_mem_kv, w_br=w_br, w_out=w_out, final_g=final_g)
    mom1 = dict(norm_g=m_norm_g, w_in=m_w_in, q_norm_g=m_q_norm_g, k_norm_g=m_k_norm_g, sg_ln_g=m_sg_ln_g,
                sg_ln_b=m_sg_ln_b, w_s=m_w_s, b_s=m_b_s, mem_norm_g=m_mem_norm_g, w_mem_kv=m_w_mem_kv, w_br=m_w_br,
                w_out=m_w_out, final_g=m_final_g)
    mom2 = dict(norm_g=v_norm_g, w_in=v_w_in, q_norm_g=v_q_norm_g, k_norm_g=v_k_norm_g, sg_ln_g=v_sg_ln_g,
                sg_ln_b=v_sg_ln_b, w_s=v_w_s, b_s=v_b_s, mem_norm_g=v_mem_norm_g, w_mem_kv=v_w_mem_kv, w_br=v_w_br,
                w_out=v_w_out, final_g=v_final_g)
    dp = w_in.shape[0]
    core = lax.axis_index("c").astype(jnp.int32).reshape(1)
    chip = (2 * lax.axis_index("x") + lax.axis_index("y")).astype(jnp.int32).reshape(1)

    t_in = lambda a: jnp.swapaxes(a, 1, 2)
    wts, mom1, mom2 = [dict(d_, w_in=t_in(d_["w_in"])) for d_ in (wts, mom1, mom2)]

    shard_bf = {n: wts[n].astype(BF16) for n in BIG}
    shards = lambda l: [shard_bf[n][l] for n in BIG]
    x_l, mem_l = x[0], mem[0]
    tabs = _rope_tables(x_l.shape[0])

    gathered = _all_gather(shards(0), "weights_all_gather")
    layers, saved = [], []
    for l in range(dp):
        g_in, g_kv, g_br, g_out = gathered
        layers.append(_layer_weights(l, g_in.reshape(IN_WIDTH, -1), g_kv.reshape(D_MODEL, -1), g_br,
                                     g_out.reshape(D_MODEL, -1), wts))
        x_l, s, gathered = _layer_fwd(x_l, mem_l, layers[l], tabs, next_shards=shards(l + 1) if l + 1 < dp else None)
        saved.append(s)
    loss_local, dx, d_final_g = _loss_head(x_l, final_g.reshape(1, -1), loss_target[0])
    loss = lax.psum(loss_local[0, 0], AXES)

    def finish(l, parts, from_chips, prev):
        return {n: _shard_sum_adamw(p, f, chip, wts[n], mom1[n], mom2[n], l, None if prev is None else prev[n],
                                    "sum_adamw_" + n)
                for n, p, f in zip(BIG, parts, from_chips)}

    grads, updated, pending = [None] * dp, None, None
    for l in reversed(range(dp)):
        dx, grads[l], scattered, own = _layer_bwd(dx, mem_l, layers[l], saved[l], tabs, pending=pending, core=core,
                                                  scatter_own=(l == 0))
        if scattered is not None:
            updated = finish(l + 1, *scattered, updated)
        pending = [grads[l][n] for n in BIG]
    updated = finish(0, *own, updated)
    grad_x = dx
    big_out = [{n: (t_in(updated[n][k]) if n == "w_in" else updated[n][k]) for n in BIG} for k in range(4)]

    small_g = {n: jnp.stack([g[n] for g in grads]) for n in SMALL if n != "final_g"}
    small_g["final_g"] = d_final_g
    (all_small,) = _all_gather([_pack([small_g[n] for n in SMALL])], "small_all_gather")
    small_bufs = _sum_and_adamw(
        all_small, _pack([wts[n] for n in SMALL]), _pack([mom1[n] for n in SMALL]), _pack([mom2[n] for n in SMALL]),
        "small_sum_adamw")

    outs = []
    for big_vals, small_buf in zip(big_out, small_bufs):
        vals = dict(big_vals)
        vals.update(zip(SMALL, _unpack(small_buf, [wts[n].shape for n in SMALL])))
        outs.append(vals)
    order = ("norm_g", "w_in", "q_norm_g", "k_norm_g", "sg_ln_g", "sg_ln_b", "w_s", "b_s", "mem_norm_g", "w_mem_kv",
             "w_br", "w_out", "final_g")
    result = [loss, grad_x[None]]
    for vals in outs:
        result += [vals[n] for n in order]
    return tuple(result)
```

```python
import functools

import jax
import jax.numpy as jnp
from jax import lax
from jax.experimental import pallas as pl
from jax.experimental.pallas import tpu as pltpu

F32 = jnp.float32
BF16 = jnp.bfloat16
SDS = jax.ShapeDtypeStruct
MESH_ID = pl.DeviceIdType.MESH
AXES = ("x", "y", "c")
N_DEV = 8

D_MODEL = 1024
DEPTH = 4
GRID_W = 64
CHUNK = 128
ROPE_THETA = 10000.0
EPS = 1e-6
HEAD_DIM = 64
A_HEADS = 8
A_KV_HEADS = 2
B_GROUPS = 4
M_HEADS = 4
M_HEAD_DIM = 128
BW = 512
W_A = 1280
W_B = 1536
W_M = 1024
W_G = 3072
IN_WIDTH = W_A + W_B + W_M + W_G
GROUP_OFFS = (0, W_A, W_A + W_B, W_A + W_B + W_M, IN_WIDTH)

ADAM_LR = 0.001
ADAM_B1 = 0.9
ADAM_B2 = 0.999
ADAM_EPS = 1e-08
ADAM_WD = 0.01
ADAM_STEP = 10

LANES = 128
KEY_CHUNK = 2048
VMEM_LIMIT = 52 * 1024 * 1024


def _tile(n, cap, unit=LANES):
    if n <= cap:
        return n
    t = (cap // unit) * unit
    while n % t:
        t -= unit
    return t


def _params(*sem):
    return pltpu.CompilerParams(dimension_semantics=sem, vmem_limit_bytes=VMEM_LIMIT)


def _sigmoid(z):
    return 0.5 * jnp.tanh(0.5 * z) + 0.5


def _silu_and_grad(z):
    s = _sigmoid(z)
    return z * s, s * (1.0 + z * (1.0 - s))


def _dot(a, b, dims):
    return lax.dot_general(a, b, (dims, ((), ())), preferred_element_type=F32)


NN = ((1,), (0,))
NT = ((1,), (1,))
TN = ((0,), (0,))
HBM = pl.BlockSpec(memory_space=pl.ANY)


class _Exchange:
    def __init__(self, ins, out_shapes, sems, make, aliases=None):
        self.ins, self.out_shapes, self.sems, self.make = list(ins), list(out_shapes), list(sems), make
        self.aliases = dict(aliases or {})

    def start(self, in_refs, out_refs, sems):
        for cp in self.make(in_refs, out_refs, sems):
            cp.start()

    def finish(self, in_refs, out_refs, sems):
        for cp in self.make(in_refs, out_refs, sems):
            cp.wait()


def _both(a, b):
    if a is None or b is None:
        return a if b is None else b
    n_in, n_out, n_sem = len(a.ins), len(a.out_shapes), len(a.sems)

    def make(in_refs, out_refs, sems):
        return (a.make(in_refs[:n_in], out_refs[:n_out], sems[:n_sem])
                + b.make(in_refs[n_in:], out_refs[n_out:], sems[n_sem:]))

    aliases = dict(a.aliases)
    aliases.update({n_in + i: n_out + o for i, o in b.aliases.items()})
    return _Exchange(a.ins + b.ins, a.out_shapes + b.out_shapes, a.sems + b.sems, make, aliases)


def _host_call(body, ins, in_specs, out_specs, out_shape, *, name, grid, semantics, scratch=(), exchange=None):
    ins, out_shape, scratch = list(ins), list(out_shape), list(scratch)
    if exchange is None:
        res = pl.pallas_call(
            body, name=name, grid=grid, in_specs=list(in_specs), out_specs=tuple(out_specs), out_shape=tuple(out_shape),
            scratch_shapes=scratch, compiler_params=_params(*semantics))(*ins)
        return tuple(res), ()
    n_in, n_out, n_scr = len(ins), len(out_shape), len(scratch)
    x_in, x_out = len(exchange.ins), len(exchange.out_shapes)

    def carrying(*refs):
        o0 = n_in + x_in
        s0 = o0 + n_out + x_out
        c_in, c_out, c_sems = refs[n_in:o0], refs[o0 + n_out:s0], refs[s0 + n_scr:]
        ids = [pl.program_id(a) for a in range(len(grid))]
        first = functools.reduce(jnp.logical_and, [i == 0 for i in ids])
        last = functools.reduce(jnp.logical_and, [i == g - 1 for i, g in zip(ids, grid)])

        @pl.when(first)
        def _():
            exchange.start(c_in, c_out, c_sems)

        body(*refs[:n_in], *refs[o0:o0 + n_out], *refs[s0:s0 + n_scr])

        @pl.when(last)
        def _():
            exchange.finish(c_in, c_out, c_sems)

    res = pl.pallas_call(
        carrying, name=name, grid=grid, in_specs=list(in_specs) + [HBM] * x_in,
        out_specs=tuple(out_specs) + (HBM,) * x_out, out_shape=tuple(out_shape) + tuple(exchange.out_shapes),
        scratch_shapes=scratch + exchange.sems,
        input_output_aliases={n_in + i: n_out + o for i, o in exchange.aliases.items()},
        compiler_params=_params(*(["arbitrary"] * len(grid))))(*ins, *exchange.ins)
    return tuple(res[:n_out]), tuple(res[n_out:])


def _exchange_call(exchange, name):
    x_in = len(exchange.ins)

    def body(*refs):
        x_out = len(exchange.out_shapes)
        c_in, c_out, c_sems = refs[:x_in], refs[x_in:x_in + x_out], refs[x_in + x_out:]
        exchange.start(c_in, c_out, c_sems)
        exchange.finish(c_in, c_out, c_sems)

    return pl.pallas_call(
        body, name=name, in_specs=[HBM] * x_in, out_specs=tuple([HBM] * len(exchange.out_shapes)),
        out_shape=tuple(exchange.out_shapes), scratch_shapes=exchange.sems, input_output_aliases=exchange.aliases,
    )(*exchange.ins)


def _mem_kv_fwd(mem, g, w_kv):
    def body(mem_ref, g_ref, w_ref, memn_ref, kv_ref):
        xf = mem_ref[...]
        r = lax.rsqrt(jnp.mean(xf * xf, axis=-1, keepdims=True) + EPS)
        memn = (xf * r * g_ref[...]).astype(BF16)
        memn_ref[...] = memn
        kv_ref[...] = _dot(memn, w_ref[...], NN)

    return pl.pallas_call(
        body, name="mem_kv_fwd", out_shape=(SDS(mem.shape, BF16), SDS((mem.shape[0], w_kv.shape[1]), F32)),
        compiler_params=pltpu.CompilerParams(vmem_limit_bytes=VMEM_LIMIT))(mem, g, w_kv)


def _mem_kv_bwd(mem, g, memn, w_kv, dkv):
    def body(mem_ref, g_ref, memn_ref, w_ref, dkv_ref, dw_ref, dg_ref):
        dkv_b = dkv_ref[...].astype(BF16)
        dw_ref[...] = _dot(memn_ref[...], dkv_b, TN).astype(BF16)
        dmemn = _dot(dkv_b, w_ref[...], NT)
        dg_ref[...] = _rmsnorm_bwd_math(mem_ref[...], g_ref[...], dmemn)[1]

    return pl.pallas_call(
        body, name="mem_kv_bwd", out_shape=(SDS(w_kv.shape, BF16), SDS(g.shape, F32)),
        compiler_params=pltpu.CompilerParams(vmem_limit_bytes=VMEM_LIMIT))(mem, g, memn, w_kv, dkv)


def _mm_tn(a, b, name):
    (k, m), (k2, n) = a.shape, b.shape
    assert k == k2, (a.shape, b.shape)
    tk = _tile(k, 2048, 16)
    nk = k // tk

    def body(a_ref, b_ref, o_ref, acc):
        kk = pl.program_id(0)
        part = _dot(a_ref[...].astype(BF16), b_ref[...].astype(BF16), TN)

        @pl.when(kk == 0)
        def _():
            acc[...] = part

        @pl.when(kk > 0)
        def _():
            acc[...] += part

        @pl.when(kk == nk - 1)
        def _():
            o_ref[...] = acc[...].astype(BF16)

    return pl.pallas_call(
        body, name=name, grid=(nk,),
        in_specs=[pl.BlockSpec((tk, m), lambda kk: (kk, 0)), pl.BlockSpec((tk, n), lambda kk: (kk, 0))],
        out_specs=pl.BlockSpec((m, n), lambda kk: (0, 0)), out_shape=SDS((m, n), BF16),
        scratch_shapes=[pltpu.VMEM((m, n), F32)], compiler_params=_params("arbitrary"),
    )(a, b)


def _rmsnorm_bwd_math(xf, g, dh):
    r = lax.rsqrt(jnp.mean(xf * xf, axis=-1, keepdims=True) + EPS)
    xh = xf * r
    gd = dh * g
    dx = r * (gd - xh * jnp.mean(gd * xh, axis=-1, keepdims=True))
    return dx, jnp.sum(dh * xh, axis=0, keepdims=True)


def _group_rows(p):
    return slice(GROUP_OFFS[p], GROUP_OFFS[p + 1])


def _d_w_in_t(dps, h):
    t, d = h.shape
    tm = 256
    assert all(off % tm == 0 for off in GROUP_OFFS)
    tiles = [a.shape[1] // tm for a in dps]
    starts = [sum(tiles[:p]) for p in range(len(dps))]

    def body(*refs):
        a_refs, h_ref, o_ref = refs[:len(dps)], refs[len(dps)], refs[len(dps) + 1]
        i = pl.program_id(0)
        for p, a_ref in enumerate(a_refs):
            @pl.when((i >= starts[p]) & (i < starts[p] + tiles[p]))
            def _():
                o_ref[...] = _dot(a_ref[...], h_ref[...], TN).astype(BF16)

    def tile_of(p):
        return lambda i: jnp.clip(i - starts[p], 0, tiles[p] - 1)

    return pl.pallas_call(
        body, name="d_w_in", grid=(sum(tiles),),
        in_specs=[pl.BlockSpec((t, tm), lambda i, p=p: (0, tile_of(p)(i))) for p in range(len(dps))]
        + [pl.BlockSpec((t, d), lambda i: (0, 0), pipeline_mode=pl.Buffered(1))],
        out_specs=pl.BlockSpec((tm, d), lambda i: (i, 0)),
        out_shape=SDS((sum(tiles) * tm, d), BF16), compiler_params=_params("arbitrary"),
    )(*dps, h)


def _norm_in_proj(x, g, w_t, cs, sn, qg2, kg2, lng, lnb, ws, bs, kv):
    t, d = x.shape
    tm = _tile(t, 512, LANES)
    widths = [GROUP_OFFS[p + 1] - GROUP_OFFS[p] for p in range(4)]
    scale = HEAD_DIM ** -0.5

    def body(x_ref, g_ref, w_ref, cs_ref, sn_ref, qg_ref, kg_ref, lng_ref, lnb_ref, ws_ref, bs_ref, kv_ref,
             h_ref, pa_ref, pb_ref, pm_ref, pg_ref, q_ref, qT_ref, k_ref, kT_ref, vT_ref, yb_ref, ym_ref):
        xf = x_ref[...]
        r = lax.rsqrt(jnp.mean(xf * xf, axis=-1, keepdims=True) + EPS)
        hb = (xf * r * g_ref[...]).astype(BF16)
        h_ref[...] = hb
        groups = [_dot(hb, w_ref[_group_rows(p), :], NT).astype(BF16) for p in range(4)]
        for o_ref, val in zip((pa_ref, pb_ref, pm_ref, pg_ref), groups):
            o_ref[...] = val
        pa = groups[0]
        lane = lax.broadcasted_iota(jnp.int32, (tm, LANES), 1)
        c, s = cs_ref[...], sn_ref[...]

        def norm_rope(xg, g2):
            rr = lax.rsqrt(_head_sums(xg * xg, lane) * (1.0 / HEAD_DIM) + EPS)
            xn = xg * rr * g2
            return xn * c + _swap16(xn, lane) * s

        for gi in range(4):
            sl = slice(gi * LANES, (gi + 1) * LANES)
            qr = norm_rope(pa[:, sl].astype(F32), qg_ref[...]) * scale
            q_ref[:, sl] = qr.astype(BF16)
            qT_ref[sl, :] = qr.T.astype(BF16)
        kr = norm_rope(pa[:, 512:640].astype(F32), kg_ref[...])
        kT_ref[...] = kr.T.astype(BF16)
        vT_ref[...] = pa[:, 640:768].astype(F32).T.astype(BF16)
        kr = kr.astype(BF16)
        for kvh in range(A_KV_HEADS):
            k_ref[kvh] = kr[:, kvh * HEAD_DIM:(kvh + 1) * HEAD_DIM]
        _gmlp_tile(groups[1], lng_ref, lnb_ref, ws_ref, bs_ref, yb_ref)
        _mem_attn_tile(groups[2], kv_ref, ym_ref)

    row = lambda wd: pl.BlockSpec((tm, wd), lambda i: (i, 0))
    col = lambda rws: pl.BlockSpec((rws, tm), lambda i: (0, i))
    whole = lambda a: pl.BlockSpec(a.shape, lambda i: (0,) * a.ndim)
    hm = pl.BlockSpec((A_KV_HEADS, tm, HEAD_DIM), lambda i: (0, i, 0))
    return pl.pallas_call(
        body, name="norm_in_proj", grid=(t // tm,),
        in_specs=[row(d), whole(g), pl.BlockSpec(w_t.shape, lambda i: (0, 0), pipeline_mode=pl.Buffered(1)),
                  row(LANES), row(LANES)] + [whole(a) for a in (qg2, kg2, lng, lnb, ws, bs, kv)],
        out_specs=(row(d),) + tuple(row(wd) for wd in widths)
        + (row(BW), col(BW), hm, col(LANES), col(LANES), row(BW), row(BW)),
        out_shape=(SDS((t, d), BF16),) + tuple(SDS((t, wd), BF16) for wd in widths)
        + (SDS((t, BW), BF16), SDS((BW, t), BF16), SDS((A_KV_HEADS, t, HEAD_DIM), BF16),
           SDS((LANES, t), BF16), SDS((LANES, t), BF16), SDS((t, BW), BF16), SDS((t, BW), BF16)),
        compiler_params=_params("parallel"),
    )(x, g, w_t, cs, sn, qg2, kg2, lng, lnb, ws, bs, kv)


def _dh_rmsnorm_bwd(dps, w_t, x, g, dres, exchange=None):
    t, d = x.shape
    tm = _tile(t, 256, 16)
    n = len(dps)

    def body(*refs):
        dp_refs, w_ref = refs[:n], refs[n]
        x_ref, g_ref, dres_ref, dx_ref, dg_ref = refs[n + 1:]
        dh = None
        for p, dp_ref in enumerate(dp_refs):
            part = _dot(dp_ref[...], w_ref[_group_rows(p), :], NN)
            dh = part if dh is None else dh + part
        dx, dg = _rmsnorm_bwd_math(x_ref[...], g_ref[...], dh)
        dx_ref[...] = dx + dres_ref[...]

        @pl.when(pl.program_id(0) == 0)
        def _():
            dg_ref[...] = jnp.zeros_like(dg_ref)

        dg_ref[...] += dg

    row = pl.BlockSpec((tm, d), lambda i: (i, 0))
    vec = pl.BlockSpec((1, d), lambda i: (0, 0))
    return _host_call(
        body, list(dps) + [w_t, x, g, dres],
        [pl.BlockSpec((tm, a.shape[1]), lambda i: (i, 0)) for a in dps]
        + [pl.BlockSpec(w_t.shape, lambda i: (0, 0), pipeline_mode=pl.Buffered(1)), row, vec, row],
        (row, vec), (SDS((t, d), F32), SDS((1, d), F32)),
        name="dh_rmsnorm_bwd", grid=(t // tm,), semantics=("arbitrary",), exchange=exchange)


def _rope_tables(t):
    rows = t // GRID_W
    row = jnp.repeat(jnp.arange(rows, dtype=F32), GRID_W)
    col = jnp.tile(jnp.arange(GRID_W, dtype=F32), rows)
    n_freq = HEAD_DIM // 4
    inv = ROPE_THETA ** (-jnp.arange(n_freq, dtype=F32) / n_freq)
    ang = jnp.stack([row[:, None] * inv, col[:, None] * inv], axis=1)
    cos, sin = jnp.cos(ang), jnp.sin(ang)
    c64 = jnp.concatenate([cos[:, 0], cos[:, 0], cos[:, 1], cos[:, 1]], axis=-1)
    s64 = jnp.concatenate([-sin[:, 0], sin[:, 0], -sin[:, 1], sin[:, 1]], axis=-1)
    return jnp.tile(c64, (1, 2)), jnp.tile(s64, (1, 2))


def _head_sums(v, lane):
    lo = jnp.sum(jnp.where(lane < HEAD_DIM, v, 0.0), axis=-1, keepdims=True)
    hi = jnp.sum(jnp.where(lane < HEAD_DIM, 0.0, v), axis=-1, keepdims=True)
    return jnp.where(lane < HEAD_DIM, lo, hi)


def _swap16(v, lane):
    return jnp.where((lane % 32) < 16, pltpu.roll(v, LANES - 16, 1), pltpu.roll(v, 16, 1))


def _attn_fwd(qT, k, vT, pa, exchange=None):
    t = qT.shape[1]
    tq = _tile(t, 256, LANES)
    grp = A_HEADS // A_KV_HEADS

    ck = _tile(t, KEY_CHUNK, LANES)

    def body(qT_ref, k_ref, vT_ref, pa_ref, o_ref, lse_ref, y_ref):
        def finish_pair(pair, o_pair):
            o2 = jnp.concatenate(o_pair, axis=0).T
            cols = slice(pair * LANES, (pair + 1) * LANES)
            o_ref[:, cols] = o2
            z = pa_ref[:, 768 + pair * LANES:768 + (pair + 1) * LANES].astype(F32)
            y_ref[:, cols] = (o2 * (z * _sigmoid(z))).astype(BF16)

        n_c = t // ck
        items = [(h, c) for h in range(A_HEADS) for c in range(n_c)]

        def scores(h, c):
            return _dot(k_ref[h // grp, c * ck:(c + 1) * ck, :], qT_ref[h * HEAD_DIM:(h + 1) * HEAD_DIM, :], NN)

        def weighted_values(h, c, pT):
            kvh = h // grp
            return _dot(vT_ref[kvh * HEAD_DIM:(kvh + 1) * HEAD_DIM, c * ck:(c + 1) * ck], pT, NN)

        o_pair, lse_rows, sT, pending, acc = [], [], scores(*items[0]), None, None
        for i in range(len(items) + 1):
            item = items[i] if i < len(items) else None
            sT_next = scores(*items[i + 1]) if i + 1 < len(items) else None
            if pending is not None:
                (ph, pc, pT, alpha, l_run) = pending
                o_c = weighted_values(ph, pc, pT)
                acc = o_c if alpha is None else acc * alpha + o_c
                if pc == n_c - 1:
                    o_pair.append(acc / l_run)
                    if ph % 2:
                        finish_pair(ph // 2, o_pair)
                        o_pair = []
                pending = None
            if item is not None:
                h, c = item
                m_c = jnp.max(sT, axis=0, keepdims=True)
                if c == 0:
                    m_new, alpha = m_c, None
                else:
                    m_new = jnp.maximum(m, m_c)
                    alpha = jnp.exp(m - m_new)
                pT = jnp.exp(sT - m_new)
                l_c = jnp.sum(pT, axis=0, keepdims=True)
                l = l_c if c == 0 else l * alpha + l_c
                m = m_new
                if c == n_c - 1:
                    lse_rows.append(m + jnp.log(l))
                pending = (h, c, pT.astype(BF16), alpha, l)
            sT = sT_next
        lse_cols = jnp.concatenate(lse_rows, axis=0).T
        for kvh in range(A_KV_HEADS):
            lse_ref[kvh] = lse_cols[:, kvh * grp:(kvh + 1) * grp]

    row = lambda w: pl.BlockSpec((tq, w), lambda i: (i, 0))
    return _host_call(
        body, (qT, k, vT, pa),
        [pl.BlockSpec((BW, tq), lambda i: (0, i)), pl.BlockSpec((A_KV_HEADS, t, HEAD_DIM), lambda i: (0, 0, 0)),
         pl.BlockSpec((A_KV_HEADS * HEAD_DIM, t), lambda i: (0, 0)), row(W_A)],
        (row(BW), pl.BlockSpec((A_KV_HEADS, tq, grp), lambda i: (0, i, 0)), row(BW)),
        (SDS((t, BW), F32), SDS((A_KV_HEADS, t, grp), F32), SDS((t, BW), BF16)),
        name="attn_fwd", grid=(t // tq,), semantics=("parallel",), exchange=exchange)


def _attn_bwd(q, qT, k, kT, vT, pa, o, lse, dy, exchange=None):
    t = q.shape[0]
    tq = _tile(t, 256, LANES)
    grp = A_HEADS // A_KV_HEADS
    gw = grp * HEAD_DIM

    def body(q_ref, qT_ref, k_ref, kT_ref, vT_ref, z_ref, o_ref, lse_ref, dy_ref, dq_ref, dkT_ref, dvT_ref):
        @pl.when(pl.program_id(1) == 0)
        def _():
            dkT_ref[...] = jnp.zeros_like(dkT_ref)
            dvT_ref[...] = jnp.zeros_like(dvT_ref)

        z = z_ref[...].astype(F32)
        do = dy_ref[...] * (z * _sigmoid(z))
        doo = do * o_ref[...]
        doT = do.T
        kk, kT, vT = k_ref[...], kT_ref[...], vT_ref[...]
        heads = [slice(j * HEAD_DIM, (j + 1) * HEAD_DIM) for j in range(grp)]

        def scores(hs):
            return _dot(q_ref[:, hs], kT, NN), _dot(do[:, hs].astype(BF16), vT, NN)

        s, dp = scores(heads[0])
        for j, hs in enumerate(heads):
            nxt = scores(heads[j + 1]) if j + 1 < grp else None
            delta = jnp.sum(doo[:, hs], axis=-1, keepdims=True)
            p = jnp.exp(s - lse_ref[:, j:j + 1])
            ds_b = (p * (dp - delta)).astype(BF16)
            dq_ref[:, hs] = _dot(ds_b, kk, NN)
            dkT_ref[...] += _dot(qT_ref[hs, :], ds_b, NN)
            dvT_ref[...] += _dot(doT[hs, :].astype(BF16), p.astype(BF16), NN)
            if nxt is not None:
                s, dp = nxt

    grp_blk = pl.BlockSpec((tq, gw), lambda g, i: (i, g))
    kvT_blk = pl.BlockSpec((HEAD_DIM, t), lambda g, i: (g, 0))
    acc_blk = pl.BlockSpec((None, HEAD_DIM, t), lambda g, i: (g, 0, 0))
    return _host_call(
        body, (q, qT, k, kT, vT, pa, o, lse, dy),
        [grp_blk, pl.BlockSpec((gw, tq), lambda g, i: (g, i)), pl.BlockSpec((None, t, HEAD_DIM), lambda g, i: (g, 0, 0)),
         kvT_blk, kvT_blk, pl.BlockSpec((tq, gw), lambda g, i: (i, 768 // gw + g)), grp_blk,
         pl.BlockSpec((None, tq, grp), lambda g, i: (g, i, 0)), grp_blk],
        (grp_blk, acc_blk, acc_blk),
        (SDS((t, BW), F32), SDS((A_KV_HEADS, HEAD_DIM, t), F32), SDS((A_KV_HEADS, HEAD_DIM, t), F32)),
        name="attn_bwd", grid=(A_KV_HEADS, t // tq), semantics=("arbitrary", "arbitrary"), exchange=exchange)


def _attn_prep_bwd(pa, cs, sn, qg2, kg2, dq, dkT, dvT, dy, o, exchange=None):
    t = pa.shape[0]
    tq = _tile(t, 512, LANES)
    scale = HEAD_DIM ** -0.5

    def body(pa_ref, cs_ref, sn_ref, qg_ref, kg_ref, dq_ref, dkT_ref, dvT_ref, dy_ref, o_ref, dpa_ref, dqg_ref, dkg_ref):
        lane = lax.broadcasted_iota(jnp.int32, (tq, LANES), 1)
        c, s = cs_ref[...], sn_ref[...]

        @pl.when(pl.program_id(0) == 0)
        def _():
            dqg_ref[...] = jnp.zeros_like(dqg_ref)
            dkg_ref[...] = jnp.zeros_like(dkg_ref)

        def norm_rope_bwd(xg, g2, dout):
            r = lax.rsqrt(_head_sums(xg * xg, lane) * (1.0 / HEAD_DIM) + EPS)
            xh = xg * r
            dxn = dout * c + _swap16(dout * s, lane)
            gd = dxn * g2
            dx = r * (gd - xh * (_head_sums(gd * xh, lane) * (1.0 / HEAD_DIM)))
            return dx, jnp.sum(dxn * xh, axis=0, keepdims=True)

        for gi in range(4):
            sl = slice(gi * LANES, (gi + 1) * LANES)
            dx, dg = norm_rope_bwd(pa_ref[:, sl].astype(F32), qg_ref[...], dq_ref[:, sl] * scale)
            dpa_ref[:, sl] = dx.astype(BF16)
            dqg_ref[...] += dg
        dx, dg = norm_rope_bwd(pa_ref[:, 512:640].astype(F32), kg_ref[...], dkT_ref[...].T)
        dpa_ref[:, 512:640] = dx.astype(BF16)
        dkg_ref[...] += dg
        dpa_ref[:, 640:768] = dvT_ref[...].T.astype(BF16)
        z = pa_ref[:, 768:1280].astype(F32)
        _, dsilu = _silu_and_grad(z)
        dpa_ref[:, 768:1280] = (dy_ref[...] * o_ref[...] * dsilu).astype(BF16)

    row = lambda w: pl.BlockSpec((tq, w), lambda i: (i, 0))
    col = pl.BlockSpec((LANES, tq), lambda i: (0, i))
    vec = pl.BlockSpec((1, LANES), lambda i: (0, 0))
    return _host_call(
        body, (pa, cs, sn, qg2, kg2, dq, dkT, dvT, dy, o),
        [row(W_A), row(LANES), row(LANES), vec, vec, row(BW), col, col, row(BW), row(BW)],
        (row(W_A), vec, vec), (SDS((t, W_A), BF16), SDS((1, LANES), F32), SDS((1, LANES), F32)),
        name="attn_prep_bwd", grid=(t // tq,), semantics=("arbitrary",), exchange=exchange)


def _layer_norm(v, g, b):
    mu = jnp.mean(v, axis=-1, keepdims=True)
    xc = v - mu
    rs = lax.rsqrt(jnp.mean(xc * xc, axis=-1, keepdims=True) + EPS)
    xh = xc * rs
    return xh * g + b, xh, rs


def _gmlp_tile(pb, g_ref, b_ref, ws_ref, bs_ref, y_ref):
    vln, _, _ = _layer_norm(pb[:, BW:2 * BW].astype(F32), g_ref[...], b_ref[...])
    vb = vln.astype(BF16)
    for gi in range(B_GROUPS):
        w = ws_ref[gi].astype(BF16)
        cs_ = slice(gi * CHUNK, (gi + 1) * CHUNK)
        for n in range(pb.shape[0] // CHUNK):
            rs_ = slice(n * CHUNK, (n + 1) * CHUNK)
            mixed = _dot(w, vb[rs_, cs_], NN) + bs_ref[gi]
            z = pb[rs_, 2 * BW + gi * CHUNK:2 * BW + (gi + 1) * CHUNK].astype(F32)
            y_ref[rs_, cs_] = (pb[rs_, cs_].astype(F32) * mixed * (z * _sigmoid(z))).astype(BF16)


def _gmlp_bwd(pb, lng, lnb, ws, bs, dy):
    t = pb.shape[0]
    tb = _tile(t, 256, CHUNK)

    def body(pb_ref, g_ref, b_ref, ws_ref, bs_ref, dy_ref, dpb_ref, dws_ref, dbs_ref, dg_ref, db_ref, dvln_ref):
        @pl.when(pl.program_id(0) == 0)
        def _():
            dws_ref[...] = jnp.zeros_like(dws_ref)
            dbs_ref[...] = jnp.zeros_like(dbs_ref)
            dg_ref[...] = jnp.zeros_like(dg_ref)
            db_ref[...] = jnp.zeros_like(db_ref)

        vln, xh, rs = _layer_norm(pb_ref[:, BW:2 * BW].astype(F32), g_ref[...], b_ref[...])
        vb = vln.astype(BF16)
        for gi in range(B_GROUPS):
            w = ws_ref[gi].astype(BF16)
            cs_ = slice(gi * CHUNK, (gi + 1) * CHUNK)
            for n in range(tb // CHUNK):
                rs_ = slice(n * CHUNK, (n + 1) * CHUNK)
                vbc = vb[rs_, cs_]
                mixed = _dot(w, vbc, NN) + bs_ref[gi]
                zs = slice(2 * BW + gi * CHUNK, 2 * BW + (gi + 1) * CHUNK)
                z = pb_ref[rs_, zs].astype(F32)
                u = pb_ref[rs_, cs_].astype(F32)
                sil, dsil = _silu_and_grad(z)
                dyc = dy_ref[rs_, cs_]
                dmixed = dyc * u * sil
                dpb_ref[rs_, cs_] = (dyc * mixed * sil).astype(BF16)
                dpb_ref[rs_, zs] = (dyc * u * mixed * dsil).astype(BF16)
                dmb = dmixed.astype(BF16)
                dws_ref[gi] += _dot(dmb, vbc, NT)
                dbs_ref[gi] += jnp.sum(dmixed, axis=-1, keepdims=True)
                dvln_ref[rs_, cs_] = _dot(w, dmb, TN)
        dvln = dvln_ref[...]
        dg_ref[...] += jnp.sum(dvln * xh, axis=0, keepdims=True)
        db_ref[...] += jnp.sum(dvln, axis=0, keepdims=True)
        gd = dvln * g_ref[...]
        dv = rs * (gd - jnp.mean(gd, axis=-1, keepdims=True) - xh * jnp.mean(gd * xh, axis=-1, keepdims=True))
        dpb_ref[:, BW:2 * BW] = dv.astype(BF16)

    vec = pl.BlockSpec((1, BW), lambda i: (0, 0))
    wsb = pl.BlockSpec((B_GROUPS, CHUNK, CHUNK), lambda i: (0, 0, 0))
    bsb = pl.BlockSpec((B_GROUPS, CHUNK, 1), lambda i: (0, 0, 0))
    return pl.pallas_call(
        body, name="gmlp_bwd", grid=(t // tb,),
        in_specs=[pl.BlockSpec((tb, W_B), lambda i: (i, 0)), vec, vec, wsb, bsb, pl.BlockSpec((tb, BW), lambda i: (i, 0))],
        out_specs=(pl.BlockSpec((tb, W_B), lambda i: (i, 0)), wsb, bsb, vec, vec),
        out_shape=(SDS((t, W_B), BF16), SDS((B_GROUPS, CHUNK, CHUNK), F32), SDS((B_GROUPS, CHUNK, 1), F32),
                   SDS((1, BW), F32), SDS((1, BW), F32)),
        scratch_shapes=[pltpu.VMEM((tb, BW), F32)],
        compiler_params=_params("arbitrary"),
    )(pb, lng, lnb, ws, bs, dy)


def _mem_scores(pm_ref, kv_ref, h):
    hs = slice(h * M_HEAD_DIM, (h + 1) * M_HEAD_DIM)
    return _dot(pm_ref[:, hs].astype(BF16), kv_ref[:, hs].astype(BF16), NT) * (M_HEAD_DIM ** -0.5)


def _softmax_rows(s):
    e = jnp.exp(s - jnp.max(s, axis=-1, keepdims=True))
    return e / jnp.sum(e, axis=-1, keepdims=True)


def _mem_attn_tile(pm, kv_ref, y_ref):
    s_next = _mem_scores(pm, kv_ref, 0)
    for h in range(M_HEADS):
        hs = slice(h * M_HEAD_DIM, (h + 1) * M_HEAD_DIM)
        vh = kv_ref[:, BW + h * M_HEAD_DIM:BW + (h + 1) * M_HEAD_DIM].astype(BF16)
        s, s_next = s_next, (_mem_scores(pm, kv_ref, h + 1) if h + 1 < M_HEADS else None)
        p = _softmax_rows(s)
        o = _dot(p.astype(BF16), vh, NN)
        z = pm[:, BW + h * M_HEAD_DIM:BW + (h + 1) * M_HEAD_DIM].astype(F32)
        y_ref[:, hs] = (o * (z * _sigmoid(z))).astype(BF16)


def _mem_attn_bwd(pm, kv, dy):
    t = pm.shape[0]
    tq = _tile(t, 512, 8)
    ml = kv.shape[0]
    scale = M_HEAD_DIM ** -0.5

    def body(pm_ref, kv_ref, dy_ref, dpm_ref, dkv_ref):
        @pl.when(pl.program_id(0) == 0)
        def _():
            dkv_ref[...] = jnp.zeros_like(dkv_ref)

        s_next = _mem_scores(pm_ref, kv_ref, 0)
        for h in range(M_HEADS):
            hs = slice(h * M_HEAD_DIM, (h + 1) * M_HEAD_DIM)
            zs = slice(BW + h * M_HEAD_DIM, BW + (h + 1) * M_HEAD_DIM)
            kh = kv_ref[:, hs].astype(BF16)
            vh = kv_ref[:, zs].astype(BF16)
            qh = pm_ref[:, hs].astype(BF16)
            s, s_next = s_next, (_mem_scores(pm_ref, kv_ref, h + 1) if h + 1 < M_HEADS else None)
            p = _softmax_rows(s)
            pb = p.astype(BF16)
            o = _dot(pb, vh, NN)
            sil, dsil = _silu_and_grad(pm_ref[:, zs].astype(F32))
            dyh = dy_ref[:, hs]
            do = dyh * sil
            dpm_ref[:, zs] = (dyh * o * dsil).astype(BF16)
            delta = jnp.sum(do * o, axis=-1, keepdims=True)
            do_b = do.astype(BF16)
            dp = _dot(do_b, vh, NT)
            dr_b = (p * (dp - delta) * scale).astype(BF16)
            dpm_ref[:, hs] = _dot(dr_b, kh, NN).astype(BF16)
            dkv_ref[:, hs] += _dot(dr_b, qh, TN)
            dkv_ref[:, zs] += _dot(pb, do_b, TN)

    kvb = pl.BlockSpec((ml, 2 * BW), lambda i: (0, 0))
    return pl.pallas_call(
        body, name="mem_attn_bwd", grid=(t // tq,),
        in_specs=[pl.BlockSpec((tq, W_M), lambda i: (i, 0)), kvb, pl.BlockSpec((tq, BW), lambda i: (i, 0))],
        out_specs=(pl.BlockSpec((tq, W_M), lambda i: (i, 0)), kvb),
        out_shape=(SDS((t, W_M), BF16), SDS((ml, 2 * BW), F32)),
        compiler_params=_params("arbitrary"),
    )(pm, kv, dy)


def _merge_fwd(ya, yb, ym, pg, g_br, x, w_out, exchange=None):
    t = ya.shape[0]
    tm = _tile(t, 512, 8)
    shard = g_br.shape[-1]

    def body(ya_ref, yb_ref, ym_ref, pg_ref, g_ref, x_ref, wo_ref, m_ref, xn_ref, w_ref):
        @pl.when(pl.program_id(0) == 0)
        def _():
            for d in range(N_DEV):
                w_ref[:, :, d * shard:(d + 1) * shard] = g_ref[d]

        ups = [_dot(y_ref[...], w_ref[n], NN) for n, y_ref in enumerate((ya_ref, yb_ref, ym_ref))]
        acc = None
        for n, up in enumerate(ups):
            term = _sigmoid(pg_ref[:, n * D_MODEL:(n + 1) * D_MODEL].astype(F32)) * up
            acc = term if acc is None else acc + term
        merged = acc.astype(BF16)
        m_ref[...] = merged
        xn_ref[...] = x_ref[...] + _dot(merged, wo_ref[...], NN)

    yb_spec = pl.BlockSpec((tm, BW), lambda i: (i, 0))
    row = pl.BlockSpec((tm, D_MODEL), lambda i: (i, 0))
    return _host_call(
        body, (ya, yb, ym, pg, g_br, x, w_out),
        [yb_spec, yb_spec, yb_spec, pl.BlockSpec((tm, W_G), lambda i: (i, 0)),
         pl.BlockSpec(g_br.shape, lambda i: (0, 0, 0, 0)), row, pl.BlockSpec(w_out.shape, lambda i: (0, 0))],
        (row, row, pl.BlockSpec((3, BW, D_MODEL), lambda i: (0, 0, 0))),
        (SDS((t, D_MODEL), BF16), SDS((t, D_MODEL), F32), SDS((3, BW, D_MODEL), BF16)),
        name="merge_fwd", grid=(t // tm,), semantics=("arbitrary",), exchange=exchange)


def _merge_bwd(ya, yb, ym, pg, wbr, dx_out, w_out, exchange=None):
    t = ya.shape[0]
    tm = _tile(t, 512, 16)
    n_steps = t // tm
    shard = D_MODEL // N_DEV

    def body(ya_ref, yb_ref, ym_ref, pg_ref, w_ref, dx_ref, wo_ref, dya_ref, dyb_ref, dym_ref, dpg_ref, dwd_ref, dw_ref):
        @pl.when(pl.program_id(0) == 0)
        def _():
            dw_ref[...] = jnp.zeros_like(dw_ref)

        dmf = _dot(dx_ref[...].astype(BF16), wo_ref[...], NT)
        branches = ((ya_ref, dya_ref), (yb_ref, dyb_ref), (ym_ref, dym_ref))

        def gate(n):
            gt = _sigmoid(pg_ref[:, n * D_MODEL:(n + 1) * D_MODEL].astype(F32))
            return gt, (dmf * gt).astype(BF16)

        gt, dup = gate(0)
        for n, (y_ref, dy_ref) in enumerate(branches):
            nxt = gate(n + 1) if n + 1 < len(branches) else None
            y, w = y_ref[...], w_ref[n]
            dy_ref[...] = _dot(dup, w, NT)
            dw_ref[n] += _dot(y, dup, TN)
            up = _dot(y, w, NN)
            dpg_ref[:, n * D_MODEL:(n + 1) * D_MODEL] = (dmf * up * gt * (1.0 - gt)).astype(BF16)
            if nxt is not None:
                gt, dup = nxt

        @pl.when(pl.program_id(0) == n_steps - 1)
        def _():
            for d in range(N_DEV):
                dwd_ref[d] = dw_ref[:, :, d * shard:(d + 1) * shard].astype(BF16)

    y_spec = pl.BlockSpec((tm, BW), lambda i: (i, 0))
    w_spec = pl.BlockSpec((3, BW, D_MODEL), lambda i: (0, 0, 0))
    return _host_call(
        body, (ya, yb, ym, pg, wbr, dx_out, w_out),
        [y_spec, y_spec, y_spec, pl.BlockSpec((tm, W_G), lambda i: (i, 0)), w_spec,
         pl.BlockSpec((tm, D_MODEL), lambda i: (i, 0)), pl.BlockSpec(w_out.shape, lambda i: (0, 0))],
        (y_spec, y_spec, y_spec, pl.BlockSpec((tm, W_G), lambda i: (i, 0)),
         pl.BlockSpec((N_DEV, 3, BW, shard), lambda i: (0, 0, 0, 0))),
        (SDS((t, BW), F32), SDS((t, BW), F32), SDS((t, BW), F32), SDS((t, W_G), BF16), SDS((N_DEV, 3, BW, shard), BF16)),
        name="merge_bwd", grid=(n_steps,), semantics=("arbitrary",), scratch=[pltpu.VMEM((3, BW, D_MODEL), F32)],
        exchange=exchange)


def _loss_head(x, g, target):
    t, d = x.shape
    tm = _tile(t, 512, 8)

    def body(x_ref, g_ref, t_ref, loss_ref, dx_ref, dg_ref):
        @pl.when(pl.program_id(0) == 0)
        def _():
            loss_ref[...] = jnp.zeros_like(loss_ref)
            dg_ref[...] = jnp.zeros_like(dg_ref)

        xf = x_ref[...]
        r = lax.rsqrt(jnp.mean(xf * xf, axis=-1, keepdims=True) + EPS)
        xh = xf * r
        err = xh * g_ref[...] - t_ref[...]
        per_tok = jnp.mean(err * err, axis=-1, keepdims=True)
        loss_ref[...] += 0.5 * jnp.sum(per_tok, axis=0, keepdims=True)
        dy = err * (1.0 / d)
        gd = dy * g_ref[...]
        dx_ref[...] = r * (gd - xh * jnp.mean(gd * xh, axis=-1, keepdims=True))
        dg_ref[...] += jnp.sum(dy * xh, axis=0, keepdims=True)

    row = pl.BlockSpec((tm, d), lambda i: (i, 0))
    vec = pl.BlockSpec((1, d), lambda i: (0, 0))
    return pl.pallas_call(
        body, name="loss_head", grid=(t // tm,),
        in_specs=[row, vec, row],
        out_specs=(pl.BlockSpec((1, 1), lambda i: (0, 0)), row, vec),
        out_shape=(SDS((1, 1), F32), SDS((t, d), F32), SDS((1, d), F32)),
        compiler_params=_params("arbitrary"),
    )(x, g, target)


def _layer_fwd(x, mem, w, tabs, next_shards=None):
    cs, sn = tabs
    riding = next_shards is not None
    memn, kv = _mem_kv_fwd(mem, w["mem_g"], w["w_kv"])
    h, pa, pb, pm, pg, q, qT, k, kT, vT, yb, ym = _norm_in_proj(
        x, w["norm_g"], w["w_in_t"], cs, sn, w["qg2"], w["kg2"], w["ln_g"], w["ln_b"], w["w_s"], w["b_s"], kv)
    (o, lse, ya), gathered = _attn_fwd(qT, k, vT, pa, exchange=_gather_first_hop(next_shards) if riding else None)
    (merged, x_next, w_br), gathered = _merge_fwd(ya, yb, ym, pg, w["g_br"], x, w["w_out"],
                                                  exchange=_gather_forward(gathered) if riding else None)
    saved = dict(x=x, h=h, pa=pa, pb=pb, pm=pm, pg=pg, q=q, qT=qT, k=k, kT=kT, vT=vT, o=o, lse=lse, ya=ya, yb=yb, ym=ym,
                 memn=memn, kv=kv, merged=merged, w_br=w_br)
    return x_next, saved, gathered


def _layer_bwd(dx_out, mem, w, s, tabs, pending=None, core=None, scatter_own=False):
    cs, sn = tabs
    t = dx_out.shape[0]
    riding = pending is not None
    by_owner = lambda a: a.reshape((N_DEV // 2, 2, -1) + a.shape[-2:])
    row_shards = lambda a: by_owner(a.reshape(N_DEV, a.shape[0] // N_DEV, a.shape[1]))
    d_w_out = _mm_tn(s["merged"], dx_out, "d_w_out")
    (dya, dyb, dym, dpg, d_w_br), recv = _merge_bwd(s["ya"], s["yb"], s["ym"], s["pg"], s["w_br"], dx_out, w["w_out"],
                                                    exchange=_scatter_to_sibling(pending) if riding else None)
    parts = _pair_sums(BIG, pending, recv, core) if riding else None
    grads = dict(w_br=by_owner(d_w_br), w_out=row_shards(d_w_out))
    early = [grads[n] for n in BIG[2:]]
    (dq, dkT, dvT), landed = _attn_bwd(
        s["q"], s["qT"], s["k"], s["kT"], s["vT"], s["pa"], s["o"], s["lse"], dya,
        exchange=_both(_scatter_to_chips(parts) if riding else None, _scatter_to_sibling(early) if scatter_own else None))
    from_chips, early_recv = (landed[:len(BIG)], landed[len(BIG):]) if riding else (None, landed)
    early_parts = _pair_sums(BIG[2:], early, early_recv, core) if scatter_own else None
    (dpa, d_qg2, d_kg2), early_from_chips = _attn_prep_bwd(
        s["pa"], cs, sn, w["qg2"], w["kg2"], dq, dkT.reshape(-1, t), dvT.reshape(-1, t), dya, s["o"],
        exchange=_scatter_to_chips(early_parts) if scatter_own else None)
    dpb, d_w_s, d_b_s, d_ln_g, d_ln_b = _gmlp_bwd(s["pb"], w["ln_g"], w["ln_b"], w["w_s"], w["b_s"], dyb)
    dpm, dkv = _mem_attn_bwd(s["pm"], s["kv"], dym)
    d_w_kv, d_mem_g = _mem_kv_bwd(mem, w["mem_g"], s["memn"], w["w_kv"], dkv)
    dps = (dpa, dpb, dpm, dpg)
    grads.update(w_in=row_shards(_d_w_in_t(dps, s["h"])), w_mem_kv=row_shards(d_w_kv))
    own = None
    if scatter_own:
        late = [grads[n] for n in BIG[:2]]
        late_parts = _pair_sums(BIG[:2], late, _exchange_call(_scatter_to_sibling(late), "rs_sibling_swap"), core)
        own = _scatter_to_chips(late_parts)
    (dx_in, d_norm_g), late_from_chips = _dh_rmsnorm_bwd(dps, w["w_in_t"], s["x"], w["norm_g"], dx_out, exchange=own)
    if scatter_own:
        own_parts, own_from_chips = late_parts + early_parts, tuple(late_from_chips) + tuple(early_from_chips)
    grads.update(norm_g=d_norm_g[0], q_norm_g=d_qg2[0, :HEAD_DIM] + d_qg2[0, HEAD_DIM:],
                 k_norm_g=d_kg2[0, :HEAD_DIM] + d_kg2[0, HEAD_DIM:], sg_ln_g=d_ln_g[0], sg_ln_b=d_ln_b[0],
                 w_s=d_w_s, b_s=d_b_s[:, :, 0], mem_norm_g=d_mem_g[0])
    return (dx_in, grads, ((parts, from_chips) if riding else None),
            ((own_parts, own_from_chips) if scatter_own else None))


def _layer_weights(l, w_in_t, w_kv, g_br, w_out, small):
    tile2 = lambda g: jnp.tile(g.reshape(1, -1), (1, 2))
    return dict(
        w_in_t=w_in_t, w_kv=w_kv, g_br=g_br, w_out=w_out,
        norm_g=small["norm_g"][l].reshape(1, -1), qg2=tile2(small["q_norm_g"][l]), kg2=tile2(small["k_norm_g"][l]),
        ln_g=small["sg_ln_g"][l].reshape(1, -1), ln_b=small["sg_ln_b"][l].reshape(1, -1),
        w_s=small["w_s"][l], b_s=small["b_s"][l][:, :, None], mem_g=small["mem_norm_g"][l].reshape(1, -1))


def _position():
    x, y, c = lax.axis_index("x"), lax.axis_index("y"), lax.axis_index("c")
    return x, y, c, [(1 - x, y), (x, 1 - y), (1 - x, 1 - y)]


def _gather_first_hop(shards):
    n = len(shards)

    def make(x_refs, out_refs, sems):
        send_sems, recv_sems, local_sems = sems
        x, y, c, chips = _position()
        me = 4 * x + 2 * y + c
        peers = [(x, y, 1 - c)] + [(cx, cy, c) for cx, cy in chips]
        copies = [pltpu.make_async_copy(x_refs[t], out_refs[t].at[me], local_sems.at[t]) for t in range(n)]
        copies += [pltpu.make_async_remote_copy(
            src_ref=x_refs[t], dst_ref=out_refs[t].at[me], send_sem=send_sems.at[t, k], recv_sem=recv_sems.at[t, k],
            device_id=peer, device_id_type=MESH_ID) for t in range(n) for k, peer in enumerate(peers)]
        return copies

    return _Exchange(shards, [SDS((N_DEV,) + a.shape, a.dtype) for a in shards],
                     [pltpu.SemaphoreType.DMA((n, 4)), pltpu.SemaphoreType.DMA((n, 4)), pltpu.SemaphoreType.DMA((n,))], make)


def _gather_forward(gathered):
    n = len(gathered)

    def make(in_refs, out_refs, sems):
        send_sems, recv_sems = sems
        x, y, c, chips = _position()
        return [pltpu.make_async_remote_copy(
            src_ref=in_refs[t].at[4 * cx + 2 * cy + c], dst_ref=out_refs[t].at[4 * cx + 2 * cy + c],
            send_sem=send_sems.at[t, j], recv_sem=recv_sems.at[t, j], device_id=(x, y, 1 - c), device_id_type=MESH_ID)
            for t in range(n) for j, (cx, cy) in enumerate(chips)]

    return _Exchange(gathered, [SDS(a.shape, a.dtype) for a in gathered],
                     [pltpu.SemaphoreType.DMA((n, 3)), pltpu.SemaphoreType.DMA((n, 3))], make,
                     aliases={t: t for t in range(n)})


def _all_gather(shards, name):
    return _exchange_call(_gather_forward(_exchange_call(_gather_first_hop(shards), name + "_hop1")), name + "_hop2")


def _scatter_to_sibling(dests):
    n = len(dests)

    def make(d_refs, recv_refs, sems):
        send_sems, recv_sems = sems
        x, y, c, _ = _position()
        return [pltpu.make_async_remote_copy(
            src_ref=d_refs[t].at[:, 1 - c], dst_ref=recv_refs[t], send_sem=send_sems.at[t],
            recv_sem=recv_sems.at[t], device_id=(x, y, 1 - c), device_id_type=MESH_ID) for t in range(n)]

    return _Exchange(dests, [SDS(a.shape[:1] + a.shape[2:], a.dtype) for a in dests],
                     [pltpu.SemaphoreType.DMA((n,)), pltpu.SemaphoreType.DMA((n,))], make)


def _scatter_to_chips(parts):
    n = len(parts)

    def make(p_refs, recv_refs, sems):
        send_sems, recv_sems = sems
        _, _, c, chips = _position()
        return [pltpu.make_async_remote_copy(
            src_ref=p_refs[t].at[2 * cx + cy], dst_ref=recv_refs[t].at[k], send_sem=send_sems.at[t, k],
            recv_sem=recv_sems.at[t, k], device_id=(cx, cy, c), device_id_type=MESH_ID)
            for t in range(n) for k, (cx, cy) in enumerate(chips)]

    return _Exchange(parts, [SDS((3,) + a.shape[1:], a.dtype) for a in parts],
                     [pltpu.SemaphoreType.DMA((n, 3)), pltpu.SemaphoreType.DMA((n, 3))], make)


def _pair_sums(names, dests, recv, core):
    n = len(dests)

    def body(core_ref, *refs):
        for a_ref, b_ref, o_ref in zip(refs[:n], refs[n:2 * n], refs[2 * n:]):
            o_ref[...] = (a_ref[...].astype(F32) + b_ref[...].astype(F32)).astype(o_ref.dtype)

    mine = [pl.BlockSpec((None, None) + d.shape[2:], lambda j, core_ref: (j, core_ref[0], 0, 0, 0)) for d in dests]
    theirs = [pl.BlockSpec((None,) + r.shape[1:], lambda j, core_ref: (j, 0, 0, 0)) for r in recv]
    return list(pl.pallas_call(
        body, name="rs_pair_sum_" + "_".join(names), out_shape=tuple(SDS(r.shape, BF16) for r in recv),
        grid_spec=pltpu.PrefetchScalarGridSpec(
            num_scalar_prefetch=1, grid=(4,), in_specs=mine + theirs, out_specs=tuple(theirs)),
        compiler_params=_params("parallel"),
    )(core, *dests, *recv))


def _adamw_math(w, g, m, v):
    m = ADAM_B1 * m + (1.0 - ADAM_B1) * g
    v = ADAM_B2 * v + (1.0 - ADAM_B2) * (g * g)
    m_hat = m / (1.0 - ADAM_B1 ** ADAM_STEP)
    v_hat = v / (1.0 - ADAM_B2 ** ADAM_STEP)
    delta = -ADAM_LR * (m_hat / (jnp.sqrt(v_hat) + ADAM_EPS) + ADAM_WD * w)
    return delta, m, v


def _sum_and_adamw(parts, w, m, v, name):
    n, r, ln = parts.shape
    tr = _tile(r, 512, 16)

    def body(p_ref, w_ref, m_ref, v_ref, g_out, d_out, m_out, v_out):
        g = p_ref[0].astype(F32)
        for j in range(1, n):
            g = g + p_ref[j].astype(F32)
        delta, nm, nv = _adamw_math(w_ref[...], g, m_ref[...], v_ref[...])
        g_out[...] = g
        d_out[...] = delta
        m_out[...] = nm
        v_out[...] = nv

    blk = pl.BlockSpec((tr, ln), lambda i: (i, 0))
    return pl.pallas_call(
        body, name=name, grid=(r // tr,),
        in_specs=[pl.BlockSpec((n, tr, ln), lambda i: (0, i, 0)), blk, blk, blk],
        out_specs=(blk, blk, blk, blk), out_shape=tuple(SDS((r, ln), F32) for _ in range(4)),
        compiler_params=_params("parallel"),
    )(parts, w, m, v)


BIG = ("w_in", "w_mem_kv", "w_br", "w_out")
SMALL = ("norm_g", "q_norm_g", "k_norm_g", "sg_ln_g", "sg_ln_b", "w_s", "b_s", "mem_norm_g", "final_g")


def _pack(arrs, row_unit=16):
    flat = jnp.concatenate([a.reshape(-1) for a in arrs])
    pad = (-flat.shape[0]) % (row_unit * LANES)
    if pad:
        flat = jnp.concatenate([flat, jnp.zeros((pad,), flat.dtype)])
    return flat.reshape(-1, LANES)


def _unpack(buf, shapes):
    flat = buf.reshape(-1)
    out, off = [], 0
    for shp in shapes:
        n = 1
        for s_ in shp:
            n *= s_
        out.append(flat[off:off + n].reshape(shp))
        off += n
    return out


def _shard_sum_adamw(part, from_chips, chip, w, m, v, layer, prev, name):
    _, na, r, cdim = part.shape
    flat = w.ndim == 3
    assert not flat or na == 1
    tr = _tile(r, max(8, (256 * 1024) // (na * cdim)), 8)
    n_prev = 0 if prev is None else len(prev)

    def body(chip_ref, p_ref, f_ref, w_ref, m_ref, v_ref, *rest):
        g_out, d_out, m_out, v_out = rest[n_prev:]
        g = p_ref[...].astype(F32)
        for j in range(3):
            g = g + f_ref[j].astype(F32)
        delta, nm, nv = _adamw_math(w_ref[...], g, m_ref[...], v_ref[...])
        g_out[...] = g
        d_out[...] = delta
        m_out[...] = nm
        v_out[...] = nv

    a_blk = None if flat else na
    if flat:
        lay = pl.BlockSpec((None, tr, cdim), lambda i, chip_ref: (layer, i, 0))
    else:
        lay = pl.BlockSpec((None, na, tr, cdim), lambda i, chip_ref: (layer, 0, i, 0))
    return pl.pallas_call(
        body, name=name, out_shape=tuple(SDS(w.shape, F32) for _ in range(4)),
        grid_spec=pltpu.PrefetchScalarGridSpec(
            num_scalar_prefetch=1, grid=(r // tr,),
            in_specs=[pl.BlockSpec((None, a_blk, tr, cdim), lambda i, chip_ref: (chip_ref[0], 0, i, 0)),
                      pl.BlockSpec((3, a_blk, tr, cdim), lambda i, chip_ref: (0, 0, i, 0)), lay, lay, lay]
            + [HBM] * n_prev,
            out_specs=(lay, lay, lay, lay)),
        input_output_aliases={6 + j: j for j in range(n_prev)},
        compiler_params=_params("parallel"),
    )(chip, part, from_chips, w, m, v, *(prev or ()))


def kernel(x, mem, norm_g, w_in, q_norm_g, k_norm_g, sg_ln_g, sg_ln_b, w_s, b_s, mem_norm_g, w_mem_kv, w_br, w_out, final_g, loss_target, m_norm_g, m_w_in, m_q_norm_g, m_k_norm_g, m_sg_ln_g, m_sg_ln_b, m_w_s, m_b_s, m_mem_norm_g, m_w_mem_kv, m_w_br, m_w_out, m_final_g, v_norm_g, v_w_in, v_q_norm_g, v_k_norm_g, v_sg_ln_g, v_sg_ln_b, v_w_s, v_b_s, v_mem_norm_g, v_w_mem_kv, v_w_br, v_w_out, v_final_g):
    wts = dict(norm_g=norm_g, w_in=w_in, q_norm_g=q_norm_g, k_norm_g=k_norm_g, sg_ln_g=sg_ln_g, sg_ln_b=sg_ln_b,
               w_s=w_s, b_s=b_s, mem_norm_g=mem_norm_g, w_mem_kv=w_mem_kv, w_br=w_br, w_out=w_out, final_g=final_g)
    mom1 = dict(norm_g=m_norm_g, w_in=m_w_in, q_norm_g=m_q_norm_g, k_norm_g=m_k_norm_g, sg_ln_g=m_sg_ln_g,
                sg_ln_b=m_sg_ln_b, w_s=m_w_s, b_s=m_b_s, mem_norm_g=m_mem_norm_g, w_mem_kv=m_w_mem_kv, w_br=m_w_br,
                w_out=m_w_out, final_g=m_final_g)
    mom2 = dict(norm_g=v_norm_g, w_in=v_w_in, q_norm_g=v_q_norm_g, k_norm_g=v_k_norm_g, sg_ln_g=v_sg_ln_g,
                sg_ln_b=v_sg_ln_b, w_s=v_w_s, b_s=v_b_s, mem_norm_g=v_mem_norm_g, w_mem_kv=v_w_mem_kv, w_br=v_w_br,
                w_out=v_w_out, final_g=v_final_g)
    dp = w_in.shape[0]
    core = lax.axis_index("c").astype(jnp.int32).reshape(1)
    chip = (2 * lax.axis_index("x") + lax.axis_index("y")).astype(jnp.int32).reshape(1)

    t_in = lambda a: jnp.swapaxes(a, 1, 2)
    wts, mom1, mom2 = [dict(d_, w_in=t_in(d_["w_in"])) for d_ in (wts, mom1, mom2)]

    shard_bf = {n: wts[n].astype(BF16) for n in BIG}
    shards = lambda l: [shard_bf[n][l] for n in BIG]
    x_l, mem_l = x[0], mem[0]
    tabs = _rope_tables(x_l.shape[0])

    gathered = _all_gather(shards(0), "weights_all_gather")
    layers, saved = [], []
    for l in range(dp):
        g_in, g_kv, g_br, g_out = gathered
        layers.append(_layer_weights(l, g_in.reshape(IN_WIDTH, -1), g_kv.reshape(D_MODEL, -1), g_br,
                                     g_out.reshape(D_MODEL, -1), wts))
        x_l, s, gathered = _layer_fwd(x_l, mem_l, layers[l], tabs, next_shards=shards(l + 1) if l + 1 < dp else None)
        saved.append(s)
    loss_local, dx, d_final_g = _loss_head(x_l, final_g.reshape(1, -1), loss_target[0])
    loss = lax.psum(loss_local[0, 0], AXES)

    def finish(l, parts, from_chips, prev):
        return {n: _shard_sum_adamw(p, f, chip, wts[n], mom1[n], mom2[n], l, None if prev is None else prev[n],
                                    "sum_adamw_" + n)
                for n, p, f in zip(BIG, parts, from_chips)}

    grads, updated, pending = [None] * dp, None, None
    for l in reversed(range(dp)):
        dx, grads[l], scattered, own = _layer_bwd(dx, mem_l, layers[l], saved[l], tabs, pending=pending, core=core,
                                                  scatter_own=(l == 0))
        if scattered is not None:
            updated = finish(l + 1, *scattered, updated)
        pending = [grads[l][n] for n in BIG]
    updated = finish(0, *own, updated)
    grad_x = dx
    big_out = [{n: (t_in(updated[n][k]) if n == "w_in" else updated[n][k]) for n in BIG} for k in range(4)]

    small_g = {n: jnp.stack([g[n] for g in grads]) for n in SMALL if n != "final_g"}
    small_g["final_g"] = d_final_g
    (all_small,) = _all_gather([_pack([small_g[n] for n in SMALL])], "small_all_gather")
    small_bufs = _sum_and_adamw(
        all_small, _pack([wts[n] for n in SMALL]), _pack([mom1[n] for n in SMALL]), _pack([mom2[n] for n in SMALL]),
        "small_sum_adamw")

    outs = []
    for big_vals, small_buf in zip(big_out, small_bufs):
        vals = dict(big_vals)
        vals.update(zip(SMALL, _unpack(small_buf, [wts[n].shape for n in SMALL])))
        outs.append(vals)
    order = ("norm_g", "w_in", "q_norm_g", "k_norm_g", "sg_ln_g", "sg_ln_b", "w_s", "b_s", "mem_norm_g", "w_mem_kv",
             "w_br", "w_out", "final_g")
    result = [loss, grad_x[None]]
    for vals in outs:
        result += [vals[n] for n in order]
    return tuple(result)
```

```python
import functools

import jax
import jax.numpy as jnp
from jax import lax
from jax.experimental import pallas as pl
from jax.experimental.pallas import tpu as pltpu

F32 = jnp.float32
BF16 = jnp.bfloat16
SDS = jax.ShapeDtypeStruct
MESH_ID = pl.DeviceIdType.MESH
AXES = ("x", "y", "c")
N_DEV = 8

D_MODEL = 1024
GRID_W = 64
CHUNK = 128
ROPE_THETA = 10000.0
EPS = 1e-6
HEAD_DIM = 64
A_HEADS = 8
A_KV_HEADS = 2
B_GROUPS = 4
M_HEADS = 4
M_HEAD_DIM = 128
BW = 512
W_A = 1280
W_B = 1536
W_M = 1024
W_G = 3072
IN_WIDTH = W_A + W_B + W_M + W_G
GROUP_OFFS = (0, W_A, W_A + W_B, W_A + W_B + W_M, IN_WIDTH)

ADAM_LR = 0.001
ADAM_B1 = 0.9
ADAM_B2 = 0.999
ADAM_EPS = 1e-08
ADAM_WD = 0.01
ADAM_STEP = 10

LANES = 128
KEY_CHUNK = 2048
VMEM_LIMIT = 52 * 1024 * 1024


def _tile(n, cap, unit=LANES):
    if n <= cap:
        return n
    t = (cap // unit) * unit
    while n % t:
        t -= unit
    return t


def _params(*sem):
    return pltpu.CompilerParams(dimension_semantics=sem, vmem_limit_bytes=VMEM_LIMIT)


def _sigmoid(z):
    return 0.5 * jnp.tanh(0.5 * z) + 0.5


def _silu_and_grad(z):
    s = _sigmoid(z)
    return z * s, s * (1.0 + z * (1.0 - s))


def _dot(a, b, dims):
    return lax.dot_general(a, b, (dims, ((), ())), preferred_element_type=F32)


NN = ((1,), (0,))
NT = ((1,), (1,))
TN = ((0,), (0,))
HBM = pl.BlockSpec(memory_space=pl.ANY)


class _Exchange:
    def __init__(self, ins, out_shapes, sems, make, aliases=None):
        self.ins, self.out_shapes, self.sems, self.make = list(ins), list(out_shapes), list(sems), make
        self.aliases = dict(aliases or {})

    def start(self, in_refs, out_refs, sems):
        for cp in self.make(in_refs, out_refs, sems):
            cp.start()

    def finish(self, in_refs, out_refs, sems):
        for cp in self.make(in_refs, out_refs, sems):
            cp.wait()


def _both(a, b):
    if a is None or b is None:
        return a if b is None else b
    n_in, n_out, n_sem = len(a.ins), len(a.out_shapes), len(a.sems)

    def make(in_refs, out_refs, sems):
        return (a.make(in_refs[:n_in], out_refs[:n_out], sems[:n_sem])
                + b.make(in_refs[n_in:], out_refs[n_out:], sems[n_sem:]))

    aliases = dict(a.aliases)
    aliases.update({n_in + i: n_out + o for i, o in b.aliases.items()})
    return _Exchange(a.ins + b.ins, a.out_shapes + b.out_shapes, a.sems + b.sems, make, aliases)


def _host_call(body, ins, in_specs, out_specs, out_shape, *, name, grid, semantics, scratch=(), exchange=None):
    ins, out_shape, scratch = list(ins), list(out_shape), list(scratch)
    if exchange is None:
        res = pl.pallas_call(
            body, name=name, grid=grid, in_specs=list(in_specs), out_specs=tuple(out_specs), out_shape=tuple(out_shape),
            scratch_shapes=scratch, compiler_params=_params(*semantics))(*ins)
        return tuple(res), ()
    n_in, n_out, n_scr = len(ins), len(out_shape), len(scratch)
    x_in, x_out = len(exchange.ins), len(exchange.out_shapes)

    def carrying(*refs):
        o0 = n_in + x_in
        s0 = o0 + n_out + x_out
        c_in, c_out, c_sems = refs[n_in:o0], refs[o0 + n_out:s0], refs[s0 + n_scr:]
        ids = [pl.program_id(a) for a in range(len(grid))]
        first = functools.reduce(jnp.logical_and, [i == 0 for i in ids])
        last = functools.reduce(jnp.logical_and, [i == g - 1 for i, g in zip(ids, grid)])

        @pl.when(first)
        def _():
            exchange.start(c_in, c_out, c_sems)

        body(*refs[:n_in], *refs[o0:o0 + n_out], *refs[s0:s0 + n_scr])

        @pl.when(last)
        def _():
            exchange.finish(c_in, c_out, c_sems)

    res = pl.pallas_call(
        carrying, name=name, grid=grid, in_specs=list(in_specs) + [HBM] * x_in,
        out_specs=tuple(out_specs) + (HBM,) * x_out, out_shape=tuple(out_shape) + tuple(exchange.out_shapes),
        scratch_shapes=scratch + exchange.sems,
        input_output_aliases={n_in + i: n_out + o for i, o in exchange.aliases.items()},
        compiler_params=_params(*(["arbitrary"] * len(grid))))(*ins, *exchange.ins)
    return tuple(res[:n_out]), tuple(res[n_out:])


def _exchange_call(exchange, name):
    x_in = len(exchange.ins)

    def body(*refs):
        x_out = len(exchange.out_shapes)
        c_in, c_out, c_sems = refs[:x_in], refs[x_in:x_in + x_out], refs[x_in + x_out:]
        exchange.start(c_in, c_out, c_sems)
        exchange.finish(c_in, c_out, c_sems)

    return pl.pallas_call(
        body, name=name, in_specs=[HBM] * x_in, out_specs=tuple([HBM] * len(exchange.out_shapes)),
        out_shape=tuple(exchange.out_shapes), scratch_shapes=exchange.sems, input_output_aliases=exchange.aliases,
    )(*exchange.ins)


def _mem_kv_fwd(mem, g, w_kv):
    def body(mem_ref, g_ref, w_ref, memn_ref, kv_ref):
        xf = mem_ref[...]
        r = lax.rsqrt(jnp.mean(xf * xf, axis=-1, keepdims=True) + EPS)
        memn = (xf * r * g_ref[...]).astype(BF16)
        memn_ref[...] = memn
        kv_ref[...] = _dot(memn, w_ref[...], NN)

    return pl.pallas_call(
        body, name="mem_kv_fwd", out_shape=(SDS(mem.shape, BF16), SDS((mem.shape[0], w_kv.shape[1]), F32)),
        compiler_params=pltpu.CompilerParams(vmem_limit_bytes=VMEM_LIMIT))(mem, g, w_kv)


def _mem_kv_bwd(mem, g, memn, w_kv, dkv):
    def body(mem_ref, g_ref, memn_ref, w_ref, dkv_ref, dw_ref, dg_ref):
        dkv_b = dkv_ref[...].astype(BF16)
        dw_ref[...] = _dot(memn_ref[...], dkv_b, TN).astype(BF16)
        dmemn = _dot(dkv_b, w_ref[...], NT)
        dg_ref[...] = _rmsnorm_bwd_math(mem_ref[...], g_ref[...], dmemn)[1]

    return pl.pallas_call(
        body, name="mem_kv_bwd", out_shape=(SDS(w_kv.shape, BF16), SDS(g.shape, F32)),
        compiler_params=pltpu.CompilerParams(vmem_limit_bytes=VMEM_LIMIT))(mem, g, memn, w_kv, dkv)


def _mm_tn(a, b, name):
    (k, m), (k2, n) = a.shape, b.shape
    assert k == k2, (a.shape, b.shape)
    tk = _tile(k, 2048, 16)
    nk = k // tk

    def body(a_ref, b_ref, o_ref, acc):
        kk = pl.program_id(0)
        part = _dot(a_ref[...].astype(BF16), b_ref[...].astype(BF16), TN)

        @pl.when(kk == 0)
        def _():
            acc[...] = part

        @pl.when(kk > 0)
        def _():
            acc[...] += part

        @pl.when(kk == nk - 1)
        def _():
            o_ref[...] = acc[...].astype(BF16)

    return pl.pallas_call(
        body, name=name, grid=(nk,),
        in_specs=[pl.BlockSpec((tk, m), lambda kk: (kk, 0)), pl.BlockSpec((tk, n), lambda kk: (kk, 0))],
        out_specs=pl.BlockSpec((m, n), lambda kk: (0, 0)), out_shape=SDS((m, n), BF16),
        scratch_shapes=[pltpu.VMEM((m, n), F32)], compiler_params=_params("arbitrary"),
    )(a, b)


def _rmsnorm_bwd_math(xf, g, dh):
    r = lax.rsqrt(jnp.mean(xf * xf, axis=-1, keepdims=True) + EPS)
    xh = xf * r
    gd = dh * g
    dx = r * (gd - xh * jnp.mean(gd * xh, axis=-1, keepdims=True))
    return dx, jnp.sum(dh * xh, axis=0, keepdims=True)


def _group_rows(p):
    return slice(GROUP_OFFS[p], GROUP_OFFS[p + 1])


def _d_w_in_t(dps, h):
    t, d = h.shape
    tm = 256
    assert all(off % tm == 0 for off in GROUP_OFFS)
    tiles = [a.shape[1] // tm for a in dps]
    starts = [sum(tiles[:p]) for p in range(len(dps))]

    def body(*refs):
        a_refs, h_ref, o_ref = refs[:len(dps)], refs[len(dps)], refs[len(dps) + 1]
        i = pl.program_id(0)
        for p, a_ref in enumerate(a_refs):
            @pl.when((i >= starts[p]) & (i < starts[p] + tiles[p]))
            def _():
                o_ref[...] = _dot(a_ref[...], h_ref[...], TN).astype(BF16)

    def tile_of(p):
        return lambda i: jnp.clip(i - starts[p], 0, tiles[p] - 1)

    return pl.pallas_call(
        body, name="d_w_in", grid=(sum(tiles),),
        in_specs=[pl.BlockSpec((t, tm), lambda i, p=p: (0, tile_of(p)(i))) for p in range(len(dps))]
        + [pl.BlockSpec((t, d), lambda i: (0, 0), pipeline_mode=pl.Buffered(1))],
        out_specs=pl.BlockSpec((tm, d), lambda i: (i, 0)),
        out_shape=SDS((sum(tiles) * tm, d), BF16), compiler_params=_params("arbitrary"),
    )(*dps, h)


def _norm_in_proj(x, g, w_t, cs, sn, qg2, kg2, lng, lnb, ws, bs, kv):
    t, d = x.shape
    tm = _tile(t, 512, LANES)
    widths = [GROUP_OFFS[p + 1] - GROUP_OFFS[p] for p in range(4)]
    scale = HEAD_DIM ** -0.5

    def body(x_ref, g_ref, w_ref, cs_ref, sn_ref, qg_ref, kg_ref, lng_ref, lnb_ref, ws_ref, bs_ref, kv_ref,
             h_ref, pa_ref, pb_ref, pm_ref, pg_ref, q_ref, qT_ref, k_ref, kT_ref, vT_ref, yb_ref, ym_ref):
        xf = x_ref[...]
        r = lax.rsqrt(jnp.mean(xf * xf, axis=-1, keepdims=True) + EPS)
        hb = (xf * r * g_ref[...]).astype(BF16)
        h_ref[...] = hb
        groups = [_dot(hb, w_ref[_group_rows(p), :], NT).astype(BF16) for p in range(4)]
        for o_ref, val in zip((pa_ref, pb_ref, pm_ref, pg_ref), groups):
            o_ref[...] = val
        pa = groups[0]
        lane = lax.broadcasted_iota(jnp.int32, (tm, LANES), 1)
        c, s = cs_ref[...], sn_ref[...]

        def norm_rope(xg, g2):
            rr = lax.rsqrt(_head_sums(xg * xg, lane) * (1.0 / HEAD_DIM) + EPS)
            xn = xg * rr * g2
            return xn * c + _swap16(xn, lane) * s

        for gi in range(4):
            sl = slice(gi * LANES, (gi + 1) * LANES)
            qr = norm_rope(pa[:, sl].astype(F32), qg_ref[...]) * scale
            q_ref[:, sl] = qr.astype(BF16)
            qT_ref[sl, :] = qr.T.astype(BF16)
        kr = norm_rope(pa[:, 512:640].astype(F32), kg_ref[...])
        kT_ref[...] = kr.T.astype(BF16)
        vT_ref[...] = pa[:, 640:768].astype(F32).T.astype(BF16)
        kr = kr.astype(BF16)
        for kvh in range(A_KV_HEADS):
            k_ref[kvh] = kr[:, kvh * HEAD_DIM:(kvh + 1) * HEAD_DIM]
        _gmlp_tile(groups[1], lng_ref, lnb_ref, ws_ref, bs_ref, yb_ref)
        _mem_attn_tile(groups[2], kv_ref, ym_ref)

    row = lambda wd: pl.BlockSpec((tm, wd), lambda i: (i, 0))
    col = lambda rws: pl.BlockSpec((rws, tm), lambda i: (0, i))
    whole = lambda a: pl.BlockSpec(a.shape, lambda i: (0,) * a.ndim)
    hm = pl.BlockSpec((A_KV_HEADS, tm, HEAD_DIM), lambda i: (0, i, 0))
    return pl.pallas_call(
        body, name="norm_in_proj", grid=(t // tm,),
        in_specs=[row(d), whole(g), pl.BlockSpec(w_t.shape, lambda i: (0, 0), pipeline_mode=pl.Buffered(1)),
                  row(LANES), row(LANES)] + [whole(a) for a in (qg2, kg2, lng, lnb, ws, bs, kv)],
        out_specs=(row(d),) + tuple(row(wd) for wd in widths)
        + (row(BW), col(BW), hm, col(LANES), col(LANES), row(BW), row(BW)),
        out_shape=(SDS((t, d), BF16),) + tuple(SDS((t, wd), BF16) for wd in widths)
        + (SDS((t, BW), BF16), SDS((BW, t), BF16), SDS((A_KV_HEADS, t, HEAD_DIM), BF16),
           SDS((LANES, t), BF16), SDS((LANES, t), BF16), SDS((t, BW), BF16), SDS((t, BW), BF16)),
        compiler_params=_params("parallel"),
    )(x, g, w_t, cs, sn, qg2, kg2, lng, lnb, ws, bs, kv)


def _dh_rmsnorm_bwd(dps, w_t, x, g, dres, exchange=None):
    t, d = x.shape
    tm = _tile(t, 256, 16)
    n = len(dps)

    def body(*refs):
        dp_refs, w_ref = refs[:n], refs[n]
        x_ref, g_ref, dres_ref, dx_ref, dg_ref = refs[n + 1:]
        dh = None
        for p, dp_ref in enumerate(dp_refs):
            part = _dot(dp_ref[...], w_ref[_group_rows(p), :], NN)
            dh = part if dh is None else dh + part
        dx, dg = _rmsnorm_bwd_math(x_ref[...], g_ref[...], dh)
        dx_ref[...] = dx + dres_ref[...]

        @pl.when(pl.program_id(0) == 0)
        def _():
            dg_ref[...] = jnp.zeros_like(dg_ref)

        dg_ref[...] += dg

    row = pl.BlockSpec((tm, d), lambda i: (i, 0))
    vec = pl.BlockSpec((1, d), lambda i: (0, 0))
    return _host_call(
        body, list(dps) + [w_t, x, g, dres],
        [pl.BlockSpec((tm, a.shape[1]), lambda i: (i, 0)) for a in dps]
        + [pl.BlockSpec(w_t.shape, lambda i: (0, 0), pipeline_mode=pl.Buffered(1)), row, vec, row],
        (row, vec), (SDS((t, d), F32), SDS((1, d), F32)),
        name="dh_rmsnorm_bwd", grid=(t // tm,), semantics=("arbitrary",), exchange=exchange)


def _rope_tables(t):
    rows = t // GRID_W
    row = jnp.repeat(jnp.arange(rows, dtype=F32), GRID_W)
    col = jnp.tile(jnp.arange(GRID_W, dtype=F32), rows)
    n_freq = HEAD_DIM // 4
    inv = ROPE_THETA ** (-jnp.arange(n_freq, dtype=F32) / n_freq)
    ang = jnp.stack([row[:, None] * inv, col[:, None] * inv], axis=1)
    cos, sin = jnp.cos(ang), jnp.sin(ang)
    c64 = jnp.concatenate([cos[:, 0], cos[:, 0], cos[:, 1], cos[:, 1]], axis=-1)
    s64 = jnp.concatenate([-sin[:, 0], sin[:, 0], -sin[:, 1], sin[:, 1]], axis=-1)
    return jnp.tile(c64, (1, 2)), jnp.tile(s64, (1, 2))


def _head_sums(v, lane):
    lo = jnp.sum(jnp.where(lane < HEAD_DIM, v, 0.0), axis=-1, keepdims=True)
    hi = jnp.sum(jnp.where(lane < HEAD_DIM, 0.0, v), axis=-1, keepdims=True)
    return jnp.where(lane < HEAD_DIM, lo, hi)


def _swap16(v, lane):
    return jnp.where((lane % 32) < 16, pltpu.roll(v, LANES - 16, 1), pltpu.roll(v, 16, 1))


def _attn_fwd(qT, k, vT, pa, exchange=None):
    t = qT.shape[1]
    tq = _tile(t, 256, LANES)
    grp = A_HEADS // A_KV_HEADS

    ck = _tile(t, KEY_CHUNK, LANES)

    def body(qT_ref, k_ref, vT_ref, pa_ref, o_ref, lse_ref, y_ref):
        def finish_pair(pair, o_pair):
            o2 = jnp.concatenate(o_pair, axis=0).T
            cols = slice(pair * LANES, (pair + 1) * LANES)
            o_ref[:, cols] = o2
            z = pa_ref[:, 768 + pair * LANES:768 + (pair + 1) * LANES].astype(F32)
            y_ref[:, cols] = (o2 * (z * _sigmoid(z))).astype(BF16)

        n_c = t // ck
        items = [(h, c) for h in range(A_HEADS) for c in range(n_c)]

        def scores(h, c):
            return _dot(k_ref[h // grp, c * ck:(c + 1) * ck, :], qT_ref[h * HEAD_DIM:(h + 1) * HEAD_DIM, :], NN)

        def weighted_values(h, c, pT):
            kvh = h // grp
            return _dot(vT_ref[kvh * HEAD_DIM:(kvh + 1) * HEAD_DIM, c * ck:(c + 1) * ck], pT, NN)

        o_pair, lse_rows, sT, pending, acc = [], [], scores(*items[0]), None, None
        for i in range(len(items) + 1):
            item = items[i] if i < len(items) else None
            sT_next = scores(*items[i + 1]) if i + 1 < len(items) else None
            if pending is not None:
                (ph, pc, pT, alpha, l_run) = pending
                o_c = weighted_values(ph, pc, pT)
                acc = o_c if alpha is None else acc * alpha + o_c
                if pc == n_c - 1:
                    o_pair.append(acc / l_run)
                    if ph % 2:
                        finish_pair(ph // 2, o_pair)
                        o_pair = []
                pending = None
            if item is not None:
                h, c = item
                m_c = jnp.max(sT, axis=0, keepdims=True)
                if c == 0:
                    m_new, alpha = m_c, None
                else:
                    m_new = jnp.maximum(m, m_c)
                    alpha = jnp.exp(m - m_new)
                pT = jnp.exp(sT - m_new)
                l_c = jnp.sum(pT, axis=0, keepdims=True)
                l = l_c if c == 0 else l * alpha + l_c
                m = m_new
                if c == n_c - 1:
                    lse_rows.append(m + jnp.log(l))
                pending = (h, c, pT.astype(BF16), alpha, l)
            sT = sT_next
        lse_cols = jnp.concatenate(lse_rows, axis=0).T
        for kvh in range(A_KV_HEADS):
            lse_ref[kvh] = lse_cols[:, kvh * grp:(kvh + 1) * grp]

    row = lambda w: pl.BlockSpec((tq, w), lambda i: (i, 0))
    return _host_call(
        body, (qT, k, vT, pa),
        [pl.BlockSpec((BW, tq), lambda i: (0, i)), pl.BlockSpec((A_KV_HEADS, t, HEAD_DIM), lambda i: (0, 0, 0)),
         pl.BlockSpec((A_KV_HEADS * HEAD_DIM, t), lambda i: (0, 0)), row(W_A)],
        (row(BW), pl.BlockSpec((A_KV_HEADS, tq, grp), lambda i: (0, i, 0)), row(BW)),
        (SDS((t, BW), F32), SDS((A_KV_HEADS, t, grp), F32), SDS((t, BW), BF16)),
        name="attn_fwd", grid=(t // tq,), semantics=("parallel",), exchange=exchange)


def _attn_bwd(q, qT, k, kT, vT, pa, o, lse, dy, exchange=None):
    t = q.shape[0]
    tq = _tile(t, 256, LANES)
    grp = A_HEADS // A_KV_HEADS
    gw = grp * HEAD_DIM

    def body(q_ref, qT_ref, k_ref, kT_ref, vT_ref, z_ref, o_ref, lse_ref, dy_ref, dq_ref, dkT_ref, dvT_ref):
        @pl.when(pl.program_id(1) == 0)
        def _():
            dkT_ref[...] = jnp.zeros_like(dkT_ref)
            dvT_ref[...] = jnp.zeros_like(dvT_ref)

        z = z_ref[...].astype(F32)
        do = dy_ref[...] * (z * _sigmoid(z))
        doo = do * o_ref[...]
        doT = do.T
        kk, kT, vT = k_ref[...], kT_ref[...], vT_ref[...]
        heads = [slice(j * HEAD_DIM, (j + 1) * HEAD_DIM) for j in range(grp)]

        def scores(hs):
            return _dot(q_ref[:, hs], kT, NN), _dot(do[:, hs].astype(BF16), vT, NN)

        s, dp = scores(heads[0])
        for j, hs in enumerate(heads):
            nxt = scores(heads[j + 1]) if j + 1 < grp else None
            delta = jnp.sum(doo[:, hs], axis=-1, keepdims=True)
            p = jnp.exp(s - lse_ref[:, j:j + 1])
            ds_b = (p * (dp - delta)).astype(BF16)
            dq_ref[:, hs] = _dot(ds_b, kk, NN)
            dkT_ref[...] += _dot(qT_ref[hs, :], ds_b, NN)
            dvT_ref[...] += _dot(doT[hs, :].astype(BF16), p.astype(BF16), NN)
            if nxt is not None:
                s, dp = nxt

    grp_blk = pl.BlockSpec((tq, gw), lambda g, i: (i, g))
    kvT_blk = pl.BlockSpec((HEAD_DIM, t), lambda g, i: (g, 0))
    acc_blk = pl.BlockSpec((None, HEAD_DIM, t), lambda g, i: (g, 0, 0))
    return _host_call(
        body, (q, qT, k, kT, vT, pa, o, lse, dy),
        [grp_blk, pl.BlockSpec((gw, tq), lambda g, i: (g, i)), pl.BlockSpec((None, t, HEAD_DIM), lambda g, i: (g, 0, 0)),
         kvT_blk, kvT_blk, pl.BlockSpec((tq, gw), lambda g, i: (i, 768 // gw + g)), grp_blk,
         pl.BlockSpec((None, tq, grp), lambda g, i: (g, i, 0)), grp_blk],
        (grp_blk, acc_blk, acc_blk),
        (SDS((t, BW), F32), SDS((A_KV_HEADS, HEAD_DIM, t), F32), SDS((A_KV_HEADS, HEAD_DIM, t), F32)),
        name="attn_bwd", grid=(A_KV_HEADS, t // tq), semantics=("arbitrary", "arbitrary"), exchange=exchange)


def _attn_prep_bwd(pa, cs, sn, qg2, kg2, dq, dkT, dvT, dy, o, exchange=None):
    t = pa.shape[0]
    tq = _tile(t, 512, LANES)
    scale = HEAD_DIM ** -0.5

    def body(pa_ref, cs_ref, sn_ref, qg_ref, kg_ref, dq_ref, dkT_ref, dvT_ref, dy_ref, o_ref, dpa_ref, dqg_ref, dkg_ref):
        lane = lax.broadcasted_iota(jnp.int32, (tq, LANES), 1)
        c, s = cs_ref[...], sn_ref[...]

        @pl.when(pl.program_id(0) == 0)
        def _():
            dqg_ref[...] = jnp.zeros_like(dqg_ref)
            dkg_ref[...] = jnp.zeros_like(dkg_ref)

        def norm_rope_bwd(xg, g2, dout):
            r = lax.rsqrt(_head_sums(xg * xg, lane) * (1.0 / HEAD_DIM) + EPS)
            xh = xg * r
            dxn = dout * c + _swap16(dout * s, lane)
            gd = dxn * g2
            dx = r * (gd - xh * (_head_sums(gd * xh, lane) * (1.0 / HEAD_DIM)))
            return dx, jnp.sum(dxn * xh, axis=0, keepdims=True)

        for gi in range(4):
            sl = slice(gi * LANES, (gi + 1) * LANES)
            dx, dg = norm_rope_bwd(pa_ref[:, sl].astype(F32), qg_ref[...], dq_ref[:, sl] * scale)
            dpa_ref[:, sl] = dx.astype(BF16)
            dqg_ref[...] += dg
        dx, dg = norm_rope_bwd(pa_ref[:, 512:640].astype(F32), kg_ref[...], dkT_ref[...].T)
        dpa_ref[:, 512:640] = dx.astype(BF16)
        dkg_ref[...] += dg
        dpa_ref[:, 640:768] = dvT_ref[...].T.astype(BF16)
        z = pa_ref[:, 768:1280].astype(F32)
        _, dsilu = _silu_and_grad(z)
        dpa_ref[:, 768:1280] = (dy_ref[...] * o_ref[...] * dsilu).astype(BF16)

    row = lambda w: pl.BlockSpec((tq, w), lambda i: (i, 0))
    col = pl.BlockSpec((LANES, tq), lambda i: (0, i))
    vec = pl.BlockSpec((1, LANES), lambda i: (0, 0))
    return _host_call(
        body, (pa, cs, sn, qg2, kg2, dq, dkT, dvT, dy, o),
        [row(W_A), row(LANES), row(LANES), vec, vec, row(BW), col, col, row(BW), row(BW)],
        (row(W_A), vec, vec), (SDS((t, W_A), BF16), SDS((1, LANES), F32), SDS((1, LANES), F32)),
        name="attn_prep_bwd", grid=(t // tq,), semantics=("arbitrary",), exchange=exchange)


def _layer_norm(v, g, b):
    mu = jnp.mean(v, axis=-1, keepdims=True)
    xc = v - mu
    rs = lax.rsqrt(jnp.mean(xc * xc, axis=-1, keepdims=True) + EPS)
    xh = xc * rs
    return xh * g + b, xh, rs


def _gmlp_tile(pb, g_ref, b_ref, ws_ref, bs_ref, y_ref):
    vln, _, _ = _layer_norm(pb[:, BW:2 * BW].astype(F32), g_ref[...], b_ref[...])
    vb = vln.astype(BF16)
    for gi in range(B_GROUPS):
        w = ws_ref[gi].astype(BF16)
        cs_ = slice(gi * CHUNK, (gi + 1) * CHUNK)
        for n in range(pb.shape[0] // CHUNK):
            rs_ = slice(n * CHUNK, (n + 1) * CHUNK)
            mixed = _dot(w, vb[rs_, cs_], NN) + bs_ref[gi]
            z = pb[rs_, 2 * BW + gi * CHUNK:2 * BW + (gi + 1) * CHUNK].astype(F32)
            y_ref[rs_, cs_] = (pb[rs_, cs_].astype(F32) * mixed * (z * _sigmoid(z))).astype(BF16)


def _gmlp_bwd(pb, lng, lnb, ws, bs, dy):
    t = pb.shape[0]
    tb = _tile(t, 256, CHUNK)

    def body(pb_ref, g_ref, b_ref, ws_ref, bs_ref, dy_ref, dpb_ref, dws_ref, dbs_ref, dg_ref, db_ref, dvln_ref):
        @pl.when(pl.program_id(0) == 0)
        def _():
            dws_ref[...] = jnp.zeros_like(dws_ref)
            dbs_ref[...] = jnp.zeros_like(dbs_ref)
            dg_ref[...] = jnp.zeros_like(dg_ref)
            db_ref[...] = jnp.zeros_like(db_ref)

        vln, xh, rs = _layer_norm(pb_ref[:, BW:2 * BW].astype(F32), g_ref[...], b_ref[...])
        vb = vln.astype(BF16)
        for gi in range(B_GROUPS):
            w = ws_ref[gi].astype(BF16)
            cs_ = slice(gi * CHUNK, (gi + 1) * CHUNK)
            for n in range(tb // CHUNK):
                rs_ = slice(n * CHUNK, (n + 1) * CHUNK)
                vbc = vb[rs_, cs_]
                mixed = _dot(w, vbc, NN) + bs_ref[gi]
                zs = slice(2 * BW + gi * CHUNK, 2 * BW + (gi + 1) * CHUNK)
                z = pb_ref[rs_, zs].astype(F32)
                u = pb_ref[rs_, cs_].astype(F32)
                sil, dsil = _silu_and_grad(z)
                dyc = dy_ref[rs_, cs_]
                dmixed = dyc * u * sil
                dpb_ref[rs_, cs_] = (dyc * mixed * sil).astype(BF16)
                dpb_ref[rs_, zs] = (dyc * u * mixed * dsil).astype(BF16)
                dmb = dmixed.astype(BF16)
                dws_ref[gi] += _dot(dmb, vbc, NT)
                dbs_ref[gi] += jnp.sum(dmixed, axis=-1, keepdims=True)
                dvln_ref[rs_, cs_] = _dot(w, dmb, TN)
        dvln = dvln_ref[...]
        dg_ref[...] += jnp.sum(dvln * xh, axis=0, keepdims=True)
        db_ref[...] += jnp.sum(dvln, axis=0, keepdims=True)
        gd = dvln * g_ref[...]
        dv = rs * (gd - jnp.mean(gd, axis=-1, keepdims=True) - xh * jnp.mean(gd * xh, axis=-1, keepdims=True))
        dpb_ref[:, BW:2 * BW] = dv.astype(BF16)

    vec = pl.BlockSpec((1, BW), lambda i: (0, 0))
    wsb = pl.BlockSpec((B_GROUPS, CHUNK, CHUNK), lambda i: (0, 0, 0))
    bsb = pl.BlockSpec((B_GROUPS, CHUNK, 1), lambda i: (0, 0, 0))
    return pl.pallas_call(
        body, name="gmlp_bwd", grid=(t // tb,),
        in_specs=[pl.BlockSpec((tb, W_B), lambda i: (i, 0)), vec, vec, wsb, bsb, pl.BlockSpec((tb, BW), lambda i: (i, 0))],
        out_specs=(pl.BlockSpec((tb, W_B), lambda i: (i, 0)), wsb, bsb, vec, vec),
        out_shape=(SDS((t, W_B), BF16), SDS((B_GROUPS, CHUNK, CHUNK), F32), SDS((B_GROUPS, CHUNK, 1), F32),
                   SDS((1, BW), F32), SDS((1, BW), F32)),
        scratch_shapes=[pltpu.VMEM((tb, BW), F32)],
        compiler_params=_params("arbitrary"),
    )(pb, lng, lnb, ws, bs, dy)


def _mem_scores(pm_ref, kv_ref, h):
    hs = slice(h * M_HEAD_DIM, (h + 1) * M_HEAD_DIM)
    return _dot(pm_ref[:, hs].astype(BF16), kv_ref[:, hs].astype(BF16), NT) * (M_HEAD_DIM ** -0.5)


def _softmax_rows(s):
    e = jnp.exp(s - jnp.max(s, axis=-1, keepdims=True))
    return e / jnp.sum(e, axis=-1, keepdims=True)


def _mem_attn_tile(pm, kv_ref, y_ref):
    s_next = _mem_scores(pm, kv_ref, 0)
    for h in range(M_HEADS):
        hs = slice(h * M_HEAD_DIM, (h + 1) * M_HEAD_DIM)
        vh = kv_ref[:, BW + h * M_HEAD_DIM:BW + (h + 1) * M_HEAD_DIM].astype(BF16)
        s, s_next = s_next, (_mem_scores(pm, kv_ref, h + 1) if h + 1 < M_HEADS else None)
        p = _softmax_rows(s)
        o = _dot(p.astype(BF16), vh, NN)
        z = pm[:, BW + h * M_HEAD_DIM:BW + (h + 1) * M_HEAD_DIM].astype(F32)
        y_ref[:, hs] = (o * (z * _sigmoid(z))).astype(BF16)


def _mem_attn_bwd(pm, kv, dy):
    t = pm.shape[0]
    tq = _tile(t, 512, 8)
    ml = kv.shape[0]
    scale = M_HEAD_DIM ** -0.5

    def body(pm_ref, kv_ref, dy_ref, dpm_ref, dkv_ref):
        @pl.when(pl.program_id(0) == 0)
        def _():
            dkv_ref[...] = jnp.zeros_like(dkv_ref)

        s_next = _mem_scores(pm_ref, kv_ref, 0)
        for h in range(M_HEADS):
            hs = slice(h * M_HEAD_DIM, (h + 1) * M_HEAD_DIM)
            zs = slice(BW + h * M_HEAD_DIM, BW + (h + 1) * M_HEAD_DIM)
            kh = kv_ref[:, hs].astype(BF16)
            vh = kv_ref[:, zs].astype(BF16)
            qh = pm_ref[:, hs].astype(BF16)
            s, s_next = s_next, (_mem_scores(pm_ref, kv_ref, h + 1) if h + 1 < M_HEADS else None)
            p = _softmax_rows(s)
            pb = p.astype(BF16)
            o = _dot(pb, vh, NN)
            sil, dsil = _silu_and_grad(pm_ref[:, zs].astype(F32))
            dyh = dy_ref[:, hs]
            do = dyh * sil
            dpm_ref[:, zs] = (dyh * o * dsil).astype(BF16)
            delta = jnp.sum(do * o, axis=-1, keepdims=True)
            do_b = do.astype(BF16)
            dp = _dot(do_b, vh, NT)
            dr_b = (p * (dp - delta) * scale).astype(BF16)
            dpm_ref[:, hs] = _dot(dr_b, kh, NN).astype(BF16)
            dkv_ref[:, hs] += _dot(dr_b, qh, TN)
            dkv_ref[:, zs] += _dot(pb, do_b, TN)

    kvb = pl.BlockSpec((ml, 2 * BW), lambda i: (0, 0))
    return pl.pallas_call(
        body, name="mem_attn_bwd", grid=(t // tq,),
        in_specs=[pl.BlockSpec((tq, W_M), lambda i: (i, 0)), kvb, pl.BlockSpec((tq, BW), lambda i: (i, 0))],
        out_specs=(pl.BlockSpec((tq, W_M), lambda i: (i, 0)), kvb),
        out_shape=(SDS((t, W_M), BF16), SDS((ml, 2 * BW), F32)),
        compiler_params=_params("arbitrary"),
    )(pm, kv, dy)


def _merge_fwd(ya, yb, ym, pg, g_br, x, w_out, exchange=None):
    t = ya.shape[0]
    tm = _tile(t, 512, 8)
    shard = g_br.shape[-1]

    def body(ya_ref, yb_ref, ym_ref, pg_ref, g_ref, x_ref, wo_ref, m_ref, xn_ref, w_ref):
        @pl.when(pl.program_id(0) == 0)
        def _():
            for d in range(N_DEV):
                w_ref[:, :, d * shard:(d + 1) * shard] = g_ref[d]

        ups = [_dot(y_ref[...], w_ref[n], NN) for n, y_ref in enumerate((ya_ref, yb_ref, ym_ref))]
        acc = None
        for n, up in enumerate(ups):
            term = _sigmoid(pg_ref[:, n * D_MODEL:(n + 1) * D_MODEL].astype(F32)) * up
            acc = term if acc is None else acc + term
        merged = acc.astype(BF16)
        m_ref[...] = merged
        xn_ref[...] = x_ref[...] + _dot(merged, wo_ref[...], NN)

    yb_spec = pl.BlockSpec((tm, BW), lambda i: (i, 0))
    row = pl.BlockSpec((tm, D_MODEL), lambda i: (i, 0))
    return _host_call(
        body, (ya, yb, ym, pg, g_br, x, w_out),
        [yb_spec, yb_spec, yb_spec, pl.BlockSpec((tm, W_G), lambda i: (i, 0)),
         pl.BlockSpec(g_br.shape, lambda i: (0, 0, 0, 0)), row, pl.BlockSpec(w_out.shape, lambda i: (0, 0))],
        (row, row, pl.BlockSpec((3, BW, D_MODEL), lambda i: (0, 0, 0))),
        (SDS((t, D_MODEL), BF16), SDS((t, D_MODEL), F32), SDS((3, BW, D_MODEL), BF16)),
        name="merge_fwd", grid=(t // tm,), semantics=("arbitrary",), exchange=exchange)


def _merge_bwd(ya, yb, ym, pg, wbr, dx_out, w_out, exchange=None):
    t = ya.shape[0]
    tm = _tile(t, 512, 16)
    n_steps = t // tm
    shard = D_MODEL // N_DEV

    def body(ya_ref, yb_ref, ym_ref, pg_ref, w_ref, dx_ref, wo_ref, dya_ref, dyb_ref, dym_ref, dpg_ref, dwd_ref, dw_ref):
        @pl.when(pl.program_id(0) == 0)
        def _():
            dw_ref[...] = jnp.zeros_like(dw_ref)

        dmf = _dot(dx_ref[...].astype(BF16), wo_ref[...], NT)
        branches = ((ya_ref, dya_ref), (yb_ref, dyb_ref), (ym_ref, dym_ref))

        def gate(n):
            gt = _sigmoid(pg_ref[:, n * D_MODEL:(n + 1) * D_MODEL].astype(F32))
            return gt, (dmf * gt).astype(BF16)

        gt, dup = gate(0)
        for n, (y_ref, dy_ref) in enumerate(branches):
            nxt = gate(n + 1) if n + 1 < len(branches) else None
            y, w = y_ref[...], w_ref[n]
            dy_ref[...] = _dot(dup, w, NT)
            dw_ref[n] += _dot(y, dup, TN)
            up = _dot(y, w, NN)
            dpg_ref[:, n * D_MODEL:(n + 1) * D_MODEL] = (dmf * up * gt * (1.0 - gt)).astype(BF16)
            if nxt is not None:
                gt, dup = nxt

        @pl.when(pl.program_id(0) == n_steps - 1)
        def _():
            for d in range(N_DEV):
                dwd_ref[d] = dw_ref[:, :, d * shard:(d + 1) * shard].astype(BF16)

    y_spec = pl.BlockSpec((tm, BW), lambda i: (i, 0))
    w_spec = pl.BlockSpec((3, BW, D_MODEL), lambda i: (0, 0, 0))
    return _host_call(
        body, (ya, yb, ym, pg, wbr, dx_out, w_out),
        [y_spec, y_spec, y_spec, pl.BlockSpec((tm, W_G), lambda i: (i, 0)), w_spec,
         pl.BlockSpec((tm, D_MODEL), lambda i: (i, 0)), pl.BlockSpec(w_out.shape, lambda i: (0, 0))],
        (y_spec, y_spec, y_spec, pl.BlockSpec((tm, W_G), lambda i: (i, 0)),
         pl.BlockSpec((N_DEV, 3, BW, shard), lambda i: (0, 0, 0, 0))),
        (SDS((t, BW), F32), SDS((t, BW), F32), SDS((t, BW), F32), SDS((t, W_G), BF16), SDS((N_DEV, 3, BW, shard), BF16)),
        name="merge_bwd", grid=(n_steps,), semantics=("arbitrary",), scratch=[pltpu.VMEM((3, BW, D_MODEL), F32)],
        exchange=exchange)


def _loss_head(x, g, target):
    t, d = x.shape
    tm = _tile(t, 512, 8)

    def body(x_ref, g_ref, t_ref, loss_ref, dx_ref, dg_ref):
        @pl.when(pl.program_id(0) == 0)
        def _():
            loss_ref[...] = jnp.zeros_like(loss_ref)
            dg_ref[...] = jnp.zeros_like(dg_ref)

        xf = x_ref[...]
        r = lax.rsqrt(jnp.mean(xf * xf, axis=-1, keepdims=True) + EPS)
        xh = xf * r
        err = xh * g_ref[...] - t_ref[...]
        per_tok = jnp.mean(err * err, axis=-1, keepdims=True)
        loss_ref[...] += 0.5 * jnp.sum(per_tok, axis=0, keepdims=True)
        dy = err * (1.0 / d)
        gd = dy * g_ref[...]
        dx_ref[...] = r * (gd - xh * jnp.mean(gd * xh, axis=-1, keepdims=True))
        dg_ref[...] += jnp.sum(dy * xh, axis=0, keepdims=True)

    row = pl.BlockSpec((tm, d), lambda i: (i, 0))
    vec = pl.BlockSpec((1, d), lambda i: (0, 0))
    return pl.pallas_call(
        body, name="loss_head", grid=(t // tm,),
        in_specs=[row, vec, row],
        out_specs=(pl.BlockSpec((1, 1), lambda i: (0, 0)), row, vec),
        out_shape=(SDS((1, 1), F32), SDS((t, d), F32), SDS((1, d), F32)),
        compiler_params=_params("arbitrary"),
    )(x, g, target)


def _layer_fwd(x, mem, w, tabs, next_shards=None):
    cs, sn = tabs
    riding = next_shards is not None
    memn, kv = _mem_kv_fwd(mem, w["mem_g"], w["w_kv"])
    h, pa, pb, pm, pg, q, qT, k, kT, vT, yb, ym = _norm_in_proj(
        x, w["norm_g"], w["w_in_t"], cs, sn, w["qg2"], w["kg2"], w["ln_g"], w["ln_b"], w["w_s"], w["b_s"], kv)
    (o, lse, ya), gathered = _attn_fwd(qT, k, vT, pa, exchange=_gather_first_hop(next_shards) if riding else None)
    (merged, x_next, w_br), gathered = _merge_fwd(ya, yb, ym, pg, w["g_br"], x, w["w_out"],
                                                  exchange=_gather_forward(gathered) if riding else None)
    saved = dict(x=x, h=h, pa=pa, pb=pb, pm=pm, pg=pg, q=q, qT=qT, k=k, kT=kT, vT=vT, o=o, lse=lse, ya=ya, yb=yb, ym=ym,
                 memn=memn, kv=kv, merged=merged, w_br=w_br)
    return x_next, saved, gathered


def _layer_bwd(dx_out, mem, w, s, tabs, pending=None, core=None, scatter_own=False):
    cs, sn = tabs
    t = dx_out.shape[0]
    riding = pending is not None
    by_owner = lambda a: a.reshape((N_DEV // 2, 2, -1) + a.shape[-2:])
    row_shards = lambda a: by_owner(a.reshape(N_DEV, a.shape[0] // N_DEV, a.shape[1]))
    d_w_out = _mm_tn(s["merged"], dx_out, "d_w_out")
    (dya, dyb, dym, dpg, d_w_br), recv = _merge_bwd(s["ya"], s["yb"], s["ym"], s["pg"], s["w_br"], dx_out, w["w_out"],
                                                    exchange=_scatter_to_sibling(pending) if riding else None)
    parts = _pair_sums(BIG, pending, recv, core) if riding else None
    grads = dict(w_br=by_owner(d_w_br), w_out=row_shards(d_w_out))
    early = [grads[n] for n in BIG[2:]]
    (dq, dkT, dvT), landed = _attn_bwd(
        s["q"], s["qT"], s["k"], s["kT"], s["vT"], s["pa"], s["o"], s["lse"], dya,
        exchange=_both(_scatter_to_chips(parts) if riding else None, _scatter_to_sibling(early) if scatter_own else None))
    from_chips, early_recv = (landed[:len(BIG)], landed[len(BIG):]) if riding else (None, landed)
    early_parts = _pair_sums(BIG[2:], early, early_recv, core) if scatter_own else None
    (dpa, d_qg2, d_kg2), early_from_chips = _attn_prep_bwd(
        s["pa"], cs, sn, w["qg2"], w["kg2"], dq, dkT.reshape(-1, t), dvT.reshape(-1, t), dya, s["o"],
        exchange=_scatter_to_chips(early_parts) if scatter_own else None)
    dpb, d_w_s, d_b_s, d_ln_g, d_ln_b = _gmlp_bwd(s["pb"], w["ln_g"], w["ln_b"], w["w_s"], w["b_s"], dyb)
    dpm, dkv = _mem_attn_bwd(s["pm"], s["kv"], dym)
    d_w_kv, d_mem_g = _mem_kv_bwd(mem, w["mem_g"], s["memn"], w["w_kv"], dkv)
    dps = (dpa, dpb, dpm, dpg)
    grads.update(w_in=row_shards(_d_w_in_t(dps, s["h"])), w_mem_kv=row_shards(d_w_kv))
    own = None
    if scatter_own:
        late = [grads[n] for n in BIG[:2]]
        late_parts = _pair_sums(BIG[:2], late, _exchange_call(_scatter_to_sibling(late), "rs_sibling_swap"), core)
        own = _scatter_to_chips(late_parts)
    (dx_in, d_norm_g), late_from_chips = _dh_rmsnorm_bwd(dps, w["w_in_t"], s["x"], w["norm_g"], dx_out, exchange=own)
    if scatter_own:
        own_parts, own_from_chips = late_parts + early_parts, tuple(late_from_chips) + tuple(early_from_chips)
    grads.update(norm_g=d_norm_g[0], q_norm_g=d_qg2[0, :HEAD_DIM] + d_qg2[0, HEAD_DIM:],
                 k_norm_g=d_kg2[0, :HEAD_DIM] + d_kg2[0, HEAD_DIM:], sg_ln_g=d_ln_g[0], sg_ln_b=d_ln_b[0],
                 w_s=d_w_s, b_s=d_b_s[:, :, 0], mem_norm_g=d_mem_g[0])
    return (dx_in, grads, ((parts, from_chips) if riding else None),
            ((own_parts, own_from_chips) if scatter_own else None))


def _layer_weights(l, w_in_t, w_kv, g_br, w_out, small):
    tile2 = lambda g: jnp.tile(g.reshape(1, -1), (1, 2))
    return dict(
        w_in_t=w_in_t, w_kv=w_kv, g_br=g_br, w_out=w_out,
        norm_g=small["norm_g"][l].reshape(1, -1), qg2=tile2(small["q_norm_g"][l]), kg2=tile2(small["k_norm_g"][l]),
        ln_g=small["sg_ln_g"][l].reshape(1, -1), ln_b=small["sg_ln_b"][l].reshape(1, -1),
        w_s=small["w_s"][l], b_s=small["b_s"][l][:, :, None], mem_g=small["mem_norm_g"][l].reshape(1, -1))


def _position():
    x, y, c = lax.axis_index("x"), lax.axis_index("y"), lax.axis_index("c")
    return x, y, c, [(1 - x, y), (x, 1 - y), (1 - x, 1 - y)]


def _gather_first_hop(shards):
    n = len(shards)

    def make(x_refs, out_refs, sems):
        send_sems, recv_sems, local_sems = sems
        x, y, c, chips = _position()
        me = 4 * x + 2 * y + c
        peers = [(x, y, 1 - c)] + [(cx, cy, c) for cx, cy in chips]
        copies = [pltpu.make_async_copy(x_refs[t], out_refs[t].at[me], local_sems.at[t]) for t in range(n)]
        copies += [pltpu.make_async_remote_copy(
            src_ref=x_refs[t], dst_ref=out_refs[t].at[me], send_sem=send_sems.at[t, k], recv_sem=recv_sems.at[t, k],
            device_id=peer, device_id_type=MESH_ID) for t in range(n) for k, peer in enumerate(peers)]
        return copies

    return _Exchange(shards, [SDS((N_DEV,) + a.shape, a.dtype) for a in shards],
                     [pltpu.SemaphoreType.DMA((n, 4)), pltpu.SemaphoreType.DMA((n, 4)), pltpu.SemaphoreType.DMA((n,))], make)


def _gather_forward(gathered):
    n = len(gathered)

    def make(in_refs, out_refs, sems):
        send_sems, recv_sems = sems
        x, y, c, chips = _position()
        return [pltpu.make_async_remote_copy(
            src_ref=in_refs[t].at[4 * cx + 2 * cy + c], dst_ref=out_refs[t].at[4 * cx + 2 * cy + c],
            send_sem=send_sems.at[t, j], recv_sem=recv_sems.at[t, j], device_id=(x, y, 1 - c), device_id_type=MESH_ID)
            for t in range(n) for j, (cx, cy) in enumerate(chips)]

    return _Exchange(gathered, [SDS(a.shape, a.dtype) for a in gathered],
                     [pltpu.SemaphoreType.DMA((n, 3)), pltpu.SemaphoreType.DMA((n, 3))], make,
                     aliases={t: t for t in range(n)})


def _all_gather(shards, name):
    return _exchange_call(_gather_forward(_exchange_call(_gather_first_hop(shards), name + "_hop1")), name + "_hop2")


def _scatter_to_sibling(dests):
    n = len(dests)

    def make(d_refs, recv_refs, sems):
        send_sems, recv_sems = sems
        x, y, c, _ = _position()
        return [pltpu.make_async_remote_copy(
            src_ref=d_refs[t].at[:, 1 - c], dst_ref=recv_refs[t], send_sem=send_sems.at[t],
            recv_sem=recv_sems.at[t], device_id=(x, y, 1 - c), device_id_type=MESH_ID) for t in range(n)]

    return _Exchange(dests, [SDS(a.shape[:1] + a.shape[2:], a.dtype) for a in dests],
                     [pltpu.SemaphoreType.DMA((n,)), pltpu.SemaphoreType.DMA((n,))], make)


def _scatter_to_chips(parts):
    n = len(parts)

    def make(p_refs, recv_refs, sems):
        send_sems, recv_sems = sems
        _, _, c, chips = _position()
        return [pltpu.make_async_remote_copy(
            src_ref=p_refs[t].at[2 * cx + cy], dst_ref=recv_refs[t].at[k], send_sem=send_sems.at[t, k],
            recv_sem=recv_sems.at[t, k], device_id=(cx, cy, c), device_id_type=MESH_ID)
            for t in range(n) for k, (cx, cy) in enumerate(chips)]

    return _Exchange(parts, [SDS((3,) + a.shape[1:], a.dtype) for a in parts],
                     [pltpu.SemaphoreType.DMA((n, 3)), pltpu.SemaphoreType.DMA((n, 3))], make)


def _pair_sums(names, dests, recv, core):
    n = len(dests)

    def body(core_ref, *refs):
        for a_ref, b_ref, o_ref in zip(refs[:n], refs[n:2 * n], refs[2 * n:]):
            o_ref[...] = (a_ref[...].astype(F32) + b_ref[...].astype(F32)).astype(o_ref.dtype)

    mine = [pl.BlockSpec((None, None) + d.shape[2:], lambda j, core_ref: (j, core_ref[0], 0, 0, 0)) for d in dests]
    theirs = [pl.BlockSpec((None,) + r.shape[1:], lambda j, core_ref: (j, 0, 0, 0)) for r in recv]
    return list(pl.pallas_call(
        body, name="rs_pair_sum_" + "_".join(names), out_shape=tuple(SDS(r.shape, BF16) for r in recv),
        grid_spec=pltpu.PrefetchScalarGridSpec(
            num_scalar_prefetch=1, grid=(4,), in_specs=mine + theirs, out_specs=tuple(theirs)),
        compiler_params=_params("parallel"),
    )(core, *dests, *recv))


def _adamw_math(w, g, m, v):
    m = ADAM_B1 * m + (1.0 - ADAM_B1) * g
    v = ADAM_B2 * v + (1.0 - ADAM_B2) * (g * g)
    m_hat = m / (1.0 - ADAM_B1 ** ADAM_STEP)
    v_hat = v / (1.0 - ADAM_B2 ** ADAM_STEP)
    delta = -ADAM_LR * (m_hat / (jnp.sqrt(v_hat) + ADAM_EPS) + ADAM_WD * w)
    return delta, m, v


def _sum_and_adamw(parts, w, m, v, name):
    n, r, ln = parts.shape
    tr = _tile(r, 512, 16)

    def body(p_ref, w_ref, m_ref, v_ref, g_out, d_out, m_out, v_out):
        g = p_ref[0].astype(F32)
        for j in range(1, n):
            g = g + p_ref[j].astype(F32)
        delta, nm, nv = _adamw_math(w_ref[...], g, m_ref[...], v_ref[...])
        g_out[...] = g
        d_out[...] = delta
        m_out[...] = nm
        v_out[...] = nv

    blk = pl.BlockSpec((tr, ln), lambda i: (i, 0))
    return pl.pallas_call(
        body, name=name, grid=(r // tr,),
        in_specs=[pl.BlockSpec((n, tr, ln), lambda i: (0, i, 0)), blk, blk, blk],
        out_specs=(blk, blk, blk, blk), out_shape=tuple(SDS((r, ln), F32) for _ in range(4)),
        compiler_params=_params("parallel"),
    )(parts, w, m, v)


BIG = ("w_in", "w_mem_kv", "w_br", "w_out")
SMALL = ("norm_g", "q_norm_g", "k_norm_g", "sg_ln_g", "sg_ln_b", "w_s", "b_s", "mem_norm_g", "final_g")


def _pack(arrs, row_unit=16):
    flat = jnp.concatenate([a.reshape(-1) for a in arrs])
    pad = (-flat.shape[0]) % (row_unit * LANES)
    if pad:
        flat = jnp.concatenate([flat, jnp.zeros((pad,), flat.dtype)])
    return flat.reshape(-1, LANES)


def _unpack(buf, shapes):
    flat = buf.reshape(-1)
    out, off = [], 0
    for shp in shapes:
        n = 1
        for s_ in shp:
            n *= s_
        out.append(flat[off:off + n].reshape(shp))
        off += n
    return out


def _shard_sum_adamw(part, from_chips, chip, w, m, v, layer, prev, name):
    _, na, r, cdim = part.shape
    flat = w.ndim == 3
    assert not flat or na == 1
    tr = _tile(r, max(8, (256 * 1024) // (na * cdim)), 8)
    n_prev = 0 if prev is None else len(prev)

    def body(chip_ref, p_ref, f_ref, w_ref, m_ref, v_ref, *rest):
        g_out, d_out, m_out, v_out = rest[n_prev:]
        g = p_ref[...].astype(F32)
        for j in range(3):
            g = g + f_ref[j].astype(F32)
        delta, nm, nv = _adamw_math(w_ref[...], g, m_ref[...], v_ref[...])
        g_out[...] = g
        d_out[...] = delta
        m_out[...] = nm
        v_out[...] = nv

    a_blk = None if flat else na
    if flat:
        lay = pl.BlockSpec((None, tr, cdim), lambda i, chip_ref: (layer, i, 0))
    else:
        lay = pl.BlockSpec((None, na, tr, cdim), lambda i, chip_ref: (layer, 0, i, 0))
    return pl.pallas_call(
        body, name=name, out_shape=tuple(SDS(w.shape, F32) for _ in range(4)),
        grid_spec=pltpu.PrefetchScalarGridSpec(
            num_scalar_prefetch=1, grid=(r // tr,),
            in_specs=[pl.BlockSpec((None, a_blk, tr, cdim), lambda i, chip_ref: (chip_ref[0], 0, i, 0)),
                      pl.BlockSpec((3, a_blk, tr, cdim), lambda i, chip_ref: (0, 0, i, 0)), lay, lay, lay]
            + [HBM] * n_prev,
            out_specs=(lay, lay, lay, lay)),
        input_output_aliases={6 + j: j for j in range(n_prev)},
        compiler_params=_params("parallel"),
    )(chip, part, from_chips, w, m, v, *(prev or ()))


def kernel(x, mem, norm_g, w_in, q_norm_g, k_norm_g, sg_ln_g, sg_ln_b, w_s, b_s, mem_norm_g, w_mem_kv, w_br, w_out, final_g, loss_target, m_norm_g, m_w_in, m_q_norm_g, m_k_norm_g, m_sg_ln_g, m_sg_ln_b, m_w_s, m_b_s, m_mem_norm_g, m_w_mem_kv, m_w_br, m_w_out, m_final_g, v_norm_g, v_w_in, v_q_norm_g, v_k_norm_g, v_sg_ln_g, v_sg_ln_b, v_w_s, v_b_s, v_mem_norm_g, v_w_mem_kv, v_w_br, v_w_out, v_final_g):
    wts = dict(norm_g=norm_g, w_in=w_in, q_norm_g=q_norm_g, k_norm_g=k_norm_g, sg_ln_g=sg_ln_g, sg_ln_b=sg_ln_b,
               w_s=w_s, b_s=b_s, mem_norm_g=mem_norm_g, w_mem_kv=w_mem_kv, w_br=w_br, w_out=w_out, final_g=final_g)
    mom1 = dict(norm_g=m_norm_g, w_in=m_w_in, q_norm_g=m_q_norm_g, k_norm_g=m_k_norm_g, sg_ln_g=m_sg_ln_g,
                sg_ln_b=m_sg_ln_b, w_s=m_w_s, b_s=m_b_s, mem_norm_g=m_mem_norm_g, w_mem_kv=m_w_mem_kv, w_br=m_w_br,
                w_out=m_w_out, final_g=m_final_g)
    mom2 = dict(norm_g=v_norm_g, w_in=v_w_in, q_norm_g=v_q_norm_g, k_norm_g=v_k_norm_g, sg_ln_g=v_sg_ln_g,
                sg_ln_b=v_sg_ln_b, w_s=v_w_s, b_s=v_b_s, mem_norm_g=v_mem_norm_g, w_mem_kv=v_w_mem_kv, w_br=v_w_br,
                w_out=v_w_out, final_g=v_final_g)
    dp = w_in.shape[0]
    core = lax.axis_index("c").astype(jnp.int32).reshape(1)
    chip = (2 * lax.axis_index("x") + lax.axis_index("y")).astype(jnp.int32).reshape(1)

    t_in = lambda a: jnp.swapaxes(a, 1, 2)
    wts, mom1, mom2 = [dict(d_, w_in=t_in(d_["w_in"])) for d_ in (wts, mom1, mom2)]

    shard_bf = {n: wts[n].astype(BF16) for n in BIG}
    shards = lambda l: [shard_bf[n][l] for n in BIG]
    x_l, mem_l = x[0], mem[0]
    tabs = _rope_tables(x_l.shape[0])

    gathered = _all_gather(shards(0), "weights_all_gather")
    layers, saved = [], []
    for l in range(dp):
        g_in, g_kv, g_br, g_out = gathered
        layers.append(_layer_weights(l, g_in.reshape(IN_WIDTH, -1), g_kv.reshape(D_MODEL, -1), g_br,
                                     g_out.reshape(D_MODEL, -1), wts))
        x_l, s, gathered = _layer_fwd(x_l, mem_l, layers[l], tabs, next_shards=shards(l + 1) if l + 1 < dp else None)
        saved.append(s)
    loss_local, dx, d_final_g = _loss_head(x_l, final_g.reshape(1, -1), loss_target[0])
    loss = lax.psum(loss_local[0, 0], AXES)

    def finish(l, parts, from_chips, prev):
        return {n: _shard_sum_adamw(p, f, chip, wts[n], mom1[n], mom2[n], l, None if prev is None else prev[n],
                                    "sum_adamw_" + n)
                for n, p, f in zip(BIG, parts, from_chips)}

    grads, updated, pending = [None] * dp, None, None
    for l in reversed(range(dp)):
        dx, grads[l], scattered, own = _layer_bwd(dx, mem_l, layers[l], saved[l], tabs, pending=pending, core=core,
                                                  scatter_own=(l == 0))
        if scattered is not None:
            updated = finish(l + 1, *scattered, updated)
        pending = [grads[l][n] for n in BIG]
    updated = finish(0, *own, updated)
    grad_x = dx
    big_out = [{n: (t_in(updated[n][k]) if n == "w_in" else updated[n][k]) for n in BIG} for k in range(4)]

    small_g = {n: jnp.stack([g[n] for g in grads]) for n in SMALL if n != "final_g"}
    small_g["final_g"] = d_final_g
    vectors = [n for n in SMALL if n != "w_s"]
    rows128 = lambda a: a.reshape(-1, LANES)
    all_vec, all_ws = _all_gather([_pack([small_g[n] for n in vectors]), rows128(small_g["w_s"]).astype(BF16)],
                                  "small_all_gather")
    vec_bufs = _sum_and_adamw(all_vec, *[_pack([d_[n] for n in vectors]) for d_ in (wts, mom1, mom2)], "small_sum_adamw")
    ws_bufs = _sum_and_adamw(all_ws, *[rows128(d_["w_s"]) for d_ in (wts, mom1, mom2)], "w_s_sum_adamw")

    outs = []
    for big_vals, vec_buf, ws_buf in zip(big_out, vec_bufs, ws_bufs):
        vals = dict(big_vals)
        vals.update(zip(vectors, _unpack(vec_buf, [wts[n].shape for n in vectors])))
        vals["w_s"] = ws_buf.reshape(wts["w_s"].shape)
        outs.append(vals)
    order = ("norm_g", "w_in", "q_norm_g", "k_norm_g", "sg_ln_g", "sg_ln_b", "w_s", "b_s", "mem_norm_g", "w_mem_kv",
             "w_br", "w_out", "final_g")
    result = [loss, grad_x[None]]
    for vals in outs:
        result += [vals[n] for n in order]
    return tuple(result)
```

```python
import functools

import jax
import jax.numpy as jnp
from jax import lax
from jax.experimental import pallas as pl
from jax.experimental.pallas import tpu as pltpu

F32 = jnp.float32
BF16 = jnp.bfloat16
SDS = jax.ShapeDtypeStruct
MESH_ID = pl.DeviceIdType.MESH
AXES = ("x", "y", "c")
N_DEV = 8

D_MODEL = 1024
GRID_W = 64
CHUNK = 128
ROPE_THETA = 10000.0
EPS = 1e-6
HEAD_DIM = 64
A_HEADS = 8
A_KV_HEADS = 2
B_GROUPS = 4
M_HEADS = 4
M_HEAD_DIM = 128
BW = 512
W_A = 1280
W_B = 1536
W_M = 1024
W_G = 3072
IN_WIDTH = W_A + W_B + W_M + W_G
GROUP_OFFS = (0, W_A, W_A + W_B, W_A + W_B + W_M, IN_WIDTH)

ADAM_LR = 0.001
ADAM_B1 = 0.9
ADAM_B2 = 0.999
ADAM_EPS = 1e-08
ADAM_WD = 0.01
ADAM_STEP = 10

LANES = 128
KEY_CHUNK = 2048
VMEM_LIMIT = 52 * 1024 * 1024


def _tile(n, cap, unit=LANES):
    if n <= cap:
        return n
    t = (cap // unit) * unit
    while n % t:
        t -= unit
    return t


def _params(*sem):
    return pltpu.CompilerParams(dimension_semantics=sem, vmem_limit_bytes=VMEM_LIMIT)


def _sigmoid(z):
    return 0.5 * jnp.tanh(0.5 * z) + 0.5


def _silu_and_grad(z):
    s = _sigmoid(z)
    return z * s, s * (1.0 + z * (1.0 - s))


def _dot(a, b, dims):
    return lax.dot_general(a, b, (dims, ((), ())), preferred_element_type=F32)


NN = ((1,), (0,))
NT = ((1,), (1,))
TN = ((0,), (0,))
HBM = pl.BlockSpec(memory_space=pl.ANY)


class _Exchange:
    def __init__(self, ins, out_shapes, sems, make, aliases=None):
        self.ins, self.out_shapes, self.sems, self.make = list(ins), list(out_shapes), list(sems), make
        self.aliases = dict(aliases or {})

    def start(self, in_refs, out_refs, sems):
        for cp in self.make(in_refs, out_refs, sems):
            cp.start()

    def finish(self, in_refs, out_refs, sems):
        for cp in self.make(in_refs, out_refs, sems):
            cp.wait()


def _both(a, b):
    if a is None or b is None:
        return a if b is None else b
    n_in, n_out, n_sem = len(a.ins), len(a.out_shapes), len(a.sems)

    def make(in_refs, out_refs, sems):
        return (a.make(in_refs[:n_in], out_refs[:n_out], sems[:n_sem])
                + b.make(in_refs[n_in:], out_refs[n_out:], sems[n_sem:]))

    aliases = dict(a.aliases)
    aliases.update({n_in + i: n_out + o for i, o in b.aliases.items()})
    return _Exchange(a.ins + b.ins, a.out_shapes + b.out_shapes, a.sems + b.sems, make, aliases)


def _host_call(body, ins, in_specs, out_specs, out_shape, *, name, grid, semantics, scratch=(), exchange=None):
    ins, out_shape, scratch = list(ins), list(out_shape), list(scratch)
    if exchange is None:
        res = pl.pallas_call(
            body, name=name, grid=grid, in_specs=list(in_specs), out_specs=tuple(out_specs), out_shape=tuple(out_shape),
            scratch_shapes=scratch, compiler_params=_params(*semantics))(*ins)
        return tuple(res), ()
    n_in, n_out, n_scr = len(ins), len(out_shape), len(scratch)
    x_in, x_out = len(exchange.ins), len(exchange.out_shapes)

    def carrying(*refs):
        o0 = n_in + x_in
        s0 = o0 + n_out + x_out
        c_in, c_out, c_sems = refs[n_in:o0], refs[o0 + n_out:s0], refs[s0 + n_scr:]
        ids = [pl.program_id(a) for a in range(len(grid))]
        first = functools.reduce(jnp.logical_and, [i == 0 for i in ids])
        last = functools.reduce(jnp.logical_and, [i == g - 1 for i, g in zip(ids, grid)])

        @pl.when(first)
        def _():
            exchange.start(c_in, c_out, c_sems)

        body(*refs[:n_in], *refs[o0:o0 + n_out], *refs[s0:s0 + n_scr])

        @pl.when(last)
        def _():
            exchange.finish(c_in, c_out, c_sems)

    res = pl.pallas_call(
        carrying, name=name, grid=grid, in_specs=list(in_specs) + [HBM] * x_in,
        out_specs=tuple(out_specs) + (HBM,) * x_out, out_shape=tuple(out_shape) + tuple(exchange.out_shapes),
        scratch_shapes=scratch + exchange.sems,
        input_output_aliases={n_in + i: n_out + o for i, o in exchange.aliases.items()},
        compiler_params=_params(*(["arbitrary"] * len(grid))))(*ins, *exchange.ins)
    return tuple(res[:n_out]), tuple(res[n_out:])


def _exchange_call(exchange, name):
    x_in = len(exchange.ins)

    def body(*refs):
        x_out = len(exchange.out_shapes)
        c_in, c_out, c_sems = refs[:x_in], refs[x_in:x_in + x_out], refs[x_in + x_out:]
        exchange.start(c_in, c_out, c_sems)
        exchange.finish(c_in, c_out, c_sems)

    return pl.pallas_call(
        body, name=name, in_specs=[HBM] * x_in, out_specs=tuple([HBM] * len(exchange.out_shapes)),
        out_shape=tuple(exchange.out_shapes), scratch_shapes=exchange.sems, input_output_aliases=exchange.aliases,
    )(*exchange.ins)


def _mem_kv_fwd(mem, g, w_kv):
    def body(mem_ref, g_ref, w_ref, memn_ref, kv_ref):
        xf = mem_ref[...]
        r = lax.rsqrt(jnp.mean(xf * xf, axis=-1, keepdims=True) + EPS)
        memn = (xf * r * g_ref[...]).astype(BF16)
        memn_ref[...] = memn
        kv_ref[...] = _dot(memn, w_ref[...], NN)

    return pl.pallas_call(
        body, name="mem_kv_fwd", out_shape=(SDS(mem.shape, BF16), SDS((mem.shape[0], w_kv.shape[1]), F32)),
        compiler_params=pltpu.CompilerParams(vmem_limit_bytes=VMEM_LIMIT))(mem, g, w_kv)


def _mem_kv_bwd(mem, g, memn, w_kv, dkv):
    def body(mem_ref, g_ref, memn_ref, w_ref, dkv_ref, dw_ref, dg_ref):
        dkv_b = dkv_ref[...].astype(BF16)
        dw_ref[...] = _dot(memn_ref[...], dkv_b, TN).astype(BF16)
        dmemn = _dot(dkv_b, w_ref[...], NT)
        dg_ref[...] = _rmsnorm_bwd_math(mem_ref[...], g_ref[...], dmemn)[1]

    return pl.pallas_call(
        body, name="mem_kv_bwd", out_shape=(SDS(w_kv.shape, BF16), SDS(g.shape, F32)),
        compiler_params=pltpu.CompilerParams(vmem_limit_bytes=VMEM_LIMIT))(mem, g, memn, w_kv, dkv)


def _mm_tn(a, b, name):
    (k, m), (k2, n) = a.shape, b.shape
    assert k == k2, (a.shape, b.shape)
    tk = _tile(k, 2048, 16)
    nk = k // tk

    def body(a_ref, b_ref, o_ref, acc):
        kk = pl.program_id(0)
        part = _dot(a_ref[...].astype(BF16), b_ref[...].astype(BF16), TN)

        @pl.when(kk == 0)
        def _():
            acc[...] = part

        @pl.when(kk > 0)
        def _():
            acc[...] += part

        @pl.when(kk == nk - 1)
        def _():
            o_ref[...] = acc[...].astype(BF16)

    return pl.pallas_call(
        body, name=name, grid=(nk,),
        in_specs=[pl.BlockSpec((tk, m), lambda kk: (kk, 0)), pl.BlockSpec((tk, n), lambda kk: (kk, 0))],
        out_specs=pl.BlockSpec((m, n), lambda kk: (0, 0)), out_shape=SDS((m, n), BF16),
        scratch_shapes=[pltpu.VMEM((m, n), F32)], compiler_params=_params("arbitrary"),
    )(a, b)


def _rmsnorm_bwd_math(xf, g, dh):
    r = lax.rsqrt(jnp.mean(xf * xf, axis=-1, keepdims=True) + EPS)
    xh = xf * r
    gd = dh * g
    dx = r * (gd - xh * jnp.mean(gd * xh, axis=-1, keepdims=True))
    return dx, jnp.sum(dh * xh, axis=0, keepdims=True)


def _group_rows(p):
    return slice(GROUP_OFFS[p], GROUP_OFFS[p + 1])


def _d_w_in_t(dps, h):
    t, d = h.shape
    tm = 256
    assert all(off % tm == 0 for off in GROUP_OFFS)
    tiles = [a.shape[1] // tm for a in dps]
    starts = [sum(tiles[:p]) for p in range(len(dps))]

    def body(*refs):
        a_refs, h_ref, o_ref = refs[:len(dps)], refs[len(dps)], refs[len(dps) + 1]
        i = pl.program_id(0)
        for p, a_ref in enumerate(a_refs):
            @pl.when((i >= starts[p]) & (i < starts[p] + tiles[p]))
            def _():
                o_ref[...] = _dot(a_ref[...], h_ref[...], TN).astype(BF16)

    def tile_of(p):
        return lambda i: jnp.clip(i - starts[p], 0, tiles[p] - 1)

    return pl.pallas_call(
        body, name="d_w_in", grid=(sum(tiles),),
        in_specs=[pl.BlockSpec((t, tm), lambda i, p=p: (0, tile_of(p)(i))) for p in range(len(dps))]
        + [pl.BlockSpec((t, d), lambda i: (0, 0), pipeline_mode=pl.Buffered(1))],
        out_specs=pl.BlockSpec((tm, d), lambda i: (i, 0)),
        out_shape=SDS((sum(tiles) * tm, d), BF16), compiler_params=_params("arbitrary"),
    )(*dps, h)


def _norm_in_proj(x, g, w_t, cs, sn, qg2, kg2, lng, lnb, ws, bs, kv):
    t, d = x.shape
    tm = _tile(t, 512, LANES)
    widths = [GROUP_OFFS[p + 1] - GROUP_OFFS[p] for p in range(4)]
    scale = HEAD_DIM ** -0.5

    def body(x_ref, g_ref, w_ref, cs_ref, sn_ref, qg_ref, kg_ref, lng_ref, lnb_ref, ws_ref, bs_ref, kv_ref,
             h_ref, pa_ref, pb_ref, pm_ref, pg_ref, q_ref, qT_ref, k_ref, kT_ref, vT_ref, yb_ref, ym_ref):
        xf = x_ref[...]
        r = lax.rsqrt(jnp.mean(xf * xf, axis=-1, keepdims=True) + EPS)
        hb = (xf * r * g_ref[...]).astype(BF16)
        h_ref[...] = hb
        groups = [_dot(hb, w_ref[_group_rows(p), :], NT).astype(BF16) for p in range(4)]
        for o_ref, val in zip((pa_ref, pb_ref, pm_ref, pg_ref), groups):
            o_ref[...] = val
        pa = groups[0]
        lane = lax.broadcasted_iota(jnp.int32, (tm, LANES), 1)
        c, s = cs_ref[...], sn_ref[...]

        def norm_rope(xg, g2):
            rr = lax.rsqrt(_head_sums(xg * xg, lane) * (1.0 / HEAD_DIM) + EPS)
            xn = xg * rr * g2
            return xn * c + _swap16(xn, lane) * s

        for gi in range(4):
            sl = slice(gi * LANES, (gi + 1) * LANES)
            qr = norm_rope(pa[:, sl].astype(F32), qg_ref[...]) * scale
            q_ref[:, sl] = qr.astype(BF16)
            qT_ref[sl, :] = qr.T.astype(BF16)
        kr = norm_rope(pa[:, 512:640].astype(F32), kg_ref[...])
        kT_ref[...] = kr.T.astype(BF16)
        vT_ref[...] = pa[:, 640:768].astype(F32).T.astype(BF16)
        kr = kr.astype(BF16)
        for kvh in range(A_KV_HEADS):
            k_ref[kvh] = kr[:, kvh * HEAD_DIM:(kvh + 1) * HEAD_DIM]
        _gmlp_tile(groups[1], lng_ref, lnb_ref, ws_ref, bs_ref, yb_ref)
        _mem_attn_tile(groups[2], kv_ref, ym_ref)

    row = lambda wd: pl.BlockSpec((tm, wd), lambda i: (i, 0))
    col = lambda rws: pl.BlockSpec((rws, tm), lambda i: (0, i))
    whole = lambda a: pl.BlockSpec(a.shape, lambda i: (0,) * a.ndim)
    hm = pl.BlockSpec((A_KV_HEADS, tm, HEAD_DIM), lambda i: (0, i, 0))
    return pl.pallas_call(
        body, name="norm_in_proj", grid=(t // tm,),
        in_specs=[row(d), whole(g), pl.BlockSpec(w_t.shape, lambda i: (0, 0), pipeline_mode=pl.Buffered(1)),
                  row(LANES), row(LANES)] + [whole(a) for a in (qg2, kg2, lng, lnb, ws, bs, kv)],
        out_specs=(row(d),) + tuple(row(wd) for wd in widths)
        + (row(BW), col(BW), hm, col(LANES), col(LANES), row(BW), row(BW)),
        out_shape=(SDS((t, d), BF16),) + tuple(SDS((t, wd), BF16) for wd in widths)
        + (SDS((t, BW), BF16), SDS((BW, t), BF16), SDS((A_KV_HEADS, t, HEAD_DIM), BF16),
           SDS((LANES, t), BF16), SDS((LANES, t), BF16), SDS((t, BW), BF16), SDS((t, BW), BF16)),
        compiler_params=_params("parallel"),
    )(x, g, w_t, cs, sn, qg2, kg2, lng, lnb, ws, bs, kv)


def _dh_rmsnorm_bwd(dps, w_t, x, g, dres, exchange=None):
    t, d = x.shape
    tm = _tile(t, 256, 16)
    n = len(dps)

    def body(*refs):
        dp_refs, w_ref = refs[:n], refs[n]
        x_ref, g_ref, dres_ref, dx_ref, dg_ref = refs[n + 1:]
        dh = None
        for p, dp_ref in enumerate(dp_refs):
            part = _dot(dp_ref[...], w_ref[_group_rows(p), :], NN)
            dh = part if dh is None else dh + part
        dx, dg = _rmsnorm_bwd_math(x_ref[...], g_ref[...], dh)
        dx_ref[...] = dx + dres_ref[...]

        @pl.when(pl.program_id(0) == 0)
        def _():
            dg_ref[...] = jnp.zeros_like(dg_ref)

        dg_ref[...] += dg

    row = pl.BlockSpec((tm, d), lambda i: (i, 0))
    vec = pl.BlockSpec((1, d), lambda i: (0, 0))
    return _host_call(
        body, list(dps) + [w_t, x, g, dres],
        [pl.BlockSpec((tm, a.shape[1]), lambda i: (i, 0)) for a in dps]
        + [pl.BlockSpec(w_t.shape, lambda i: (0, 0), pipeline_mode=pl.Buffered(1)), row, vec, row],
        (row, vec), (SDS((t, d), F32), SDS((1, d), F32)),
        name="dh_rmsnorm_bwd", grid=(t // tm,), semantics=("arbitrary",), exchange=exchange)


def _rope_tables(t):
    rows = t // GRID_W
    row = jnp.repeat(jnp.arange(rows, dtype=F32), GRID_W)
    col = jnp.tile(jnp.arange(GRID_W, dtype=F32), rows)
    n_freq = HEAD_DIM // 4
    inv = ROPE_THETA ** (-jnp.arange(n_freq, dtype=F32) / n_freq)
    ang = jnp.stack([row[:, None] * inv, col[:, None] * inv], axis=1)
    cos, sin = jnp.cos(ang), jnp.sin(ang)
    c64 = jnp.concatenate([cos[:, 0], cos[:, 0], cos[:, 1], cos[:, 1]], axis=-1)
    s64 = jnp.concatenate([-sin[:, 0], sin[:, 0], -sin[:, 1], sin[:, 1]], axis=-1)
    return jnp.tile(c64, (1, 2)), jnp.tile(s64, (1, 2))


def _head_sums(v, lane):
    lo = jnp.sum(jnp.where(lane < HEAD_DIM, v, 0.0), axis=-1, keepdims=True)
    hi = jnp.sum(jnp.where(lane < HEAD_DIM, 0.0, v), axis=-1, keepdims=True)
    return jnp.where(lane < HEAD_DIM, lo, hi)


def _swap16(v, lane):
    return jnp.where((lane % 32) < 16, pltpu.roll(v, LANES - 16, 1), pltpu.roll(v, 16, 1))


def _attn_fwd(qT, k, vT, pa, exchange=None):
    t = qT.shape[1]
    tq = _tile(t, 512, LANES)
    grp = A_HEADS // A_KV_HEADS

    ck = _tile(t, KEY_CHUNK, LANES)

    def body(qT_ref, k_ref, vT_ref, pa_ref, o_ref, lse_ref, y_ref):
        def finish_pair(pair, o_pair):
            o2 = jnp.concatenate(o_pair, axis=0).T
            cols = slice(pair * LANES, (pair + 1) * LANES)
            o_ref[:, cols] = o2
            z = pa_ref[:, 768 + pair * LANES:768 + (pair + 1) * LANES].astype(F32)
            y_ref[:, cols] = (o2 * (z * _sigmoid(z))).astype(BF16)

        n_c = t // ck
        items = [(h, c) for h in range(A_HEADS) for c in range(n_c)]

        def scores(h, c):
            return _dot(k_ref[h // grp, c * ck:(c + 1) * ck, :], qT_ref[h * HEAD_DIM:(h + 1) * HEAD_DIM, :], NN)

        def weighted_values(h, c, pT):
            kvh = h // grp
            return _dot(vT_ref[kvh * HEAD_DIM:(kvh + 1) * HEAD_DIM, c * ck:(c + 1) * ck], pT, NN)

        o_pair, lse_rows, sT, pending, acc = [], [], scores(*items[0]), None, None
        for i in range(len(items) + 1):
            item = items[i] if i < len(items) else None
            sT_next = scores(*items[i + 1]) if i + 1 < len(items) else None
            if pending is not None:
                (ph, pc, pT, alpha, l_run) = pending
                o_c = weighted_values(ph, pc, pT)
                acc = o_c if alpha is None else acc * alpha + o_c
                if pc == n_c - 1:
                    o_pair.append(acc / l_run)
                    if ph % 2:
                        finish_pair(ph // 2, o_pair)
                        o_pair = []
                pending = None
            if item is not None:
                h, c = item
                m_c = jnp.max(sT, axis=0, keepdims=True)
                if c == 0:
                    m_new, alpha = m_c, None
                else:
                    m_new = jnp.maximum(m, m_c)
                    alpha = jnp.exp(m - m_new)
                pT = jnp.exp(sT - m_new)
                l_c = jnp.sum(pT, axis=0, keepdims=True)
                l = l_c if c == 0 else l * alpha + l_c
                m = m_new
                if c == n_c - 1:
                    lse_rows.append(m + jnp.log(l))
                pending = (h, c, pT.astype(BF16), alpha, l)
            sT = sT_next
        lse_cols = jnp.concatenate(lse_rows, axis=0).T
        for kvh in range(A_KV_HEADS):
            lse_ref[kvh] = lse_cols[:, kvh * grp:(kvh + 1) * grp]

    row = lambda w: pl.BlockSpec((tq, w), lambda i: (i, 0))
    return _host_call(
        body, (qT, k, vT, pa),
        [pl.BlockSpec((BW, tq), lambda i: (0, i)), pl.BlockSpec((A_KV_HEADS, t, HEAD_DIM), lambda i: (0, 0, 0)),
         pl.BlockSpec((A_KV_HEADS * HEAD_DIM, t), lambda i: (0, 0)), row(W_A)],
        (row(BW), pl.BlockSpec((A_KV_HEADS, tq, grp), lambda i: (0, i, 0)), row(BW)),
        (SDS((t, BW), F32), SDS((A_KV_HEADS, t, grp), F32), SDS((t, BW), BF16)),
        name="attn_fwd", grid=(t // tq,), semantics=("parallel",), exchange=exchange)


def _attn_bwd(q, qT, k, kT, vT, pa, o, lse, dy, exchange=None):
    t = q.shape[0]
    tq = _tile(t, 256, LANES)
    grp = A_HEADS // A_KV_HEADS
    gw = grp * HEAD_DIM

    def body(q_ref, qT_ref, k_ref, kT_ref, vT_ref, z_ref, o_ref, lse_ref, dy_ref, dq_ref, dkT_ref, dvT_ref):
        @pl.when(pl.program_id(1) == 0)
        def _():
            dkT_ref[...] = jnp.zeros_like(dkT_ref)
            dvT_ref[...] = jnp.zeros_like(dvT_ref)

        z = z_ref[...].astype(F32)
        do = dy_ref[...] * (z * _sigmoid(z))
        doo = do * o_ref[...]
        doT = do.T
        kk, kT, vT = k_ref[...], kT_ref[...], vT_ref[...]
        heads = [slice(j * HEAD_DIM, (j + 1) * HEAD_DIM) for j in range(grp)]

        def scores(hs):
            return _dot(q_ref[:, hs], kT, NN), _dot(do[:, hs].astype(BF16), vT, NN)

        s, dp = scores(heads[0])
        for j, hs in enumerate(heads):
            nxt = scores(heads[j + 1]) if j + 1 < grp else None
            delta = jnp.sum(doo[:, hs], axis=-1, keepdims=True)
            p = jnp.exp(s - lse_ref[:, j:j + 1])
            ds_b = (p * (dp - delta)).astype(BF16)
            dq_ref[:, hs] = _dot(ds_b, kk, NN)
            dkT_ref[...] += _dot(qT_ref[hs, :], ds_b, NN)
            dvT_ref[...] += _dot(doT[hs, :].astype(BF16), p.astype(BF16), NN)
            if nxt is not None:
                s, dp = nxt

    grp_blk = pl.BlockSpec((tq, gw), lambda g, i: (i, g))
    kvT_blk = pl.BlockSpec((HEAD_DIM, t), lambda g, i: (g, 0))
    acc_blk = pl.BlockSpec((None, HEAD_DIM, t), lambda g, i: (g, 0, 0))
    return _host_call(
        body, (q, qT, k, kT, vT, pa, o, lse, dy),
        [grp_blk, pl.BlockSpec((gw, tq), lambda g, i: (g, i)), pl.BlockSpec((None, t, HEAD_DIM), lambda g, i: (g, 0, 0)),
         kvT_blk, kvT_blk, pl.BlockSpec((tq, gw), lambda g, i: (i, 768 // gw + g)), grp_blk,
         pl.BlockSpec((None, tq, grp), lambda g, i: (g, i, 0)), grp_blk],
        (grp_blk, acc_blk, acc_blk),
        (SDS((t, BW), F32), SDS((A_KV_HEADS, HEAD_DIM, t), F32), SDS((A_KV_HEADS, HEAD_DIM, t), F32)),
        name="attn_bwd", grid=(A_KV_HEADS, t // tq), semantics=("arbitrary", "arbitrary"), exchange=exchange)


def _attn_prep_bwd(pa, cs, sn, qg2, kg2, dq, dkT, dvT, dy, o, exchange=None):
    t = pa.shape[0]
    tq = _tile(t, 512, LANES)
    scale = HEAD_DIM ** -0.5

    def body(pa_ref, cs_ref, sn_ref, qg_ref, kg_ref, dq_ref, dkT_ref, dvT_ref, dy_ref, o_ref, dpa_ref, dqg_ref, dkg_ref):
        lane = lax.broadcasted_iota(jnp.int32, (tq, LANES), 1)
        c, s = cs_ref[...], sn_ref[...]

        @pl.when(pl.program_id(0) == 0)
        def _():
            dqg_ref[...] = jnp.zeros_like(dqg_ref)
            dkg_ref[...] = jnp.zeros_like(dkg_ref)

        def norm_rope_bwd(xg, g2, dout):
            r = lax.rsqrt(_head_sums(xg * xg, lane) * (1.0 / HEAD_DIM) + EPS)
            xh = xg * r
            dxn = dout * c + _swap16(dout * s, lane)
            gd = dxn * g2
            dx = r * (gd - xh * (_head_sums(gd * xh, lane) * (1.0 / HEAD_DIM)))
            return dx, jnp.sum(dxn * xh, axis=0, keepdims=True)

        for gi in range(4):
            sl = slice(gi * LANES, (gi + 1) * LANES)
            dx, dg = norm_rope_bwd(pa_ref[:, sl].astype(F32), qg_ref[...], dq_ref[:, sl] * scale)
            dpa_ref[:, sl] = dx.astype(BF16)
            dqg_ref[...] += dg
        dx, dg = norm_rope_bwd(pa_ref[:, 512:640].astype(F32), kg_ref[...], dkT_ref[...].T)
        dpa_ref[:, 512:640] = dx.astype(BF16)
        dkg_ref[...] += dg
        dpa_ref[:, 640:768] = dvT_ref[...].T.astype(BF16)
        z = pa_ref[:, 768:1280].astype(F32)
        _, dsilu = _silu_and_grad(z)
        dpa_ref[:, 768:1280] = (dy_ref[...] * o_ref[...] * dsilu).astype(BF16)

    row = lambda w: pl.BlockSpec((tq, w), lambda i: (i, 0))
    col = pl.BlockSpec((LANES, tq), lambda i: (0, i))
    vec = pl.BlockSpec((1, LANES), lambda i: (0, 0))
    return _host_call(
        body, (pa, cs, sn, qg2, kg2, dq, dkT, dvT, dy, o),
        [row(W_A), row(LANES), row(LANES), vec, vec, row(BW), col, col, row(BW), row(BW)],
        (row(W_A), vec, vec), (SDS((t, W_A), BF16), SDS((1, LANES), F32), SDS((1, LANES), F32)),
        name="attn_prep_bwd", grid=(t // tq,), semantics=("arbitrary",), exchange=exchange)


def _layer_norm(v, g, b):
    mu = jnp.mean(v, axis=-1, keepdims=True)
    xc = v - mu
    rs = lax.rsqrt(jnp.mean(xc * xc, axis=-1, keepdims=True) + EPS)
    xh = xc * rs
    return xh * g + b, xh, rs


def _gmlp_tile(pb, g_ref, b_ref, ws_ref, bs_ref, y_ref):
    vln, _, _ = _layer_norm(pb[:, BW:2 * BW].astype(F32), g_ref[...], b_ref[...])
    vb = vln.astype(BF16)
    for gi in range(B_GROUPS):
        w = ws_ref[gi].astype(BF16)
        cs_ = slice(gi * CHUNK, (gi + 1) * CHUNK)
        for n in range(pb.shape[0] // CHUNK):
            rs_ = slice(n * CHUNK, (n + 1) * CHUNK)
            mixed = _dot(w, vb[rs_, cs_], NN) + bs_ref[gi]
            z = pb[rs_, 2 * BW + gi * CHUNK:2 * BW + (gi + 1) * CHUNK].astype(F32)
            y_ref[rs_, cs_] = (pb[rs_, cs_].astype(F32) * mixed * (z * _sigmoid(z))).astype(BF16)


def _gmlp_bwd(pb, lng, lnb, ws, bs, dy):
    t = pb.shape[0]
    tb = _tile(t, 256, CHUNK)

    def body(pb_ref, g_ref, b_ref, ws_ref, bs_ref, dy_ref, dpb_ref, dws_ref, dbs_ref, dg_ref, db_ref, dvln_ref):
        @pl.when(pl.program_id(0) == 0)
        def _():
            dws_ref[...] = jnp.zeros_like(dws_ref)
            dbs_ref[...] = jnp.zeros_like(dbs_ref)
            dg_ref[...] = jnp.zeros_like(dg_ref)
            db_ref[...] = jnp.zeros_like(db_ref)

        vln, xh, rs = _layer_norm(pb_ref[:, BW:2 * BW].astype(F32), g_ref[...], b_ref[...])
        vb = vln.astype(BF16)
        for gi in range(B_GROUPS):
            w = ws_ref[gi].astype(BF16)
            cs_ = slice(gi * CHUNK, (gi + 1) * CHUNK)
            for n in range(tb // CHUNK):
                rs_ = slice(n * CHUNK, (n + 1) * CHUNK)
                vbc = vb[rs_, cs_]
                mixed = _dot(w, vbc, NN) + bs_ref[gi]
                zs = slice(2 * BW + gi * CHUNK, 2 * BW + (gi + 1) * CHUNK)
                z = pb_ref[rs_, zs].astype(F32)
                u = pb_ref[rs_, cs_].astype(F32)
                sil, dsil = _silu_and_grad(z)
                dyc = dy_ref[rs_, cs_]
                dmixed = dyc * u * sil
                dpb_ref[rs_, cs_] = (dyc * mixed * sil).astype(BF16)
                dpb_ref[rs_, zs] = (dyc * u * mixed * dsil).astype(BF16)
                dmb = dmixed.astype(BF16)
                dws_ref[gi] += _dot(dmb, vbc, NT)
                dbs_ref[gi] += jnp.sum(dmixed, axis=-1, keepdims=True)
                dvln_ref[rs_, cs_] = _dot(w, dmb, TN)
        dvln = dvln_ref[...]
        dg_ref[...] += jnp.sum(dvln * xh, axis=0, keepdims=True)
        db_ref[...] += jnp.sum(dvln, axis=0, keepdims=True)
        gd = dvln * g_ref[...]
        dv = rs * (gd - jnp.mean(gd, axis=-1, keepdims=True) - xh * jnp.mean(gd * xh, axis=-1, keepdims=True))
        dpb_ref[:, BW:2 * BW] = dv.astype(BF16)

    vec = pl.BlockSpec((1, BW), lambda i: (0, 0))
    wsb = pl.BlockSpec((B_GROUPS, CHUNK, CHUNK), lambda i: (0, 0, 0))
    bsb = pl.BlockSpec((B_GROUPS, CHUNK, 1), lambda i: (0, 0, 0))
    return pl.pallas_call(
        body, name="gmlp_bwd", grid=(t // tb,),
        in_specs=[pl.BlockSpec((tb, W_B), lambda i: (i, 0)), vec, vec, wsb, bsb, pl.BlockSpec((tb, BW), lambda i: (i, 0))],
        out_specs=(pl.BlockSpec((tb, W_B), lambda i: (i, 0)), wsb, bsb, vec, vec),
        out_shape=(SDS((t, W_B), BF16), SDS((B_GROUPS, CHUNK, CHUNK), F32), SDS((B_GROUPS, CHUNK, 1), F32),
                   SDS((1, BW), F32), SDS((1, BW), F32)),
        scratch_shapes=[pltpu.VMEM((tb, BW), F32)],
        compiler_params=_params("arbitrary"),
    )(pb, lng, lnb, ws, bs, dy)


def _mem_scores(pm_ref, kv_ref, h):
    hs = slice(h * M_HEAD_DIM, (h + 1) * M_HEAD_DIM)
    return _dot(pm_ref[:, hs].astype(BF16), kv_ref[:, hs].astype(BF16), NT) * (M_HEAD_DIM ** -0.5)


def _softmax_rows(s):
    e = jnp.exp(s - jnp.max(s, axis=-1, keepdims=True))
    return e / jnp.sum(e, axis=-1, keepdims=True)


def _mem_attn_tile(pm, kv_ref, y_ref):
    s_next = _mem_scores(pm, kv_ref, 0)
    for h in range(M_HEADS):
        hs = slice(h * M_HEAD_DIM, (h + 1) * M_HEAD_DIM)
        vh = kv_ref[:, BW + h * M_HEAD_DIM:BW + (h + 1) * M_HEAD_DIM].astype(BF16)
        s, s_next = s_next, (_mem_scores(pm, kv_ref, h + 1) if h + 1 < M_HEADS else None)
        p = _softmax_rows(s)
        o = _dot(p.astype(BF16), vh, NN)
        z = pm[:, BW + h * M_HEAD_DIM:BW + (h + 1) * M_HEAD_DIM].astype(F32)
        y_ref[:, hs] = (o * (z * _sigmoid(z))).astype(BF16)


def _mem_attn_bwd(pm, kv, dy):
    t = pm.shape[0]
    tq = _tile(t, 512, 8)
    ml = kv.shape[0]
    scale = M_HEAD_DIM ** -0.5

    def body(pm_ref, kv_ref, dy_ref, dpm_ref, dkv_ref):
        @pl.when(pl.program_id(0) == 0)
        def _():
            dkv_ref[...] = jnp.zeros_like(dkv_ref)

        s_next = _mem_scores(pm_ref, kv_ref, 0)
        for h in range(M_HEADS):
            hs = slice(h * M_HEAD_DIM, (h + 1) * M_HEAD_DIM)
            zs = slice(BW + h * M_HEAD_DIM, BW + (h + 1) * M_HEAD_DIM)
            kh = kv_ref[:, hs].astype(BF16)
            vh = kv_ref[:, zs].astype(BF16)
            qh = pm_ref[:, hs].astype(BF16)
            s, s_next = s_next, (_mem_scores(pm_ref, kv_ref, h + 1) if h + 1 < M_HEADS else None)
            p = _softmax_rows(s)
            pb = p.astype(BF16)
            o = _dot(pb, vh, NN)
            sil, dsil = _silu_and_grad(pm_ref[:, zs].astype(F32))
            dyh = dy_ref[:, hs]
            do = dyh * sil
            dpm_ref[:, zs] = (dyh * o * dsil).astype(BF16)
            delta = jnp.sum(do * o, axis=-1, keepdims=True)
            do_b = do.astype(BF16)
            dp = _dot(do_b, vh, NT)
            dr_b = (p * (dp - delta) * scale).astype(BF16)
            dpm_ref[:, hs] = _dot(dr_b, kh, NN).astype(BF16)
            dkv_ref[:, hs] += _dot(dr_b, qh, TN)
            dkv_ref[:, zs] += _dot(pb, do_b, TN)

    kvb = pl.BlockSpec((ml, 2 * BW), lambda i: (0, 0))
    return pl.pallas_call(
        body, name="mem_attn_bwd", grid=(t // tq,),
        in_specs=[pl.BlockSpec((tq, W_M), lambda i: (i, 0)), kvb, pl.BlockSpec((tq, BW), lambda i: (i, 0))],
        out_specs=(pl.BlockSpec((tq, W_M), lambda i: (i, 0)), kvb),
        out_shape=(SDS((t, W_M), BF16), SDS((ml, 2 * BW), F32)),
        compiler_params=_params("arbitrary"),
    )(pm, kv, dy)


def _merge_fwd(ya, yb, ym, pg, g_br, x, w_out, exchange=None):
    t = ya.shape[0]
    tm = _tile(t, 512, 8)
    shard = g_br.shape[-1]

    def body(ya_ref, yb_ref, ym_ref, pg_ref, g_ref, x_ref, wo_ref, m_ref, xn_ref, w_ref):
        @pl.when(pl.program_id(0) == 0)
        def _():
            for d in range(N_DEV):
                w_ref[:, :, d * shard:(d + 1) * shard] = g_ref[d]

        ups = [_dot(y_ref[...], w_ref[n], NN) for n, y_ref in enumerate((ya_ref, yb_ref, ym_ref))]
        acc = None
        for n, up in enumerate(ups):
            term = _sigmoid(pg_ref[:, n * D_MODEL:(n + 1) * D_MODEL].astype(F32)) * up
            acc = term if acc is None else acc + term
        merged = acc.astype(BF16)
        m_ref[...] = merged
        xn_ref[...] = x_ref[...] + _dot(merged, wo_ref[...], NN)

    yb_spec = pl.BlockSpec((tm, BW), lambda i: (i, 0))
    row = pl.BlockSpec((tm, D_MODEL), lambda i: (i, 0))
    return _host_call(
        body, (ya, yb, ym, pg, g_br, x, w_out),
        [yb_spec, yb_spec, yb_spec, pl.BlockSpec((tm, W_G), lambda i: (i, 0)),
         pl.BlockSpec(g_br.shape, lambda i: (0, 0, 0, 0)), row, pl.BlockSpec(w_out.shape, lambda i: (0, 0))],
        (row, row, pl.BlockSpec((3, BW, D_MODEL), lambda i: (0, 0, 0))),
        (SDS((t, D_MODEL), BF16), SDS((t, D_MODEL), F32), SDS((3, BW, D_MODEL), BF16)),
        name="merge_fwd", grid=(t // tm,), semantics=("arbitrary",), exchange=exchange)


def _merge_bwd(ya, yb, ym, pg, wbr, dx_out, w_out, exchange=None):
    t = ya.shape[0]
    tm = _tile(t, 512, 16)
    n_steps = t // tm
    shard = D_MODEL // N_DEV

    def body(ya_ref, yb_ref, ym_ref, pg_ref, w_ref, dx_ref, wo_ref, dya_ref, dyb_ref, dym_ref, dpg_ref, dwd_ref, dw_ref):
        @pl.when(pl.program_id(0) == 0)
        def _():
            dw_ref[...] = jnp.zeros_like(dw_ref)

        dmf = _dot(dx_ref[...].astype(BF16), wo_ref[...], NT)
        branches = ((ya_ref, dya_ref), (yb_ref, dyb_ref), (ym_ref, dym_ref))

        def gate(n):
            gt = _sigmoid(pg_ref[:, n * D_MODEL:(n + 1) * D_MODEL].astype(F32))
            return gt, (dmf * gt).astype(BF16)

        gt, dup = gate(0)
        for n, (y_ref, dy_ref) in enumerate(branches):
            nxt = gate(n + 1) if n + 1 < len(branches) else None
            y, w = y_ref[...], w_ref[n]
            dy_ref[...] = _dot(dup, w, NT)
            dw_ref[n] += _dot(y, dup, TN)
            up = _dot(y, w, NN)
            dpg_ref[:, n * D_MODEL:(n + 1) * D_MODEL] = (dmf * up * gt * (1.0 - gt)).astype(BF16)
            if nxt is not None:
                gt, dup = nxt

        @pl.when(pl.program_id(0) == n_steps - 1)
        def _():
            for d in range(N_DEV):
                dwd_ref[d] = dw_ref[:, :, d * shard:(d + 1) * shard].astype(BF16)

    y_spec = pl.BlockSpec((tm, BW), lambda i: (i, 0))
    w_spec = pl.BlockSpec((3, BW, D_MODEL), lambda i: (0, 0, 0))
    return _host_call(
        body, (ya, yb, ym, pg, wbr, dx_out, w_out),
        [y_spec, y_spec, y_spec, pl.BlockSpec((tm, W_G), lambda i: (i, 0)), w_spec,
         pl.BlockSpec((tm, D_MODEL), lambda i: (i, 0)), pl.BlockSpec(w_out.shape, lambda i: (0, 0))],
        (y_spec, y_spec, y_spec, pl.BlockSpec((tm, W_G), lambda i: (i, 0)),
         pl.BlockSpec((N_DEV, 3, BW, shard), lambda i: (0, 0, 0, 0))),
        (SDS((t, BW), F32), SDS((t, BW), F32), SDS((t, BW), F32), SDS((t, W_G), BF16), SDS((N_DEV, 3, BW, shard), BF16)),
        name="merge_bwd", grid=(n_steps,), semantics=("arbitrary",), scratch=[pltpu.VMEM((3, BW, D_MODEL), F32)],
        exchange=exchange)


def _loss_head(x, g, target):
    t, d = x.shape
    tm = _tile(t, 512, 8)

    def body(x_ref, g_ref, t_ref, loss_ref, dx_ref, dg_ref):
        @pl.when(pl.program_id(0) == 0)
        def _():
            loss_ref[...] = jnp.zeros_like(loss_ref)
            dg_ref[...] = jnp.zeros_like(dg_ref)

        xf = x_ref[...]
        r = lax.rsqrt(jnp.mean(xf * xf, axis=-1, keepdims=True) + EPS)
        xh = xf * r
        err = xh * g_ref[...] - t_ref[...]
        per_tok = jnp.mean(err * err, axis=-1, keepdims=True)
        loss_ref[...] += 0.5 * jnp.sum(per_tok, axis=0, keepdims=True)
        dy = err * (1.0 / d)
        gd = dy * g_ref[...]
        dx_ref[...] = r * (gd - xh * jnp.mean(gd * xh, axis=-1, keepdims=True))
        dg_ref[...] += jnp.sum(dy * xh, axis=0, keepdims=True)

    row = pl.BlockSpec((tm, d), lambda i: (i, 0))
    vec = pl.BlockSpec((1, d), lambda i: (0, 0))
    return pl.pallas_call(
        body, name="loss_head", grid=(t // tm,),
        in_specs=[row, vec, row],
        out_specs=(pl.BlockSpec((1, 1), lambda i: (0, 0)), row, vec),
        out_shape=(SDS((1, 1), F32), SDS((t, d), F32), SDS((1, d), F32)),
        compiler_params=_params("arbitrary"),
    )(x, g, target)


def _layer_fwd(x, mem, w, tabs, next_shards=None):
    cs, sn = tabs
    riding = next_shards is not None
    memn, kv = _mem_kv_fwd(mem, w["mem_g"], w["w_kv"])
    h, pa, pb, pm, pg, q, qT, k, kT, vT, yb, ym = _norm_in_proj(
        x, w["norm_g"], w["w_in_t"], cs, sn, w["qg2"], w["kg2"], w["ln_g"], w["ln_b"], w["w_s"], w["b_s"], kv)
    (o, lse, ya), gathered = _attn_fwd(qT, k, vT, pa, exchange=_gather_first_hop(next_shards) if riding else None)
    (merged, x_next, w_br), gathered = _merge_fwd(ya, yb, ym, pg, w["g_br"], x, w["w_out"],
                                                  exchange=_gather_forward(gathered) if riding else None)
    saved = dict(x=x, h=h, pa=pa, pb=pb, pm=pm, pg=pg, q=q, qT=qT, k=k, kT=kT, vT=vT, o=o, lse=lse, ya=ya, yb=yb, ym=ym,
                 memn=memn, kv=kv, merged=merged, w_br=w_br)
    return x_next, saved, gathered


def _layer_bwd(dx_out, mem, w, s, tabs, pending=None, core=None, scatter_own=False):
    cs, sn = tabs
    t = dx_out.shape[0]
    riding = pending is not None
    by_owner = lambda a: a.reshape((N_DEV // 2, 2, -1) + a.shape[-2:])
    row_shards = lambda a: by_owner(a.reshape(N_DEV, a.shape[0] // N_DEV, a.shape[1]))
    d_w_out = _mm_tn(s["merged"], dx_out, "d_w_out")
    (dya, dyb, dym, dpg, d_w_br), recv = _merge_bwd(s["ya"], s["yb"], s["ym"], s["pg"], s["w_br"], dx_out, w["w_out"],
                                                    exchange=_scatter_to_sibling(pending) if riding else None)
    parts = _pair_sums(BIG, pending, recv, core) if riding else None
    grads = dict(w_br=by_owner(d_w_br), w_out=row_shards(d_w_out))
    early = [grads[n] for n in BIG[2:]]
    (dq, dkT, dvT), landed = _attn_bwd(
        s["q"], s["qT"], s["k"], s["kT"], s["vT"], s["pa"], s["o"], s["lse"], dya,
        exchange=_both(_scatter_to_chips(parts) if riding else None, _scatter_to_sibling(early) if scatter_own else None))
    from_chips, early_recv = (landed[:len(BIG)], landed[len(BIG):]) if riding else (None, landed)
    early_parts = _pair_sums(BIG[2:], early, early_recv, core) if scatter_own else None
    (dpa, d_qg2, d_kg2), early_from_chips = _attn_prep_bwd(
        s["pa"], cs, sn, w["qg2"], w["kg2"], dq, dkT.reshape(-1, t), dvT.reshape(-1, t), dya, s["o"],
        exchange=_scatter_to_chips(early_parts) if scatter_own else None)
    dpb, d_w_s, d_b_s, d_ln_g, d_ln_b = _gmlp_bwd(s["pb"], w["ln_g"], w["ln_b"], w["w_s"], w["b_s"], dyb)
    dpm, dkv = _mem_attn_bwd(s["pm"], s["kv"], dym)
    d_w_kv, d_mem_g = _mem_kv_bwd(mem, w["mem_g"], s["memn"], w["w_kv"], dkv)
    dps = (dpa, dpb, dpm, dpg)
    grads.update(w_in=row_shards(_d_w_in_t(dps, s["h"])), w_mem_kv=row_shards(d_w_kv))
    own = None
    if scatter_own:
        late = [grads[n] for n in BIG[:2]]
        late_parts = _pair_sums(BIG[:2], late, _exchange_call(_scatter_to_sibling(late), "rs_sibling_swap"), core)
        own = _scatter_to_chips(late_parts)
    (dx_in, d_norm_g), late_from_chips = _dh_rmsnorm_bwd(dps, w["w_in_t"], s["x"], w["norm_g"], dx_out, exchange=own)
    if scatter_own:
        own_parts, own_from_chips = late_parts + early_parts, tuple(late_from_chips) + tuple(early_from_chips)
    grads.update(norm_g=d_norm_g[0], q_norm_g=d_qg2[0, :HEAD_DIM] + d_qg2[0, HEAD_DIM:],
                 k_norm_g=d_kg2[0, :HEAD_DIM] + d_kg2[0, HEAD_DIM:], sg_ln_g=d_ln_g[0], sg_ln_b=d_ln_b[0],
                 w_s=d_w_s, b_s=d_b_s[:, :, 0], mem_norm_g=d_mem_g[0])
    return (dx_in, grads, ((parts, from_chips) if riding else None),
            ((own_parts, own_from_chips) if scatter_own else None))


def _layer_weights(l, w_in_t, w_kv, g_br, w_out, small):
    tile2 = lambda g: jnp.tile(g.reshape(1, -1), (1, 2))
    return dict(
        w_in_t=w_in_t, w_kv=w_kv, g_br=g_br, w_out=w_out,
        norm_g=small["norm_g"][l].reshape(1, -1), qg2=tile2(small["q_norm_g"][l]), kg2=tile2(small["k_norm_g"][l]),
        ln_g=small["sg_ln_g"][l].reshape(1, -1), ln_b=small["sg_ln_b"][l].reshape(1, -1),
        w_s=small["w_s"][l], b_s=small["b_s"][l][:, :, None], mem_g=small["mem_norm_g"][l].reshape(1, -1))


def _position():
    x, y, c = lax.axis_index("x"), lax.axis_index("y"), lax.axis_index("c")
    return x, y, c, [(1 - x, y), (x, 1 - y), (1 - x, 1 - y)]


def _gather_first_hop(shards):
    n = len(shards)

    def make(x_refs, out_refs, sems):
        send_sems, recv_sems, local_sems = sems
        x, y, c, chips = _position()
        me = 4 * x + 2 * y + c
        peers = [(x, y, 1 - c)] + [(cx, cy, c) for cx, cy in chips]
        copies = [pltpu.make_async_copy(x_refs[t], out_refs[t].at[me], local_sems.at[t]) for t in range(n)]
        copies += [pltpu.make_async_remote_copy(
            src_ref=x_refs[t], dst_ref=out_refs[t].at[me], send_sem=send_sems.at[t, k], recv_sem=recv_sems.at[t, k],
            device_id=peer, device_id_type=MESH_ID) for t in range(n) for k, peer in enumerate(peers)]
        return copies

    return _Exchange(shards, [SDS((N_DEV,) + a.shape, a.dtype) for a in shards],
                     [pltpu.SemaphoreType.DMA((n, 4)), pltpu.SemaphoreType.DMA((n, 4)), pltpu.SemaphoreType.DMA((n,))], make)


def _gather_forward(gathered):
    n = len(gathered)

    def make(in_refs, out_refs, sems):
        send_sems, recv_sems = sems
        x, y, c, chips = _position()
        return [pltpu.make_async_remote_copy(
            src_ref=in_refs[t].at[4 * cx + 2 * cy + c], dst_ref=out_refs[t].at[4 * cx + 2 * cy + c],
            send_sem=send_sems.at[t, j], recv_sem=recv_sems.at[t, j], device_id=(x, y, 1 - c), device_id_type=MESH_ID)
            for t in range(n) for j, (cx, cy) in enumerate(chips)]

    return _Exchange(gathered, [SDS(a.shape, a.dtype) for a in gathered],
                     [pltpu.SemaphoreType.DMA((n, 3)), pltpu.SemaphoreType.DMA((n, 3))], make,
                     aliases={t: t for t in range(n)})


def _all_gather(shards, name):
    return _exchange_call(_gather_forward(_exchange_call(_gather_first_hop(shards), name + "_hop1")), name + "_hop2")


def _scatter_to_sibling(dests):
    n = len(dests)

    def make(d_refs, recv_refs, sems):
        send_sems, recv_sems = sems
        x, y, c, _ = _position()
        return [pltpu.make_async_remote_copy(
            src_ref=d_refs[t].at[:, 1 - c], dst_ref=recv_refs[t], send_sem=send_sems.at[t],
            recv_sem=recv_sems.at[t], device_id=(x, y, 1 - c), device_id_type=MESH_ID) for t in range(n)]

    return _Exchange(dests, [SDS(a.shape[:1] + a.shape[2:], a.dtype) for a in dests],
                     [pltpu.SemaphoreType.DMA((n,)), pltpu.SemaphoreType.DMA((n,))], make)


def _scatter_to_chips(parts):
    n = len(parts)

    def make(p_refs, recv_refs, sems):
        send_sems, recv_sems = sems
        _, _, c, chips = _position()
        return [pltpu.make_async_remote_copy(
            src_ref=p_refs[t].at[2 * cx + cy], dst_ref=recv_refs[t].at[k], send_sem=send_sems.at[t, k],
            recv_sem=recv_sems.at[t, k], device_id=(cx, cy, c), device_id_type=MESH_ID)
            for t in range(n) for k, (cx, cy) in enumerate(chips)]

    return _Exchange(parts, [SDS((3,) + a.shape[1:], a.dtype) for a in parts],
                     [pltpu.SemaphoreType.DMA((n, 3)), pltpu.SemaphoreType.DMA((n, 3))], make)


def _pair_sums(names, dests, recv, core):
    n = len(dests)

    def body(core_ref, *refs):
        for a_ref, b_ref, o_ref in zip(refs[:n], refs[n:2 * n], refs[2 * n:]):
            o_ref[...] = (a_ref[...].astype(F32) + b_ref[...].astype(F32)).astype(o_ref.dtype)

    mine = [pl.BlockSpec((None, None) + d.shape[2:], lambda j, core_ref: (j, core_ref[0], 0, 0, 0)) for d in dests]
    theirs = [pl.BlockSpec((None,) + r.shape[1:], lambda j, core_ref: (j, 0, 0, 0)) for r in recv]
    return list(pl.pallas_call(
        body, name="rs_pair_sum_" + "_".join(names), out_shape=tuple(SDS(r.shape, BF16) for r in recv),
        grid_spec=pltpu.PrefetchScalarGridSpec(
            num_scalar_prefetch=1, grid=(4,), in_specs=mine + theirs, out_specs=tuple(theirs)),
        compiler_params=_params("parallel"),
    )(core, *dests, *recv))


def _adamw_math(w, g, m, v):
    m = ADAM_B1 * m + (1.0 - ADAM_B1) * g
    v = ADAM_B2 * v + (1.0 - ADAM_B2) * (g * g)
    m_hat = m / (1.0 - ADAM_B1 ** ADAM_STEP)
    v_hat = v / (1.0 - ADAM_B2 ** ADAM_STEP)
    delta = -ADAM_LR * (m_hat / (jnp.sqrt(v_hat) + ADAM_EPS) + ADAM_WD * w)
    return delta, m, v


def _sum_and_adamw(parts, w, m, v, name):
    n, r, ln = parts.shape
    tr = _tile(r, 512, 16)

    def body(p_ref, w_ref, m_ref, v_ref, g_out, d_out, m_out, v_out):
        g = p_ref[0].astype(F32)
        for j in range(1, n):
            g = g + p_ref[j].astype(F32)
        delta, nm, nv = _adamw_math(w_ref[...], g, m_ref[...], v_ref[...])
        g_out[...] = g
        d_out[...] = delta
        m_out[...] = nm
        v_out[...] = nv

    blk = pl.BlockSpec((tr, ln), lambda i: (i, 0))
    return pl.pallas_call(
        body, name=name, grid=(r // tr,),
        in_specs=[pl.BlockSpec((n, tr, ln), lambda i: (0, i, 0)), blk, blk, blk],
        out_specs=(blk, blk, blk, blk), out_shape=tuple(SDS((r, ln), F32) for _ in range(4)),
        compiler_params=_params("parallel"),
    )(parts, w, m, v)


BIG = ("w_in", "w_mem_kv", "w_br", "w_out")
SMALL = ("norm_g", "q_norm_g", "k_norm_g", "sg_ln_g", "sg_ln_b", "w_s", "b_s", "mem_norm_g", "final_g")


def _pack(arrs, row_unit=16):
    flat = jnp.concatenate([a.reshape(-1) for a in arrs])
    pad = (-flat.shape[0]) % (row_unit * LANES)
    if pad:
        flat = jnp.concatenate([flat, jnp.zeros((pad,), flat.dtype)])
    return flat.reshape(-1, LANES)


def _unpack(buf, shapes):
    flat = buf.reshape(-1)
    out, off = [], 0
    for shp in shapes:
        n = 1
        for s_ in shp:
            n *= s_
        out.append(flat[off:off + n].reshape(shp))
        off += n
    return out


def _shard_sum_adamw(part, from_chips, chip, w, m, v, layer, prev, name):
    _, na, r, cdim = part.shape
    flat = w.ndim == 3
    assert not flat or na == 1
    tr = _tile(r, max(8, (256 * 1024) // (na * cdim)), 8)
    n_prev = 0 if prev is None else len(prev)

    def body(chip_ref, p_ref, f_ref, w_ref, m_ref, v_ref, *rest):
        g_out, d_out, m_out, v_out = rest[n_prev:]
        g = p_ref[...].astype(F32)
        for j in range(3):
            g = g + f_ref[j].astype(F32)
        delta, nm, nv = _adamw_math(w_ref[...], g, m_ref[...], v_ref[...])
        g_out[...] = g
        d_out[...] = delta
        m_out[...] = nm
        v_out[...] = nv

    a_blk = None if flat else na
    if flat:
        lay = pl.BlockSpec((None, tr, cdim), lambda i, chip_ref: (layer, i, 0))
    else:
        lay = pl.BlockSpec((None, na, tr, cdim), lambda i, chip_ref: (layer, 0, i, 0))
    return pl.pallas_call(
        body, name=name, out_shape=tuple(SDS(w.shape, F32) for _ in range(4)),
        grid_spec=pltpu.PrefetchScalarGridSpec(
            num_scalar_prefetch=1, grid=(r // tr,),
            in_specs=[pl.BlockSpec((None, a_blk, tr, cdim), lambda i, chip_ref: (chip_ref[0], 0, i, 0)),
                      pl.BlockSpec((3, a_blk, tr, cdim), lambda i, chip_ref: (0, 0, i, 0)), lay, lay, lay]
            + [HBM] * n_prev,
            out_specs=(lay, lay, lay, lay)),
        input_output_aliases={6 + j: j for j in range(n_prev)},
        compiler_params=_params("parallel"),
    )(chip, part, from_chips, w, m, v, *(prev or ()))


def kernel(x, mem, norm_g, w_in, q_norm_g, k_norm_g, sg_ln_g, sg_ln_b, w_s, b_s, mem_norm_g, w_mem_kv, w_br, w_out, final_g, loss_target, m_norm_g, m_w_in, m_q_norm_g, m_k_norm_g, m_sg_ln_g, m_sg_ln_b, m_w_s, m_b_s, m_mem_norm_g, m_w_mem_kv, m_w_br, m_w_out, m_final_g, v_norm_g, v_w_in, v_q_norm_g, v_k_norm_g, v_sg_ln_g, v_sg_ln_b, v_w_s, v_b_s, v_mem_norm_g, v_w_mem_kv, v_w_br, v_w_out, v_final_g):
    wts = dict(norm_g=norm_g, w_in=w_in, q_norm_g=q_norm_g, k_norm_g=k_norm_g, sg_ln_g=sg_ln_g, sg_ln_b=sg_ln_b,
               w_s=w_s, b_s=b_s, mem_norm_g=mem_norm_g, w_mem_kv=w_mem_kv, w_br=w_br, w_out=w_out, final_g=final_g)
    mom1 = dict(norm_g=m_norm_g, w_in=m_w_in, q_norm_g=m_q_norm_g, k_norm_g=m_k_norm_g, sg_ln_g=m_sg_ln_g,
                sg_ln_b=m_sg_ln_b, w_s=m_w_s, b_s=m_b_s, mem_norm_g=m_mem_norm_g, w_mem_kv=m_w_mem_kv, w_br=m_w_br,
                w_out=m_w_out, final_g=m_final_g)
    mom2 = dict(norm_g=v_norm_g, w_in=v_w_in, q_norm_g=v_q_norm_g, k_norm_g=v_k_norm_g, sg_ln_g=v_sg_ln_g,
                sg_ln_b=v_sg_ln_b, w_s=v_w_s, b_s=v_b_s, mem_norm_g=v_mem_norm_g, w_mem_kv=v_w_mem_kv, w_br=v_w_br,
                w_out=v_w_out, final_g=v_final_g)
    dp = w_in.shape[0]
    core = lax.axis_index("c").astype(jnp.int32).reshape(1)
    chip = (2 * lax.axis_index("x") + lax.axis_index("y")).astype(jnp.int32).reshape(1)

    t_in = lambda a: jnp.swapaxes(a, 1, 2)
    wts, mom1, mom2 = [dict(d_, w_in=t_in(d_["w_in"])) for d_ in (wts, mom1, mom2)]

    shard_bf = {n: wts[n].astype(BF16) for n in BIG}
    shards = lambda l: [shard_bf[n][l] for n in BIG]
    x_l, mem_l = x[0], mem[0]
    tabs = _rope_tables(x_l.shape[0])

    gathered = _all_gather(shards(0), "weights_all_gather")
    layers, saved = [], []
    for l in range(dp):
        g_in, g_kv, g_br, g_out = gathered
        layers.append(_layer_weights(l, g_in.reshape(IN_WIDTH, -1), g_kv.reshape(D_MODEL, -1), g_br,
                                     g_out.reshape(D_MODEL, -1), wts))
        x_l, s, gathered = _layer_fwd(x_l, mem_l, layers[l], tabs, next_shards=shards(l + 1) if l + 1 < dp else None)
        saved.append(s)
    loss_local, dx, d_final_g = _loss_head(x_l, final_g.reshape(1, -1), loss_target[0])
    loss = lax.psum(loss_local[0, 0], AXES)

    def finish(l, parts, from_chips, prev):
        return {n: _shard_sum_adamw(p, f, chip, wts[n], mom1[n], mom2[n], l, None if prev is None else prev[n],
                                    "sum_adamw_" + n)
                for n, p, f in zip(BIG, parts, from_chips)}

    grads, updated, pending = [None] * dp, None, None
    for l in reversed(range(dp)):
        dx, grads[l], scattered, own = _layer_bwd(dx, mem_l, layers[l], saved[l], tabs, pending=pending, core=core,
                                                  scatter_own=(l == 0))
        if scattered is not None:
            updated = finish(l + 1, *scattered, updated)
        pending = [grads[l][n] for n in BIG]
    updated = finish(0, *own, updated)
    grad_x = dx
    big_out = [{n: (t_in(updated[n][k]) if n == "w_in" else updated[n][k]) for n in BIG} for k in range(4)]

    small_g = {n: jnp.stack([g[n] for g in grads]) for n in SMALL if n != "final_g"}
    small_g["final_g"] = d_final_g
    vectors = [n for n in SMALL if n != "w_s"]
    rows128 = lambda a: a.reshape(-1, LANES)
    all_vec, all_ws = _all_gather([_pack([small_g[n] for n in vectors]), rows128(small_g["w_s"]).astype(BF16)],
                                  "small_all_gather")
    vec_bufs = _sum_and_adamw(all_vec, *[_pack([d_[n] for n in vectors]) for d_ in (wts, mom1, mom2)], "small_sum_adamw")
    ws_bufs = _sum_and_adamw(all_ws, *[rows128(d_["w_s"]) for d_ in (wts, mom1, mom2)], "w_s_sum_adamw")

    outs = []
    for big_vals, vec_buf, ws_buf in zip(big_out, vec_bufs, ws_bufs):
        vals = dict(big_vals)
        vals.update(zip(vectors, _unpack(vec_buf, [wts[n].shape for n in vectors])))
        vals["w_s"] = ws_buf.reshape(wts["w_s"].shape)
        outs.append(vals)
    order = ("norm_g", "w_in", "q_norm_g", "k_norm_g", "sg_ln_g", "sg_ln_b", "w_s", "b_s", "mem_norm_g", "w_mem_kv",
             "w_br", "w_out", "final_g")
    result = [loss, grad_x[None]]
    for vals in outs:
        result += [vals[n] for n in order]
    return tuple(result)
```
